```python
import jax, jax.numpy as jnp
from jax import lax
import numpy as np

D_MODEL = 1024
BATCH = 8
SEQ = 8192
DEPTH = 2

N_MIXERS = 2
N_CONV_LAYERS = (DEPTH + 1) // 2
N_LRU_LAYERS = DEPTH // 2
CONF_KERNEL = 31
D_RNN = D_MODEL
LRU_HEADS = 4
LRU_HEAD_DIM = D_RNN // LRU_HEADS
LRU_CONV = 4
LRU_C = 8.0
N_GROUPS = 4
EXPERTS_PER_GROUP = 8
N_EXPERTS = N_GROUPS * EXPERTS_PER_GROUP
TOP_K = 2
D_EXPERT = 512
ROW_BLOCK = 256
EPS = 1e-6

kernel_name = "hybrid_conformer_rglru_hmoe_adaln"


def rmsnorm(x, g):
    x32 = x.astype(jnp.float32)
    y = x32 * lax.rsqrt(jnp.mean(x32 * x32, axis=-1, keepdims=True) + EPS)
    return (y * g.astype(jnp.float32)).astype(x.dtype)


def modulate(h, shift, scale):
    return h * (1 + scale[:, None, :]) + shift[:, None, :]


def causal_depthwise_conv(x, w, b):
    k = w.shape[0]
    y = lax.conv_general_dilated(
        x, w[:, None, :].astype(x.dtype), window_strides=(1,),
        padding=[(k - 1, 0)], dimension_numbers=("NWC", "WIO", "NWC"),
        feature_group_count=x.shape[-1])
    return y + b


def conformer_conv(h, w1, b1, dw, dwb, ln_g, ln_b, w2, b2):
    u = h @ w1 + b1
    a, g = jnp.split(u, 2, axis=-1)
    v = a * jax.nn.sigmoid(g)
    v = causal_depthwise_conv(v, dw, dwb)
    v32 = v.astype(jnp.float32)
    mu = jnp.mean(v32, axis=-1, keepdims=True)
    var = jnp.mean(jnp.square(v32 - mu), axis=-1, keepdims=True)
    v = ((v32 - mu) * lax.rsqrt(var + EPS) * ln_g.astype(jnp.float32)
         + ln_b.astype(jnp.float32)).astype(h.dtype)
    v = jax.nn.silu(v)
    return v @ w2 + b2


def _lin_combine(left, right):
    a_l, b_l = left
    a_r, b_r = right
    return a_l * a_r, a_r * b_l + b_r


def rglru_block(h, wy, by, wx, bx, cw, cb, wa, ba, wi, bi, lam, wo, bo):
    bsz, s, _ = h.shape
    y = jax.nn.gelu(h @ wy + by)
    u = causal_depthwise_conv(h @ wx + bx, cw, cb)
    uh = u.reshape(bsz, s, LRU_HEADS, LRU_HEAD_DIM)
    r = jax.nn.sigmoid(jnp.einsum("bshi,hij->bshj", uh, wa) + ba).reshape(bsz, s, D_RNN)
    ig = jax.nn.sigmoid(jnp.einsum("bshi,hij->bshj", uh, wi) + bi).reshape(bsz, s, D_RNN)
    log_a = -LRU_C * r.astype(jnp.float32) * jax.nn.softplus(-lam.astype(jnp.float32))
    a = jnp.exp(log_a)
    mult = jnp.sqrt(-jnp.expm1(2.0 * log_a))
    b_term = mult * (ig * u).astype(jnp.float32)
    _, hs = lax.associative_scan(_lin_combine, (a, b_term), axis=1)
    return (hs.astype(h.dtype) * y) @ wo + bo


def hier_moe(h, wg, bg, wr, br, w_gate, w_up, w_down):
    bsz, s, d = h.shape
    t = bsz * s
    ht = h.reshape(t, d)
    glog = (ht @ wg + bg).astype(jnp.float32)
    gprob = jax.nn.softmax(glog, axis=-1)
    g_sel = jnp.argmax(glog, axis=-1).astype(jnp.int32)
    p_g = jnp.take_along_axis(gprob, g_sel[:, None], axis=1)[:, 0]
    elog_all = (ht @ wr + br).astype(jnp.float32).reshape(t, N_GROUPS, EXPERTS_PER_GROUP)
    elog = jnp.take_along_axis(elog_all, g_sel[:, None, None], axis=1)[:, 0]
    top_v, top_i = lax.top_k(elog, TOP_K)
    gate_w = jax.nn.softmax(top_v, axis=-1) * p_g[:, None]
    eid = (g_sel[:, None] * EXPERTS_PER_GROUP + top_i).reshape(-1).astype(jnp.int32)
    gates = gate_w.reshape(-1)
    tok = jnp.repeat(jnp.arange(t, dtype=jnp.int32), TOP_K)
    n_assign = t * TOP_K
    order = jnp.argsort(eid)
    s_eid, s_tok, s_gate = eid[order], tok[order], gates[order]
    counts = jax.ops.segment_sum(jnp.ones_like(eid), eid, num_segments=N_EXPERTS)
    starts = jnp.cumsum(counts) - counts
    padded = ((counts + ROW_BLOCK - 1) // ROW_BLOCK) * ROW_BLOCK
    pends = jnp.cumsum(padded)
    pstarts = pends - padded
    dest = pstarts[s_eid] + (jnp.arange(n_assign, dtype=jnp.int32) - starts[s_eid])
    n_blocks = -(-n_assign // ROW_BLOCK) + N_EXPERTS
    n_rows = n_blocks * ROW_BLOCK
    row_tok = jnp.full((n_rows,), t, jnp.int32).at[dest].set(s_tok)
    row_gate = jnp.zeros((n_rows,), h.dtype).at[dest].set(s_gate.astype(h.dtype))
    block_eid = jnp.minimum(
        jnp.searchsorted(pends, jnp.arange(n_blocks, dtype=jnp.int32) * ROW_BLOCK, side="right"),
        N_EXPERTS - 1).astype(jnp.int32)
    ht_pad = jnp.concatenate([ht, jnp.zeros((1, d), ht.dtype)], axis=0)

    def run_block(args):
        idx, e = args
        xb = ht_pad[idx]
        hid = jax.nn.silu(xb @ w_gate[e]) * (xb @ w_up[e])
        return hid @ w_down[e]

    y_rows = lax.map(run_block, (row_tok.reshape(n_blocks, ROW_BLOCK), block_eid))
    y_rows = y_rows.reshape(n_rows, d) * row_gate[:, None]
    y = jnp.zeros((t + 1, d), h.dtype).at[row_tok].add(y_rows)[:t]
    return y.reshape(bsz, s, d)


def setup_inputs(seed: int = 0) -> dict:
    key = jax.random.key(seed)
    ks = iter(jax.random.split(key, 64))
    f32 = jnp.float32

    def nrm(shape, scale):
        return jax.random.normal(next(ks), shape, f32) * scale

    def gain(shape):
        return 1.0 + 0.02 * jax.random.normal(next(ks), shape, f32)

    D, DR, NA, NB = D_MODEL, D_RNN, N_CONV_LAYERS, N_LRU_LAYERS
    s_lam = jax.random.uniform(next(ks), (NB, DR), f32, 0.9, 0.999) ** (1.0 / LRU_C)
    return {
        "x": jax.random.normal(next(ks), (BATCH, SEQ, D), f32),
        "c": jax.random.normal(next(ks), (BATCH, D), f32),
        "ada_w": nrm((DEPTH, D, 6 * D), D ** -0.5),
        "ada_b": nrm((DEPTH, 6 * D), 0.01),
        "norm1_g": gain((DEPTH, D)),
        "norm2_g": gain((DEPTH, D)),
        "cm_w1": nrm((NA, D, 2 * D), D ** -0.5),
        "cm_b1": nrm((NA, 2 * D), 0.01),
        "cm_dw": nrm((NA, CONF_KERNEL, D), CONF_KERNEL ** -0.5),
        "cm_dwb": nrm((NA, D), 0.01),
        "cm_ln_g": gain((NA, D)),
        "cm_ln_b": nrm((NA, D), 0.01),
        "cm_w2": nrm((NA, D, D), D ** -0.5),
        "cm_b2": nrm((NA, D), 0.01),
        "rg_wy": nrm((NB, D, DR), D ** -0.5),
        "rg_by": nrm((NB, DR), 0.01),
        "rg_wx": nrm((NB, D, DR), D ** -0.5),
        "rg_bx": nrm((NB, DR), 0.01),
        "rg_cw": nrm((NB, LRU_CONV, DR), LRU_CONV ** -0.5),
        "rg_cb": nrm((NB, DR), 0.01),
        "rg_wa": nrm((NB, LRU_HEADS, LRU_HEAD_DIM, LRU_HEAD_DIM), LRU_HEAD_DIM ** -0.5),
        "rg_ba": nrm((NB, LRU_HEADS, LRU_HEAD_DIM), 0.01),
        "rg_wi": nrm((NB, LRU_HEADS, LRU_HEAD_DIM, LRU_HEAD_DIM), LRU_HEAD_DIM ** -0.5),
        "rg_bi": nrm((NB, LRU_HEADS, LRU_HEAD_DIM), 0.01),
        "rg_lambda": jnp.log(s_lam) - jnp.log1p(-s_lam),
        "rg_wo": nrm((NB, DR, D), DR ** -0.5),
        "rg_bo": nrm((NB, D), 0.01),
        "moe_wg": nrm((DEPTH, D, N_GROUPS), D ** -0.5),
        "moe_bg": nrm((DEPTH, N_GROUPS), 0.01),
        "moe_wr": nrm((DEPTH, D, N_EXPERTS), D ** -0.5),
        "moe_br": nrm((DEPTH, N_EXPERTS), 0.01),
        "moe_w_gate": nrm((DEPTH, N_EXPERTS, D, D_EXPERT), D ** -0.5),
        "moe_w_up": nrm((DEPTH, N_EXPERTS, D, D_EXPERT), D ** -0.5),
        "moe_w_down": nrm((DEPTH, N_EXPERTS, D_EXPERT, D), D_EXPERT ** -0.5),
        "fin_ada_w": nrm((D, 2 * D), D ** -0.5),
        "fin_ada_b": nrm((2 * D,), 0.01),
        "fin_g": gain((D,)),
    }


def reference(x, c, ada_w, ada_b, norm1_g, norm2_g,
              cm_w1, cm_b1, cm_dw, cm_dwb, cm_ln_g, cm_ln_b, cm_w2, cm_b2,
              rg_wy, rg_by, rg_wx, rg_bx, rg_cw, rg_cb, rg_wa, rg_ba, rg_wi, rg_bi,
              rg_lambda, rg_wo, rg_bo,
              moe_wg, moe_bg, moe_wr, moe_br, moe_w_gate, moe_w_up, moe_w_down,
              fin_ada_w, fin_ada_b, fin_g):
    c_act = jax.nn.silu(c)
    for i in range(DEPTH):
        mod = c_act @ ada_w[i] + ada_b[i]
        sh1, sc1, g1, sh2, sc2, g2 = jnp.split(mod, 6, axis=-1)
        h = modulate(rmsnorm(x, norm1_g[i]), sh1, sc1)
        j = i // N_MIXERS
        if i % N_MIXERS == 0:
            m = conformer_conv(h, cm_w1[j], cm_b1[j], cm_dw[j], cm_dwb[j],
                               cm_ln_g[j], cm_ln_b[j], cm_w2[j], cm_b2[j])
        else:
            m = rglru_block(h, rg_wy[j], rg_by[j], rg_wx[j], rg_bx[j], rg_cw[j], rg_cb[j],
                            rg_wa[j], rg_ba[j], rg_wi[j], rg_bi[j], rg_lambda[j],
                            rg_wo[j], rg_bo[j])
        x = x + g1[:, None, :] * m
        h = modulate(rmsnorm(x, norm2_g[i]), sh2, sc2)
        x = x + g2[:, None, :] * hier_moe(h, moe_wg[i], moe_bg[i], moe_wr[i], moe_br[i],
                                          moe_w_gate[i], moe_w_up[i], moe_w_down[i])
    fsh, fsc = jnp.split(c_act @ fin_ada_w + fin_ada_b, 2, axis=-1)
    return modulate(rmsnorm(x, fin_g), fsh, fsc)
```

```python
import functools

import jax
import jax.numpy as jnp
import numpy as np
from jax import lax
from jax.experimental import pallas as pl
from jax.experimental.pallas import tpu as pltpu

EPS = 1e-6
CONF_KERNEL = 31
LRU_HEADS = 4
LRU_CONV = 4
LRU_C = 8.0
N_GROUPS = 4
EXPERTS_PER_GROUP = 8
N_EXPERTS = N_GROUPS * EXPERTS_PER_GROUP
PAIRS_PER_GROUP = EXPERTS_PER_GROUP * (EXPERTS_PER_GROUP - 1) // 2
N_CLASSES = N_GROUPS * PAIRS_PER_GROUP

LANES = 128
SUBLANES = 8
VMEM_LIMIT_BYTES = 56 * 1024 * 1024

TOKEN_TILE = 256
ROW_BLOCK = 256
CONV_HIST = 32
LRU_HIST = SUBLANES
ROUTER_ROWS = 64
GATE_LANES = LANES
MOD_ROWS = 8

_NEG_INF = float("-inf")


def _class_tables():
    lo = np.zeros((N_CLASSES,), np.int32)
    hi = np.zeros((N_CLASSES,), np.int32)
    for g in range(N_GROUPS):
        for a in range(EXPERTS_PER_GROUP):
            for b in range(a + 1, EXPERTS_PER_GROUP):
                c = g * PAIRS_PER_GROUP + (a * (2 * EXPERTS_PER_GROUP - 1 - a)) // 2 + (b - a - 1)
                lo[c] = g * EXPERTS_PER_GROUP + a
                hi[c] = g * EXPERTS_PER_GROUP + b
    return lo, hi


_CLASS_LO, _CLASS_HI = _class_tables()


def _bf16(x):
    return x.astype(jnp.bfloat16)


def _dot(a, b):
    return jnp.dot(a, b, preferred_element_type=jnp.float32)


def _ada_kernel(c_ref, w_ref, b_ref, o_ref):
    c = c_ref[...]
    ca = _bf16(c * jax.nn.sigmoid(c))
    o_ref[...] = _dot(ca, _bf16(w_ref[...])) + b_ref[...]


def _ada_call(c, w, b):
    n_layers, d, n = w.shape
    bsz = c.shape[0]
    nb = 1024
    return pl.pallas_call(
        _ada_kernel,
        out_shape=jax.ShapeDtypeStruct((n_layers, bsz, n), jnp.float32),
        grid=(n_layers, n // nb),
        in_specs=[
            pl.BlockSpec((bsz, d), lambda l, j: (0, 0)),
            pl.BlockSpec((None, d, nb), lambda l, j: (l, 0, j)),
            pl.BlockSpec((None, 1, nb), lambda l, j: (l, 0, j)),
        ],
        out_specs=pl.BlockSpec((None, bsz, nb), lambda l, j: (l, 0, j)),
        compiler_params=pltpu.CompilerParams(
            dimension_semantics=("arbitrary", "arbitrary"),
            vmem_limit_bytes=VMEM_LIMIT_BYTES),
        name="ada_mod",
    )(c, w, b)


def _rms_mod(x, gain, shift, scale):
    ms = jnp.mean(x * x, axis=-1, keepdims=True)
    return (x * lax.rsqrt(ms + EPS)) * (gain * (1.0 + scale)) + shift


def _route(logits_t):
    ts = logits_t.shape[1]
    row = lax.broadcasted_iota(jnp.int32, (SUBLANES, ts), 0)
    gl = jnp.where(row < N_GROUPS, logits_t[0:SUBLANES], _NEG_INF)
    gmax = jnp.max(gl, axis=0, keepdims=True)
    gsel = jnp.min(jnp.where(gl == gmax, row, SUBLANES), axis=0, keepdims=True)
    p_g = 1.0 / jnp.sum(jnp.exp(gl - gmax), axis=0, keepdims=True)
    es = jnp.zeros((SUBLANES, ts), jnp.float32)
    for g in range(N_GROUPS):
        lo = SUBLANES * (1 + g)
        es = jnp.where(gsel == g, logits_t[lo:lo + EXPERTS_PER_GROUP], es)
    m1 = jnp.max(es, axis=0, keepdims=True)
    i1 = jnp.min(jnp.where(es == m1, row, SUBLANES), axis=0, keepdims=True)
    es2 = jnp.where(row == i1, _NEG_INF, es)
    m2 = jnp.max(es2, axis=0, keepdims=True)
    i2 = jnp.min(jnp.where(es2 == m2, row, SUBLANES), axis=0, keepdims=True)
    t = jnp.exp(m2 - m1)
    w1 = p_g / (1.0 + t)
    w2 = w1 * t
    first_lo = i1 < i2
    e_lo = jnp.minimum(i1, i2)
    e_hi = jnp.maximum(i1, i2)
    gate_lo = jnp.where(first_lo, w1, w2)
    gate_hi = jnp.where(first_lo, w2, w1)
    pair = lax.shift_right_logical(e_lo * (2 * EXPERTS_PER_GROUP - 1 - e_lo), 1) + (e_hi - e_lo - 1)
    cls = gsel * PAIRS_PER_GROUP + pair
    return cls, gate_lo, gate_hi


def _post_mixer(x1, mod_ref, n2_ref, wrt_ref, brt_ref, hrow_ref, cls_ref, cnt_ref, step):
    d = x1.shape[1]
    ts = x1.shape[0]
    h2 = _rms_mod(x1, n2_ref[...], mod_ref[3:4, :], mod_ref[4:5, :])
    h2b = _bf16(h2)
    logits_t = lax.dot_general(wrt_ref[...], h2b, (((1,), (1,)), ((), ())),
                               preferred_element_type=jnp.float32) + brt_ref[:, 0:1]
    cls, gate_lo, gate_hi = _route(logits_t)
    gates_t = jnp.concatenate(
        [gate_lo, gate_hi, jnp.zeros((GATE_LANES - 2, ts), jnp.float32)], axis=0)
    hrow_ref[:, 0:d] = h2
    hrow_ref[:, d:d + GATE_LANES] = gates_t.T
    cls_ref[0] = cls
    crow = lax.broadcasted_iota(jnp.int32, (LANES, ts), 0)
    onehot = jnp.where(crow == cls, 1.0, 0.0)
    cnt = jnp.sum(onehot, axis=1, keepdims=True)

    @pl.when(step == 0)
    def _():
        cnt_ref[...] = jnp.zeros_like(cnt_ref)

    cnt_ref[...] += jnp.broadcast_to(cnt, cnt_ref.shape)


def _conf_kernel(tiles_per_seq, x_ref, mod_ref, n1_ref, n2_ref, w1_ref, b1_ref, dw_ref, dwb_ref,
                 lng_ref, lnb_ref, w2_ref, b2_ref, wrt_ref, brt_ref,
                 x1_ref, hrow_ref, cls_ref, cnt_ref, vbuf):
    i = pl.program_id(0)
    ts, d = x_ref.shape
    x = x_ref[...]
    h = _bf16(_rms_mod(x, n1_ref[...], mod_ref[0:1, :], mod_ref[1:2, :]))
    u = _dot(h, w1_ref[...]) + b1_ref[...]
    v = u[:, 0:d] * jax.nn.sigmoid(u[:, d:2 * d])

    @pl.when(i % tiles_per_seq == 0)
    def _():
        vbuf[0:CONV_HIST, :] = jnp.zeros((CONV_HIST, d), jnp.float32)

    vbuf[CONV_HIST:CONV_HIST + ts, :] = v
    base = CONV_HIST - (CONF_KERNEL - 1)
    cols = []
    for cb in range(d // LANES):
        sl = slice(cb * LANES, (cb + 1) * LANES)
        acc = jnp.broadcast_to(dwb_ref[:, sl], (ts, LANES))
        for k in range(CONF_KERNEL):
            acc = acc + dw_ref[k:k + 1, sl] * vbuf[base + k:base + k + ts, sl]
        cols.append(acc)
    y = jnp.concatenate(cols, axis=1)
    vbuf[0:CONV_HIST, :] = vbuf[ts:ts + CONV_HIST, :]

    mu = jnp.mean(y, axis=-1, keepdims=True)
    yc = y - mu
    var = jnp.mean(yc * yc, axis=-1, keepdims=True)
    z = yc * lax.rsqrt(var + EPS) * lng_ref[...] + lnb_ref[...]
    z = z * jax.nn.sigmoid(z)
    m = _dot(_bf16(z), w2_ref[...]) + b2_ref[...]
    x1 = x + mod_ref[2:3, :] * m
    x1_ref[...] = x1
    _post_mixer(x1, mod_ref, n2_ref, wrt_ref, brt_ref, hrow_ref, cls_ref, cnt_ref, i)


def _mixer_out(t, d, ts):
    n_tiles = t // ts
    out_shape = (
        jax.ShapeDtypeStruct((t, d), jnp.float32),
        jax.ShapeDtypeStruct((t, d + GATE_LANES), jnp.float32),
        jax.ShapeDtypeStruct((n_tiles, 1, ts), jnp.int32),
        jax.ShapeDtypeStruct((LANES, LANES), jnp.float32),
    )
    out_specs = (
        pl.BlockSpec((ts, d), lambda i: (i, 0)),
        pl.BlockSpec((ts, d + GATE_LANES), lambda i: (i, 0)),
        pl.BlockSpec((1, 1, ts), lambda i: (i, 0, 0)),
        pl.BlockSpec((LANES, LANES), lambda i: (0, 0)),
    )
    return out_shape, out_specs


def _full(shape):
    return pl.BlockSpec(shape, lambda i: tuple(0 for _ in shape))


def _conf_call(x, mod, n1, n2, w1, b1, dw, dwb, lng, lnb, w2, b2, wrt, brt, seq):
    t, d = x.shape
    ts = TOKEN_TILE
    tiles_per_seq = seq // ts
    out_shape, out_specs = _mixer_out(t, d, ts)
    return pl.pallas_call(
        functools.partial(_conf_kernel, tiles_per_seq),
        out_shape=out_shape,
        grid=(t // ts,),
        in_specs=[
            pl.BlockSpec((ts, d), lambda i: (i, 0)),
            pl.BlockSpec((None, MOD_ROWS, d), lambda i: (i // tiles_per_seq, 0, 0)),
            _full((1, d)), _full((1, d)),
            _full((d, 2 * d)), _full((1, 2 * d)),
            _full((CONV_HIST, d)), _full((1, d)),
            _full((1, d)), _full((1, d)),
            _full((d, d)), _full((1, d)),
            _full((ROUTER_ROWS, d)), _full((ROUTER_ROWS, LANES)),
        ],
        out_specs=out_specs,
        scratch_shapes=[pltpu.VMEM((CONV_HIST + ts, d), jnp.float32)],
        compiler_params=pltpu.CompilerParams(
            dimension_semantics=("arbitrary",), vmem_limit_bytes=VMEM_LIMIT_BYTES),
        name="conf_mixer",
    )(x, mod, n1, n2, w1, b1, dw, dwb, lng, lnb, w2, b2, wrt, brt)


def _shift_rows(x, s, fill):
    rolled = pltpu.roll(x, s, 0)
    row = lax.broadcasted_iota(jnp.int32, x.shape, 0) & (SUBLANES - 1)
    return jnp.where(row < s, fill, rolled)


def _lru_kernel(tiles_per_seq, x_ref, mod_ref, n1_ref, n2_ref, wy_ref, by_ref, wx_ref, bx_ref,
                cw_ref, cb_ref, wa_ref, ba_ref, wi_ref, bi_ref, lam_ref, wo_ref, bo_ref,
                wrt_ref, brt_ref,
                x1_ref, hrow_ref, cls_ref, cnt_ref, ubuf, hcar):
    i = pl.program_id(0)
    ts, d = x_ref.shape
    hd = d // LRU_HEADS
    x = x_ref[...]
    h = _bf16(_rms_mod(x, n1_ref[...], mod_ref[0:1, :], mod_ref[1:2, :]))
    ygate = jax.nn.gelu(_dot(h, wy_ref[...]) + by_ref[...])
    xb = _dot(h, wx_ref[...]) + bx_ref[...]

    @pl.when(i % tiles_per_seq == 0)
    def _():
        ubuf[0:LRU_HIST, :] = jnp.zeros((LRU_HIST, d), jnp.float32)
        hcar[...] = jnp.zeros_like(hcar)

    ubuf[LRU_HIST:LRU_HIST + ts, :] = xb
    base = LRU_HIST - (LRU_CONV - 1)
    u = jnp.broadcast_to(cb_ref[...], (ts, d))
    for k in range(LRU_CONV):
        u = u + cw_ref[k:k + 1, :] * ubuf[base + k:base + k + ts, :]
    ubuf[0:LRU_HIST, :] = ubuf[ts:ts + LRU_HIST, :]

    ub = _bf16(u)
    r_parts, i_parts = [], []
    for hh in range(LRU_HEADS):
        uh = ub[:, hh * hd:(hh + 1) * hd]
        r_parts.append(_dot(uh, wa_ref[hh]))
        i_parts.append(_dot(uh, wi_ref[hh]))
    r = jax.nn.sigmoid(jnp.concatenate(r_parts, axis=1) + ba_ref[...])
    ig = jax.nn.sigmoid(jnp.concatenate(i_parts, axis=1) + bi_ref[...])

    nl = -lam_ref[...]
    softplus = jnp.maximum(nl, 0.0) + jnp.log1p(jnp.exp(-jnp.abs(nl)))
    log_a = (-LRU_C) * r * softplus
    a = jnp.exp(log_a)
    mult = jnp.sqrt(-jnp.tanh(log_a) * (1.0 + a * a))
    b = mult * (ig * u)

    s = 1
    while s < SUBLANES:
        a_prev = _shift_rows(a, s, 1.0)
        b_prev = _shift_rows(b, s, 0.0)
        b = a * b_prev + b
        a = a * a_prev
        s *= 2
    carry = hcar[...]
    outs = []
    for gidx in range(ts // SUBLANES):
        sl = slice(gidx * SUBLANES, (gidx + 1) * SUBLANES)
        hg = a[sl] * carry + b[sl]
        outs.append(hg)
        carry = hg[SUBLANES - 1:SUBLANES, :]
    hcar[...] = carry
    hs = jnp.concatenate(outs, axis=0)

    m = _dot(_bf16(hs * ygate), wo_ref[...]) + bo_ref[...]
    x1 = x + mod_ref[2:3, :] * m
    x1_ref[...] = x1
    _post_mixer(x1, mod_ref, n2_ref, wrt_ref, brt_ref, hrow_ref, cls_ref, cnt_ref, i)


def _lru_call(x, mod, n1, n2, wy, by, wx, bx, cw, cb, wa, ba, wi, bi, lam, wo, bo, wrt, brt, seq):
    t, d = x.shape
    ts = TOKEN_TILE
    hd = d // LRU_HEADS
    tiles_per_seq = seq // ts
    out_shape, out_specs = _mixer_out(t, d, ts)
    return pl.pallas_call(
        functools.partial(_lru_kernel, tiles_per_seq),
        out_shape=out_shape,
        grid=(t // ts,),
        in_specs=[
            pl.BlockSpec((ts, d), lambda i: (i, 0)),
            pl.BlockSpec((None, MOD_ROWS, d), lambda i: (i // tiles_per_seq, 0, 0)),
            _full((1, d)), _full((1, d)),
            _full((d, d)), _full((1, d)),
            _full((d, d)), _full((1, d)),
            _full((SUBLANES, d)), _full((1, d)),
            _full((LRU_HEADS, hd, hd)), _full((1, d)),
            _full((LRU_HEADS, hd, hd)), _full((1, d)),
            _full((1, d)),
            _full((d, d)), _full((1, d)),
            _full((ROUTER_ROWS, d)), _full((ROUTER_ROWS, LANES)),
        ],
        out_specs=out_specs,
        scratch_shapes=[pltpu.VMEM((LRU_HIST + ts, d), jnp.float32),
                        pltpu.VMEM((1, d), jnp.float32)],
        compiler_params=pltpu.CompilerParams(
            dimension_semantics=("arbitrary",), vmem_limit_bytes=VMEM_LIMIT_BYTES),
        name="lru_mixer",
    )(x, mod, n1, n2, wy, by, wx, bx, cw, cb, wa, ba, wi, bi, lam, wo, bo, wrt, brt)


def _dispatch_kernel(n_tiles, n_blocks, cls_ref, cnt_ref, clo_ref, chi_ref, hrow_ref, xs_in_ref,
                     xs_ref, dest_ref, blo_ref, bhi_ref, nblk_ref, cursor, sem):
    del xs_in_ref
    i = pl.program_id(0)
    ts = cls_ref.shape[2]

    @pl.when(i == 0)
    def _():
        def per_class(c, nb_total):
            cursor[c] = nb_total * ROW_BLOCK
            nb = lax.shift_right_logical(cnt_ref[c] + (ROW_BLOCK - 1), ROW_BLOCK.bit_length() - 1)

            def fill(k, carry):
                blo_ref[nb_total + k] = clo_ref[c]
                bhi_ref[nb_total + k] = chi_ref[c]
                return carry

            lax.fori_loop(0, nb, fill, 0)
            return nb_total + nb

        nb_used = lax.fori_loop(0, N_CLASSES, per_class, 0)
        nblk_ref[0] = nb_used
        last = jnp.maximum(nb_used - 1, 0)

        def tail(k, carry):
            blo_ref[k] = blo_ref[last]
            bhi_ref[k] = bhi_ref[last]
            return carry

        lax.fori_loop(nb_used, n_blocks, tail, 0)

    def row_copy(r, dst_row):
        return pltpu.make_async_copy(hrow_ref.at[pl.ds(i * ts + r, 1)],
                                     xs_ref.at[pl.ds(dst_row, 1)], sem)

    def issue(r, carry):
        c = cls_ref[0, 0, r]
        dst_row = cursor[c]
        cursor[c] = dst_row + 1
        dest_ref[0, 0, r] = dst_row
        row_copy(r, dst_row).start()
        return carry

    lax.fori_loop(0, ts, issue, 0)

    def drain(r, carry):
        row_copy(r, 0).wait()
        return carry

    @pl.when(i > 0)
    def _():
        lax.fori_loop(0, ts, drain, 0)

    @pl.when(i == n_tiles - 1)
    def _():
        lax.fori_loop(0, ts, drain, 0)


def _dispatch_call(cls, counts, hrows, n_blocks):
    n_tiles, _, ts = cls.shape
    t, width = hrows.shape
    n_rows = n_blocks * ROW_BLOCK
    xs0 = jnp.zeros((n_rows, width), jnp.float32)
    smem_full = pl.BlockSpec(memory_space=pltpu.SMEM)
    any_spec = pl.BlockSpec(memory_space=pl.ANY)
    return pl.pallas_call(
        functools.partial(_dispatch_kernel, n_tiles, n_blocks),
        out_shape=(
            jax.ShapeDtypeStruct((n_rows, width), jnp.float32),
            jax.ShapeDtypeStruct((n_tiles, 1, ts), jnp.int32),
            jax.ShapeDtypeStruct((n_blocks,), jnp.int32),
            jax.ShapeDtypeStruct((n_blocks,), jnp.int32),
            jax.ShapeDtypeStruct((1,), jnp.int32),
        ),
        grid=(n_tiles,),
        in_specs=[
            pl.BlockSpec((1, 1, ts), lambda i: (i, 0, 0), memory_space=pltpu.SMEM),
            smem_full, smem_full, smem_full, any_spec, any_spec,
        ],
        out_specs=(
            any_spec,
            pl.BlockSpec((1, 1, ts), lambda i: (i, 0, 0), memory_space=pltpu.SMEM),
            smem_full, smem_full, smem_full,
        ),
        scratch_shapes=[pltpu.SMEM((LANES,), jnp.int32), pltpu.SemaphoreType.DMA],
        input_output_aliases={5: 0},
        compiler_params=pltpu.CompilerParams(
            dimension_semantics=("arbitrary",), vmem_limit_bytes=VMEM_LIMIT_BYTES),
        name="moe_dispatch",
    )(cls, counts, jnp.asarray(_CLASS_LO), jnp.asarray(_CLASS_HI), hrows, xs0)


def _expert_kernel(blo_ref, bhi_ref, nblk_ref, xs_ref, wg_lo, wu_lo, wd_lo, wg_hi, wu_hi, wd_hi, ys_ref):
    del blo_ref, bhi_ref
    b = pl.program_id(0)
    d = ys_ref.shape[1]

    @pl.when(b < nblk_ref[0])
    def _():
        xb = _bf16(xs_ref[:, 0:d])
        gates = xs_ref[:, d:d + GATE_LANES]
        acc = None
        for idx, (wg, wu, wd) in enumerate(((wg_lo, wu_lo, wd_lo), (wg_hi, wu_hi, wd_hi))):
            a = _dot(xb, wg[...])
            hid = (a * jax.nn.sigmoid(a)) * _dot(xb, wu[...]) * gates[:, idx:idx + 1]
            part = _dot(_bf16(hid), wd[...])
            acc = part if acc is None else acc + part
        ys_ref[...] = acc

    @pl.when(b >= nblk_ref[0])
    def _():
        ys_ref[...] = jnp.zeros_like(ys_ref)


def _expert_call(blo, bhi, nblk, xs, wg, wu, wd):
    n_rows, width = xs.shape
    d = width - GATE_LANES
    de = wg.shape[2]
    n_blocks = n_rows // ROW_BLOCK

    def w_spec(shape, which):
        if which == 0:
            return pl.BlockSpec((None,) + shape, lambda b, lo, hi, n: (lo[b], 0, 0))
        return pl.BlockSpec((None,) + shape, lambda b, lo, hi, n: (hi[b], 0, 0))

    def x_map(b, lo, hi, n):
        return (jnp.minimum(b, jnp.maximum(n[0] - 1, 0)), 0)

    grid_spec = pltpu.PrefetchScalarGridSpec(
        num_scalar_prefetch=3,
        grid=(n_blocks,),
        in_specs=[
            pl.BlockSpec((ROW_BLOCK, width), x_map),
            w_spec((d, de), 0), w_spec((d, de), 0), w_spec((de, d), 0),
            w_spec((d, de), 1), w_spec((d, de), 1), w_spec((de, d), 1),
        ],
        out_specs=pl.BlockSpec((ROW_BLOCK, d), lambda b, lo, hi, n: (b, 0)),
    )
    return pl.pallas_call(
        _expert_kernel,
        out_shape=jax.ShapeDtypeStruct((n_rows, d), jnp.float32),
        grid_spec=grid_spec,
        compiler_params=pltpu.CompilerParams(
            dimension_semantics=("arbitrary",), vmem_limit_bytes=VMEM_LIMIT_BYTES),
        name="moe_experts",
    )(blo, bhi, nblk, xs, wg, wu, wd, wg, wu, wd)


def _combine_kernel(n_tiles, final, dest_ref, dnext_ref, x1_ref, mod_ref, *rest):
    if final:
        fmod_ref, fg_ref, ys_ref, out_ref, ybuf, sem = rest
    else:
        ys_ref, out_ref, ybuf, sem = rest
    i = pl.program_id(0)
    ts = x1_ref.shape[0]
    slot = i % 2

    def row_copy(dref, r, sl):
        return pltpu.make_async_copy(ys_ref.at[pl.ds(dref[0, 0, r], 1)],
                                     ybuf.at[sl, pl.ds(r, 1)], sem.at[sl])

    def issue(dref, sl):
        def body(r, carry):
            row_copy(dref, r, sl).start()
            return carry
        lax.fori_loop(0, ts, body, 0)

    @pl.when(i == 0)
    def _():
        issue(dest_ref, 0)

    @pl.when(i + 1 < n_tiles)
    def _():
        issue(dnext_ref, 1 - slot)

    def drain(r, carry):
        row_copy(dest_ref, r, slot).wait()
        return carry

    lax.fori_loop(0, ts, drain, 0)
    x2 = x1_ref[...] + mod_ref[5:6, :] * ybuf[slot]
    if final:
        out_ref[...] = _rms_mod(x2, fg_ref[...], fmod_ref[0:1, :], fmod_ref[1:2, :])
    else:
        out_ref[...] = x2


def _combine_call(dest, x1, mod, ys, seq, fmod=None, fin_g=None):
    t, d = x1.shape
    n_tiles, _, ts = dest.shape
    tiles_per_seq = seq // ts
    final = fmod is not None
    mod_spec = pl.BlockSpec((None, MOD_ROWS, d), lambda i: (i // tiles_per_seq, 0, 0))
    in_specs = [
        pl.BlockSpec((1, 1, ts), lambda i: (i, 0, 0), memory_space=pltpu.SMEM),
        pl.BlockSpec((1, 1, ts), lambda i: (jnp.minimum(i + 1, n_tiles - 1), 0, 0),
                     memory_space=pltpu.SMEM),
        pl.BlockSpec((ts, d), lambda i: (i, 0)),
        mod_spec,
    ]
    args = [dest, dest, x1, mod]
    if final:
        in_specs += [mod_spec, _full((1, d))]
        args += [fmod, fin_g]
    in_specs.append(pl.BlockSpec(memory_space=pl.ANY))
    args.append(ys)
    return pl.pallas_call(
        functools.partial(_combine_kernel, n_tiles, final),
        out_shape=jax.ShapeDtypeStruct((t, d), jnp.float32),
        grid=(n_tiles,),
        in_specs=in_specs,
        out_specs=pl.BlockSpec((ts, d), lambda i: (i, 0)),
        scratch_shapes=[pltpu.VMEM((2, ts, d), jnp.float32), pltpu.SemaphoreType.DMA((2,))],
        compiler_params=pltpu.CompilerParams(
            dimension_semantics=("arbitrary",), vmem_limit_bytes=VMEM_LIMIT_BYTES),
        name="moe_combine_final" if final else "moe_combine",
    )(*args)


def _router_operands(wg, bg, wr, br):
    d = wg.shape[0]
    pad_g = jnp.zeros((SUBLANES - N_GROUPS, d), jnp.float32)
    pad_e = jnp.zeros((ROUTER_ROWS - SUBLANES - N_EXPERTS, d), jnp.float32)
    wrt = jnp.concatenate([wg.T, pad_g, wr.T, pad_e], axis=0)
    bcol = jnp.concatenate([bg, jnp.zeros((SUBLANES - N_GROUPS,), jnp.float32), br,
                            jnp.zeros((ROUTER_ROWS - SUBLANES - N_EXPERTS,), jnp.float32)])
    brt = jnp.broadcast_to(bcol[:, None], (ROUTER_ROWS, LANES))
    return _bf16(wrt), brt


def _mod_table(m, parts):
    bsz = m.shape[0]
    d = m.shape[1] // parts
    m = m.reshape(bsz, parts, d)
    return jnp.concatenate([m, jnp.zeros((bsz, MOD_ROWS - parts, d), jnp.float32)], axis=1)


def _row(v):
    return v.reshape(1, -1)


def _moe(x1, hrows, cls, cnt, mod, wg, wu, wd, seq, fmod=None, fin_g=None):
    t = x1.shape[0]
    n_blocks = t // ROW_BLOCK + N_CLASSES
    counts = cnt[:, 0].astype(jnp.int32)
    xs, dest, blo, bhi, nblk = _dispatch_call(cls, counts, hrows, n_blocks)
    ys = _expert_call(blo, bhi, nblk, xs, wg, wu, wd)
    return _combine_call(dest, x1, mod, ys, seq, fmod, fin_g)


def kernel(x, c, ada_w, ada_b, norm1_g, norm2_g, cm_w1, cm_b1, cm_dw, cm_dwb, cm_ln_g, cm_ln_b, cm_w2, cm_b2, rg_wy, rg_by, rg_wx, rg_bx, rg_cw, rg_cb, rg_wa, rg_ba, rg_wi, rg_bi, rg_lambda, rg_wo, rg_bo, moe_wg, moe_bg, moe_wr, moe_br, moe_w_gate, moe_w_up, moe_w_down, fin_ada_w, fin_ada_b, fin_g):
    bsz, seq, d = x.shape
    t = bsz * seq
    depth = ada_w.shape[0]
    assert depth == 2 and seq % TOKEN_TILE == 0 and t % ROW_BLOCK == 0 and d % LANES == 0

    mods = _ada_call(c, ada_w, ada_b[:, None, :])
    fmods = _ada_call(c, fin_ada_w[None], fin_ada_b[None, None, :])
    fmod = _mod_table(fmods[0], 2)

    xt = x.reshape(t, d)
    for layer in range(depth):
        mod = _mod_table(mods[layer], 6)
        wrt, brt = _router_operands(moe_wg[layer], moe_bg[layer], moe_wr[layer], moe_br[layer])
        n1, n2 = _row(norm1_g[layer]), _row(norm2_g[layer])
        if layer == 0:
            dw = jnp.concatenate(
                [cm_dw[0], jnp.zeros((CONV_HIST - CONF_KERNEL, d), jnp.float32)], axis=0)
            x1, hrows, cls, cnt = _conf_call(
                xt, mod, n1, n2, _bf16(cm_w1[0]), _row(cm_b1[0]), dw, _row(cm_dwb[0]),
                _row(cm_ln_g[0]), _row(cm_ln_b[0]), _bf16(cm_w2[0]), _row(cm_b2[0]), wrt, brt, seq)
        else:
            cw = jnp.concatenate(
                [rg_cw[0], jnp.zeros((SUBLANES - LRU_CONV, d), jnp.float32)], axis=0)
            x1, hrows, cls, cnt = _lru_call(
                xt, mod, n1, n2, _bf16(rg_wy[0]), _row(rg_by[0]), _bf16(rg_wx[0]), _row(rg_bx[0]),
                cw, _row(rg_cb[0]), _bf16(rg_wa[0]), _row(rg_ba[0]), _bf16(rg_wi[0]), _row(rg_bi[0]),
                _row(rg_lambda[0]), _bf16(rg_wo[0]), _row(rg_bo[0]), wrt, brt, seq)
        last = layer == depth - 1
        xt = _moe(x1, hrows, cls, cnt, mod,
                  _bf16(moe_w_gate[layer]), _bf16(moe_w_up[layer]), _bf16(moe_w_down[layer]), seq,
                  fmod if last else None, _row(fin_g) if last else None)
    return xt.reshape(bsz, seq, d)
```

```python
import functools

import jax
import jax.numpy as jnp
import numpy as np
from jax import lax
from jax.experimental import pallas as pl
from jax.experimental.pallas import tpu as pltpu

EPS = 1e-6
CONF_KERNEL = 31
LRU_HEADS = 4
LRU_CONV = 4
LRU_C = 8.0
N_GROUPS = 4
EXPERTS_PER_GROUP = 8
N_EXPERTS = N_GROUPS * EXPERTS_PER_GROUP
PAIRS_PER_GROUP = EXPERTS_PER_GROUP * (EXPERTS_PER_GROUP - 1) // 2
N_CLASSES = N_GROUPS * PAIRS_PER_GROUP

LANES = 128
SUBLANES = 8
VMEM_LIMIT_BYTES = 56 * 1024 * 1024

TOKEN_TILE = 256
ROW_BLOCK = 256
CONV_HIST = 32
LRU_HIST = SUBLANES
ROUTER_ROWS = 64
PACK_SUBLANES = 4
MOD_ROWS = 8
DISPATCH_SLOTS = 3
DMA_UNROLL = 8

_NEG_INF = float("-inf")


def _class_tables():
    lo = np.zeros((N_CLASSES,), np.int32)
    hi = np.zeros((N_CLASSES,), np.int32)
    for g in range(N_GROUPS):
        for a in range(EXPERTS_PER_GROUP):
            for b in range(a + 1, EXPERTS_PER_GROUP):
                c = g * PAIRS_PER_GROUP + (a * (2 * EXPERTS_PER_GROUP - 1 - a)) // 2 + (b - a - 1)
                lo[c] = g * EXPERTS_PER_GROUP + a
                hi[c] = g * EXPERTS_PER_GROUP + b
    return lo, hi


_CLASS_LO, _CLASS_HI = _class_tables()


def _bf16(x):
    return x.astype(jnp.bfloat16)


def _dot(a, b):
    return jnp.dot(a, b, preferred_element_type=jnp.float32)


def _bits(x):
    return lax.bitcast_convert_type(x, jnp.uint32)


def _f32_from_bits(x):
    return lax.bitcast_convert_type(x, jnp.float32)


def _token_rows(ref, sublane, n):
    return ref[pl.ds(sublane, n, stride=SUBLANES), :]


def _ada_kernel(c_ref, w_ref, b_ref, o_ref):
    c = c_ref[...]
    ca = _bf16(c * jax.nn.sigmoid(c))
    o_ref[...] = _dot(ca, _bf16(w_ref[...])) + b_ref[...]


def _ada_call(c, w, b):
    n_layers, d, n = w.shape
    bsz = c.shape[0]
    nb = 1024
    return pl.pallas_call(
        _ada_kernel,
        out_shape=jax.ShapeDtypeStruct((n_layers, bsz, n), jnp.float32),
        grid=(n_layers, n // nb),
        in_specs=[
            pl.BlockSpec((bsz, d), lambda l, j: (0, 0)),
            pl.BlockSpec((None, d, nb), lambda l, j: (l, 0, j)),
            pl.BlockSpec((None, 1, nb), lambda l, j: (l, 0, j)),
        ],
        out_specs=pl.BlockSpec((None, bsz, nb), lambda l, j: (l, 0, j)),
        compiler_params=pltpu.CompilerParams(
            dimension_semantics=("arbitrary", "arbitrary"),
            vmem_limit_bytes=VMEM_LIMIT_BYTES),
        name="ada_mod",
    )(c, w, b)


def _rms_mod(x, gain, shift, scale):
    ms = jnp.mean(x * x, axis=-1, keepdims=True)
    return (x * lax.rsqrt(ms + EPS)) * (gain * (1.0 + scale)) + shift


def _causal_taps(w_ref, bias_ref, base_ref, shifted_ref, hist, n_taps, ts, d):
    full = base_ref[...]
    for r in range(1, shifted_ref.shape[0] + 1):
        shifted_ref[r - 1, r:r + hist + ts, :] = full
    cols = []
    for cb in range(d // LANES):
        sl = slice(cb * LANES, (cb + 1) * LANES)
        acc = jnp.broadcast_to(bias_ref[:, sl], (ts, LANES))
        for k in range(n_taps):
            q, r = divmod(n_taps - 1 - k, SUBLANES)
            lo = hist - SUBLANES * q
            if r == 0:
                win = base_ref[lo:lo + ts, sl]
            else:
                win = shifted_ref[r - 1, lo:lo + ts, sl]
            acc = acc + w_ref[k:k + 1, sl] * win
        cols.append(acc)
    return jnp.concatenate(cols, axis=1)


def _route(logits_t):
    ts = logits_t.shape[1]
    row = lax.broadcasted_iota(jnp.int32, (SUBLANES, ts), 0)
    gl = jnp.where(row < N_GROUPS, logits_t[0:SUBLANES], _NEG_INF)
    gmax = jnp.max(gl, axis=0, keepdims=True)
    gsel = jnp.min(jnp.where(gl == gmax, row, SUBLANES), axis=0, keepdims=True)
    p_g = 1.0 / jnp.sum(jnp.exp(gl - gmax), axis=0, keepdims=True)
    es = jnp.zeros((SUBLANES, ts), jnp.float32)
    for g in range(N_GROUPS):
        lo = SUBLANES * (1 + g)
        es = jnp.where(gsel == g, logits_t[lo:lo + EXPERTS_PER_GROUP], es)
    m1 = jnp.max(es, axis=0, keepdims=True)
    i1 = jnp.min(jnp.where(es == m1, row, SUBLANES), axis=0, keepdims=True)
    es2 = jnp.where(row == i1, _NEG_INF, es)
    m2 = jnp.max(es2, axis=0, keepdims=True)
    i2 = jnp.min(jnp.where(es2 == m2, row, SUBLANES), axis=0, keepdims=True)
    t = jnp.exp(m2 - m1)
    w1 = p_g / (1.0 + t)
    w2 = w1 * t
    first_lo = i1 < i2
    e_lo = jnp.minimum(i1, i2)
    e_hi = jnp.maximum(i1, i2)
    gate_lo = jnp.where(first_lo, w1, w2)
    gate_hi = jnp.where(first_lo, w2, w1)
    pair = lax.shift_right_logical(e_lo * (2 * EXPERTS_PER_GROUP - 1 - e_lo), 1) + (e_hi - e_lo - 1)
    cls = gsel * PAIRS_PER_GROUP + pair
    return cls, gate_lo, gate_hi


def _post_mixer(x1, mod_ref, n2_ref, wrt_ref, brt_ref, hrow_ref, cls_ref, cnt_ref, step):
    ts = x1.shape[0]
    h2 = _rms_mod(x1, n2_ref[...], mod_ref[3:4, :], mod_ref[4:5, :])
    h2b = _bf16(h2)
    logits_t = lax.dot_general(wrt_ref[...], h2b, (((1,), (1,)), ((), ())),
                               preferred_element_type=jnp.float32) + brt_ref[:, 0:1]
    cls, gate_lo, gate_hi = _route(logits_t)
    gates_t = jnp.concatenate(
        [gate_lo, gate_hi, jnp.zeros((LANES - 2, ts), jnp.float32)], axis=0)
    h2r = h2b.astype(jnp.float32)
    for j in range(PACK_SUBLANES):
        hi = _bits(h2r[:, (2 * j) * LANES:(2 * j + 1) * LANES])
        lo = _bits(h2r[:, (2 * j + 1) * LANES:(2 * j + 2) * LANES])
        hrow_ref[pl.ds(j, ts, stride=SUBLANES), :] = hi | lax.shift_right_logical(lo, jnp.uint32(16))
    hrow_ref[pl.ds(PACK_SUBLANES, ts, stride=SUBLANES), :] = _bits(gates_t.T)
    for j in range(PACK_SUBLANES + 1, SUBLANES):
        hrow_ref[pl.ds(j, ts, stride=SUBLANES), :] = jnp.zeros((ts, LANES), jnp.uint32)
    cls_ref[0] = cls
    crow = lax.broadcasted_iota(jnp.int32, (LANES, ts), 0)
    onehot = jnp.where(crow == cls, 1.0, 0.0)
    cnt = jnp.sum(onehot, axis=1, keepdims=True)

    @pl.when(step == 0)
    def _():
        cnt_ref[...] = jnp.zeros_like(cnt_ref)

    cnt_ref[...] += jnp.broadcast_to(cnt, cnt_ref.shape)


def _mixer_out(t, d, ts):
    n_tiles = t // ts
    out_shape = (
        jax.ShapeDtypeStruct((t, d), jnp.float32),
        jax.ShapeDtypeStruct((t * SUBLANES, LANES), jnp.uint32),
        jax.ShapeDtypeStruct((n_tiles, 1, ts), jnp.int32),
        jax.ShapeDtypeStruct((LANES, LANES), jnp.float32),
    )
    out_specs = (
        pl.BlockSpec((ts, d), lambda i: (i, 0)),
        pl.BlockSpec((ts * SUBLANES, LANES), lambda i: (i, 0)),
        pl.BlockSpec((1, 1, ts), lambda i: (i, 0, 0)),
        pl.BlockSpec((LANES, LANES), lambda i: (0, 0)),
    )
    return out_shape, out_specs


def _full(shape):
    return pl.BlockSpec(shape, lambda i: tuple(0 for _ in shape))


def _conf_kernel(tiles_per_seq, x_ref, mod_ref, n1_ref, n2_ref, w1_ref, b1_ref, dw_ref, dwb_ref,
                 lng_ref, lnb_ref, w2_ref, b2_ref, wrt_ref, brt_ref,
                 x1_ref, hrow_ref, cls_ref, cnt_ref, vbuf, vshift):
    i = pl.program_id(0)
    ts, d = x_ref.shape
    x = x_ref[...]
    h = _bf16(_rms_mod(x, n1_ref[...], mod_ref[0:1, :], mod_ref[1:2, :]))
    u = _dot(h, w1_ref[...]) + b1_ref[...]
    v = u[:, 0:d] * jax.nn.sigmoid(u[:, d:2 * d])

    @pl.when(i % tiles_per_seq == 0)
    def _():
        vbuf[0:CONV_HIST, :] = jnp.zeros((CONV_HIST, d), jnp.float32)

    vbuf[CONV_HIST:CONV_HIST + ts, :] = v
    y = _causal_taps(dw_ref, dwb_ref, vbuf, vshift, CONV_HIST, CONF_KERNEL, ts, d)
    vbuf[0:CONV_HIST, :] = vbuf[ts:ts + CONV_HIST, :]

    mu = jnp.mean(y, axis=-1, keepdims=True)
    yc = y - mu
    var = jnp.mean(yc * yc, axis=-1, keepdims=True)
    z = yc * lax.rsqrt(var + EPS) * lng_ref[...] + lnb_ref[...]
    z = z * jax.nn.sigmoid(z)
    m = _dot(_bf16(z), w2_ref[...]) + b2_ref[...]
    x1 = x + mod_ref[2:3, :] * m
    x1_ref[...] = x1
    _post_mixer(x1, mod_ref, n2_ref, wrt_ref, brt_ref, hrow_ref, cls_ref, cnt_ref, i)


def _conf_call(x, mod, n1, n2, w1, b1, dw, dwb, lng, lnb, w2, b2, wrt, brt, seq):
    t, d = x.shape
    ts = TOKEN_TILE
    tiles_per_seq = seq // ts
    out_shape, out_specs = _mixer_out(t, d, ts)
    return pl.pallas_call(
        functools.partial(_conf_kernel, tiles_per_seq),
        out_shape=out_shape,
        grid=(t // ts,),
        in_specs=[
            pl.BlockSpec((ts, d), lambda i: (i, 0)),
            pl.BlockSpec((None, MOD_ROWS, d), lambda i: (i // tiles_per_seq, 0, 0)),
            _full((1, d)), _full((1, d)),
            _full((d, 2 * d)), _full((1, 2 * d)),
            _full((CONV_HIST, d)), _full((1, d)),
            _full((1, d)), _full((1, d)),
            _full((d, d)), _full((1, d)),
            _full((ROUTER_ROWS, d)), _full((ROUTER_ROWS, LANES)),
        ],
        out_specs=out_specs,
        scratch_shapes=[
            pltpu.VMEM((CONV_HIST + ts, d), jnp.float32),
            pltpu.VMEM((SUBLANES - 1, CONV_HIST + ts + SUBLANES, d), jnp.float32),
        ],
        compiler_params=pltpu.CompilerParams(
            dimension_semantics=("arbitrary",), vmem_limit_bytes=VMEM_LIMIT_BYTES),
        name="conf_mixer",
    )(x, mod, n1, n2, w1, b1, dw, dwb, lng, lnb, w2, b2, wrt, brt)


def _shift_rows(x, s, fill):
    rolled = pltpu.roll(x, s, 0)
    row = lax.broadcasted_iota(jnp.int32, x.shape, 0) & (SUBLANES - 1)
    return jnp.where(row < s, fill, rolled)


def _lru_kernel(tiles_per_seq, x_ref, mod_ref, n1_ref, n2_ref, wy_ref, by_ref, wx_ref, bx_ref,
                cw_ref, cb_ref, wa_ref, ba_ref, wi_ref, bi_ref, lam_ref, wo_ref, bo_ref,
                wrt_ref, brt_ref,
                x1_ref, hrow_ref, cls_ref, cnt_ref, ubuf, ushift, hcar):
    i = pl.program_id(0)
    ts, d = x_ref.shape
    hd = d // LRU_HEADS
    x = x_ref[...]
    h = _bf16(_rms_mod(x, n1_ref[...], mod_ref[0:1, :], mod_ref[1:2, :]))
    ygate = jax.nn.gelu(_dot(h, wy_ref[...]) + by_ref[...])
    xb = _dot(h, wx_ref[...]) + bx_ref[...]

    @pl.when(i % tiles_per_seq == 0)
    def _():
        ubuf[0:LRU_HIST, :] = jnp.zeros((LRU_HIST, d), jnp.float32)
        hcar[...] = jnp.zeros_like(hcar)

    ubuf[LRU_HIST:LRU_HIST + ts, :] = xb
    u = _causal_taps(cw_ref, cb_ref, ubuf, ushift, LRU_HIST, LRU_CONV, ts, d)
    ubuf[0:LRU_HIST, :] = ubuf[ts:ts + LRU_HIST, :]

    ub = _bf16(u)
    r_parts, i_parts = [], []
    for hh in range(LRU_HEADS):
        uh = ub[:, hh * hd:(hh + 1) * hd]
        r_parts.append(_dot(uh, wa_ref[hh]))
        i_parts.append(_dot(uh, wi_ref[hh]))
    r = jax.nn.sigmoid(jnp.concatenate(r_parts, axis=1) + ba_ref[...])
    ig = jax.nn.sigmoid(jnp.concatenate(i_parts, axis=1) + bi_ref[...])

    nl = -lam_ref[...]
    softplus = jnp.maximum(nl, 0.0) + jnp.log1p(jnp.exp(-jnp.abs(nl)))
    log_a = (-LRU_C) * r * softplus
    a = jnp.exp(log_a)
    mult = jnp.sqrt(-jnp.tanh(log_a) * (1.0 + a * a))
    b = mult * (ig * u)

    s = 1
    while s < SUBLANES:
        a_prev = _shift_rows(a, s, 1.0)
        b_prev = _shift_rows(b, s, 0.0)
        b = a * b_prev + b
        a = a * a_prev
        s *= 2
    carry = hcar[...]
    outs = []
    for gidx in range(ts // SUBLANES):
        sl = slice(gidx * SUBLANES, (gidx + 1) * SUBLANES)
        hg = a[sl] * carry + b[sl]
        outs.append(hg)
        carry = hg[SUBLANES - 1:SUBLANES, :]
    hcar[...] = carry
    hs = jnp.concatenate(outs, axis=0)

    m = _dot(_bf16(hs * ygate), wo_ref[...]) + bo_ref[...]
    x1 = x + mod_ref[2:3, :] * m
    x1_ref[...] = x1
    _post_mixer(x1, mod_ref, n2_ref, wrt_ref, brt_ref, hrow_ref, cls_ref, cnt_ref, i)


def _lru_call(x, mod, n1, n2, wy, by, wx, bx, cw, cb, wa, ba, wi, bi, lam, wo, bo, wrt, brt, seq):
    t, d = x.shape
    ts = TOKEN_TILE
    hd = d // LRU_HEADS
    tiles_per_seq = seq // ts
    out_shape, out_specs = _mixer_out(t, d, ts)
    return pl.pallas_call(
        functools.partial(_lru_kernel, tiles_per_seq),
        out_shape=out_shape,
        grid=(t // ts,),
        in_specs=[
            pl.BlockSpec((ts, d), lambda i: (i, 0)),
            pl.BlockSpec((None, MOD_ROWS, d), lambda i: (i // tiles_per_seq, 0, 0)),
            _full((1, d)), _full((1, d)),
            _full((d, d)), _full((1, d)),
            _full((d, d)), _full((1, d)),
            _full((SUBLANES, d)), _full((1, d)),
            _full((LRU_HEADS, hd, hd)), _full((1, d)),
            _full((LRU_HEADS, hd, hd)), _full((1, d)),
            _full((1, d)),
            _full((d, d)), _full((1, d)),
            _full((ROUTER_ROWS, d)), _full((ROUTER_ROWS, LANES)),
        ],
        out_specs=out_specs,
        scratch_shapes=[
            pltpu.VMEM((LRU_HIST + ts, d), jnp.float32),
            pltpu.VMEM((LRU_CONV - 1, LRU_HIST + ts + SUBLANES, d), jnp.float32),
            pltpu.VMEM((1, d), jnp.float32),
        ],
        compiler_params=pltpu.CompilerParams(
            dimension_semantics=("arbitrary",), vmem_limit_bytes=VMEM_LIMIT_BYTES),
        name="lru_mixer",
    )(x, mod, n1, n2, wy, by, wx, bx, cw, cb, wa, ba, wi, bi, lam, wo, bo, wrt, brt)


def _dispatch_kernel(n_tiles, n_blocks, cls_ref, cnt_ref, clo_ref, chi_ref, hrow_ref, xs_in_ref,
                     xs_ref, dest_ref, blo_ref, bhi_ref, nblk_ref, cursor, stage, sem_in, sem_out):
    del xs_in_ref
    i = pl.program_id(0)
    ts = cls_ref.shape[2]
    rows = ts * SUBLANES

    @pl.when(i == 0)
    def _():
        def per_class(c, nb_total):
            cursor[c] = nb_total * ROW_BLOCK
            nb = lax.shift_right_logical(cnt_ref[c] + (ROW_BLOCK - 1), ROW_BLOCK.bit_length() - 1)

            def fill(k, carry):
                blo_ref[nb_total + k] = clo_ref[c]
                bhi_ref[nb_total + k] = chi_ref[c]
                return carry

            lax.fori_loop(0, nb, fill, 0)
            return nb_total + nb

        nb_used = lax.fori_loop(0, N_CLASSES, per_class, 0)
        nblk_ref[0] = nb_used
        last = jnp.maximum(nb_used - 1, 0)

        def tail(k, carry):
            blo_ref[k] = blo_ref[last]
            bhi_ref[k] = bhi_ref[last]
            return carry

        lax.fori_loop(nb_used, n_blocks, tail, 0)

    def load(tile, slot):
        return pltpu.make_async_copy(hrow_ref.at[pl.ds(pl.multiple_of(tile * rows, rows), rows)],
                                     stage.at[slot], sem_in.at[slot])

    def tile_copy(slot, r, dst_row):
        return pltpu.make_async_copy(
            stage.at[slot, pl.ds(pl.multiple_of(r * SUBLANES, SUBLANES), SUBLANES)],
            xs_ref.at[pl.ds(pl.multiple_of(dst_row * SUBLANES, SUBLANES), SUBLANES)],
            sem_out.at[slot])

    def drain(slot):
        def body(r, carry):
            tile_copy(slot, r, 0).wait()
            return carry
        lax.fori_loop(0, ts, body, 0, unroll=DMA_UNROLL)

    slot = i % DISPATCH_SLOTS
    nxt = (i + 1) % DISPATCH_SLOTS

    @pl.when(i == 0)
    def _():
        load(0, 0).start()

    @pl.when(i >= 2)
    def _():
        drain(nxt)

    @pl.when(i + 1 < n_tiles)
    def _():
        load(i + 1, nxt).start()

    load(i, slot).wait()

    def issue(r, carry):
        c = cls_ref[0, 0, r]
        dst_row = cursor[c]
        cursor[c] = dst_row + 1
        dest_ref[0, 0, r] = dst_row
        tile_copy(slot, r, dst_row).start()
        return carry

    lax.fori_loop(0, ts, issue, 0, unroll=DMA_UNROLL)

    @pl.when(i == n_tiles - 1)
    def _():
        @pl.when(i >= 1)
        def _():
            drain((i + DISPATCH_SLOTS - 1) % DISPATCH_SLOTS)
        drain(slot)


def _dispatch_call(cls, counts, hrows, n_blocks):
    n_tiles, _, ts = cls.shape
    n_rows = n_blocks * ROW_BLOCK
    xs0 = jnp.zeros((n_rows * SUBLANES, LANES), jnp.uint32)
    smem_full = pl.BlockSpec(memory_space=pltpu.SMEM)
    any_spec = pl.BlockSpec(memory_space=pl.ANY)
    return pl.pallas_call(
        functools.partial(_dispatch_kernel, n_tiles, n_blocks),
        out_shape=(
            jax.ShapeDtypeStruct((n_rows * SUBLANES, LANES), jnp.uint32),
            jax.ShapeDtypeStruct((n_tiles, 1, ts), jnp.int32),
            jax.ShapeDtypeStruct((n_blocks,), jnp.int32),
            jax.ShapeDtypeStruct((n_blocks,), jnp.int32),
            jax.ShapeDtypeStruct((1,), jnp.int32),
        ),
        grid=(n_tiles,),
        in_specs=[
            pl.BlockSpec((1, 1, ts), lambda i: (i, 0, 0), memory_space=pltpu.SMEM),
            smem_full, smem_full, smem_full, any_spec, any_spec,
        ],
        out_specs=(
            any_spec,
            pl.BlockSpec((1, 1, ts), lambda i: (i, 0, 0), memory_space=pltpu.SMEM),
            smem_full, smem_full, smem_full,
        ),
        scratch_shapes=[
            pltpu.SMEM((LANES,), jnp.int32),
            pltpu.VMEM((DISPATCH_SLOTS, ts * SUBLANES, LANES), jnp.uint32),
            pltpu.SemaphoreType.DMA((DISPATCH_SLOTS,)),
            pltpu.SemaphoreType.DMA((DISPATCH_SLOTS,)),
        ],
        input_output_aliases={5: 0},
        compiler_params=pltpu.CompilerParams(
            dimension_semantics=("arbitrary",), vmem_limit_bytes=VMEM_LIMIT_BYTES),
        name="moe_dispatch",
    )(cls, counts, jnp.asarray(_CLASS_LO), jnp.asarray(_CLASS_HI), hrows, xs0)


def _expert_kernel(blo_ref, bhi_ref, nblk_ref, xs_ref, wg_lo, wu_lo, wd_lo, wg_hi, wu_hi, wd_hi, ys_ref):
    del blo_ref, bhi_ref
    b = pl.program_id(0)
    rb = ROW_BLOCK

    @pl.when(b < nblk_ref[0])
    def _():
        pieces = []
        for j in range(PACK_SUBLANES):
            w = _token_rows(xs_ref, j, rb)
            pieces.append(_bf16(_f32_from_bits(w & jnp.uint32(0xFFFF0000))))
            pieces.append(_bf16(_f32_from_bits(lax.shift_left(w, jnp.uint32(16)))))
        xb = jnp.concatenate(pieces, axis=1)
        gates = _f32_from_bits(_token_rows(xs_ref, PACK_SUBLANES, rb))
        acc = None
        for idx, (wg, wu, wd) in enumerate(((wg_lo, wu_lo, wd_lo), (wg_hi, wu_hi, wd_hi))):
            a = _dot(xb, wg[...])
            hid = (a * jax.nn.sigmoid(a)) * _dot(xb, wu[...]) * gates[:, idx:idx + 1]
            part = _dot(_bf16(hid), wd[...])
            acc = part if acc is None else acc + part
        for s in range(SUBLANES):
            ys_ref[pl.ds(s, rb, stride=SUBLANES), :] = acc[:, s * LANES:(s + 1) * LANES]

    @pl.when(b >= nblk_ref[0])
    def _():
        ys_ref[...] = jnp.zeros_like(ys_ref)


def _expert_call(blo, bhi, nblk, xs, wg, wu, wd):
    n_rows = xs.shape[0] // SUBLANES
    d, de = wg.shape[1], wg.shape[2]
    assert d == SUBLANES * LANES
    n_blocks = n_rows // ROW_BLOCK

    def w_spec(shape, which):
        if which == 0:
            return pl.BlockSpec((None,) + shape, lambda b, lo, hi, n: (lo[b], 0, 0))
        return pl.BlockSpec((None,) + shape, lambda b, lo, hi, n: (hi[b], 0, 0))

    def x_map(b, lo, hi, n):
        return (jnp.minimum(b, jnp.maximum(n[0] - 1, 0)), 0)

    grid_spec = pltpu.PrefetchScalarGridSpec(
        num_scalar_prefetch=3,
        grid=(n_blocks,),
        in_specs=[
            pl.BlockSpec((ROW_BLOCK * SUBLANES, LANES), x_map),
            w_spec((d, de), 0), w_spec((d, de), 0), w_spec((de, d), 0),
            w_spec((d, de), 1), w_spec((d, de), 1), w_spec((de, d), 1),
        ],
        out_specs=pl.BlockSpec((ROW_BLOCK * SUBLANES, LANES), lambda b, lo, hi, n: (b, 0)),
    )
    return pl.pallas_call(
        _expert_kernel,
        out_shape=jax.ShapeDtypeStruct((n_rows * SUBLANES, LANES), jnp.float32),
        grid_spec=grid_spec,
        compiler_params=pltpu.CompilerParams(
            dimension_semantics=("arbitrary",), vmem_limit_bytes=VMEM_LIMIT_BYTES),
        name="moe_experts",
    )(blo, bhi, nblk, xs, wg, wu, wd, wg, wu, wd)


def _combine_kernel(n_tiles, final, dest_ref, dnext_ref, x1_ref, mod_ref, *rest):
    if final:
        fmod_ref, fg_ref, ys_ref, out_ref, ybuf, sem = rest
    else:
        ys_ref, out_ref, ybuf, sem = rest
    i = pl.program_id(0)
    ts = x1_ref.shape[0]
    slot = i % 2

    def tile_copy(dref, r, sl):
        src = pl.multiple_of(dref[0, 0, r] * SUBLANES, SUBLANES)
        return pltpu.make_async_copy(
            ys_ref.at[pl.ds(src, SUBLANES)],
            ybuf.at[sl, pl.ds(pl.multiple_of(r * SUBLANES, SUBLANES), SUBLANES)],
            sem.at[sl])

    def issue(dref, sl):
        def body(r, carry):
            tile_copy(dref, r, sl).start()
            return carry
        lax.fori_loop(0, ts, body, 0, unroll=DMA_UNROLL)

    @pl.when(i == 0)
    def _():
        issue(dest_ref, 0)

    @pl.when(i + 1 < n_tiles)
    def _():
        issue(dnext_ref, 1 - slot)

    def drain(r, carry):
        tile_copy(dest_ref, r, slot).wait()
        return carry

    lax.fori_loop(0, ts, drain, 0, unroll=DMA_UNROLL)
    yv = ybuf.at[slot]
    y = jnp.concatenate([_token_rows(yv, s, ts) for s in range(SUBLANES)], axis=1)
    x2 = x1_ref[...] + mod_ref[5:6, :] * y
    if final:
        out_ref[...] = _rms_mod(x2, fg_ref[...], fmod_ref[0:1, :], fmod_ref[1:2, :])
    else:
        out_ref[...] = x2


def _combine_call(dest, x1, mod, ys, seq, fmod=None, fin_g=None):
    t, d = x1.shape
    n_tiles, _, ts = dest.shape
    tiles_per_seq = seq // ts
    final = fmod is not None
    mod_spec = pl.BlockSpec((None, MOD_ROWS, d), lambda i: (i // tiles_per_seq, 0, 0))
    in_specs = [
        pl.BlockSpec((1, 1, ts), lambda i: (i, 0, 0), memory_space=pltpu.SMEM),
        pl.BlockSpec((1, 1, ts), lambda i: (jnp.minimum(i + 1, n_tiles - 1), 0, 0),
                     memory_space=pltpu.SMEM),
        pl.BlockSpec((ts, d), lambda i: (i, 0)),
        mod_spec,
    ]
    args = [dest, dest, x1, mod]
    if final:
        in_specs += [mod_spec, _full((1, d))]
        args += [fmod, fin_g]
    in_specs.append(pl.BlockSpec(memory_space=pl.ANY))
    args.append(ys)
    return pl.pallas_call(
        functools.partial(_combine_kernel, n_tiles, final),
        out_shape=jax.ShapeDtypeStruct((t, d), jnp.float32),
        grid=(n_tiles,),
        in_specs=in_specs,
        out_specs=pl.BlockSpec((ts, d), lambda i: (i, 0)),
        scratch_shapes=[pltpu.VMEM((2, ts * SUBLANES, LANES), jnp.float32),
                        pltpu.SemaphoreType.DMA((2,))],
        compiler_params=pltpu.CompilerParams(
            dimension_semantics=("arbitrary",), vmem_limit_bytes=VMEM_LIMIT_BYTES),
        name="moe_combine_final" if final else "moe_combine",
    )(*args)


def _router_operands(wg, bg, wr, br):
    d = wg.shape[0]
    pad_g = jnp.zeros((SUBLANES - N_GROUPS, d), jnp.float32)
    pad_e = jnp.zeros((ROUTER_ROWS - SUBLANES - N_EXPERTS, d), jnp.float32)
    wrt = jnp.concatenate([wg.T, pad_g, wr.T, pad_e], axis=0)
    bcol = jnp.concatenate([bg, jnp.zeros((SUBLANES - N_GROUPS,), jnp.float32), br,
                            jnp.zeros((ROUTER_ROWS - SUBLANES - N_EXPERTS,), jnp.float32)])
    brt = jnp.broadcast_to(bcol[:, None], (ROUTER_ROWS, LANES))
    return _bf16(wrt), brt


def _mod_table(m, parts):
    bsz = m.shape[0]
    d = m.shape[1] // parts
    m = m.reshape(bsz, parts, d)
    return jnp.concatenate([m, jnp.zeros((bsz, MOD_ROWS - parts, d), jnp.float32)], axis=1)


def _row(v):
    return v.reshape(1, -1)


def _moe(x1, hrows, cls, cnt, mod, wg, wu, wd, seq, fmod=None, fin_g=None):
    t = x1.shape[0]
    n_blocks = t // ROW_BLOCK + N_CLASSES
    counts = cnt[:, 0].astype(jnp.int32)
    xs, dest, blo, bhi, nblk = _dispatch_call(cls, counts, hrows, n_blocks)
    ys = _expert_call(blo, bhi, nblk, xs, wg, wu, wd)
    return _combine_call(dest, x1, mod, ys, seq, fmod, fin_g)


def kernel(x, c, ada_w, ada_b, norm1_g, norm2_g, cm_w1, cm_b1, cm_dw, cm_dwb, cm_ln_g, cm_ln_b, cm_w2, cm_b2, rg_wy, rg_by, rg_wx, rg_bx, rg_cw, rg_cb, rg_wa, rg_ba, rg_wi, rg_bi, rg_lambda, rg_wo, rg_bo, moe_wg, moe_bg, moe_wr, moe_br, moe_w_gate, moe_w_up, moe_w_down, fin_ada_w, fin_ada_b, fin_g):
    bsz, seq, d = x.shape
    t = bsz * seq
    depth = ada_w.shape[0]
    assert depth == 2 and seq % TOKEN_TILE == 0 and t % ROW_BLOCK == 0
    assert d == 2 * PACK_SUBLANES * LANES

    mods = _ada_call(c, ada_w, ada_b[:, None, :])
    fmods = _ada_call(c, fin_ada_w[None], fin_ada_b[None, None, :])
    fmod = _mod_table(fmods[0], 2)

    xt = x.reshape(t, d)
    for layer in range(depth):
        mod = _mod_table(mods[layer], 6)
        wrt, brt = _router_operands(moe_wg[layer], moe_bg[layer], moe_wr[layer], moe_br[layer])
        n1, n2 = _row(norm1_g[layer]), _row(norm2_g[layer])
        if layer == 0:
            dw = jnp.concatenate(
                [cm_dw[0], jnp.zeros((CONV_HIST - CONF_KERNEL, d), jnp.float32)], axis=0)
            x1, hrows, cls, cnt = _conf_call(
                xt, mod, n1, n2, _bf16(cm_w1[0]), _row(cm_b1[0]), dw, _row(cm_dwb[0]),
                _row(cm_ln_g[0]), _row(cm_ln_b[0]), _bf16(cm_w2[0]), _row(cm_b2[0]), wrt, brt, seq)
        else:
            cw = jnp.concatenate(
                [rg_cw[0], jnp.zeros((SUBLANES - LRU_CONV, d), jnp.float32)], axis=0)
            x1, hrows, cls, cnt = _lru_call(
                xt, mod, n1, n2, _bf16(rg_wy[0]), _row(rg_by[0]), _bf16(rg_wx[0]), _row(rg_bx[0]),
                cw, _row(rg_cb[0]), _bf16(rg_wa[0]), _row(rg_ba[0]), _bf16(rg_wi[0]), _row(rg_bi[0]),
                _row(rg_lambda[0]), _bf16(rg_wo[0]), _row(rg_bo[0]), wrt, brt, seq)
        last = layer == depth - 1
        xt = _moe(x1, hrows, cls, cnt, mod,
                  _bf16(moe_w_gate[layer]), _bf16(moe_w_up[layer]), _bf16(moe_w_down[layer]), seq,
                  fmod if last else None, _row(fin_g) if last else None)
    return xt.reshape(bsz, seq, d)
```

```python
import functools

import jax
import jax.numpy as jnp
import numpy as np
from jax import lax
from jax.experimental import pallas as pl
from jax.experimental.pallas import tpu as pltpu

EPS = 1e-6
CONF_KERNEL = 31
LRU_HEADS = 4
LRU_CONV = 4
LRU_C = 8.0
N_GROUPS = 4
EXPERTS_PER_GROUP = 8
N_EXPERTS = N_GROUPS * EXPERTS_PER_GROUP
PAIRS_PER_GROUP = EXPERTS_PER_GROUP * (EXPERTS_PER_GROUP - 1) // 2
N_CLASSES = N_GROUPS * PAIRS_PER_GROUP

LANES = 128
SUBLANES = 8
VMEM_LIMIT_BYTES = 56 * 1024 * 1024

TOKEN_TILE = 256
ROW_BLOCK = 256
CONV_HIST = 32
LRU_HIST = SUBLANES
ROUTER_ROWS = 64
PACK_SUBLANES = 4
MOD_ROWS = 8
DISPATCH_SLOTS = 3
DMA_UNROLL = 8

_NEG_INF = float("-inf")


def _class_tables():
    lo = np.zeros((N_CLASSES,), np.int32)
    hi = np.zeros((N_CLASSES,), np.int32)
    for g in range(N_GROUPS):
        for a in range(EXPERTS_PER_GROUP):
            for b in range(a + 1, EXPERTS_PER_GROUP):
                c = g * PAIRS_PER_GROUP + (a * (2 * EXPERTS_PER_GROUP - 1 - a)) // 2 + (b - a - 1)
                lo[c] = g * EXPERTS_PER_GROUP + a
                hi[c] = g * EXPERTS_PER_GROUP + b
    return lo, hi


_CLASS_LO, _CLASS_HI = _class_tables()


def _bf16(x):
    return x.astype(jnp.bfloat16)


def _dot(a, b):
    return jnp.dot(a, b, preferred_element_type=jnp.float32)


def _bits(x):
    return lax.bitcast_convert_type(x, jnp.uint32)


def _f32_from_bits(x):
    return lax.bitcast_convert_type(x, jnp.float32)


def _token_rows(ref, sublane, n):
    return ref[pl.ds(sublane, n, stride=SUBLANES), :]


def _ada_kernel(c_ref, w_ref, b_ref, o_ref):
    c = c_ref[...]
    ca = _bf16(c * jax.nn.sigmoid(c))
    o_ref[...] = _dot(ca, _bf16(w_ref[...])) + b_ref[...]


def _ada_call(c, w, b):
    n_layers, d, n = w.shape
    bsz = c.shape[0]
    nb = 1024
    return pl.pallas_call(
        _ada_kernel,
        out_shape=jax.ShapeDtypeStruct((n_layers, bsz, n), jnp.float32),
        grid=(n_layers, n // nb),
        in_specs=[
            pl.BlockSpec((bsz, d), lambda l, j: (0, 0)),
            pl.BlockSpec((None, d, nb), lambda l, j: (l, 0, j)),
            pl.BlockSpec((None, 1, nb), lambda l, j: (l, 0, j)),
        ],
        out_specs=pl.BlockSpec((None, bsz, nb), lambda l, j: (l, 0, j)),
        compiler_params=pltpu.CompilerParams(
            dimension_semantics=("arbitrary", "arbitrary"),
            vmem_limit_bytes=VMEM_LIMIT_BYTES),
        name="ada_mod",
    )(c, w, b)


def _rms_mod(x, gain, shift, scale):
    ms = jnp.mean(x * x, axis=-1, keepdims=True)
    return (x * lax.rsqrt(ms + EPS)) * (gain * (1.0 + scale)) + shift


def _causal_taps(w_ref, bias_ref, base_ref, shifted_ref, hist, n_taps, ts, d):
    full = base_ref[...]
    for r in range(1, shifted_ref.shape[0] + 1):
        shifted_ref[r - 1, r:r + hist + ts, :] = full
    cols = []
    for cb in range(d // LANES):
        sl = slice(cb * LANES, (cb + 1) * LANES)
        acc = jnp.broadcast_to(bias_ref[:, sl], (ts, LANES))
        for k in range(n_taps):
            q, r = divmod(n_taps - 1 - k, SUBLANES)
            lo = hist - SUBLANES * q
            if r == 0:
                win = base_ref[lo:lo + ts, sl]
            else:
                win = shifted_ref[r - 1, lo:lo + ts, sl]
            acc = acc + w_ref[k:k + 1, sl] * win
        cols.append(acc)
    return jnp.concatenate(cols, axis=1)


def _route(logits_t):
    ts = logits_t.shape[1]
    row = lax.broadcasted_iota(jnp.int32, (SUBLANES, ts), 0)
    gl = jnp.where(row < N_GROUPS, logits_t[0:SUBLANES], _NEG_INF)
    gmax = jnp.max(gl, axis=0, keepdims=True)
    gsel = jnp.min(jnp.where(gl == gmax, row, SUBLANES), axis=0, keepdims=True)
    p_g = 1.0 / jnp.sum(jnp.exp(gl - gmax), axis=0, keepdims=True)
    es = jnp.zeros((SUBLANES, ts), jnp.float32)
    for g in range(N_GROUPS):
        lo = SUBLANES * (1 + g)
        es = jnp.where(gsel == g, logits_t[lo:lo + EXPERTS_PER_GROUP], es)
    m1 = jnp.max(es, axis=0, keepdims=True)
    i1 = jnp.min(jnp.where(es == m1, row, SUBLANES), axis=0, keepdims=True)
    es2 = jnp.where(row == i1, _NEG_INF, es)
    m2 = jnp.max(es2, axis=0, keepdims=True)
    i2 = jnp.min(jnp.where(es2 == m2, row, SUBLANES), axis=0, keepdims=True)
    t = jnp.exp(m2 - m1)
    w1 = p_g / (1.0 + t)
    w2 = w1 * t
    first_lo = i1 < i2
    e_lo = jnp.minimum(i1, i2)
    e_hi = jnp.maximum(i1, i2)
    gate_lo = jnp.where(first_lo, w1, w2)
    gate_hi = jnp.where(first_lo, w2, w1)
    pair = lax.shift_right_logical(e_lo * (2 * EXPERTS_PER_GROUP - 1 - e_lo), 1) + (e_hi - e_lo - 1)
    cls = gsel * PAIRS_PER_GROUP + pair
    return cls, gate_lo, gate_hi


def _gather_expert_tiles(ys_ref, ybuf, sem, dest_ref, dnext_ref, step, n_steps, ts):
    slot = step % 2

    def tile_copy(dref, r, sl):
        src = pl.multiple_of(dref[0, 0, r] * SUBLANES, SUBLANES)
        return pltpu.make_async_copy(
            ys_ref.at[pl.ds(src, SUBLANES)],
            ybuf.at[sl, pl.ds(pl.multiple_of(r * SUBLANES, SUBLANES), SUBLANES)],
            sem.at[sl])

    def issue(dref, sl):
        def body(r, carry):
            tile_copy(dref, r, sl).start()
            return carry
        lax.fori_loop(0, ts, body, 0, unroll=DMA_UNROLL)

    @pl.when(step == 0)
    def _():
        issue(dest_ref, 0)

    @pl.when(step + 1 < n_steps)
    def _():
        issue(dnext_ref, 1 - slot)

    def drain(r, carry):
        tile_copy(dest_ref, r, slot).wait()
        return carry

    lax.fori_loop(0, ts, drain, 0, unroll=DMA_UNROLL)
    yv = ybuf.at[slot]
    return jnp.concatenate([_token_rows(yv, s, ts) for s in range(SUBLANES)], axis=1)


def _post_mixer(x1, mod_ref, n2_ref, wrt_ref, brt_ref, hrow_ref, cls_ref, rank_ref, cnt_ref, step):
    ts = x1.shape[0]
    h2 = _rms_mod(x1, n2_ref[...], mod_ref[3:4, :], mod_ref[4:5, :])
    h2b = _bf16(h2)
    logits_t = lax.dot_general(wrt_ref[...], h2b, (((1,), (1,)), ((), ())),
                               preferred_element_type=jnp.float32) + brt_ref[:, 0:1]
    cls, gate_lo, gate_hi = _route(logits_t)
    gates_t = jnp.concatenate(
        [gate_lo, gate_hi, jnp.zeros((LANES - 2, ts), jnp.float32)], axis=0)
    h2r = h2b.astype(jnp.float32)
    for j in range(PACK_SUBLANES):
        hi = _bits(h2r[:, (2 * j) * LANES:(2 * j + 1) * LANES])
        lo = _bits(h2r[:, (2 * j + 1) * LANES:(2 * j + 2) * LANES])
        hrow_ref[pl.ds(j, ts, stride=SUBLANES), :] = hi | lax.shift_right_logical(lo, jnp.uint32(16))
    hrow_ref[pl.ds(PACK_SUBLANES, ts, stride=SUBLANES), :] = _bits(gates_t.T)
    for j in range(PACK_SUBLANES + 1, SUBLANES):
        hrow_ref[pl.ds(j, ts, stride=SUBLANES), :] = jnp.zeros((ts, LANES), jnp.uint32)
    cls_ref[0] = cls
    crow = lax.broadcasted_iota(jnp.int32, (LANES, ts), 0)
    onehot = jnp.where(crow == cls, 1.0, 0.0)
    cnt = jnp.sum(onehot, axis=1, keepdims=True)

    @pl.when(step == 0)
    def _():
        cnt_ref[...] = jnp.zeros_like(cnt_ref)

    earlier = (lax.broadcasted_iota(jnp.int32, (ts, ts), 0)
               < lax.broadcasted_iota(jnp.int32, (ts, ts), 1))
    prefix = _dot(_bf16(onehot), _bf16(jnp.where(earlier, 1.0, 0.0)))
    rank = jnp.sum(onehot * (prefix + cnt_ref[:, 0:1]), axis=0, keepdims=True)
    rank_ref[0] = rank.astype(jnp.int32)
    cnt_ref[...] += jnp.broadcast_to(cnt, cnt_ref.shape)


def _mixer_out(t, d, ts):
    n_tiles = t // ts
    out_shape = (
        jax.ShapeDtypeStruct((t, d), jnp.float32),
        jax.ShapeDtypeStruct((t * SUBLANES, LANES), jnp.uint32),
        jax.ShapeDtypeStruct((n_tiles, 1, ts), jnp.int32),
        jax.ShapeDtypeStruct((n_tiles, 1, ts), jnp.int32),
        jax.ShapeDtypeStruct((LANES, LANES), jnp.float32),
    )
    out_specs = (
        pl.BlockSpec((ts, d), lambda i: (i, 0)),
        pl.BlockSpec((ts * SUBLANES, LANES), lambda i: (i, 0)),
        pl.BlockSpec((1, 1, ts), lambda i: (i, 0, 0)),
        pl.BlockSpec((1, 1, ts), lambda i: (i, 0, 0)),
        pl.BlockSpec((LANES, LANES), lambda i: (0, 0)),
    )
    return out_shape, out_specs


def _full(shape):
    return pl.BlockSpec(shape, lambda i: tuple(0 for _ in shape))


def _conf_kernel(tiles_per_seq, x_ref, mod_ref, n1_ref, n2_ref, w1_ref, b1_ref, dw_ref, dwb_ref,
                 lng_ref, lnb_ref, w2_ref, b2_ref, wrt_ref, brt_ref,
                 x1_ref, hrow_ref, cls_ref, rank_ref, cnt_ref, vbuf, vshift):
    i = pl.program_id(0)
    ts, d = x_ref.shape
    x = x_ref[...]
    h = _bf16(_rms_mod(x, n1_ref[...], mod_ref[0:1, :], mod_ref[1:2, :]))
    u = _dot(h, w1_ref[...]) + b1_ref[...]
    v = u[:, 0:d] * jax.nn.sigmoid(u[:, d:2 * d])

    @pl.when(i % tiles_per_seq == 0)
    def _():
        vbuf[0:CONV_HIST, :] = jnp.zeros((CONV_HIST, d), jnp.float32)

    vbuf[CONV_HIST:CONV_HIST + ts, :] = v
    y = _causal_taps(dw_ref, dwb_ref, vbuf, vshift, CONV_HIST, CONF_KERNEL, ts, d)
    vbuf[0:CONV_HIST, :] = vbuf[ts:ts + CONV_HIST, :]

    mu = jnp.mean(y, axis=-1, keepdims=True)
    yc = y - mu
    var = jnp.mean(yc * yc, axis=-1, keepdims=True)
    z = yc * lax.rsqrt(var + EPS) * lng_ref[...] + lnb_ref[...]
    z = z * jax.nn.sigmoid(z)
    m = _dot(_bf16(z), w2_ref[...]) + b2_ref[...]
    x1 = x + mod_ref[2:3, :] * m
    x1_ref[...] = x1
    _post_mixer(x1, mod_ref, n2_ref, wrt_ref, brt_ref, hrow_ref, cls_ref, rank_ref, cnt_ref, i)


def _conf_call(x, mod, n1, n2, w1, b1, dw, dwb, lng, lnb, w2, b2, wrt, brt, seq):
    t, d = x.shape
    ts = TOKEN_TILE
    tiles_per_seq = seq // ts
    out_shape, out_specs = _mixer_out(t, d, ts)
    return pl.pallas_call(
        functools.partial(_conf_kernel, tiles_per_seq),
        out_shape=out_shape,
        grid=(t // ts,),
        in_specs=[
            pl.BlockSpec((ts, d), lambda i: (i, 0)),
            pl.BlockSpec((None, MOD_ROWS, d), lambda i: (i // tiles_per_seq, 0, 0)),
            _full((1, d)), _full((1, d)),
            _full((d, 2 * d)), _full((1, 2 * d)),
            _full((CONV_HIST, d)), _full((1, d)),
            _full((1, d)), _full((1, d)),
            _full((d, d)), _full((1, d)),
            _full((ROUTER_ROWS, d)), _full((ROUTER_ROWS, LANES)),
        ],
        out_specs=out_specs,
        scratch_shapes=[
            pltpu.VMEM((CONV_HIST + ts, d), jnp.float32),
            pltpu.VMEM((SUBLANES - 1, CONV_HIST + ts + SUBLANES, d), jnp.float32),
        ],
        compiler_params=pltpu.CompilerParams(
            dimension_semantics=("arbitrary",), vmem_limit_bytes=VMEM_LIMIT_BYTES),
        name="conf_mixer",
    )(x, mod, n1, n2, w1, b1, dw, dwb, lng, lnb, w2, b2, wrt, brt)


def _shift_rows(x, s, fill):
    rolled = pltpu.roll(x, s, 0)
    row = lax.broadcasted_iota(jnp.int32, x.shape, 0) & (SUBLANES - 1)
    return jnp.where(row < s, fill, rolled)


def _lru_kernel(tiles_per_seq, n_tiles, dest_ref, dnext_ref, xprev_ref, pmod_ref, mod_ref, n1_ref, n2_ref,
                wy_ref, by_ref, wx_ref, bx_ref,
                cw_ref, cb_ref, wa_ref, ba_ref, wi_ref, bi_ref, lam_ref, wo_ref, bo_ref,
                wrt_ref, brt_ref, ys_ref,
                x1_ref, hrow_ref, cls_ref, rank_ref, cnt_ref, ubuf, ushift, hcar, ybuf, ysem):
    i = pl.program_id(0)
    ts, d = xprev_ref.shape
    hd = d // LRU_HEADS
    y_prev = _gather_expert_tiles(ys_ref, ybuf, ysem, dest_ref, dnext_ref, i, n_tiles, ts)
    x = xprev_ref[...] + pmod_ref[5:6, :] * y_prev
    h = _bf16(_rms_mod(x, n1_ref[...], mod_ref[0:1, :], mod_ref[1:2, :]))
    ygate = jax.nn.gelu(_dot(h, wy_ref[...]) + by_ref[...])
    xb = _dot(h, wx_ref[...]) + bx_ref[...]

    @pl.when(i % tiles_per_seq == 0)
    def _():
        ubuf[0:LRU_HIST, :] = jnp.zeros((LRU_HIST, d), jnp.float32)
        hcar[...] = jnp.zeros_like(hcar)

    ubuf[LRU_HIST:LRU_HIST + ts, :] = xb
    u = _causal_taps(cw_ref, cb_ref, ubuf, ushift, LRU_HIST, LRU_CONV, ts, d)
    ubuf[0:LRU_HIST, :] = ubuf[ts:ts + LRU_HIST, :]

    ub = _bf16(u)
    r_parts, i_parts = [], []
    for hh in range(LRU_HEADS):
        uh = ub[:, hh * hd:(hh + 1) * hd]
        r_parts.append(_dot(uh, wa_ref[hh]))
        i_parts.append(_dot(uh, wi_ref[hh]))
    r = jax.nn.sigmoid(jnp.concatenate(r_parts, axis=1) + ba_ref[...])
    ig = jax.nn.sigmoid(jnp.concatenate(i_parts, axis=1) + bi_ref[...])

    nl = -lam_ref[...]
    softplus = jnp.maximum(nl, 0.0) + jnp.log1p(jnp.exp(-jnp.abs(nl)))
    log_a = (-LRU_C) * r * softplus
    a = jnp.exp(log_a)
    mult = jnp.sqrt(-jnp.tanh(log_a) * (1.0 + a * a))
    b = mult * (ig * u)

    s = 1
    while s < SUBLANES:
        a_prev = _shift_rows(a, s, 1.0)
        b_prev = _shift_rows(b, s, 0.0)
        b = a * b_prev + b
        a = a * a_prev
        s *= 2
    carry = hcar[...]
    outs = []
    for gidx in range(ts // SUBLANES):
        sl = slice(gidx * SUBLANES, (gidx + 1) * SUBLANES)
        hg = a[sl] * carry + b[sl]
        outs.append(hg)
        carry = hg[SUBLANES - 1:SUBLANES, :]
    hcar[...] = carry
    hs = jnp.concatenate(outs, axis=0)

    m = _dot(_bf16(hs * ygate), wo_ref[...]) + bo_ref[...]
    x1 = x + mod_ref[2:3, :] * m
    x1_ref[...] = x1
    _post_mixer(x1, mod_ref, n2_ref, wrt_ref, brt_ref, hrow_ref, cls_ref, rank_ref, cnt_ref, i)


def _lru_call(dest, x_prev, mod_prev, ys, mod, n1, n2, wy, by, wx, bx, cw, cb, wa, ba, wi, bi, lam,
              wo, bo, wrt, brt, seq):
    t, d = x_prev.shape
    ts = TOKEN_TILE
    n_tiles = t // ts
    hd = d // LRU_HEADS
    tiles_per_seq = seq // ts
    out_shape, out_specs = _mixer_out(t, d, ts)
    mod_spec = pl.BlockSpec((None, MOD_ROWS, d), lambda i: (i // tiles_per_seq, 0, 0))
    return pl.pallas_call(
        functools.partial(_lru_kernel, tiles_per_seq, n_tiles),
        out_shape=out_shape,
        grid=(n_tiles,),
        in_specs=[
            pl.BlockSpec((1, 1, ts), lambda i: (i, 0, 0), memory_space=pltpu.SMEM),
            pl.BlockSpec((1, 1, ts), lambda i: (jnp.minimum(i + 1, n_tiles - 1), 0, 0),
                         memory_space=pltpu.SMEM),
            pl.BlockSpec((ts, d), lambda i: (i, 0)),
            mod_spec, mod_spec,
            _full((1, d)), _full((1, d)),
            _full((d, d)), _full((1, d)),
            _full((d, d)), _full((1, d)),
            _full((SUBLANES, d)), _full((1, d)),
            _full((LRU_HEADS, hd, hd)), _full((1, d)),
            _full((LRU_HEADS, hd, hd)), _full((1, d)),
            _full((1, d)),
            _full((d, d)), _full((1, d)),
            _full((ROUTER_ROWS, d)), _full((ROUTER_ROWS, LANES)),
            pl.BlockSpec(memory_space=pl.ANY),
        ],
        out_specs=out_specs,
        scratch_shapes=[
            pltpu.VMEM((LRU_HIST + ts, d), jnp.float32),
            pltpu.VMEM((LRU_CONV - 1, LRU_HIST + ts + SUBLANES, d), jnp.float32),
            pltpu.VMEM((1, d), jnp.float32),
            pltpu.VMEM((2, ts * SUBLANES, LANES), jnp.float32),
            pltpu.SemaphoreType.DMA((2,)),
        ],
        compiler_params=pltpu.CompilerParams(
            dimension_semantics=("arbitrary",), vmem_limit_bytes=VMEM_LIMIT_BYTES),
        name="lru_mixer",
    )(dest, dest, x_prev, mod_prev, mod, n1, n2, wy, by, wx, bx, cw, cb, wa, ba, wi, bi, lam, wo, bo,
      wrt, brt, ys)


def _dispatch_kernel(n_tiles, n_blocks, cls_ref, rank_ref, cnt_ref, clo_ref, chi_ref, hrow_ref, xs_in_ref,
                     xs_ref, dest_ref, blo_ref, bhi_ref, nblk_ref, class_start, stage, sem_in, sem_out):
    del xs_in_ref
    i = pl.program_id(0)
    ts = cls_ref.shape[2]
    rows = ts * SUBLANES

    @pl.when(i == 0)
    def _():
        def per_class(c, nb_total):
            class_start[c] = nb_total * ROW_BLOCK
            nb = lax.shift_right_logical(cnt_ref[c] + (ROW_BLOCK - 1), ROW_BLOCK.bit_length() - 1)

            def fill(k, carry):
                blo_ref[nb_total + k] = clo_ref[c]
                bhi_ref[nb_total + k] = chi_ref[c]
                return carry

            lax.fori_loop(0, nb, fill, 0)
            return nb_total + nb

        nb_used = lax.fori_loop(0, N_CLASSES, per_class, 0)
        nblk_ref[0] = nb_used
        last = jnp.maximum(nb_used - 1, 0)

        def tail(k, carry):
            blo_ref[k] = blo_ref[last]
            bhi_ref[k] = bhi_ref[last]
            return carry

        lax.fori_loop(nb_used, n_blocks, tail, 0)

    def load(tile, slot):
        return pltpu.make_async_copy(hrow_ref.at[pl.ds(pl.multiple_of(tile * rows, rows), rows)],
                                     stage.at[slot], sem_in.at[slot])

    def tile_copy(slot, r, dst_row):
        return pltpu.make_async_copy(
            stage.at[slot, pl.ds(pl.multiple_of(r * SUBLANES, SUBLANES), SUBLANES)],
            xs_ref.at[pl.ds(pl.multiple_of(dst_row * SUBLANES, SUBLANES), SUBLANES)],
            sem_out.at[slot])

    def drain(slot):
        def body(r, carry):
            tile_copy(slot, r, 0).wait()
            return carry
        lax.fori_loop(0, ts, body, 0, unroll=DMA_UNROLL)

    slot = i % DISPATCH_SLOTS
    nxt = (i + 1) % DISPATCH_SLOTS

    @pl.when(i == 0)
    def _():
        load(0, 0).start()

    @pl.when(i >= 2)
    def _():
        drain(nxt)

    @pl.when(i + 1 < n_tiles)
    def _():
        load(i + 1, nxt).start()

    load(i, slot).wait()

    def issue(r, carry):
        dst_row = class_start[cls_ref[0, 0, r]] + rank_ref[0, 0, r]
        dest_ref[0, 0, r] = dst_row
        tile_copy(slot, r, dst_row).start()
        return carry

    lax.fori_loop(0, ts, issue, 0, unroll=DMA_UNROLL)

    @pl.when(i == n_tiles - 1)
    def _():
        @pl.when(i >= 1)
        def _():
            drain((i + DISPATCH_SLOTS - 1) % DISPATCH_SLOTS)
        drain(slot)


def _dispatch_call(cls, rank, counts, hrows, n_blocks, xs_init):
    n_tiles, _, ts = cls.shape
    n_rows = n_blocks * ROW_BLOCK
    assert xs_init.shape == (n_rows * SUBLANES, LANES) and xs_init.dtype == jnp.uint32
    smem_full = pl.BlockSpec(memory_space=pltpu.SMEM)
    smem_tile = pl.BlockSpec((1, 1, ts), lambda i: (i, 0, 0), memory_space=pltpu.SMEM)
    any_spec = pl.BlockSpec(memory_space=pl.ANY)
    return pl.pallas_call(
        functools.partial(_dispatch_kernel, n_tiles, n_blocks),
        out_shape=(
            jax.ShapeDtypeStruct((n_rows * SUBLANES, LANES), jnp.uint32),
            jax.ShapeDtypeStruct((n_tiles, 1, ts), jnp.int32),
            jax.ShapeDtypeStruct((n_blocks,), jnp.int32),
            jax.ShapeDtypeStruct((n_blocks,), jnp.int32),
            jax.ShapeDtypeStruct((1,), jnp.int32),
        ),
        grid=(n_tiles,),
        in_specs=[
            smem_tile, smem_tile,
            smem_full, smem_full, smem_full, any_spec, any_spec,
        ],
        out_specs=(
            any_spec,
            smem_tile,
            smem_full, smem_full, smem_full,
        ),
        scratch_shapes=[
            pltpu.SMEM((LANES,), jnp.int32),
            pltpu.VMEM((DISPATCH_SLOTS, ts * SUBLANES, LANES), jnp.uint32),
            pltpu.SemaphoreType.DMA((DISPATCH_SLOTS,)),
            pltpu.SemaphoreType.DMA((DISPATCH_SLOTS,)),
        ],
        input_output_aliases={6: 0},
        compiler_params=pltpu.CompilerParams(
            dimension_semantics=("arbitrary",), vmem_limit_bytes=VMEM_LIMIT_BYTES),
        name="moe_dispatch",
    )(cls, rank, counts, jnp.asarray(_CLASS_LO), jnp.asarray(_CLASS_HI), hrows, xs_init)


def _expert_kernel(blo_ref, bhi_ref, nblk_ref, xs_ref, wg_lo, wu_lo, wd_lo, wg_hi, wu_hi, wd_hi, ys_ref):
    del blo_ref, bhi_ref
    b = pl.program_id(0)
    rb = ROW_BLOCK

    @pl.when(b < nblk_ref[0])
    def _():
        pieces = []
        for j in range(PACK_SUBLANES):
            w = _token_rows(xs_ref, j, rb)
            pieces.append(_bf16(_f32_from_bits(w & jnp.uint32(0xFFFF0000))))
            pieces.append(_bf16(_f32_from_bits(lax.shift_left(w, jnp.uint32(16)))))
        xb = jnp.concatenate(pieces, axis=1)
        gates = _f32_from_bits(_token_rows(xs_ref, PACK_SUBLANES, rb))
        acc = None
        for idx, (wg, wu, wd) in enumerate(((wg_lo, wu_lo, wd_lo), (wg_hi, wu_hi, wd_hi))):
            a = _dot(xb, wg[...])
            hid = (a * jax.nn.sigmoid(a)) * _dot(xb, wu[...]) * gates[:, idx:idx + 1]
            part = _dot(_bf16(hid), wd[...])
            acc = part if acc is None else acc + part
        for s in range(SUBLANES):
            ys_ref[pl.ds(s, rb, stride=SUBLANES), :] = acc[:, s * LANES:(s + 1) * LANES]

    @pl.when(b >= nblk_ref[0])
    def _():
        ys_ref[...] = jnp.zeros_like(ys_ref)


def _expert_call(blo, bhi, nblk, xs, wg, wu, wd):
    n_rows = xs.shape[0] // SUBLANES
    d, de = wg.shape[1], wg.shape[2]
    assert d == SUBLANES * LANES
    n_blocks = n_rows // ROW_BLOCK

    def w_spec(shape, which):
        if which == 0:
            return pl.BlockSpec((None,) + shape, lambda b, lo, hi, n: (lo[b], 0, 0))
        return pl.BlockSpec((None,) + shape, lambda b, lo, hi, n: (hi[b], 0, 0))

    def x_map(b, lo, hi, n):
        return (jnp.minimum(b, jnp.maximum(n[0] - 1, 0)), 0)

    grid_spec = pltpu.PrefetchScalarGridSpec(
        num_scalar_prefetch=3,
        grid=(n_blocks,),
        in_specs=[
            pl.BlockSpec((ROW_BLOCK * SUBLANES, LANES), x_map),
            w_spec((d, de), 0), w_spec((d, de), 0), w_spec((de, d), 0),
            w_spec((d, de), 1), w_spec((d, de), 1), w_spec((de, d), 1),
        ],
        out_specs=pl.BlockSpec((ROW_BLOCK * SUBLANES, LANES), lambda b, lo, hi, n: (b, 0)),
    )
    return pl.pallas_call(
        _expert_kernel,
        out_shape=jax.ShapeDtypeStruct((n_rows * SUBLANES, LANES), jnp.float32),
        grid_spec=grid_spec,
        compiler_params=pltpu.CompilerParams(
            dimension_semantics=("arbitrary",), vmem_limit_bytes=VMEM_LIMIT_BYTES),
        name="moe_experts",
    )(blo, bhi, nblk, xs, wg, wu, wd, wg, wu, wd)


def _final_kernel(n_tiles, dest_ref, dnext_ref, x1_ref, mod_ref, fmod_ref, fg_ref, ys_ref,
                  out_ref, ybuf, sem):
    i = pl.program_id(0)
    ts = x1_ref.shape[0]
    y = _gather_expert_tiles(ys_ref, ybuf, sem, dest_ref, dnext_ref, i, n_tiles, ts)
    x2 = x1_ref[...] + mod_ref[5:6, :] * y
    out_ref[...] = _rms_mod(x2, fg_ref[...], fmod_ref[0:1, :], fmod_ref[1:2, :])


def _final_call(dest, x1, mod, ys, seq, fmod, fin_g):
    t, d = x1.shape
    n_tiles, _, ts = dest.shape
    tiles_per_seq = seq // ts
    mod_spec = pl.BlockSpec((None, MOD_ROWS, d), lambda i: (i // tiles_per_seq, 0, 0))
    return pl.pallas_call(
        functools.partial(_final_kernel, n_tiles),
        out_shape=jax.ShapeDtypeStruct((t, d), jnp.float32),
        grid=(n_tiles,),
        in_specs=[
            pl.BlockSpec((1, 1, ts), lambda i: (i, 0, 0), memory_space=pltpu.SMEM),
            pl.BlockSpec((1, 1, ts), lambda i: (jnp.minimum(i + 1, n_tiles - 1), 0, 0),
                         memory_space=pltpu.SMEM),
            pl.BlockSpec((ts, d), lambda i: (i, 0)),
            mod_spec, mod_spec, _full((1, d)),
            pl.BlockSpec(memory_space=pl.ANY),
        ],
        out_specs=pl.BlockSpec((ts, d), lambda i: (i, 0)),
        scratch_shapes=[pltpu.VMEM((2, ts * SUBLANES, LANES), jnp.float32),
                        pltpu.SemaphoreType.DMA((2,))],
        compiler_params=pltpu.CompilerParams(
            dimension_semantics=("arbitrary",), vmem_limit_bytes=VMEM_LIMIT_BYTES),
        name="moe_combine_final",
    )(dest, dest, x1, mod, fmod, fin_g, ys)


def _router_operands(wg, bg, wr, br):
    d = wg.shape[0]
    pad_g = jnp.zeros((SUBLANES - N_GROUPS, d), jnp.float32)
    pad_e = jnp.zeros((ROUTER_ROWS - SUBLANES - N_EXPERTS, d), jnp.float32)
    wrt = jnp.concatenate([wg.T, pad_g, wr.T, pad_e], axis=0)
    bcol = jnp.concatenate([bg, jnp.zeros((SUBLANES - N_GROUPS,), jnp.float32), br,
                            jnp.zeros((ROUTER_ROWS - SUBLANES - N_EXPERTS,), jnp.float32)])
    brt = jnp.broadcast_to(bcol[:, None], (ROUTER_ROWS, LANES))
    return _bf16(wrt), brt


def _mod_table(m, parts):
    bsz = m.shape[0]
    d = m.shape[1] // parts
    m = m.reshape(bsz, parts, d)
    return jnp.concatenate([m, jnp.zeros((bsz, MOD_ROWS - parts, d), jnp.float32)], axis=1)


def _row(v):
    return v.reshape(1, -1)


def _moe_experts(hrows, cls, rank, cnt, wg, wu, wd, n_blocks, xs_init):
    counts = cnt[:, 0].astype(jnp.int32)
    xs, dest, blo, bhi, nblk = _dispatch_call(cls, rank, counts, hrows, n_blocks, xs_init)
    ys = _expert_call(blo, bhi, nblk, xs, _bf16(wg), _bf16(wu), _bf16(wd))
    return xs, ys, dest


def kernel(x, c, ada_w, ada_b, norm1_g, norm2_g, cm_w1, cm_b1, cm_dw, cm_dwb, cm_ln_g, cm_ln_b, cm_w2, cm_b2, rg_wy, rg_by, rg_wx, rg_bx, rg_cw, rg_cb, rg_wa, rg_ba, rg_wi, rg_bi, rg_lambda, rg_wo, rg_bo, moe_wg, moe_bg, moe_wr, moe_br, moe_w_gate, moe_w_up, moe_w_down, fin_ada_w, fin_ada_b, fin_g):
    bsz, seq, d = x.shape
    t = bsz * seq
    depth = ada_w.shape[0]
    assert depth == 2 and seq % TOKEN_TILE == 0 and t % ROW_BLOCK == 0
    assert d == 2 * PACK_SUBLANES * LANES

    mods = _ada_call(c, ada_w, ada_b[:, None, :])
    fmods = _ada_call(c, fin_ada_w[None], fin_ada_b[None, None, :])
    fmod = _mod_table(fmods[0], 2)

    n_blocks = t // ROW_BLOCK + N_CLASSES
    mod0, mod1 = _mod_table(mods[0], 6), _mod_table(mods[1], 6)

    wrt, brt = _router_operands(moe_wg[0], moe_bg[0], moe_wr[0], moe_br[0])
    dw = jnp.concatenate([cm_dw[0], jnp.zeros((CONV_HIST - CONF_KERNEL, d), jnp.float32)], axis=0)
    x1, hrows, cls, rank, cnt = _conf_call(
        x.reshape(t, d), mod0, _row(norm1_g[0]), _row(norm2_g[0]), _bf16(cm_w1[0]), _row(cm_b1[0]),
        dw, _row(cm_dwb[0]), _row(cm_ln_g[0]), _row(cm_ln_b[0]), _bf16(cm_w2[0]), _row(cm_b2[0]),
        wrt, brt, seq)
    xs_init = jnp.zeros((n_blocks * ROW_BLOCK * SUBLANES, LANES), jnp.uint32)
    xs, ys, dest = _moe_experts(hrows, cls, rank, cnt, moe_w_gate[0], moe_w_up[0], moe_w_down[0],
                                n_blocks, xs_init)

    wrt, brt = _router_operands(moe_wg[1], moe_bg[1], moe_wr[1], moe_br[1])
    cw = jnp.concatenate([rg_cw[0], jnp.zeros((SUBLANES - LRU_CONV, d), jnp.float32)], axis=0)
    x1, hrows, cls, rank, cnt = _lru_call(
        dest, x1, mod0, ys, mod1, _row(norm1_g[1]), _row(norm2_g[1]),
        _bf16(rg_wy[0]), _row(rg_by[0]), _bf16(rg_wx[0]), _row(rg_bx[0]),
        cw, _row(rg_cb[0]), _bf16(rg_wa[0]), _row(rg_ba[0]), _bf16(rg_wi[0]), _row(rg_bi[0]),
        _row(rg_lambda[0]), _bf16(rg_wo[0]), _row(rg_bo[0]), wrt, brt, seq)
    _, ys, dest = _moe_experts(hrows, cls, rank, cnt, moe_w_gate[1], moe_w_up[1], moe_w_down[1],
                               n_blocks, xs)
    out = _final_call(dest, x1, mod1, ys, seq, fmod, _row(fin_g))
    return out.reshape(bsz, seq, d)
```

```python
import functools

import jax
import jax.numpy as jnp
import numpy as np
from jax import lax
from jax.experimental import pallas as pl
from jax.experimental.pallas import tpu as pltpu

EPS = 1e-6
CONF_KERNEL = 31
LRU_HEADS = 4
LRU_CONV = 4
LRU_C = 8.0
N_GROUPS = 4
EXPERTS_PER_GROUP = 8
N_EXPERTS = N_GROUPS * EXPERTS_PER_GROUP
PAIRS_PER_GROUP = EXPERTS_PER_GROUP * (EXPERTS_PER_GROUP - 1) // 2
N_CLASSES = N_GROUPS * PAIRS_PER_GROUP

LANES = 128
SUBLANES = 8
VMEM_LIMIT_BYTES = 56 * 1024 * 1024

TOKEN_TILE = 256
TIME_TILE = TOKEN_TILE // SUBLANES
ROW_BLOCK = 256
PERM_PITCH = TIME_TILE + SUBLANES
CONV_ROWS = 64
ROUTER_ROWS = 64
PACK_SUBLANES = 4
MOD_ROWS = 8
DISPATCH_SLOTS = 3
DMA_UNROLL = 8

_NEG_INF = float("-inf")


def _class_tables():
    lo = np.zeros((N_CLASSES,), np.int32)
    hi = np.zeros((N_CLASSES,), np.int32)
    for g in range(N_GROUPS):
        for a in range(EXPERTS_PER_GROUP):
            for b in range(a + 1, EXPERTS_PER_GROUP):
                c = g * PAIRS_PER_GROUP + (a * (2 * EXPERTS_PER_GROUP - 1 - a)) // 2 + (b - a - 1)
                lo[c] = g * EXPERTS_PER_GROUP + a
                hi[c] = g * EXPERTS_PER_GROUP + b
    return lo, hi


_CLASS_LO, _CLASS_HI = _class_tables()


def _bf16(x):
    return x.astype(jnp.bfloat16)


def _dot(a, b):
    return jnp.dot(a, b, preferred_element_type=jnp.float32)


def _bits(x):
    return lax.bitcast_convert_type(x, jnp.uint32)


def _f32_from_bits(x):
    return lax.bitcast_convert_type(x, jnp.float32)


def _token_rows(ref, sublane, n):
    return ref[pl.ds(sublane, n, stride=SUBLANES), :]


def _per_batch(v, ts):
    return jnp.tile(v, (ts // SUBLANES, 1))


def _ada_kernel(c_ref, w_ref, b_ref, o_ref):
    c = c_ref[...]
    ca = _bf16(c * jax.nn.sigmoid(c))
    o_ref[...] = _dot(ca, _bf16(w_ref[...])) + b_ref[...]


def _ada_call(c, w, b):
    n_layers, d, n = w.shape
    bsz = c.shape[0]
    nb = 1024
    return pl.pallas_call(
        _ada_kernel,
        out_shape=jax.ShapeDtypeStruct((n_layers, bsz, n), jnp.float32),
        grid=(n_layers, n // nb),
        in_specs=[
            pl.BlockSpec((bsz, d), lambda l, j: (0, 0)),
            pl.BlockSpec((None, d, nb), lambda l, j: (l, 0, j)),
            pl.BlockSpec((None, 1, nb), lambda l, j: (l, 0, j)),
        ],
        out_specs=pl.BlockSpec((None, bsz, nb), lambda l, j: (l, 0, j)),
        compiler_params=pltpu.CompilerParams(
            dimension_semantics=("arbitrary", "arbitrary"),
            vmem_limit_bytes=VMEM_LIMIT_BYTES),
        name="ada_mod",
    )(c, w, b)


def _rms_mod(x, gain, shift, scale):
    ts = x.shape[0]
    ms = jnp.mean(x * x, axis=-1, keepdims=True)
    return (x * lax.rsqrt(ms + EPS)) * _per_batch(gain * (1.0 + scale), ts) + _per_batch(shift, ts)


def _causal_taps(w_ref, bias_ref, ext_ref, n_taps, ts):
    d = ext_ref.shape[1]
    cols = []
    for cb in range(d // LANES):
        sl = slice(cb * LANES, (cb + 1) * LANES)
        acc = jnp.broadcast_to(bias_ref[:, sl], (ts, LANES))
        for k in range(n_taps):
            acc = acc + w_ref[k:k + 1, sl] * ext_ref[SUBLANES * k:SUBLANES * k + ts, sl]
        cols.append(acc)
    return jnp.concatenate(cols, axis=1)


def _causal_taps_chunked(w_ref, bias_ref, ext_ref, out_ref, n_taps, ts):
    n_chunks = ext_ref.shape[0]
    units_per_chunk = ts // CONV_ROWS

    def unit(idx, carry):
        cb = idx // units_per_chunk
        r0 = pl.multiple_of((idx % units_per_chunk) * CONV_ROWS, CONV_ROWS)
        w_cb, ext_cb = w_ref.at[cb], ext_ref.at[cb]
        acc = jnp.broadcast_to(bias_ref[cb], (CONV_ROWS, LANES))
        for k in range(n_taps):
            win = ext_cb[pl.ds(pl.multiple_of(r0 + SUBLANES * k, SUBLANES), CONV_ROWS), :]
            acc = acc + w_cb[k:k + 1, :] * win
        out_ref.at[cb][pl.ds(r0, CONV_ROWS), :] = acc
        return carry

    lax.fori_loop(0, n_chunks * units_per_chunk, unit, 0)


def _start_tile_dmas(make_copy, n):
    def body(g, carry):
        for k in range(DMA_UNROLL):
            make_copy(g * DMA_UNROLL + k).start(priority=k % 2)
        return carry
    lax.fori_loop(0, n // DMA_UNROLL, body, 0)


def _wait_tile_dmas(make_copy, n):
    def body(r, carry):
        make_copy(r).wait()
        return carry
    lax.fori_loop(0, n, body, 0, unroll=DMA_UNROLL)


def _gather_expert_tiles(ys_ref, ybuf, sem, dest_ref, dnext_ref, step, n_steps, ts):
    slot = step % 2

    def tile_copy(dref, sl):
        def make(r):
            src = pl.multiple_of(dref[0, 0, r] * SUBLANES, SUBLANES)
            return pltpu.make_async_copy(
                ys_ref.at[pl.ds(src, SUBLANES)],
                ybuf.at[sl, pl.ds(pl.multiple_of(r * SUBLANES, SUBLANES), SUBLANES)],
                sem.at[sl])
        return make

    @pl.when(step == 0)
    def _():
        _start_tile_dmas(tile_copy(dest_ref, 0), ts)

    @pl.when(step + 1 < n_steps)
    def _():
        _start_tile_dmas(tile_copy(dnext_ref, 1 - slot), ts)

    _wait_tile_dmas(tile_copy(dest_ref, slot), ts)
    yv = ybuf.at[slot]
    return jnp.concatenate([_token_rows(yv, s, ts) for s in range(SUBLANES)], axis=1)


def _route(logits_t):
    ts = logits_t.shape[1]
    row = lax.broadcasted_iota(jnp.int32, (SUBLANES, ts), 0)
    gl = jnp.where(row < N_GROUPS, logits_t[0:SUBLANES], _NEG_INF)
    gmax = jnp.max(gl, axis=0, keepdims=True)
    gsel = jnp.min(jnp.where(gl == gmax, row, SUBLANES), axis=0, keepdims=True)
    p_g = 1.0 / jnp.sum(jnp.exp(gl - gmax), axis=0, keepdims=True)
    es = jnp.zeros((SUBLANES, ts), jnp.float32)
    for g in range(N_GROUPS):
        lo = SUBLANES * (1 + g)
        es = jnp.where(gsel == g, logits_t[lo:lo + EXPERTS_PER_GROUP], es)
    m1 = jnp.max(es, axis=0, keepdims=True)
    i1 = jnp.min(jnp.where(es == m1, row, SUBLANES), axis=0, keepdims=True)
    es2 = jnp.where(row == i1, _NEG_INF, es)
    m2 = jnp.max(es2, axis=0, keepdims=True)
    i2 = jnp.min(jnp.where(es2 == m2, row, SUBLANES), axis=0, keepdims=True)
    t = jnp.exp(m2 - m1)
    w1 = p_g / (1.0 + t)
    w2 = w1 * t
    first_lo = i1 < i2
    e_lo = jnp.minimum(i1, i2)
    e_hi = jnp.maximum(i1, i2)
    gate_lo = jnp.where(first_lo, w1, w2)
    gate_hi = jnp.where(first_lo, w2, w1)
    pair = lax.shift_right_logical(e_lo * (2 * EXPERTS_PER_GROUP - 1 - e_lo), 1) + (e_hi - e_lo - 1)
    cls = gsel * PAIRS_PER_GROUP + pair
    return cls, gate_lo, gate_hi


def _post_mixer(x1, mod_ref, n2_ref, wrt_ref, brt_ref, hrow_ref, cls_ref, rank_ref, cnt_ref, step):
    ts = x1.shape[0]
    h2 = _rms_mod(x1, n2_ref[...], mod_ref[3], mod_ref[4])
    h2b = _bf16(h2)
    logits_t = lax.dot_general(wrt_ref[...], h2b, (((1,), (1,)), ((), ())),
                               preferred_element_type=jnp.float32) + brt_ref[:, 0:1]
    cls, gate_lo, gate_hi = _route(logits_t)
    gates_t = jnp.concatenate(
        [gate_lo, gate_hi, jnp.zeros((LANES - 2, ts), jnp.float32)], axis=0)
    h2r = h2b.astype(jnp.float32)
    for j in range(PACK_SUBLANES):
        hi = _bits(h2r[:, (2 * j) * LANES:(2 * j + 1) * LANES])
        lo = _bits(h2r[:, (2 * j + 1) * LANES:(2 * j + 2) * LANES])
        hrow_ref[pl.ds(j, ts, stride=SUBLANES), :] = hi | lax.shift_right_logical(lo, jnp.uint32(16))
    hrow_ref[pl.ds(PACK_SUBLANES, ts, stride=SUBLANES), :] = _bits(gates_t.T)
    for j in range(PACK_SUBLANES + 1, SUBLANES):
        hrow_ref[pl.ds(j, ts, stride=SUBLANES), :] = jnp.zeros((ts, LANES), jnp.uint32)
    cls_ref[0] = cls
    crow = lax.broadcasted_iota(jnp.int32, (LANES, ts), 0)
    onehot = jnp.where(crow == cls, 1.0, 0.0)
    cnt = jnp.sum(onehot, axis=1, keepdims=True)

    @pl.when(step == 0)
    def _():
        cnt_ref[...] = jnp.zeros_like(cnt_ref)

    earlier = (lax.broadcasted_iota(jnp.int32, (ts, ts), 0)
               < lax.broadcasted_iota(jnp.int32, (ts, ts), 1))
    prefix = _dot(_bf16(onehot), _bf16(jnp.where(earlier, 1.0, 0.0)))
    rank = jnp.sum(onehot * (prefix + cnt_ref[:, 0:1]), axis=0, keepdims=True)
    rank_ref[0] = rank.astype(jnp.int32)
    cnt_ref[...] += jnp.broadcast_to(cnt, cnt_ref.shape)


def _mixer_out(t, d, ts):
    n_tiles = t // ts
    out_shape = (
        jax.ShapeDtypeStruct((t, d), jnp.float32),
        jax.ShapeDtypeStruct((t * SUBLANES, LANES), jnp.uint32),
        jax.ShapeDtypeStruct((n_tiles, 1, ts), jnp.int32),
        jax.ShapeDtypeStruct((n_tiles, 1, ts), jnp.int32),
        jax.ShapeDtypeStruct((LANES, LANES), jnp.float32),
    )
    out_specs = (
        pl.BlockSpec((ts, d), lambda i: (i, 0)),
        pl.BlockSpec((ts * SUBLANES, LANES), lambda i: (i, 0)),
        pl.BlockSpec((1, 1, ts), lambda i: (i, 0, 0)),
        pl.BlockSpec((1, 1, ts), lambda i: (i, 0, 0)),
        pl.BlockSpec((LANES, LANES), lambda i: (0, 0)),
    )
    return out_shape, out_specs


def _full(shape):
    return pl.BlockSpec(shape, lambda i: tuple(0 for _ in shape))


def _conf_kernel(x_ref, mod_ref, n1_ref, n2_ref, w1_ref, b1_ref, dw_ref, dwb_ref,
                 lng_ref, lnb_ref, w2_ref, b2_ref, wrt_ref, brt_ref,
                 x1_ref, hrow_ref, cls_ref, rank_ref, cnt_ref, perm, vext, conv):
    i = pl.program_id(0)
    bsz, tm, d = x_ref.shape
    ts = bsz * tm
    hist = SUBLANES * (CONF_KERNEL - 1)
    for cb in range(d // LANES):
        for p in range(bsz):
            perm[cb, PERM_PITCH * p:PERM_PITCH * p + tm, :] = x_ref[p, :, cb * LANES:(cb + 1) * LANES]
    x = jnp.concatenate(
        [jnp.concatenate([perm.at[cb][pl.ds(m, bsz, stride=PERM_PITCH), :] for cb in range(d // LANES)],
                         axis=1) for m in range(tm)], axis=0)

    h = _bf16(_rms_mod(x, n1_ref[...], mod_ref[0], mod_ref[1]))
    u = _dot(h, w1_ref[...]) + b1_ref[...]
    v = u[:, 0:d] * jax.nn.sigmoid(u[:, d:2 * d])

    @pl.when(i == 0)
    def _():
        vext[:, 0:hist, :] = jnp.zeros((d // LANES, hist, LANES), jnp.float32)

    for cb in range(d // LANES):
        vext[cb, hist:hist + ts, :] = v[:, cb * LANES:(cb + 1) * LANES]
    _causal_taps_chunked(dw_ref, dwb_ref, vext, conv, CONF_KERNEL, ts)
    vext[:, 0:hist, :] = vext[:, ts:ts + hist, :]
    y = jnp.concatenate([conv[cb] for cb in range(d // LANES)], axis=1)

    mu = jnp.mean(y, axis=-1, keepdims=True)
    yc = y - mu
    var = jnp.mean(yc * yc, axis=-1, keepdims=True)
    z = yc * lax.rsqrt(var + EPS) * lng_ref[...] + lnb_ref[...]
    z = z * jax.nn.sigmoid(z)
    m = _dot(_bf16(z), w2_ref[...]) + b2_ref[...]
    x1 = x + _per_batch(mod_ref[2], ts) * m
    x1_ref[...] = x1
    _post_mixer(x1, mod_ref, n2_ref, wrt_ref, brt_ref, hrow_ref, cls_ref, rank_ref, cnt_ref, i)


def _conf_call(x, mod, n1, n2, w1, b1, dw, dwb, lng, lnb, w2, b2, wrt, brt):
    bsz, seq, d = x.shape
    tm = TIME_TILE
    ts = bsz * tm
    out_shape, out_specs = _mixer_out(bsz * seq, d, ts)
    hist = SUBLANES * (CONF_KERNEL - 1)
    return pl.pallas_call(
        _conf_kernel,
        out_shape=out_shape,
        grid=(seq // tm,),
        in_specs=[
            pl.BlockSpec((bsz, tm, d), lambda i: (0, i, 0)),
            _full((MOD_ROWS, bsz, d)),
            _full((1, d)), _full((1, d)),
            _full((d, 2 * d)), _full((1, 2 * d)),
            _full((d // LANES, CONF_KERNEL + 1, LANES)), _full((d // LANES, 1, LANES)),
            _full((1, d)), _full((1, d)),
            _full((d, d)), _full((1, d)),
            _full((ROUTER_ROWS, d)), _full((ROUTER_ROWS, LANES)),
        ],
        out_specs=out_specs,
        scratch_shapes=[
            pltpu.VMEM((d // LANES, bsz * PERM_PITCH, LANES), jnp.float32),
            pltpu.VMEM((d // LANES, hist + ts, LANES), jnp.float32),
            pltpu.VMEM((d // LANES, ts, LANES), jnp.float32),
        ],
        compiler_params=pltpu.CompilerParams(
            dimension_semantics=("arbitrary",), vmem_limit_bytes=VMEM_LIMIT_BYTES),
        name="conf_mixer",
    )(x, mod, n1, n2, w1, b1, dw, dwb, lng, lnb, w2, b2, wrt, brt)


def _lru_kernel(n_tiles, dest_ref, dnext_ref, xprev_ref, pmod_ref, mod_ref, n1_ref, n2_ref,
                wy_ref, by_ref, wx_ref, bx_ref,
                cw_ref, cb_ref, wa_ref, ba_ref, wi_ref, bi_ref, lam_ref, wo_ref, bo_ref,
                wrt_ref, brt_ref, ys_ref,
                x1_ref, hrow_ref, cls_ref, rank_ref, cnt_ref, uext, hcar, ybuf, ysem):
    i = pl.program_id(0)
    ts, d = xprev_ref.shape
    hd = d // LRU_HEADS
    hist = SUBLANES * (LRU_CONV - 1)
    y_prev = _gather_expert_tiles(ys_ref, ybuf, ysem, dest_ref, dnext_ref, i, n_tiles, ts)
    x = xprev_ref[...] + _per_batch(pmod_ref[5], ts) * y_prev
    h = _bf16(_rms_mod(x, n1_ref[...], mod_ref[0], mod_ref[1]))
    ygate = jax.nn.gelu(_dot(h, wy_ref[...]) + by_ref[...])
    xb = _dot(h, wx_ref[...]) + bx_ref[...]

    @pl.when(i == 0)
    def _():
        uext[0:hist, :] = jnp.zeros((hist, d), jnp.float32)
        hcar[...] = jnp.zeros_like(hcar)

    uext[hist:hist + ts, :] = xb
    u = _causal_taps(cw_ref, cb_ref, uext, LRU_CONV, ts)
    uext[0:hist, :] = uext[ts:ts + hist, :]

    ub = _bf16(u)
    r_parts, i_parts = [], []
    for hh in range(LRU_HEADS):
        uh = ub[:, hh * hd:(hh + 1) * hd]
        r_parts.append(_dot(uh, wa_ref[hh]))
        i_parts.append(_dot(uh, wi_ref[hh]))
    r = jax.nn.sigmoid(jnp.concatenate(r_parts, axis=1) + ba_ref[...])
    ig = jax.nn.sigmoid(jnp.concatenate(i_parts, axis=1) + bi_ref[...])

    nl = -lam_ref[...]
    softplus = jnp.maximum(nl, 0.0) + jnp.log1p(jnp.exp(-jnp.abs(nl)))
    log_a = (-LRU_C) * r * softplus
    a = jnp.exp(log_a)
    mult = jnp.sqrt(-jnp.tanh(log_a) * (1.0 + a * a))
    b = mult * (ig * u)

    hstate = hcar[...]
    outs = []
    for m in range(ts // SUBLANES):
        sl = slice(m * SUBLANES, (m + 1) * SUBLANES)
        hstate = a[sl] * hstate + b[sl]
        outs.append(hstate)
    hcar[...] = hstate
    hs = jnp.concatenate(outs, axis=0)

    mix = _dot(_bf16(hs * ygate), wo_ref[...]) + bo_ref[...]
    x1 = x + _per_batch(mod_ref[2], ts) * mix
    x1_ref[...] = x1
    _post_mixer(x1, mod_ref, n2_ref, wrt_ref, brt_ref, hrow_ref, cls_ref, rank_ref, cnt_ref, i)


def _lru_call(dest, x_prev, mod_prev, ys, mod, n1, n2, wy, by, wx, bx, cw, cb, wa, ba, wi, bi, lam,
              wo, bo, wrt, brt):
    t, d = x_prev.shape
    n_tiles, _, ts = dest.shape
    bsz = mod.shape[1]
    hd = d // LRU_HEADS
    hist = SUBLANES * (LRU_CONV - 1)
    out_shape, out_specs = _mixer_out(t, d, ts)
    return pl.pallas_call(
        functools.partial(_lru_kernel, n_tiles),
        out_shape=out_shape,
        grid=(n_tiles,),
        in_specs=[
            pl.BlockSpec((1, 1, ts), lambda i: (i, 0, 0), memory_space=pltpu.SMEM),
            pl.BlockSpec((1, 1, ts), lambda i: (jnp.minimum(i + 1, n_tiles - 1), 0, 0),
                         memory_space=pltpu.SMEM),
            pl.BlockSpec((ts, d), lambda i: (i, 0)),
            _full((MOD_ROWS, bsz, d)), _full((MOD_ROWS, bsz, d)),
            _full((1, d)), _full((1, d)),
            _full((d, d)), _full((1, d)),
            _full((d, d)), _full((1, d)),
            _full((LRU_CONV, d)), _full((1, d)),
            _full((LRU_HEADS, hd, hd)), _full((1, d)),
            _full((LRU_HEADS, hd, hd)), _full((1, d)),
            _full((1, d)),
            _full((d, d)), _full((1, d)),
            _full((ROUTER_ROWS, d)), _full((ROUTER_ROWS, LANES)),
            pl.BlockSpec(memory_space=pl.ANY),
        ],
        out_specs=out_specs,
        scratch_shapes=[
            pltpu.VMEM((hist + ts, d), jnp.float32),
            pltpu.VMEM((SUBLANES, d), jnp.float32),
            pltpu.VMEM((2, ts * SUBLANES, LANES), jnp.float32),
            pltpu.SemaphoreType.DMA((2,)),
        ],
        compiler_params=pltpu.CompilerParams(
            dimension_semantics=("arbitrary",), vmem_limit_bytes=VMEM_LIMIT_BYTES),
        name="lru_mixer",
    )(dest, dest, x_prev, mod_prev, mod, n1, n2, wy, by, wx, bx, cw, cb, wa, ba, wi, bi, lam, wo, bo,
      wrt, brt, ys)


def _dispatch_kernel(n_tiles, n_blocks, cls_ref, rank_ref, cnt_ref, clo_ref, chi_ref, hrow_ref, xs_in_ref,
                     xs_ref, dest_ref, blo_ref, bhi_ref, nblk_ref, class_start, stage, sem_in, sem_out):
    del xs_in_ref
    i = pl.program_id(0)
    ts = cls_ref.shape[2]
    rows = ts * SUBLANES

    @pl.when(i == 0)
    def _():
        def per_class(c, nb_total):
            class_start[c] = nb_total * ROW_BLOCK
            nb = lax.shift_right_logical(cnt_ref[c] + (ROW_BLOCK - 1), ROW_BLOCK.bit_length() - 1)

            def fill(k, carry):
                blo_ref[nb_total + k] = clo_ref[c]
                bhi_ref[nb_total + k] = chi_ref[c]
                return carry

            lax.fori_loop(0, nb, fill, 0)
            return nb_total + nb

        nb_used = lax.fori_loop(0, N_CLASSES, per_class, 0)
        nblk_ref[0] = nb_used
        last = jnp.maximum(nb_used - 1, 0)

        def tail(k, carry):
            blo_ref[k] = blo_ref[last]
            bhi_ref[k] = bhi_ref[last]
            return carry

        lax.fori_loop(nb_used, n_blocks, tail, 0)

    def load(tile, slot):
        return pltpu.make_async_copy(hrow_ref.at[pl.ds(pl.multiple_of(tile * rows, rows), rows)],
                                     stage.at[slot], sem_in.at[slot])

    def tile_copy(slot, r, dst_row):
        return pltpu.make_async_copy(
            stage.at[slot, pl.ds(pl.multiple_of(r * SUBLANES, SUBLANES), SUBLANES)],
            xs_ref.at[pl.ds(pl.multiple_of(dst_row * SUBLANES, SUBLANES), SUBLANES)],
            sem_out.at[slot])

    def drain(slot):
        _wait_tile_dmas(lambda r: tile_copy(slot, r, 0), ts)

    slot = i % DISPATCH_SLOTS
    nxt = (i + 1) % DISPATCH_SLOTS

    @pl.when(i == 0)
    def _():
        load(0, 0).start()

    @pl.when(i >= 2)
    def _():
        drain(nxt)

    @pl.when(i + 1 < n_tiles)
    def _():
        load(i + 1, nxt).start()

    load(i, slot).wait()

    def scatter(r):
        dst_row = class_start[cls_ref[0, 0, r]] + rank_ref[0, 0, r]
        dest_ref[0, 0, r] = dst_row
        return tile_copy(slot, r, dst_row)

    _start_tile_dmas(scatter, ts)

    @pl.when(i == n_tiles - 1)
    def _():
        @pl.when(i >= 1)
        def _():
            drain((i + DISPATCH_SLOTS - 1) % DISPATCH_SLOTS)
        drain(slot)


def _dispatch_call(cls, rank, counts, hrows, n_blocks, xs_init):
    n_tiles, _, ts = cls.shape
    n_rows = n_blocks * ROW_BLOCK
    assert xs_init.shape == (n_rows * SUBLANES, LANES) and xs_init.dtype == jnp.uint32
    smem_full = pl.BlockSpec(memory_space=pltpu.SMEM)
    smem_tile = pl.BlockSpec((1, 1, ts), lambda i: (i, 0, 0), memory_space=pltpu.SMEM)
    any_spec = pl.BlockSpec(memory_space=pl.ANY)
    return pl.pallas_call(
        functools.partial(_dispatch_kernel, n_tiles, n_blocks),
        out_shape=(
            jax.ShapeDtypeStruct((n_rows * SUBLANES, LANES), jnp.uint32),
            jax.ShapeDtypeStruct((n_tiles, 1, ts), jnp.int32),
            jax.ShapeDtypeStruct((n_blocks,), jnp.int32),
            jax.ShapeDtypeStruct((n_blocks,), jnp.int32),
            jax.ShapeDtypeStruct((1,), jnp.int32),
        ),
        grid=(n_tiles,),
        in_specs=[
            smem_tile, smem_tile,
            smem_full, smem_full, smem_full, any_spec, any_spec,
        ],
        out_specs=(
            any_spec,
            smem_tile,
            smem_full, smem_full, smem_full,
        ),
        scratch_shapes=[
            pltpu.SMEM((LANES,), jnp.int32),
            pltpu.VMEM((DISPATCH_SLOTS, ts * SUBLANES, LANES), jnp.uint32),
            pltpu.SemaphoreType.DMA((DISPATCH_SLOTS,)),
            pltpu.SemaphoreType.DMA((DISPATCH_SLOTS,)),
        ],
        input_output_aliases={6: 0},
        compiler_params=pltpu.CompilerParams(
            dimension_semantics=("arbitrary",), vmem_limit_bytes=VMEM_LIMIT_BYTES),
        name="moe_dispatch",
    )(cls, rank, counts, jnp.asarray(_CLASS_LO), jnp.asarray(_CLASS_HI), hrows, xs_init)


def _expert_kernel(blo_ref, bhi_ref, nblk_ref, xs_ref, wg_lo, wu_lo, wd_lo, wg_hi, wu_hi, wd_hi, ys_ref):
    del blo_ref, bhi_ref
    b = pl.program_id(0)
    rb = ROW_BLOCK

    @pl.when(b < nblk_ref[0])
    def _():
        pieces = []
        for j in range(PACK_SUBLANES):
            w = _token_rows(xs_ref, j, rb)
            pieces.append(_bf16(_f32_from_bits(w & jnp.uint32(0xFFFF0000))))
            pieces.append(_bf16(_f32_from_bits(lax.shift_left(w, jnp.uint32(16)))))
        xb = jnp.concatenate(pieces, axis=1)
        gates = _f32_from_bits(_token_rows(xs_ref, PACK_SUBLANES, rb))
        acc = None
        for idx, (wg, wu, wd) in enumerate(((wg_lo, wu_lo, wd_lo), (wg_hi, wu_hi, wd_hi))):
            a = _dot(xb, wg[...])
            hid = (a * jax.nn.sigmoid(a)) * _dot(xb, wu[...]) * gates[:, idx:idx + 1]
            part = _dot(_bf16(hid), wd[...])
            acc = part if acc is None else acc + part
        for s in range(SUBLANES):
            ys_ref[pl.ds(s, rb, stride=SUBLANES), :] = acc[:, s * LANES:(s + 1) * LANES]

    @pl.when(b >= nblk_ref[0])
    def _():
        ys_ref[...] = jnp.zeros_like(ys_ref)


def _expert_call(blo, bhi, nblk, xs, wg, wu, wd):
    n_rows = xs.shape[0] // SUBLANES
    d, de = wg.shape[1], wg.shape[2]
    assert d == SUBLANES * LANES
    n_blocks = n_rows // ROW_BLOCK

    def w_spec(shape, which):
        if which == 0:
            return pl.BlockSpec((None,) + shape, lambda b, lo, hi, n: (lo[b], 0, 0))
        return pl.BlockSpec((None,) + shape, lambda b, lo, hi, n: (hi[b], 0, 0))

    def x_map(b, lo, hi, n):
        return (jnp.minimum(b, jnp.maximum(n[0] - 1, 0)), 0)

    grid_spec = pltpu.PrefetchScalarGridSpec(
        num_scalar_prefetch=3,
        grid=(n_blocks,),
        in_specs=[
            pl.BlockSpec((ROW_BLOCK * SUBLANES, LANES), x_map),
            w_spec((d, de), 0), w_spec((d, de), 0), w_spec((de, d), 0),
            w_spec((d, de), 1), w_spec((d, de), 1), w_spec((de, d), 1),
        ],
        out_specs=pl.BlockSpec((ROW_BLOCK * SUBLANES, LANES), lambda b, lo, hi, n: (b, 0)),
    )
    return pl.pallas_call(
        _expert_kernel,
        out_shape=jax.ShapeDtypeStruct((n_rows * SUBLANES, LANES), jnp.float32),
        grid_spec=grid_spec,
        compiler_params=pltpu.CompilerParams(
            dimension_semantics=("arbitrary",), vmem_limit_bytes=VMEM_LIMIT_BYTES),
        name="moe_experts",
    )(blo, bhi, nblk, xs, wg, wu, wd, wg, wu, wd)


def _final_kernel(n_tiles, dest_ref, dnext_ref, x1_ref, mod_ref, fmod_ref, fg_ref, ys_ref,
                  out_ref, ybuf, sem, unperm):
    i = pl.program_id(0)
    ts, d = x1_ref.shape
    bsz, tm, _ = out_ref.shape
    y = _gather_expert_tiles(ys_ref, ybuf, sem, dest_ref, dnext_ref, i, n_tiles, ts)
    x2 = x1_ref[...] + _per_batch(mod_ref[5], ts) * y
    res = _rms_mod(x2, fg_ref[...], fmod_ref[0], fmod_ref[1])
    for cb in range(d // LANES):
        unperm[cb] = res[:, cb * LANES:(cb + 1) * LANES]
    for cb in range(d // LANES):
        for p in range(bsz):
            out_ref[p, :, cb * LANES:(cb + 1) * LANES] = unperm.at[cb][pl.ds(p, tm, stride=SUBLANES), :]


def _final_call(dest, x1, mod, ys, bsz, fmod, fin_g):
    t, d = x1.shape
    n_tiles, _, ts = dest.shape
    tm = ts // bsz
    return pl.pallas_call(
        functools.partial(_final_kernel, n_tiles),
        out_shape=jax.ShapeDtypeStruct((bsz, t // bsz, d), jnp.float32),
        grid=(n_tiles,),
        in_specs=[
            pl.BlockSpec((1, 1, ts), lambda i: (i, 0, 0), memory_space=pltpu.SMEM),
            pl.BlockSpec((1, 1, ts), lambda i: (jnp.minimum(i + 1, n_tiles - 1), 0, 0),
                         memory_space=pltpu.SMEM),
            pl.BlockSpec((ts, d), lambda i: (i, 0)),
            _full((MOD_ROWS, bsz, d)), _full((MOD_ROWS, bsz, d)), _full((1, d)),
            pl.BlockSpec(memory_space=pl.ANY),
        ],
        out_specs=pl.BlockSpec((bsz, tm, d), lambda i: (0, i, 0)),
        scratch_shapes=[pltpu.VMEM((2, ts * SUBLANES, LANES), jnp.float32),
                        pltpu.SemaphoreType.DMA((2,)),
                        pltpu.VMEM((d // LANES, ts, LANES), jnp.float32)],
        compiler_params=pltpu.CompilerParams(
            dimension_semantics=("arbitrary",), vmem_limit_bytes=VMEM_LIMIT_BYTES),
        name="moe_combine_final",
    )(dest, dest, x1, mod, fmod, fin_g, ys)


def _router_operands(wg, bg, wr, br):
    d = wg.shape[0]
    pad_g = jnp.zeros((SUBLANES - N_GROUPS, d), jnp.float32)
    pad_e = jnp.zeros((ROUTER_ROWS - SUBLANES - N_EXPERTS, d), jnp.float32)
    wrt = jnp.concatenate([wg.T, pad_g, wr.T, pad_e], axis=0)
    bcol = jnp.concatenate([bg, jnp.zeros((SUBLANES - N_GROUPS,), jnp.float32), br,
                            jnp.zeros((ROUTER_ROWS - SUBLANES - N_EXPERTS,), jnp.float32)])
    brt = jnp.broadcast_to(bcol[:, None], (ROUTER_ROWS, LANES))
    return _bf16(wrt), brt


def _mod_table(m, parts):
    bsz = m.shape[0]
    d = m.shape[1] // parts
    m = jnp.transpose(m.reshape(bsz, parts, d), (1, 0, 2))
    return jnp.concatenate([m, jnp.zeros((MOD_ROWS - parts, bsz, d), jnp.float32)], axis=0)


def _row(v):
    return v.reshape(1, -1)


def _moe_experts(hrows, cls, rank, cnt, wg, wu, wd, n_blocks, xs_init):
    counts = cnt[:, 0].astype(jnp.int32)
    xs, dest, blo, bhi, nblk = _dispatch_call(cls, rank, counts, hrows, n_blocks, xs_init)
    ys = _expert_call(blo, bhi, nblk, xs, _bf16(wg), _bf16(wu), _bf16(wd))
    return xs, ys, dest


def kernel(x, c, ada_w, ada_b, norm1_g, norm2_g, cm_w1, cm_b1, cm_dw, cm_dwb, cm_ln_g, cm_ln_b, cm_w2, cm_b2, rg_wy, rg_by, rg_wx, rg_bx, rg_cw, rg_cb, rg_wa, rg_ba, rg_wi, rg_bi, rg_lambda, rg_wo, rg_bo, moe_wg, moe_bg, moe_wr, moe_br, moe_w_gate, moe_w_up, moe_w_down, fin_ada_w, fin_ada_b, fin_g):
    bsz, seq, d = x.shape
    t = bsz * seq
    depth = ada_w.shape[0]
    assert depth == 2 and bsz == SUBLANES and seq % TIME_TILE == 0 and t % ROW_BLOCK == 0
    assert d == 2 * PACK_SUBLANES * LANES

    mods = _ada_call(c, ada_w, ada_b[:, None, :])
    fmods = _ada_call(c, fin_ada_w[None], fin_ada_b[None, None, :])
    fmod = _mod_table(fmods[0], 2)
    n_blocks = t // ROW_BLOCK + N_CLASSES
    mod0, mod1 = _mod_table(mods[0], 6), _mod_table(mods[1], 6)

    wrt, brt = _router_operands(moe_wg[0], moe_bg[0], moe_wr[0], moe_br[0])
    dw = jnp.concatenate([cm_dw[0], jnp.zeros((1, d), jnp.float32)], axis=0)
    dw = jnp.transpose(dw.reshape(CONF_KERNEL + 1, d // LANES, LANES), (1, 0, 2))
    x1, hrows, cls, rank, cnt = _conf_call(
        x, mod0, _row(norm1_g[0]), _row(norm2_g[0]), _bf16(cm_w1[0]), _row(cm_b1[0]),
        dw, cm_dwb[0].reshape(d // LANES, 1, LANES), _row(cm_ln_g[0]), _row(cm_ln_b[0]),
        _bf16(cm_w2[0]), _row(cm_b2[0]), wrt, brt)
    xs_init = jnp.zeros((n_blocks * ROW_BLOCK * SUBLANES, LANES), jnp.uint32)
    xs, ys, dest = _moe_experts(hrows, cls, rank, cnt, moe_w_gate[0], moe_w_up[0], moe_w_down[0],
                                n_blocks, xs_init)

    wrt, brt = _router_operands(moe_wg[1], moe_bg[1], moe_wr[1], moe_br[1])
    x1, hrows, cls, rank, cnt = _lru_call(
        dest, x1, mod0, ys, mod1, _row(norm1_g[1]), _row(norm2_g[1]),
        _bf16(rg_wy[0]), _row(rg_by[0]), _bf16(rg_wx[0]), _row(rg_bx[0]),
        rg_cw[0], _row(rg_cb[0]), _bf16(rg_wa[0]), _row(rg_ba[0]), _bf16(rg_wi[0]), _row(rg_bi[0]),
        _row(rg_lambda[0]), _bf16(rg_wo[0]), _row(rg_bo[0]), wrt, brt)
    _, ys, dest = _moe_experts(hrows, cls, rank, cnt, moe_w_gate[1], moe_w_up[1], moe_w_down[1],
                               n_blocks, xs)
    return _final_call(dest, x1, mod1, ys, bsz, fmod, _row(fin_g))
```

```python
import functools

import jax
import jax.numpy as jnp
import numpy as np
from jax import lax
from jax.experimental import pallas as pl
from jax.experimental.pallas import tpu as pltpu

EPS = 1e-6
CONF_KERNEL = 31
LRU_HEADS = 4
LRU_CONV = 4
LRU_C = 8.0
N_GROUPS = 4
EXPERTS_PER_GROUP = 8
N_EXPERTS = N_GROUPS * EXPERTS_PER_GROUP
PAIRS_PER_GROUP = EXPERTS_PER_GROUP * (EXPERTS_PER_GROUP - 1) // 2
N_CLASSES = N_GROUPS * PAIRS_PER_GROUP

LANES = 128
SUBLANES = 8
VMEM_LIMIT_BYTES = 56 * 1024 * 1024

TOKEN_TILE = 512
TIME_TILE = TOKEN_TILE // SUBLANES
SUB_TILE = 256
ROW_BLOCK = 256
PERM_PITCH = TIME_TILE + SUBLANES
CONV_ROWS = 128
ROUTER_ROWS = 64
PACK_SUBLANES = 4
MOD_ROWS = 8
DISPATCH_SLOTS = 3
DMA_UNROLL = 8

_NEG_INF = float("-inf")


def _class_tables():
    lo = np.zeros((N_CLASSES,), np.int32)
    hi = np.zeros((N_CLASSES,), np.int32)
    for g in range(N_GROUPS):
        for a in range(EXPERTS_PER_GROUP):
            for b in range(a + 1, EXPERTS_PER_GROUP):
                c = g * PAIRS_PER_GROUP + (a * (2 * EXPERTS_PER_GROUP - 1 - a)) // 2 + (b - a - 1)
                lo[c] = g * EXPERTS_PER_GROUP + a
                hi[c] = g * EXPERTS_PER_GROUP + b
    return lo, hi


_CLASS_LO, _CLASS_HI = _class_tables()


def _bf16(x):
    return x.astype(jnp.bfloat16)


def _dot(a, b):
    return jnp.dot(a, b, preferred_element_type=jnp.float32)


def _bits(x):
    return lax.bitcast_convert_type(x, jnp.uint32)


def _f32_from_bits(x):
    return lax.bitcast_convert_type(x, jnp.float32)


def _token_rows(ref, sublane, n):
    return ref[pl.ds(sublane, n, stride=SUBLANES), :]


def _per_batch(v, ts):
    return jnp.tile(v, (ts // SUBLANES, 1))


def _ada_kernel(c_ref, w_ref, b_ref, o_ref):
    c = c_ref[...]
    ca = _bf16(c * jax.nn.sigmoid(c))
    o_ref[...] = _dot(ca, _bf16(w_ref[...])) + b_ref[...]


def _ada_call(c, w, b):
    n_layers, d, n = w.shape
    bsz = c.shape[0]
    nb = 1024
    return pl.pallas_call(
        _ada_kernel,
        out_shape=jax.ShapeDtypeStruct((n_layers, bsz, n), jnp.float32),
        grid=(n_layers, n // nb),
        in_specs=[
            pl.BlockSpec((bsz, d), lambda l, j: (0, 0)),
            pl.BlockSpec((None, d, nb), lambda l, j: (l, 0, j)),
            pl.BlockSpec((None, 1, nb), lambda l, j: (l, 0, j)),
        ],
        out_specs=pl.BlockSpec((None, bsz, nb), lambda l, j: (l, 0, j)),
        compiler_params=pltpu.CompilerParams(
            dimension_semantics=("arbitrary", "arbitrary"),
            vmem_limit_bytes=VMEM_LIMIT_BYTES),
        name="ada_mod",
    )(c, w, b)


def _rms_mod(x, gain, shift, scale):
    ts = x.shape[0]
    ms = jnp.mean(x * x, axis=-1, keepdims=True)
    return (x * lax.rsqrt(ms + EPS)) * _per_batch(gain * (1.0 + scale), ts) + _per_batch(shift, ts)


def _causal_taps(w_ref, bias_ref, ext_ref, n_taps, row0, n):
    d = ext_ref.shape[1]
    cols = []
    for cb in range(d // LANES):
        sl = slice(cb * LANES, (cb + 1) * LANES)
        acc = jnp.broadcast_to(bias_ref[:, sl], (n, LANES))
        for k in range(n_taps):
            lo = row0 + SUBLANES * k
            acc = acc + w_ref[k:k + 1, sl] * ext_ref[lo:lo + n, sl]
        cols.append(acc)
    return jnp.concatenate(cols, axis=1)


def _causal_taps_chunked(w_ref, bias_ref, ext_ref, out_ref, n_taps, ts):
    n_chunks = ext_ref.shape[0]
    units_per_chunk = ts // CONV_ROWS

    def unit(idx, carry):
        cb = idx // units_per_chunk
        r0 = pl.multiple_of((idx % units_per_chunk) * CONV_ROWS, CONV_ROWS)
        w_cb, ext_cb = w_ref.at[cb], ext_ref.at[cb]
        acc = jnp.broadcast_to(bias_ref[cb], (CONV_ROWS, LANES))
        for k in range(n_taps):
            win = ext_cb[pl.ds(pl.multiple_of(r0 + SUBLANES * k, SUBLANES), CONV_ROWS), :]
            acc = acc + w_cb[k:k + 1, :] * win
        out_ref.at[cb][pl.ds(r0, CONV_ROWS), :] = acc
        return carry

    lax.fori_loop(0, n_chunks * units_per_chunk, unit, 0)


def _start_tile_dmas(make_copy, n):
    def body(r, carry):
        make_copy(r).start()
        return carry
    lax.fori_loop(0, n, body, 0, unroll=DMA_UNROLL)


def _wait_tile_dmas(make_copy, n):
    def body(r, carry):
        make_copy(r).wait()
        return carry
    lax.fori_loop(0, n, body, 0, unroll=DMA_UNROLL)


def _gather_expert_tiles(ys_ref, ybuf, sem, dest_ref, dnext_ref, step, n_steps, ts):
    slot = step % 2

    def tile_copy(dref, sl):
        def make(r):
            src = pl.multiple_of(dref[0, 0, r] * SUBLANES, SUBLANES)
            return pltpu.make_async_copy(
                ys_ref.at[pl.ds(src, SUBLANES)],
                ybuf.at[sl, pl.ds(pl.multiple_of(r * SUBLANES, SUBLANES), SUBLANES)],
                sem.at[sl])
        return make

    @pl.when(step == 0)
    def _():
        _start_tile_dmas(tile_copy(dest_ref, 0), ts)

    @pl.when(step + 1 < n_steps)
    def _():
        _start_tile_dmas(tile_copy(dnext_ref, 1 - slot), ts)

    _wait_tile_dmas(tile_copy(dest_ref, slot), ts)
    yv = ybuf.at[slot]
    return jnp.concatenate([_token_rows(yv, s, ts) for s in range(SUBLANES)], axis=1)


def _route(logits_t):
    ts = logits_t.shape[1]
    row = lax.broadcasted_iota(jnp.int32, (SUBLANES, ts), 0)
    gl = jnp.where(row < N_GROUPS, logits_t[0:SUBLANES], _NEG_INF)
    gmax = jnp.max(gl, axis=0, keepdims=True)
    gsel = jnp.min(jnp.where(gl == gmax, row, SUBLANES), axis=0, keepdims=True)
    p_g = 1.0 / jnp.sum(jnp.exp(gl - gmax), axis=0, keepdims=True)
    es = jnp.zeros((SUBLANES, ts), jnp.float32)
    for g in range(N_GROUPS):
        lo = SUBLANES * (1 + g)
        es = jnp.where(gsel == g, logits_t[lo:lo + EXPERTS_PER_GROUP], es)
    m1 = jnp.max(es, axis=0, keepdims=True)
    i1 = jnp.min(jnp.where(es == m1, row, SUBLANES), axis=0, keepdims=True)
    es2 = jnp.where(row == i1, _NEG_INF, es)
    m2 = jnp.max(es2, axis=0, keepdims=True)
    i2 = jnp.min(jnp.where(es2 == m2, row, SUBLANES), axis=0, keepdims=True)
    t = jnp.exp(m2 - m1)
    w1 = p_g / (1.0 + t)
    w2 = w1 * t
    first_lo = i1 < i2
    e_lo = jnp.minimum(i1, i2)
    e_hi = jnp.maximum(i1, i2)
    gate_lo = jnp.where(first_lo, w1, w2)
    gate_hi = jnp.where(first_lo, w2, w1)
    pair = lax.shift_right_logical(e_lo * (2 * EXPERTS_PER_GROUP - 1 - e_lo), 1) + (e_hi - e_lo - 1)
    cls = gsel * PAIRS_PER_GROUP + pair
    return cls, gate_lo, gate_hi


def _post_mixer(x1, mod_ref, n2_ref, wrt_ref, brt_ref, hrow_ref, cls_ref, rank_ref, cnt_ref, sub):
    ts = x1.shape[0]
    hrow_ref = hrow_ref.at[pl.ds(sub * ts * SUBLANES, ts * SUBLANES)]
    cols = slice(sub * ts, (sub + 1) * ts)
    h2 = _rms_mod(x1, n2_ref[...], mod_ref[3], mod_ref[4])
    h2b = _bf16(h2)
    logits_t = lax.dot_general(wrt_ref[...], h2b, (((1,), (1,)), ((), ())),
                               preferred_element_type=jnp.float32) + brt_ref[:, 0:1]
    cls, gate_lo, gate_hi = _route(logits_t)
    gates_t = jnp.concatenate(
        [gate_lo, gate_hi, jnp.zeros((LANES - 2, ts), jnp.float32)], axis=0)
    h2r = h2b.astype(jnp.float32)
    for j in range(PACK_SUBLANES):
        hi = _bits(h2r[:, (2 * j) * LANES:(2 * j + 1) * LANES])
        lo = _bits(h2r[:, (2 * j + 1) * LANES:(2 * j + 2) * LANES])
        hrow_ref[pl.ds(j, ts, stride=SUBLANES), :] = hi | lax.shift_right_logical(lo, jnp.uint32(16))
    hrow_ref[pl.ds(PACK_SUBLANES, ts, stride=SUBLANES), :] = _bits(gates_t.T)
    for j in range(PACK_SUBLANES + 1, SUBLANES):
        hrow_ref[pl.ds(j, ts, stride=SUBLANES), :] = jnp.zeros((ts, LANES), jnp.uint32)
    cls_ref[0, :, cols] = cls
    crow = lax.broadcasted_iota(jnp.int32, (LANES, ts), 0)
    onehot = jnp.where(crow == cls, 1.0, 0.0)
    cnt = jnp.sum(onehot, axis=1, keepdims=True)
    earlier = (lax.broadcasted_iota(jnp.int32, (ts, ts), 0)
               < lax.broadcasted_iota(jnp.int32, (ts, ts), 1))
    prefix = _dot(_bf16(onehot), _bf16(jnp.where(earlier, 1.0, 0.0)))
    rank = jnp.sum(onehot * (prefix + cnt_ref[:, 0:1]), axis=0, keepdims=True)
    rank_ref[0, :, cols] = rank.astype(jnp.int32)
    cnt_ref[...] += jnp.broadcast_to(cnt, cnt_ref.shape)


def _mixer_out(t, d, ts):
    n_tiles = t // ts
    out_shape = (
        jax.ShapeDtypeStruct((t, d), jnp.float32),
        jax.ShapeDtypeStruct((t * SUBLANES, LANES), jnp.uint32),
        jax.ShapeDtypeStruct((n_tiles, 1, ts), jnp.int32),
        jax.ShapeDtypeStruct((n_tiles, 1, ts), jnp.int32),
        jax.ShapeDtypeStruct((LANES, LANES), jnp.float32),
    )
    out_specs = (
        pl.BlockSpec((ts, d), lambda i: (i, 0)),
        pl.BlockSpec((ts * SUBLANES, LANES), lambda i: (i, 0)),
        pl.BlockSpec((1, 1, ts), lambda i: (i, 0, 0)),
        pl.BlockSpec((1, 1, ts), lambda i: (i, 0, 0)),
        pl.BlockSpec((LANES, LANES), lambda i: (0, 0)),
    )
    return out_shape, out_specs


def _full(shape):
    return pl.BlockSpec(shape, lambda i: tuple(0 for _ in shape))


def _conf_kernel(x_ref, mod_ref, n1_ref, n2_ref, w1_ref, b1_ref, dw_ref, dwb_ref,
                 lng_ref, lnb_ref, w2_ref, b2_ref, wrt_ref, brt_ref,
                 x1_ref, hrow_ref, cls_ref, rank_ref, cnt_ref, perm, xs, vext, conv):
    i = pl.program_id(0)
    bsz, tm, d = x_ref.shape
    ts = bsz * tm
    n_chunks = d // LANES
    hist = SUBLANES * (CONF_KERNEL - 1)

    @pl.when(i == 0)
    def _():
        vext[:, 0:hist, :] = jnp.zeros((n_chunks, hist, LANES), jnp.float32)
        cnt_ref[...] = jnp.zeros_like(cnt_ref)

    for cb in range(n_chunks):
        for p in range(bsz):
            perm[cb, PERM_PITCH * p:PERM_PITCH * p + tm, :] = x_ref[p, :, cb * LANES:(cb + 1) * LANES]
    for m in range(tm):
        for cb in range(n_chunks):
            xs[m * bsz:(m + 1) * bsz, cb * LANES:(cb + 1) * LANES] = (
                perm.at[cb][pl.ds(m, bsz, stride=PERM_PITCH), :])

    for sub in range(ts // SUB_TILE):
        rows = slice(sub * SUB_TILE, (sub + 1) * SUB_TILE)
        h = _bf16(_rms_mod(xs[rows, :], n1_ref[...], mod_ref[0], mod_ref[1]))
        u = _dot(h, w1_ref[...]) + b1_ref[...]
        v = u[:, 0:d] * jax.nn.sigmoid(u[:, d:2 * d])
        for cb in range(n_chunks):
            vext[cb, hist + sub * SUB_TILE:hist + (sub + 1) * SUB_TILE, :] = v[:, cb * LANES:(cb + 1) * LANES]

    _causal_taps_chunked(dw_ref, dwb_ref, vext, conv, CONF_KERNEL, ts)
    vext[:, 0:hist, :] = vext[:, ts:ts + hist, :]

    for sub in range(ts // SUB_TILE):
        rows = slice(sub * SUB_TILE, (sub + 1) * SUB_TILE)
        y = jnp.concatenate([conv[cb, rows, :] for cb in range(n_chunks)], axis=1)
        mu = jnp.mean(y, axis=-1, keepdims=True)
        yc = y - mu
        var = jnp.mean(yc * yc, axis=-1, keepdims=True)
        z = yc * lax.rsqrt(var + EPS) * lng_ref[...] + lnb_ref[...]
        z = z * jax.nn.sigmoid(z)
        mix = _dot(_bf16(z), w2_ref[...]) + b2_ref[...]
        x1 = xs[rows, :] + _per_batch(mod_ref[2], SUB_TILE) * mix
        x1_ref[rows, :] = x1
        _post_mixer(x1, mod_ref, n2_ref, wrt_ref, brt_ref, hrow_ref, cls_ref, rank_ref, cnt_ref, sub)


def _conf_call(x, mod, n1, n2, w1, b1, dw, dwb, lng, lnb, w2, b2, wrt, brt):
    bsz, seq, d = x.shape
    tm = TIME_TILE
    ts = bsz * tm
    out_shape, out_specs = _mixer_out(bsz * seq, d, ts)
    hist = SUBLANES * (CONF_KERNEL - 1)
    return pl.pallas_call(
        _conf_kernel,
        out_shape=out_shape,
        grid=(seq // tm,),
        in_specs=[
            pl.BlockSpec((bsz, tm, d), lambda i: (0, i, 0)),
            _full((MOD_ROWS, bsz, d)),
            _full((1, d)), _full((1, d)),
            _full((d, 2 * d)), _full((1, 2 * d)),
            _full((d // LANES, CONF_KERNEL + 1, LANES)), _full((d // LANES, 1, LANES)),
            _full((1, d)), _full((1, d)),
            _full((d, d)), _full((1, d)),
            _full((ROUTER_ROWS, d)), _full((ROUTER_ROWS, LANES)),
        ],
        out_specs=out_specs,
        scratch_shapes=[
            pltpu.VMEM((d // LANES, bsz * PERM_PITCH, LANES), jnp.float32),
            pltpu.VMEM((ts, d), jnp.float32),
            pltpu.VMEM((d // LANES, hist + ts, LANES), jnp.float32),
            pltpu.VMEM((d // LANES, ts, LANES), jnp.float32),
        ],
        compiler_params=pltpu.CompilerParams(
            dimension_semantics=("arbitrary",), vmem_limit_bytes=VMEM_LIMIT_BYTES),
        name="conf_mixer",
    )(x, mod, n1, n2, w1, b1, dw, dwb, lng, lnb, w2, b2, wrt, brt)


def _lru_kernel(n_tiles, dest_ref, dnext_ref, xprev_ref, pmod_ref, mod_ref, n1_ref, n2_ref,
                wy_ref, by_ref, wx_ref, bx_ref,
                cw_ref, cb_ref, wa_ref, ba_ref, wi_ref, bi_ref, lam_ref, wo_ref, bo_ref,
                wrt_ref, brt_ref, ys_ref,
                x1_ref, hrow_ref, cls_ref, rank_ref, cnt_ref, uext, hcar, ybuf, ysem):
    i = pl.program_id(0)
    ts, d = xprev_ref.shape
    hd = d // LRU_HEADS
    hist = SUBLANES * (LRU_CONV - 1)

    @pl.when(i == 0)
    def _():
        uext[0:hist, :] = jnp.zeros((hist, d), jnp.float32)
        hcar[...] = jnp.zeros_like(hcar)
        cnt_ref[...] = jnp.zeros_like(cnt_ref)

    y_prev = _gather_expert_tiles(ys_ref, ybuf, ysem, dest_ref, dnext_ref, i, n_tiles, ts)
    nl = -lam_ref[...]
    softplus = jnp.maximum(nl, 0.0) + jnp.log1p(jnp.exp(-jnp.abs(nl)))
    hstate = hcar[...]
    for sub in range(ts // SUB_TILE):
        rows = slice(sub * SUB_TILE, (sub + 1) * SUB_TILE)
        x = xprev_ref[rows, :] + _per_batch(pmod_ref[5], SUB_TILE) * y_prev[rows, :]
        h = _bf16(_rms_mod(x, n1_ref[...], mod_ref[0], mod_ref[1]))
        ygate = jax.nn.gelu(_dot(h, wy_ref[...]) + by_ref[...])
        xb = _dot(h, wx_ref[...]) + bx_ref[...]
        uext[hist + sub * SUB_TILE:hist + (sub + 1) * SUB_TILE, :] = xb
        u = _causal_taps(cw_ref, cb_ref, uext, LRU_CONV, sub * SUB_TILE, SUB_TILE)

        ub = _bf16(u)
        r_parts, i_parts = [], []
        for hh in range(LRU_HEADS):
            uh = ub[:, hh * hd:(hh + 1) * hd]
            r_parts.append(_dot(uh, wa_ref[hh]))
            i_parts.append(_dot(uh, wi_ref[hh]))
        r = jax.nn.sigmoid(jnp.concatenate(r_parts, axis=1) + ba_ref[...])
        ig = jax.nn.sigmoid(jnp.concatenate(i_parts, axis=1) + bi_ref[...])

        log_a = (-LRU_C) * r * softplus
        a = jnp.exp(log_a)
        mult = jnp.sqrt(-jnp.tanh(log_a) * (1.0 + a * a))
        b = mult * (ig * u)

        outs = []
        for m in range(SUB_TILE // SUBLANES):
            sl = slice(m * SUBLANES, (m + 1) * SUBLANES)
            hstate = a[sl] * hstate + b[sl]
            outs.append(hstate)
        hs = jnp.concatenate(outs, axis=0)

        mix = _dot(_bf16(hs * ygate), wo_ref[...]) + bo_ref[...]
        x1 = x + _per_batch(mod_ref[2], SUB_TILE) * mix
        x1_ref[rows, :] = x1
        _post_mixer(x1, mod_ref, n2_ref, wrt_ref, brt_ref, hrow_ref, cls_ref, rank_ref, cnt_ref, sub)
    hcar[...] = hstate
    uext[0:hist, :] = uext[ts:ts + hist, :]


def _lru_call(dest, x_prev, mod_prev, ys, mod, n1, n2, wy, by, wx, bx, cw, cb, wa, ba, wi, bi, lam,
              wo, bo, wrt, brt):
    t, d = x_prev.shape
    n_tiles, _, ts = dest.shape
    bsz = mod.shape[1]
    hd = d // LRU_HEADS
    hist = SUBLANES * (LRU_CONV - 1)
    out_shape, out_specs = _mixer_out(t, d, ts)
    return pl.pallas_call(
        functools.partial(_lru_kernel, n_tiles),
        out_shape=out_shape,
        grid=(n_tiles,),
        in_specs=[
            pl.BlockSpec((1, 1, ts), lambda i: (i, 0, 0), memory_space=pltpu.SMEM),
            pl.BlockSpec((1, 1, ts), lambda i: (jnp.minimum(i + 1, n_tiles - 1), 0, 0),
                         memory_space=pltpu.SMEM),
            pl.BlockSpec((ts, d), lambda i: (i, 0)),
            _full((MOD_ROWS, bsz, d)), _full((MOD_ROWS, bsz, d)),
            _full((1, d)), _full((1, d)),
            _full((d, d)), _full((1, d)),
            _full((d, d)), _full((1, d)),
            _full((LRU_CONV, d)), _full((1, d)),
            _full((LRU_HEADS, hd, hd)), _full((1, d)),
            _full((LRU_HEADS, hd, hd)), _full((1, d)),
            _full((1, d)),
            _full((d, d)), _full((1, d)),
            _full((ROUTER_ROWS, d)), _full((ROUTER_ROWS, LANES)),
            pl.BlockSpec(memory_space=pl.ANY),
        ],
        out_specs=out_specs,
        scratch_shapes=[
            pltpu.VMEM((hist + ts, d), jnp.float32),
            pltpu.VMEM((SUBLANES, d), jnp.float32),
            pltpu.VMEM((2, ts * SUBLANES, LANES), jnp.float32),
            pltpu.SemaphoreType.DMA((2,)),
        ],
        compiler_params=pltpu.CompilerParams(
            dimension_semantics=("arbitrary",), vmem_limit_bytes=VMEM_LIMIT_BYTES),
        name="lru_mixer",
    )(dest, dest, x_prev, mod_prev, mod, n1, n2, wy, by, wx, bx, cw, cb, wa, ba, wi, bi, lam, wo, bo,
      wrt, brt, ys)


def _dispatch_kernel(n_tiles, n_blocks, cls_ref, rank_ref, cnt_ref, clo_ref, chi_ref, hrow_ref, xs_in_ref,
                     xs_ref, dest_ref, blo_ref, bhi_ref, nblk_ref, class_start, stage, sem_in, sem_out):
    del xs_in_ref
    i = pl.program_id(0)
    ts = cls_ref.shape[2]
    rows = ts * SUBLANES

    @pl.when(i == 0)
    def _():
        def per_class(c, nb_total):
            class_start[c] = nb_total * ROW_BLOCK
            nb = lax.shift_right_logical(cnt_ref[c] + (ROW_BLOCK - 1), ROW_BLOCK.bit_length() - 1)

            def fill(k, carry):
                blo_ref[nb_total + k] = clo_ref[c]
                bhi_ref[nb_total + k] = chi_ref[c]
                return carry

            lax.fori_loop(0, nb, fill, 0)
            return nb_total + nb

        nb_used = lax.fori_loop(0, N_CLASSES, per_class, 0)
        nblk_ref[0] = nb_used
        last = jnp.maximum(nb_used - 1, 0)

        def tail(k, carry):
            blo_ref[k] = blo_ref[last]
            bhi_ref[k] = bhi_ref[last]
            return carry

        lax.fori_loop(nb_used, n_blocks, tail, 0)

    def load(tile, slot):
        return pltpu.make_async_copy(hrow_ref.at[pl.ds(pl.multiple_of(tile * rows, rows), rows)],
                                     stage.at[slot], sem_in.at[slot])

    def tile_copy(slot, r, dst_row):
        return pltpu.make_async_copy(
            stage.at[slot, pl.ds(pl.multiple_of(r * SUBLANES, SUBLANES), SUBLANES)],
            xs_ref.at[pl.ds(pl.multiple_of(dst_row * SUBLANES, SUBLANES), SUBLANES)],
            sem_out.at[slot])

    def drain(slot):
        _wait_tile_dmas(lambda r: tile_copy(slot, r, 0), ts)

    slot = i % DISPATCH_SLOTS
    nxt = (i + 1) % DISPATCH_SLOTS

    @pl.when(i == 0)
    def _():
        load(0, 0).start()

    @pl.when(i >= 2)
    def _():
        drain(nxt)

    @pl.when(i + 1 < n_tiles)
    def _():
        load(i + 1, nxt).start()

    load(i, slot).wait()

    def scatter(r):
        dst_row = class_start[cls_ref[0, 0, r]] + rank_ref[0, 0, r]
        dest_ref[0, 0, r] = dst_row
        return tile_copy(slot, r, dst_row)

    _start_tile_dmas(scatter, ts)

    @pl.when(i == n_tiles - 1)
    def _():
        @pl.when(i >= 1)
        def _():
            drain((i + DISPATCH_SLOTS - 1) % DISPATCH_SLOTS)
        drain(slot)


def _dispatch_call(cls, rank, counts, hrows, n_blocks, xs_init):
    n_tiles, _, ts = cls.shape
    n_rows = n_blocks * ROW_BLOCK
    assert xs_init.shape == (n_rows * SUBLANES, LANES) and xs_init.dtype == jnp.uint32
    smem_full = pl.BlockSpec(memory_space=pltpu.SMEM)
    smem_tile = pl.BlockSpec((1, 1, ts), lambda i: (i, 0, 0), memory_space=pltpu.SMEM)
    any_spec = pl.BlockSpec(memory_space=pl.ANY)
    return pl.pallas_call(
        functools.partial(_dispatch_kernel, n_tiles, n_blocks),
        out_shape=(
            jax.ShapeDtypeStruct((n_rows * SUBLANES, LANES), jnp.uint32),
            jax.ShapeDtypeStruct((n_tiles, 1, ts), jnp.int32),
            jax.ShapeDtypeStruct((n_blocks,), jnp.int32),
            jax.ShapeDtypeStruct((n_blocks,), jnp.int32),
            jax.ShapeDtypeStruct((1,), jnp.int32),
        ),
        grid=(n_tiles,),
        in_specs=[
            smem_tile, smem_tile,
            smem_full, smem_full, smem_full, any_spec, any_spec,
        ],
        out_specs=(
            any_spec,
            smem_tile,
            smem_full, smem_full, smem_full,
        ),
        scratch_shapes=[
            pltpu.SMEM((LANES,), jnp.int32),
            pltpu.VMEM((DISPATCH_SLOTS, ts * SUBLANES, LANES), jnp.uint32),
            pltpu.SemaphoreType.DMA((DISPATCH_SLOTS,)),
            pltpu.SemaphoreType.DMA((DISPATCH_SLOTS,)),
        ],
        input_output_aliases={6: 0},
        compiler_params=pltpu.CompilerParams(
            dimension_semantics=("arbitrary",), vmem_limit_bytes=VMEM_LIMIT_BYTES),
        name="moe_dispatch",
    )(cls, rank, counts, jnp.asarray(_CLASS_LO), jnp.asarray(_CLASS_HI), hrows, xs_init)


def _expert_kernel(blo_ref, bhi_ref, nblk_ref, xs_ref, wg_lo, wu_lo, wd_lo, wg_hi, wu_hi, wd_hi, ys_ref):
    del blo_ref, bhi_ref
    b = pl.program_id(0)
    rb = ROW_BLOCK

    @pl.when(b < nblk_ref[0])
    def _():
        pieces = []
        for j in range(PACK_SUBLANES):
            w = _token_rows(xs_ref, j, rb)
            pieces.append(_bf16(_f32_from_bits(w & jnp.uint32(0xFFFF0000))))
            pieces.append(_bf16(_f32_from_bits(lax.shift_left(w, jnp.uint32(16)))))
        xb = jnp.concatenate(pieces, axis=1)
        gates = _f32_from_bits(_token_rows(xs_ref, PACK_SUBLANES, rb))
        acc = None
        for idx, (wg, wu, wd) in enumerate(((wg_lo, wu_lo, wd_lo), (wg_hi, wu_hi, wd_hi))):
            a = _dot(xb, wg[...])
            hid = (a * jax.nn.sigmoid(a)) * _dot(xb, wu[...]) * gates[:, idx:idx + 1]
            part = _dot(_bf16(hid), wd[...])
            acc = part if acc is None else acc + part
        for s in range(SUBLANES):
            ys_ref[pl.ds(s, rb, stride=SUBLANES), :] = acc[:, s * LANES:(s + 1) * LANES]

    @pl.when(b >= nblk_ref[0])
    def _():
        ys_ref[...] = jnp.zeros_like(ys_ref)


def _expert_call(blo, bhi, nblk, xs, wg, wu, wd):
    n_rows = xs.shape[0] // SUBLANES
    d, de = wg.shape[1], wg.shape[2]
    assert d == SUBLANES * LANES
    n_blocks = n_rows // ROW_BLOCK

    def w_spec(shape, which):
        if which == 0:
            return pl.BlockSpec((None,) + shape, lambda b, lo, hi, n: (lo[b], 0, 0))
        return pl.BlockSpec((None,) + shape, lambda b, lo, hi, n: (hi[b], 0, 0))

    def x_map(b, lo, hi, n):
        return (jnp.minimum(b, jnp.maximum(n[0] - 1, 0)), 0)

    grid_spec = pltpu.PrefetchScalarGridSpec(
        num_scalar_prefetch=3,
        grid=(n_blocks,),
        in_specs=[
            pl.BlockSpec((ROW_BLOCK * SUBLANES, LANES), x_map),
            w_spec((d, de), 0), w_spec((d, de), 0), w_spec((de, d), 0),
            w_spec((d, de), 1), w_spec((d, de), 1), w_spec((de, d), 1),
        ],
        out_specs=pl.BlockSpec((ROW_BLOCK * SUBLANES, LANES), lambda b, lo, hi, n: (b, 0)),
    )
    return pl.pallas_call(
        _expert_kernel,
        out_shape=jax.ShapeDtypeStruct((n_rows * SUBLANES, LANES), jnp.float32),
        grid_spec=grid_spec,
        compiler_params=pltpu.CompilerParams(
            dimension_semantics=("arbitrary",), vmem_limit_bytes=VMEM_LIMIT_BYTES),
        name="moe_experts",
    )(blo, bhi, nblk, xs, wg, wu, wd, wg, wu, wd)


def _final_kernel(n_tiles, dest_ref, dnext_ref, x1_ref, mod_ref, fmod_ref, fg_ref, ys_ref,
                  out_ref, ybuf, sem, unperm):
    i = pl.program_id(0)
    ts, d = x1_ref.shape
    bsz, tm, _ = out_ref.shape
    y = _gather_expert_tiles(ys_ref, ybuf, sem, dest_ref, dnext_ref, i, n_tiles, ts)
    x2 = x1_ref[...] + _per_batch(mod_ref[5], ts) * y
    res = _rms_mod(x2, fg_ref[...], fmod_ref[0], fmod_ref[1])
    for cb in range(d // LANES):
        unperm[cb] = res[:, cb * LANES:(cb + 1) * LANES]
    for cb in range(d // LANES):
        for p in range(bsz):
            out_ref[p, :, cb * LANES:(cb + 1) * LANES] = unperm.at[cb][pl.ds(p, tm, stride=SUBLANES), :]


def _final_call(dest, x1, mod, ys, bsz, fmod, fin_g):
    t, d = x1.shape
    n_tiles, _, ts = dest.shape
    tm = ts // bsz
    return pl.pallas_call(
        functools.partial(_final_kernel, n_tiles),
        out_shape=jax.ShapeDtypeStruct((bsz, t // bsz, d), jnp.float32),
        grid=(n_tiles,),
        in_specs=[
            pl.BlockSpec((1, 1, ts), lambda i: (i, 0, 0), memory_space=pltpu.SMEM),
            pl.BlockSpec((1, 1, ts), lambda i: (jnp.minimum(i + 1, n_tiles - 1), 0, 0),
                         memory_space=pltpu.SMEM),
            pl.BlockSpec((ts, d), lambda i: (i, 0)),
            _full((MOD_ROWS, bsz, d)), _full((MOD_ROWS, bsz, d)), _full((1, d)),
            pl.BlockSpec(memory_space=pl.ANY),
        ],
        out_specs=pl.BlockSpec((bsz, tm, d), lambda i: (0, i, 0)),
        scratch_shapes=[pltpu.VMEM((2, ts * SUBLANES, LANES), jnp.float32),
                        pltpu.SemaphoreType.DMA((2,)),
                        pltpu.VMEM((d // LANES, ts, LANES), jnp.float32)],
        compiler_params=pltpu.CompilerParams(
            dimension_semantics=("arbitrary",), vmem_limit_bytes=VMEM_LIMIT_BYTES),
        name="moe_combine_final",
    )(dest, dest, x1, mod, fmod, fin_g, ys)


def _router_operands(wg, bg, wr, br):
    d = wg.shape[0]
    pad_g = jnp.zeros((SUBLANES - N_GROUPS, d), jnp.float32)
    pad_e = jnp.zeros((ROUTER_ROWS - SUBLANES - N_EXPERTS, d), jnp.float32)
    wrt = jnp.concatenate([wg.T, pad_g, wr.T, pad_e], axis=0)
    bcol = jnp.concatenate([bg, jnp.zeros((SUBLANES - N_GROUPS,), jnp.float32), br,
                            jnp.zeros((ROUTER_ROWS - SUBLANES - N_EXPERTS,), jnp.float32)])
    brt = jnp.broadcast_to(bcol[:, None], (ROUTER_ROWS, LANES))
    return _bf16(wrt), brt


def _mod_table(m, parts):
    bsz = m.shape[0]
    d = m.shape[1] // parts
    m = jnp.transpose(m.reshape(bsz, parts, d), (1, 0, 2))
    return jnp.concatenate([m, jnp.zeros((MOD_ROWS - parts, bsz, d), jnp.float32)], axis=0)


def _row(v):
    return v.reshape(1, -1)


def _moe_experts(hrows, cls, rank, cnt, wg, wu, wd, n_blocks, xs_init):
    counts = cnt[:, 0].astype(jnp.int32)
    xs, dest, blo, bhi, nblk = _dispatch_call(cls, rank, counts, hrows, n_blocks, xs_init)
    ys = _expert_call(blo, bhi, nblk, xs, _bf16(wg), _bf16(wu), _bf16(wd))
    return xs, ys, dest


def kernel(x, c, ada_w, ada_b, norm1_g, norm2_g, cm_w1, cm_b1, cm_dw, cm_dwb, cm_ln_g, cm_ln_b, cm_w2, cm_b2, rg_wy, rg_by, rg_wx, rg_bx, rg_cw, rg_cb, rg_wa, rg_ba, rg_wi, rg_bi, rg_lambda, rg_wo, rg_bo, moe_wg, moe_bg, moe_wr, moe_br, moe_w_gate, moe_w_up, moe_w_down, fin_ada_w, fin_ada_b, fin_g):
    bsz, seq, d = x.shape
    t = bsz * seq
    depth = ada_w.shape[0]
    assert depth == 2 and bsz == SUBLANES and seq % TIME_TILE == 0 and t % ROW_BLOCK == 0
    assert d == 2 * PACK_SUBLANES * LANES

    mods = _ada_call(c, ada_w, ada_b[:, None, :])
    fmods = _ada_call(c, fin_ada_w[None], fin_ada_b[None, None, :])
    fmod = _mod_table(fmods[0], 2)
    n_blocks = t // ROW_BLOCK + N_CLASSES
    mod0, mod1 = _mod_table(mods[0], 6), _mod_table(mods[1], 6)

    wrt, brt = _router_operands(moe_wg[0], moe_bg[0], moe_wr[0], moe_br[0])
    dw = jnp.concatenate([cm_dw[0], jnp.zeros((1, d), jnp.float32)], axis=0)
    dw = jnp.transpose(dw.reshape(CONF_KERNEL + 1, d // LANES, LANES), (1, 0, 2))
    x1, hrows, cls, rank, cnt = _conf_call(
        x, mod0, _row(norm1_g[0]), _row(norm2_g[0]), _bf16(cm_w1[0]), _row(cm_b1[0]),
        dw, cm_dwb[0].reshape(d // LANES, 1, LANES), _row(cm_ln_g[0]), _row(cm_ln_b[0]),
        _bf16(cm_w2[0]), _row(cm_b2[0]), wrt, brt)
    xs_init = jnp.zeros((n_blocks * ROW_BLOCK * SUBLANES, LANES), jnp.uint32)
    xs, ys, dest = _moe_experts(hrows, cls, rank, cnt, moe_w_gate[0], moe_w_up[0], moe_w_down[0],
                                n_blocks, xs_init)

    wrt, brt = _router_operands(moe_wg[1], moe_bg[1], moe_wr[1], moe_br[1])
    x1, hrows, cls, rank, cnt = _lru_call(
        dest, x1, mod0, ys, mod1, _row(norm1_g[1]), _row(norm2_g[1]),
        _bf16(rg_wy[0]), _row(rg_by[0]), _bf16(rg_wx[0]), _row(rg_bx[0]),
        rg_cw[0], _row(rg_cb[0]), _bf16(rg_wa[0]), _row(rg_ba[0]), _bf16(rg_wi[0]), _row(rg_bi[0]),
        _row(rg_lambda[0]), _bf16(rg_wo[0]), _row(rg_bo[0]), wrt, brt)
    _, ys, dest = _moe_experts(hrows, cls, rank, cnt, moe_w_gate[1], moe_w_up[1], moe_w_down[1],
                               n_blocks, xs)
    return _final_call(dest, x1, mod1, ys, bsz, fmod, _row(fin_g))
```

```python
import functools

import jax
import jax.numpy as jnp
import numpy as np
from jax import lax
from jax.experimental import pallas as pl
from jax.experimental.pallas import tpu as pltpu

EPS = 1e-6
CONF_KERNEL = 31
LRU_HEADS = 4
LRU_CONV = 4
LRU_C = 8.0
N_GROUPS = 4
EXPERTS_PER_GROUP = 8
N_EXPERTS = N_GROUPS * EXPERTS_PER_GROUP
PAIRS_PER_GROUP = EXPERTS_PER_GROUP * (EXPERTS_PER_GROUP - 1) // 2
N_CLASSES = N_GROUPS * PAIRS_PER_GROUP

LANES = 128
SUBLANES = 8
VMEM_LIMIT_BYTES = 56 * 1024 * 1024

TOKEN_TILE = 512
TIME_TILE = TOKEN_TILE // SUBLANES
SUB_TILE = 256
ROW_BLOCK = 256
PERM_PITCH = TIME_TILE + SUBLANES
CONV_ROWS = 128
ROUTER_ROWS = 64
PACK_SUBLANES = 4
MOD_ROWS = 8
DISPATCH_SLOTS = 3
DMA_UNROLL = 8

_NEG_INF = float("-inf")


def _class_tables():
    lo = np.zeros((N_CLASSES,), np.int32)
    hi = np.zeros((N_CLASSES,), np.int32)
    for g in range(N_GROUPS):
        for a in range(EXPERTS_PER_GROUP):
            for b in range(a + 1, EXPERTS_PER_GROUP):
                c = g * PAIRS_PER_GROUP + (a * (2 * EXPERTS_PER_GROUP - 1 - a)) // 2 + (b - a - 1)
                lo[c] = g * EXPERTS_PER_GROUP + a
                hi[c] = g * EXPERTS_PER_GROUP + b
    return lo, hi


_CLASS_LO, _CLASS_HI = _class_tables()


def _bf16(x):
    return x.astype(jnp.bfloat16)


def _dot(a, b):
    return jnp.dot(a, b, preferred_element_type=jnp.float32)


def _bits(x):
    return lax.bitcast_convert_type(x, jnp.uint32)


def _f32_from_bits(x):
    return lax.bitcast_convert_type(x, jnp.float32)


def _token_rows(ref, sublane, n):
    return ref[pl.ds(sublane, n, stride=SUBLANES), :]


def _per_batch(v, ts):
    return jnp.tile(v, (ts // SUBLANES, 1))


def _ada_kernel(c_ref, w_ref, b_ref, o_ref):
    c = c_ref[...]
    ca = _bf16(c * jax.nn.sigmoid(c))
    o_ref[...] = _dot(ca, _bf16(w_ref[...])) + b_ref[...]


def _ada_call(c, w, b):
    n_layers, d, n = w.shape
    bsz = c.shape[0]
    nb = 1024
    return pl.pallas_call(
        _ada_kernel,
        out_shape=jax.ShapeDtypeStruct((n_layers, bsz, n), jnp.float32),
        grid=(n_layers, n // nb),
        in_specs=[
            pl.BlockSpec((bsz, d), lambda l, j: (0, 0)),
            pl.BlockSpec((None, d, nb), lambda l, j: (l, 0, j)),
            pl.BlockSpec((None, 1, nb), lambda l, j: (l, 0, j)),
        ],
        out_specs=pl.BlockSpec((None, bsz, nb), lambda l, j: (l, 0, j)),
        compiler_params=pltpu.CompilerParams(
            dimension_semantics=("arbitrary", "arbitrary"),
            vmem_limit_bytes=VMEM_LIMIT_BYTES),
        name="ada_mod",
    )(c, w, b)


def _rms_mod(x, gain, shift, scale):
    ts = x.shape[0]
    ms = jnp.mean(x * x, axis=-1, keepdims=True)
    return (x * lax.rsqrt(ms + EPS)) * _per_batch(gain * (1.0 + scale), ts) + _per_batch(shift, ts)


def _causal_taps(w_ref, bias_ref, ext_ref, n_taps, row0, n):
    d = ext_ref.shape[1]
    cols = []
    for cb in range(d // LANES):
        sl = slice(cb * LANES, (cb + 1) * LANES)
        acc = jnp.broadcast_to(bias_ref[:, sl], (n, LANES))
        for k in range(n_taps):
            lo = row0 + SUBLANES * k
            acc = acc + w_ref[k:k + 1, sl] * ext_ref[lo:lo + n, sl]
        cols.append(acc)
    return jnp.concatenate(cols, axis=1)


def _causal_taps_chunked(w_ref, bias_ref, ext_ref, out_ref, n_taps, ts):
    n_chunks = ext_ref.shape[0]
    units_per_chunk = ts // CONV_ROWS

    def unit(idx, carry):
        cb = idx // units_per_chunk
        r0 = pl.multiple_of((idx % units_per_chunk) * CONV_ROWS, CONV_ROWS)
        w_cb, ext_cb = w_ref.at[cb], ext_ref.at[cb]
        acc = jnp.broadcast_to(bias_ref[cb], (CONV_ROWS, LANES))
        for k in range(n_taps):
            win = ext_cb[pl.ds(pl.multiple_of(r0 + SUBLANES * k, SUBLANES), CONV_ROWS), :]
            acc = acc + w_cb[k:k + 1, :] * win
        out_ref.at[cb][pl.ds(r0, CONV_ROWS), :] = acc
        return carry

    lax.fori_loop(0, n_chunks * units_per_chunk, unit, 0)


def _start_tile_dmas(make_copy, n):
    def body(r, carry):
        make_copy(r).start()
        return carry
    lax.fori_loop(0, n, body, 0, unroll=DMA_UNROLL)


def _wait_tile_dmas(make_copy, n):
    def body(r, carry):
        make_copy(r).wait()
        return carry
    lax.fori_loop(0, n, body, 0, unroll=DMA_UNROLL)


def _gather_expert_tiles(ys_ref, ybuf, sem, dest_ref, dnext_ref, step, n_steps, ts):
    slot = step % 2

    def tile_copy(dref, sl):
        def make(r):
            src = pl.multiple_of(dref[0, 0, r] * SUBLANES, SUBLANES)
            return pltpu.make_async_copy(
                ys_ref.at[pl.ds(src, SUBLANES)],
                ybuf.at[sl, pl.ds(pl.multiple_of(r * SUBLANES, SUBLANES), SUBLANES)],
                sem.at[sl])
        return make

    @pl.when(step == 0)
    def _():
        _start_tile_dmas(tile_copy(dest_ref, 0), ts)

    @pl.when(step + 1 < n_steps)
    def _():
        _start_tile_dmas(tile_copy(dnext_ref, 1 - slot), ts)

    _wait_tile_dmas(tile_copy(dest_ref, slot), ts)
    yv = ybuf.at[slot]
    return jnp.concatenate([_token_rows(yv, s, ts) for s in range(SUBLANES)], axis=1)


def _route(logits_t):
    ts = logits_t.shape[1]
    row = lax.broadcasted_iota(jnp.int32, (SUBLANES, ts), 0)
    gl = jnp.where(row < N_GROUPS, logits_t[0:SUBLANES], _NEG_INF)
    gmax = jnp.max(gl, axis=0, keepdims=True)
    gsel = jnp.min(jnp.where(gl == gmax, row, SUBLANES), axis=0, keepdims=True)
    p_g = 1.0 / jnp.sum(jnp.exp(gl - gmax), axis=0, keepdims=True)
    es = jnp.zeros((SUBLANES, ts), jnp.float32)
    for g in range(N_GROUPS):
        lo = SUBLANES * (1 + g)
        es = jnp.where(gsel == g, logits_t[lo:lo + EXPERTS_PER_GROUP], es)
    m1 = jnp.max(es, axis=0, keepdims=True)
    i1 = jnp.min(jnp.where(es == m1, row, SUBLANES), axis=0, keepdims=True)
    es2 = jnp.where(row == i1, _NEG_INF, es)
    m2 = jnp.max(es2, axis=0, keepdims=True)
    i2 = jnp.min(jnp.where(es2 == m2, row, SUBLANES), axis=0, keepdims=True)
    t = jnp.exp(m2 - m1)
    w1 = p_g / (1.0 + t)
    w2 = w1 * t
    first_lo = i1 < i2
    e_lo = jnp.minimum(i1, i2)
    e_hi = jnp.maximum(i1, i2)
    gate_lo = jnp.where(first_lo, w1, w2)
    gate_hi = jnp.where(first_lo, w2, w1)
    pair = lax.shift_right_logical(e_lo * (2 * EXPERTS_PER_GROUP - 1 - e_lo), 1) + (e_hi - e_lo - 1)
    cls = gsel * PAIRS_PER_GROUP + pair
    return cls, gate_lo, gate_hi


def _post_mixer(x1, mod_ref, n2_ref, wrt_ref, brt_ref, hrow_ref, cls_ref, rank_ref, cnt_ref, sub):
    ts = x1.shape[0]
    hrow_ref = hrow_ref.at[pl.ds(sub * ts * SUBLANES, ts * SUBLANES)]
    cols = slice(sub * ts, (sub + 1) * ts)
    h2 = _rms_mod(x1, n2_ref[...], mod_ref[3], mod_ref[4])
    h2b = _bf16(h2)
    logits_t = lax.dot_general(wrt_ref[...], h2b, (((1,), (1,)), ((), ())),
                               preferred_element_type=jnp.float32) + brt_ref[:, 0:1]
    cls, gate_lo, gate_hi = _route(logits_t)
    gates_t = jnp.concatenate(
        [gate_lo, gate_hi, jnp.zeros((LANES - 2, ts), jnp.float32)], axis=0)
    h2r = h2b.astype(jnp.float32)
    for j in range(PACK_SUBLANES):
        hi = _bits(h2r[:, (2 * j) * LANES:(2 * j + 1) * LANES])
        lo = _bits(h2r[:, (2 * j + 1) * LANES:(2 * j + 2) * LANES])
        hrow_ref[pl.ds(j, ts, stride=SUBLANES), :] = hi | lax.shift_right_logical(lo, jnp.uint32(16))
    hrow_ref[pl.ds(PACK_SUBLANES, ts, stride=SUBLANES), :] = _bits(gates_t.T)
    for j in range(PACK_SUBLANES + 1, SUBLANES):
        hrow_ref[pl.ds(j, ts, stride=SUBLANES), :] = jnp.zeros((ts, LANES), jnp.uint32)
    cls_ref[0, :, cols] = cls
    crow = lax.broadcasted_iota(jnp.int32, (LANES, ts), 0)
    onehot = jnp.where(crow == cls, 1.0, 0.0)
    cnt = jnp.sum(onehot, axis=1, keepdims=True)
    earlier = (lax.broadcasted_iota(jnp.int32, (ts, ts), 0)
               < lax.broadcasted_iota(jnp.int32, (ts, ts), 1))
    prefix = _dot(_bf16(onehot), _bf16(jnp.where(earlier, 1.0, 0.0)))
    rank = jnp.sum(onehot * (prefix + cnt_ref[:, 0:1]), axis=0, keepdims=True)
    rank_ref[0, :, cols] = rank.astype(jnp.int32)
    cnt_ref[...] += jnp.broadcast_to(cnt, cnt_ref.shape)


def _mixer_out(t, d, ts):
    n_tiles = t // ts
    out_shape = (
        jax.ShapeDtypeStruct((t, d), jnp.float32),
        jax.ShapeDtypeStruct((t * SUBLANES, LANES), jnp.uint32),
        jax.ShapeDtypeStruct((n_tiles, 1, ts), jnp.int32),
        jax.ShapeDtypeStruct((n_tiles, 1, ts), jnp.int32),
        jax.ShapeDtypeStruct((LANES, LANES), jnp.float32),
    )
    out_specs = (
        pl.BlockSpec((ts, d), lambda i: (i, 0)),
        pl.BlockSpec((ts * SUBLANES, LANES), lambda i: (i, 0)),
        pl.BlockSpec((1, 1, ts), lambda i: (i, 0, 0)),
        pl.BlockSpec((1, 1, ts), lambda i: (i, 0, 0)),
        pl.BlockSpec((LANES, LANES), lambda i: (0, 0)),
    )
    return out_shape, out_specs


def _full(shape):
    return pl.BlockSpec(shape, lambda i: tuple(0 for _ in shape))


def _conf_kernel(x_ref, mod_ref, n1_ref, n2_ref, w1_ref, b1_ref, dw_ref, dwb_ref,
                 lng_ref, lnb_ref, w2_ref, b2_ref, wrt_ref, brt_ref,
                 x1_ref, hrow_ref, cls_ref, rank_ref, cnt_ref, perm, xs, vext, conv):
    i = pl.program_id(0)
    bsz, tm, d = x_ref.shape
    ts = bsz * tm
    n_chunks = d // LANES
    hist = SUBLANES * (CONF_KERNEL - 1)

    @pl.when(i == 0)
    def _():
        vext[:, 0:hist, :] = jnp.zeros((n_chunks, hist, LANES), jnp.float32)
        cnt_ref[...] = jnp.zeros_like(cnt_ref)

    for cb in range(n_chunks):
        for p in range(bsz):
            perm[cb, PERM_PITCH * p:PERM_PITCH * p + tm, :] = x_ref[p, :, cb * LANES:(cb + 1) * LANES]
    for m in range(tm):
        for cb in range(n_chunks):
            xs[m * bsz:(m + 1) * bsz, cb * LANES:(cb + 1) * LANES] = (
                perm.at[cb][pl.ds(m, bsz, stride=PERM_PITCH), :])

    for sub in range(ts // SUB_TILE):
        rows = slice(sub * SUB_TILE, (sub + 1) * SUB_TILE)
        h = _bf16(_rms_mod(xs[rows, :], n1_ref[...], mod_ref[0], mod_ref[1]))
        u = _dot(h, w1_ref[...]) + b1_ref[...]
        v = u[:, 0:d] * jax.nn.sigmoid(u[:, d:2 * d])
        for cb in range(n_chunks):
            vext[cb, hist + sub * SUB_TILE:hist + (sub + 1) * SUB_TILE, :] = v[:, cb * LANES:(cb + 1) * LANES]

    _causal_taps_chunked(dw_ref, dwb_ref, vext, conv, CONF_KERNEL, ts)
    vext[:, 0:hist, :] = vext[:, ts:ts + hist, :]

    for sub in range(ts // SUB_TILE):
        rows = slice(sub * SUB_TILE, (sub + 1) * SUB_TILE)
        y = jnp.concatenate([conv[cb, rows, :] for cb in range(n_chunks)], axis=1)
        mu = jnp.mean(y, axis=-1, keepdims=True)
        yc = y - mu
        var = jnp.mean(yc * yc, axis=-1, keepdims=True)
        z = yc * lax.rsqrt(var + EPS) * lng_ref[...] + lnb_ref[...]
        z = z * jax.nn.sigmoid(z)
        mix = _dot(_bf16(z), w2_ref[...]) + b2_ref[...]
        x1 = xs[rows, :] + _per_batch(mod_ref[2], SUB_TILE) * mix
        x1_ref[rows, :] = x1
        _post_mixer(x1, mod_ref, n2_ref, wrt_ref, brt_ref, hrow_ref, cls_ref, rank_ref, cnt_ref, sub)


def _conf_call(x, mod, n1, n2, w1, b1, dw, dwb, lng, lnb, w2, b2, wrt, brt):
    bsz, seq, d = x.shape
    tm = TIME_TILE
    ts = bsz * tm
    out_shape, out_specs = _mixer_out(bsz * seq, d, ts)
    hist = SUBLANES * (CONF_KERNEL - 1)
    return pl.pallas_call(
        _conf_kernel,
        out_shape=out_shape,
        grid=(seq // tm,),
        in_specs=[
            pl.BlockSpec((bsz, tm, d), lambda i: (0, i, 0)),
            _full((MOD_ROWS, bsz, d)),
            _full((1, d)), _full((1, d)),
            _full((d, 2 * d)), _full((1, 2 * d)),
            _full((d // LANES, CONF_KERNEL + 1, LANES)), _full((d // LANES, 1, LANES)),
            _full((1, d)), _full((1, d)),
            _full((d, d)), _full((1, d)),
            _full((ROUTER_ROWS, d)), _full((ROUTER_ROWS, LANES)),
        ],
        out_specs=out_specs,
        scratch_shapes=[
            pltpu.VMEM((d // LANES, bsz * PERM_PITCH, LANES), jnp.float32),
            pltpu.VMEM((ts, d), jnp.float32),
            pltpu.VMEM((d // LANES, hist + ts, LANES), jnp.float32),
            pltpu.VMEM((d // LANES, ts, LANES), jnp.float32),
        ],
        compiler_params=pltpu.CompilerParams(
            dimension_semantics=("arbitrary",), vmem_limit_bytes=VMEM_LIMIT_BYTES),
        name="conf_mixer",
    )(x, mod, n1, n2, w1, b1, dw, dwb, lng, lnb, w2, b2, wrt, brt)


def _lru_kernel(n_tiles, dest_ref, dnext_ref, xprev_ref, pmod_ref, mod_ref, n1_ref, n2_ref,
                wy_ref, by_ref, wx_ref, bx_ref,
                cw_ref, cb_ref, wa_ref, ba_ref, wi_ref, bi_ref, lam_ref, wo_ref, bo_ref,
                wrt_ref, brt_ref, ys_ref,
                x1_ref, hrow_ref, cls_ref, rank_ref, cnt_ref, uext, hcar, ybuf, ysem):
    i = pl.program_id(0)
    ts, d = xprev_ref.shape
    hd = d // LRU_HEADS
    hist = SUBLANES * (LRU_CONV - 1)

    @pl.when(i == 0)
    def _():
        uext[0:hist, :] = jnp.zeros((hist, d), jnp.float32)
        hcar[...] = jnp.zeros_like(hcar)
        cnt_ref[...] = jnp.zeros_like(cnt_ref)

    y_prev = _gather_expert_tiles(ys_ref, ybuf, ysem, dest_ref, dnext_ref, i, n_tiles, ts)
    nl = -lam_ref[...]
    softplus = jnp.maximum(nl, 0.0) + jnp.log1p(jnp.exp(-jnp.abs(nl)))
    hstate = hcar[...]
    for sub in range(ts // SUB_TILE):
        rows = slice(sub * SUB_TILE, (sub + 1) * SUB_TILE)
        x = xprev_ref[rows, :] + _per_batch(pmod_ref[5], SUB_TILE) * y_prev[rows, :]
        h = _bf16(_rms_mod(x, n1_ref[...], mod_ref[0], mod_ref[1]))
        ygate = jax.nn.gelu(_dot(h, wy_ref[...]) + by_ref[...])
        xb = _dot(h, wx_ref[...]) + bx_ref[...]
        uext[hist + sub * SUB_TILE:hist + (sub + 1) * SUB_TILE, :] = xb
        u = _causal_taps(cw_ref, cb_ref, uext, LRU_CONV, sub * SUB_TILE, SUB_TILE)

        ub = _bf16(u)
        r_parts, i_parts = [], []
        for hh in range(LRU_HEADS):
            uh = ub[:, hh * hd:(hh + 1) * hd]
            r_parts.append(_dot(uh, wa_ref[hh]))
            i_parts.append(_dot(uh, wi_ref[hh]))
        r = jax.nn.sigmoid(jnp.concatenate(r_parts, axis=1) + ba_ref[...])
        ig = jax.nn.sigmoid(jnp.concatenate(i_parts, axis=1) + bi_ref[...])

        log_a = (-LRU_C) * r * softplus
        a = jnp.exp(log_a)
        mult = jnp.sqrt(-jnp.tanh(log_a) * (1.0 + a * a))
        b = mult * (ig * u)

        outs = []
        for m in range(SUB_TILE // SUBLANES):
            sl = slice(m * SUBLANES, (m + 1) * SUBLANES)
            hstate = a[sl] * hstate + b[sl]
            outs.append(hstate)
        hs = jnp.concatenate(outs, axis=0)

        mix = _dot(_bf16(hs * ygate), wo_ref[...]) + bo_ref[...]
        x1 = x + _per_batch(mod_ref[2], SUB_TILE) * mix
        x1_ref[rows, :] = x1
        _post_mixer(x1, mod_ref, n2_ref, wrt_ref, brt_ref, hrow_ref, cls_ref, rank_ref, cnt_ref, sub)
    hcar[...] = hstate
    uext[0:hist, :] = uext[ts:ts + hist, :]


def _lru_call(dest, x_prev, mod_prev, ys, mod, n1, n2, wy, by, wx, bx, cw, cb, wa, ba, wi, bi, lam,
              wo, bo, wrt, brt):
    t, d = x_prev.shape
    n_tiles, _, ts = dest.shape
    bsz = mod.shape[1]
    hd = d // LRU_HEADS
    hist = SUBLANES * (LRU_CONV - 1)
    out_shape, out_specs = _mixer_out(t, d, ts)
    return pl.pallas_call(
        functools.partial(_lru_kernel, n_tiles),
        out_shape=out_shape,
        grid=(n_tiles,),
        in_specs=[
            pl.BlockSpec((1, 1, ts), lambda i: (i, 0, 0), memory_space=pltpu.SMEM),
            pl.BlockSpec((1, 1, ts), lambda i: (jnp.minimum(i + 1, n_tiles - 1), 0, 0),
                         memory_space=pltpu.SMEM),
            pl.BlockSpec((ts, d), lambda i: (i, 0)),
            _full((MOD_ROWS, bsz, d)), _full((MOD_ROWS, bsz, d)),
            _full((1, d)), _full((1, d)),
            _full((d, d)), _full((1, d)),
            _full((d, d)), _full((1, d)),
            _full((LRU_CONV, d)), _full((1, d)),
            _full((LRU_HEADS, hd, hd)), _full((1, d)),
            _full((LRU_HEADS, hd, hd)), _full((1, d)),
            _full((1, d)),
            _full((d, d)), _full((1, d)),
            _full((ROUTER_ROWS, d)), _full((ROUTER_ROWS, LANES)),
            pl.BlockSpec(memory_space=pl.ANY),
        ],
        out_specs=out_specs,
        scratch_shapes=[
            pltpu.VMEM((hist + ts, d), jnp.float32),
            pltpu.VMEM((SUBLANES, d), jnp.float32),
            pltpu.VMEM((2, ts * SUBLANES, LANES), jnp.float32),
            pltpu.SemaphoreType.DMA((2,)),
        ],
        compiler_params=pltpu.CompilerParams(
            dimension_semantics=("arbitrary",), vmem_limit_bytes=VMEM_LIMIT_BYTES),
        name="lru_mixer",
    )(dest, dest, x_prev, mod_prev, mod, n1, n2, wy, by, wx, bx, cw, cb, wa, ba, wi, bi, lam, wo, bo,
      wrt, brt, ys)


def _dispatch_kernel(n_tiles, n_blocks, cls_ref, rank_ref, cnt_ref, clo_ref, chi_ref, hrow_ref, xs_in_ref,
                     xs_ref, dest_ref, blo_ref, bhi_ref, nblk_ref, class_start, stage, sem_in, sem_out):
    del xs_in_ref
    i = pl.program_id(0)
    ts = cls_ref.shape[2]
    rows = ts * SUBLANES

    @pl.when(i == 0)
    def _():
        def per_class(c, nb_total):
            class_start[c] = nb_total * ROW_BLOCK
            nb = lax.shift_right_logical(cnt_ref[c] + (ROW_BLOCK - 1), ROW_BLOCK.bit_length() - 1)

            def fill(k, carry):
                blo_ref[nb_total + k] = clo_ref[c]
                bhi_ref[nb_total + k] = chi_ref[c]
                return carry

            lax.fori_loop(0, nb, fill, 0)
            return nb_total + nb

        nb_used = lax.fori_loop(0, N_CLASSES, per_class, 0)
        nblk_ref[0] = nb_used
        last = jnp.maximum(nb_used - 1, 0)

        def tail(k, carry):
            blo_ref[k] = blo_ref[last]
            bhi_ref[k] = bhi_ref[last]
            return carry

        lax.fori_loop(nb_used, n_blocks, tail, 0)

    def load(tile, slot):
        return pltpu.make_async_copy(hrow_ref.at[pl.ds(pl.multiple_of(tile * rows, rows), rows)],
                                     stage.at[slot], sem_in.at[slot])

    def tile_copy(slot, r, dst_row):
        return pltpu.make_async_copy(
            stage.at[slot, pl.ds(pl.multiple_of(r * SUBLANES, SUBLANES), SUBLANES)],
            xs_ref.at[pl.ds(pl.multiple_of(dst_row * SUBLANES, SUBLANES), SUBLANES)],
            sem_out.at[slot])

    def drain(slot):
        _wait_tile_dmas(lambda r: tile_copy(slot, r, 0), ts)

    slot = i % DISPATCH_SLOTS
    nxt = (i + 1) % DISPATCH_SLOTS

    @pl.when(i == 0)
    def _():
        load(0, 0).start()

    @pl.when(i >= 2)
    def _():
        drain(nxt)

    @pl.when(i + 1 < n_tiles)
    def _():
        load(i + 1, nxt).start()

    load(i, slot).wait()

    def scatter(r):
        dst_row = class_start[cls_ref[0, 0, r]] + rank_ref[0, 0, r]
        dest_ref[0, 0, r] = dst_row
        return tile_copy(slot, r, dst_row)

    _start_tile_dmas(scatter, ts)

    @pl.when(i == n_tiles - 1)
    def _():
        @pl.when(i >= 1)
        def _():
            drain((i + DISPATCH_SLOTS - 1) % DISPATCH_SLOTS)
        drain(slot)


def _dispatch_call(cls, rank, counts, hrows, n_blocks, xs_init):
    n_tiles, _, ts = cls.shape
    n_rows = n_blocks * ROW_BLOCK
    assert xs_init.shape == (n_rows * SUBLANES, LANES) and xs_init.dtype == jnp.uint32
    smem_full = pl.BlockSpec(memory_space=pltpu.SMEM)
    smem_tile = pl.BlockSpec((1, 1, ts), lambda i: (i, 0, 0), memory_space=pltpu.SMEM)
    any_spec = pl.BlockSpec(memory_space=pl.ANY)
    return pl.pallas_call(
        functools.partial(_dispatch_kernel, n_tiles, n_blocks),
        out_shape=(
            jax.ShapeDtypeStruct((n_rows * SUBLANES, LANES), jnp.uint32),
            jax.ShapeDtypeStruct((n_tiles, 1, ts), jnp.int32),
            jax.ShapeDtypeStruct((n_blocks,), jnp.int32),
            jax.ShapeDtypeStruct((n_blocks,), jnp.int32),
            jax.ShapeDtypeStruct((1,), jnp.int32),
        ),
        grid=(n_tiles,),
        in_specs=[
            smem_tile, smem_tile,
            smem_full, smem_full, smem_full, any_spec, any_spec,
        ],
        out_specs=(
            any_spec,
            smem_tile,
            smem_full, smem_full, smem_full,
        ),
        scratch_shapes=[
            pltpu.SMEM((LANES,), jnp.int32),
            pltpu.VMEM((DISPATCH_SLOTS, ts * SUBLANES, LANES), jnp.uint32),
            pltpu.SemaphoreType.DMA((DISPATCH_SLOTS,)),
            pltpu.SemaphoreType.DMA((DISPATCH_SLOTS,)),
        ],
        input_output_aliases={6: 0},
        compiler_params=pltpu.CompilerParams(
            dimension_semantics=("arbitrary",), vmem_limit_bytes=VMEM_LIMIT_BYTES),
        name="moe_dispatch",
    )(cls, rank, counts, jnp.asarray(_CLASS_LO), jnp.asarray(_CLASS_HI), hrows, xs_init)


def _expert_kernel(blo_ref, bhi_ref, nblk_ref, xs_ref, wg_lo, wu_lo, wd_lo, wg_hi, wu_hi, wd_hi, ys_ref,
                   bg_lo, bu_lo, bd_lo, bg_hi, bu_hi, bd_hi):
    b = pl.program_id(0)
    rb = ROW_BLOCK
    active = b < nblk_ref[0]
    prev = jnp.maximum(b - 1, 0)

    @pl.when(active & ((b == 0) | (blo_ref[b] != blo_ref[prev])))
    def _():
        bg_lo[...], bu_lo[...], bd_lo[...] = _bf16(wg_lo[...]), _bf16(wu_lo[...]), _bf16(wd_lo[...])

    @pl.when(active & ((b == 0) | (bhi_ref[b] != bhi_ref[prev])))
    def _():
        bg_hi[...], bu_hi[...], bd_hi[...] = _bf16(wg_hi[...]), _bf16(wu_hi[...]), _bf16(wd_hi[...])

    @pl.when(active)
    def _():
        pieces = []
        for j in range(PACK_SUBLANES):
            w = _token_rows(xs_ref, j, rb)
            pieces.append(_bf16(_f32_from_bits(w & jnp.uint32(0xFFFF0000))))
            pieces.append(_bf16(_f32_from_bits(lax.shift_left(w, jnp.uint32(16)))))
        xb = jnp.concatenate(pieces, axis=1)
        gates = _f32_from_bits(_token_rows(xs_ref, PACK_SUBLANES, rb))
        acc = None
        for idx, (wg, wu, wd) in enumerate(((bg_lo, bu_lo, bd_lo), (bg_hi, bu_hi, bd_hi))):
            a = _dot(xb, wg[...])
            hid = (a * jax.nn.sigmoid(a)) * _dot(xb, wu[...]) * gates[:, idx:idx + 1]
            part = _dot(_bf16(hid), wd[...])
            acc = part if acc is None else acc + part
        for s in range(SUBLANES):
            ys_ref[pl.ds(s, rb, stride=SUBLANES), :] = acc[:, s * LANES:(s + 1) * LANES]

    @pl.when(b >= nblk_ref[0])
    def _():
        ys_ref[...] = jnp.zeros_like(ys_ref)


def _expert_call(blo, bhi, nblk, xs, wg, wu, wd, layer):
    n_rows = xs.shape[0] // SUBLANES
    d, de = wg.shape[2], wg.shape[3]
    assert d == SUBLANES * LANES
    n_blocks = n_rows // ROW_BLOCK

    def w_spec(shape, which):
        if which == 0:
            return pl.BlockSpec((None, None) + shape, lambda b, lo, hi, n: (layer, lo[b], 0, 0))
        return pl.BlockSpec((None, None) + shape, lambda b, lo, hi, n: (layer, hi[b], 0, 0))

    def x_map(b, lo, hi, n):
        return (jnp.minimum(b, jnp.maximum(n[0] - 1, 0)), 0)

    grid_spec = pltpu.PrefetchScalarGridSpec(
        num_scalar_prefetch=3,
        grid=(n_blocks,),
        in_specs=[
            pl.BlockSpec((ROW_BLOCK * SUBLANES, LANES), x_map),
            w_spec((d, de), 0), w_spec((d, de), 0), w_spec((de, d), 0),
            w_spec((d, de), 1), w_spec((d, de), 1), w_spec((de, d), 1),
        ],
        out_specs=pl.BlockSpec((ROW_BLOCK * SUBLANES, LANES), lambda b, lo, hi, n: (b, 0)),
        scratch_shapes=[pltpu.VMEM(shape, jnp.bfloat16)
                        for shape in ((d, de), (d, de), (de, d)) * 2],
    )
    return pl.pallas_call(
        _expert_kernel,
        out_shape=jax.ShapeDtypeStruct((n_rows * SUBLANES, LANES), jnp.float32),
        grid_spec=grid_spec,
        compiler_params=pltpu.CompilerParams(
            dimension_semantics=("arbitrary",), vmem_limit_bytes=VMEM_LIMIT_BYTES),
        name="moe_experts",
    )(blo, bhi, nblk, xs, wg, wu, wd, wg, wu, wd)


def _final_kernel(n_tiles, dest_ref, dnext_ref, x1_ref, mod_ref, fmod_ref, fg_ref, ys_ref,
                  out_ref, ybuf, sem, unperm):
    i = pl.program_id(0)
    ts, d = x1_ref.shape
    bsz, tm, _ = out_ref.shape
    y = _gather_expert_tiles(ys_ref, ybuf, sem, dest_ref, dnext_ref, i, n_tiles, ts)
    x2 = x1_ref[...] + _per_batch(mod_ref[5], ts) * y
    res = _rms_mod(x2, fg_ref[...], fmod_ref[0], fmod_ref[1])
    for cb in range(d // LANES):
        unperm[cb] = res[:, cb * LANES:(cb + 1) * LANES]
    for cb in range(d // LANES):
        for p in range(bsz):
            out_ref[p, :, cb * LANES:(cb + 1) * LANES] = unperm.at[cb][pl.ds(p, tm, stride=SUBLANES), :]


def _final_call(dest, x1, mod, ys, bsz, fmod, fin_g):
    t, d = x1.shape
    n_tiles, _, ts = dest.shape
    tm = ts // bsz
    return pl.pallas_call(
        functools.partial(_final_kernel, n_tiles),
        out_shape=jax.ShapeDtypeStruct((bsz, t // bsz, d), jnp.float32),
        grid=(n_tiles,),
        in_specs=[
            pl.BlockSpec((1, 1, ts), lambda i: (i, 0, 0), memory_space=pltpu.SMEM),
            pl.BlockSpec((1, 1, ts), lambda i: (jnp.minimum(i + 1, n_tiles - 1), 0, 0),
                         memory_space=pltpu.SMEM),
            pl.BlockSpec((ts, d), lambda i: (i, 0)),
            _full((MOD_ROWS, bsz, d)), _full((MOD_ROWS, bsz, d)), _full((1, d)),
            pl.BlockSpec(memory_space=pl.ANY),
        ],
        out_specs=pl.BlockSpec((bsz, tm, d), lambda i: (0, i, 0)),
        scratch_shapes=[pltpu.VMEM((2, ts * SUBLANES, LANES), jnp.float32),
                        pltpu.SemaphoreType.DMA((2,)),
                        pltpu.VMEM((d // LANES, ts, LANES), jnp.float32)],
        compiler_params=pltpu.CompilerParams(
            dimension_semantics=("arbitrary",), vmem_limit_bytes=VMEM_LIMIT_BYTES),
        name="moe_combine_final",
    )(dest, dest, x1, mod, fmod, fin_g, ys)


def _router_operands(wg, bg, wr, br):
    d = wg.shape[0]
    pad_g = jnp.zeros((SUBLANES - N_GROUPS, d), jnp.float32)
    pad_e = jnp.zeros((ROUTER_ROWS - SUBLANES - N_EXPERTS, d), jnp.float32)
    wrt = jnp.concatenate([wg.T, pad_g, wr.T, pad_e], axis=0)
    bcol = jnp.concatenate([bg, jnp.zeros((SUBLANES - N_GROUPS,), jnp.float32), br,
                            jnp.zeros((ROUTER_ROWS - SUBLANES - N_EXPERTS,), jnp.float32)])
    brt = jnp.broadcast_to(bcol[:, None], (ROUTER_ROWS, LANES))
    return _bf16(wrt), brt


def _mod_table(m, parts):
    bsz = m.shape[0]
    d = m.shape[1] // parts
    m = jnp.transpose(m.reshape(bsz, parts, d), (1, 0, 2))
    return jnp.concatenate([m, jnp.zeros((MOD_ROWS - parts, bsz, d), jnp.float32)], axis=0)


def _row(v):
    return v.reshape(1, -1)


def _moe_experts(hrows, cls, rank, cnt, wg, wu, wd, layer, n_blocks, xs_init):
    counts = cnt[:, 0].astype(jnp.int32)
    xs, dest, blo, bhi, nblk = _dispatch_call(cls, rank, counts, hrows, n_blocks, xs_init)
    ys = _expert_call(blo, bhi, nblk, xs, wg, wu, wd, layer)
    return xs, ys, dest


def kernel(x, c, ada_w, ada_b, norm1_g, norm2_g, cm_w1, cm_b1, cm_dw, cm_dwb, cm_ln_g, cm_ln_b, cm_w2, cm_b2, rg_wy, rg_by, rg_wx, rg_bx, rg_cw, rg_cb, rg_wa, rg_ba, rg_wi, rg_bi, rg_lambda, rg_wo, rg_bo, moe_wg, moe_bg, moe_wr, moe_br, moe_w_gate, moe_w_up, moe_w_down, fin_ada_w, fin_ada_b, fin_g):
    bsz, seq, d = x.shape
    t = bsz * seq
    depth = ada_w.shape[0]
    assert depth == 2 and bsz == SUBLANES and seq % TIME_TILE == 0 and t % ROW_BLOCK == 0
    assert d == 2 * PACK_SUBLANES * LANES

    mods = _ada_call(c, ada_w, ada_b[:, None, :])
    fmods = _ada_call(c, fin_ada_w[None], fin_ada_b[None, None, :])
    fmod = _mod_table(fmods[0], 2)
    n_blocks = t // ROW_BLOCK + N_CLASSES
    mod0, mod1 = _mod_table(mods[0], 6), _mod_table(mods[1], 6)

    wrt, brt = _router_operands(moe_wg[0], moe_bg[0], moe_wr[0], moe_br[0])
    dw = jnp.concatenate([cm_dw[0], jnp.zeros((1, d), jnp.float32)], axis=0)
    dw = jnp.transpose(dw.reshape(CONF_KERNEL + 1, d // LANES, LANES), (1, 0, 2))
    x1, hrows, cls, rank, cnt = _conf_call(
        x, mod0, _row(norm1_g[0]), _row(norm2_g[0]), _bf16(cm_w1[0]), _row(cm_b1[0]),
        dw, cm_dwb[0].reshape(d // LANES, 1, LANES), _row(cm_ln_g[0]), _row(cm_ln_b[0]),
        _bf16(cm_w2[0]), _row(cm_b2[0]), wrt, brt)
    xs_init = jnp.zeros((n_blocks * ROW_BLOCK * SUBLANES, LANES), jnp.uint32)
    xs, ys, dest = _moe_experts(hrows, cls, rank, cnt, moe_w_gate, moe_w_up, moe_w_down, 0,
                                n_blocks, xs_init)

    wrt, brt = _router_operands(moe_wg[1], moe_bg[1], moe_wr[1], moe_br[1])
    x1, hrows, cls, rank, cnt = _lru_call(
        dest, x1, mod0, ys, mod1, _row(norm1_g[1]), _row(norm2_g[1]),
        _bf16(rg_wy[0]), _row(rg_by[0]), _bf16(rg_wx[0]), _row(rg_bx[0]),
        rg_cw[0], _row(rg_cb[0]), _bf16(rg_wa[0]), _row(rg_ba[0]), _bf16(rg_wi[0]), _row(rg_bi[0]),
        _row(rg_lambda[0]), _bf16(rg_wo[0]), _row(rg_bo[0]), wrt, brt)
    _, ys, dest = _moe_experts(hrows, cls, rank, cnt, moe_w_gate, moe_w_up, moe_w_down, 1,
                               n_blocks, xs)
    return _final_call(dest, x1, mod1, ys, bsz, fmod, _row(fin_g))
```

```python
import functools
import math

import jax
import jax.numpy as jnp
import numpy as np
from jax import lax
from jax.experimental import pallas as pl
from jax.experimental.pallas import tpu as pltpu

EPS = 1e-6
CONF_KERNEL = 31
LRU_HEADS = 4
LRU_CONV = 4
LRU_C = 8.0
N_GROUPS = 4
EXPERTS_PER_GROUP = 8
N_EXPERTS = N_GROUPS * EXPERTS_PER_GROUP
PAIRS_PER_GROUP = EXPERTS_PER_GROUP * (EXPERTS_PER_GROUP - 1) // 2
N_CLASSES = N_GROUPS * PAIRS_PER_GROUP

LANES = 128
SUBLANES = 8
VMEM_LIMIT_BYTES = 56 * 1024 * 1024

TOKEN_TILE = 512
TIME_TILE = TOKEN_TILE // SUBLANES
SUB_TILE = 256
ROW_BLOCK = 256
PERM_PITCH = TIME_TILE + SUBLANES
CONV_ROWS = 128
ROUTER_ROWS = 64
PACK_SUBLANES = 4
MOD_ROWS = 8
DISPATCH_SLOTS = 3
DMA_UNROLL = 8

_NEG_INF = float("-inf")


def _class_tables():
    lo = np.zeros((N_CLASSES,), np.int32)
    hi = np.zeros((N_CLASSES,), np.int32)
    for g in range(N_GROUPS):
        for a in range(EXPERTS_PER_GROUP):
            for b in range(a + 1, EXPERTS_PER_GROUP):
                c = g * PAIRS_PER_GROUP + (a * (2 * EXPERTS_PER_GROUP - 1 - a)) // 2 + (b - a - 1)
                lo[c] = g * EXPERTS_PER_GROUP + a
                hi[c] = g * EXPERTS_PER_GROUP + b
    return lo, hi


_CLASS_LO, _CLASS_HI = _class_tables()


def _bf16(x):
    return x.astype(jnp.bfloat16)


def _dot(a, b):
    return jnp.dot(a, b, preferred_element_type=jnp.float32)


def _bits(x):
    return lax.bitcast_convert_type(x, jnp.uint32)


def _f32_from_bits(x):
    return lax.bitcast_convert_type(x, jnp.float32)


def _token_rows(ref, sublane, n):
    return ref[pl.ds(sublane, n, stride=SUBLANES), :]


def _per_batch(v, ts):
    return jnp.tile(v, (ts // SUBLANES, 1))


def _ada_kernel(c_ref, w_ref, b_ref, o_ref):
    c = c_ref[...]
    ca = _bf16(c * jax.nn.sigmoid(c))
    o_ref[...] = _dot(ca, _bf16(w_ref[...])) + b_ref[...]


def _ada_call(c, w, b):
    n_layers, d, n = w.shape
    bsz = c.shape[0]
    nb = 1024
    return pl.pallas_call(
        _ada_kernel,
        out_shape=jax.ShapeDtypeStruct((n_layers, bsz, n), jnp.float32),
        grid=(n_layers, n // nb),
        in_specs=[
            pl.BlockSpec((bsz, d), lambda l, j: (0, 0)),
            pl.BlockSpec((None, d, nb), lambda l, j: (l, 0, j)),
            pl.BlockSpec((None, 1, nb), lambda l, j: (l, 0, j)),
        ],
        out_specs=pl.BlockSpec((None, bsz, nb), lambda l, j: (l, 0, j)),
        compiler_params=pltpu.CompilerParams(
            dimension_semantics=("arbitrary", "arbitrary"),
            vmem_limit_bytes=VMEM_LIMIT_BYTES),
        name="ada_mod",
    )(c, w, b)


def _rms_mod(x, gain, shift, scale):
    ts = x.shape[0]
    ms = jnp.mean(x * x, axis=-1, keepdims=True)
    return (x * lax.rsqrt(ms + EPS)) * _per_batch(gain * (1.0 + scale), ts) + _per_batch(shift, ts)


def _causal_taps(w_ref, bias_ref, ext_ref, n_taps, row0, n):
    d = ext_ref.shape[1]
    cols = []
    for cb in range(d // LANES):
        sl = slice(cb * LANES, (cb + 1) * LANES)
        acc = jnp.broadcast_to(bias_ref[:, sl], (n, LANES))
        for k in range(n_taps):
            lo = row0 + SUBLANES * k
            acc = acc + w_ref[k:k + 1, sl] * ext_ref[lo:lo + n, sl]
        cols.append(acc)
    return jnp.concatenate(cols, axis=1)


def _causal_taps_chunked(w_ref, bias_ref, ext_ref, out_ref, n_taps, ts):
    n_chunks = ext_ref.shape[0]
    units_per_chunk = ts // CONV_ROWS

    def unit(idx, carry):
        cb = idx // units_per_chunk
        r0 = pl.multiple_of((idx % units_per_chunk) * CONV_ROWS, CONV_ROWS)
        w_cb, ext_cb = w_ref.at[cb], ext_ref.at[cb]
        acc = jnp.broadcast_to(bias_ref[cb], (CONV_ROWS, LANES))
        for k in range(n_taps):
            win = ext_cb[pl.ds(pl.multiple_of(r0 + SUBLANES * k, SUBLANES), CONV_ROWS), :]
            acc = acc + w_cb[k:k + 1, :] * win
        out_ref.at[cb][pl.ds(r0, CONV_ROWS), :] = acc
        return carry

    lax.fori_loop(0, n_chunks * units_per_chunk, unit, 0)


def _start_tile_dmas(make_copy, n):
    def body(r, carry):
        make_copy(r).start()
        return carry
    lax.fori_loop(0, n, body, 0, unroll=DMA_UNROLL)


def _wait_tile_dmas(make_copy, n):
    def body(r, carry):
        make_copy(r).wait()
        return carry
    lax.fori_loop(0, n, body, 0, unroll=DMA_UNROLL)


def _gather_expert_tiles(ys_ref, ybuf, sem, dest_ref, dnext_ref, step, n_steps, ts):
    slot = step % 2

    def tile_copy(dref, sl):
        def make(r):
            src = pl.multiple_of(dref[0, 0, r] * SUBLANES, SUBLANES)
            return pltpu.make_async_copy(
                ys_ref.at[pl.ds(src, SUBLANES)],
                ybuf.at[sl, pl.ds(pl.multiple_of(r * SUBLANES, SUBLANES), SUBLANES)],
                sem.at[sl])
        return make

    @pl.when(step == 0)
    def _():
        _start_tile_dmas(tile_copy(dest_ref, 0), ts)

    @pl.when(step + 1 < n_steps)
    def _():
        _start_tile_dmas(tile_copy(dnext_ref, 1 - slot), ts)

    _wait_tile_dmas(tile_copy(dest_ref, slot), ts)
    yv = ybuf.at[slot]
    return jnp.concatenate([_token_rows(yv, s, ts) for s in range(SUBLANES)], axis=1)


def _route(logits_t):
    ts = logits_t.shape[1]
    row = lax.broadcasted_iota(jnp.int32, (SUBLANES, ts), 0)
    gl = jnp.where(row < N_GROUPS, logits_t[0:SUBLANES], _NEG_INF)
    gmax = jnp.max(gl, axis=0, keepdims=True)
    gsel = jnp.min(jnp.where(gl == gmax, row, SUBLANES), axis=0, keepdims=True)
    p_g = 1.0 / jnp.sum(jnp.exp(gl - gmax), axis=0, keepdims=True)
    es = jnp.zeros((SUBLANES, ts), jnp.float32)
    for g in range(N_GROUPS):
        lo = SUBLANES * (1 + g)
        es = jnp.where(gsel == g, logits_t[lo:lo + EXPERTS_PER_GROUP], es)
    m1 = jnp.max(es, axis=0, keepdims=True)
    i1 = jnp.min(jnp.where(es == m1, row, SUBLANES), axis=0, keepdims=True)
    es2 = jnp.where(row == i1, _NEG_INF, es)
    m2 = jnp.max(es2, axis=0, keepdims=True)
    i2 = jnp.min(jnp.where(es2 == m2, row, SUBLANES), axis=0, keepdims=True)
    t = jnp.exp(m2 - m1)
    w1 = p_g / (1.0 + t)
    w2 = w1 * t
    first_lo = i1 < i2
    e_lo = jnp.minimum(i1, i2)
    e_hi = jnp.maximum(i1, i2)
    gate_lo = jnp.where(first_lo, w1, w2)
    gate_hi = jnp.where(first_lo, w2, w1)
    pair = lax.shift_right_logical(e_lo * (2 * EXPERTS_PER_GROUP - 1 - e_lo), 1) + (e_hi - e_lo - 1)
    cls = gsel * PAIRS_PER_GROUP + pair
    return cls, gate_lo, gate_hi


def _post_mixer(x1, mod_ref, n2_ref, wrt_ref, brt_ref, hrow_ref, cls_ref, rank_ref, cnt_ref, sub):
    ts = x1.shape[0]
    hrow_ref = hrow_ref.at[pl.ds(sub * ts * SUBLANES, ts * SUBLANES)]
    cols = slice(sub * ts, (sub + 1) * ts)
    h2 = _rms_mod(x1, n2_ref[...], mod_ref[3], mod_ref[4])
    h2b = _bf16(h2)
    logits_t = lax.dot_general(wrt_ref[...], h2b, (((1,), (1,)), ((), ())),
                               preferred_element_type=jnp.float32) + brt_ref[:, 0:1]
    cls, gate_lo, gate_hi = _route(logits_t)
    gates_t = jnp.concatenate(
        [gate_lo, gate_hi, jnp.zeros((LANES - 2, ts), jnp.float32)], axis=0)
    h2r = h2b.astype(jnp.float32)
    for j in range(PACK_SUBLANES):
        hi = _bits(h2r[:, (2 * j) * LANES:(2 * j + 1) * LANES])
        lo = _bits(h2r[:, (2 * j + 1) * LANES:(2 * j + 2) * LANES])
        hrow_ref[pl.ds(j, ts, stride=SUBLANES), :] = hi | lax.shift_right_logical(lo, jnp.uint32(16))
    hrow_ref[pl.ds(PACK_SUBLANES, ts, stride=SUBLANES), :] = _bits(gates_t.T)
    for j in range(PACK_SUBLANES + 1, SUBLANES):
        hrow_ref[pl.ds(j, ts, stride=SUBLANES), :] = jnp.zeros((ts, LANES), jnp.uint32)
    cls_ref[0, :, cols] = cls
    crow = lax.broadcasted_iota(jnp.int32, (LANES, ts), 0)
    onehot = jnp.where(crow == cls, 1.0, 0.0)
    cnt = jnp.sum(onehot, axis=1, keepdims=True)
    earlier = (lax.broadcasted_iota(jnp.int32, (ts, ts), 0)
               < lax.broadcasted_iota(jnp.int32, (ts, ts), 1))
    prefix = _dot(_bf16(onehot), _bf16(jnp.where(earlier, 1.0, 0.0)))
    rank = jnp.sum(onehot * (prefix + cnt_ref[:, 0:1]), axis=0, keepdims=True)
    rank_ref[0, :, cols] = rank.astype(jnp.int32)
    cnt_ref[...] += jnp.broadcast_to(cnt, cnt_ref.shape)


def _mixer_out(t, d, ts):
    n_tiles = t // ts
    out_shape = (
        jax.ShapeDtypeStruct((t, d), jnp.float32),
        jax.ShapeDtypeStruct((t * SUBLANES, LANES), jnp.uint32),
        jax.ShapeDtypeStruct((n_tiles, 1, ts), jnp.int32),
        jax.ShapeDtypeStruct((n_tiles, 1, ts), jnp.int32),
        jax.ShapeDtypeStruct((LANES, LANES), jnp.float32),
    )
    out_specs = (
        pl.BlockSpec((ts, d), lambda i: (i, 0)),
        pl.BlockSpec((ts * SUBLANES, LANES), lambda i: (i, 0)),
        pl.BlockSpec((1, 1, ts), lambda i: (i, 0, 0)),
        pl.BlockSpec((1, 1, ts), lambda i: (i, 0, 0)),
        pl.BlockSpec((LANES, LANES), lambda i: (0, 0)),
    )
    return out_shape, out_specs


def _full(shape):
    return pl.BlockSpec(shape, lambda i: tuple(0 for _ in shape))


def _conf_kernel(x_ref, mod_ref, n1_ref, n2_ref, w1_ref, b1_ref, dw_ref, dwb_ref,
                 lng_ref, lnb_ref, w2_ref, b2_ref, wrt_ref, brt_ref,
                 x1_ref, hrow_ref, cls_ref, rank_ref, cnt_ref, zeros_ref, perm, xs, vext, conv):
    i = pl.program_id(0)
    bsz, tm, d = x_ref.shape
    ts = bsz * tm
    n_chunks = d // LANES
    hist = SUBLANES * (CONF_KERNEL - 1)

    @pl.when(i == 0)
    def _():
        vext[:, 0:hist, :] = jnp.zeros((n_chunks, hist, LANES), jnp.float32)
        cnt_ref[...] = jnp.zeros_like(cnt_ref)

    zeros_ref[...] = jnp.zeros_like(zeros_ref)

    for cb in range(n_chunks):
        for p in range(bsz):
            perm[cb, PERM_PITCH * p:PERM_PITCH * p + tm, :] = x_ref[p, :, cb * LANES:(cb + 1) * LANES]
    for m in range(tm):
        for cb in range(n_chunks):
            xs[m * bsz:(m + 1) * bsz, cb * LANES:(cb + 1) * LANES] = (
                perm.at[cb][pl.ds(m, bsz, stride=PERM_PITCH), :])

    for sub in range(ts // SUB_TILE):
        rows = slice(sub * SUB_TILE, (sub + 1) * SUB_TILE)
        h = _bf16(_rms_mod(xs[rows, :], n1_ref[...], mod_ref[0], mod_ref[1]))
        u = _dot(h, w1_ref[...]) + b1_ref[...]
        v = u[:, 0:d] * jax.nn.sigmoid(u[:, d:2 * d])
        for cb in range(n_chunks):
            vext[cb, hist + sub * SUB_TILE:hist + (sub + 1) * SUB_TILE, :] = v[:, cb * LANES:(cb + 1) * LANES]

    _causal_taps_chunked(dw_ref, dwb_ref, vext, conv, CONF_KERNEL, ts)
    vext[:, 0:hist, :] = vext[:, ts:ts + hist, :]

    for sub in range(ts // SUB_TILE):
        rows = slice(sub * SUB_TILE, (sub + 1) * SUB_TILE)
        y = jnp.concatenate([conv[cb, rows, :] for cb in range(n_chunks)], axis=1)
        mu = jnp.mean(y, axis=-1, keepdims=True)
        yc = y - mu
        var = jnp.mean(yc * yc, axis=-1, keepdims=True)
        z = yc * lax.rsqrt(var + EPS) * lng_ref[...] + lnb_ref[...]
        z = z * jax.nn.sigmoid(z)
        mix = _dot(_bf16(z), w2_ref[...]) + b2_ref[...]
        x1 = xs[rows, :] + _per_batch(mod_ref[2], SUB_TILE) * mix
        x1_ref[rows, :] = x1
        _post_mixer(x1, mod_ref, n2_ref, wrt_ref, brt_ref, hrow_ref, cls_ref, rank_ref, cnt_ref, sub)


def _conf_call(x, mod, n1, n2, w1, b1, dw, dwb, lng, lnb, w2, b2, wrt, brt, zero_rows):
    bsz, seq, d = x.shape
    tm = TIME_TILE
    ts = bsz * tm
    n_steps = seq // tm
    assert zero_rows % (n_steps * SUBLANES) == 0
    out_shape, out_specs = _mixer_out(bsz * seq, d, ts)
    out_shape += (jax.ShapeDtypeStruct((zero_rows, LANES), jnp.uint32),)
    out_specs += (pl.BlockSpec((zero_rows // n_steps, LANES), lambda i: (i, 0)),)
    hist = SUBLANES * (CONF_KERNEL - 1)
    return pl.pallas_call(
        _conf_kernel,
        out_shape=out_shape,
        grid=(n_steps,),
        in_specs=[
            pl.BlockSpec((bsz, tm, d), lambda i: (0, i, 0)),
            _full((MOD_ROWS, bsz, d)),
            _full((1, d)), _full((1, d)),
            _full((d, 2 * d)), _full((1, 2 * d)),
            _full((d // LANES, CONF_KERNEL + 1, LANES)), _full((d // LANES, 1, LANES)),
            _full((1, d)), _full((1, d)),
            _full((d, d)), _full((1, d)),
            _full((ROUTER_ROWS, d)), _full((ROUTER_ROWS, LANES)),
        ],
        out_specs=out_specs,
        scratch_shapes=[
            pltpu.VMEM((d // LANES, bsz * PERM_PITCH, LANES), jnp.float32),
            pltpu.VMEM((ts, d), jnp.float32),
            pltpu.VMEM((d // LANES, hist + ts, LANES), jnp.float32),
            pltpu.VMEM((d // LANES, ts, LANES), jnp.float32),
        ],
        compiler_params=pltpu.CompilerParams(
            dimension_semantics=("arbitrary",), vmem_limit_bytes=VMEM_LIMIT_BYTES),
        name="conf_mixer",
    )(x, mod, n1, n2, w1, b1, dw, dwb, lng, lnb, w2, b2, wrt, brt)


def _lru_kernel(n_tiles, dest_ref, dnext_ref, xprev_ref, pmod_ref, mod_ref, n1_ref, n2_ref,
                wy_ref, by_ref, wx_ref, bx_ref,
                cw_ref, cb_ref, wa_ref, ba_ref, wi_ref, bi_ref, lam_ref, wo_ref, bo_ref,
                wrt_ref, brt_ref, ys_ref,
                x1_ref, hrow_ref, cls_ref, rank_ref, cnt_ref, uext, hcar, ybuf, ysem):
    i = pl.program_id(0)
    ts, d = xprev_ref.shape
    hd = d // LRU_HEADS
    hist = SUBLANES * (LRU_CONV - 1)

    @pl.when(i == 0)
    def _():
        uext[0:hist, :] = jnp.zeros((hist, d), jnp.float32)
        hcar[...] = jnp.zeros_like(hcar)
        cnt_ref[...] = jnp.zeros_like(cnt_ref)

    y_prev = _gather_expert_tiles(ys_ref, ybuf, ysem, dest_ref, dnext_ref, i, n_tiles, ts)
    nl = -lam_ref[...]
    softplus = jnp.maximum(nl, 0.0) + jnp.log1p(jnp.exp(-jnp.abs(nl)))
    hstate = hcar[...]
    for sub in range(ts // SUB_TILE):
        rows = slice(sub * SUB_TILE, (sub + 1) * SUB_TILE)
        x = xprev_ref[rows, :] + _per_batch(pmod_ref[5], SUB_TILE) * y_prev[rows, :]
        h = _bf16(_rms_mod(x, n1_ref[...], mod_ref[0], mod_ref[1]))
        ygate = jax.nn.gelu(_dot(h, wy_ref[...]) + by_ref[...])
        xb = _dot(h, wx_ref[...]) + bx_ref[...]
        uext[hist + sub * SUB_TILE:hist + (sub + 1) * SUB_TILE, :] = xb
        u = _causal_taps(cw_ref, cb_ref, uext, LRU_CONV, sub * SUB_TILE, SUB_TILE)

        ub = _bf16(u)
        r_parts, i_parts = [], []
        for hh in range(LRU_HEADS):
            uh = ub[:, hh * hd:(hh + 1) * hd]
            r_parts.append(_dot(uh, wa_ref[hh]))
            i_parts.append(_dot(uh, wi_ref[hh]))
        r = jax.nn.sigmoid(jnp.concatenate(r_parts, axis=1) + ba_ref[...])
        ig = jax.nn.sigmoid(jnp.concatenate(i_parts, axis=1) + bi_ref[...])

        log_a = (-LRU_C) * r * softplus
        a = jnp.exp(log_a)
        mult = jnp.sqrt(-jnp.tanh(log_a) * (1.0 + a * a))
        b = mult * (ig * u)

        outs = []
        for m in range(SUB_TILE // SUBLANES):
            sl = slice(m * SUBLANES, (m + 1) * SUBLANES)
            hstate = a[sl] * hstate + b[sl]
            outs.append(hstate)
        hs = jnp.concatenate(outs, axis=0)

        mix = _dot(_bf16(hs * ygate), wo_ref[...]) + bo_ref[...]
        x1 = x + _per_batch(mod_ref[2], SUB_TILE) * mix
        x1_ref[rows, :] = x1
        _post_mixer(x1, mod_ref, n2_ref, wrt_ref, brt_ref, hrow_ref, cls_ref, rank_ref, cnt_ref, sub)
    hcar[...] = hstate
    uext[0:hist, :] = uext[ts:ts + hist, :]


def _lru_call(dest, x_prev, mod_prev, ys, mod, n1, n2, wy, by, wx, bx, cw, cb, wa, ba, wi, bi, lam,
              wo, bo, wrt, brt):
    t, d = x_prev.shape
    n_tiles, _, ts = dest.shape
    bsz = mod.shape[1]
    hd = d // LRU_HEADS
    hist = SUBLANES * (LRU_CONV - 1)
    out_shape, out_specs = _mixer_out(t, d, ts)
    return pl.pallas_call(
        functools.partial(_lru_kernel, n_tiles),
        out_shape=out_shape,
        grid=(n_tiles,),
        in_specs=[
            pl.BlockSpec((1, 1, ts), lambda i: (i, 0, 0), memory_space=pltpu.SMEM),
            pl.BlockSpec((1, 1, ts), lambda i: (jnp.minimum(i + 1, n_tiles - 1), 0, 0),
                         memory_space=pltpu.SMEM),
            pl.BlockSpec((ts, d), lambda i: (i, 0)),
            _full((MOD_ROWS, bsz, d)), _full((MOD_ROWS, bsz, d)),
            _full((1, d)), _full((1, d)),
            _full((d, d)), _full((1, d)),
            _full((d, d)), _full((1, d)),
            _full((LRU_CONV, d)), _full((1, d)),
            _full((LRU_HEADS, hd, hd)), _full((1, d)),
            _full((LRU_HEADS, hd, hd)), _full((1, d)),
            _full((1, d)),
            _full((d, d)), _full((1, d)),
            _full((ROUTER_ROWS, d)), _full((ROUTER_ROWS, LANES)),
            pl.BlockSpec(memory_space=pl.ANY),
        ],
        out_specs=out_specs,
        scratch_shapes=[
            pltpu.VMEM((hist + ts, d), jnp.float32),
            pltpu.VMEM((SUBLANES, d), jnp.float32),
            pltpu.VMEM((2, ts * SUBLANES, LANES), jnp.float32),
            pltpu.SemaphoreType.DMA((2,)),
        ],
        compiler_params=pltpu.CompilerParams(
            dimension_semantics=("arbitrary",), vmem_limit_bytes=VMEM_LIMIT_BYTES),
        name="lru_mixer",
    )(dest, dest, x_prev, mod_prev, mod, n1, n2, wy, by, wx, bx, cw, cb, wa, ba, wi, bi, lam, wo, bo,
      wrt, brt, ys)


def _dispatch_kernel(n_tiles, n_blocks, cls_ref, rank_ref, cnt_ref, clo_ref, chi_ref, hrow_ref, xs_in_ref,
                     xs_ref, dest_ref, blo_ref, bhi_ref, nblk_ref, class_start, stage, sem_in, sem_out):
    del xs_in_ref
    i = pl.program_id(0)
    ts = cls_ref.shape[2]
    rows = ts * SUBLANES

    @pl.when(i == 0)
    def _():
        def per_class(c, nb_total):
            class_start[c] = nb_total * ROW_BLOCK
            nb = lax.shift_right_logical(cnt_ref[c] + (ROW_BLOCK - 1), ROW_BLOCK.bit_length() - 1)

            def fill(k, carry):
                blo_ref[nb_total + k] = clo_ref[c]
                bhi_ref[nb_total + k] = chi_ref[c]
                return carry

            lax.fori_loop(0, nb, fill, 0)
            return nb_total + nb

        nb_used = lax.fori_loop(0, N_CLASSES, per_class, 0)
        nblk_ref[0] = nb_used
        last = jnp.maximum(nb_used - 1, 0)

        def tail(k, carry):
            blo_ref[k] = blo_ref[last]
            bhi_ref[k] = bhi_ref[last]
            return carry

        lax.fori_loop(nb_used, n_blocks, tail, 0)

    def load(tile, slot):
        return pltpu.make_async_copy(hrow_ref.at[pl.ds(pl.multiple_of(tile * rows, rows), rows)],
                                     stage.at[slot], sem_in.at[slot])

    def tile_copy(slot, r, dst_row):
        return pltpu.make_async_copy(
            stage.at[slot, pl.ds(pl.multiple_of(r * SUBLANES, SUBLANES), SUBLANES)],
            xs_ref.at[pl.ds(pl.multiple_of(dst_row * SUBLANES, SUBLANES), SUBLANES)],
            sem_out.at[slot])

    def drain(slot):
        _wait_tile_dmas(lambda r: tile_copy(slot, r, 0), ts)

    slot = i % DISPATCH_SLOTS
    nxt = (i + 1) % DISPATCH_SLOTS

    @pl.when(i == 0)
    def _():
        load(0, 0).start()

    @pl.when(i >= 2)
    def _():
        drain(nxt)

    @pl.when(i + 1 < n_tiles)
    def _():
        load(i + 1, nxt).start()

    load(i, slot).wait()

    def scatter(r):
        dst_row = class_start[cls_ref[0, 0, r]] + rank_ref[0, 0, r]
        dest_ref[0, 0, r] = dst_row
        return tile_copy(slot, r, dst_row)

    _start_tile_dmas(scatter, ts)

    @pl.when(i == n_tiles - 1)
    def _():
        @pl.when(i >= 1)
        def _():
            drain((i + DISPATCH_SLOTS - 1) % DISPATCH_SLOTS)
        drain(slot)


def _dispatch_call(cls, rank, counts, hrows, n_blocks, xs_init):
    n_tiles, _, ts = cls.shape
    n_rows = n_blocks * ROW_BLOCK
    assert xs_init.shape == (n_rows * SUBLANES, LANES) and xs_init.dtype == jnp.uint32
    smem_full = pl.BlockSpec(memory_space=pltpu.SMEM)
    smem_tile = pl.BlockSpec((1, 1, ts), lambda i: (i, 0, 0), memory_space=pltpu.SMEM)
    any_spec = pl.BlockSpec(memory_space=pl.ANY)
    return pl.pallas_call(
        functools.partial(_dispatch_kernel, n_tiles, n_blocks),
        out_shape=(
            jax.ShapeDtypeStruct((n_rows * SUBLANES, LANES), jnp.uint32),
            jax.ShapeDtypeStruct((n_tiles, 1, ts), jnp.int32),
            jax.ShapeDtypeStruct((n_blocks,), jnp.int32),
            jax.ShapeDtypeStruct((n_blocks,), jnp.int32),
            jax.ShapeDtypeStruct((1,), jnp.int32),
        ),
        grid=(n_tiles,),
        in_specs=[
            smem_tile, smem_tile,
            smem_full, smem_full, smem_full, any_spec, any_spec,
        ],
        out_specs=(
            any_spec,
            smem_tile,
            smem_full, smem_full, smem_full,
        ),
        scratch_shapes=[
            pltpu.SMEM((LANES,), jnp.int32),
            pltpu.VMEM((DISPATCH_SLOTS, ts * SUBLANES, LANES), jnp.uint32),
            pltpu.SemaphoreType.DMA((DISPATCH_SLOTS,)),
            pltpu.SemaphoreType.DMA((DISPATCH_SLOTS,)),
        ],
        input_output_aliases={6: 0},
        compiler_params=pltpu.CompilerParams(
            dimension_semantics=("arbitrary",), vmem_limit_bytes=VMEM_LIMIT_BYTES),
        name="moe_dispatch",
    )(cls, rank, counts, jnp.asarray(_CLASS_LO), jnp.asarray(_CLASS_HI), hrows, xs_init)


def _expert_kernel(blo_ref, bhi_ref, nblk_ref, xs_ref, wg_lo, wu_lo, wd_lo, wg_hi, wu_hi, wd_hi, ys_ref):
    del blo_ref, bhi_ref
    b = pl.program_id(0)
    rb = ROW_BLOCK

    @pl.when(b < nblk_ref[0])
    def _():
        pieces = []
        for j in range(PACK_SUBLANES):
            w = _token_rows(xs_ref, j, rb)
            pieces.append(_bf16(_f32_from_bits(w & jnp.uint32(0xFFFF0000))))
            pieces.append(_bf16(_f32_from_bits(lax.shift_left(w, jnp.uint32(16)))))
        xb = jnp.concatenate(pieces, axis=1)
        gates = _f32_from_bits(_token_rows(xs_ref, PACK_SUBLANES, rb))
        acc = None
        for idx, (wg, wu, wd) in enumerate(((wg_lo, wu_lo, wd_lo), (wg_hi, wu_hi, wd_hi))):
            a = _dot(xb, wg[...])
            hid = (a * jax.nn.sigmoid(a)) * _dot(xb, wu[...]) * gates[:, idx:idx + 1]
            part = _dot(_bf16(hid), wd[...])
            acc = part if acc is None else acc + part
        for s in range(SUBLANES):
            ys_ref[pl.ds(s, rb, stride=SUBLANES), :] = acc[:, s * LANES:(s + 1) * LANES]

    @pl.when(b >= nblk_ref[0])
    def _():
        ys_ref[...] = jnp.zeros_like(ys_ref)


def _expert_call(blo, bhi, nblk, xs, wg, wu, wd, layer):
    n_rows = xs.shape[0] // SUBLANES
    d, de = wg.shape[2], wg.shape[3]
    assert d == SUBLANES * LANES
    n_blocks = n_rows // ROW_BLOCK

    def w_spec(shape, which):
        if which == 0:
            return pl.BlockSpec((None, None) + shape, lambda b, lo, hi, n: (layer, lo[b], 0, 0))
        return pl.BlockSpec((None, None) + shape, lambda b, lo, hi, n: (layer, hi[b], 0, 0))

    def x_map(b, lo, hi, n):
        return (jnp.minimum(b, jnp.maximum(n[0] - 1, 0)), 0)

    grid_spec = pltpu.PrefetchScalarGridSpec(
        num_scalar_prefetch=3,
        grid=(n_blocks,),
        in_specs=[
            pl.BlockSpec((ROW_BLOCK * SUBLANES, LANES), x_map),
            w_spec((d, de), 0), w_spec((d, de), 0), w_spec((de, d), 0),
            w_spec((d, de), 1), w_spec((d, de), 1), w_spec((de, d), 1),
        ],
        out_specs=pl.BlockSpec((ROW_BLOCK * SUBLANES, LANES), lambda b, lo, hi, n: (b, 0)),
    )
    return pl.pallas_call(
        _expert_kernel,
        out_shape=jax.ShapeDtypeStruct((n_rows * SUBLANES, LANES), jnp.float32),
        grid_spec=grid_spec,
        compiler_params=pltpu.CompilerParams(
            dimension_semantics=("arbitrary",), vmem_limit_bytes=VMEM_LIMIT_BYTES),
        name="moe_experts",
    )(blo, bhi, nblk, xs, wg, wu, wd, wg, wu, wd)


def _final_kernel(n_tiles, dest_ref, dnext_ref, x1_ref, mod_ref, fmod_ref, fg_ref, ys_ref,
                  out_ref, ybuf, sem, unperm):
    i = pl.program_id(0)
    ts, d = x1_ref.shape
    bsz, tm, _ = out_ref.shape
    y = _gather_expert_tiles(ys_ref, ybuf, sem, dest_ref, dnext_ref, i, n_tiles, ts)
    x2 = x1_ref[...] + _per_batch(mod_ref[5], ts) * y
    res = _rms_mod(x2, fg_ref[...], fmod_ref[0], fmod_ref[1])
    for cb in range(d // LANES):
        unperm[cb] = res[:, cb * LANES:(cb + 1) * LANES]
    for cb in range(d // LANES):
        for p in range(bsz):
            out_ref[p, :, cb * LANES:(cb + 1) * LANES] = unperm.at[cb][pl.ds(p, tm, stride=SUBLANES), :]


def _final_call(dest, x1, mod, ys, bsz, fmod, fin_g):
    t, d = x1.shape
    n_tiles, _, ts = dest.shape
    tm = ts // bsz
    return pl.pallas_call(
        functools.partial(_final_kernel, n_tiles),
        out_shape=jax.ShapeDtypeStruct((bsz, t // bsz, d), jnp.float32),
        grid=(n_tiles,),
        in_specs=[
            pl.BlockSpec((1, 1, ts), lambda i: (i, 0, 0), memory_space=pltpu.SMEM),
            pl.BlockSpec((1, 1, ts), lambda i: (jnp.minimum(i + 1, n_tiles - 1), 0, 0),
                         memory_space=pltpu.SMEM),
            pl.BlockSpec((ts, d), lambda i: (i, 0)),
            _full((MOD_ROWS, bsz, d)), _full((MOD_ROWS, bsz, d)), _full((1, d)),
            pl.BlockSpec(memory_space=pl.ANY),
        ],
        out_specs=pl.BlockSpec((bsz, tm, d), lambda i: (0, i, 0)),
        scratch_shapes=[pltpu.VMEM((2, ts * SUBLANES, LANES), jnp.float32),
                        pltpu.SemaphoreType.DMA((2,)),
                        pltpu.VMEM((d // LANES, ts, LANES), jnp.float32)],
        compiler_params=pltpu.CompilerParams(
            dimension_semantics=("arbitrary",), vmem_limit_bytes=VMEM_LIMIT_BYTES),
        name="moe_combine_final",
    )(dest, dest, x1, mod, fmod, fin_g, ys)


def _router_operands(wg, bg, wr, br):
    d = wg.shape[0]
    pad_g = jnp.zeros((SUBLANES - N_GROUPS, d), jnp.float32)
    pad_e = jnp.zeros((ROUTER_ROWS - SUBLANES - N_EXPERTS, d), jnp.float32)
    wrt = jnp.concatenate([wg.T, pad_g, wr.T, pad_e], axis=0)
    bcol = jnp.concatenate([bg, jnp.zeros((SUBLANES - N_GROUPS,), jnp.float32), br,
                            jnp.zeros((ROUTER_ROWS - SUBLANES - N_EXPERTS,), jnp.float32)])
    brt = jnp.broadcast_to(bcol[:, None], (ROUTER_ROWS, LANES))
    return _bf16(wrt), brt


def _mod_table(m, parts):
    bsz = m.shape[0]
    d = m.shape[1] // parts
    m = jnp.transpose(m.reshape(bsz, parts, d), (1, 0, 2))
    return jnp.concatenate([m, jnp.zeros((MOD_ROWS - parts, bsz, d), jnp.float32)], axis=0)


def _row(v):
    return v.reshape(1, -1)


def _moe_experts(hrows, cls, rank, cnt, wg, wu, wd, layer, n_blocks, xs_init):
    counts = cnt[:, 0].astype(jnp.int32)
    xs, dest, blo, bhi, nblk = _dispatch_call(cls, rank, counts, hrows, n_blocks, xs_init)
    ys = _expert_call(blo, bhi, nblk, xs, wg, wu, wd, layer)
    return xs, ys, dest


def kernel(x, c, ada_w, ada_b, norm1_g, norm2_g, cm_w1, cm_b1, cm_dw, cm_dwb, cm_ln_g, cm_ln_b, cm_w2, cm_b2, rg_wy, rg_by, rg_wx, rg_bx, rg_cw, rg_cb, rg_wa, rg_ba, rg_wi, rg_bi, rg_lambda, rg_wo, rg_bo, moe_wg, moe_bg, moe_wr, moe_br, moe_w_gate, moe_w_up, moe_w_down, fin_ada_w, fin_ada_b, fin_g):
    bsz, seq, d = x.shape
    t = bsz * seq
    depth = ada_w.shape[0]
    assert depth == 2 and bsz == SUBLANES and seq % TIME_TILE == 0 and t % ROW_BLOCK == 0
    assert d == 2 * PACK_SUBLANES * LANES

    mods = _ada_call(c, ada_w, ada_b[:, None, :])
    fmods = _ada_call(c, fin_ada_w[None], fin_ada_b[None, None, :])
    fmod = _mod_table(fmods[0], 2)
    n_steps = seq // TIME_TILE
    granule = n_steps // math.gcd(n_steps, ROW_BLOCK)
    n_blocks = -(-(t // ROW_BLOCK + N_CLASSES) // granule) * granule
    mod0, mod1 = _mod_table(mods[0], 6), _mod_table(mods[1], 6)
    moe_w_gate, moe_w_up, moe_w_down = _bf16(moe_w_gate), _bf16(moe_w_up), _bf16(moe_w_down)

    wrt, brt = _router_operands(moe_wg[0], moe_bg[0], moe_wr[0], moe_br[0])
    dw = jnp.concatenate([cm_dw[0], jnp.zeros((1, d), jnp.float32)], axis=0)
    dw = jnp.transpose(dw.reshape(CONF_KERNEL + 1, d // LANES, LANES), (1, 0, 2))
    x1, hrows, cls, rank, cnt, xs_init = _conf_call(
        x, mod0, _row(norm1_g[0]), _row(norm2_g[0]), _bf16(cm_w1[0]), _row(cm_b1[0]),
        dw, cm_dwb[0].reshape(d // LANES, 1, LANES), _row(cm_ln_g[0]), _row(cm_ln_b[0]),
        _bf16(cm_w2[0]), _row(cm_b2[0]), wrt, brt, n_blocks * ROW_BLOCK * SUBLANES)
    xs, ys, dest = _moe_experts(hrows, cls, rank, cnt, moe_w_gate, moe_w_up, moe_w_down, 0,
                                n_blocks, xs_init)

    wrt, brt = _router_operands(moe_wg[1], moe_bg[1], moe_wr[1], moe_br[1])
    x1, hrows, cls, rank, cnt = _lru_call(
        dest, x1, mod0, ys, mod1, _row(norm1_g[1]), _row(norm2_g[1]),
        _bf16(rg_wy[0]), _row(rg_by[0]), _bf16(rg_wx[0]), _row(rg_bx[0]),
        rg_cw[0], _row(rg_cb[0]), _bf16(rg_wa[0]), _row(rg_ba[0]), _bf16(rg_wi[0]), _row(rg_bi[0]),
        _row(rg_lambda[0]), _bf16(rg_wo[0]), _row(rg_bo[0]), wrt, brt)
    _, ys, dest = _moe_experts(hrows, cls, rank, cnt, moe_w_gate, moe_w_up, moe_w_down, 1,
                               n_blocks, xs)
    return _final_call(dest, x1, mod1, ys, bsz, fmod, _row(fin_g))
```

```python
import functools
import math

import jax
import jax.numpy as jnp
import numpy as np
from jax import lax
from jax.experimental import pallas as pl
from jax.experimental.pallas import tpu as pltpu

EPS = 1e-6
CONF_KERNEL = 31
LRU_HEADS = 4
LRU_CONV = 4
LRU_C = 8.0
N_GROUPS = 4
EXPERTS_PER_GROUP = 8
N_EXPERTS = N_GROUPS * EXPERTS_PER_GROUP
PAIRS_PER_GROUP = EXPERTS_PER_GROUP * (EXPERTS_PER_GROUP - 1) // 2
N_CLASSES = N_GROUPS * PAIRS_PER_GROUP

LANES = 128
SUBLANES = 8
VMEM_LIMIT_BYTES = 56 * 1024 * 1024

TOKEN_TILE = 512
TIME_TILE = TOKEN_TILE // SUBLANES
SUB_TILE = 256
ROW_BLOCK = 320
PERM_PITCH = TIME_TILE + SUBLANES
CONV_ROWS = 128
ROUTER_ROWS = 64
PACK_SUBLANES = 4
MOD_ROWS = 8
DISPATCH_SLOTS = 3
DMA_UNROLL = 8

_NEG_INF = float("-inf")


def _class_tables():
    lo = np.zeros((N_CLASSES,), np.int32)
    hi = np.zeros((N_CLASSES,), np.int32)
    for g in range(N_GROUPS):
        for a in range(EXPERTS_PER_GROUP):
            for b in range(a + 1, EXPERTS_PER_GROUP):
                c = g * PAIRS_PER_GROUP + (a * (2 * EXPERTS_PER_GROUP - 1 - a)) // 2 + (b - a - 1)
                lo[c] = g * EXPERTS_PER_GROUP + a
                hi[c] = g * EXPERTS_PER_GROUP + b
    return lo, hi


_CLASS_LO, _CLASS_HI = _class_tables()


def _bf16(x):
    return x.astype(jnp.bfloat16)


def _dot(a, b):
    return jnp.dot(a, b, preferred_element_type=jnp.float32)


def _bits(x):
    return lax.bitcast_convert_type(x, jnp.uint32)


def _f32_from_bits(x):
    return lax.bitcast_convert_type(x, jnp.float32)


def _token_rows(ref, sublane, n):
    return ref[pl.ds(sublane, n, stride=SUBLANES), :]


def _per_batch(v, ts):
    return jnp.tile(v, (ts // SUBLANES, 1))


def _ada_kernel(c_ref, w_ref, b_ref, o_ref):
    c = c_ref[...]
    ca = _bf16(c * jax.nn.sigmoid(c))
    o_ref[...] = _dot(ca, _bf16(w_ref[...])) + b_ref[...]


def _ada_call(c, w, b):
    n_layers, d, n = w.shape
    bsz = c.shape[0]
    nb = 1024
    return pl.pallas_call(
        _ada_kernel,
        out_shape=jax.ShapeDtypeStruct((n_layers, bsz, n), jnp.float32),
        grid=(n_layers, n // nb),
        in_specs=[
            pl.BlockSpec((bsz, d), lambda l, j: (0, 0)),
            pl.BlockSpec((None, d, nb), lambda l, j: (l, 0, j)),
            pl.BlockSpec((None, 1, nb), lambda l, j: (l, 0, j)),
        ],
        out_specs=pl.BlockSpec((None, bsz, nb), lambda l, j: (l, 0, j)),
        compiler_params=pltpu.CompilerParams(
            dimension_semantics=("arbitrary", "arbitrary"),
            vmem_limit_bytes=VMEM_LIMIT_BYTES),
        name="ada_mod",
    )(c, w, b)


def _rms_mod(x, gain, shift, scale):
    ts = x.shape[0]
    ms = jnp.mean(x * x, axis=-1, keepdims=True)
    return (x * lax.rsqrt(ms + EPS)) * _per_batch(gain * (1.0 + scale), ts) + _per_batch(shift, ts)


def _causal_taps(w_ref, bias_ref, ext_ref, n_taps, row0, n):
    d = ext_ref.shape[1]
    cols = []
    for cb in range(d // LANES):
        sl = slice(cb * LANES, (cb + 1) * LANES)
        acc = jnp.broadcast_to(bias_ref[:, sl], (n, LANES))
        for k in range(n_taps):
            lo = row0 + SUBLANES * k
            acc = acc + w_ref[k:k + 1, sl] * ext_ref[lo:lo + n, sl]
        cols.append(acc)
    return jnp.concatenate(cols, axis=1)


def _causal_taps_chunked(w_ref, bias_ref, ext_ref, out_ref, n_taps, ts):
    n_chunks = ext_ref.shape[0]
    units_per_chunk = ts // CONV_ROWS

    def unit(idx, carry):
        cb = idx // units_per_chunk
        r0 = pl.multiple_of((idx % units_per_chunk) * CONV_ROWS, CONV_ROWS)
        w_cb, ext_cb = w_ref.at[cb], ext_ref.at[cb]
        acc = jnp.broadcast_to(bias_ref[cb], (CONV_ROWS, LANES))
        for k in range(n_taps):
            win = ext_cb[pl.ds(pl.multiple_of(r0 + SUBLANES * k, SUBLANES), CONV_ROWS), :]
            acc = acc + w_cb[k:k + 1, :] * win
        out_ref.at[cb][pl.ds(r0, CONV_ROWS), :] = acc
        return carry

    lax.fori_loop(0, n_chunks * units_per_chunk, unit, 0)


def _start_tile_dmas(make_copy, n):
    def body(r, carry):
        make_copy(r).start()
        return carry
    lax.fori_loop(0, n, body, 0, unroll=DMA_UNROLL)


def _wait_all_tiles(hbm_ref, vmem_tiles_ref, sem, to_hbm):
    hbm_view = hbm_ref.at[pl.ds(0, vmem_tiles_ref.shape[0])]
    if to_hbm:
        pltpu.make_async_copy(vmem_tiles_ref, hbm_view, sem).wait()
    else:
        pltpu.make_async_copy(hbm_view, vmem_tiles_ref, sem).wait()


def _gather_expert_tiles(ys_ref, ybuf, sem, dest_ref, dnext_ref, step, n_steps, ts):
    slot = step % 2

    def tile_copy(dref, sl):
        def make(r):
            src = pl.multiple_of(dref[0, 0, r] * SUBLANES, SUBLANES)
            return pltpu.make_async_copy(
                ys_ref.at[pl.ds(src, SUBLANES)],
                ybuf.at[sl, pl.ds(pl.multiple_of(r * SUBLANES, SUBLANES), SUBLANES)],
                sem.at[sl])
        return make

    @pl.when(step == 0)
    def _():
        _start_tile_dmas(tile_copy(dest_ref, 0), ts)

    @pl.when(step + 1 < n_steps)
    def _():
        _start_tile_dmas(tile_copy(dnext_ref, 1 - slot), ts)

    _wait_all_tiles(ys_ref, ybuf.at[slot], sem.at[slot], to_hbm=False)
    yv = ybuf.at[slot]
    return jnp.concatenate([_token_rows(yv, s, ts) for s in range(SUBLANES)], axis=1)


def _route(logits_t):
    ts = logits_t.shape[1]
    row = lax.broadcasted_iota(jnp.int32, (SUBLANES, ts), 0)
    gl = jnp.where(row < N_GROUPS, logits_t[0:SUBLANES], _NEG_INF)
    gmax = jnp.max(gl, axis=0, keepdims=True)
    gsel = jnp.min(jnp.where(gl == gmax, row, SUBLANES), axis=0, keepdims=True)
    p_g = 1.0 / jnp.sum(jnp.exp(gl - gmax), axis=0, keepdims=True)
    es = jnp.zeros((SUBLANES, ts), jnp.float32)
    for g in range(N_GROUPS):
        lo = SUBLANES * (1 + g)
        es = jnp.where(gsel == g, logits_t[lo:lo + EXPERTS_PER_GROUP], es)
    m1 = jnp.max(es, axis=0, keepdims=True)
    i1 = jnp.min(jnp.where(es == m1, row, SUBLANES), axis=0, keepdims=True)
    es2 = jnp.where(row == i1, _NEG_INF, es)
    m2 = jnp.max(es2, axis=0, keepdims=True)
    i2 = jnp.min(jnp.where(es2 == m2, row, SUBLANES), axis=0, keepdims=True)
    t = jnp.exp(m2 - m1)
    w1 = p_g / (1.0 + t)
    w2 = w1 * t
    first_lo = i1 < i2
    e_lo = jnp.minimum(i1, i2)
    e_hi = jnp.maximum(i1, i2)
    gate_lo = jnp.where(first_lo, w1, w2)
    gate_hi = jnp.where(first_lo, w2, w1)
    pair = lax.shift_right_logical(e_lo * (2 * EXPERTS_PER_GROUP - 1 - e_lo), 1) + (e_hi - e_lo - 1)
    cls = gsel * PAIRS_PER_GROUP + pair
    return cls, gate_lo, gate_hi


def _post_mixer(x1, mod_ref, n2_ref, wrt_ref, brt_ref, hrow_ref, cls_ref, rank_ref, cnt_ref, sub):
    ts = x1.shape[0]
    hrow_ref = hrow_ref.at[pl.ds(sub * ts * SUBLANES, ts * SUBLANES)]
    cols = slice(sub * ts, (sub + 1) * ts)
    h2 = _rms_mod(x1, n2_ref[...], mod_ref[3], mod_ref[4])
    h2b = _bf16(h2)
    logits_t = lax.dot_general(wrt_ref[...], h2b, (((1,), (1,)), ((), ())),
                               preferred_element_type=jnp.float32) + brt_ref[:, 0:1]
    cls, gate_lo, gate_hi = _route(logits_t)
    gates_t = jnp.concatenate(
        [gate_lo, gate_hi, jnp.zeros((LANES - 2, ts), jnp.float32)], axis=0)
    h2r = h2b.astype(jnp.float32)
    for j in range(PACK_SUBLANES):
        hi = _bits(h2r[:, (2 * j) * LANES:(2 * j + 1) * LANES])
        lo = _bits(h2r[:, (2 * j + 1) * LANES:(2 * j + 2) * LANES])
        hrow_ref[pl.ds(j, ts, stride=SUBLANES), :] = hi | lax.shift_right_logical(lo, jnp.uint32(16))
    hrow_ref[pl.ds(PACK_SUBLANES, ts, stride=SUBLANES), :] = _bits(gates_t.T)
    for j in range(PACK_SUBLANES + 1, SUBLANES):
        hrow_ref[pl.ds(j, ts, stride=SUBLANES), :] = jnp.zeros((ts, LANES), jnp.uint32)
    cls_ref[0, :, cols] = cls
    crow = lax.broadcasted_iota(jnp.int32, (LANES, ts), 0)
    onehot = jnp.where(crow == cls, 1.0, 0.0)
    cnt = jnp.sum(onehot, axis=1, keepdims=True)
    earlier = (lax.broadcasted_iota(jnp.int32, (ts, ts), 0)
               < lax.broadcasted_iota(jnp.int32, (ts, ts), 1))
    prefix = _dot(_bf16(onehot), _bf16(jnp.where(earlier, 1.0, 0.0)))
    rank = jnp.sum(onehot * (prefix + cnt_ref[:, 0:1]), axis=0, keepdims=True)
    rank_ref[0, :, cols] = rank.astype(jnp.int32)
    cnt_ref[...] += jnp.broadcast_to(cnt, cnt_ref.shape)


def _mixer_out(t, d, ts):
    n_tiles = t // ts
    out_shape = (
        jax.ShapeDtypeStruct((t, d), jnp.float32),
        jax.ShapeDtypeStruct((t * SUBLANES, LANES), jnp.uint32),
        jax.ShapeDtypeStruct((n_tiles, 1, ts), jnp.int32),
        jax.ShapeDtypeStruct((n_tiles, 1, ts), jnp.int32),
        jax.ShapeDtypeStruct((LANES, LANES), jnp.float32),
    )
    out_specs = (
        pl.BlockSpec((ts, d), lambda i: (i, 0)),
        pl.BlockSpec((ts * SUBLANES, LANES), lambda i: (i, 0)),
        pl.BlockSpec((1, 1, ts), lambda i: (i, 0, 0)),
        pl.BlockSpec((1, 1, ts), lambda i: (i, 0, 0)),
        pl.BlockSpec((LANES, LANES), lambda i: (0, 0)),
    )
    return out_shape, out_specs


def _full(shape):
    return pl.BlockSpec(shape, lambda i: tuple(0 for _ in shape))


def _conf_kernel(x_ref, mod_ref, n1_ref, n2_ref, w1_ref, b1_ref, dw_ref, dwb_ref,
                 lng_ref, lnb_ref, w2_ref, b2_ref, wrt_ref, brt_ref,
                 x1_ref, hrow_ref, cls_ref, rank_ref, cnt_ref, zeros_ref, perm, xs, vext, conv):
    i = pl.program_id(0)
    bsz, tm, d = x_ref.shape
    ts = bsz * tm
    n_chunks = d // LANES
    hist = SUBLANES * (CONF_KERNEL - 1)

    @pl.when(i == 0)
    def _():
        vext[:, 0:hist, :] = jnp.zeros((n_chunks, hist, LANES), jnp.float32)
        cnt_ref[...] = jnp.zeros_like(cnt_ref)

    zeros_ref[...] = jnp.zeros_like(zeros_ref)

    for cb in range(n_chunks):
        for p in range(bsz):
            perm[cb, PERM_PITCH * p:PERM_PITCH * p + tm, :] = x_ref[p, :, cb * LANES:(cb + 1) * LANES]
    for m in range(tm):
        for cb in range(n_chunks):
            xs[m * bsz:(m + 1) * bsz, cb * LANES:(cb + 1) * LANES] = (
                perm.at[cb][pl.ds(m, bsz, stride=PERM_PITCH), :])

    for sub in range(ts // SUB_TILE):
        rows = slice(sub * SUB_TILE, (sub + 1) * SUB_TILE)
        h = _bf16(_rms_mod(xs[rows, :], n1_ref[...], mod_ref[0], mod_ref[1]))
        u = _dot(h, w1_ref[...]) + b1_ref[...]
        v = u[:, 0:d] * jax.nn.sigmoid(u[:, d:2 * d])
        for cb in range(n_chunks):
            vext[cb, hist + sub * SUB_TILE:hist + (sub + 1) * SUB_TILE, :] = v[:, cb * LANES:(cb + 1) * LANES]

    _causal_taps_chunked(dw_ref, dwb_ref, vext, conv, CONF_KERNEL, ts)
    vext[:, 0:hist, :] = vext[:, ts:ts + hist, :]

    for sub in range(ts // SUB_TILE):
        rows = slice(sub * SUB_TILE, (sub + 1) * SUB_TILE)
        y = jnp.concatenate([conv[cb, rows, :] for cb in range(n_chunks)], axis=1)
        mu = jnp.mean(y, axis=-1, keepdims=True)
        yc = y - mu
        var = jnp.mean(yc * yc, axis=-1, keepdims=True)
        z = yc * lax.rsqrt(var + EPS) * lng_ref[...] + lnb_ref[...]
        z = z * jax.nn.sigmoid(z)
        mix = _dot(_bf16(z), w2_ref[...]) + b2_ref[...]
        x1 = xs[rows, :] + _per_batch(mod_ref[2], SUB_TILE) * mix
        x1_ref[rows, :] = x1
        _post_mixer(x1, mod_ref, n2_ref, wrt_ref, brt_ref, hrow_ref, cls_ref, rank_ref, cnt_ref, sub)


def _conf_call(x, mod, n1, n2, w1, b1, dw, dwb, lng, lnb, w2, b2, wrt, brt, zero_rows):
    bsz, seq, d = x.shape
    tm = TIME_TILE
    ts = bsz * tm
    n_steps = seq // tm
    assert zero_rows % (n_steps * SUBLANES) == 0
    out_shape, out_specs = _mixer_out(bsz * seq, d, ts)
    out_shape += (jax.ShapeDtypeStruct((zero_rows, LANES), jnp.uint32),)
    out_specs += (pl.BlockSpec((zero_rows // n_steps, LANES), lambda i: (i, 0)),)
    hist = SUBLANES * (CONF_KERNEL - 1)
    return pl.pallas_call(
        _conf_kernel,
        out_shape=out_shape,
        grid=(n_steps,),
        in_specs=[
            pl.BlockSpec((bsz, tm, d), lambda i: (0, i, 0)),
            _full((MOD_ROWS, bsz, d)),
            _full((1, d)), _full((1, d)),
            _full((d, 2 * d)), _full((1, 2 * d)),
            _full((d // LANES, CONF_KERNEL + 1, LANES)), _full((d // LANES, 1, LANES)),
            _full((1, d)), _full((1, d)),
            _full((d, d)), _full((1, d)),
            _full((ROUTER_ROWS, d)), _full((ROUTER_ROWS, LANES)),
        ],
        out_specs=out_specs,
        scratch_shapes=[
            pltpu.VMEM((d // LANES, bsz * PERM_PITCH, LANES), jnp.float32),
            pltpu.VMEM((ts, d), jnp.float32),
            pltpu.VMEM((d // LANES, hist + ts, LANES), jnp.float32),
            pltpu.VMEM((d // LANES, ts, LANES), jnp.float32),
        ],
        compiler_params=pltpu.CompilerParams(
            dimension_semantics=("arbitrary",), vmem_limit_bytes=VMEM_LIMIT_BYTES),
        name="conf_mixer",
    )(x, mod, n1, n2, w1, b1, dw, dwb, lng, lnb, w2, b2, wrt, brt)


def _lru_kernel(n_tiles, dest_ref, dnext_ref, xprev_ref, pmod_ref, mod_ref, n1_ref, n2_ref,
                wy_ref, by_ref, wx_ref, bx_ref,
                cw_ref, cb_ref, wa_ref, ba_ref, wi_ref, bi_ref, lam_ref, wo_ref, bo_ref,
                wrt_ref, brt_ref, ys_ref,
                x1_ref, hrow_ref, cls_ref, rank_ref, cnt_ref, uext, hcar, ybuf, ysem):
    i = pl.program_id(0)
    ts, d = xprev_ref.shape
    hd = d // LRU_HEADS
    hist = SUBLANES * (LRU_CONV - 1)

    @pl.when(i == 0)
    def _():
        uext[0:hist, :] = jnp.zeros((hist, d), jnp.float32)
        hcar[...] = jnp.zeros_like(hcar)
        cnt_ref[...] = jnp.zeros_like(cnt_ref)

    y_prev = _gather_expert_tiles(ys_ref, ybuf, ysem, dest_ref, dnext_ref, i, n_tiles, ts)
    nl = -lam_ref[...]
    softplus = jnp.maximum(nl, 0.0) + jnp.log1p(jnp.exp(-jnp.abs(nl)))
    hstate = hcar[...]
    for sub in range(ts // SUB_TILE):
        rows = slice(sub * SUB_TILE, (sub + 1) * SUB_TILE)
        x = xprev_ref[rows, :] + _per_batch(pmod_ref[5], SUB_TILE) * y_prev[rows, :]
        h = _bf16(_rms_mod(x, n1_ref[...], mod_ref[0], mod_ref[1]))
        ygate = jax.nn.gelu(_dot(h, wy_ref[...]) + by_ref[...])
        xb = _dot(h, wx_ref[...]) + bx_ref[...]
        uext[hist + sub * SUB_TILE:hist + (sub + 1) * SUB_TILE, :] = xb
        u = _causal_taps(cw_ref, cb_ref, uext, LRU_CONV, sub * SUB_TILE, SUB_TILE)

        ub = _bf16(u)
        r_parts, i_parts = [], []
        for hh in range(LRU_HEADS):
            uh = ub[:, hh * hd:(hh + 1) * hd]
            r_parts.append(_dot(uh, wa_ref[hh]))
            i_parts.append(_dot(uh, wi_ref[hh]))
        r = jax.nn.sigmoid(jnp.concatenate(r_parts, axis=1) + ba_ref[...])
        ig = jax.nn.sigmoid(jnp.concatenate(i_parts, axis=1) + bi_ref[...])

        log_a = (-LRU_C) * r * softplus
        a = jnp.exp(log_a)
        mult = jnp.sqrt(-jnp.tanh(log_a) * (1.0 + a * a))
        b = mult * (ig * u)

        outs = []
        for m in range(SUB_TILE // SUBLANES):
            sl = slice(m * SUBLANES, (m + 1) * SUBLANES)
            hstate = a[sl] * hstate + b[sl]
            outs.append(hstate)
        hs = jnp.concatenate(outs, axis=0)

        mix = _dot(_bf16(hs * ygate), wo_ref[...]) + bo_ref[...]
        x1 = x + _per_batch(mod_ref[2], SUB_TILE) * mix
        x1_ref[rows, :] = x1
        _post_mixer(x1, mod_ref, n2_ref, wrt_ref, brt_ref, hrow_ref, cls_ref, rank_ref, cnt_ref, sub)
    hcar[...] = hstate
    uext[0:hist, :] = uext[ts:ts + hist, :]


def _lru_call(dest, x_prev, mod_prev, ys, mod, n1, n2, wy, by, wx, bx, cw, cb, wa, ba, wi, bi, lam,
              wo, bo, wrt, brt):
    t, d = x_prev.shape
    n_tiles, _, ts = dest.shape
    bsz = mod.shape[1]
    hd = d // LRU_HEADS
    hist = SUBLANES * (LRU_CONV - 1)
    out_shape, out_specs = _mixer_out(t, d, ts)
    return pl.pallas_call(
        functools.partial(_lru_kernel, n_tiles),
        out_shape=out_shape,
        grid=(n_tiles,),
        in_specs=[
            pl.BlockSpec((1, 1, ts), lambda i: (i, 0, 0), memory_space=pltpu.SMEM),
            pl.BlockSpec((1, 1, ts), lambda i: (jnp.minimum(i + 1, n_tiles - 1), 0, 0),
                         memory_space=pltpu.SMEM),
            pl.BlockSpec((ts, d), lambda i: (i, 0)),
            _full((MOD_ROWS, bsz, d)), _full((MOD_ROWS, bsz, d)),
            _full((1, d)), _full((1, d)),
            _full((d, d)), _full((1, d)),
            _full((d, d)), _full((1, d)),
            _full((LRU_CONV, d)), _full((1, d)),
            _full((LRU_HEADS, hd, hd)), _full((1, d)),
            _full((LRU_HEADS, hd, hd)), _full((1, d)),
            _full((1, d)),
            _full((d, d)), _full((1, d)),
            _full((ROUTER_ROWS, d)), _full((ROUTER_ROWS, LANES)),
            pl.BlockSpec(memory_space=pl.ANY),
        ],
        out_specs=out_specs,
        scratch_shapes=[
            pltpu.VMEM((hist + ts, d), jnp.float32),
            pltpu.VMEM((SUBLANES, d), jnp.float32),
            pltpu.VMEM((2, ts * SUBLANES, LANES), jnp.float32),
            pltpu.SemaphoreType.DMA((2,)),
        ],
        compiler_params=pltpu.CompilerParams(
            dimension_semantics=("arbitrary",), vmem_limit_bytes=VMEM_LIMIT_BYTES),
        name="lru_mixer",
    )(dest, dest, x_prev, mod_prev, mod, n1, n2, wy, by, wx, bx, cw, cb, wa, ba, wi, bi, lam, wo, bo,
      wrt, brt, ys)


def _dispatch_kernel(n_tiles, n_blocks, cls_ref, rank_ref, cnt_ref, clo_ref, chi_ref, hrow_ref, xs_in_ref,
                     xs_ref, dest_ref, blo_ref, bhi_ref, nblk_ref, class_start, stage, sem_in, sem_out):
    del xs_in_ref
    i = pl.program_id(0)
    ts = cls_ref.shape[2]
    rows = ts * SUBLANES

    @pl.when(i == 0)
    def _():
        def per_class(c, nb_total):
            class_start[c] = nb_total * ROW_BLOCK
            nb = (cnt_ref[c] + (ROW_BLOCK - 1)) // ROW_BLOCK

            def fill(k, carry):
                blo_ref[nb_total + k] = clo_ref[c]
                bhi_ref[nb_total + k] = chi_ref[c]
                return carry

            lax.fori_loop(0, nb, fill, 0)
            return nb_total + nb

        nb_used = lax.fori_loop(0, N_CLASSES, per_class, 0)
        nblk_ref[0] = nb_used
        last = jnp.maximum(nb_used - 1, 0)

        def tail(k, carry):
            blo_ref[k] = blo_ref[last]
            bhi_ref[k] = bhi_ref[last]
            return carry

        lax.fori_loop(nb_used, n_blocks, tail, 0)

    def load(tile, slot):
        return pltpu.make_async_copy(hrow_ref.at[pl.ds(pl.multiple_of(tile * rows, rows), rows)],
                                     stage.at[slot], sem_in.at[slot])

    def tile_copy(slot, r, dst_row):
        return pltpu.make_async_copy(
            stage.at[slot, pl.ds(pl.multiple_of(r * SUBLANES, SUBLANES), SUBLANES)],
            xs_ref.at[pl.ds(pl.multiple_of(dst_row * SUBLANES, SUBLANES), SUBLANES)],
            sem_out.at[slot])

    def drain(slot):
        _wait_all_tiles(xs_ref, stage.at[slot], sem_out.at[slot], to_hbm=True)

    slot = i % DISPATCH_SLOTS
    nxt = (i + 1) % DISPATCH_SLOTS

    @pl.when(i == 0)
    def _():
        load(0, 0).start()

    @pl.when(i >= 2)
    def _():
        drain(nxt)

    @pl.when(i + 1 < n_tiles)
    def _():
        load(i + 1, nxt).start()

    load(i, slot).wait()

    def scatter(r):
        dst_row = class_start[cls_ref[0, 0, r]] + rank_ref[0, 0, r]
        dest_ref[0, 0, r] = dst_row
        return tile_copy(slot, r, dst_row)

    _start_tile_dmas(scatter, ts)

    @pl.when(i == n_tiles - 1)
    def _():
        @pl.when(i >= 1)
        def _():
            drain((i + DISPATCH_SLOTS - 1) % DISPATCH_SLOTS)
        drain(slot)


def _dispatch_call(cls, rank, counts, hrows, n_blocks, xs_init):
    n_tiles, _, ts = cls.shape
    n_rows = n_blocks * ROW_BLOCK
    assert xs_init.shape == (n_rows * SUBLANES, LANES) and xs_init.dtype == jnp.uint32
    smem_full = pl.BlockSpec(memory_space=pltpu.SMEM)
    smem_tile = pl.BlockSpec((1, 1, ts), lambda i: (i, 0, 0), memory_space=pltpu.SMEM)
    any_spec = pl.BlockSpec(memory_space=pl.ANY)
    return pl.pallas_call(
        functools.partial(_dispatch_kernel, n_tiles, n_blocks),
        out_shape=(
            jax.ShapeDtypeStruct((n_rows * SUBLANES, LANES), jnp.uint32),
            jax.ShapeDtypeStruct((n_tiles, 1, ts), jnp.int32),
            jax.ShapeDtypeStruct((n_blocks,), jnp.int32),
            jax.ShapeDtypeStruct((n_blocks,), jnp.int32),
            jax.ShapeDtypeStruct((1,), jnp.int32),
        ),
        grid=(n_tiles,),
        in_specs=[
            smem_tile, smem_tile,
            smem_full, smem_full, smem_full, any_spec, any_spec,
        ],
        out_specs=(
            any_spec,
            smem_tile,
            smem_full, smem_full, smem_full,
        ),
        scratch_shapes=[
            pltpu.SMEM((LANES,), jnp.int32),
            pltpu.VMEM((DISPATCH_SLOTS, ts * SUBLANES, LANES), jnp.uint32),
            pltpu.SemaphoreType.DMA((DISPATCH_SLOTS,)),
            pltpu.SemaphoreType.DMA((DISPATCH_SLOTS,)),
        ],
        input_output_aliases={6: 0},
        compiler_params=pltpu.CompilerParams(
            dimension_semantics=("arbitrary",), vmem_limit_bytes=VMEM_LIMIT_BYTES),
        name="moe_dispatch",
    )(cls, rank, counts, jnp.asarray(_CLASS_LO), jnp.asarray(_CLASS_HI), hrows, xs_init)


def _expert_kernel(blo_ref, bhi_ref, nblk_ref, xs_ref, wg_lo, wu_lo, wd_lo, wg_hi, wu_hi, wd_hi, ys_ref):
    del blo_ref, bhi_ref
    b = pl.program_id(0)
    rb = ROW_BLOCK

    @pl.when(b < nblk_ref[0])
    def _():
        pieces = []
        for j in range(PACK_SUBLANES):
            w = _token_rows(xs_ref, j, rb)
            pieces.append(_bf16(_f32_from_bits(w & jnp.uint32(0xFFFF0000))))
            pieces.append(_bf16(_f32_from_bits(lax.shift_left(w, jnp.uint32(16)))))
        xb = jnp.concatenate(pieces, axis=1)
        gates = _f32_from_bits(_token_rows(xs_ref, PACK_SUBLANES, rb))
        acc = None
        for idx, (wg, wu, wd) in enumerate(((wg_lo, wu_lo, wd_lo), (wg_hi, wu_hi, wd_hi))):
            a = _dot(xb, wg[...])
            hid = (a * jax.nn.sigmoid(a)) * _dot(xb, wu[...]) * gates[:, idx:idx + 1]
            part = _dot(_bf16(hid), wd[...])
            acc = part if acc is None else acc + part
        for s in range(SUBLANES):
            ys_ref[pl.ds(s, rb, stride=SUBLANES), :] = acc[:, s * LANES:(s + 1) * LANES]

    @pl.when(b >= nblk_ref[0])
    def _():
        ys_ref[...] = jnp.zeros_like(ys_ref)


def _expert_call(blo, bhi, nblk, xs, wg, wu, wd, layer):
    n_rows = xs.shape[0] // SUBLANES
    d, de = wg.shape[2], wg.shape[3]
    assert d == SUBLANES * LANES
    n_blocks = n_rows // ROW_BLOCK

    def w_spec(shape, which):
        if which == 0:
            return pl.BlockSpec((None, None) + shape, lambda b, lo, hi, n: (layer, lo[b], 0, 0))
        return pl.BlockSpec((None, None) + shape, lambda b, lo, hi, n: (layer, hi[b], 0, 0))

    def x_map(b, lo, hi, n):
        return (jnp.minimum(b, jnp.maximum(n[0] - 1, 0)), 0)

    grid_spec = pltpu.PrefetchScalarGridSpec(
        num_scalar_prefetch=3,
        grid=(n_blocks,),
        in_specs=[
            pl.BlockSpec((ROW_BLOCK * SUBLANES, LANES), x_map),
            w_spec((d, de), 0), w_spec((d, de), 0), w_spec((de, d), 0),
            w_spec((d, de), 1), w_spec((d, de), 1), w_spec((de, d), 1),
        ],
        out_specs=pl.BlockSpec((ROW_BLOCK * SUBLANES, LANES), lambda b, lo, hi, n: (b, 0)),
    )
    return pl.pallas_call(
        _expert_kernel,
        out_shape=jax.ShapeDtypeStruct((n_rows * SUBLANES, LANES), jnp.float32),
        grid_spec=grid_spec,
        compiler_params=pltpu.CompilerParams(
            dimension_semantics=("arbitrary",), vmem_limit_bytes=VMEM_LIMIT_BYTES),
        name="moe_experts",
    )(blo, bhi, nblk, xs, wg, wu, wd, wg, wu, wd)


def _final_kernel(n_tiles, dest_ref, dnext_ref, x1_ref, mod_ref, fmod_ref, fg_ref, ys_ref,
                  out_ref, ybuf, sem, unperm):
    i = pl.program_id(0)
    ts, d = x1_ref.shape
    bsz, tm, _ = out_ref.shape
    y = _gather_expert_tiles(ys_ref, ybuf, sem, dest_ref, dnext_ref, i, n_tiles, ts)
    x2 = x1_ref[...] + _per_batch(mod_ref[5], ts) * y
    res = _rms_mod(x2, fg_ref[...], fmod_ref[0], fmod_ref[1])
    for cb in range(d // LANES):
        unperm[cb] = res[:, cb * LANES:(cb + 1) * LANES]
    for cb in range(d // LANES):
        for p in range(bsz):
            out_ref[p, :, cb * LANES:(cb + 1) * LANES] = unperm.at[cb][pl.ds(p, tm, stride=SUBLANES), :]


def _final_call(dest, x1, mod, ys, bsz, fmod, fin_g):
    t, d = x1.shape
    n_tiles, _, ts = dest.shape
    tm = ts // bsz
    return pl.pallas_call(
        functools.partial(_final_kernel, n_tiles),
        out_shape=jax.ShapeDtypeStruct((bsz, t // bsz, d), jnp.float32),
        grid=(n_tiles,),
        in_specs=[
            pl.BlockSpec((1, 1, ts), lambda i: (i, 0, 0), memory_space=pltpu.SMEM),
            pl.BlockSpec((1, 1, ts), lambda i: (jnp.minimum(i + 1, n_tiles - 1), 0, 0),
                         memory_space=pltpu.SMEM),
            pl.BlockSpec((ts, d), lambda i: (i, 0)),
            _full((MOD_ROWS, bsz, d)), _full((MOD_ROWS, bsz, d)), _full((1, d)),
            pl.BlockSpec(memory_space=pl.ANY),
        ],
        out_specs=pl.BlockSpec((bsz, tm, d), lambda i: (0, i, 0)),
        scratch_shapes=[pltpu.VMEM((2, ts * SUBLANES, LANES), jnp.float32),
                        pltpu.SemaphoreType.DMA((2,)),
                        pltpu.VMEM((d // LANES, ts, LANES), jnp.float32)],
        compiler_params=pltpu.CompilerParams(
            dimension_semantics=("arbitrary",), vmem_limit_bytes=VMEM_LIMIT_BYTES),
        name="moe_combine_final",
    )(dest, dest, x1, mod, fmod, fin_g, ys)


def _router_operands(wg, bg, wr, br):
    d = wg.shape[0]
    pad_g = jnp.zeros((SUBLANES - N_GROUPS, d), jnp.float32)
    pad_e = jnp.zeros((ROUTER_ROWS - SUBLANES - N_EXPERTS, d), jnp.float32)
    wrt = jnp.concatenate([wg.T, pad_g, wr.T, pad_e], axis=0)
    bcol = jnp.concatenate([bg, jnp.zeros((SUBLANES - N_GROUPS,), jnp.float32), br,
                            jnp.zeros((ROUTER_ROWS - SUBLANES - N_EXPERTS,), jnp.float32)])
    brt = jnp.broadcast_to(bcol[:, None], (ROUTER_ROWS, LANES))
    return _bf16(wrt), brt


def _mod_table(m, parts):
    bsz = m.shape[0]
    d = m.shape[1] // parts
    m = jnp.transpose(m.reshape(bsz, parts, d), (1, 0, 2))
    return jnp.concatenate([m, jnp.zeros((MOD_ROWS - parts, bsz, d), jnp.float32)], axis=0)


def _row(v):
    return v.reshape(1, -1)


def _moe_experts(hrows, cls, rank, cnt, wg, wu, wd, layer, n_blocks, xs_init):
    counts = cnt[:, 0].astype(jnp.int32)
    xs, dest, blo, bhi, nblk = _dispatch_call(cls, rank, counts, hrows, n_blocks, xs_init)
    ys = _expert_call(blo, bhi, nblk, xs, wg, wu, wd, layer)
    return xs, ys, dest


def kernel(x, c, ada_w, ada_b, norm1_g, norm2_g, cm_w1, cm_b1, cm_dw, cm_dwb, cm_ln_g, cm_ln_b, cm_w2, cm_b2, rg_wy, rg_by, rg_wx, rg_bx, rg_cw, rg_cb, rg_wa, rg_ba, rg_wi, rg_bi, rg_lambda, rg_wo, rg_bo, moe_wg, moe_bg, moe_wr, moe_br, moe_w_gate, moe_w_up, moe_w_down, fin_ada_w, fin_ada_b, fin_g):
    bsz, seq, d = x.shape
    t = bsz * seq
    depth = ada_w.shape[0]
    assert depth == 2 and bsz == SUBLANES and seq % TIME_TILE == 0
    assert d == 2 * PACK_SUBLANES * LANES

    mods = _ada_call(c, ada_w, ada_b[:, None, :])
    fmods = _ada_call(c, fin_ada_w[None], fin_ada_b[None, None, :])
    fmod = _mod_table(fmods[0], 2)
    n_steps = seq // TIME_TILE
    granule = n_steps // math.gcd(n_steps, ROW_BLOCK)
    n_blocks = -(-(-(-t // ROW_BLOCK) + N_CLASSES) // granule) * granule
    mod0, mod1 = _mod_table(mods[0], 6), _mod_table(mods[1], 6)
    moe_w_gate, moe_w_up, moe_w_down = _bf16(moe_w_gate), _bf16(moe_w_up), _bf16(moe_w_down)

    wrt, brt = _router_operands(moe_wg[0], moe_bg[0], moe_wr[0], moe_br[0])
    dw = jnp.concatenate([cm_dw[0], jnp.zeros((1, d), jnp.float32)], axis=0)
    dw = jnp.transpose(dw.reshape(CONF_KERNEL + 1, d // LANES, LANES), (1, 0, 2))
    x1, hrows, cls, rank, cnt, xs_init = _conf_call(
        x, mod0, _row(norm1_g[0]), _row(norm2_g[0]), _bf16(cm_w1[0]), _row(cm_b1[0]),
        dw, cm_dwb[0].reshape(d // LANES, 1, LANES), _row(cm_ln_g[0]), _row(cm_ln_b[0]),
        _bf16(cm_w2[0]), _row(cm_b2[0]), wrt, brt, n_blocks * ROW_BLOCK * SUBLANES)
    xs, ys, dest = _moe_experts(hrows, cls, rank, cnt, moe_w_gate, moe_w_up, moe_w_down, 0,
                                n_blocks, xs_init)

    wrt, brt = _router_operands(moe_wg[1], moe_bg[1], moe_wr[1], moe_br[1])
    x1, hrows, cls, rank, cnt = _lru_call(
        dest, x1, mod0, ys, mod1, _row(norm1_g[1]), _row(norm2_g[1]),
        _bf16(rg_wy[0]), _row(rg_by[0]), _bf16(rg_wx[0]), _row(rg_bx[0]),
        rg_cw[0], _row(rg_cb[0]), _bf16(rg_wa[0]), _row(rg_ba[0]), _bf16(rg_wi[0]), _row(rg_bi[0]),
        _row(rg_lambda[0]), _bf16(rg_wo[0]), _row(rg_bo[0]), wrt, brt)
    _, ys, dest = _moe_experts(hrows, cls, rank, cnt, moe_w_gate, moe_w_up, moe_w_down, 1,
                               n_blocks, xs)
    return _final_call(dest, x1, mod1, ys, bsz, fmod, _row(fin_g))
```

```python
import functools
import math

import jax
import jax.numpy as jnp
import numpy as np
from jax import lax
from jax.experimental import pallas as pl
from jax.experimental.pallas import tpu as pltpu

EPS = 1e-6
CONF_KERNEL = 31
LRU_HEADS = 4
LRU_CONV = 4
LRU_C = 8.0
N_GROUPS = 4
EXPERTS_PER_GROUP = 8
N_EXPERTS = N_GROUPS * EXPERTS_PER_GROUP
PAIRS_PER_GROUP = EXPERTS_PER_GROUP * (EXPERTS_PER_GROUP - 1) // 2
N_CLASSES = N_GROUPS * PAIRS_PER_GROUP

LANES = 128
SUBLANES = 8
VMEM_LIMIT_BYTES = 56 * 1024 * 1024

TOKEN_TILE = 512
TIME_TILE = TOKEN_TILE // SUBLANES
SUB_TILE = 256
ROW_BLOCK = 320
PERM_PITCH = TIME_TILE + SUBLANES
CONV_ROWS = 128
ROUTER_ROWS = 64
PACK_SUBLANES = 4
MOD_ROWS = 8
DISPATCH_SLOTS = 3
DMA_UNROLL = 8

_NEG_INF = float("-inf")


def _class_tables():
    lo = np.zeros((N_CLASSES,), np.int32)
    hi = np.zeros((N_CLASSES,), np.int32)
    for g in range(N_GROUPS):
        for a in range(EXPERTS_PER_GROUP):
            for b in range(a + 1, EXPERTS_PER_GROUP):
                c = g * PAIRS_PER_GROUP + (a * (2 * EXPERTS_PER_GROUP - 1 - a)) // 2 + (b - a - 1)
                lo[c] = g * EXPERTS_PER_GROUP + a
                hi[c] = g * EXPERTS_PER_GROUP + b
    return lo, hi


_CLASS_LO, _CLASS_HI = _class_tables()


def _bf16(x):
    return x.astype(jnp.bfloat16)


def _dot(a, b):
    return jnp.dot(a, b, preferred_element_type=jnp.float32)


def _bits(x):
    return lax.bitcast_convert_type(x, jnp.uint32)


def _f32_from_bits(x):
    return lax.bitcast_convert_type(x, jnp.float32)


def _token_rows(ref, sublane, n):
    return ref[pl.ds(sublane, n, stride=SUBLANES), :]


def _per_batch(v, ts):
    return jnp.tile(v, (ts // SUBLANES, 1))


def _ada_kernel(c_ref, w_ref, b_ref, o_ref):
    c = c_ref[...]
    ca = _bf16(c * jax.nn.sigmoid(c))
    o_ref[...] = _dot(ca, _bf16(w_ref[...])) + b_ref[...]


def _ada_call(c, w, b):
    n_layers, d, n = w.shape
    bsz = c.shape[0]
    nb = 1024
    return pl.pallas_call(
        _ada_kernel,
        out_shape=jax.ShapeDtypeStruct((n_layers, bsz, n), jnp.float32),
        grid=(n_layers, n // nb),
        in_specs=[
            pl.BlockSpec((bsz, d), lambda l, j: (0, 0)),
            pl.BlockSpec((None, d, nb), lambda l, j: (l, 0, j)),
            pl.BlockSpec((None, 1, nb), lambda l, j: (l, 0, j)),
        ],
        out_specs=pl.BlockSpec((None, bsz, nb), lambda l, j: (l, 0, j)),
        compiler_params=pltpu.CompilerParams(
            dimension_semantics=("arbitrary", "arbitrary"),
            vmem_limit_bytes=VMEM_LIMIT_BYTES),
        name="ada_mod",
    )(c, w, b)


def _rms_mod(x, gain, shift, scale):
    ts = x.shape[0]
    ms = jnp.mean(x * x, axis=-1, keepdims=True)
    return (x * lax.rsqrt(ms + EPS)) * _per_batch(gain * (1.0 + scale), ts) + _per_batch(shift, ts)


def _causal_taps(w_ref, bias_ref, ext_ref, n_taps, row0, n):
    d = ext_ref.shape[1]
    cols = []
    for cb in range(d // LANES):
        sl = slice(cb * LANES, (cb + 1) * LANES)
        acc = jnp.broadcast_to(bias_ref[:, sl], (n, LANES))
        for k in range(n_taps):
            lo = row0 + SUBLANES * k
            acc = acc + w_ref[k:k + 1, sl] * ext_ref[lo:lo + n, sl]
        cols.append(acc)
    return jnp.concatenate(cols, axis=1)


def _causal_taps_chunked(w_ref, bias_ref, ext_ref, out_ref, n_taps, ts):
    n_chunks = ext_ref.shape[0]
    units_per_chunk = ts // CONV_ROWS

    def unit(idx, carry):
        cb = idx // units_per_chunk
        r0 = pl.multiple_of((idx % units_per_chunk) * CONV_ROWS, CONV_ROWS)
        w_cb, ext_cb = w_ref.at[cb], ext_ref.at[cb]
        acc = jnp.broadcast_to(bias_ref[cb], (CONV_ROWS, LANES))
        for k in range(n_taps):
            win = ext_cb[pl.ds(pl.multiple_of(r0 + SUBLANES * k, SUBLANES), CONV_ROWS), :]
            acc = acc + w_cb[k:k + 1, :] * win
        out_ref.at[cb][pl.ds(r0, CONV_ROWS), :] = acc
        return carry

    lax.fori_loop(0, n_chunks * units_per_chunk, unit, 0)


def _start_tile_dmas(make_copy, n, inline=False):
    if inline:
        for r in range(n):
            make_copy(r).start()
        return

    def body(r, carry):
        make_copy(r).start()
        return carry
    lax.fori_loop(0, n, body, 0, unroll=DMA_UNROLL)


def _wait_all_tiles(hbm_ref, vmem_tiles_ref, sem, to_hbm):
    hbm_view = hbm_ref.at[pl.ds(0, vmem_tiles_ref.shape[0])]
    if to_hbm:
        pltpu.make_async_copy(vmem_tiles_ref, hbm_view, sem).wait()
    else:
        pltpu.make_async_copy(hbm_view, vmem_tiles_ref, sem).wait()


def _gather_expert_tiles(ys_ref, ybuf, sem, dest_ref, dnext_ref, step, n_steps, ts):
    slot = step % 2

    def tile_copy(dref, sl):
        def make(r):
            src = pl.multiple_of(dref[0, 0, r] * SUBLANES, SUBLANES)
            return pltpu.make_async_copy(
                ys_ref.at[pl.ds(src, SUBLANES)],
                ybuf.at[sl, pl.ds(pl.multiple_of(r * SUBLANES, SUBLANES), SUBLANES)],
                sem.at[sl])
        return make

    @pl.when(step == 0)
    def _():
        _start_tile_dmas(tile_copy(dest_ref, 0), ts)

    _wait_all_tiles(ys_ref, ybuf.at[slot], sem.at[slot], to_hbm=False)
    _start_tile_dmas(tile_copy(dnext_ref, 1 - slot), ts, inline=True)
    yv = ybuf.at[slot]
    y = jnp.concatenate([_token_rows(yv, s, ts) for s in range(SUBLANES)], axis=1)

    def retire_last():
        @pl.when(step + 1 == n_steps)
        def _():
            _wait_all_tiles(ys_ref, ybuf.at[1 - slot], sem.at[1 - slot], to_hbm=False)

    return y, retire_last


def _route(logits_t):
    ts = logits_t.shape[1]
    row = lax.broadcasted_iota(jnp.int32, (SUBLANES, ts), 0)
    gl = jnp.where(row < N_GROUPS, logits_t[0:SUBLANES], _NEG_INF)
    gmax = jnp.max(gl, axis=0, keepdims=True)
    gsel = jnp.min(jnp.where(gl == gmax, row, SUBLANES), axis=0, keepdims=True)
    p_g = 1.0 / jnp.sum(jnp.exp(gl - gmax), axis=0, keepdims=True)
    es = jnp.zeros((SUBLANES, ts), jnp.float32)
    for g in range(N_GROUPS):
        lo = SUBLANES * (1 + g)
        es = jnp.where(gsel == g, logits_t[lo:lo + EXPERTS_PER_GROUP], es)
    m1 = jnp.max(es, axis=0, keepdims=True)
    i1 = jnp.min(jnp.where(es == m1, row, SUBLANES), axis=0, keepdims=True)
    es2 = jnp.where(row == i1, _NEG_INF, es)
    m2 = jnp.max(es2, axis=0, keepdims=True)
    i2 = jnp.min(jnp.where(es2 == m2, row, SUBLANES), axis=0, keepdims=True)
    t = jnp.exp(m2 - m1)
    w1 = p_g / (1.0 + t)
    w2 = w1 * t
    first_lo = i1 < i2
    e_lo = jnp.minimum(i1, i2)
    e_hi = jnp.maximum(i1, i2)
    gate_lo = jnp.where(first_lo, w1, w2)
    gate_hi = jnp.where(first_lo, w2, w1)
    pair = lax.shift_right_logical(e_lo * (2 * EXPERTS_PER_GROUP - 1 - e_lo), 1) + (e_hi - e_lo - 1)
    cls = gsel * PAIRS_PER_GROUP + pair
    return cls, gate_lo, gate_hi


def _post_mixer(x1, mod_ref, n2_ref, wrt_ref, brt_ref, hrow_ref, cls_ref, rank_ref, cnt_ref, sub):
    ts = x1.shape[0]
    hrow_ref = hrow_ref.at[pl.ds(sub * ts * SUBLANES, ts * SUBLANES)]
    cols = slice(sub * ts, (sub + 1) * ts)
    h2 = _rms_mod(x1, n2_ref[...], mod_ref[3], mod_ref[4])
    h2b = _bf16(h2)
    logits_t = lax.dot_general(wrt_ref[...], h2b, (((1,), (1,)), ((), ())),
                               preferred_element_type=jnp.float32) + brt_ref[:, 0:1]
    cls, gate_lo, gate_hi = _route(logits_t)
    gates_t = jnp.concatenate(
        [gate_lo, gate_hi, jnp.zeros((LANES - 2, ts), jnp.float32)], axis=0)
    h2r = h2b.astype(jnp.float32)
    for j in range(PACK_SUBLANES):
        hi = _bits(h2r[:, (2 * j) * LANES:(2 * j + 1) * LANES])
        lo = _bits(h2r[:, (2 * j + 1) * LANES:(2 * j + 2) * LANES])
        hrow_ref[pl.ds(j, ts, stride=SUBLANES), :] = hi | lax.shift_right_logical(lo, jnp.uint32(16))
    hrow_ref[pl.ds(PACK_SUBLANES, ts, stride=SUBLANES), :] = _bits(gates_t.T)
    for j in range(PACK_SUBLANES + 1, SUBLANES):
        hrow_ref[pl.ds(j, ts, stride=SUBLANES), :] = jnp.zeros((ts, LANES), jnp.uint32)
    cls_ref[0, :, cols] = cls
    crow = lax.broadcasted_iota(jnp.int32, (LANES, ts), 0)
    onehot = jnp.where(crow == cls, 1.0, 0.0)
    cnt = jnp.sum(onehot, axis=1, keepdims=True)
    earlier = (lax.broadcasted_iota(jnp.int32, (ts, ts), 0)
               < lax.broadcasted_iota(jnp.int32, (ts, ts), 1))
    prefix = _dot(_bf16(onehot), _bf16(jnp.where(earlier, 1.0, 0.0)))
    rank = jnp.sum(onehot * (prefix + cnt_ref[:, 0:1]), axis=0, keepdims=True)
    rank_ref[0, :, cols] = rank.astype(jnp.int32)
    cnt_ref[...] += jnp.broadcast_to(cnt, cnt_ref.shape)


def _mixer_out(t, d, ts):
    n_tiles = t // ts
    out_shape = (
        jax.ShapeDtypeStruct((t, d), jnp.float32),
        jax.ShapeDtypeStruct((t * SUBLANES, LANES), jnp.uint32),
        jax.ShapeDtypeStruct((n_tiles, 1, ts), jnp.int32),
        jax.ShapeDtypeStruct((n_tiles, 1, ts), jnp.int32),
        jax.ShapeDtypeStruct((LANES, LANES), jnp.float32),
    )
    out_specs = (
        pl.BlockSpec((ts, d), lambda i: (i, 0)),
        pl.BlockSpec((ts * SUBLANES, LANES), lambda i: (i, 0)),
        pl.BlockSpec((1, 1, ts), lambda i: (i, 0, 0)),
        pl.BlockSpec((1, 1, ts), lambda i: (i, 0, 0)),
        pl.BlockSpec((LANES, LANES), lambda i: (0, 0)),
    )
    return out_shape, out_specs


def _full(shape):
    return pl.BlockSpec(shape, lambda i: tuple(0 for _ in shape))


def _to_tile_order(x_ref, perm, xs):
    bsz, tm, d = x_ref.shape
    for cb in range(d // LANES):
        for p in range(bsz):
            perm[cb, PERM_PITCH * p:PERM_PITCH * p + tm, :] = x_ref[p, :, cb * LANES:(cb + 1) * LANES]
    for m in range(tm):
        for cb in range(d // LANES):
            xs[m * bsz:(m + 1) * bsz, cb * LANES:(cb + 1) * LANES] = (
                perm.at[cb][pl.ds(m, bsz, stride=PERM_PITCH), :])


def _conf_kernel(x_ref, mod_ref, n1_ref, n2_ref, w1_ref, b1_ref, dw_ref, dwb_ref,
                 lng_ref, lnb_ref, w2_ref, b2_ref, wrt_ref, brt_ref,
                 x1_ref, hrow_ref, cls_ref, rank_ref, cnt_ref, zeros_ref, perm, xs, vext, conv):
    i = pl.program_id(0)
    bsz, tm, d = x_ref.shape
    ts = bsz * tm
    n_chunks = d // LANES
    hist = SUBLANES * (CONF_KERNEL - 1)

    @pl.when(i == 0)
    def _():
        vext[:, 0:hist, :] = jnp.zeros((n_chunks, hist, LANES), jnp.float32)
        cnt_ref[...] = jnp.zeros_like(cnt_ref)

    zeros_ref[...] = jnp.zeros_like(zeros_ref)

    _to_tile_order(x_ref, perm, xs)
    for sub in range(ts // SUB_TILE):
        rows = slice(sub * SUB_TILE, (sub + 1) * SUB_TILE)
        h = _bf16(_rms_mod(xs[rows, :], n1_ref[...], mod_ref[0], mod_ref[1]))
        u = _dot(h, w1_ref[...]) + b1_ref[...]
        v = u[:, 0:d] * jax.nn.sigmoid(u[:, d:2 * d])
        for cb in range(n_chunks):
            vext[cb, hist + sub * SUB_TILE:hist + (sub + 1) * SUB_TILE, :] = v[:, cb * LANES:(cb + 1) * LANES]

    _causal_taps_chunked(dw_ref, dwb_ref, vext, conv, CONF_KERNEL, ts)
    vext[:, 0:hist, :] = vext[:, ts:ts + hist, :]

    for sub in range(ts // SUB_TILE):
        rows = slice(sub * SUB_TILE, (sub + 1) * SUB_TILE)
        y = jnp.concatenate([conv[cb, rows, :] for cb in range(n_chunks)], axis=1)
        mu = jnp.mean(y, axis=-1, keepdims=True)
        yc = y - mu
        var = jnp.mean(yc * yc, axis=-1, keepdims=True)
        z = yc * lax.rsqrt(var + EPS) * lng_ref[...] + lnb_ref[...]
        z = z * jax.nn.sigmoid(z)
        mix = _dot(_bf16(z), w2_ref[...]) + b2_ref[...]
        x1 = xs[rows, :] + _per_batch(mod_ref[2], SUB_TILE) * mix
        x1_ref[rows, :] = x1
        _post_mixer(x1, mod_ref, n2_ref, wrt_ref, brt_ref, hrow_ref, cls_ref, rank_ref, cnt_ref, sub)


def _conf_call(x, mod, n1, n2, w1, b1, dw, dwb, lng, lnb, w2, b2, wrt, brt, zero_rows):
    bsz, seq, d = x.shape
    tm = TIME_TILE
    ts = bsz * tm
    n_steps = seq // tm
    assert zero_rows % (n_steps * SUBLANES) == 0
    out_shape, out_specs = _mixer_out(bsz * seq, d, ts)
    out_shape += (jax.ShapeDtypeStruct((zero_rows, LANES), jnp.uint32),)
    out_specs += (pl.BlockSpec((zero_rows // n_steps, LANES), lambda i: (i, 0)),)
    hist = SUBLANES * (CONF_KERNEL - 1)
    return pl.pallas_call(
        _conf_kernel,
        out_shape=out_shape,
        grid=(n_steps,),
        in_specs=[
            pl.BlockSpec((bsz, tm, d), lambda i: (0, i, 0)),
            _full((MOD_ROWS, bsz, d)),
            _full((1, d)), _full((1, d)),
            _full((d, 2 * d)), _full((1, 2 * d)),
            _full((d // LANES, CONF_KERNEL + 1, LANES)), _full((d // LANES, 1, LANES)),
            _full((1, d)), _full((1, d)),
            _full((d, d)), _full((1, d)),
            _full((ROUTER_ROWS, d)), _full((ROUTER_ROWS, LANES)),
        ],
        out_specs=out_specs,
        scratch_shapes=[
            pltpu.VMEM((d // LANES, bsz * PERM_PITCH, LANES), jnp.float32),
            pltpu.VMEM((ts, d), jnp.float32),
            pltpu.VMEM((d // LANES, hist + ts, LANES), jnp.float32),
            pltpu.VMEM((d // LANES, ts, LANES), jnp.float32),
        ],
        compiler_params=pltpu.CompilerParams(
            dimension_semantics=("arbitrary",), vmem_limit_bytes=VMEM_LIMIT_BYTES),
        name="conf_mixer",
    )(x, mod, n1, n2, w1, b1, dw, dwb, lng, lnb, w2, b2, wrt, brt)


def _lru_kernel(n_tiles, dest_ref, dnext_ref, xprev_ref, pmod_ref, mod_ref, n1_ref, n2_ref,
                wy_ref, by_ref, wx_ref, bx_ref,
                cw_ref, cb_ref, wa_ref, ba_ref, wi_ref, bi_ref, lam_ref, wo_ref, bo_ref,
                wrt_ref, brt_ref, ys_ref,
                x1_ref, hrow_ref, cls_ref, rank_ref, cnt_ref, uext, hcar, ybuf, ysem):
    i = pl.program_id(0)
    ts, d = xprev_ref.shape
    hd = d // LRU_HEADS
    hist = SUBLANES * (LRU_CONV - 1)

    @pl.when(i == 0)
    def _():
        uext[0:hist, :] = jnp.zeros((hist, d), jnp.float32)
        hcar[...] = jnp.zeros_like(hcar)
        cnt_ref[...] = jnp.zeros_like(cnt_ref)

    y_prev, retire_last = _gather_expert_tiles(ys_ref, ybuf, ysem, dest_ref, dnext_ref, i, n_tiles, ts)
    nl = -lam_ref[...]
    softplus = jnp.maximum(nl, 0.0) + jnp.log1p(jnp.exp(-jnp.abs(nl)))
    hstate = hcar[...]
    for sub in range(ts // SUB_TILE):
        rows = slice(sub * SUB_TILE, (sub + 1) * SUB_TILE)
        x = xprev_ref[rows, :] + _per_batch(pmod_ref[5], SUB_TILE) * y_prev[rows, :]
        h = _bf16(_rms_mod(x, n1_ref[...], mod_ref[0], mod_ref[1]))
        ygate = jax.nn.gelu(_dot(h, wy_ref[...]) + by_ref[...])
        xb = _dot(h, wx_ref[...]) + bx_ref[...]
        uext[hist + sub * SUB_TILE:hist + (sub + 1) * SUB_TILE, :] = xb
        u = _causal_taps(cw_ref, cb_ref, uext, LRU_CONV, sub * SUB_TILE, SUB_TILE)

        ub = _bf16(u)
        r_parts, i_parts = [], []
        for hh in range(LRU_HEADS):
            uh = ub[:, hh * hd:(hh + 1) * hd]
            r_parts.append(_dot(uh, wa_ref[hh]))
            i_parts.append(_dot(uh, wi_ref[hh]))
        r = jax.nn.sigmoid(jnp.concatenate(r_parts, axis=1) + ba_ref[...])
        ig = jax.nn.sigmoid(jnp.concatenate(i_parts, axis=1) + bi_ref[...])

        log_a = (-LRU_C) * r * softplus
        a = jnp.exp(log_a)
        mult = jnp.sqrt(-jnp.tanh(log_a) * (1.0 + a * a))
        b = mult * (ig * u)

        outs = []
        for m in range(SUB_TILE // SUBLANES):
            sl = slice(m * SUBLANES, (m + 1) * SUBLANES)
            hstate = a[sl] * hstate + b[sl]
            outs.append(hstate)
        hs = jnp.concatenate(outs, axis=0)

        mix = _dot(_bf16(hs * ygate), wo_ref[...]) + bo_ref[...]
        x1 = x + _per_batch(mod_ref[2], SUB_TILE) * mix
        x1_ref[rows, :] = x1
        _post_mixer(x1, mod_ref, n2_ref, wrt_ref, brt_ref, hrow_ref, cls_ref, rank_ref, cnt_ref, sub)
    hcar[...] = hstate
    uext[0:hist, :] = uext[ts:ts + hist, :]
    retire_last()


def _lru_call(dest, x_prev, mod_prev, ys, mod, n1, n2, wy, by, wx, bx, cw, cb, wa, ba, wi, bi, lam,
              wo, bo, wrt, brt):
    t, d = x_prev.shape
    n_tiles, _, ts = dest.shape
    bsz = mod.shape[1]
    hd = d // LRU_HEADS
    hist = SUBLANES * (LRU_CONV - 1)
    out_shape, out_specs = _mixer_out(t, d, ts)
    return pl.pallas_call(
        functools.partial(_lru_kernel, n_tiles),
        out_shape=out_shape,
        grid=(n_tiles,),
        in_specs=[
            pl.BlockSpec((1, 1, ts), lambda i: (i, 0, 0), memory_space=pltpu.SMEM),
            pl.BlockSpec((1, 1, ts), lambda i: (jnp.minimum(i + 1, n_tiles - 1), 0, 0),
                         memory_space=pltpu.SMEM),
            pl.BlockSpec((ts, d), lambda i: (i, 0)),
            _full((MOD_ROWS, bsz, d)), _full((MOD_ROWS, bsz, d)),
            _full((1, d)), _full((1, d)),
            _full((d, d)), _full((1, d)),
            _full((d, d)), _full((1, d)),
            _full((LRU_CONV, d)), _full((1, d)),
            _full((LRU_HEADS, hd, hd)), _full((1, d)),
            _full((LRU_HEADS, hd, hd)), _full((1, d)),
            _full((1, d)),
            _full((d, d)), _full((1, d)),
            _full((ROUTER_ROWS, d)), _full((ROUTER_ROWS, LANES)),
            pl.BlockSpec(memory_space=pl.ANY),
        ],
        out_specs=out_specs,
        scratch_shapes=[
            pltpu.VMEM((hist + ts, d), jnp.float32),
            pltpu.VMEM((SUBLANES, d), jnp.float32),
            pltpu.VMEM((2, ts * SUBLANES, LANES), jnp.float32),
            pltpu.SemaphoreType.DMA((2,)),
        ],
        compiler_params=pltpu.CompilerParams(
            dimension_semantics=("arbitrary",), vmem_limit_bytes=VMEM_LIMIT_BYTES),
        name="lru_mixer",
    )(dest, dest, x_prev, mod_prev, mod, n1, n2, wy, by, wx, bx, cw, cb, wa, ba, wi, bi, lam, wo, bo,
      wrt, brt, ys)


def _dispatch_kernel(n_tiles, n_blocks, cls_ref, rank_ref, cnt_ref, clo_ref, chi_ref, hrow_ref, xs_in_ref,
                     xs_ref, dest_ref, blo_ref, bhi_ref, nblk_ref, class_start, stage, sem_in, sem_out):
    del xs_in_ref
    i = pl.program_id(0)
    ts = cls_ref.shape[2]
    rows = ts * SUBLANES

    @pl.when(i == 0)
    def _():
        def per_class(c, nb_total):
            class_start[c] = nb_total * ROW_BLOCK
            nb = (cnt_ref[c] + (ROW_BLOCK - 1)) // ROW_BLOCK

            def fill(k, carry):
                blo_ref[nb_total + k] = clo_ref[c]
                bhi_ref[nb_total + k] = chi_ref[c]
                return carry

            lax.fori_loop(0, nb, fill, 0)
            return nb_total + nb

        nb_used = lax.fori_loop(0, N_CLASSES, per_class, 0)
        nblk_ref[0] = nb_used
        last = jnp.maximum(nb_used - 1, 0)

        def tail(k, carry):
            blo_ref[k] = blo_ref[last]
            bhi_ref[k] = bhi_ref[last]
            return carry

        lax.fori_loop(nb_used, n_blocks, tail, 0)

    def load(tile, slot):
        return pltpu.make_async_copy(hrow_ref.at[pl.ds(pl.multiple_of(tile * rows, rows), rows)],
                                     stage.at[slot], sem_in.at[slot])

    def tile_copy(slot, r, dst_row):
        return pltpu.make_async_copy(
            stage.at[slot, pl.ds(pl.multiple_of(r * SUBLANES, SUBLANES), SUBLANES)],
            xs_ref.at[pl.ds(pl.multiple_of(dst_row * SUBLANES, SUBLANES), SUBLANES)],
            sem_out.at[slot])

    def drain(slot):
        _wait_all_tiles(xs_ref, stage.at[slot], sem_out.at[slot], to_hbm=True)

    slot = i % DISPATCH_SLOTS
    nxt = (i + 1) % DISPATCH_SLOTS

    @pl.when(i == 0)
    def _():
        load(0, 0).start()

    @pl.when(i >= 2)
    def _():
        drain(nxt)

    @pl.when(i + 1 < n_tiles)
    def _():
        load(i + 1, nxt).start()

    load(i, slot).wait()

    def scatter(r):
        dst_row = class_start[cls_ref[0, 0, r]] + rank_ref[0, 0, r]
        dest_ref[0, 0, r] = dst_row
        return tile_copy(slot, r, dst_row)

    _start_tile_dmas(scatter, ts)

    @pl.when(i == n_tiles - 1)
    def _():
        @pl.when(i >= 1)
        def _():
            drain((i + DISPATCH_SLOTS - 1) % DISPATCH_SLOTS)
        drain(slot)


def _dispatch_call(cls, rank, counts, hrows, n_blocks, xs_init):
    n_tiles, _, ts = cls.shape
    n_rows = n_blocks * ROW_BLOCK
    assert xs_init.shape == (n_rows * SUBLANES, LANES) and xs_init.dtype == jnp.uint32
    smem_full = pl.BlockSpec(memory_space=pltpu.SMEM)
    smem_tile = pl.BlockSpec((1, 1, ts), lambda i: (i, 0, 0), memory_space=pltpu.SMEM)
    any_spec = pl.BlockSpec(memory_space=pl.ANY)
    return pl.pallas_call(
        functools.partial(_dispatch_kernel, n_tiles, n_blocks),
        out_shape=(
            jax.ShapeDtypeStruct((n_rows * SUBLANES, LANES), jnp.uint32),
            jax.ShapeDtypeStruct((n_tiles, 1, ts), jnp.int32),
            jax.ShapeDtypeStruct((n_blocks,), jnp.int32),
            jax.ShapeDtypeStruct((n_blocks,), jnp.int32),
            jax.ShapeDtypeStruct((1,), jnp.int32),
        ),
        grid=(n_tiles,),
        in_specs=[
            smem_tile, smem_tile,
            smem_full, smem_full, smem_full, any_spec, any_spec,
        ],
        out_specs=(
            any_spec,
            smem_tile,
            smem_full, smem_full, smem_full,
        ),
        scratch_shapes=[
            pltpu.SMEM((LANES,), jnp.int32),
            pltpu.VMEM((DISPATCH_SLOTS, ts * SUBLANES, LANES), jnp.uint32),
            pltpu.SemaphoreType.DMA((DISPATCH_SLOTS,)),
            pltpu.SemaphoreType.DMA((DISPATCH_SLOTS,)),
        ],
        input_output_aliases={6: 0},
        compiler_params=pltpu.CompilerParams(
            dimension_semantics=("arbitrary",), vmem_limit_bytes=VMEM_LIMIT_BYTES),
        name="moe_dispatch",
    )(cls, rank, counts, jnp.asarray(_CLASS_LO), jnp.asarray(_CLASS_HI), hrows, xs_init)


def _expert_kernel(blo_ref, bhi_ref, nblk_ref, xs_ref, wg_lo, wu_lo, wd_lo, wg_hi, wu_hi, wd_hi, ys_ref):
    del blo_ref, bhi_ref
    b = pl.program_id(0)
    rb = ROW_BLOCK

    @pl.when(b < nblk_ref[0])
    def _():
        pieces = []
        for j in range(PACK_SUBLANES):
            w = _token_rows(xs_ref, j, rb)
            pieces.append(_bf16(_f32_from_bits(w & jnp.uint32(0xFFFF0000))))
            pieces.append(_bf16(_f32_from_bits(lax.shift_left(w, jnp.uint32(16)))))
        xb = jnp.concatenate(pieces, axis=1)
        gates = _f32_from_bits(_token_rows(xs_ref, PACK_SUBLANES, rb))
        acc = None
        for idx, (wg, wu, wd) in enumerate(((wg_lo, wu_lo, wd_lo), (wg_hi, wu_hi, wd_hi))):
            a = _dot(xb, wg[...])
            hid = (a * jax.nn.sigmoid(a)) * _dot(xb, wu[...]) * gates[:, idx:idx + 1]
            part = _dot(_bf16(hid), wd[...])
            acc = part if acc is None else acc + part
        for s in range(SUBLANES):
            ys_ref[pl.ds(s, rb, stride=SUBLANES), :] = acc[:, s * LANES:(s + 1) * LANES]

    @pl.when(b >= nblk_ref[0])
    def _():
        ys_ref[...] = jnp.zeros_like(ys_ref)


def _expert_call(blo, bhi, nblk, xs, wg, wu, wd, layer):
    n_rows = xs.shape[0] // SUBLANES
    d, de = wg.shape[2], wg.shape[3]
    assert d == SUBLANES * LANES
    n_blocks = n_rows // ROW_BLOCK

    def w_spec(shape, which):
        if which == 0:
            return pl.BlockSpec((None, None) + shape, lambda b, lo, hi, n: (layer, lo[b], 0, 0))
        return pl.BlockSpec((None, None) + shape, lambda b, lo, hi, n: (layer, hi[b], 0, 0))

    def x_map(b, lo, hi, n):
        return (jnp.minimum(b, jnp.maximum(n[0] - 1, 0)), 0)

    grid_spec = pltpu.PrefetchScalarGridSpec(
        num_scalar_prefetch=3,
        grid=(n_blocks,),
        in_specs=[
            pl.BlockSpec((ROW_BLOCK * SUBLANES, LANES), x_map),
            w_spec((d, de), 0), w_spec((d, de), 0), w_spec((de, d), 0),
            w_spec((d, de), 1), w_spec((d, de), 1), w_spec((de, d), 1),
        ],
        out_specs=pl.BlockSpec((ROW_BLOCK * SUBLANES, LANES), lambda b, lo, hi, n: (b, 0)),
    )
    return pl.pallas_call(
        _expert_kernel,
        out_shape=jax.ShapeDtypeStruct((n_rows * SUBLANES, LANES), jnp.float32),
        grid_spec=grid_spec,
        compiler_params=pltpu.CompilerParams(
            dimension_semantics=("arbitrary",), vmem_limit_bytes=VMEM_LIMIT_BYTES),
        name="moe_experts",
    )(blo, bhi, nblk, xs, wg, wu, wd, wg, wu, wd)


def _final_kernel(n_tiles, dest_ref, dnext_ref, x1_ref, mod_ref, fmod_ref, fg_ref, ys_ref,
                  out_ref, ybuf, sem, unperm):
    i = pl.program_id(0)
    ts, d = x1_ref.shape
    bsz, tm, _ = out_ref.shape
    y, retire_last = _gather_expert_tiles(ys_ref, ybuf, sem, dest_ref, dnext_ref, i, n_tiles, ts)
    x2 = x1_ref[...] + _per_batch(mod_ref[5], ts) * y
    res = _rms_mod(x2, fg_ref[...], fmod_ref[0], fmod_ref[1])
    for cb in range(d // LANES):
        unperm[cb] = res[:, cb * LANES:(cb + 1) * LANES]
    for cb in range(d // LANES):
        for p in range(bsz):
            out_ref[p, :, cb * LANES:(cb + 1) * LANES] = unperm.at[cb][pl.ds(p, tm, stride=SUBLANES), :]
    retire_last()


def _final_call(dest, x1, mod, ys, bsz, fmod, fin_g):
    t, d = x1.shape
    n_tiles, _, ts = dest.shape
    tm = ts // bsz
    return pl.pallas_call(
        functools.partial(_final_kernel, n_tiles),
        out_shape=jax.ShapeDtypeStruct((bsz, t // bsz, d), jnp.float32),
        grid=(n_tiles,),
        in_specs=[
            pl.BlockSpec((1, 1, ts), lambda i: (i, 0, 0), memory_space=pltpu.SMEM),
            pl.BlockSpec((1, 1, ts), lambda i: (jnp.minimum(i + 1, n_tiles - 1), 0, 0),
                         memory_space=pltpu.SMEM),
            pl.BlockSpec((ts, d), lambda i: (i, 0)),
            _full((MOD_ROWS, bsz, d)), _full((MOD_ROWS, bsz, d)), _full((1, d)),
            pl.BlockSpec(memory_space=pl.ANY),
        ],
        out_specs=pl.BlockSpec((bsz, tm, d), lambda i: (0, i, 0)),
        scratch_shapes=[pltpu.VMEM((2, ts * SUBLANES, LANES), jnp.float32),
                        pltpu.SemaphoreType.DMA((2,)),
                        pltpu.VMEM((d // LANES, ts, LANES), jnp.float32)],
        compiler_params=pltpu.CompilerParams(
            dimension_semantics=("arbitrary",), vmem_limit_bytes=VMEM_LIMIT_BYTES),
        name="moe_combine_final",
    )(dest, dest, x1, mod, fmod, fin_g, ys)


def _router_operands(wg, bg, wr, br):
    d = wg.shape[0]
    pad_g = jnp.zeros((SUBLANES - N_GROUPS, d), jnp.float32)
    pad_e = jnp.zeros((ROUTER_ROWS - SUBLANES - N_EXPERTS, d), jnp.float32)
    wrt = jnp.concatenate([wg.T, pad_g, wr.T, pad_e], axis=0)
    bcol = jnp.concatenate([bg, jnp.zeros((SUBLANES - N_GROUPS,), jnp.float32), br,
                            jnp.zeros((ROUTER_ROWS - SUBLANES - N_EXPERTS,), jnp.float32)])
    brt = jnp.broadcast_to(bcol[:, None], (ROUTER_ROWS, LANES))
    return _bf16(wrt), brt


def _mod_table(m, parts):
    bsz = m.shape[0]
    d = m.shape[1] // parts
    m = jnp.transpose(m.reshape(bsz, parts, d), (1, 0, 2))
    return jnp.concatenate([m, jnp.zeros((MOD_ROWS - parts, bsz, d), jnp.float32)], axis=0)


def _row(v):
    return v.reshape(1, -1)


def _moe_experts(hrows, cls, rank, cnt, wg, wu, wd, layer, n_blocks, xs_init):
    counts = cnt[:, 0].astype(jnp.int32)
    xs, dest, blo, bhi, nblk = _dispatch_call(cls, rank, counts, hrows, n_blocks, xs_init)
    ys = _expert_call(blo, bhi, nblk, xs, wg, wu, wd, layer)
    return xs, ys, dest


def kernel(x, c, ada_w, ada_b, norm1_g, norm2_g, cm_w1, cm_b1, cm_dw, cm_dwb, cm_ln_g, cm_ln_b, cm_w2, cm_b2, rg_wy, rg_by, rg_wx, rg_bx, rg_cw, rg_cb, rg_wa, rg_ba, rg_wi, rg_bi, rg_lambda, rg_wo, rg_bo, moe_wg, moe_bg, moe_wr, moe_br, moe_w_gate, moe_w_up, moe_w_down, fin_ada_w, fin_ada_b, fin_g):
    bsz, seq, d = x.shape
    t = bsz * seq
    depth = ada_w.shape[0]
    assert depth == 2 and bsz == SUBLANES and seq % TIME_TILE == 0
    assert d == 2 * PACK_SUBLANES * LANES

    mods = _ada_call(c, ada_w, ada_b[:, None, :])
    fmods = _ada_call(c, fin_ada_w[None], fin_ada_b[None, None, :])
    fmod = _mod_table(fmods[0], 2)
    n_steps = seq // TIME_TILE
    granule = n_steps // math.gcd(n_steps, ROW_BLOCK)
    n_blocks = -(-(-(-t // ROW_BLOCK) + N_CLASSES) // granule) * granule
    mod0, mod1 = _mod_table(mods[0], 6), _mod_table(mods[1], 6)
    moe_w_gate, moe_w_up, moe_w_down = _bf16(moe_w_gate), _bf16(moe_w_up), _bf16(moe_w_down)

    wrt, brt = _router_operands(moe_wg[0], moe_bg[0], moe_wr[0], moe_br[0])
    dw = jnp.concatenate([cm_dw[0], jnp.zeros((1, d), jnp.float32)], axis=0)
    dw = jnp.transpose(dw.reshape(CONF_KERNEL + 1, d // LANES, LANES), (1, 0, 2))
    x1, hrows, cls, rank, cnt, xs_init = _conf_call(
        x, mod0, _row(norm1_g[0]), _row(norm2_g[0]), _bf16(cm_w1[0]), _row(cm_b1[0]),
        dw, cm_dwb[0].reshape(d // LANES, 1, LANES), _row(cm_ln_g[0]), _row(cm_ln_b[0]),
        _bf16(cm_w2[0]), _row(cm_b2[0]), wrt, brt, n_blocks * ROW_BLOCK * SUBLANES)
    xs, ys, dest = _moe_experts(hrows, cls, rank, cnt, moe_w_gate, moe_w_up, moe_w_down, 0,
                                n_blocks, xs_init)

    wrt, brt = _router_operands(moe_wg[1], moe_bg[1], moe_wr[1], moe_br[1])
    x1, hrows, cls, rank, cnt = _lru_call(
        dest, x1, mod0, ys, mod1, _row(norm1_g[1]), _row(norm2_g[1]),
        _bf16(rg_wy[0]), _row(rg_by[0]), _bf16(rg_wx[0]), _row(rg_bx[0]),
        rg_cw[0], _row(rg_cb[0]), _bf16(rg_wa[0]), _row(rg_ba[0]), _bf16(rg_wi[0]), _row(rg_bi[0]),
        _row(rg_lambda[0]), _bf16(rg_wo[0]), _row(rg_bo[0]), wrt, brt)
    _, ys, dest = _moe_experts(hrows, cls, rank, cnt, moe_w_gate, moe_w_up, moe_w_down, 1,
                               n_blocks, xs)
    return _final_call(dest, x1, mod1, ys, bsz, fmod, _row(fin_g))
```

```python
import functools
import math

import jax
import jax.numpy as jnp
import numpy as np
from jax import lax
from jax.experimental import pallas as pl
from jax.experimental.pallas import tpu as pltpu

EPS = 1e-6
CONF_KERNEL = 31
LRU_HEADS = 4
LRU_CONV = 4
LRU_C = 8.0
N_GROUPS = 4
EXPERTS_PER_GROUP = 8
N_EXPERTS = N_GROUPS * EXPERTS_PER_GROUP
PAIRS_PER_GROUP = EXPERTS_PER_GROUP * (EXPERTS_PER_GROUP - 1) // 2
N_CLASSES = N_GROUPS * PAIRS_PER_GROUP

LANES = 128
SUBLANES = 8
VMEM_LIMIT_BYTES = 56 * 1024 * 1024

TOKEN_TILE = 512
TIME_TILE = TOKEN_TILE // SUBLANES
SUB_TILE = 256
ROW_BLOCK = 320
PERM_PITCH = TIME_TILE + SUBLANES
CONV_ROWS = 128
ROUTER_ROWS = 64
PACK_SUBLANES = 4
MOD_ROWS = 8
DISPATCH_SLOTS = 3
DMA_UNROLL = 8

_NEG_INF = float("-inf")


def _class_tables():
    lo = np.zeros((N_CLASSES,), np.int32)
    hi = np.zeros((N_CLASSES,), np.int32)
    for g in range(N_GROUPS):
        for a in range(EXPERTS_PER_GROUP):
            for b in range(a + 1, EXPERTS_PER_GROUP):
                c = g * PAIRS_PER_GROUP + (a * (2 * EXPERTS_PER_GROUP - 1 - a)) // 2 + (b - a - 1)
                lo[c] = g * EXPERTS_PER_GROUP + a
                hi[c] = g * EXPERTS_PER_GROUP + b
    return lo, hi


_CLASS_LO, _CLASS_HI = _class_tables()


def _bf16(x):
    return x.astype(jnp.bfloat16)


def _dot(a, b):
    return jnp.dot(a, b, preferred_element_type=jnp.float32)


def _bits(x):
    return lax.bitcast_convert_type(x, jnp.uint32)


def _f32_from_bits(x):
    return lax.bitcast_convert_type(x, jnp.float32)


def _token_rows(ref, sublane, n):
    return ref[pl.ds(sublane, n, stride=SUBLANES), :]


def _per_batch(v, ts):
    return jnp.tile(v, (ts // SUBLANES, 1))


def _ada_kernel(c_ref, w_ref, b_ref, o_ref):
    c = c_ref[...]
    ca = _bf16(c * jax.nn.sigmoid(c))
    o_ref[...] = _dot(ca, _bf16(w_ref[...])) + b_ref[...]


def _ada_call(c, w, b):
    n_layers, d, n = w.shape
    bsz = c.shape[0]
    nb = 1024
    return pl.pallas_call(
        _ada_kernel,
        out_shape=jax.ShapeDtypeStruct((n_layers, bsz, n), jnp.float32),
        grid=(n_layers, n // nb),
        in_specs=[
            pl.BlockSpec((bsz, d), lambda l, j: (0, 0)),
            pl.BlockSpec((None, d, nb), lambda l, j: (l, 0, j)),
            pl.BlockSpec((None, 1, nb), lambda l, j: (l, 0, j)),
        ],
        out_specs=pl.BlockSpec((None, bsz, nb), lambda l, j: (l, 0, j)),
        compiler_params=pltpu.CompilerParams(
            dimension_semantics=("arbitrary", "arbitrary"),
            vmem_limit_bytes=VMEM_LIMIT_BYTES),
        name="ada_mod",
    )(c, w, b)


def _rms_mod(x, gain, shift, scale):
    ts = x.shape[0]
    ms = jnp.mean(x * x, axis=-1, keepdims=True)
    return (x * lax.rsqrt(ms + EPS)) * _per_batch(gain * (1.0 + scale), ts) + _per_batch(shift, ts)


def _causal_taps(w_ref, bias_ref, ext_ref, n_taps, row0, n):
    d = ext_ref.shape[1]
    cols = []
    for cb in range(d // LANES):
        sl = slice(cb * LANES, (cb + 1) * LANES)
        acc = jnp.broadcast_to(bias_ref[:, sl], (n, LANES))
        for k in range(n_taps):
            lo = row0 + SUBLANES * k
            acc = acc + w_ref[k:k + 1, sl] * ext_ref[lo:lo + n, sl]
        cols.append(acc)
    return jnp.concatenate(cols, axis=1)


def _causal_taps_chunked(w_ref, bias_ref, ext_ref, out_ref, n_taps, ts):
    n_chunks = ext_ref.shape[0]
    units_per_chunk = ts // CONV_ROWS

    def unit(idx, carry):
        cb = idx // units_per_chunk
        r0 = pl.multiple_of((idx % units_per_chunk) * CONV_ROWS, CONV_ROWS)
        w_cb, ext_cb = w_ref.at[cb], ext_ref.at[cb]
        acc = jnp.broadcast_to(bias_ref[cb], (CONV_ROWS, LANES))
        for k in range(n_taps):
            win = ext_cb[pl.ds(pl.multiple_of(r0 + SUBLANES * k, SUBLANES), CONV_ROWS), :]
            acc = acc + w_cb[k:k + 1, :] * win
        out_ref.at[cb][pl.ds(r0, CONV_ROWS), :] = acc
        return carry

    lax.fori_loop(0, n_chunks * units_per_chunk, unit, 0)


def _start_tile_dmas(make_copy, n, inline=False):
    if inline:
        for r in range(n):
            make_copy(r).start()
        return

    def body(r, carry):
        make_copy(r).start()
        return carry
    lax.fori_loop(0, n, body, 0, unroll=DMA_UNROLL)


def _wait_all_tiles(hbm_ref, vmem_tiles_ref, sem, to_hbm):
    hbm_view = hbm_ref.at[pl.ds(0, vmem_tiles_ref.shape[0])]
    if to_hbm:
        pltpu.make_async_copy(vmem_tiles_ref, hbm_view, sem).wait()
    else:
        pltpu.make_async_copy(hbm_view, vmem_tiles_ref, sem).wait()


def _gather_expert_tiles(ys_ref, ybuf, sem, dest_ref, dnext_ref, step, n_steps, ts, inline):
    slot = step % 2

    def tile_copy(dref, sl):
        def make(r):
            src = pl.multiple_of(dref[0, 0, r] * SUBLANES, SUBLANES)
            return pltpu.make_async_copy(
                ys_ref.at[pl.ds(src, SUBLANES)],
                ybuf.at[sl, pl.ds(pl.multiple_of(r * SUBLANES, SUBLANES), SUBLANES)],
                sem.at[sl])
        return make

    @pl.when(step == 0)
    def _():
        _start_tile_dmas(tile_copy(dest_ref, 0), ts)

    _wait_all_tiles(ys_ref, ybuf.at[slot], sem.at[slot], to_hbm=False)
    _start_tile_dmas(tile_copy(dnext_ref, 1 - slot), ts, inline=inline)
    yv = ybuf.at[slot]
    y = jnp.concatenate([_token_rows(yv, s, ts) for s in range(SUBLANES)], axis=1)

    def retire_last():
        @pl.when(step + 1 == n_steps)
        def _():
            _wait_all_tiles(ys_ref, ybuf.at[1 - slot], sem.at[1 - slot], to_hbm=False)

    return y, retire_last


def _route(logits_t):
    ts = logits_t.shape[1]
    row = lax.broadcasted_iota(jnp.int32, (SUBLANES, ts), 0)
    gl = jnp.where(row < N_GROUPS, logits_t[0:SUBLANES], _NEG_INF)
    gmax = jnp.max(gl, axis=0, keepdims=True)
    gsel = jnp.min(jnp.where(gl == gmax, row, SUBLANES), axis=0, keepdims=True)
    p_g = 1.0 / jnp.sum(jnp.exp(gl - gmax), axis=0, keepdims=True)
    es = jnp.zeros((SUBLANES, ts), jnp.float32)
    for g in range(N_GROUPS):
        lo = SUBLANES * (1 + g)
        es = jnp.where(gsel == g, logits_t[lo:lo + EXPERTS_PER_GROUP], es)
    m1 = jnp.max(es, axis=0, keepdims=True)
    i1 = jnp.min(jnp.where(es == m1, row, SUBLANES), axis=0, keepdims=True)
    es2 = jnp.where(row == i1, _NEG_INF, es)
    m2 = jnp.max(es2, axis=0, keepdims=True)
    i2 = jnp.min(jnp.where(es2 == m2, row, SUBLANES), axis=0, keepdims=True)
    t = jnp.exp(m2 - m1)
    w1 = p_g / (1.0 + t)
    w2 = w1 * t
    first_lo = i1 < i2
    e_lo = jnp.minimum(i1, i2)
    e_hi = jnp.maximum(i1, i2)
    gate_lo = jnp.where(first_lo, w1, w2)
    gate_hi = jnp.where(first_lo, w2, w1)
    pair = lax.shift_right_logical(e_lo * (2 * EXPERTS_PER_GROUP - 1 - e_lo), 1) + (e_hi - e_lo - 1)
    cls = gsel * PAIRS_PER_GROUP + pair
    return cls, gate_lo, gate_hi


def _post_mixer(x1, mod_ref, n2_ref, wrt_ref, brt_ref, hrow_ref, cls_ref, rank_ref, cnt_ref, sub):
    ts = x1.shape[0]
    hrow_ref = hrow_ref.at[pl.ds(sub * ts * SUBLANES, ts * SUBLANES)]
    cols = slice(sub * ts, (sub + 1) * ts)
    h2 = _rms_mod(x1, n2_ref[...], mod_ref[3], mod_ref[4])
    h2b = _bf16(h2)
    logits_t = lax.dot_general(wrt_ref[...], h2b, (((1,), (1,)), ((), ())),
                               preferred_element_type=jnp.float32) + brt_ref[:, 0:1]
    cls, gate_lo, gate_hi = _route(logits_t)
    gates_t = jnp.concatenate(
        [gate_lo, gate_hi, jnp.zeros((LANES - 2, ts), jnp.float32)], axis=0)
    h2r = h2b.astype(jnp.float32)
    for j in range(PACK_SUBLANES):
        hi = _bits(h2r[:, (2 * j) * LANES:(2 * j + 1) * LANES])
        lo = _bits(h2r[:, (2 * j + 1) * LANES:(2 * j + 2) * LANES])
        hrow_ref[pl.ds(j, ts, stride=SUBLANES), :] = hi | lax.shift_right_logical(lo, jnp.uint32(16))
    hrow_ref[pl.ds(PACK_SUBLANES, ts, stride=SUBLANES), :] = _bits(gates_t.T)
    for j in range(PACK_SUBLANES + 1, SUBLANES):
        hrow_ref[pl.ds(j, ts, stride=SUBLANES), :] = jnp.zeros((ts, LANES), jnp.uint32)
    cls_ref[0, :, cols] = cls
    crow = lax.broadcasted_iota(jnp.int32, (LANES, ts), 0)
    onehot = jnp.where(crow == cls, 1.0, 0.0)
    cnt = jnp.sum(onehot, axis=1, keepdims=True)
    earlier = (lax.broadcasted_iota(jnp.int32, (ts, ts), 0)
               < lax.broadcasted_iota(jnp.int32, (ts, ts), 1))
    prefix = _dot(_bf16(onehot), _bf16(jnp.where(earlier, 1.0, 0.0)))
    rank = jnp.sum(onehot * (prefix + cnt_ref[:, 0:1]), axis=0, keepdims=True)
    rank_ref[0, :, cols] = rank.astype(jnp.int32)
    cnt_ref[...] += jnp.broadcast_to(cnt, cnt_ref.shape)


def _mixer_out(t, d, ts):
    n_tiles = t // ts
    out_shape = (
        jax.ShapeDtypeStruct((t, d), jnp.float32),
        jax.ShapeDtypeStruct((t * SUBLANES, LANES), jnp.uint32),
        jax.ShapeDtypeStruct((n_tiles, 1, ts), jnp.int32),
        jax.ShapeDtypeStruct((n_tiles, 1, ts), jnp.int32),
        jax.ShapeDtypeStruct((LANES, LANES), jnp.float32),
    )
    out_specs = (
        pl.BlockSpec((ts, d), lambda i: (i, 0)),
        pl.BlockSpec((ts * SUBLANES, LANES), lambda i: (i, 0)),
        pl.BlockSpec((1, 1, ts), lambda i: (i, 0, 0)),
        pl.BlockSpec((1, 1, ts), lambda i: (i, 0, 0)),
        pl.BlockSpec((LANES, LANES), lambda i: (0, 0)),
    )
    return out_shape, out_specs


def _full(shape):
    return pl.BlockSpec(shape, lambda i: tuple(0 for _ in shape))


def _to_tile_order(x_ref, perm, xs):
    bsz, tm, d = x_ref.shape
    for cb in range(d // LANES):
        for p in range(bsz):
            perm[cb, PERM_PITCH * p:PERM_PITCH * p + tm, :] = x_ref[p, :, cb * LANES:(cb + 1) * LANES]
    for m in range(tm):
        for cb in range(d // LANES):
            xs[m * bsz:(m + 1) * bsz, cb * LANES:(cb + 1) * LANES] = (
                perm.at[cb][pl.ds(m, bsz, stride=PERM_PITCH), :])


def _conf_kernel(x_ref, mod_ref, n1_ref, n2_ref, w1_ref, b1_ref, dw_ref, dwb_ref,
                 lng_ref, lnb_ref, w2_ref, b2_ref, wrt_ref, brt_ref,
                 x1_ref, hrow_ref, cls_ref, rank_ref, cnt_ref, zeros_ref, perm, xs, vext, conv):
    i = pl.program_id(0)
    bsz, tm, d = x_ref.shape
    ts = bsz * tm
    n_chunks = d // LANES
    hist = SUBLANES * (CONF_KERNEL - 1)

    @pl.when(i == 0)
    def _():
        vext[:, 0:hist, :] = jnp.zeros((n_chunks, hist, LANES), jnp.float32)
        cnt_ref[...] = jnp.zeros_like(cnt_ref)

    zeros_ref[...] = jnp.zeros_like(zeros_ref)

    _to_tile_order(x_ref, perm, xs)
    for sub in range(ts // SUB_TILE):
        rows = slice(sub * SUB_TILE, (sub + 1) * SUB_TILE)
        h = _bf16(_rms_mod(xs[rows, :], n1_ref[...], mod_ref[0], mod_ref[1]))
        u = _dot(h, w1_ref[...]) + b1_ref[...]
        v = u[:, 0:d] * jax.nn.sigmoid(u[:, d:2 * d])
        for cb in range(n_chunks):
            vext[cb, hist + sub * SUB_TILE:hist + (sub + 1) * SUB_TILE, :] = v[:, cb * LANES:(cb + 1) * LANES]

    _causal_taps_chunked(dw_ref, dwb_ref, vext, conv, CONF_KERNEL, ts)
    vext[:, 0:hist, :] = vext[:, ts:ts + hist, :]

    for sub in range(ts // SUB_TILE):
        rows = slice(sub * SUB_TILE, (sub + 1) * SUB_TILE)
        y = jnp.concatenate([conv[cb, rows, :] for cb in range(n_chunks)], axis=1)
        mu = jnp.mean(y, axis=-1, keepdims=True)
        yc = y - mu
        var = jnp.mean(yc * yc, axis=-1, keepdims=True)
        z = yc * lax.rsqrt(var + EPS) * lng_ref[...] + lnb_ref[...]
        z = z * jax.nn.sigmoid(z)
        mix = _dot(_bf16(z), w2_ref[...]) + b2_ref[...]
        x1 = xs[rows, :] + _per_batch(mod_ref[2], SUB_TILE) * mix
        x1_ref[rows, :] = x1
        _post_mixer(x1, mod_ref, n2_ref, wrt_ref, brt_ref, hrow_ref, cls_ref, rank_ref, cnt_ref, sub)


def _conf_call(x, mod, n1, n2, w1, b1, dw, dwb, lng, lnb, w2, b2, wrt, brt, zero_rows):
    bsz, seq, d = x.shape
    tm = TIME_TILE
    ts = bsz * tm
    n_steps = seq // tm
    assert zero_rows % (n_steps * SUBLANES) == 0
    out_shape, out_specs = _mixer_out(bsz * seq, d, ts)
    out_shape += (jax.ShapeDtypeStruct((zero_rows, LANES), jnp.uint32),)
    out_specs += (pl.BlockSpec((zero_rows // n_steps, LANES), lambda i: (i, 0)),)
    hist = SUBLANES * (CONF_KERNEL - 1)
    return pl.pallas_call(
        _conf_kernel,
        out_shape=out_shape,
        grid=(n_steps,),
        in_specs=[
            pl.BlockSpec((bsz, tm, d), lambda i: (0, i, 0)),
            _full((MOD_ROWS, bsz, d)),
            _full((1, d)), _full((1, d)),
            _full((d, 2 * d)), _full((1, 2 * d)),
            _full((d // LANES, CONF_KERNEL + 1, LANES)), _full((d // LANES, 1, LANES)),
            _full((1, d)), _full((1, d)),
            _full((d, d)), _full((1, d)),
            _full((ROUTER_ROWS, d)), _full((ROUTER_ROWS, LANES)),
        ],
        out_specs=out_specs,
        scratch_shapes=[
            pltpu.VMEM((d // LANES, bsz * PERM_PITCH, LANES), jnp.float32),
            pltpu.VMEM((ts, d), jnp.float32),
            pltpu.VMEM((d // LANES, hist + ts, LANES), jnp.float32),
            pltpu.VMEM((d // LANES, ts, LANES), jnp.float32),
        ],
        compiler_params=pltpu.CompilerParams(
            dimension_semantics=("arbitrary",), vmem_limit_bytes=VMEM_LIMIT_BYTES),
        name="conf_mixer",
    )(x, mod, n1, n2, w1, b1, dw, dwb, lng, lnb, w2, b2, wrt, brt)


def _lru_kernel(n_tiles, dest_ref, dnext_ref, xprev_ref, pmod_ref, mod_ref, n1_ref, n2_ref,
                wy_ref, by_ref, wx_ref, bx_ref,
                cw_ref, cb_ref, wa_ref, ba_ref, wi_ref, bi_ref, lam_ref, wo_ref, bo_ref,
                wrt_ref, brt_ref, ys_ref,
                x1_ref, hrow_ref, cls_ref, rank_ref, cnt_ref, uext, hcar, ybuf, ysem):
    i = pl.program_id(0)
    ts, d = xprev_ref.shape
    hd = d // LRU_HEADS
    hist = SUBLANES * (LRU_CONV - 1)

    @pl.when(i == 0)
    def _():
        uext[0:hist, :] = jnp.zeros((hist, d), jnp.float32)
        hcar[...] = jnp.zeros_like(hcar)
        cnt_ref[...] = jnp.zeros_like(cnt_ref)

    y_prev, retire_last = _gather_expert_tiles(ys_ref, ybuf, ysem, dest_ref, dnext_ref, i, n_tiles, ts,
                                               inline=True)
    nl = -lam_ref[...]
    softplus = jnp.maximum(nl, 0.0) + jnp.log1p(jnp.exp(-jnp.abs(nl)))
    hstate = hcar[...]
    for sub in range(ts // SUB_TILE):
        rows = slice(sub * SUB_TILE, (sub + 1) * SUB_TILE)
        x = xprev_ref[rows, :] + _per_batch(pmod_ref[5], SUB_TILE) * y_prev[rows, :]
        h = _bf16(_rms_mod(x, n1_ref[...], mod_ref[0], mod_ref[1]))
        ygate = jax.nn.gelu(_dot(h, wy_ref[...]) + by_ref[...])
        xb = _dot(h, wx_ref[...]) + bx_ref[...]
        uext[hist + sub * SUB_TILE:hist + (sub + 1) * SUB_TILE, :] = xb
        u = _causal_taps(cw_ref, cb_ref, uext, LRU_CONV, sub * SUB_TILE, SUB_TILE)

        ub = _bf16(u)
        r_parts, i_parts = [], []
        for hh in range(LRU_HEADS):
            uh = ub[:, hh * hd:(hh + 1) * hd]
            r_parts.append(_dot(uh, wa_ref[hh]))
            i_parts.append(_dot(uh, wi_ref[hh]))
        r = jax.nn.sigmoid(jnp.concatenate(r_parts, axis=1) + ba_ref[...])
        ig = jax.nn.sigmoid(jnp.concatenate(i_parts, axis=1) + bi_ref[...])

        log_a = (-LRU_C) * r * softplus
        a = jnp.exp(log_a)
        mult = jnp.sqrt(-jnp.tanh(log_a) * (1.0 + a * a))
        b = mult * (ig * u)

        outs = []
        for m in range(SUB_TILE // SUBLANES):
            sl = slice(m * SUBLANES, (m + 1) * SUBLANES)
            hstate = a[sl] * hstate + b[sl]
            outs.append(hstate)
        hs = jnp.concatenate(outs, axis=0)

        mix = _dot(_bf16(hs * ygate), wo_ref[...]) + bo_ref[...]
        x1 = x + _per_batch(mod_ref[2], SUB_TILE) * mix
        x1_ref[rows, :] = x1
        _post_mixer(x1, mod_ref, n2_ref, wrt_ref, brt_ref, hrow_ref, cls_ref, rank_ref, cnt_ref, sub)
    hcar[...] = hstate
    uext[0:hist, :] = uext[ts:ts + hist, :]
    retire_last()


def _lru_call(dest, x_prev, mod_prev, ys, mod, n1, n2, wy, by, wx, bx, cw, cb, wa, ba, wi, bi, lam,
              wo, bo, wrt, brt):
    t, d = x_prev.shape
    n_tiles, _, ts = dest.shape
    bsz = mod.shape[1]
    hd = d // LRU_HEADS
    hist = SUBLANES * (LRU_CONV - 1)
    out_shape, out_specs = _mixer_out(t, d, ts)
    return pl.pallas_call(
        functools.partial(_lru_kernel, n_tiles),
        out_shape=out_shape,
        grid=(n_tiles,),
        in_specs=[
            pl.BlockSpec((1, 1, ts), lambda i: (i, 0, 0), memory_space=pltpu.SMEM),
            pl.BlockSpec((1, 1, ts), lambda i: (jnp.minimum(i + 1, n_tiles - 1), 0, 0),
                         memory_space=pltpu.SMEM),
            pl.BlockSpec((ts, d), lambda i: (i, 0)),
            _full((MOD_ROWS, bsz, d)), _full((MOD_ROWS, bsz, d)),
            _full((1, d)), _full((1, d)),
            _full((d, d)), _full((1, d)),
            _full((d, d)), _full((1, d)),
            _full((LRU_CONV, d)), _full((1, d)),
            _full((LRU_HEADS, hd, hd)), _full((1, d)),
            _full((LRU_HEADS, hd, hd)), _full((1, d)),
            _full((1, d)),
            _full((d, d)), _full((1, d)),
            _full((ROUTER_ROWS, d)), _full((ROUTER_ROWS, LANES)),
            pl.BlockSpec(memory_space=pl.ANY),
        ],
        out_specs=out_specs,
        scratch_shapes=[
            pltpu.VMEM((hist + ts, d), jnp.float32),
            pltpu.VMEM((SUBLANES, d), jnp.float32),
            pltpu.VMEM((2, ts * SUBLANES, LANES), jnp.float32),
            pltpu.SemaphoreType.DMA((2,)),
        ],
        compiler_params=pltpu.CompilerParams(
            dimension_semantics=("arbitrary",), vmem_limit_bytes=VMEM_LIMIT_BYTES),
        name="lru_mixer",
    )(dest, dest, x_prev, mod_prev, mod, n1, n2, wy, by, wx, bx, cw, cb, wa, ba, wi, bi, lam, wo, bo,
      wrt, brt, ys)


def _plan_kernel(n_blocks, cnt_ref, clo_ref, chi_ref, cls_ref, rank_ref,
                 dest_ref, blo_ref, bhi_ref, nblk_ref, class_start):
    n_tiles, _, ts = cls_ref.shape
    class_start[...] = jnp.zeros_like(class_start)

    def per_class(c, nb_total):
        class_start[pl.ds(c, 1), :] = jnp.full((1, LANES), nb_total * ROW_BLOCK, jnp.int32)
        nb = (cnt_ref[c] + (ROW_BLOCK - 1)) // ROW_BLOCK

        def fill(k, carry):
            blo_ref[nb_total + k] = clo_ref[c]
            bhi_ref[nb_total + k] = chi_ref[c]
            return carry

        lax.fori_loop(0, nb, fill, 0)
        return nb_total + nb

    nb_used = lax.fori_loop(0, N_CLASSES, per_class, 0)
    nblk_ref[0] = nb_used
    last = jnp.maximum(nb_used - 1, 0)

    def tail(k, carry):
        blo_ref[k] = blo_ref[last]
        bhi_ref[k] = bhi_ref[last]
        return carry

    lax.fori_loop(nb_used, n_blocks, tail, 0)

    starts = class_start[:, 0:1]
    class_id = lax.broadcasted_iota(jnp.int32, (LANES, ts), 0)

    def per_tile(t, carry):
        first = jnp.sum(jnp.where(class_id == cls_ref[t], starts, 0), axis=0, keepdims=True)
        dest_ref[t] = first + rank_ref[t]
        return carry

    lax.fori_loop(0, n_tiles, per_tile, 0)


def _plan_call(cls, rank, counts, n_blocks):
    n_tiles, _, ts = cls.shape
    smem_full = pl.BlockSpec(memory_space=pltpu.SMEM)
    vmem_full = pl.BlockSpec(memory_space=pltpu.VMEM)
    return pl.pallas_call(
        functools.partial(_plan_kernel, n_blocks),
        out_shape=(
            jax.ShapeDtypeStruct((n_tiles, 1, ts), jnp.int32),
            jax.ShapeDtypeStruct((n_blocks,), jnp.int32),
            jax.ShapeDtypeStruct((n_blocks,), jnp.int32),
            jax.ShapeDtypeStruct((1,), jnp.int32),
        ),
        in_specs=[smem_full, smem_full, smem_full, vmem_full, vmem_full],
        out_specs=(vmem_full, smem_full, smem_full, smem_full),
        scratch_shapes=[pltpu.VMEM((LANES, LANES), jnp.int32)],
        compiler_params=pltpu.CompilerParams(vmem_limit_bytes=VMEM_LIMIT_BYTES),
        name="moe_plan",
    )(counts, jnp.asarray(_CLASS_LO), jnp.asarray(_CLASS_HI), cls, rank)


def _dispatch_kernel(n_tiles, dest_ref, hrow_ref, xs_in_ref, xs_ref, stage, sem_in, sem_out):
    del xs_in_ref
    i = pl.program_id(0)
    ts = dest_ref.shape[2]
    rows = ts * SUBLANES

    def load(tile, slot):
        return pltpu.make_async_copy(hrow_ref.at[pl.ds(pl.multiple_of(tile * rows, rows), rows)],
                                     stage.at[slot], sem_in.at[slot])

    def tile_copy(slot, r, dst_row):
        return pltpu.make_async_copy(
            stage.at[slot, pl.ds(pl.multiple_of(r * SUBLANES, SUBLANES), SUBLANES)],
            xs_ref.at[pl.ds(pl.multiple_of(dst_row * SUBLANES, SUBLANES), SUBLANES)],
            sem_out.at[slot])

    def drain(slot):
        _wait_all_tiles(xs_ref, stage.at[slot], sem_out.at[slot], to_hbm=True)

    slot = i % DISPATCH_SLOTS
    nxt = (i + 1) % DISPATCH_SLOTS

    @pl.when(i == 0)
    def _():
        load(0, 0).start()

    @pl.when(i >= 2)
    def _():
        drain(nxt)

    @pl.when(i + 1 < n_tiles)
    def _():
        load(i + 1, nxt).start()

    load(i, slot).wait()

    _start_tile_dmas(lambda r: tile_copy(slot, r, dest_ref[0, 0, r]), ts)

    @pl.when(i == n_tiles - 1)
    def _():
        @pl.when(i >= 1)
        def _():
            drain((i + DISPATCH_SLOTS - 1) % DISPATCH_SLOTS)
        drain(slot)


def _dispatch_call(dest, hrows, xs_init):
    n_tiles, _, ts = dest.shape
    assert xs_init.dtype == jnp.uint32 and xs_init.shape[1] == LANES
    any_spec = pl.BlockSpec(memory_space=pl.ANY)
    return pl.pallas_call(
        functools.partial(_dispatch_kernel, n_tiles),
        out_shape=jax.ShapeDtypeStruct(xs_init.shape, jnp.uint32),
        grid=(n_tiles,),
        in_specs=[
            pl.BlockSpec((1, 1, ts), lambda i: (i, 0, 0), memory_space=pltpu.SMEM),
            any_spec, any_spec,
        ],
        out_specs=any_spec,
        scratch_shapes=[
            pltpu.VMEM((DISPATCH_SLOTS, ts * SUBLANES, LANES), jnp.uint32),
            pltpu.SemaphoreType.DMA((DISPATCH_SLOTS,)),
            pltpu.SemaphoreType.DMA((DISPATCH_SLOTS,)),
        ],
        input_output_aliases={2: 0},
        compiler_params=pltpu.CompilerParams(
            dimension_semantics=("arbitrary",), vmem_limit_bytes=VMEM_LIMIT_BYTES),
        name="moe_dispatch",
    )(dest, hrows, xs_init)


def _expert_kernel(blo_ref, bhi_ref, nblk_ref, xs_ref, wg_lo, wu_lo, wd_lo, wg_hi, wu_hi, wd_hi, ys_ref):
    del blo_ref, bhi_ref
    b = pl.program_id(0)
    rb = ROW_BLOCK

    @pl.when(b < nblk_ref[0])
    def _():
        pieces = []
        for j in range(PACK_SUBLANES):
            w = _token_rows(xs_ref, j, rb)
            pieces.append(_bf16(_f32_from_bits(w & jnp.uint32(0xFFFF0000))))
            pieces.append(_bf16(_f32_from_bits(lax.shift_left(w, jnp.uint32(16)))))
        xb = jnp.concatenate(pieces, axis=1)
        gates = _f32_from_bits(_token_rows(xs_ref, PACK_SUBLANES, rb))
        acc = None
        for idx, (wg, wu, wd) in enumerate(((wg_lo, wu_lo, wd_lo), (wg_hi, wu_hi, wd_hi))):
            a = _dot(xb, wg[...])
            hid = (a * jax.nn.sigmoid(a)) * _dot(xb, wu[...]) * gates[:, idx:idx + 1]
            part = _dot(_bf16(hid), wd[...])
            acc = part if acc is None else acc + part
        for s in range(SUBLANES):
            ys_ref[pl.ds(s, rb, stride=SUBLANES), :] = acc[:, s * LANES:(s + 1) * LANES]

    @pl.when(b >= nblk_ref[0])
    def _():
        ys_ref[...] = jnp.zeros_like(ys_ref)


def _expert_call(blo, bhi, nblk, xs, wg, wu, wd, layer):
    n_rows = xs.shape[0] // SUBLANES
    d, de = wg.shape[2], wg.shape[3]
    assert d == SUBLANES * LANES
    n_blocks = n_rows // ROW_BLOCK

    def w_spec(shape, which):
        if which == 0:
            return pl.BlockSpec((None, None) + shape, lambda b, lo, hi, n: (layer, lo[b], 0, 0))
        return pl.BlockSpec((None, None) + shape, lambda b, lo, hi, n: (layer, hi[b], 0, 0))

    def x_map(b, lo, hi, n):
        return (jnp.minimum(b, jnp.maximum(n[0] - 1, 0)), 0)

    grid_spec = pltpu.PrefetchScalarGridSpec(
        num_scalar_prefetch=3,
        grid=(n_blocks,),
        in_specs=[
            pl.BlockSpec((ROW_BLOCK * SUBLANES, LANES), x_map),
            w_spec((d, de), 0), w_spec((d, de), 0), w_spec((de, d), 0),
            w_spec((d, de), 1), w_spec((d, de), 1), w_spec((de, d), 1),
        ],
        out_specs=pl.BlockSpec((ROW_BLOCK * SUBLANES, LANES), lambda b, lo, hi, n: (b, 0)),
    )
    return pl.pallas_call(
        _expert_kernel,
        out_shape=jax.ShapeDtypeStruct((n_rows * SUBLANES, LANES), jnp.float32),
        grid_spec=grid_spec,
        compiler_params=pltpu.CompilerParams(
            dimension_semantics=("arbitrary",), vmem_limit_bytes=VMEM_LIMIT_BYTES),
        name="moe_experts",
    )(blo, bhi, nblk, xs, wg, wu, wd, wg, wu, wd)


def _final_kernel(n_tiles, dest_ref, dnext_ref, x1_ref, mod_ref, fmod_ref, fg_ref, ys_ref,
                  out_ref, ybuf, sem, unperm):
    i = pl.program_id(0)
    ts, d = x1_ref.shape
    bsz, tm, _ = out_ref.shape
    y, retire_last = _gather_expert_tiles(ys_ref, ybuf, sem, dest_ref, dnext_ref, i, n_tiles, ts,
                                          inline=False)
    x2 = x1_ref[...] + _per_batch(mod_ref[5], ts) * y
    res = _rms_mod(x2, fg_ref[...], fmod_ref[0], fmod_ref[1])
    for cb in range(d // LANES):
        unperm[cb] = res[:, cb * LANES:(cb + 1) * LANES]
    for cb in range(d // LANES):
        for p in range(bsz):
            out_ref[p, :, cb * LANES:(cb + 1) * LANES] = unperm.at[cb][pl.ds(p, tm, stride=SUBLANES), :]
    retire_last()


def _final_call(dest, x1, mod, ys, bsz, fmod, fin_g):
    t, d = x1.shape
    n_tiles, _, ts = dest.shape
    tm = ts // bsz
    return pl.pallas_call(
        functools.partial(_final_kernel, n_tiles),
        out_shape=jax.ShapeDtypeStruct((bsz, t // bsz, d), jnp.float32),
        grid=(n_tiles,),
        in_specs=[
            pl.BlockSpec((1, 1, ts), lambda i: (i, 0, 0), memory_space=pltpu.SMEM),
            pl.BlockSpec((1, 1, ts), lambda i: (jnp.minimum(i + 1, n_tiles - 1), 0, 0),
                         memory_space=pltpu.SMEM),
            pl.BlockSpec((ts, d), lambda i: (i, 0)),
            _full((MOD_ROWS, bsz, d)), _full((MOD_ROWS, bsz, d)), _full((1, d)),
            pl.BlockSpec(memory_space=pl.ANY),
        ],
        out_specs=pl.BlockSpec((bsz, tm, d), lambda i: (0, i, 0)),
        scratch_shapes=[pltpu.VMEM((2, ts * SUBLANES, LANES), jnp.float32),
                        pltpu.SemaphoreType.DMA((2,)),
                        pltpu.VMEM((d // LANES, ts, LANES), jnp.float32)],
        compiler_params=pltpu.CompilerParams(
            dimension_semantics=("arbitrary",), vmem_limit_bytes=VMEM_LIMIT_BYTES),
        name="moe_combine_final",
    )(dest, dest, x1, mod, fmod, fin_g, ys)


def _router_operands(wg, bg, wr, br):
    d = wg.shape[0]
    pad_g = jnp.zeros((SUBLANES - N_GROUPS, d), jnp.float32)
    pad_e = jnp.zeros((ROUTER_ROWS - SUBLANES - N_EXPERTS, d), jnp.float32)
    wrt = jnp.concatenate([wg.T, pad_g, wr.T, pad_e], axis=0)
    bcol = jnp.concatenate([bg, jnp.zeros((SUBLANES - N_GROUPS,), jnp.float32), br,
                            jnp.zeros((ROUTER_ROWS - SUBLANES - N_EXPERTS,), jnp.float32)])
    brt = jnp.broadcast_to(bcol[:, None], (ROUTER_ROWS, LANES))
    return _bf16(wrt), brt


def _mod_table(m, parts):
    bsz = m.shape[0]
    d = m.shape[1] // parts
    m = jnp.transpose(m.reshape(bsz, parts, d), (1, 0, 2))
    return jnp.concatenate([m, jnp.zeros((MOD_ROWS - parts, bsz, d), jnp.float32)], axis=0)


def _row(v):
    return v.reshape(1, -1)


def _moe_experts(hrows, cls, rank, cnt, wg, wu, wd, layer, n_blocks, xs_init):
    counts = cnt[:, 0].astype(jnp.int32)
    dest, blo, bhi, nblk = _plan_call(cls, rank, counts, n_blocks)
    xs = _dispatch_call(dest, hrows, xs_init)
    ys = _expert_call(blo, bhi, nblk, xs, wg, wu, wd, layer)
    return xs, ys, dest


def kernel(x, c, ada_w, ada_b, norm1_g, norm2_g, cm_w1, cm_b1, cm_dw, cm_dwb, cm_ln_g, cm_ln_b, cm_w2, cm_b2, rg_wy, rg_by, rg_wx, rg_bx, rg_cw, rg_cb, rg_wa, rg_ba, rg_wi, rg_bi, rg_lambda, rg_wo, rg_bo, moe_wg, moe_bg, moe_wr, moe_br, moe_w_gate, moe_w_up, moe_w_down, fin_ada_w, fin_ada_b, fin_g):
    bsz, seq, d = x.shape
    t = bsz * seq
    depth = ada_w.shape[0]
    assert depth == 2 and bsz == SUBLANES and seq % TIME_TILE == 0
    assert d == 2 * PACK_SUBLANES * LANES

    mods = _ada_call(c, ada_w, ada_b[:, None, :])
    fmods = _ada_call(c, fin_ada_w[None], fin_ada_b[None, None, :])
    fmod = _mod_table(fmods[0], 2)
    n_steps = seq // TIME_TILE
    granule = n_steps // math.gcd(n_steps, ROW_BLOCK)
    n_blocks = -(-(-(-t // ROW_BLOCK) + N_CLASSES) // granule) * granule
    mod0, mod1 = _mod_table(mods[0], 6), _mod_table(mods[1], 6)
    moe_w_gate, moe_w_up, moe_w_down = _bf16(moe_w_gate), _bf16(moe_w_up), _bf16(moe_w_down)

    wrt, brt = _router_operands(moe_wg[0], moe_bg[0], moe_wr[0], moe_br[0])
    dw = jnp.concatenate([cm_dw[0], jnp.zeros((1, d), jnp.float32)], axis=0)
    dw = jnp.transpose(dw.reshape(CONF_KERNEL + 1, d // LANES, LANES), (1, 0, 2))
    x1, hrows, cls, rank, cnt, xs_init = _conf_call(
        x, mod0, _row(norm1_g[0]), _row(norm2_g[0]), _bf16(cm_w1[0]), _row(cm_b1[0]),
        dw, cm_dwb[0].reshape(d // LANES, 1, LANES), _row(cm_ln_g[0]), _row(cm_ln_b[0]),
        _bf16(cm_w2[0]), _row(cm_b2[0]), wrt, brt, n_blocks * ROW_BLOCK * SUBLANES)
    xs, ys, dest = _moe_experts(hrows, cls, rank, cnt, moe_w_gate, moe_w_up, moe_w_down, 0,
                                n_blocks, xs_init)

    wrt, brt = _router_operands(moe_wg[1], moe_bg[1], moe_wr[1], moe_br[1])
    x1, hrows, cls, rank, cnt = _lru_call(
        dest, x1, mod0, ys, mod1, _row(norm1_g[1]), _row(norm2_g[1]),
        _bf16(rg_wy[0]), _row(rg_by[0]), _bf16(rg_wx[0]), _row(rg_bx[0]),
        rg_cw[0], _row(rg_cb[0]), _bf16(rg_wa[0]), _row(rg_ba[0]), _bf16(rg_wi[0]), _row(rg_bi[0]),
        _row(rg_lambda[0]), _bf16(rg_wo[0]), _row(rg_bo[0]), wrt, brt)
    _, ys, dest = _moe_experts(hrows, cls, rank, cnt, moe_w_gate, moe_w_up, moe_w_down, 1,
                               n_blocks, xs)
    return _final_call(dest, x1, mod1, ys, bsz, fmod, _row(fin_g))
```

```python
import functools
import math

import jax
import jax.numpy as jnp
import numpy as np
from jax import lax
from jax.experimental import pallas as pl
from jax.experimental.pallas import tpu as pltpu

EPS = 1e-6
CONF_KERNEL = 31
LRU_HEADS = 4
LRU_CONV = 4
LRU_C = 8.0
N_GROUPS = 4
EXPERTS_PER_GROUP = 8
N_EXPERTS = N_GROUPS * EXPERTS_PER_GROUP
PAIRS_PER_GROUP = EXPERTS_PER_GROUP * (EXPERTS_PER_GROUP - 1) // 2
N_CLASSES = N_GROUPS * PAIRS_PER_GROUP

LANES = 128
SUBLANES = 8
BF16_SUBLANES = 16
VMEM_LIMIT_BYTES = 56 * 1024 * 1024

TOKEN_TILE = 512
TIME_TILE = TOKEN_TILE // SUBLANES
SUB_TILE = 256
ROW_BLOCK = 320
PERM_PITCH = TIME_TILE + SUBLANES
CONV_ROWS = 128
ROUTER_ROWS = 64
PACK_SUBLANES = 4
MOD_ROWS = 8
DISPATCH_SLOTS = 3
DMA_UNROLL = 8

_NEG_INF = float("-inf")


def _class_tables():
    lo = np.zeros((N_CLASSES,), np.int32)
    hi = np.zeros((N_CLASSES,), np.int32)
    for g in range(N_GROUPS):
        for a in range(EXPERTS_PER_GROUP):
            for b in range(a + 1, EXPERTS_PER_GROUP):
                c = g * PAIRS_PER_GROUP + (a * (2 * EXPERTS_PER_GROUP - 1 - a)) // 2 + (b - a - 1)
                lo[c] = g * EXPERTS_PER_GROUP + a
                hi[c] = g * EXPERTS_PER_GROUP + b
    return lo, hi


_CLASS_LO, _CLASS_HI = _class_tables()


def _bf16(x):
    return x.astype(jnp.bfloat16)


def _dot(a, b):
    return jnp.dot(a, b, preferred_element_type=jnp.float32)


def _bits(x):
    return lax.bitcast_convert_type(x, jnp.uint32)


def _f32_from_bits(x):
    return lax.bitcast_convert_type(x, jnp.float32)


def _token_rows(ref, sublane, n):
    return ref[pl.ds(sublane, n, stride=SUBLANES), :]


def _per_batch(v, ts):
    return jnp.tile(v, (ts // SUBLANES, 1))


def _ada_kernel(c_ref, w_ref, b_ref, o_ref):
    c = c_ref[...]
    ca = _bf16(c * jax.nn.sigmoid(c))
    o_ref[...] = _dot(ca, _bf16(w_ref[...])) + b_ref[...]


def _ada_call(c, w, b):
    n_layers, d, n = w.shape
    bsz = c.shape[0]
    nb = 1024
    return pl.pallas_call(
        _ada_kernel,
        out_shape=jax.ShapeDtypeStruct((n_layers, bsz, n), jnp.float32),
        grid=(n_layers, n // nb),
        in_specs=[
            pl.BlockSpec((bsz, d), lambda l, j: (0, 0)),
            pl.BlockSpec((None, d, nb), lambda l, j: (l, 0, j)),
            pl.BlockSpec((None, 1, nb), lambda l, j: (l, 0, j)),
        ],
        out_specs=pl.BlockSpec((None, bsz, nb), lambda l, j: (l, 0, j)),
        compiler_params=pltpu.CompilerParams(
            dimension_semantics=("arbitrary", "arbitrary"),
            vmem_limit_bytes=VMEM_LIMIT_BYTES),
        name="ada_mod",
    )(c, w, b)


def _rms_mod(x, gain, shift, scale):
    ts = x.shape[0]
    ms = jnp.mean(x * x, axis=-1, keepdims=True)
    return (x * lax.rsqrt(ms + EPS)) * _per_batch(gain * (1.0 + scale), ts) + _per_batch(shift, ts)


def _causal_taps(w_ref, bias_ref, ext_ref, n_taps, row0, n):
    d = ext_ref.shape[1]
    cols = []
    for cb in range(d // LANES):
        sl = slice(cb * LANES, (cb + 1) * LANES)
        acc = jnp.broadcast_to(bias_ref[:, sl], (n, LANES))
        for k in range(n_taps):
            lo = row0 + SUBLANES * k
            acc = acc + w_ref[k:k + 1, sl] * ext_ref[lo:lo + n, sl]
        cols.append(acc)
    return jnp.concatenate(cols, axis=1)


def _causal_taps_chunked(w_ref, bias_ref, ext_ref, out_ref, n_taps, ts):
    n_chunks = ext_ref.shape[0]
    units_per_chunk = ts // CONV_ROWS

    def unit(idx, carry):
        cb = idx // units_per_chunk
        r0 = pl.multiple_of((idx % units_per_chunk) * CONV_ROWS, CONV_ROWS)
        w_cb, ext_cb = w_ref.at[cb], ext_ref.at[cb]
        acc = jnp.broadcast_to(bias_ref[cb], (CONV_ROWS, LANES))
        for k in range(n_taps):
            win = ext_cb[pl.ds(pl.multiple_of(r0 + SUBLANES * k, SUBLANES), CONV_ROWS), :]
            acc = acc + w_cb[k:k + 1, :] * win
        out_ref.at[cb][pl.ds(r0, CONV_ROWS), :] = acc
        return carry

    lax.fori_loop(0, n_chunks * units_per_chunk, unit, 0)


def _start_tile_dmas(make_copy, n, inline=False, both_queues=False):
    if inline:
        for r in range(n):
            make_copy(r).start()
        return

    def body(g, carry):
        for k in range(DMA_UNROLL):
            make_copy(g * DMA_UNROLL + k).start(priority=k % 2 if both_queues else 0)
        return carry
    lax.fori_loop(0, n // DMA_UNROLL, body, 0)


def _wait_all_tiles(hbm_ref, vmem_tiles_ref, sem, to_hbm):
    hbm_view = hbm_ref.at[pl.ds(0, vmem_tiles_ref.shape[0])]
    if to_hbm:
        pltpu.make_async_copy(vmem_tiles_ref, hbm_view, sem).wait()
    else:
        pltpu.make_async_copy(hbm_view, vmem_tiles_ref, sem).wait()


def _gather_expert_tiles(ys_ref, ybuf, sem, dest_ref, dnext_ref, step, n_steps, ts, inline):
    slot = step % 2

    def tile_copy(dref, sl):
        def make(r):
            src = pl.multiple_of(dref[0, 0, r] * SUBLANES, SUBLANES)
            return pltpu.make_async_copy(
                ys_ref.at[pl.ds(src, SUBLANES)],
                ybuf.at[sl, pl.ds(pl.multiple_of(r * SUBLANES, SUBLANES), SUBLANES)],
                sem.at[sl])
        return make

    @pl.when(step == 0)
    def _():
        _start_tile_dmas(tile_copy(dest_ref, 0), ts)

    if not inline:
        @pl.when(step + 1 < n_steps)
        def _():
            _start_tile_dmas(tile_copy(dnext_ref, 1 - slot), ts)

    _wait_all_tiles(ys_ref, ybuf.at[slot], sem.at[slot], to_hbm=False)
    if inline:
        _start_tile_dmas(tile_copy(dnext_ref, 1 - slot), ts, inline=True)
    yv = ybuf.at[slot]
    y = jnp.concatenate([_token_rows(yv, s, ts) for s in range(SUBLANES)], axis=1)

    def retire_last():
        if inline:
            @pl.when(step + 1 == n_steps)
            def _():
                _wait_all_tiles(ys_ref, ybuf.at[1 - slot], sem.at[1 - slot], to_hbm=False)

    return y, retire_last


def _route(logits_t):
    ts = logits_t.shape[1]
    row = lax.broadcasted_iota(jnp.int32, (SUBLANES, ts), 0)
    gl = jnp.where(row < N_GROUPS, logits_t[0:SUBLANES], _NEG_INF)
    gmax = jnp.max(gl, axis=0, keepdims=True)
    gsel = jnp.min(jnp.where(gl == gmax, row, SUBLANES), axis=0, keepdims=True)
    p_g = 1.0 / jnp.sum(jnp.exp(gl - gmax), axis=0, keepdims=True)
    es = jnp.zeros((SUBLANES, ts), jnp.float32)
    for g in range(N_GROUPS):
        lo = SUBLANES * (1 + g)
        es = jnp.where(gsel == g, logits_t[lo:lo + EXPERTS_PER_GROUP], es)
    m1 = jnp.max(es, axis=0, keepdims=True)
    i1 = jnp.min(jnp.where(es == m1, row, SUBLANES), axis=0, keepdims=True)
    es2 = jnp.where(row == i1, _NEG_INF, es)
    m2 = jnp.max(es2, axis=0, keepdims=True)
    i2 = jnp.min(jnp.where(es2 == m2, row, SUBLANES), axis=0, keepdims=True)
    t = jnp.exp(m2 - m1)
    w1 = p_g / (1.0 + t)
    w2 = w1 * t
    first_lo = i1 < i2
    e_lo = jnp.minimum(i1, i2)
    e_hi = jnp.maximum(i1, i2)
    gate_lo = jnp.where(first_lo, w1, w2)
    gate_hi = jnp.where(first_lo, w2, w1)
    pair = lax.shift_right_logical(e_lo * (2 * EXPERTS_PER_GROUP - 1 - e_lo), 1) + (e_hi - e_lo - 1)
    cls = gsel * PAIRS_PER_GROUP + pair
    return cls, gate_lo, gate_hi


def _post_mixer(x1, mod_ref, n2_ref, wrt_ref, brt_ref, hrow_ref, cls_ref, rank_ref, cnt_ref, sub):
    ts = x1.shape[0]
    hrow_ref = hrow_ref.at[pl.ds(sub * ts * SUBLANES, ts * SUBLANES)]
    cols = slice(sub * ts, (sub + 1) * ts)
    h2 = _rms_mod(x1, n2_ref[...], mod_ref[3], mod_ref[4])
    h2b = _bf16(h2)
    logits_t = lax.dot_general(wrt_ref[...], h2b, (((1,), (1,)), ((), ())),
                               preferred_element_type=jnp.float32) + brt_ref[:, 0:1]
    cls, gate_lo, gate_hi = _route(logits_t)
    gates_t = jnp.concatenate(
        [gate_lo, gate_hi, jnp.zeros((LANES - 2, ts), jnp.float32)], axis=0)
    h2r = h2b.astype(jnp.float32)
    for j in range(PACK_SUBLANES):
        hi = _bits(h2r[:, (2 * j) * LANES:(2 * j + 1) * LANES])
        lo = _bits(h2r[:, (2 * j + 1) * LANES:(2 * j + 2) * LANES])
        hrow_ref[pl.ds(j, ts, stride=SUBLANES), :] = hi | lax.shift_right_logical(lo, jnp.uint32(16))
    hrow_ref[pl.ds(PACK_SUBLANES, ts, stride=SUBLANES), :] = _bits(gates_t.T)
    for j in range(PACK_SUBLANES + 1, SUBLANES):
        hrow_ref[pl.ds(j, ts, stride=SUBLANES), :] = jnp.zeros((ts, LANES), jnp.uint32)
    cls_ref[0, :, cols] = cls
    crow = lax.broadcasted_iota(jnp.int32, (LANES, ts), 0)
    onehot = jnp.where(crow == cls, 1.0, 0.0)
    cnt = jnp.sum(onehot, axis=1, keepdims=True)
    earlier = (lax.broadcasted_iota(jnp.int32, (ts, ts), 0)
               < lax.broadcasted_iota(jnp.int32, (ts, ts), 1))
    prefix = _dot(_bf16(onehot), _bf16(jnp.where(earlier, 1.0, 0.0)))
    rank = jnp.sum(onehot * (prefix + cnt_ref[:, 0:1]), axis=0, keepdims=True)
    rank_ref[0, :, cols] = rank.astype(jnp.int32)
    cnt_ref[...] += jnp.broadcast_to(cnt, cnt_ref.shape)


def _mixer_out(t, d, ts):
    n_tiles = t // ts
    out_shape = (
        jax.ShapeDtypeStruct((t, d), jnp.float32),
        jax.ShapeDtypeStruct((t * SUBLANES, LANES), jnp.uint32),
        jax.ShapeDtypeStruct((n_tiles, 1, ts), jnp.int32),
        jax.ShapeDtypeStruct((n_tiles, 1, ts), jnp.int32),
        jax.ShapeDtypeStruct((LANES, LANES), jnp.float32),
    )
    out_specs = (
        pl.BlockSpec((ts, d), lambda i: (i, 0)),
        pl.BlockSpec((ts * SUBLANES, LANES), lambda i: (i, 0)),
        pl.BlockSpec((1, 1, ts), lambda i: (i, 0, 0)),
        pl.BlockSpec((1, 1, ts), lambda i: (i, 0, 0)),
        pl.BlockSpec((LANES, LANES), lambda i: (0, 0)),
    )
    return out_shape, out_specs


def _full(shape):
    return pl.BlockSpec(shape, lambda i: tuple(0 for _ in shape))


def _to_tile_order(x_ref, perm, xs):
    bsz, tm, d = x_ref.shape
    for cb in range(d // LANES):
        for p in range(bsz):
            perm[cb, PERM_PITCH * p:PERM_PITCH * p + tm, :] = x_ref[p, :, cb * LANES:(cb + 1) * LANES]
    for m in range(tm):
        for cb in range(d // LANES):
            xs[m * bsz:(m + 1) * bsz, cb * LANES:(cb + 1) * LANES] = (
                perm.at[cb][pl.ds(m, bsz, stride=PERM_PITCH), :])


def _conf_kernel(x_ref, mod_ref, n1_ref, n2_ref, w1_ref, b1_ref, dw_ref, dwb_ref,
                 lng_ref, lnb_ref, w2_ref, b2_ref, wrt_ref, brt_ref,
                 x1_ref, hrow_ref, cls_ref, rank_ref, cnt_ref, zeros_ref, perm, xs, vext, conv):
    i = pl.program_id(0)
    bsz, tm, d = x_ref.shape
    ts = bsz * tm
    n_chunks = d // LANES
    hist = SUBLANES * (CONF_KERNEL - 1)

    @pl.when(i == 0)
    def _():
        vext[:, 0:hist, :] = jnp.zeros((n_chunks, hist, LANES), jnp.float32)
        cnt_ref[...] = jnp.zeros_like(cnt_ref)

    zeros_ref[...] = jnp.zeros_like(zeros_ref)

    _to_tile_order(x_ref, perm, xs)
    for sub in range(ts // SUB_TILE):
        rows = slice(sub * SUB_TILE, (sub + 1) * SUB_TILE)
        h = _bf16(_rms_mod(xs[rows, :], n1_ref[...], mod_ref[0], mod_ref[1]))
        u = _dot(h, w1_ref[...]) + b1_ref[...]
        v = u[:, 0:d] * jax.nn.sigmoid(u[:, d:2 * d])
        for cb in range(n_chunks):
            vext[cb, hist + sub * SUB_TILE:hist + (sub + 1) * SUB_TILE, :] = v[:, cb * LANES:(cb + 1) * LANES]

    _causal_taps_chunked(dw_ref, dwb_ref, vext, conv, CONF_KERNEL, ts)
    vext[:, 0:hist, :] = vext[:, ts:ts + hist, :]

    for sub in range(ts // SUB_TILE):
        rows = slice(sub * SUB_TILE, (sub + 1) * SUB_TILE)
        y = jnp.concatenate([conv[cb, rows, :] for cb in range(n_chunks)], axis=1)
        mu = jnp.mean(y, axis=-1, keepdims=True)
        yc = y - mu
        var = jnp.mean(yc * yc, axis=-1, keepdims=True)
        z = yc * lax.rsqrt(var + EPS) * lng_ref[...] + lnb_ref[...]
        z = z * jax.nn.sigmoid(z)
        mix = _dot(_bf16(z), w2_ref[...]) + b2_ref[...]
        x1 = xs[rows, :] + _per_batch(mod_ref[2], SUB_TILE) * mix
        x1_ref[rows, :] = x1
        _post_mixer(x1, mod_ref, n2_ref, wrt_ref, brt_ref, hrow_ref, cls_ref, rank_ref, cnt_ref, sub)


def _conf_call(x, mod, n1, n2, w1, b1, dw, dwb, lng, lnb, w2, b2, wrt, brt, zero_rows):
    bsz, seq, d = x.shape
    tm = TIME_TILE
    ts = bsz * tm
    n_steps = seq // tm
    assert zero_rows % (n_steps * SUBLANES) == 0
    out_shape, out_specs = _mixer_out(bsz * seq, d, ts)
    out_shape += (jax.ShapeDtypeStruct((zero_rows, LANES), jnp.uint32),)
    out_specs += (pl.BlockSpec((zero_rows // n_steps, LANES), lambda i: (i, 0)),)
    hist = SUBLANES * (CONF_KERNEL - 1)
    return pl.pallas_call(
        _conf_kernel,
        out_shape=out_shape,
        grid=(n_steps,),
        in_specs=[
            pl.BlockSpec((bsz, tm, d), lambda i: (0, i, 0)),
            _full((MOD_ROWS, bsz, d)),
            _full((1, d)), _full((1, d)),
            _full((d, 2 * d)), _full((1, 2 * d)),
            _full((d // LANES, CONF_KERNEL + 1, LANES)), _full((d // LANES, 1, LANES)),
            _full((1, d)), _full((1, d)),
            _full((d, d)), _full((1, d)),
            _full((ROUTER_ROWS, d)), _full((ROUTER_ROWS, LANES)),
        ],
        out_specs=out_specs,
        scratch_shapes=[
            pltpu.VMEM((d // LANES, bsz * PERM_PITCH, LANES), jnp.float32),
            pltpu.VMEM((ts, d), jnp.float32),
            pltpu.VMEM((d // LANES, hist + ts, LANES), jnp.float32),
            pltpu.VMEM((d // LANES, ts, LANES), jnp.float32),
        ],
        compiler_params=pltpu.CompilerParams(
            dimension_semantics=("arbitrary",), vmem_limit_bytes=VMEM_LIMIT_BYTES),
        name="conf_mixer",
    )(x, mod, n1, n2, w1, b1, dw, dwb, lng, lnb, w2, b2, wrt, brt)


def _lru_kernel(n_tiles, dest_ref, dnext_ref, xprev_ref, pmod_ref, mod_ref, n1_ref, n2_ref,
                wy_ref, by_ref, wx_ref, bx_ref,
                cw_ref, cb_ref, wa_ref, ba_ref, wi_ref, bi_ref, lam_ref, wo_ref, bo_ref,
                wrt_ref, brt_ref, ys_ref,
                x1_ref, hrow_ref, cls_ref, rank_ref, cnt_ref, uext, hcar, ybuf, ysem):
    i = pl.program_id(0)
    ts, d = xprev_ref.shape
    hd = d // LRU_HEADS
    hist = SUBLANES * (LRU_CONV - 1)

    @pl.when(i == 0)
    def _():
        uext[0:hist, :] = jnp.zeros((hist, d), jnp.float32)
        hcar[...] = jnp.zeros_like(hcar)
        cnt_ref[...] = jnp.zeros_like(cnt_ref)

    y_prev, retire_last = _gather_expert_tiles(ys_ref, ybuf, ysem, dest_ref, dnext_ref, i, n_tiles, ts,
                                               inline=True)
    nl = -lam_ref[...]
    softplus = jnp.maximum(nl, 0.0) + jnp.log1p(jnp.exp(-jnp.abs(nl)))
    hstate = hcar[...]
    for sub in range(ts // SUB_TILE):
        rows = slice(sub * SUB_TILE, (sub + 1) * SUB_TILE)
        x = xprev_ref[rows, :] + _per_batch(pmod_ref[5], SUB_TILE) * y_prev[rows, :]
        h = _bf16(_rms_mod(x, n1_ref[...], mod_ref[0], mod_ref[1]))
        ygate = jax.nn.gelu(_dot(h, wy_ref[...]) + by_ref[...])
        xb = _dot(h, wx_ref[...]) + bx_ref[...]
        uext[hist + sub * SUB_TILE:hist + (sub + 1) * SUB_TILE, :] = xb
        u = _causal_taps(cw_ref, cb_ref, uext, LRU_CONV, sub * SUB_TILE, SUB_TILE)

        ub = _bf16(u)
        r_parts, i_parts = [], []
        for hh in range(LRU_HEADS):
            uh = ub[:, hh * hd:(hh + 1) * hd]
            r_parts.append(_dot(uh, wa_ref[hh]))
            i_parts.append(_dot(uh, wi_ref[hh]))
        r = jax.nn.sigmoid(jnp.concatenate(r_parts, axis=1) + ba_ref[...])
        ig = jax.nn.sigmoid(jnp.concatenate(i_parts, axis=1) + bi_ref[...])

        log_a = (-LRU_C) * r * softplus
        a = jnp.exp(log_a)
        mult = jnp.sqrt(-jnp.tanh(log_a) * (1.0 + a * a))
        b = mult * (ig * u)

        outs = []
        for m in range(SUB_TILE // SUBLANES):
            sl = slice(m * SUBLANES, (m + 1) * SUBLANES)
            hstate = a[sl] * hstate + b[sl]
            outs.append(hstate)
        hs = jnp.concatenate(outs, axis=0)

        mix = _dot(_bf16(hs * ygate), wo_ref[...]) + bo_ref[...]
        x1 = x + _per_batch(mod_ref[2], SUB_TILE) * mix
        x1_ref[rows, :] = x1
        _post_mixer(x1, mod_ref, n2_ref, wrt_ref, brt_ref, hrow_ref, cls_ref, rank_ref, cnt_ref, sub)
    hcar[...] = hstate
    uext[0:hist, :] = uext[ts:ts + hist, :]
    retire_last()


def _lru_call(dest, x_prev, mod_prev, ys, mod, n1, n2, wy, by, wx, bx, cw, cb, wa, ba, wi, bi, lam,
              wo, bo, wrt, brt):
    t, d = x_prev.shape
    n_tiles, _, ts = dest.shape
    bsz = mod.shape[1]
    hd = d // LRU_HEADS
    hist = SUBLANES * (LRU_CONV - 1)
    out_shape, out_specs = _mixer_out(t, d, ts)
    return pl.pallas_call(
        functools.partial(_lru_kernel, n_tiles),
        out_shape=out_shape,
        grid=(n_tiles,),
        in_specs=[
            pl.BlockSpec((1, 1, ts), lambda i: (i, 0, 0), memory_space=pltpu.SMEM),
            pl.BlockSpec((1, 1, ts), lambda i: (jnp.minimum(i + 1, n_tiles - 1), 0, 0),
                         memory_space=pltpu.SMEM),
            pl.BlockSpec((ts, d), lambda i: (i, 0)),
            _full((MOD_ROWS, bsz, d)), _full((MOD_ROWS, bsz, d)),
            _full((1, d)), _full((1, d)),
            _full((d, d)), _full((1, d)),
            _full((d, d)), _full((1, d)),
            _full((LRU_CONV, d)), _full((1, d)),
            _full((LRU_HEADS, hd, hd)), _full((1, d)),
            _full((LRU_HEADS, hd, hd)), _full((1, d)),
            _full((1, d)),
            _full((d, d)), _full((1, d)),
            _full((ROUTER_ROWS, d)), _full((ROUTER_ROWS, LANES)),
            pl.BlockSpec(memory_space=pl.ANY),
        ],
        out_specs=out_specs,
        scratch_shapes=[
            pltpu.VMEM((hist + ts, d), jnp.float32),
            pltpu.VMEM((SUBLANES, d), jnp.float32),
            pltpu.VMEM((2, ts * SUBLANES, LANES), jnp.float32),
            pltpu.SemaphoreType.DMA((2,)),
        ],
        compiler_params=pltpu.CompilerParams(
            dimension_semantics=("arbitrary",), vmem_limit_bytes=VMEM_LIMIT_BYTES),
        name="lru_mixer",
    )(dest, dest, x_prev, mod_prev, mod, n1, n2, wy, by, wx, bx, cw, cb, wa, ba, wi, bi, lam, wo, bo,
      wrt, brt, ys)


def _plan_kernel(n_blocks, cnt_ref, clo_ref, chi_ref, cls_ref, rank_ref,
                 dest_ref, blo_ref, bhi_ref, nblk_ref, class_start):
    n_tiles, _, ts = cls_ref.shape
    class_start[...] = jnp.zeros_like(class_start)

    def per_class(c, nb_total):
        class_start[pl.ds(c, 1), :] = jnp.full((1, LANES), nb_total * ROW_BLOCK, jnp.int32)
        nb = (cnt_ref[c] + (ROW_BLOCK - 1)) // ROW_BLOCK

        def fill(k, carry):
            blo_ref[nb_total + k] = clo_ref[c]
            bhi_ref[nb_total + k] = chi_ref[c]
            return carry

        lax.fori_loop(0, nb, fill, 0)
        return nb_total + nb

    nb_used = lax.fori_loop(0, N_CLASSES, per_class, 0)
    nblk_ref[0] = nb_used
    last = jnp.maximum(nb_used - 1, 0)

    def tail(k, carry):
        blo_ref[k] = blo_ref[last]
        bhi_ref[k] = bhi_ref[last]
        return carry

    lax.fori_loop(nb_used, n_blocks, tail, 0)

    starts = class_start[:, 0:1]
    class_id = lax.broadcasted_iota(jnp.int32, (LANES, ts), 0)

    def per_tile(t, carry):
        first = jnp.sum(jnp.where(class_id == cls_ref[t], starts, 0), axis=0, keepdims=True)
        dest_ref[t] = first + rank_ref[t]
        return carry

    lax.fori_loop(0, n_tiles, per_tile, 0)


def _plan_call(cls, rank, counts, n_blocks):
    n_tiles, _, ts = cls.shape
    smem_full = pl.BlockSpec(memory_space=pltpu.SMEM)
    vmem_full = pl.BlockSpec(memory_space=pltpu.VMEM)
    return pl.pallas_call(
        functools.partial(_plan_kernel, n_blocks),
        out_shape=(
            jax.ShapeDtypeStruct((n_tiles, 1, ts), jnp.int32),
            jax.ShapeDtypeStruct((n_blocks,), jnp.int32),
            jax.ShapeDtypeStruct((n_blocks,), jnp.int32),
            jax.ShapeDtypeStruct((1,), jnp.int32),
        ),
        in_specs=[smem_full, smem_full, smem_full, vmem_full, vmem_full],
        out_specs=(vmem_full, smem_full, smem_full, smem_full),
        scratch_shapes=[pltpu.VMEM((LANES, LANES), jnp.int32)],
        compiler_params=pltpu.CompilerParams(vmem_limit_bytes=VMEM_LIMIT_BYTES),
        name="moe_plan",
    )(counts, jnp.asarray(_CLASS_LO), jnp.asarray(_CLASS_HI), cls, rank)


def _dispatch_kernel(n_tiles, n_cast, dest_ref, hrow_ref, xs_in_ref, *rest):
    cast_in, rest = rest[:n_cast], rest[n_cast:]
    xs_ref, rest = rest[0], rest[1:]
    cast_out, (stage, sem_in, sem_out) = rest[:n_cast], rest[n_cast:]
    del xs_in_ref
    i = pl.program_id(0)
    ts = dest_ref.shape[2]
    rows = ts * SUBLANES

    for src, dst in zip(cast_in, cast_out):
        dst[...] = _bf16(src[...])

    def load(tile, slot):
        return pltpu.make_async_copy(hrow_ref.at[pl.ds(pl.multiple_of(tile * rows, rows), rows)],
                                     stage.at[slot], sem_in.at[slot])

    def tile_copy(slot, r, dst_row):
        return pltpu.make_async_copy(
            stage.at[slot, pl.ds(pl.multiple_of(r * SUBLANES, SUBLANES), SUBLANES)],
            xs_ref.at[pl.ds(pl.multiple_of(dst_row * SUBLANES, SUBLANES), SUBLANES)],
            sem_out.at[slot])

    def drain(slot):
        _wait_all_tiles(xs_ref, stage.at[slot], sem_out.at[slot], to_hbm=True)

    slot = i % DISPATCH_SLOTS
    nxt = (i + 1) % DISPATCH_SLOTS

    @pl.when(i == 0)
    def _():
        load(0, 0).start()

    @pl.when(i >= 2)
    def _():
        drain(nxt)

    @pl.when(i + 1 < n_tiles)
    def _():
        load(i + 1, nxt).start()

    load(i, slot).wait()

    _start_tile_dmas(lambda r: tile_copy(slot, r, dest_ref[0, 0, r]), ts, both_queues=True)

    @pl.when(i == n_tiles - 1)
    def _():
        @pl.when(i >= 1)
        def _():
            drain((i + DISPATCH_SLOTS - 1) % DISPATCH_SLOTS)
        drain(slot)


def _dispatch_call(dest, hrows, xs_init, weights, layer):
    n_tiles, _, ts = dest.shape
    assert xs_init.dtype == jnp.uint32 and xs_init.shape[1] == LANES
    any_spec = pl.BlockSpec(memory_space=pl.ANY)
    flat = [w.reshape(w.shape[0], w.shape[1] * w.shape[2], w.shape[3]) for w in weights]
    in_kernel = all(w.shape[1] % (n_tiles * BF16_SUBLANES) == 0 for w in flat)
    cast = flat if in_kernel else []
    outs = pl.pallas_call(
        functools.partial(_dispatch_kernel, n_tiles, len(cast)),
        out_shape=(jax.ShapeDtypeStruct(xs_init.shape, jnp.uint32),)
        + tuple(jax.ShapeDtypeStruct(w.shape[1:], jnp.bfloat16) for w in cast),
        grid=(n_tiles,),
        in_specs=[
            pl.BlockSpec((1, 1, ts), lambda i: (i, 0, 0), memory_space=pltpu.SMEM),
            any_spec, any_spec,
        ] + [pl.BlockSpec((None, w.shape[1] // n_tiles, w.shape[2]), lambda i: (layer, i, 0))
             for w in cast],
        out_specs=(any_spec,)
        + tuple(pl.BlockSpec((w.shape[1] // n_tiles, w.shape[2]), lambda i: (i, 0)) for w in cast),
        scratch_shapes=[
            pltpu.VMEM((DISPATCH_SLOTS, ts * SUBLANES, LANES), jnp.uint32),
            pltpu.SemaphoreType.DMA((DISPATCH_SLOTS,)),
            pltpu.SemaphoreType.DMA((DISPATCH_SLOTS,)),
        ],
        input_output_aliases={2: 0},
        compiler_params=pltpu.CompilerParams(
            dimension_semantics=("arbitrary",), vmem_limit_bytes=VMEM_LIMIT_BYTES),
        name="moe_dispatch",
    )(dest, hrows, xs_init, *cast)
    if in_kernel:
        wb = [o.reshape(w.shape[1:]) for o, w in zip(outs[1:], weights)]
    else:
        wb = [_bf16(w[layer]) for w in weights]
    return outs[0], wb


def _expert_kernel(blo_ref, bhi_ref, nblk_ref, xs_ref, wg_lo, wu_lo, wd_lo, wg_hi, wu_hi, wd_hi, ys_ref):
    del blo_ref, bhi_ref
    b = pl.program_id(0)
    rb = ROW_BLOCK

    @pl.when(b < nblk_ref[0])
    def _():
        pieces = []
        for j in range(PACK_SUBLANES):
            w = _token_rows(xs_ref, j, rb)
            pieces.append(_bf16(_f32_from_bits(w & jnp.uint32(0xFFFF0000))))
            pieces.append(_bf16(_f32_from_bits(lax.shift_left(w, jnp.uint32(16)))))
        xb = jnp.concatenate(pieces, axis=1)
        gates = _f32_from_bits(_token_rows(xs_ref, PACK_SUBLANES, rb))
        acc = None
        for idx, (wg, wu, wd) in enumerate(((wg_lo, wu_lo, wd_lo), (wg_hi, wu_hi, wd_hi))):
            a = _dot(xb, wg[...])
            hid = (a * jax.nn.sigmoid(a)) * _dot(xb, wu[...]) * gates[:, idx:idx + 1]
            part = _dot(_bf16(hid), wd[...])
            acc = part if acc is None else acc + part
        for s in range(SUBLANES):
            ys_ref[pl.ds(s, rb, stride=SUBLANES), :] = acc[:, s * LANES:(s + 1) * LANES]

    @pl.when(b >= nblk_ref[0])
    def _():
        ys_ref[...] = jnp.zeros_like(ys_ref)


def _expert_call(blo, bhi, nblk, xs, wg, wu, wd):
    n_rows = xs.shape[0] // SUBLANES
    d, de = wg.shape[1], wg.shape[2]
    assert d == SUBLANES * LANES
    n_blocks = n_rows // ROW_BLOCK

    def w_spec(shape, which):
        if which == 0:
            return pl.BlockSpec((None,) + shape, lambda b, lo, hi, n: (lo[b], 0, 0))
        return pl.BlockSpec((None,) + shape, lambda b, lo, hi, n: (hi[b], 0, 0))

    def x_map(b, lo, hi, n):
        return (jnp.minimum(b, jnp.maximum(n[0] - 1, 0)), 0)

    grid_spec = pltpu.PrefetchScalarGridSpec(
        num_scalar_prefetch=3,
        grid=(n_blocks,),
        in_specs=[
            pl.BlockSpec((ROW_BLOCK * SUBLANES, LANES), x_map),
            w_spec((d, de), 0), w_spec((d, de), 0), w_spec((de, d), 0),
            w_spec((d, de), 1), w_spec((d, de), 1), w_spec((de, d), 1),
        ],
        out_specs=pl.BlockSpec((ROW_BLOCK * SUBLANES, LANES), lambda b, lo, hi, n: (b, 0)),
    )
    return pl.pallas_call(
        _expert_kernel,
        out_shape=jax.ShapeDtypeStruct((n_rows * SUBLANES, LANES), jnp.float32),
        grid_spec=grid_spec,
        compiler_params=pltpu.CompilerParams(
            dimension_semantics=("arbitrary",), vmem_limit_bytes=VMEM_LIMIT_BYTES),
        name="moe_experts",
    )(blo, bhi, nblk, xs, wg, wu, wd, wg, wu, wd)


def _final_kernel(n_tiles, dest_ref, dnext_ref, x1_ref, mod_ref, fmod_ref, fg_ref, ys_ref,
                  out_ref, ybuf, sem, unperm):
    i = pl.program_id(0)
    ts, d = x1_ref.shape
    bsz, tm, _ = out_ref.shape
    y, retire_last = _gather_expert_tiles(ys_ref, ybuf, sem, dest_ref, dnext_ref, i, n_tiles, ts,
                                          inline=False)
    x2 = x1_ref[...] + _per_batch(mod_ref[5], ts) * y
    res = _rms_mod(x2, fg_ref[...], fmod_ref[0], fmod_ref[1])
    for cb in range(d // LANES):
        unperm[cb] = res[:, cb * LANES:(cb + 1) * LANES]
    for cb in range(d // LANES):
        for p in range(bsz):
            out_ref[p, :, cb * LANES:(cb + 1) * LANES] = unperm.at[cb][pl.ds(p, tm, stride=SUBLANES), :]
    retire_last()


def _final_call(dest, x1, mod, ys, bsz, fmod, fin_g):
    t, d = x1.shape
    n_tiles, _, ts = dest.shape
    tm = ts // bsz
    return pl.pallas_call(
        functools.partial(_final_kernel, n_tiles),
        out_shape=jax.ShapeDtypeStruct((bsz, t // bsz, d), jnp.float32),
        grid=(n_tiles,),
        in_specs=[
            pl.BlockSpec((1, 1, ts), lambda i: (i, 0, 0), memory_space=pltpu.SMEM),
            pl.BlockSpec((1, 1, ts), lambda i: (jnp.minimum(i + 1, n_tiles - 1), 0, 0),
                         memory_space=pltpu.SMEM),
            pl.BlockSpec((ts, d), lambda i: (i, 0)),
            _full((MOD_ROWS, bsz, d)), _full((MOD_ROWS, bsz, d)), _full((1, d)),
            pl.BlockSpec(memory_space=pl.ANY),
        ],
        out_specs=pl.BlockSpec((bsz, tm, d), lambda i: (0, i, 0)),
        scratch_shapes=[pltpu.VMEM((2, ts * SUBLANES, LANES), jnp.float32),
                        pltpu.SemaphoreType.DMA((2,)),
                        pltpu.VMEM((d // LANES, ts, LANES), jnp.float32)],
        compiler_params=pltpu.CompilerParams(
            dimension_semantics=("arbitrary",), vmem_limit_bytes=VMEM_LIMIT_BYTES),
        name="moe_combine_final",
    )(dest, dest, x1, mod, fmod, fin_g, ys)


def _router_operands(wg, bg, wr, br):
    d = wg.shape[0]
    pad_g = jnp.zeros((SUBLANES - N_GROUPS, d), jnp.float32)
    pad_e = jnp.zeros((ROUTER_ROWS - SUBLANES - N_EXPERTS, d), jnp.float32)
    wrt = jnp.concatenate([wg.T, pad_g, wr.T, pad_e], axis=0)
    bcol = jnp.concatenate([bg, jnp.zeros((SUBLANES - N_GROUPS,), jnp.float32), br,
                            jnp.zeros((ROUTER_ROWS - SUBLANES - N_EXPERTS,), jnp.float32)])
    brt = jnp.broadcast_to(bcol[:, None], (ROUTER_ROWS, LANES))
    return _bf16(wrt), brt


def _mod_table(m, parts):
    bsz = m.shape[0]
    d = m.shape[1] // parts
    m = jnp.transpose(m.reshape(bsz, parts, d), (1, 0, 2))
    return jnp.concatenate([m, jnp.zeros((MOD_ROWS - parts, bsz, d), jnp.float32)], axis=0)


def _row(v):
    return v.reshape(1, -1)


def _moe_experts(hrows, cls, rank, cnt, wg, wu, wd, layer, n_blocks, xs_init):
    counts = cnt[:, 0].astype(jnp.int32)
    dest, blo, bhi, nblk = _plan_call(cls, rank, counts, n_blocks)
    xs, (wgb, wub, wdb) = _dispatch_call(dest, hrows, xs_init, (wg, wu, wd), layer)
    ys = _expert_call(blo, bhi, nblk, xs, wgb, wub, wdb)
    return xs, ys, dest


def kernel(x, c, ada_w, ada_b, norm1_g, norm2_g, cm_w1, cm_b1, cm_dw, cm_dwb, cm_ln_g, cm_ln_b, cm_w2, cm_b2, rg_wy, rg_by, rg_wx, rg_bx, rg_cw, rg_cb, rg_wa, rg_ba, rg_wi, rg_bi, rg_lambda, rg_wo, rg_bo, moe_wg, moe_bg, moe_wr, moe_br, moe_w_gate, moe_w_up, moe_w_down, fin_ada_w, fin_ada_b, fin_g):
    bsz, seq, d = x.shape
    t = bsz * seq
    depth = ada_w.shape[0]
    assert depth == 2 and bsz == SUBLANES and seq % TIME_TILE == 0
    assert d == 2 * PACK_SUBLANES * LANES

    mods = _ada_call(c, ada_w, ada_b[:, None, :])
    fmods = _ada_call(c, fin_ada_w[None], fin_ada_b[None, None, :])
    fmod = _mod_table(fmods[0], 2)
    n_steps = seq // TIME_TILE
    granule = n_steps // math.gcd(n_steps, ROW_BLOCK)
    n_blocks = -(-(-(-t // ROW_BLOCK) + N_CLASSES) // granule) * granule
    mod0, mod1 = _mod_table(mods[0], 6), _mod_table(mods[1], 6)

    wrt, brt = _router_operands(moe_wg[0], moe_bg[0], moe_wr[0], moe_br[0])
    dw = jnp.concatenate([cm_dw[0], jnp.zeros((1, d), jnp.float32)], axis=0)
    dw = jnp.transpose(dw.reshape(CONF_KERNEL + 1, d // LANES, LANES), (1, 0, 2))
    x1, hrows, cls, rank, cnt, xs_init = _conf_call(
        x, mod0, _row(norm1_g[0]), _row(norm2_g[0]), _bf16(cm_w1[0]), _row(cm_b1[0]),
        dw, cm_dwb[0].reshape(d // LANES, 1, LANES), _row(cm_ln_g[0]), _row(cm_ln_b[0]),
        _bf16(cm_w2[0]), _row(cm_b2[0]), wrt, brt, n_blocks * ROW_BLOCK * SUBLANES)
    xs, ys, dest = _moe_experts(hrows, cls, rank, cnt, moe_w_gate, moe_w_up, moe_w_down, 0,
                                n_blocks, xs_init)

    wrt, brt = _router_operands(moe_wg[1], moe_bg[1], moe_wr[1], moe_br[1])
    x1, hrows, cls, rank, cnt = _lru_call(
        dest, x1, mod0, ys, mod1, _row(norm1_g[1]), _row(norm2_g[1]),
        _bf16(rg_wy[0]), _row(rg_by[0]), _bf16(rg_wx[0]), _row(rg_bx[0]),
        rg_cw[0], _row(rg_cb[0]), _bf16(rg_wa[0]), _row(rg_ba[0]), _bf16(rg_wi[0]), _row(rg_bi[0]),
        _row(rg_lambda[0]), _bf16(rg_wo[0]), _row(rg_bo[0]), wrt, brt)
    _, ys, dest = _moe_experts(hrows, cls, rank, cnt, moe_w_gate, moe_w_up, moe_w_down, 1,
                               n_blocks, xs)
    return _final_call(dest, x1, mod1, ys, bsz, fmod, _row(fin_g))
```

```python
import functools
import math

import jax
import jax.numpy as jnp
import numpy as np
from jax import lax
from jax.experimental import pallas as pl
from jax.experimental.pallas import tpu as pltpu

EPS = 1e-6
CONF_KERNEL = 31
LRU_HEADS = 4
LRU_CONV = 4
LRU_C = 8.0
N_GROUPS = 4
EXPERTS_PER_GROUP = 8
N_EXPERTS = N_GROUPS * EXPERTS_PER_GROUP
PAIRS_PER_GROUP = EXPERTS_PER_GROUP * (EXPERTS_PER_GROUP - 1) // 2
N_CLASSES = N_GROUPS * PAIRS_PER_GROUP

LANES = 128
SUBLANES = 8
BF16_SUBLANES = 16
VMEM_LIMIT_BYTES = 56 * 1024 * 1024

TOKEN_TILE = 512
TIME_TILE = TOKEN_TILE // SUBLANES
SUB_TILE = 256
ROW_BLOCK = 320
PERM_PITCH = TIME_TILE + SUBLANES
CONV_ROWS = 128
ROUTER_ROWS = 64
PACK_SUBLANES = 4
MOD_ROWS = 8
DISPATCH_SLOTS = 3
DMA_UNROLL = 8

_NEG_INF = float("-inf")


def _class_tables():
    lo = np.zeros((N_CLASSES,), np.int32)
    hi = np.zeros((N_CLASSES,), np.int32)
    for g in range(N_GROUPS):
        for a in range(EXPERTS_PER_GROUP):
            for b in range(a + 1, EXPERTS_PER_GROUP):
                c = g * PAIRS_PER_GROUP + (a * (2 * EXPERTS_PER_GROUP - 1 - a)) // 2 + (b - a - 1)
                lo[c] = g * EXPERTS_PER_GROUP + a
                hi[c] = g * EXPERTS_PER_GROUP + b
    return lo, hi


_CLASS_LO, _CLASS_HI = _class_tables()


def _bf16(x):
    return x.astype(jnp.bfloat16)


def _dot(a, b):
    return jnp.dot(a, b, preferred_element_type=jnp.float32)


def _bits(x):
    return lax.bitcast_convert_type(x, jnp.uint32)


def _f32_from_bits(x):
    return lax.bitcast_convert_type(x, jnp.float32)


def _token_rows(ref, sublane, n):
    return ref[pl.ds(sublane, n, stride=SUBLANES), :]


def _per_batch(v, ts):
    return jnp.tile(v, (ts // SUBLANES, 1))


def _ada_kernel(c_ref, w_ref, b_ref, o_ref):
    c = c_ref[...]
    ca = _bf16(c * jax.nn.sigmoid(c))
    o_ref[...] = _dot(ca, _bf16(w_ref[...])) + b_ref[...]


def _ada_call(c, w, b):
    n_layers, d, n = w.shape
    bsz = c.shape[0]
    nb = 1024
    return pl.pallas_call(
        _ada_kernel,
        out_shape=jax.ShapeDtypeStruct((n_layers, bsz, n), jnp.float32),
        grid=(n_layers, n // nb),
        in_specs=[
            pl.BlockSpec((bsz, d), lambda l, j: (0, 0)),
            pl.BlockSpec((None, d, nb), lambda l, j: (l, 0, j)),
            pl.BlockSpec((None, 1, nb), lambda l, j: (l, 0, j)),
        ],
        out_specs=pl.BlockSpec((None, bsz, nb), lambda l, j: (l, 0, j)),
        compiler_params=pltpu.CompilerParams(
            dimension_semantics=("arbitrary", "arbitrary"),
            vmem_limit_bytes=VMEM_LIMIT_BYTES),
        name="ada_mod",
    )(c, w, b)


def _rms_mod(x, gain, shift, scale):
    ts = x.shape[0]
    ms = jnp.mean(x * x, axis=-1, keepdims=True)
    return (x * lax.rsqrt(ms + EPS)) * _per_batch(gain * (1.0 + scale), ts) + _per_batch(shift, ts)


def _causal_taps(w_ref, bias_ref, ext_ref, n_taps, row0, n):
    d = ext_ref.shape[1]
    cols = []
    for cb in range(d // LANES):
        sl = slice(cb * LANES, (cb + 1) * LANES)
        acc = jnp.broadcast_to(bias_ref[:, sl], (n, LANES))
        for k in range(n_taps):
            lo = row0 + SUBLANES * k
            acc = acc + w_ref[k:k + 1, sl] * ext_ref[lo:lo + n, sl]
        cols.append(acc)
    return jnp.concatenate(cols, axis=1)


def _causal_taps_chunked(w_ref, bias_ref, ext_ref, out_ref, n_taps, ts):
    n_chunks = ext_ref.shape[0]
    units_per_chunk = ts // CONV_ROWS

    def unit(idx, carry):
        cb = idx // units_per_chunk
        r0 = pl.multiple_of((idx % units_per_chunk) * CONV_ROWS, CONV_ROWS)
        w_cb, ext_cb = w_ref.at[cb], ext_ref.at[cb]
        acc = jnp.broadcast_to(bias_ref[cb], (CONV_ROWS, LANES))
        for k in range(n_taps):
            win = ext_cb[pl.ds(pl.multiple_of(r0 + SUBLANES * k, SUBLANES), CONV_ROWS), :]
            acc = acc + w_cb[k:k + 1, :] * win
        out_ref.at[cb][pl.ds(r0, CONV_ROWS), :] = acc
        return carry

    lax.fori_loop(0, n_chunks * units_per_chunk, unit, 0)


def _start_tile_dmas(make_copy, n, inline=False, both_queues=False):
    if inline:
        for r in range(n):
            make_copy(r).start()
        return

    def body(g, carry):
        for k in range(DMA_UNROLL):
            make_copy(g * DMA_UNROLL + k).start(priority=k % 2 if both_queues else 0)
        return carry
    lax.fori_loop(0, n // DMA_UNROLL, body, 0)


def _wait_all_tiles(hbm_ref, vmem_tiles_ref, sem, to_hbm):
    hbm_view = hbm_ref.at[pl.ds(0, vmem_tiles_ref.shape[0])]
    if to_hbm:
        pltpu.make_async_copy(vmem_tiles_ref, hbm_view, sem).wait()
    else:
        pltpu.make_async_copy(hbm_view, vmem_tiles_ref, sem).wait()


def _gather_expert_tiles(ys_ref, ybuf, sem, dest_ref, dnext_ref, step, n_steps, ts, inline):
    slot = step % 2

    def tile_copy(dref, sl):
        def make(r):
            src = pl.multiple_of(dref[0, 0, r] * SUBLANES, SUBLANES)
            return pltpu.make_async_copy(
                ys_ref.at[pl.ds(src, SUBLANES)],
                ybuf.at[sl, pl.ds(pl.multiple_of(r * SUBLANES, SUBLANES), SUBLANES)],
                sem.at[sl])
        return make

    @pl.when(step == 0)
    def _():
        _start_tile_dmas(tile_copy(dest_ref, 0), ts)

    if not inline:
        @pl.when(step + 1 < n_steps)
        def _():
            _start_tile_dmas(tile_copy(dnext_ref, 1 - slot), ts)

    _wait_all_tiles(ys_ref, ybuf.at[slot], sem.at[slot], to_hbm=False)
    if inline:
        _start_tile_dmas(tile_copy(dnext_ref, 1 - slot), ts, inline=True)
    yv = ybuf.at[slot]
    y = jnp.concatenate([_token_rows(yv, s, ts) for s in range(SUBLANES)], axis=1)

    def retire_last():
        if inline:
            @pl.when(step + 1 == n_steps)
            def _():
                _wait_all_tiles(ys_ref, ybuf.at[1 - slot], sem.at[1 - slot], to_hbm=False)

    return y, retire_last


def _route(logits_t):
    ts = logits_t.shape[1]
    row = lax.broadcasted_iota(jnp.int32, (SUBLANES, ts), 0)
    gl = jnp.where(row < N_GROUPS, logits_t[0:SUBLANES], _NEG_INF)
    gmax = jnp.max(gl, axis=0, keepdims=True)
    gsel = jnp.min(jnp.where(gl == gmax, row, SUBLANES), axis=0, keepdims=True)
    p_g = 1.0 / jnp.sum(jnp.exp(gl - gmax), axis=0, keepdims=True)
    es = jnp.zeros((SUBLANES, ts), jnp.float32)
    for g in range(N_GROUPS):
        lo = SUBLANES * (1 + g)
        es = jnp.where(gsel == g, logits_t[lo:lo + EXPERTS_PER_GROUP], es)
    m1 = jnp.max(es, axis=0, keepdims=True)
    i1 = jnp.min(jnp.where(es == m1, row, SUBLANES), axis=0, keepdims=True)
    es2 = jnp.where(row == i1, _NEG_INF, es)
    m2 = jnp.max(es2, axis=0, keepdims=True)
    i2 = jnp.min(jnp.where(es2 == m2, row, SUBLANES), axis=0, keepdims=True)
    t = jnp.exp(m2 - m1)
    w1 = p_g / (1.0 + t)
    w2 = w1 * t
    first_lo = i1 < i2
    e_lo = jnp.minimum(i1, i2)
    e_hi = jnp.maximum(i1, i2)
    gate_lo = jnp.where(first_lo, w1, w2)
    gate_hi = jnp.where(first_lo, w2, w1)
    pair = lax.shift_right_logical(e_lo * (2 * EXPERTS_PER_GROUP - 1 - e_lo), 1) + (e_hi - e_lo - 1)
    cls = gsel * PAIRS_PER_GROUP + pair
    return cls, gate_lo, gate_hi


def _post_mixer(x1, mod_ref, n2_ref, wrt_ref, brt_ref, hrow_ref, cls_ref, rank_ref, cnt_ref, sub):
    ts = x1.shape[0]
    hrow_ref = hrow_ref.at[pl.ds(sub * ts * SUBLANES, ts * SUBLANES)]
    cols = slice(sub * ts, (sub + 1) * ts)
    h2 = _rms_mod(x1, n2_ref[...], mod_ref[3], mod_ref[4])
    h2b = _bf16(h2)
    logits_t = lax.dot_general(wrt_ref[...], h2b, (((1,), (1,)), ((), ())),
                               preferred_element_type=jnp.float32) + brt_ref[:, 0:1]
    cls, gate_lo, gate_hi = _route(logits_t)
    gates_t = jnp.concatenate(
        [gate_lo, gate_hi, jnp.zeros((LANES - 2, ts), jnp.float32)], axis=0)
    h2r = h2b.astype(jnp.float32)
    for j in range(PACK_SUBLANES):
        hi = _bits(h2r[:, (2 * j) * LANES:(2 * j + 1) * LANES])
        lo = _bits(h2r[:, (2 * j + 1) * LANES:(2 * j + 2) * LANES])
        hrow_ref[pl.ds(j, ts, stride=SUBLANES), :] = hi | lax.shift_right_logical(lo, jnp.uint32(16))
    hrow_ref[pl.ds(PACK_SUBLANES, ts, stride=SUBLANES), :] = _bits(gates_t.T)
    for j in range(PACK_SUBLANES + 1, SUBLANES):
        hrow_ref[pl.ds(j, ts, stride=SUBLANES), :] = jnp.zeros((ts, LANES), jnp.uint32)
    cls_ref[0, :, cols] = cls
    crow = lax.broadcasted_iota(jnp.int32, (LANES, ts), 0)
    onehot = jnp.where(crow == cls, 1.0, 0.0)
    cnt = jnp.sum(onehot, axis=1, keepdims=True)
    earlier = (lax.broadcasted_iota(jnp.int32, (ts, ts), 0)
               < lax.broadcasted_iota(jnp.int32, (ts, ts), 1))
    prefix = _dot(_bf16(onehot), _bf16(jnp.where(earlier, 1.0, 0.0)))
    rank = jnp.sum(onehot * (prefix + cnt_ref[:, 0:1]), axis=0, keepdims=True)
    rank_ref[0, :, cols] = rank.astype(jnp.int32)
    cnt_ref[...] += jnp.broadcast_to(cnt, cnt_ref.shape)


def _mixer_out(t, d, ts):
    n_tiles = t // ts
    out_shape = (
        jax.ShapeDtypeStruct((t, d), jnp.float32),
        jax.ShapeDtypeStruct((t * SUBLANES, LANES), jnp.uint32),
        jax.ShapeDtypeStruct((n_tiles, 1, ts), jnp.int32),
        jax.ShapeDtypeStruct((n_tiles, 1, ts), jnp.int32),
        jax.ShapeDtypeStruct((LANES, LANES), jnp.float32),
    )
    out_specs = (
        pl.BlockSpec((ts, d), lambda i: (i, 0)),
        pl.BlockSpec((ts * SUBLANES, LANES), lambda i: (i, 0)),
        pl.BlockSpec((1, 1, ts), lambda i: (i, 0, 0)),
        pl.BlockSpec((1, 1, ts), lambda i: (i, 0, 0)),
        pl.BlockSpec((LANES, LANES), lambda i: (0, 0)),
    )
    return out_shape, out_specs


def _full(shape):
    return pl.BlockSpec(shape, lambda i: tuple(0 for _ in shape))


def _expert_weight_cast(weights, layer, n_steps):
    flat = [w.reshape(w.shape[0], w.shape[1] * w.shape[2], w.shape[3]) for w in weights]
    if any(w.shape[1] % (n_steps * BF16_SUBLANES) for w in flat):
        return [], [], (), ()
    in_specs = [pl.BlockSpec((None, w.shape[1] // n_steps, w.shape[2]), lambda i: (layer, i, 0))
                for w in flat]
    out_shapes = tuple(jax.ShapeDtypeStruct(w.shape[1:], jnp.bfloat16) for w in flat)
    out_specs = tuple(pl.BlockSpec((w.shape[1] // n_steps, w.shape[2]), lambda i: (i, 0))
                      for w in flat)
    return flat, in_specs, out_shapes, out_specs


def _cast_slices(cast_in, cast_out):
    for src, dst in zip(cast_in, cast_out):
        dst[...] = _bf16(src[...])


def _cast_results(outs, weights, layer):
    if outs:
        return [o.reshape(w.shape[1:]) for o, w in zip(outs, weights)]
    return [_bf16(w[layer]) for w in weights]


def _to_tile_order(x_ref, perm, xs):
    bsz, tm, d = x_ref.shape
    for cb in range(d // LANES):
        for p in range(bsz):
            perm[cb, PERM_PITCH * p:PERM_PITCH * p + tm, :] = x_ref[p, :, cb * LANES:(cb + 1) * LANES]
    for m in range(tm):
        for cb in range(d // LANES):
            xs[m * bsz:(m + 1) * bsz, cb * LANES:(cb + 1) * LANES] = (
                perm.at[cb][pl.ds(m, bsz, stride=PERM_PITCH), :])


def _conf_kernel(n_cast, x_ref, mod_ref, n1_ref, n2_ref, w1_ref, b1_ref, dw_ref, dwb_ref,
                 lng_ref, lnb_ref, w2_ref, b2_ref, wrt_ref, brt_ref, *rest):
    cast_in, rest = rest[:n_cast], rest[n_cast:]
    x1_ref, hrow_ref, cls_ref, rank_ref, cnt_ref, zeros_ref = rest[:6]
    cast_out, (perm, xs, vext, conv) = rest[6:6 + n_cast], rest[6 + n_cast:]
    _cast_slices(cast_in, cast_out)
    i = pl.program_id(0)
    bsz, tm, d = x_ref.shape
    ts = bsz * tm
    n_chunks = d // LANES
    hist = SUBLANES * (CONF_KERNEL - 1)

    @pl.when(i == 0)
    def _():
        vext[:, 0:hist, :] = jnp.zeros((n_chunks, hist, LANES), jnp.float32)
        cnt_ref[...] = jnp.zeros_like(cnt_ref)

    zeros_ref[...] = jnp.zeros_like(zeros_ref)

    _to_tile_order(x_ref, perm, xs)
    for sub in range(ts // SUB_TILE):
        rows = slice(sub * SUB_TILE, (sub + 1) * SUB_TILE)
        h = _bf16(_rms_mod(xs[rows, :], n1_ref[...], mod_ref[0], mod_ref[1]))
        u = _dot(h, w1_ref[...]) + b1_ref[...]
        v = u[:, 0:d] * jax.nn.sigmoid(u[:, d:2 * d])
        for cb in range(n_chunks):
            vext[cb, hist + sub * SUB_TILE:hist + (sub + 1) * SUB_TILE, :] = v[:, cb * LANES:(cb + 1) * LANES]

    _causal_taps_chunked(dw_ref, dwb_ref, vext, conv, CONF_KERNEL, ts)
    vext[:, 0:hist, :] = vext[:, ts:ts + hist, :]

    for sub in range(ts // SUB_TILE):
        rows = slice(sub * SUB_TILE, (sub + 1) * SUB_TILE)
        y = jnp.concatenate([conv[cb, rows, :] for cb in range(n_chunks)], axis=1)
        mu = jnp.mean(y, axis=-1, keepdims=True)
        yc = y - mu
        var = jnp.mean(yc * yc, axis=-1, keepdims=True)
        z = yc * lax.rsqrt(var + EPS) * lng_ref[...] + lnb_ref[...]
        z = z * jax.nn.sigmoid(z)
        mix = _dot(_bf16(z), w2_ref[...]) + b2_ref[...]
        x1 = xs[rows, :] + _per_batch(mod_ref[2], SUB_TILE) * mix
        x1_ref[rows, :] = x1
        _post_mixer(x1, mod_ref, n2_ref, wrt_ref, brt_ref, hrow_ref, cls_ref, rank_ref, cnt_ref, sub)


def _conf_call(x, mod, n1, n2, w1, b1, dw, dwb, lng, lnb, w2, b2, wrt, brt, zero_rows,
               expert_weights, layer):
    bsz, seq, d = x.shape
    tm = TIME_TILE
    ts = bsz * tm
    n_steps = seq // tm
    assert zero_rows % (n_steps * SUBLANES) == 0
    cast, cast_specs, cast_shapes, cast_out_specs = _expert_weight_cast(expert_weights, layer, n_steps)
    out_shape, out_specs = _mixer_out(bsz * seq, d, ts)
    out_shape += (jax.ShapeDtypeStruct((zero_rows, LANES), jnp.uint32),) + cast_shapes
    out_specs += (pl.BlockSpec((zero_rows // n_steps, LANES), lambda i: (i, 0)),) + cast_out_specs
    hist = SUBLANES * (CONF_KERNEL - 1)
    outs = pl.pallas_call(
        functools.partial(_conf_kernel, len(cast)),
        out_shape=out_shape,
        grid=(n_steps,),
        in_specs=[
            pl.BlockSpec((bsz, tm, d), lambda i: (0, i, 0)),
            _full((MOD_ROWS, bsz, d)),
            _full((1, d)), _full((1, d)),
            _full((d, 2 * d)), _full((1, 2 * d)),
            _full((d // LANES, CONF_KERNEL + 1, LANES)), _full((d // LANES, 1, LANES)),
            _full((1, d)), _full((1, d)),
            _full((d, d)), _full((1, d)),
            _full((ROUTER_ROWS, d)), _full((ROUTER_ROWS, LANES)),
        ] + cast_specs,
        out_specs=out_specs,
        scratch_shapes=[
            pltpu.VMEM((d // LANES, bsz * PERM_PITCH, LANES), jnp.float32),
            pltpu.VMEM((ts, d), jnp.float32),
            pltpu.VMEM((d // LANES, hist + ts, LANES), jnp.float32),
            pltpu.VMEM((d // LANES, ts, LANES), jnp.float32),
        ],
        compiler_params=pltpu.CompilerParams(
            dimension_semantics=("arbitrary",), vmem_limit_bytes=VMEM_LIMIT_BYTES),
        name="conf_mixer",
    )(x, mod, n1, n2, w1, b1, dw, dwb, lng, lnb, w2, b2, wrt, brt, *cast)
    return outs[:6], _cast_results(outs[6:], expert_weights, layer)


def _lru_kernel(n_tiles, n_cast, dest_ref, dnext_ref, xprev_ref, pmod_ref, mod_ref, n1_ref, n2_ref,
                wy_ref, by_ref, wx_ref, bx_ref,
                cw_ref, cb_ref, wa_ref, ba_ref, wi_ref, bi_ref, lam_ref, wo_ref, bo_ref,
                wrt_ref, brt_ref, ys_ref, *rest):
    cast_in, rest = rest[:n_cast], rest[n_cast:]
    x1_ref, hrow_ref, cls_ref, rank_ref, cnt_ref = rest[:5]
    cast_out, (uext, hcar, ybuf, ysem) = rest[5:5 + n_cast], rest[5 + n_cast:]
    _cast_slices(cast_in, cast_out)
    i = pl.program_id(0)
    ts, d = xprev_ref.shape
    hd = d // LRU_HEADS
    hist = SUBLANES * (LRU_CONV - 1)

    @pl.when(i == 0)
    def _():
        uext[0:hist, :] = jnp.zeros((hist, d), jnp.float32)
        hcar[...] = jnp.zeros_like(hcar)
        cnt_ref[...] = jnp.zeros_like(cnt_ref)

    y_prev, retire_last = _gather_expert_tiles(ys_ref, ybuf, ysem, dest_ref, dnext_ref, i, n_tiles, ts,
                                               inline=True)
    nl = -lam_ref[...]
    softplus = jnp.maximum(nl, 0.0) + jnp.log1p(jnp.exp(-jnp.abs(nl)))
    hstate = hcar[...]
    for sub in range(ts // SUB_TILE):
        rows = slice(sub * SUB_TILE, (sub + 1) * SUB_TILE)
        x = xprev_ref[rows, :] + _per_batch(pmod_ref[5], SUB_TILE) * y_prev[rows, :]
        h = _bf16(_rms_mod(x, n1_ref[...], mod_ref[0], mod_ref[1]))
        ygate = jax.nn.gelu(_dot(h, wy_ref[...]) + by_ref[...])
        xb = _dot(h, wx_ref[...]) + bx_ref[...]
        uext[hist + sub * SUB_TILE:hist + (sub + 1) * SUB_TILE, :] = xb
        u = _causal_taps(cw_ref, cb_ref, uext, LRU_CONV, sub * SUB_TILE, SUB_TILE)

        ub = _bf16(u)
        r_parts, i_parts = [], []
        for hh in range(LRU_HEADS):
            uh = ub[:, hh * hd:(hh + 1) * hd]
            r_parts.append(_dot(uh, wa_ref[hh]))
            i_parts.append(_dot(uh, wi_ref[hh]))
        r = jax.nn.sigmoid(jnp.concatenate(r_parts, axis=1) + ba_ref[...])
        ig = jax.nn.sigmoid(jnp.concatenate(i_parts, axis=1) + bi_ref[...])

        log_a = (-LRU_C) * r * softplus
        a = jnp.exp(log_a)
        mult = jnp.sqrt(-jnp.tanh(log_a) * (1.0 + a * a))
        b = mult * (ig * u)

        outs = []
        for m in range(SUB_TILE // SUBLANES):
            sl = slice(m * SUBLANES, (m + 1) * SUBLANES)
            hstate = a[sl] * hstate + b[sl]
            outs.append(hstate)
        hs = jnp.concatenate(outs, axis=0)

        mix = _dot(_bf16(hs * ygate), wo_ref[...]) + bo_ref[...]
        x1 = x + _per_batch(mod_ref[2], SUB_TILE) * mix
        x1_ref[rows, :] = x1
        _post_mixer(x1, mod_ref, n2_ref, wrt_ref, brt_ref, hrow_ref, cls_ref, rank_ref, cnt_ref, sub)
    hcar[...] = hstate
    uext[0:hist, :] = uext[ts:ts + hist, :]
    retire_last()


def _lru_call(dest, x_prev, mod_prev, ys, mod, n1, n2, wy, by, wx, bx, cw, cb, wa, ba, wi, bi, lam,
              wo, bo, wrt, brt, expert_weights, layer):
    t, d = x_prev.shape
    n_tiles, _, ts = dest.shape
    bsz = mod.shape[1]
    hd = d // LRU_HEADS
    hist = SUBLANES * (LRU_CONV - 1)
    cast, cast_specs, cast_shapes, cast_out_specs = _expert_weight_cast(expert_weights, layer, n_tiles)
    out_shape, out_specs = _mixer_out(t, d, ts)
    out_shape += cast_shapes
    out_specs += cast_out_specs
    outs = pl.pallas_call(
        functools.partial(_lru_kernel, n_tiles, len(cast)),
        out_shape=out_shape,
        grid=(n_tiles,),
        in_specs=[
            pl.BlockSpec((1, 1, ts), lambda i: (i, 0, 0), memory_space=pltpu.SMEM),
            pl.BlockSpec((1, 1, ts), lambda i: (jnp.minimum(i + 1, n_tiles - 1), 0, 0),
                         memory_space=pltpu.SMEM),
            pl.BlockSpec((ts, d), lambda i: (i, 0)),
            _full((MOD_ROWS, bsz, d)), _full((MOD_ROWS, bsz, d)),
            _full((1, d)), _full((1, d)),
            _full((d, d)), _full((1, d)),
            _full((d, d)), _full((1, d)),
            _full((LRU_CONV, d)), _full((1, d)),
            _full((LRU_HEADS, hd, hd)), _full((1, d)),
            _full((LRU_HEADS, hd, hd)), _full((1, d)),
            _full((1, d)),
            _full((d, d)), _full((1, d)),
            _full((ROUTER_ROWS, d)), _full((ROUTER_ROWS, LANES)),
            pl.BlockSpec(memory_space=pl.ANY),
        ] + cast_specs,
        out_specs=out_specs,
        scratch_shapes=[
            pltpu.VMEM((hist + ts, d), jnp.float32),
            pltpu.VMEM((SUBLANES, d), jnp.float32),
            pltpu.VMEM((2, ts * SUBLANES, LANES), jnp.float32),
            pltpu.SemaphoreType.DMA((2,)),
        ],
        compiler_params=pltpu.CompilerParams(
            dimension_semantics=("arbitrary",), vmem_limit_bytes=VMEM_LIMIT_BYTES),
        name="lru_mixer",
    )(dest, dest, x_prev, mod_prev, mod, n1, n2, wy, by, wx, bx, cw, cb, wa, ba, wi, bi, lam, wo, bo,
      wrt, brt, ys, *cast)
    return outs[:5], _cast_results(outs[5:], expert_weights, layer)


def _plan_kernel(n_blocks, cnt_ref, clo_ref, chi_ref, cls_ref, rank_ref,
                 dest_ref, blo_ref, bhi_ref, nblk_ref, class_start):
    n_tiles, _, ts = cls_ref.shape
    class_start[...] = jnp.zeros_like(class_start)

    def per_class(c, nb_total):
        class_start[pl.ds(c, 1), :] = jnp.full((1, LANES), nb_total * ROW_BLOCK, jnp.int32)
        nb = (cnt_ref[c] + (ROW_BLOCK - 1)) // ROW_BLOCK

        def fill(k, carry):
            blo_ref[nb_total + k] = clo_ref[c]
            bhi_ref[nb_total + k] = chi_ref[c]
            return carry

        lax.fori_loop(0, nb, fill, 0)
        return nb_total + nb

    nb_used = lax.fori_loop(0, N_CLASSES, per_class, 0)
    nblk_ref[0] = nb_used
    last = jnp.maximum(nb_used - 1, 0)

    def tail(k, carry):
        blo_ref[k] = blo_ref[last]
        bhi_ref[k] = bhi_ref[last]
        return carry

    lax.fori_loop(nb_used, n_blocks, tail, 0)

    starts = class_start[:, 0:1]
    class_id = lax.broadcasted_iota(jnp.int32, (LANES, ts), 0)

    def per_tile(t, carry):
        first = jnp.sum(jnp.where(class_id == cls_ref[t], starts, 0), axis=0, keepdims=True)
        dest_ref[t] = first + rank_ref[t]
        return carry

    lax.fori_loop(0, n_tiles, per_tile, 0)


def _plan_call(cls, rank, counts, n_blocks):
    n_tiles, _, ts = cls.shape
    smem_full = pl.BlockSpec(memory_space=pltpu.SMEM)
    vmem_full = pl.BlockSpec(memory_space=pltpu.VMEM)
    return pl.pallas_call(
        functools.partial(_plan_kernel, n_blocks),
        out_shape=(
            jax.ShapeDtypeStruct((n_tiles, 1, ts), jnp.int32),
            jax.ShapeDtypeStruct((n_blocks,), jnp.int32),
            jax.ShapeDtypeStruct((n_blocks,), jnp.int32),
            jax.ShapeDtypeStruct((1,), jnp.int32),
        ),
        in_specs=[smem_full, smem_full, smem_full, vmem_full, vmem_full],
        out_specs=(vmem_full, smem_full, smem_full, smem_full),
        scratch_shapes=[pltpu.VMEM((LANES, LANES), jnp.int32)],
        compiler_params=pltpu.CompilerParams(vmem_limit_bytes=VMEM_LIMIT_BYTES),
        name="moe_plan",
    )(counts, jnp.asarray(_CLASS_LO), jnp.asarray(_CLASS_HI), cls, rank)


def _dispatch_kernel(n_tiles, dest_ref, hrow_ref, xs_in_ref, xs_ref, stage, sem_in, sem_out):
    del xs_in_ref
    i = pl.program_id(0)
    ts = dest_ref.shape[2]
    rows = ts * SUBLANES

    def load(tile, slot):
        return pltpu.make_async_copy(hrow_ref.at[pl.ds(pl.multiple_of(tile * rows, rows), rows)],
                                     stage.at[slot], sem_in.at[slot])

    def tile_copy(slot, r, dst_row):
        return pltpu.make_async_copy(
            stage.at[slot, pl.ds(pl.multiple_of(r * SUBLANES, SUBLANES), SUBLANES)],
            xs_ref.at[pl.ds(pl.multiple_of(dst_row * SUBLANES, SUBLANES), SUBLANES)],
            sem_out.at[slot])

    def drain(slot):
        _wait_all_tiles(xs_ref, stage.at[slot], sem_out.at[slot], to_hbm=True)

    slot = i % DISPATCH_SLOTS
    nxt = (i + 1) % DISPATCH_SLOTS

    @pl.when(i == 0)
    def _():
        load(0, 0).start()

    @pl.when(i >= 2)
    def _():
        drain(nxt)

    @pl.when(i + 1 < n_tiles)
    def _():
        load(i + 1, nxt).start()

    load(i, slot).wait()

    _start_tile_dmas(lambda r: tile_copy(slot, r, dest_ref[0, 0, r]), ts, both_queues=True)

    @pl.when(i == n_tiles - 1)
    def _():
        @pl.when(i >= 1)
        def _():
            drain((i + DISPATCH_SLOTS - 1) % DISPATCH_SLOTS)
        drain(slot)


def _dispatch_call(dest, hrows, xs_init):
    n_tiles, _, ts = dest.shape
    assert xs_init.dtype == jnp.uint32 and xs_init.shape[1] == LANES
    any_spec = pl.BlockSpec(memory_space=pl.ANY)
    return pl.pallas_call(
        functools.partial(_dispatch_kernel, n_tiles),
        out_shape=jax.ShapeDtypeStruct(xs_init.shape, jnp.uint32),
        grid=(n_tiles,),
        in_specs=[
            pl.BlockSpec((1, 1, ts), lambda i: (i, 0, 0), memory_space=pltpu.SMEM),
            any_spec, any_spec,
        ],
        out_specs=any_spec,
        scratch_shapes=[
            pltpu.VMEM((DISPATCH_SLOTS, ts * SUBLANES, LANES), jnp.uint32),
            pltpu.SemaphoreType.DMA((DISPATCH_SLOTS,)),
            pltpu.SemaphoreType.DMA((DISPATCH_SLOTS,)),
        ],
        input_output_aliases={2: 0},
        compiler_params=pltpu.CompilerParams(
            dimension_semantics=("arbitrary",), vmem_limit_bytes=VMEM_LIMIT_BYTES),
        name="moe_dispatch",
    )(dest, hrows, xs_init)


def _expert_kernel(blo_ref, bhi_ref, nblk_ref, xs_ref, wg_lo, wu_lo, wd_lo, wg_hi, wu_hi, wd_hi, ys_ref):
    del blo_ref, bhi_ref
    b = pl.program_id(0)
    rb = ROW_BLOCK

    @pl.when(b < nblk_ref[0])
    def _():
        pieces = []
        for j in range(PACK_SUBLANES):
            w = _token_rows(xs_ref, j, rb)
            pieces.append(_bf16(_f32_from_bits(w & jnp.uint32(0xFFFF0000))))
            pieces.append(_bf16(_f32_from_bits(lax.shift_left(w, jnp.uint32(16)))))
        xb = jnp.concatenate(pieces, axis=1)
        gates = _f32_from_bits(_token_rows(xs_ref, PACK_SUBLANES, rb))
        acc = None
        for idx, (wg, wu, wd) in enumerate(((wg_lo, wu_lo, wd_lo), (wg_hi, wu_hi, wd_hi))):
            a = _dot(xb, wg[...])
            hid = (a * jax.nn.sigmoid(a)) * _dot(xb, wu[...]) * gates[:, idx:idx + 1]
            part = _dot(_bf16(hid), wd[...])
            acc = part if acc is None else acc + part
        for s in range(SUBLANES):
            ys_ref[pl.ds(s, rb, stride=SUBLANES), :] = acc[:, s * LANES:(s + 1) * LANES]

    @pl.when(b >= nblk_ref[0])
    def _():
        ys_ref[...] = jnp.zeros_like(ys_ref)


def _expert_call(blo, bhi, nblk, xs, wg, wu, wd):
    n_rows = xs.shape[0] // SUBLANES
    d, de = wg.shape[1], wg.shape[2]
    assert d == SUBLANES * LANES
    n_blocks = n_rows // ROW_BLOCK

    def w_spec(shape, which):
        if which == 0:
            return pl.BlockSpec((None,) + shape, lambda b, lo, hi, n: (lo[b], 0, 0))
        return pl.BlockSpec((None,) + shape, lambda b, lo, hi, n: (hi[b], 0, 0))

    def x_map(b, lo, hi, n):
        return (jnp.minimum(b, jnp.maximum(n[0] - 1, 0)), 0)

    grid_spec = pltpu.PrefetchScalarGridSpec(
        num_scalar_prefetch=3,
        grid=(n_blocks,),
        in_specs=[
            pl.BlockSpec((ROW_BLOCK * SUBLANES, LANES), x_map),
            w_spec((d, de), 0), w_spec((d, de), 0), w_spec((de, d), 0),
            w_spec((d, de), 1), w_spec((d, de), 1), w_spec((de, d), 1),
        ],
        out_specs=pl.BlockSpec((ROW_BLOCK * SUBLANES, LANES), lambda b, lo, hi, n: (b, 0)),
    )
    return pl.pallas_call(
        _expert_kernel,
        out_shape=jax.ShapeDtypeStruct((n_rows * SUBLANES, LANES), jnp.float32),
        grid_spec=grid_spec,
        compiler_params=pltpu.CompilerParams(
            dimension_semantics=("arbitrary",), vmem_limit_bytes=VMEM_LIMIT_BYTES),
        name="moe_experts",
    )(blo, bhi, nblk, xs, wg, wu, wd, wg, wu, wd)


def _final_kernel(n_tiles, dest_ref, dnext_ref, x1_ref, mod_ref, fmod_ref, fg_ref, ys_ref,
                  out_ref, ybuf, sem, unperm):
    i = pl.program_id(0)
    ts, d = x1_ref.shape
    bsz, tm, _ = out_ref.shape
    y, retire_last = _gather_expert_tiles(ys_ref, ybuf, sem, dest_ref, dnext_ref, i, n_tiles, ts,
                                          inline=False)
    x2 = x1_ref[...] + _per_batch(mod_ref[5], ts) * y
    res = _rms_mod(x2, fg_ref[...], fmod_ref[0], fmod_ref[1])
    for cb in range(d // LANES):
        unperm[cb] = res[:, cb * LANES:(cb + 1) * LANES]
    for cb in range(d // LANES):
        for p in range(bsz):
            out_ref[p, :, cb * LANES:(cb + 1) * LANES] = unperm.at[cb][pl.ds(p, tm, stride=SUBLANES), :]
    retire_last()


def _final_call(dest, x1, mod, ys, bsz, fmod, fin_g):
    t, d = x1.shape
    n_tiles, _, ts = dest.shape
    tm = ts // bsz
    return pl.pallas_call(
        functools.partial(_final_kernel, n_tiles),
        out_shape=jax.ShapeDtypeStruct((bsz, t // bsz, d), jnp.float32),
        grid=(n_tiles,),
        in_specs=[
            pl.BlockSpec((1, 1, ts), lambda i: (i, 0, 0), memory_space=pltpu.SMEM),
            pl.BlockSpec((1, 1, ts), lambda i: (jnp.minimum(i + 1, n_tiles - 1), 0, 0),
                         memory_space=pltpu.SMEM),
            pl.BlockSpec((ts, d), lambda i: (i, 0)),
            _full((MOD_ROWS, bsz, d)), _full((MOD_ROWS, bsz, d)), _full((1, d)),
            pl.BlockSpec(memory_space=pl.ANY),
        ],
        out_specs=pl.BlockSpec((bsz, tm, d), lambda i: (0, i, 0)),
        scratch_shapes=[pltpu.VMEM((2, ts * SUBLANES, LANES), jnp.float32),
                        pltpu.SemaphoreType.DMA((2,)),
                        pltpu.VMEM((d // LANES, ts, LANES), jnp.float32)],
        compiler_params=pltpu.CompilerParams(
            dimension_semantics=("arbitrary",), vmem_limit_bytes=VMEM_LIMIT_BYTES),
        name="moe_combine_final",
    )(dest, dest, x1, mod, fmod, fin_g, ys)


def _router_operands(wg, bg, wr, br):
    d = wg.shape[0]
    pad_g = jnp.zeros((SUBLANES - N_GROUPS, d), jnp.float32)
    pad_e = jnp.zeros((ROUTER_ROWS - SUBLANES - N_EXPERTS, d), jnp.float32)
    wrt = jnp.concatenate([wg.T, pad_g, wr.T, pad_e], axis=0)
    bcol = jnp.concatenate([bg, jnp.zeros((SUBLANES - N_GROUPS,), jnp.float32), br,
                            jnp.zeros((ROUTER_ROWS - SUBLANES - N_EXPERTS,), jnp.float32)])
    brt = jnp.broadcast_to(bcol[:, None], (ROUTER_ROWS, LANES))
    return _bf16(wrt), brt


def _mod_table(m, parts):
    bsz = m.shape[0]
    d = m.shape[1] // parts
    m = jnp.transpose(m.reshape(bsz, parts, d), (1, 0, 2))
    return jnp.concatenate([m, jnp.zeros((MOD_ROWS - parts, bsz, d), jnp.float32)], axis=0)


def _row(v):
    return v.reshape(1, -1)


def _moe_experts(hrows, cls, rank, cnt, weights_bf16, n_blocks, xs_init):
    counts = cnt[:, 0].astype(jnp.int32)
    dest, blo, bhi, nblk = _plan_call(cls, rank, counts, n_blocks)
    xs = _dispatch_call(dest, hrows, xs_init)
    ys = _expert_call(blo, bhi, nblk, xs, *weights_bf16)
    return xs, ys, dest


def kernel(x, c, ada_w, ada_b, norm1_g, norm2_g, cm_w1, cm_b1, cm_dw, cm_dwb, cm_ln_g, cm_ln_b, cm_w2, cm_b2, rg_wy, rg_by, rg_wx, rg_bx, rg_cw, rg_cb, rg_wa, rg_ba, rg_wi, rg_bi, rg_lambda, rg_wo, rg_bo, moe_wg, moe_bg, moe_wr, moe_br, moe_w_gate, moe_w_up, moe_w_down, fin_ada_w, fin_ada_b, fin_g):
    bsz, seq, d = x.shape
    t = bsz * seq
    depth = ada_w.shape[0]
    assert depth == 2 and bsz == SUBLANES and seq % TIME_TILE == 0
    assert d == 2 * PACK_SUBLANES * LANES

    mods = _ada_call(c, ada_w, ada_b[:, None, :])
    fmods = _ada_call(c, fin_ada_w[None], fin_ada_b[None, None, :])
    fmod = _mod_table(fmods[0], 2)
    n_steps = seq // TIME_TILE
    granule = n_steps // math.gcd(n_steps, ROW_BLOCK)
    n_blocks = -(-(-(-t // ROW_BLOCK) + N_CLASSES) // granule) * granule
    mod0, mod1 = _mod_table(mods[0], 6), _mod_table(mods[1], 6)

    wrt, brt = _router_operands(moe_wg[0], moe_bg[0], moe_wr[0], moe_br[0])
    dw = jnp.concatenate([cm_dw[0], jnp.zeros((1, d), jnp.float32)], axis=0)
    dw = jnp.transpose(dw.reshape(CONF_KERNEL + 1, d // LANES, LANES), (1, 0, 2))
    expert_weights = (moe_w_gate, moe_w_up, moe_w_down)
    (x1, hrows, cls, rank, cnt, xs_init), weights0 = _conf_call(
        x, mod0, _row(norm1_g[0]), _row(norm2_g[0]), _bf16(cm_w1[0]), _row(cm_b1[0]),
        dw, cm_dwb[0].reshape(d // LANES, 1, LANES), _row(cm_ln_g[0]), _row(cm_ln_b[0]),
        _bf16(cm_w2[0]), _row(cm_b2[0]), wrt, brt, n_blocks * ROW_BLOCK * SUBLANES,
        expert_weights, 0)
    xs, ys, dest = _moe_experts(hrows, cls, rank, cnt, weights0, n_blocks, xs_init)

    wrt, brt = _router_operands(moe_wg[1], moe_bg[1], moe_wr[1], moe_br[1])
    (x1, hrows, cls, rank, cnt), weights1 = _lru_call(
        dest, x1, mod0, ys, mod1, _row(norm1_g[1]), _row(norm2_g[1]),
        _bf16(rg_wy[0]), _row(rg_by[0]), _bf16(rg_wx[0]), _row(rg_bx[0]),
        rg_cw[0], _row(rg_cb[0]), _bf16(rg_wa[0]), _row(rg_ba[0]), _bf16(rg_wi[0]), _row(rg_bi[0]),
        _row(rg_lambda[0]), _bf16(rg_wo[0]), _row(rg_bo[0]), wrt, brt, expert_weights, 1)
    _, ys, dest = _moe_experts(hrows, cls, rank, cnt, weights1, n_blocks, xs)
    return _final_call(dest, x1, mod1, ys, bsz, fmod, _row(fin_g))
```

```python
import functools
import math

import jax
import jax.numpy as jnp
import numpy as np
from jax import lax
from jax.experimental import pallas as pl
from jax.experimental.pallas import tpu as pltpu

EPS = 1e-6
CONF_KERNEL = 31
LRU_HEADS = 4
LRU_CONV = 4
LRU_C = 8.0
N_GROUPS = 4
EXPERTS_PER_GROUP = 8
N_EXPERTS = N_GROUPS * EXPERTS_PER_GROUP
PAIRS_PER_GROUP = EXPERTS_PER_GROUP * (EXPERTS_PER_GROUP - 1) // 2
N_CLASSES = N_GROUPS * PAIRS_PER_GROUP

LANES = 128
SUBLANES = 8
BF16_SUBLANES = 16
VMEM_LIMIT_BYTES = 56 * 1024 * 1024

TOKEN_TILE = 512
TIME_TILE = TOKEN_TILE // SUBLANES
SUB_TILE = 256
ROW_BLOCK = 320
PERM_PITCH = TIME_TILE + SUBLANES
CONV_ROWS = 128
ROUTER_ROWS = 64
PACK_SUBLANES = 4
MOD_ROWS = 8
DISPATCH_SLOTS = 3
DMA_UNROLL = 8

_NEG_INF = float("-inf")


def _class_tables():
    lo = np.zeros((N_CLASSES,), np.int32)
    hi = np.zeros((N_CLASSES,), np.int32)
    for g in range(N_GROUPS):
        for a in range(EXPERTS_PER_GROUP):
            for b in range(a + 1, EXPERTS_PER_GROUP):
                c = g * PAIRS_PER_GROUP + (a * (2 * EXPERTS_PER_GROUP - 1 - a)) // 2 + (b - a - 1)
                lo[c] = g * EXPERTS_PER_GROUP + a
                hi[c] = g * EXPERTS_PER_GROUP + b
    return lo, hi


_CLASS_LO, _CLASS_HI = _class_tables()


def _bf16(x):
    return x.astype(jnp.bfloat16)


def _dot(a, b):
    return jnp.dot(a, b, preferred_element_type=jnp.float32)


def _bits(x):
    return lax.bitcast_convert_type(x, jnp.uint32)


def _f32_from_bits(x):
    return lax.bitcast_convert_type(x, jnp.float32)


def _token_rows(ref, sublane, n):
    return ref[pl.ds(sublane, n, stride=SUBLANES), :]


def _per_batch(v, ts):
    return jnp.tile(v, (ts // SUBLANES, 1))


def _ada_kernel(c_ref, w_ref, b_ref, o_ref):
    c = c_ref[...]
    ca = _bf16(c * jax.nn.sigmoid(c))
    o_ref[...] = _dot(ca, _bf16(w_ref[...])) + b_ref[...]


def _ada_call(c, w, b):
    n_layers, d, n = w.shape
    bsz = c.shape[0]
    nb = 1024
    return pl.pallas_call(
        _ada_kernel,
        out_shape=jax.ShapeDtypeStruct((n_layers, bsz, n), jnp.float32),
        grid=(n_layers, n // nb),
        in_specs=[
            pl.BlockSpec((bsz, d), lambda l, j: (0, 0)),
            pl.BlockSpec((None, d, nb), lambda l, j: (l, 0, j)),
            pl.BlockSpec((None, 1, nb), lambda l, j: (l, 0, j)),
        ],
        out_specs=pl.BlockSpec((None, bsz, nb), lambda l, j: (l, 0, j)),
        compiler_params=pltpu.CompilerParams(
            dimension_semantics=("arbitrary", "arbitrary"),
            vmem_limit_bytes=VMEM_LIMIT_BYTES),
        name="ada_mod",
    )(c, w, b)


def _rms_mod(x, gain, shift, scale):
    ts = x.shape[0]
    ms = jnp.mean(x * x, axis=-1, keepdims=True)
    return (x * lax.rsqrt(ms + EPS)) * _per_batch(gain * (1.0 + scale), ts) + _per_batch(shift, ts)


def _causal_taps(w_ref, bias_ref, ext_ref, n_taps, row0, n):
    d = ext_ref.shape[1]
    cols = []
    for cb in range(d // LANES):
        sl = slice(cb * LANES, (cb + 1) * LANES)
        acc = jnp.broadcast_to(bias_ref[:, sl], (n, LANES))
        for k in range(n_taps):
            lo = row0 + SUBLANES * k
            acc = acc + w_ref[k:k + 1, sl] * ext_ref[lo:lo + n, sl]
        cols.append(acc)
    return jnp.concatenate(cols, axis=1)


def _causal_taps_unit(w_ref, bias_ref, ext_ref, out_ref, cb, r0, n_taps):
    w_cb, ext_cb = w_ref.at[cb], ext_ref.at[cb]
    acc = jnp.broadcast_to(bias_ref[cb], (CONV_ROWS, LANES))
    for k in range(n_taps):
        win = ext_cb[pl.ds(pl.multiple_of(r0 + SUBLANES * k, SUBLANES), CONV_ROWS), :]
        acc = acc + w_cb[k:k + 1, :] * win
    out_ref.at[cb][pl.ds(r0, CONV_ROWS), :] = acc


def _start_tile_dmas(make_copy, n, inline=False, both_queues=False):
    if inline:
        for r in range(n):
            make_copy(r).start()
        return

    def body(g, carry):
        for k in range(DMA_UNROLL):
            make_copy(g * DMA_UNROLL + k).start(priority=k % 2 if both_queues else 0)
        return carry
    lax.fori_loop(0, n // DMA_UNROLL, body, 0)


def _wait_all_tiles(hbm_ref, vmem_tiles_ref, sem, to_hbm):
    hbm_view = hbm_ref.at[pl.ds(0, vmem_tiles_ref.shape[0])]
    if to_hbm:
        pltpu.make_async_copy(vmem_tiles_ref, hbm_view, sem).wait()
    else:
        pltpu.make_async_copy(hbm_view, vmem_tiles_ref, sem).wait()


def _gather_expert_tiles(ys_ref, ybuf, sem, dest_ref, dnext_ref, step, n_steps, ts, inline):
    slot = step % 2

    def tile_copy(dref, sl):
        def make(r):
            src = pl.multiple_of(dref[0, 0, r] * SUBLANES, SUBLANES)
            return pltpu.make_async_copy(
                ys_ref.at[pl.ds(src, SUBLANES)],
                ybuf.at[sl, pl.ds(pl.multiple_of(r * SUBLANES, SUBLANES), SUBLANES)],
                sem.at[sl])
        return make

    @pl.when(step == 0)
    def _():
        _start_tile_dmas(tile_copy(dest_ref, 0), ts)

    if not inline:
        @pl.when(step + 1 < n_steps)
        def _():
            _start_tile_dmas(tile_copy(dnext_ref, 1 - slot), ts)

    _wait_all_tiles(ys_ref, ybuf.at[slot], sem.at[slot], to_hbm=False)
    if inline:
        _start_tile_dmas(tile_copy(dnext_ref, 1 - slot), ts, inline=True)
    yv = ybuf.at[slot]
    y = jnp.concatenate([_token_rows(yv, s, ts) for s in range(SUBLANES)], axis=1)

    def retire_last():
        if inline:
            @pl.when(step + 1 == n_steps)
            def _():
                _wait_all_tiles(ys_ref, ybuf.at[1 - slot], sem.at[1 - slot], to_hbm=False)

    return y, retire_last


def _route(logits_t):
    ts = logits_t.shape[1]
    row = lax.broadcasted_iota(jnp.int32, (SUBLANES, ts), 0)
    gl = jnp.where(row < N_GROUPS, logits_t[0:SUBLANES], _NEG_INF)
    gmax = jnp.max(gl, axis=0, keepdims=True)
    gsel = jnp.min(jnp.where(gl == gmax, row, SUBLANES), axis=0, keepdims=True)
    p_g = 1.0 / jnp.sum(jnp.exp(gl - gmax), axis=0, keepdims=True)
    es = jnp.zeros((SUBLANES, ts), jnp.float32)
    for g in range(N_GROUPS):
        lo = SUBLANES * (1 + g)
        es = jnp.where(gsel == g, logits_t[lo:lo + EXPERTS_PER_GROUP], es)
    m1 = jnp.max(es, axis=0, keepdims=True)
    i1 = jnp.min(jnp.where(es == m1, row, SUBLANES), axis=0, keepdims=True)
    es2 = jnp.where(row == i1, _NEG_INF, es)
    m2 = jnp.max(es2, axis=0, keepdims=True)
    i2 = jnp.min(jnp.where(es2 == m2, row, SUBLANES), axis=0, keepdims=True)
    t = jnp.exp(m2 - m1)
    w1 = p_g / (1.0 + t)
    w2 = w1 * t
    first_lo = i1 < i2
    e_lo = jnp.minimum(i1, i2)
    e_hi = jnp.maximum(i1, i2)
    gate_lo = jnp.where(first_lo, w1, w2)
    gate_hi = jnp.where(first_lo, w2, w1)
    pair = lax.shift_right_logical(e_lo * (2 * EXPERTS_PER_GROUP - 1 - e_lo), 1) + (e_hi - e_lo - 1)
    cls = gsel * PAIRS_PER_GROUP + pair
    return cls, gate_lo, gate_hi


def _post_mixer(x1, mod_ref, n2_ref, wrt_ref, brt_ref, hrow_ref, cls_ref, rank_ref, cnt_ref, sub):
    ts = x1.shape[0]
    hrow_ref = hrow_ref.at[pl.ds(sub * ts * SUBLANES, ts * SUBLANES)]
    cols = slice(sub * ts, (sub + 1) * ts)
    h2 = _rms_mod(x1, n2_ref[...], mod_ref[3], mod_ref[4])
    h2b = _bf16(h2)
    logits_t = lax.dot_general(wrt_ref[...], h2b, (((1,), (1,)), ((), ())),
                               preferred_element_type=jnp.float32) + brt_ref[:, 0:1]
    cls, gate_lo, gate_hi = _route(logits_t)
    gates_t = jnp.concatenate(
        [gate_lo, gate_hi, jnp.zeros((LANES - 2, ts), jnp.float32)], axis=0)
    h2r = h2b.astype(jnp.float32)
    for j in range(PACK_SUBLANES):
        hi = _bits(h2r[:, (2 * j) * LANES:(2 * j + 1) * LANES])
        lo = _bits(h2r[:, (2 * j + 1) * LANES:(2 * j + 2) * LANES])
        hrow_ref[pl.ds(j, ts, stride=SUBLANES), :] = hi | lax.shift_right_logical(lo, jnp.uint32(16))
    hrow_ref[pl.ds(PACK_SUBLANES, ts, stride=SUBLANES), :] = _bits(gates_t.T)
    for j in range(PACK_SUBLANES + 1, SUBLANES):
        hrow_ref[pl.ds(j, ts, stride=SUBLANES), :] = jnp.zeros((ts, LANES), jnp.uint32)
    cls_ref[0, :, cols] = cls
    crow = lax.broadcasted_iota(jnp.int32, (LANES, ts), 0)
    onehot = jnp.where(crow == cls, 1.0, 0.0)
    cnt = jnp.sum(onehot, axis=1, keepdims=True)
    earlier = (lax.broadcasted_iota(jnp.int32, (ts, ts), 0)
               < lax.broadcasted_iota(jnp.int32, (ts, ts), 1))
    prefix = _dot(_bf16(onehot), _bf16(jnp.where(earlier, 1.0, 0.0)))
    rank = jnp.sum(onehot * (prefix + cnt_ref[:, 0:1]), axis=0, keepdims=True)
    rank_ref[0, :, cols] = rank.astype(jnp.int32)
    cnt_ref[...] += jnp.broadcast_to(cnt, cnt_ref.shape)


def _mixer_out(t, d, ts):
    n_tiles = t // ts
    out_shape = (
        jax.ShapeDtypeStruct((t, d), jnp.float32),
        jax.ShapeDtypeStruct((t * SUBLANES, LANES), jnp.uint32),
        jax.ShapeDtypeStruct((n_tiles, 1, ts), jnp.int32),
        jax.ShapeDtypeStruct((n_tiles, 1, ts), jnp.int32),
        jax.ShapeDtypeStruct((LANES, LANES), jnp.float32),
    )
    out_specs = (
        pl.BlockSpec((ts, d), lambda i: (i, 0)),
        pl.BlockSpec((ts * SUBLANES, LANES), lambda i: (i, 0)),
        pl.BlockSpec((1, 1, ts), lambda i: (i, 0, 0)),
        pl.BlockSpec((1, 1, ts), lambda i: (i, 0, 0)),
        pl.BlockSpec((LANES, LANES), lambda i: (0, 0)),
    )
    return out_shape, out_specs


def _full(shape):
    return pl.BlockSpec(shape, lambda i: tuple(0 for _ in shape))


def _expert_weight_cast(weights, layer, n_steps):
    flat = [w.reshape(w.shape[0], w.shape[1] * w.shape[2], w.shape[3]) for w in weights]
    if any(w.shape[1] % (n_steps * BF16_SUBLANES) for w in flat):
        return [], [], (), ()
    in_specs = [pl.BlockSpec((None, w.shape[1] // n_steps, w.shape[2]), lambda i: (layer, i, 0))
                for w in flat]
    out_shapes = tuple(jax.ShapeDtypeStruct(w.shape[1:], jnp.bfloat16) for w in flat)
    out_specs = tuple(pl.BlockSpec((w.shape[1] // n_steps, w.shape[2]), lambda i: (i, 0))
                      for w in flat)
    return flat, in_specs, out_shapes, out_specs


def _cast_slices(cast_in, cast_out):
    for src, dst in zip(cast_in, cast_out):
        dst[...] = _bf16(src[...])


def _cast_results(outs, weights, layer):
    if outs:
        return [o.reshape(w.shape[1:]) for o, w in zip(outs, weights)]
    return [_bf16(w[layer]) for w in weights]


def _to_tile_order(x_ref, perm, xs):
    bsz, tm, d = x_ref.shape
    for cb in range(d // LANES):
        for p in range(bsz):
            perm[cb, PERM_PITCH * p:PERM_PITCH * p + tm, :] = x_ref[p, :, cb * LANES:(cb + 1) * LANES]
    for m in range(tm):
        for cb in range(d // LANES):
            xs[m * bsz:(m + 1) * bsz, cb * LANES:(cb + 1) * LANES] = (
                perm.at[cb][pl.ds(m, bsz, stride=PERM_PITCH), :])


def _conf_kernel(n_cast, x0_ref, xn_ref, mod_ref, n1_ref, n2_ref, w1_ref, b1_ref, dw_ref, dwb_ref,
                 lng_ref, lnb_ref, w2_ref, b2_ref, wrt_ref, brt_ref, *rest):
    cast_in, rest = rest[:n_cast], rest[n_cast:]
    x1_ref, hrow_ref, cls_ref, rank_ref, cnt_ref, zeros_ref = rest[:6]
    cast_out, (perm, xs, hn, vext, conv) = rest[6:6 + n_cast], rest[6 + n_cast:]
    _cast_slices(cast_in, cast_out)
    i = pl.program_id(0)
    bsz, tm, d = xn_ref.shape
    ts = bsz * tm
    n_chunks = d // LANES
    n_sub = ts // SUB_TILE
    hist = SUBLANES * (CONF_KERNEL - 1)
    slot = i % 2

    def input_norm(x_ref, xs_slot):
        _to_tile_order(x_ref, perm, xs.at[xs_slot])
        for sub in range(n_sub):
            rows = slice(sub * SUB_TILE, (sub + 1) * SUB_TILE)
            hn[rows, :] = _bf16(_rms_mod(xs.at[xs_slot][rows, :], n1_ref[...], mod_ref[0], mod_ref[1]))

    def glu_slice(cb, sub, v_slot):
        r0 = pl.multiple_of(sub * SUB_TILE, SUB_TILE)
        u = _dot(hn[pl.ds(r0, SUB_TILE), :], w1_ref[cb]) + b1_ref[cb]
        v = u[:, 0:LANES] * jax.nn.sigmoid(u[:, LANES:2 * LANES])
        vext.at[v_slot].at[cb][pl.ds(hist + r0, SUB_TILE), :] = v

    @pl.when(i == 0)
    def _():
        vext[0, :, 0:hist, :] = jnp.zeros((n_chunks, hist, LANES), jnp.float32)
        cnt_ref[...] = jnp.zeros_like(cnt_ref)
        input_norm(x0_ref, 0)

        def first(idx, carry):
            glu_slice(idx // n_sub, idx % n_sub, 0)
            return carry

        lax.fori_loop(0, n_chunks * n_sub, first, 0)

    zeros_ref[...] = jnp.zeros_like(zeros_ref)

    input_norm(xn_ref, 1 - slot)
    units_per_sub = SUB_TILE // CONV_ROWS

    def overlapped(idx, carry):
        cb, sub = idx // n_sub, idx % n_sub
        for k in range(units_per_sub):
            r0 = pl.multiple_of(sub * SUB_TILE + k * CONV_ROWS, CONV_ROWS)
            _causal_taps_unit(dw_ref, dwb_ref, vext.at[slot], conv, cb, r0, CONF_KERNEL)
        glu_slice(cb, sub, 1 - slot)
        return carry

    lax.fori_loop(0, n_chunks * n_sub, overlapped, 0)
    vext.at[1 - slot][:, 0:hist, :] = vext.at[slot][:, ts:ts + hist, :]

    xcur = xs.at[slot]
    for sub in range(n_sub):
        rows = slice(sub * SUB_TILE, (sub + 1) * SUB_TILE)
        y = jnp.concatenate([conv[cb, rows, :] for cb in range(n_chunks)], axis=1)
        mu = jnp.mean(y, axis=-1, keepdims=True)
        yc = y - mu
        var = jnp.mean(yc * yc, axis=-1, keepdims=True)
        z = yc * lax.rsqrt(var + EPS) * lng_ref[...] + lnb_ref[...]
        z = z * jax.nn.sigmoid(z)
        mix = _dot(_bf16(z), w2_ref[...]) + b2_ref[...]
        x1 = xcur[rows, :] + _per_batch(mod_ref[2], SUB_TILE) * mix
        x1_ref[rows, :] = x1
        _post_mixer(x1, mod_ref, n2_ref, wrt_ref, brt_ref, hrow_ref, cls_ref, rank_ref, cnt_ref, sub)


def _conf_call(x, mod, n1, n2, w1, b1, dw, dwb, lng, lnb, w2, b2, wrt, brt, zero_rows,
               expert_weights, layer):
    bsz, seq, d = x.shape
    tm = TIME_TILE
    ts = bsz * tm
    n_steps = seq // tm
    assert zero_rows % (n_steps * SUBLANES) == 0
    cast, cast_specs, cast_shapes, cast_out_specs = _expert_weight_cast(expert_weights, layer, n_steps)
    out_shape, out_specs = _mixer_out(bsz * seq, d, ts)
    out_shape += (jax.ShapeDtypeStruct((zero_rows, LANES), jnp.uint32),) + cast_shapes
    out_specs += (pl.BlockSpec((zero_rows // n_steps, LANES), lambda i: (i, 0)),) + cast_out_specs
    hist = SUBLANES * (CONF_KERNEL - 1)
    outs = pl.pallas_call(
        functools.partial(_conf_kernel, len(cast)),
        out_shape=out_shape,
        grid=(n_steps,),
        in_specs=[
            pl.BlockSpec((bsz, tm, d), lambda i: (0, 0, 0)),
            pl.BlockSpec((bsz, tm, d), lambda i: (0, jnp.minimum(i + 1, n_steps - 1), 0)),
            _full((MOD_ROWS, bsz, d)),
            _full((1, d)), _full((1, d)),
            _full((d // LANES, d, 2 * LANES)), _full((d // LANES, 1, 2 * LANES)),
            _full((d // LANES, CONF_KERNEL + 1, LANES)), _full((d // LANES, 1, LANES)),
            _full((1, d)), _full((1, d)),
            _full((d, d)), _full((1, d)),
            _full((ROUTER_ROWS, d)), _full((ROUTER_ROWS, LANES)),
        ] + cast_specs,
        out_specs=out_specs,
        scratch_shapes=[
            pltpu.VMEM((d // LANES, bsz * PERM_PITCH, LANES), jnp.float32),
            pltpu.VMEM((2, ts, d), jnp.float32),
            pltpu.VMEM((ts, d), jnp.bfloat16),
            pltpu.VMEM((2, d // LANES, hist + ts, LANES), jnp.float32),
            pltpu.VMEM((d // LANES, ts, LANES), jnp.float32),
        ],
        compiler_params=pltpu.CompilerParams(
            dimension_semantics=("arbitrary",), vmem_limit_bytes=VMEM_LIMIT_BYTES),
        name="conf_mixer",
    )(x, x, mod, n1, n2, w1, b1, dw, dwb, lng, lnb, w2, b2, wrt, brt, *cast)
    return outs[:6], _cast_results(outs[6:], expert_weights, layer)


def _lru_kernel(n_tiles, n_cast, dest_ref, dnext_ref, xprev_ref, pmod_ref, mod_ref, n1_ref, n2_ref,
                wy_ref, by_ref, wx_ref, bx_ref,
                cw_ref, cb_ref, wa_ref, ba_ref, wi_ref, bi_ref, lam_ref, wo_ref, bo_ref,
                wrt_ref, brt_ref, ys_ref, *rest):
    cast_in, rest = rest[:n_cast], rest[n_cast:]
    x1_ref, hrow_ref, cls_ref, rank_ref, cnt_ref = rest[:5]
    cast_out, (uext, hcar, ybuf, ysem) = rest[5:5 + n_cast], rest[5 + n_cast:]
    _cast_slices(cast_in, cast_out)
    i = pl.program_id(0)
    ts, d = xprev_ref.shape
    hd = d // LRU_HEADS
    hist = SUBLANES * (LRU_CONV - 1)

    @pl.when(i == 0)
    def _():
        uext[0:hist, :] = jnp.zeros((hist, d), jnp.float32)
        hcar[...] = jnp.zeros_like(hcar)
        cnt_ref[...] = jnp.zeros_like(cnt_ref)

    y_prev, retire_last = _gather_expert_tiles(ys_ref, ybuf, ysem, dest_ref, dnext_ref, i, n_tiles, ts,
                                               inline=True)
    nl = -lam_ref[...]
    softplus = jnp.maximum(nl, 0.0) + jnp.log1p(jnp.exp(-jnp.abs(nl)))
    hstate = hcar[...]
    for sub in range(ts // SUB_TILE):
        rows = slice(sub * SUB_TILE, (sub + 1) * SUB_TILE)
        x = xprev_ref[rows, :] + _per_batch(pmod_ref[5], SUB_TILE) * y_prev[rows, :]
        h = _bf16(_rms_mod(x, n1_ref[...], mod_ref[0], mod_ref[1]))
        ygate = jax.nn.gelu(_dot(h, wy_ref[...]) + by_ref[...])
        xb = _dot(h, wx_ref[...]) + bx_ref[...]
        uext[hist + sub * SUB_TILE:hist + (sub + 1) * SUB_TILE, :] = xb
        u = _causal_taps(cw_ref, cb_ref, uext, LRU_CONV, sub * SUB_TILE, SUB_TILE)

        ub = _bf16(u)
        r_parts, i_parts = [], []
        for hh in range(LRU_HEADS):
            uh = ub[:, hh * hd:(hh + 1) * hd]
            r_parts.append(_dot(uh, wa_ref[hh]))
            i_parts.append(_dot(uh, wi_ref[hh]))
        r = jax.nn.sigmoid(jnp.concatenate(r_parts, axis=1) + ba_ref[...])
        ig = jax.nn.sigmoid(jnp.concatenate(i_parts, axis=1) + bi_ref[...])

        log_a = (-LRU_C) * r * softplus
        a = jnp.exp(log_a)
        mult = jnp.sqrt(-jnp.tanh(log_a) * (1.0 + a * a))
        b = mult * (ig * u)

        outs = []
        for m in range(SUB_TILE // SUBLANES):
            sl = slice(m * SUBLANES, (m + 1) * SUBLANES)
            hstate = a[sl] * hstate + b[sl]
            outs.append(hstate)
        hs = jnp.concatenate(outs, axis=0)

        mix = _dot(_bf16(hs * ygate), wo_ref[...]) + bo_ref[...]
        x1 = x + _per_batch(mod_ref[2], SUB_TILE) * mix
        x1_ref[rows, :] = x1
        _post_mixer(x1, mod_ref, n2_ref, wrt_ref, brt_ref, hrow_ref, cls_ref, rank_ref, cnt_ref, sub)
    hcar[...] = hstate
    uext[0:hist, :] = uext[ts:ts + hist, :]
    retire_last()


def _lru_call(dest, x_prev, mod_prev, ys, mod, n1, n2, wy, by, wx, bx, cw, cb, wa, ba, wi, bi, lam,
              wo, bo, wrt, brt, expert_weights, layer):
    t, d = x_prev.shape
    n_tiles, _, ts = dest.shape
    bsz = mod.shape[1]
    hd = d // LRU_HEADS
    hist = SUBLANES * (LRU_CONV - 1)
    cast, cast_specs, cast_shapes, cast_out_specs = _expert_weight_cast(expert_weights, layer, n_tiles)
    out_shape, out_specs = _mixer_out(t, d, ts)
    out_shape += cast_shapes
    out_specs += cast_out_specs
    outs = pl.pallas_call(
        functools.partial(_lru_kernel, n_tiles, len(cast)),
        out_shape=out_shape,
        grid=(n_tiles,),
        in_specs=[
            pl.BlockSpec((1, 1, ts), lambda i: (i, 0, 0), memory_space=pltpu.SMEM),
            pl.BlockSpec((1, 1, ts), lambda i: (jnp.minimum(i + 1, n_tiles - 1), 0, 0),
                         memory_space=pltpu.SMEM),
            pl.BlockSpec((ts, d), lambda i: (i, 0)),
            _full((MOD_ROWS, bsz, d)), _full((MOD_ROWS, bsz, d)),
            _full((1, d)), _full((1, d)),
            _full((d, d)), _full((1, d)),
            _full((d, d)), _full((1, d)),
            _full((LRU_CONV, d)), _full((1, d)),
            _full((LRU_HEADS, hd, hd)), _full((1, d)),
            _full((LRU_HEADS, hd, hd)), _full((1, d)),
            _full((1, d)),
            _full((d, d)), _full((1, d)),
            _full((ROUTER_ROWS, d)), _full((ROUTER_ROWS, LANES)),
            pl.BlockSpec(memory_space=pl.ANY),
        ] + cast_specs,
        out_specs=out_specs,
        scratch_shapes=[
            pltpu.VMEM((hist + ts, d), jnp.float32),
            pltpu.VMEM((SUBLANES, d), jnp.float32),
            pltpu.VMEM((2, ts * SUBLANES, LANES), jnp.float32),
            pltpu.SemaphoreType.DMA((2,)),
        ],
        compiler_params=pltpu.CompilerParams(
            dimension_semantics=("arbitrary",), vmem_limit_bytes=VMEM_LIMIT_BYTES),
        name="lru_mixer",
    )(dest, dest, x_prev, mod_prev, mod, n1, n2, wy, by, wx, bx, cw, cb, wa, ba, wi, bi, lam, wo, bo,
      wrt, brt, ys, *cast)
    return outs[:5], _cast_results(outs[5:], expert_weights, layer)


def _plan_kernel(n_blocks, cnt_ref, clo_ref, chi_ref, cls_ref, rank_ref,
                 dest_ref, blo_ref, bhi_ref, nblk_ref, class_start):
    n_tiles, _, ts = cls_ref.shape
    class_start[...] = jnp.zeros_like(class_start)

    def per_class(c, nb_total):
        class_start[pl.ds(c, 1), :] = jnp.full((1, LANES), nb_total * ROW_BLOCK, jnp.int32)
        nb = (cnt_ref[c] + (ROW_BLOCK - 1)) // ROW_BLOCK

        def fill(k, carry):
            blo_ref[nb_total + k] = clo_ref[c]
            bhi_ref[nb_total + k] = chi_ref[c]
            return carry

        lax.fori_loop(0, nb, fill, 0)
        return nb_total + nb

    nb_used = lax.fori_loop(0, N_CLASSES, per_class, 0)
    nblk_ref[0] = nb_used
    last = jnp.maximum(nb_used - 1, 0)

    def tail(k, carry):
        blo_ref[k] = blo_ref[last]
        bhi_ref[k] = bhi_ref[last]
        return carry

    lax.fori_loop(nb_used, n_blocks, tail, 0)

    starts = class_start[:, 0:1]
    class_id = lax.broadcasted_iota(jnp.int32, (LANES, ts), 0)

    def per_tile(t, carry):
        first = jnp.sum(jnp.where(class_id == cls_ref[t], starts, 0), axis=0, keepdims=True)
        dest_ref[t] = first + rank_ref[t]
        return carry

    lax.fori_loop(0, n_tiles, per_tile, 0)


def _plan_call(cls, rank, counts, n_blocks):
    n_tiles, _, ts = cls.shape
    smem_full = pl.BlockSpec(memory_space=pltpu.SMEM)
    vmem_full = pl.BlockSpec(memory_space=pltpu.VMEM)
    return pl.pallas_call(
        functools.partial(_plan_kernel, n_blocks),
        out_shape=(
            jax.ShapeDtypeStruct((n_tiles, 1, ts), jnp.int32),
            jax.ShapeDtypeStruct((n_blocks,), jnp.int32),
            jax.ShapeDtypeStruct((n_blocks,), jnp.int32),
            jax.ShapeDtypeStruct((1,), jnp.int32),
        ),
        in_specs=[smem_full, smem_full, smem_full, vmem_full, vmem_full],
        out_specs=(vmem_full, smem_full, smem_full, smem_full),
        scratch_shapes=[pltpu.VMEM((LANES, LANES), jnp.int32)],
        compiler_params=pltpu.CompilerParams(vmem_limit_bytes=VMEM_LIMIT_BYTES),
        name="moe_plan",
    )(counts, jnp.asarray(_CLASS_LO), jnp.asarray(_CLASS_HI), cls, rank)


def _dispatch_kernel(n_tiles, dest_ref, hrow_ref, xs_in_ref, xs_ref, stage, sem_in, sem_out):
    del xs_in_ref
    i = pl.program_id(0)
    ts = dest_ref.shape[2]
    rows = ts * SUBLANES

    def load(tile, slot):
        return pltpu.make_async_copy(hrow_ref.at[pl.ds(pl.multiple_of(tile * rows, rows), rows)],
                                     stage.at[slot], sem_in.at[slot])

    def tile_copy(slot, r, dst_row):
        return pltpu.make_async_copy(
            stage.at[slot, pl.ds(pl.multiple_of(r * SUBLANES, SUBLANES), SUBLANES)],
            xs_ref.at[pl.ds(pl.multiple_of(dst_row * SUBLANES, SUBLANES), SUBLANES)],
            sem_out.at[slot])

    def drain(slot):
        _wait_all_tiles(xs_ref, stage.at[slot], sem_out.at[slot], to_hbm=True)

    slot = i % DISPATCH_SLOTS
    nxt = (i + 1) % DISPATCH_SLOTS

    @pl.when(i == 0)
    def _():
        load(0, 0).start()

    @pl.when(i >= 2)
    def _():
        drain(nxt)

    @pl.when(i + 1 < n_tiles)
    def _():
        load(i + 1, nxt).start()

    load(i, slot).wait()

    _start_tile_dmas(lambda r: tile_copy(slot, r, dest_ref[0, 0, r]), ts, both_queues=True)

    @pl.when(i == n_tiles - 1)
    def _():
        @pl.when(i >= 1)
        def _():
            drain((i + DISPATCH_SLOTS - 1) % DISPATCH_SLOTS)
        drain(slot)


def _dispatch_call(dest, hrows, xs_init):
    n_tiles, _, ts = dest.shape
    assert xs_init.dtype == jnp.uint32 and xs_init.shape[1] == LANES
    any_spec = pl.BlockSpec(memory_space=pl.ANY)
    return pl.pallas_call(
        functools.partial(_dispatch_kernel, n_tiles),
        out_shape=jax.ShapeDtypeStruct(xs_init.shape, jnp.uint32),
        grid=(n_tiles,),
        in_specs=[
            pl.BlockSpec((1, 1, ts), lambda i: (i, 0, 0), memory_space=pltpu.SMEM),
            any_spec, any_spec,
        ],
        out_specs=any_spec,
        scratch_shapes=[
            pltpu.VMEM((DISPATCH_SLOTS, ts * SUBLANES, LANES), jnp.uint32),
            pltpu.SemaphoreType.DMA((DISPATCH_SLOTS,)),
            pltpu.SemaphoreType.DMA((DISPATCH_SLOTS,)),
        ],
        input_output_aliases={2: 0},
        compiler_params=pltpu.CompilerParams(
            dimension_semantics=("arbitrary",), vmem_limit_bytes=VMEM_LIMIT_BYTES),
        name="moe_dispatch",
    )(dest, hrows, xs_init)


def _expert_kernel(blo_ref, bhi_ref, nblk_ref, xs_ref, wg_lo, wu_lo, wd_lo, wg_hi, wu_hi, wd_hi, ys_ref):
    del blo_ref, bhi_ref
    b = pl.program_id(0)
    rb = ROW_BLOCK

    @pl.when(b < nblk_ref[0])
    def _():
        pieces = []
        for j in range(PACK_SUBLANES):
            w = _token_rows(xs_ref, j, rb)
            pieces.append(_bf16(_f32_from_bits(w & jnp.uint32(0xFFFF0000))))
            pieces.append(_bf16(_f32_from_bits(lax.shift_left(w, jnp.uint32(16)))))
        xb = jnp.concatenate(pieces, axis=1)
        gates = _f32_from_bits(_token_rows(xs_ref, PACK_SUBLANES, rb))
        acc = None
        for idx, (wg, wu, wd) in enumerate(((wg_lo, wu_lo, wd_lo), (wg_hi, wu_hi, wd_hi))):
            a = _dot(xb, wg[...])
            hid = (a * jax.nn.sigmoid(a)) * _dot(xb, wu[...]) * gates[:, idx:idx + 1]
            part = _dot(_bf16(hid), wd[...])
            acc = part if acc is None else acc + part
        for s in range(SUBLANES):
            ys_ref[pl.ds(s, rb, stride=SUBLANES), :] = acc[:, s * LANES:(s + 1) * LANES]

    @pl.when(b >= nblk_ref[0])
    def _():
        ys_ref[...] = jnp.zeros_like(ys_ref)


def _expert_call(blo, bhi, nblk, xs, wg, wu, wd):
    n_rows = xs.shape[0] // SUBLANES
    d, de = wg.shape[1], wg.shape[2]
    assert d == SUBLANES * LANES
    n_blocks = n_rows // ROW_BLOCK

    def w_spec(shape, which):
        if which == 0:
            return pl.BlockSpec((None,) + shape, lambda b, lo, hi, n: (lo[b], 0, 0))
        return pl.BlockSpec((None,) + shape, lambda b, lo, hi, n: (hi[b], 0, 0))

    def x_map(b, lo, hi, n):
        return (jnp.minimum(b, jnp.maximum(n[0] - 1, 0)), 0)

    grid_spec = pltpu.PrefetchScalarGridSpec(
        num_scalar_prefetch=3,
        grid=(n_blocks,),
        in_specs=[
            pl.BlockSpec((ROW_BLOCK * SUBLANES, LANES), x_map),
            w_spec((d, de), 0), w_spec((d, de), 0), w_spec((de, d), 0),
            w_spec((d, de), 1), w_spec((d, de), 1), w_spec((de, d), 1),
        ],
        out_specs=pl.BlockSpec((ROW_BLOCK * SUBLANES, LANES), lambda b, lo, hi, n: (b, 0)),
    )
    return pl.pallas_call(
        _expert_kernel,
        out_shape=jax.ShapeDtypeStruct((n_rows * SUBLANES, LANES), jnp.float32),
        grid_spec=grid_spec,
        compiler_params=pltpu.CompilerParams(
            dimension_semantics=("arbitrary",), vmem_limit_bytes=VMEM_LIMIT_BYTES),
        name="moe_experts",
    )(blo, bhi, nblk, xs, wg, wu, wd, wg, wu, wd)


def _final_kernel(n_tiles, dest_ref, dnext_ref, x1_ref, mod_ref, fmod_ref, fg_ref, ys_ref,
                  out_ref, ybuf, sem, unperm):
    i = pl.program_id(0)
    ts, d = x1_ref.shape
    bsz, tm, _ = out_ref.shape
    y, retire_last = _gather_expert_tiles(ys_ref, ybuf, sem, dest_ref, dnext_ref, i, n_tiles, ts,
                                          inline=False)
    x2 = x1_ref[...] + _per_batch(mod_ref[5], ts) * y
    res = _rms_mod(x2, fg_ref[...], fmod_ref[0], fmod_ref[1])
    for cb in range(d // LANES):
        unperm[cb] = res[:, cb * LANES:(cb + 1) * LANES]
    for cb in range(d // LANES):
        for p in range(bsz):
            out_ref[p, :, cb * LANES:(cb + 1) * LANES] = unperm.at[cb][pl.ds(p, tm, stride=SUBLANES), :]
    retire_last()


def _final_call(dest, x1, mod, ys, bsz, fmod, fin_g):
    t, d = x1.shape
    n_tiles, _, ts = dest.shape
    tm = ts // bsz
    return pl.pallas_call(
        functools.partial(_final_kernel, n_tiles),
        out_shape=jax.ShapeDtypeStruct((bsz, t // bsz, d), jnp.float32),
        grid=(n_tiles,),
        in_specs=[
            pl.BlockSpec((1, 1, ts), lambda i: (i, 0, 0), memory_space=pltpu.SMEM),
            pl.BlockSpec((1, 1, ts), lambda i: (jnp.minimum(i + 1, n_tiles - 1), 0, 0),
                         memory_space=pltpu.SMEM),
            pl.BlockSpec((ts, d), lambda i: (i, 0)),
            _full((MOD_ROWS, bsz, d)), _full((MOD_ROWS, bsz, d)), _full((1, d)),
            pl.BlockSpec(memory_space=pl.ANY),
        ],
        out_specs=pl.BlockSpec((bsz, tm, d), lambda i: (0, i, 0)),
        scratch_shapes=[pltpu.VMEM((2, ts * SUBLANES, LANES), jnp.float32),
                        pltpu.SemaphoreType.DMA((2,)),
                        pltpu.VMEM((d // LANES, ts, LANES), jnp.float32)],
        compiler_params=pltpu.CompilerParams(
            dimension_semantics=("arbitrary",), vmem_limit_bytes=VMEM_LIMIT_BYTES),
        name="moe_combine_final",
    )(dest, dest, x1, mod, fmod, fin_g, ys)


def _router_operands(wg, bg, wr, br):
    d = wg.shape[0]
    pad_g = jnp.zeros((SUBLANES - N_GROUPS, d), jnp.float32)
    pad_e = jnp.zeros((ROUTER_ROWS - SUBLANES - N_EXPERTS, d), jnp.float32)
    wrt = jnp.concatenate([wg.T, pad_g, wr.T, pad_e], axis=0)
    bcol = jnp.concatenate([bg, jnp.zeros((SUBLANES - N_GROUPS,), jnp.float32), br,
                            jnp.zeros((ROUTER_ROWS - SUBLANES - N_EXPERTS,), jnp.float32)])
    brt = jnp.broadcast_to(bcol[:, None], (ROUTER_ROWS, LANES))
    return _bf16(wrt), brt


def _mod_table(m, parts):
    bsz = m.shape[0]
    d = m.shape[1] // parts
    m = jnp.transpose(m.reshape(bsz, parts, d), (1, 0, 2))
    return jnp.concatenate([m, jnp.zeros((MOD_ROWS - parts, bsz, d), jnp.float32)], axis=0)


def _row(v):
    return v.reshape(1, -1)


def _moe_experts(hrows, cls, rank, cnt, weights_bf16, n_blocks, xs_init):
    counts = cnt[:, 0].astype(jnp.int32)
    dest, blo, bhi, nblk = _plan_call(cls, rank, counts, n_blocks)
    xs = _dispatch_call(dest, hrows, xs_init)
    ys = _expert_call(blo, bhi, nblk, xs, *weights_bf16)
    return xs, ys, dest


def kernel(x, c, ada_w, ada_b, norm1_g, norm2_g, cm_w1, cm_b1, cm_dw, cm_dwb, cm_ln_g, cm_ln_b, cm_w2, cm_b2, rg_wy, rg_by, rg_wx, rg_bx, rg_cw, rg_cb, rg_wa, rg_ba, rg_wi, rg_bi, rg_lambda, rg_wo, rg_bo, moe_wg, moe_bg, moe_wr, moe_br, moe_w_gate, moe_w_up, moe_w_down, fin_ada_w, fin_ada_b, fin_g):
    bsz, seq, d = x.shape
    t = bsz * seq
    depth = ada_w.shape[0]
    assert depth == 2 and bsz == SUBLANES and seq % TIME_TILE == 0
    assert d == 2 * PACK_SUBLANES * LANES

    mods = _ada_call(c, ada_w, ada_b[:, None, :])
    fmods = _ada_call(c, fin_ada_w[None], fin_ada_b[None, None, :])
    fmod = _mod_table(fmods[0], 2)
    n_steps = seq // TIME_TILE
    granule = n_steps // math.gcd(n_steps, ROW_BLOCK)
    n_blocks = -(-(-(-t // ROW_BLOCK) + N_CLASSES) // granule) * granule
    mod0, mod1 = _mod_table(mods[0], 6), _mod_table(mods[1], 6)

    wrt, brt = _router_operands(moe_wg[0], moe_bg[0], moe_wr[0], moe_br[0])
    dw = jnp.concatenate([cm_dw[0], jnp.zeros((1, d), jnp.float32)], axis=0)
    dw = jnp.transpose(dw.reshape(CONF_KERNEL + 1, d // LANES, LANES), (1, 0, 2))
    expert_weights = (moe_w_gate, moe_w_up, moe_w_down)
    n_chunks = d // LANES
    w1c = jnp.transpose(_bf16(cm_w1[0]).reshape(d, 2, n_chunks, LANES), (2, 0, 1, 3))
    w1c = w1c.reshape(n_chunks, d, 2 * LANES)
    b1c = jnp.transpose(cm_b1[0].reshape(2, n_chunks, LANES), (1, 0, 2)).reshape(n_chunks, 1, 2 * LANES)
    (x1, hrows, cls, rank, cnt, xs_init), weights0 = _conf_call(
        x, mod0, _row(norm1_g[0]), _row(norm2_g[0]), w1c, b1c,
        dw, cm_dwb[0].reshape(d // LANES, 1, LANES), _row(cm_ln_g[0]), _row(cm_ln_b[0]),
        _bf16(cm_w2[0]), _row(cm_b2[0]), wrt, brt, n_blocks * ROW_BLOCK * SUBLANES,
        expert_weights, 0)
    xs, ys, dest = _moe_experts(hrows, cls, rank, cnt, weights0, n_blocks, xs_init)

    wrt, brt = _router_operands(moe_wg[1], moe_bg[1], moe_wr[1], moe_br[1])
    (x1, hrows, cls, rank, cnt), weights1 = _lru_call(
        dest, x1, mod0, ys, mod1, _row(norm1_g[1]), _row(norm2_g[1]),
        _bf16(rg_wy[0]), _row(rg_by[0]), _bf16(rg_wx[0]), _row(rg_bx[0]),
        rg_cw[0], _row(rg_cb[0]), _bf16(rg_wa[0]), _row(rg_ba[0]), _bf16(rg_wi[0]), _row(rg_bi[0]),
        _row(rg_lambda[0]), _bf16(rg_wo[0]), _row(rg_bo[0]), wrt, brt, expert_weights, 1)
    _, ys, dest = _moe_experts(hrows, cls, rank, cnt, weights1, n_blocks, xs)
    return _final_call(dest, x1, mod1, ys, bsz, fmod, _row(fin_g))
```

```python
import functools
import math

import jax
import jax.numpy as jnp
import numpy as np
from jax import lax
from jax.experimental import pallas as pl
from jax.experimental.pallas import tpu as pltpu

EPS = 1e-6
CONF_KERNEL = 31
LRU_HEADS = 4
LRU_CONV = 4
LRU_C = 8.0
N_GROUPS = 4
EXPERTS_PER_GROUP = 8
N_EXPERTS = N_GROUPS * EXPERTS_PER_GROUP
PAIRS_PER_GROUP = EXPERTS_PER_GROUP * (EXPERTS_PER_GROUP - 1) // 2
N_CLASSES = N_GROUPS * PAIRS_PER_GROUP

LANES = 128
SUBLANES = 8
BF16_SUBLANES = 16
VMEM_LIMIT_BYTES = 56 * 1024 * 1024

TOKEN_TILE = 512
TIME_TILE = TOKEN_TILE // SUBLANES
SUB_TILE = 256
ROW_BLOCK = 512
ROW_STEP = 128
PERM_PITCH = TIME_TILE + SUBLANES
CONV_ROWS = 128
ROUTER_ROWS = 64
PACK_SUBLANES = 4
MOD_ROWS = 8
DISPATCH_SLOTS = 3
DMA_UNROLL = 8

_NEG_INF = float("-inf")


def _class_tables():
    lo = np.zeros((N_CLASSES,), np.int32)
    hi = np.zeros((N_CLASSES,), np.int32)
    for g in range(N_GROUPS):
        for a in range(EXPERTS_PER_GROUP):
            for b in range(a + 1, EXPERTS_PER_GROUP):
                c = g * PAIRS_PER_GROUP + (a * (2 * EXPERTS_PER_GROUP - 1 - a)) // 2 + (b - a - 1)
                lo[c] = g * EXPERTS_PER_GROUP + a
                hi[c] = g * EXPERTS_PER_GROUP + b
    return lo, hi


_CLASS_LO, _CLASS_HI = _class_tables()


def _bf16(x):
    return x.astype(jnp.bfloat16)


def _dot(a, b):
    return jnp.dot(a, b, preferred_element_type=jnp.float32)


def _bits(x):
    return lax.bitcast_convert_type(x, jnp.uint32)


def _f32_from_bits(x):
    return lax.bitcast_convert_type(x, jnp.float32)


def _token_rows(ref, sublane, n):
    return ref[pl.ds(sublane, n, stride=SUBLANES), :]


def _per_batch(v, ts):
    return jnp.tile(v, (ts // SUBLANES, 1))


def _ada_kernel(c_ref, w_ref, b_ref, o_ref):
    c = c_ref[...]
    ca = _bf16(c * jax.nn.sigmoid(c))
    o_ref[...] = _dot(ca, _bf16(w_ref[...])) + b_ref[...]


def _ada_call(c, w, b):
    n_layers, d, n = w.shape
    bsz = c.shape[0]
    nb = 1024
    return pl.pallas_call(
        _ada_kernel,
        out_shape=jax.ShapeDtypeStruct((n_layers, bsz, n), jnp.float32),
        grid=(n_layers, n // nb),
        in_specs=[
            pl.BlockSpec((bsz, d), lambda l, j: (0, 0)),
            pl.BlockSpec((None, d, nb), lambda l, j: (l, 0, j)),
            pl.BlockSpec((None, 1, nb), lambda l, j: (l, 0, j)),
        ],
        out_specs=pl.BlockSpec((None, bsz, nb), lambda l, j: (l, 0, j)),
        compiler_params=pltpu.CompilerParams(
            dimension_semantics=("arbitrary", "arbitrary"),
            vmem_limit_bytes=VMEM_LIMIT_BYTES),
        name="ada_mod",
    )(c, w, b)


def _rms_mod(x, gain, shift, scale):
    ts = x.shape[0]
    ms = jnp.mean(x * x, axis=-1, keepdims=True)
    return (x * lax.rsqrt(ms + EPS)) * _per_batch(gain * (1.0 + scale), ts) + _per_batch(shift, ts)


def _causal_taps(w_ref, bias_ref, ext_ref, n_taps, row0, n):
    d = ext_ref.shape[1]
    cols = []
    for cb in range(d // LANES):
        sl = slice(cb * LANES, (cb + 1) * LANES)
        acc = jnp.broadcast_to(bias_ref[:, sl], (n, LANES))
        for k in range(n_taps):
            lo = row0 + SUBLANES * k
            acc = acc + w_ref[k:k + 1, sl] * ext_ref[lo:lo + n, sl]
        cols.append(acc)
    return jnp.concatenate(cols, axis=1)


def _causal_taps_unit(w_ref, bias_ref, ext_ref, out_ref, cb, r0, n_taps):
    w_cb, ext_cb = w_ref.at[cb], ext_ref.at[cb]
    acc = jnp.broadcast_to(bias_ref[cb], (CONV_ROWS, LANES))
    for k in range(n_taps):
        win = ext_cb[pl.ds(pl.multiple_of(r0 + SUBLANES * k, SUBLANES), CONV_ROWS), :]
        acc = acc + w_cb[k:k + 1, :] * win
    out_ref.at[cb][pl.ds(r0, CONV_ROWS), :] = acc


def _start_tile_dmas(make_copy, n, inline=False, both_queues=False):
    if inline:
        for r in range(n):
            make_copy(r).start()
        return

    def body(g, carry):
        for k in range(DMA_UNROLL):
            make_copy(g * DMA_UNROLL + k).start(priority=k % 2 if both_queues else 0)
        return carry
    lax.fori_loop(0, n // DMA_UNROLL, body, 0)


def _wait_all_tiles(hbm_ref, vmem_tiles_ref, sem, to_hbm):
    hbm_view = hbm_ref.at[pl.ds(0, vmem_tiles_ref.shape[0])]
    if to_hbm:
        pltpu.make_async_copy(vmem_tiles_ref, hbm_view, sem).wait()
    else:
        pltpu.make_async_copy(hbm_view, vmem_tiles_ref, sem).wait()


def _gather_expert_tiles(ys_ref, ybuf, sem, dest_ref, dnext_ref, step, n_steps, ts, inline):
    slot = step % 2

    def tile_copy(dref, sl):
        def make(r):
            src = pl.multiple_of(dref[0, 0, r] * SUBLANES, SUBLANES)
            return pltpu.make_async_copy(
                ys_ref.at[pl.ds(src, SUBLANES)],
                ybuf.at[sl, pl.ds(pl.multiple_of(r * SUBLANES, SUBLANES), SUBLANES)],
                sem.at[sl])
        return make

    @pl.when(step == 0)
    def _():
        _start_tile_dmas(tile_copy(dest_ref, 0), ts)

    if not inline:
        @pl.when(step + 1 < n_steps)
        def _():
            _start_tile_dmas(tile_copy(dnext_ref, 1 - slot), ts)

    _wait_all_tiles(ys_ref, ybuf.at[slot], sem.at[slot], to_hbm=False)
    if inline:
        _start_tile_dmas(tile_copy(dnext_ref, 1 - slot), ts, inline=True)
    yv = ybuf.at[slot]
    y = jnp.concatenate([_token_rows(yv, s, ts) for s in range(SUBLANES)], axis=1)

    def retire_last():
        if inline:
            @pl.when(step + 1 == n_steps)
            def _():
                _wait_all_tiles(ys_ref, ybuf.at[1 - slot], sem.at[1 - slot], to_hbm=False)

    return y, retire_last


def _route(logits_t):
    ts = logits_t.shape[1]
    row = lax.broadcasted_iota(jnp.int32, (SUBLANES, ts), 0)
    gl = jnp.where(row < N_GROUPS, logits_t[0:SUBLANES], _NEG_INF)
    gmax = jnp.max(gl, axis=0, keepdims=True)
    gsel = jnp.min(jnp.where(gl == gmax, row, SUBLANES), axis=0, keepdims=True)
    p_g = 1.0 / jnp.sum(jnp.exp(gl - gmax), axis=0, keepdims=True)
    es = jnp.zeros((SUBLANES, ts), jnp.float32)
    for g in range(N_GROUPS):
        lo = SUBLANES * (1 + g)
        es = jnp.where(gsel == g, logits_t[lo:lo + EXPERTS_PER_GROUP], es)
    m1 = jnp.max(es, axis=0, keepdims=True)
    i1 = jnp.min(jnp.where(es == m1, row, SUBLANES), axis=0, keepdims=True)
    es2 = jnp.where(row == i1, _NEG_INF, es)
    m2 = jnp.max(es2, axis=0, keepdims=True)
    i2 = jnp.min(jnp.where(es2 == m2, row, SUBLANES), axis=0, keepdims=True)
    t = jnp.exp(m2 - m1)
    w1 = p_g / (1.0 + t)
    w2 = w1 * t
    first_lo = i1 < i2
    e_lo = jnp.minimum(i1, i2)
    e_hi = jnp.maximum(i1, i2)
    gate_lo = jnp.where(first_lo, w1, w2)
    gate_hi = jnp.where(first_lo, w2, w1)
    pair = lax.shift_right_logical(e_lo * (2 * EXPERTS_PER_GROUP - 1 - e_lo), 1) + (e_hi - e_lo - 1)
    cls = gsel * PAIRS_PER_GROUP + pair
    return cls, gate_lo, gate_hi


def _post_mixer(x1, mod_ref, n2_ref, wrt_ref, brt_ref, hrow_ref, cls_ref, rank_ref, cnt_ref, sub):
    ts = x1.shape[0]
    hrow_ref = hrow_ref.at[pl.ds(sub * ts * SUBLANES, ts * SUBLANES)]
    cols = slice(sub * ts, (sub + 1) * ts)
    h2 = _rms_mod(x1, n2_ref[...], mod_ref[3], mod_ref[4])
    h2b = _bf16(h2)
    logits_t = lax.dot_general(wrt_ref[...], h2b, (((1,), (1,)), ((), ())),
                               preferred_element_type=jnp.float32) + brt_ref[:, 0:1]
    cls, gate_lo, gate_hi = _route(logits_t)
    gates_t = jnp.concatenate(
        [gate_lo, gate_hi, jnp.zeros((LANES - 2, ts), jnp.float32)], axis=0)
    h2r = h2b.astype(jnp.float32)
    for j in range(PACK_SUBLANES):
        hi = _bits(h2r[:, (2 * j) * LANES:(2 * j + 1) * LANES])
        lo = _bits(h2r[:, (2 * j + 1) * LANES:(2 * j + 2) * LANES])
        hrow_ref[pl.ds(j, ts, stride=SUBLANES), :] = hi | lax.shift_right_logical(lo, jnp.uint32(16))
    hrow_ref[pl.ds(PACK_SUBLANES, ts, stride=SUBLANES), :] = _bits(gates_t.T)
    for j in range(PACK_SUBLANES + 1, SUBLANES):
        hrow_ref[pl.ds(j, ts, stride=SUBLANES), :] = jnp.zeros((ts, LANES), jnp.uint32)
    cls_ref[0, :, cols] = cls
    crow = lax.broadcasted_iota(jnp.int32, (LANES, ts), 0)
    onehot = jnp.where(crow == cls, 1.0, 0.0)
    cnt = jnp.sum(onehot, axis=1, keepdims=True)
    earlier = (lax.broadcasted_iota(jnp.int32, (ts, ts), 0)
               < lax.broadcasted_iota(jnp.int32, (ts, ts), 1))
    prefix = _dot(_bf16(onehot), _bf16(jnp.where(earlier, 1.0, 0.0)))
    rank = jnp.sum(onehot * (prefix + cnt_ref[:, 0:1]), axis=0, keepdims=True)
    rank_ref[0, :, cols] = rank.astype(jnp.int32)
    cnt_ref[...] += jnp.broadcast_to(cnt, cnt_ref.shape)


def _mixer_out(t, d, ts):
    n_tiles = t // ts
    out_shape = (
        jax.ShapeDtypeStruct((t, d), jnp.float32),
        jax.ShapeDtypeStruct((t * SUBLANES, LANES), jnp.uint32),
        jax.ShapeDtypeStruct((n_tiles, 1, ts), jnp.int32),
        jax.ShapeDtypeStruct((n_tiles, 1, ts), jnp.int32),
        jax.ShapeDtypeStruct((LANES, LANES), jnp.float32),
    )
    out_specs = (
        pl.BlockSpec((ts, d), lambda i: (i, 0)),
        pl.BlockSpec((ts * SUBLANES, LANES), lambda i: (i, 0)),
        pl.BlockSpec((1, 1, ts), lambda i: (i, 0, 0)),
        pl.BlockSpec((1, 1, ts), lambda i: (i, 0, 0)),
        pl.BlockSpec((LANES, LANES), lambda i: (0, 0)),
    )
    return out_shape, out_specs


def _full(shape):
    return pl.BlockSpec(shape, lambda i: tuple(0 for _ in shape))


def _expert_weight_cast(weights, layer, n_steps):
    flat = [w.reshape(w.shape[0], w.shape[1] * w.shape[2], w.shape[3]) for w in weights]
    if any(w.shape[1] % (n_steps * BF16_SUBLANES) for w in flat):
        return [], [], (), ()
    in_specs = [pl.BlockSpec((None, w.shape[1] // n_steps, w.shape[2]), lambda i: (layer, i, 0))
                for w in flat]
    out_shapes = tuple(jax.ShapeDtypeStruct(w.shape[1:], jnp.bfloat16) for w in flat)
    out_specs = tuple(pl.BlockSpec((w.shape[1] // n_steps, w.shape[2]), lambda i: (i, 0))
                      for w in flat)
    return flat, in_specs, out_shapes, out_specs


def _cast_slices(cast_in, cast_out):
    for src, dst in zip(cast_in, cast_out):
        dst[...] = _bf16(src[...])


def _cast_results(outs, weights, layer):
    if outs:
        return [o.reshape(w.shape[1:]) for o, w in zip(outs, weights)]
    return [_bf16(w[layer]) for w in weights]


def _to_tile_order(x_ref, perm, xs):
    bsz, tm, d = x_ref.shape
    for cb in range(d // LANES):
        for p in range(bsz):
            perm[cb, PERM_PITCH * p:PERM_PITCH * p + tm, :] = x_ref[p, :, cb * LANES:(cb + 1) * LANES]
    for m in range(tm):
        for cb in range(d // LANES):
            xs[m * bsz:(m + 1) * bsz, cb * LANES:(cb + 1) * LANES] = (
                perm.at[cb][pl.ds(m, bsz, stride=PERM_PITCH), :])


def _conf_kernel(n_cast, x_ref, mod_ref, n1_ref, n2_ref, w1_ref, b1_ref, dw_ref, dwb_ref,
                 lng_ref, lnb_ref, w2_ref, b2_ref, wrt_ref, brt_ref, *rest):
    cast_in, rest = rest[:n_cast], rest[n_cast:]
    x1_ref, hrow_ref, cls_ref, rank_ref, cnt_ref, zeros_ref = rest[:6]
    cast_out, (perm, xs, vext, conv) = rest[6:6 + n_cast], rest[6 + n_cast:]
    _cast_slices(cast_in, cast_out)
    i = pl.program_id(0)
    bsz, tm, d = x_ref.shape
    ts = bsz * tm
    n_chunks = d // LANES
    hist = SUBLANES * (CONF_KERNEL - 1)

    @pl.when(i == 0)
    def _():
        vext[:, 0:hist, :] = jnp.zeros((n_chunks, hist, LANES), jnp.float32)
        cnt_ref[...] = jnp.zeros_like(cnt_ref)

    zeros_ref[...] = jnp.zeros_like(zeros_ref)

    _to_tile_order(x_ref, perm, xs)
    for sub in range(ts // SUB_TILE):
        rows = slice(sub * SUB_TILE, (sub + 1) * SUB_TILE)
        h = _bf16(_rms_mod(xs[rows, :], n1_ref[...], mod_ref[0], mod_ref[1]))
        u = _dot(h, w1_ref[...]) + b1_ref[...]
        v = u[:, 0:d] * jax.nn.sigmoid(u[:, d:2 * d])
        for cb in range(n_chunks):
            vext[cb, hist + sub * SUB_TILE:hist + (sub + 1) * SUB_TILE, :] = v[:, cb * LANES:(cb + 1) * LANES]

    units_per_chunk = ts // CONV_ROWS

    def conv_unit(idx, carry):
        r0 = pl.multiple_of((idx % units_per_chunk) * CONV_ROWS, CONV_ROWS)
        _causal_taps_unit(dw_ref, dwb_ref, vext, conv, idx // units_per_chunk, r0, CONF_KERNEL)
        return carry

    lax.fori_loop(0, n_chunks * units_per_chunk, conv_unit, 0)
    vext[:, 0:hist, :] = vext[:, ts:ts + hist, :]

    for sub in range(ts // SUB_TILE):
        rows = slice(sub * SUB_TILE, (sub + 1) * SUB_TILE)
        y = jnp.concatenate([conv[cb, rows, :] for cb in range(n_chunks)], axis=1)
        mu = jnp.mean(y, axis=-1, keepdims=True)
        yc = y - mu
        var = jnp.mean(yc * yc, axis=-1, keepdims=True)
        z = yc * lax.rsqrt(var + EPS) * lng_ref[...] + lnb_ref[...]
        z = z * jax.nn.sigmoid(z)
        mix = _dot(_bf16(z), w2_ref[...]) + b2_ref[...]
        x1 = xs[rows, :] + _per_batch(mod_ref[2], SUB_TILE) * mix
        x1_ref[rows, :] = x1
        _post_mixer(x1, mod_ref, n2_ref, wrt_ref, brt_ref, hrow_ref, cls_ref, rank_ref, cnt_ref, sub)


def _conf_call(x, mod, n1, n2, w1, b1, dw, dwb, lng, lnb, w2, b2, wrt, brt, zero_rows,
               expert_weights, layer):
    bsz, seq, d = x.shape
    tm = TIME_TILE
    ts = bsz * tm
    n_steps = seq // tm
    assert zero_rows % (n_steps * SUBLANES) == 0
    cast, cast_specs, cast_shapes, cast_out_specs = _expert_weight_cast(expert_weights, layer, n_steps)
    out_shape, out_specs = _mixer_out(bsz * seq, d, ts)
    out_shape += (jax.ShapeDtypeStruct((zero_rows, LANES), jnp.uint32),) + cast_shapes
    out_specs += (pl.BlockSpec((zero_rows // n_steps, LANES), lambda i: (i, 0)),) + cast_out_specs
    hist = SUBLANES * (CONF_KERNEL - 1)
    outs = pl.pallas_call(
        functools.partial(_conf_kernel, len(cast)),
        out_shape=out_shape,
        grid=(n_steps,),
        in_specs=[
            pl.BlockSpec((bsz, tm, d), lambda i: (0, i, 0)),
            _full((MOD_ROWS, bsz, d)),
            _full((1, d)), _full((1, d)),
            _full((d, 2 * d)), _full((1, 2 * d)),
            _full((d // LANES, CONF_KERNEL + 1, LANES)), _full((d // LANES, 1, LANES)),
            _full((1, d)), _full((1, d)),
            _full((d, d)), _full((1, d)),
            _full((ROUTER_ROWS, d)), _full((ROUTER_ROWS, LANES)),
        ] + cast_specs,
        out_specs=out_specs,
        scratch_shapes=[
            pltpu.VMEM((d // LANES, bsz * PERM_PITCH, LANES), jnp.float32),
            pltpu.VMEM((ts, d), jnp.float32),
            pltpu.VMEM((d // LANES, hist + ts, LANES), jnp.float32),
            pltpu.VMEM((d // LANES, ts, LANES), jnp.float32),
        ],
        compiler_params=pltpu.CompilerParams(
            dimension_semantics=("arbitrary",), vmem_limit_bytes=VMEM_LIMIT_BYTES),
        name="conf_mixer",
    )(x, mod, n1, n2, w1, b1, dw, dwb, lng, lnb, w2, b2, wrt, brt, *cast)
    return outs[:6], _cast_results(outs[6:], expert_weights, layer)


def _lru_kernel(n_tiles, n_cast, dest_ref, dnext_ref, xprev_ref, pmod_ref, mod_ref, n1_ref, n2_ref,
                wy_ref, by_ref, wx_ref, bx_ref,
                cw_ref, cb_ref, wa_ref, ba_ref, wi_ref, bi_ref, lam_ref, wo_ref, bo_ref,
                wrt_ref, brt_ref, ys_ref, *rest):
    cast_in, rest = rest[:n_cast], rest[n_cast:]
    x1_ref, hrow_ref, cls_ref, rank_ref, cnt_ref = rest[:5]
    cast_out, (uext, hcar, ybuf, ysem) = rest[5:5 + n_cast], rest[5 + n_cast:]
    _cast_slices(cast_in, cast_out)
    i = pl.program_id(0)
    ts, d = xprev_ref.shape
    hd = d // LRU_HEADS
    hist = SUBLANES * (LRU_CONV - 1)

    @pl.when(i == 0)
    def _():
        uext[0:hist, :] = jnp.zeros((hist, d), jnp.float32)
        hcar[...] = jnp.zeros_like(hcar)
        cnt_ref[...] = jnp.zeros_like(cnt_ref)

    y_prev, retire_last = _gather_expert_tiles(ys_ref, ybuf, ysem, dest_ref, dnext_ref, i, n_tiles, ts,
                                               inline=True)
    nl = -lam_ref[...]
    softplus = jnp.maximum(nl, 0.0) + jnp.log1p(jnp.exp(-jnp.abs(nl)))
    hstate = hcar[...]
    for sub in range(ts // SUB_TILE):
        rows = slice(sub * SUB_TILE, (sub + 1) * SUB_TILE)
        x = xprev_ref[rows, :] + _per_batch(pmod_ref[5], SUB_TILE) * y_prev[rows, :]
        h = _bf16(_rms_mod(x, n1_ref[...], mod_ref[0], mod_ref[1]))
        ygate = jax.nn.gelu(_dot(h, wy_ref[...]) + by_ref[...])
        xb = _dot(h, wx_ref[...]) + bx_ref[...]
        uext[hist + sub * SUB_TILE:hist + (sub + 1) * SUB_TILE, :] = xb
        u = _causal_taps(cw_ref, cb_ref, uext, LRU_CONV, sub * SUB_TILE, SUB_TILE)

        ub = _bf16(u)
        r_parts, i_parts = [], []
        for hh in range(LRU_HEADS):
            uh = ub[:, hh * hd:(hh + 1) * hd]
            r_parts.append(_dot(uh, wa_ref[hh]))
            i_parts.append(_dot(uh, wi_ref[hh]))
        r = jax.nn.sigmoid(jnp.concatenate(r_parts, axis=1) + ba_ref[...])
        ig = jax.nn.sigmoid(jnp.concatenate(i_parts, axis=1) + bi_ref[...])

        log_a = (-LRU_C) * r * softplus
        a = jnp.exp(log_a)
        mult = jnp.sqrt(-jnp.tanh(log_a) * (1.0 + a * a))
        b = mult * (ig * u)

        outs = []
        for m in range(SUB_TILE // SUBLANES):
            sl = slice(m * SUBLANES, (m + 1) * SUBLANES)
            hstate = a[sl] * hstate + b[sl]
            outs.append(hstate)
        hs = jnp.concatenate(outs, axis=0)

        mix = _dot(_bf16(hs * ygate), wo_ref[...]) + bo_ref[...]
        x1 = x + _per_batch(mod_ref[2], SUB_TILE) * mix
        x1_ref[rows, :] = x1
        _post_mixer(x1, mod_ref, n2_ref, wrt_ref, brt_ref, hrow_ref, cls_ref, rank_ref, cnt_ref, sub)
    hcar[...] = hstate
    uext[0:hist, :] = uext[ts:ts + hist, :]
    retire_last()


def _lru_call(dest, x_prev, mod_prev, ys, mod, n1, n2, wy, by, wx, bx, cw, cb, wa, ba, wi, bi, lam,
              wo, bo, wrt, brt, expert_weights, layer):
    t, d = x_prev.shape
    n_tiles, _, ts = dest.shape
    bsz = mod.shape[1]
    hd = d // LRU_HEADS
    hist = SUBLANES * (LRU_CONV - 1)
    cast, cast_specs, cast_shapes, cast_out_specs = _expert_weight_cast(expert_weights, layer, n_tiles)
    out_shape, out_specs = _mixer_out(t, d, ts)
    out_shape += cast_shapes
    out_specs += cast_out_specs
    outs = pl.pallas_call(
        functools.partial(_lru_kernel, n_tiles, len(cast)),
        out_shape=out_shape,
        grid=(n_tiles,),
        in_specs=[
            pl.BlockSpec((1, 1, ts), lambda i: (i, 0, 0), memory_space=pltpu.SMEM),
            pl.BlockSpec((1, 1, ts), lambda i: (jnp.minimum(i + 1, n_tiles - 1), 0, 0),
                         memory_space=pltpu.SMEM),
            pl.BlockSpec((ts, d), lambda i: (i, 0)),
            _full((MOD_ROWS, bsz, d)), _full((MOD_ROWS, bsz, d)),
            _full((1, d)), _full((1, d)),
            _full((d, d)), _full((1, d)),
            _full((d, d)), _full((1, d)),
            _full((LRU_CONV, d)), _full((1, d)),
            _full((LRU_HEADS, hd, hd)), _full((1, d)),
            _full((LRU_HEADS, hd, hd)), _full((1, d)),
            _full((1, d)),
            _full((d, d)), _full((1, d)),
            _full((ROUTER_ROWS, d)), _full((ROUTER_ROWS, LANES)),
            pl.BlockSpec(memory_space=pl.ANY),
        ] + cast_specs,
        out_specs=out_specs,
        scratch_shapes=[
            pltpu.VMEM((hist + ts, d), jnp.float32),
            pltpu.VMEM((SUBLANES, d), jnp.float32),
            pltpu.VMEM((2, ts * SUBLANES, LANES), jnp.float32),
            pltpu.SemaphoreType.DMA((2,)),
        ],
        compiler_params=pltpu.CompilerParams(
            dimension_semantics=("arbitrary",), vmem_limit_bytes=VMEM_LIMIT_BYTES),
        name="lru_mixer",
    )(dest, dest, x_prev, mod_prev, mod, n1, n2, wy, by, wx, bx, cw, cb, wa, ba, wi, bi, lam, wo, bo,
      wrt, brt, ys, *cast)
    return outs[:5], _cast_results(outs[5:], expert_weights, layer)


def _plan_kernel(n_blocks, cnt_ref, clo_ref, chi_ref, cls_ref, rank_ref,
                 dest_ref, blo_ref, bhi_ref, bvalid_ref, nblk_ref, class_start):
    n_tiles, _, ts = cls_ref.shape
    class_start[...] = jnp.zeros_like(class_start)

    def per_class(c, nb_total):
        class_start[pl.ds(c, 1), :] = jnp.full((1, LANES), nb_total * ROW_BLOCK, jnp.int32)
        nb = (cnt_ref[c] + (ROW_BLOCK - 1)) // ROW_BLOCK

        def fill(k, carry):
            blo_ref[nb_total + k] = clo_ref[c]
            bhi_ref[nb_total + k] = chi_ref[c]
            bvalid_ref[nb_total + k] = jnp.minimum(cnt_ref[c] - k * ROW_BLOCK, ROW_BLOCK)
            return carry

        lax.fori_loop(0, nb, fill, 0)
        return nb_total + nb

    nb_used = lax.fori_loop(0, N_CLASSES, per_class, 0)
    nblk_ref[0] = nb_used
    last = jnp.maximum(nb_used - 1, 0)

    def tail(k, carry):
        blo_ref[k] = blo_ref[last]
        bhi_ref[k] = bhi_ref[last]
        bvalid_ref[k] = 0
        return carry

    lax.fori_loop(nb_used, n_blocks, tail, 0)

    starts = class_start[:, 0:1]
    class_id = lax.broadcasted_iota(jnp.int32, (LANES, ts), 0)

    def per_tile(t, carry):
        first = jnp.sum(jnp.where(class_id == cls_ref[t], starts, 0), axis=0, keepdims=True)
        dest_ref[t] = first + rank_ref[t]
        return carry

    lax.fori_loop(0, n_tiles, per_tile, 0)


def _plan_call(cls, rank, counts, n_blocks):
    n_tiles, _, ts = cls.shape
    smem_full = pl.BlockSpec(memory_space=pltpu.SMEM)
    vmem_full = pl.BlockSpec(memory_space=pltpu.VMEM)
    return pl.pallas_call(
        functools.partial(_plan_kernel, n_blocks),
        out_shape=(
            jax.ShapeDtypeStruct((n_tiles, 1, ts), jnp.int32),
            jax.ShapeDtypeStruct((n_blocks,), jnp.int32),
            jax.ShapeDtypeStruct((n_blocks,), jnp.int32),
            jax.ShapeDtypeStruct((n_blocks,), jnp.int32),
            jax.ShapeDtypeStruct((1,), jnp.int32),
        ),
        in_specs=[smem_full, smem_full, smem_full, vmem_full, vmem_full],
        out_specs=(vmem_full, smem_full, smem_full, smem_full, smem_full),
        scratch_shapes=[pltpu.VMEM((LANES, LANES), jnp.int32)],
        compiler_params=pltpu.CompilerParams(vmem_limit_bytes=VMEM_LIMIT_BYTES),
        name="moe_plan",
    )(counts, jnp.asarray(_CLASS_LO), jnp.asarray(_CLASS_HI), cls, rank)


def _dispatch_kernel(n_tiles, dest_ref, hrow_ref, xs_in_ref, xs_ref, stage, sem_in, sem_out):
    del xs_in_ref
    i = pl.program_id(0)
    ts = dest_ref.shape[2]
    rows = ts * SUBLANES

    def load(tile, slot):
        return pltpu.make_async_copy(hrow_ref.at[pl.ds(pl.multiple_of(tile * rows, rows), rows)],
                                     stage.at[slot], sem_in.at[slot])

    def tile_copy(slot, r, dst_row):
        return pltpu.make_async_copy(
            stage.at[slot, pl.ds(pl.multiple_of(r * SUBLANES, SUBLANES), SUBLANES)],
            xs_ref.at[pl.ds(pl.multiple_of(dst_row * SUBLANES, SUBLANES), SUBLANES)],
            sem_out.at[slot])

    def drain(slot):
        _wait_all_tiles(xs_ref, stage.at[slot], sem_out.at[slot], to_hbm=True)

    slot = i % DISPATCH_SLOTS
    nxt = (i + 1) % DISPATCH_SLOTS

    @pl.when(i == 0)
    def _():
        load(0, 0).start()

    @pl.when(i >= 2)
    def _():
        drain(nxt)

    @pl.when(i + 1 < n_tiles)
    def _():
        load(i + 1, nxt).start()

    load(i, slot).wait()

    _start_tile_dmas(lambda r: tile_copy(slot, r, dest_ref[0, 0, r]), ts, both_queues=True)

    @pl.when(i == n_tiles - 1)
    def _():
        @pl.when(i >= 1)
        def _():
            drain((i + DISPATCH_SLOTS - 1) % DISPATCH_SLOTS)
        drain(slot)


def _dispatch_call(dest, hrows, xs_init):
    n_tiles, _, ts = dest.shape
    assert xs_init.dtype == jnp.uint32 and xs_init.shape[1] == LANES
    any_spec = pl.BlockSpec(memory_space=pl.ANY)
    return pl.pallas_call(
        functools.partial(_dispatch_kernel, n_tiles),
        out_shape=jax.ShapeDtypeStruct(xs_init.shape, jnp.uint32),
        grid=(n_tiles,),
        in_specs=[
            pl.BlockSpec((1, 1, ts), lambda i: (i, 0, 0), memory_space=pltpu.SMEM),
            any_spec, any_spec,
        ],
        out_specs=any_spec,
        scratch_shapes=[
            pltpu.VMEM((DISPATCH_SLOTS, ts * SUBLANES, LANES), jnp.uint32),
            pltpu.SemaphoreType.DMA((DISPATCH_SLOTS,)),
            pltpu.SemaphoreType.DMA((DISPATCH_SLOTS,)),
        ],
        input_output_aliases={2: 0},
        compiler_params=pltpu.CompilerParams(
            dimension_semantics=("arbitrary",), vmem_limit_bytes=VMEM_LIMIT_BYTES),
        name="moe_dispatch",
    )(dest, hrows, xs_init)


def _expert_kernel(blo_ref, bhi_ref, bvalid_ref, nblk_ref, xs_ref,
                   wg_lo, wu_lo, wd_lo, wg_hi, wu_hi, wd_hi, ys_ref):
    del blo_ref, bhi_ref, nblk_ref
    valid = bvalid_ref[pl.program_id(0)]

    def experts_on(m):
        pieces = []
        for j in range(PACK_SUBLANES):
            w = _token_rows(xs_ref, j, m)
            pieces.append(_bf16(_f32_from_bits(w & jnp.uint32(0xFFFF0000))))
            pieces.append(_bf16(_f32_from_bits(lax.shift_left(w, jnp.uint32(16)))))
        xb = jnp.concatenate(pieces, axis=1)
        gates = _f32_from_bits(_token_rows(xs_ref, PACK_SUBLANES, m))
        acc = None
        for idx, (wg, wu, wd) in enumerate(((wg_lo, wu_lo, wd_lo), (wg_hi, wu_hi, wd_hi))):
            a = _dot(xb, wg[...])
            hid = (a * jax.nn.sigmoid(a)) * _dot(xb, wu[...]) * gates[:, idx:idx + 1]
            part = _dot(_bf16(hid), wd[...])
            acc = part if acc is None else acc + part
        for s in range(SUBLANES):
            ys_ref[pl.ds(s, m, stride=SUBLANES), :] = acc[:, s * LANES:(s + 1) * LANES]
        if m < ROW_BLOCK:
            ys_ref[m * SUBLANES:ROW_BLOCK * SUBLANES, :] = jnp.zeros(
                ((ROW_BLOCK - m) * SUBLANES, LANES), jnp.float32)

    for m in range(ROW_STEP, ROW_BLOCK + 1, ROW_STEP):
        pl.when((valid > m - ROW_STEP) & (valid <= m))(functools.partial(experts_on, m))

    @pl.when(valid == 0)
    def _():
        ys_ref[...] = jnp.zeros_like(ys_ref)


def _expert_call(blo, bhi, bvalid, nblk, xs, wg, wu, wd):
    n_rows = xs.shape[0] // SUBLANES
    d, de = wg.shape[1], wg.shape[2]
    assert d == SUBLANES * LANES
    n_blocks = n_rows // ROW_BLOCK

    def w_spec(shape, which):
        if which == 0:
            return pl.BlockSpec((None,) + shape, lambda b, lo, hi, bv, n: (lo[b], 0, 0))
        return pl.BlockSpec((None,) + shape, lambda b, lo, hi, bv, n: (hi[b], 0, 0))

    def x_map(b, lo, hi, bv, n):
        return (jnp.minimum(b, jnp.maximum(n[0] - 1, 0)), 0)

    grid_spec = pltpu.PrefetchScalarGridSpec(
        num_scalar_prefetch=4,
        grid=(n_blocks,),
        in_specs=[
            pl.BlockSpec((ROW_BLOCK * SUBLANES, LANES), x_map),
            w_spec((d, de), 0), w_spec((d, de), 0), w_spec((de, d), 0),
            w_spec((d, de), 1), w_spec((d, de), 1), w_spec((de, d), 1),
        ],
        out_specs=pl.BlockSpec((ROW_BLOCK * SUBLANES, LANES), lambda b, lo, hi, bv, n: (b, 0)),
    )
    return pl.pallas_call(
        _expert_kernel,
        out_shape=jax.ShapeDtypeStruct((n_rows * SUBLANES, LANES), jnp.float32),
        grid_spec=grid_spec,
        compiler_params=pltpu.CompilerParams(
            dimension_semantics=("arbitrary",), vmem_limit_bytes=VMEM_LIMIT_BYTES),
        name="moe_experts",
    )(blo, bhi, bvalid, nblk, xs, wg, wu, wd, wg, wu, wd)


def _final_kernel(n_tiles, dest_ref, dnext_ref, x1_ref, mod_ref, fmod_ref, fg_ref, ys_ref,
                  out_ref, ybuf, sem, unperm):
    i = pl.program_id(0)
    ts, d = x1_ref.shape
    bsz, tm, _ = out_ref.shape
    y, retire_last = _gather_expert_tiles(ys_ref, ybuf, sem, dest_ref, dnext_ref, i, n_tiles, ts,
                                          inline=False)
    x2 = x1_ref[...] + _per_batch(mod_ref[5], ts) * y
    res = _rms_mod(x2, fg_ref[...], fmod_ref[0], fmod_ref[1])
    for cb in range(d // LANES):
        unperm[cb] = res[:, cb * LANES:(cb + 1) * LANES]
    for cb in range(d // LANES):
        for p in range(bsz):
            out_ref[p, :, cb * LANES:(cb + 1) * LANES] = unperm.at[cb][pl.ds(p, tm, stride=SUBLANES), :]
    retire_last()


def _final_call(dest, x1, mod, ys, bsz, fmod, fin_g):
    t, d = x1.shape
    n_tiles, _, ts = dest.shape
    tm = ts // bsz
    return pl.pallas_call(
        functools.partial(_final_kernel, n_tiles),
        out_shape=jax.ShapeDtypeStruct((bsz, t // bsz, d), jnp.float32),
        grid=(n_tiles,),
        in_specs=[
            pl.BlockSpec((1, 1, ts), lambda i: (i, 0, 0), memory_space=pltpu.SMEM),
            pl.BlockSpec((1, 1, ts), lambda i: (jnp.minimum(i + 1, n_tiles - 1), 0, 0),
                         memory_space=pltpu.SMEM),
            pl.BlockSpec((ts, d), lambda i: (i, 0)),
            _full((MOD_ROWS, bsz, d)), _full((MOD_ROWS, bsz, d)), _full((1, d)),
            pl.BlockSpec(memory_space=pl.ANY),
        ],
        out_specs=pl.BlockSpec((bsz, tm, d), lambda i: (0, i, 0)),
        scratch_shapes=[pltpu.VMEM((2, ts * SUBLANES, LANES), jnp.float32),
                        pltpu.SemaphoreType.DMA((2,)),
                        pltpu.VMEM((d // LANES, ts, LANES), jnp.float32)],
        compiler_params=pltpu.CompilerParams(
            dimension_semantics=("arbitrary",), vmem_limit_bytes=VMEM_LIMIT_BYTES),
        name="moe_combine_final",
    )(dest, dest, x1, mod, fmod, fin_g, ys)


def _router_operands(wg, bg, wr, br):
    d = wg.shape[0]
    pad_g = jnp.zeros((SUBLANES - N_GROUPS, d), jnp.float32)
    pad_e = jnp.zeros((ROUTER_ROWS - SUBLANES - N_EXPERTS, d), jnp.float32)
    wrt = jnp.concatenate([wg.T, pad_g, wr.T, pad_e], axis=0)
    bcol = jnp.concatenate([bg, jnp.zeros((SUBLANES - N_GROUPS,), jnp.float32), br,
                            jnp.zeros((ROUTER_ROWS - SUBLANES - N_EXPERTS,), jnp.float32)])
    brt = jnp.broadcast_to(bcol[:, None], (ROUTER_ROWS, LANES))
    return _bf16(wrt), brt


def _mod_table(m, parts):
    bsz = m.shape[0]
    d = m.shape[1] // parts
    m = jnp.transpose(m.reshape(bsz, parts, d), (1, 0, 2))
    return jnp.concatenate([m, jnp.zeros((MOD_ROWS - parts, bsz, d), jnp.float32)], axis=0)


def _row(v):
    return v.reshape(1, -1)


def _moe_experts(hrows, cls, rank, cnt, weights_bf16, n_blocks, xs_init):
    counts = cnt[:, 0].astype(jnp.int32)
    dest, blo, bhi, bvalid, nblk = _plan_call(cls, rank, counts, n_blocks)
    xs = _dispatch_call(dest, hrows, xs_init)
    ys = _expert_call(blo, bhi, bvalid, nblk, xs, *weights_bf16)
    return xs, ys, dest


def kernel(x, c, ada_w, ada_b, norm1_g, norm2_g, cm_w1, cm_b1, cm_dw, cm_dwb, cm_ln_g, cm_ln_b, cm_w2, cm_b2, rg_wy, rg_by, rg_wx, rg_bx, rg_cw, rg_cb, rg_wa, rg_ba, rg_wi, rg_bi, rg_lambda, rg_wo, rg_bo, moe_wg, moe_bg, moe_wr, moe_br, moe_w_gate, moe_w_up, moe_w_down, fin_ada_w, fin_ada_b, fin_g):
    bsz, seq, d = x.shape
    t = bsz * seq
    depth = ada_w.shape[0]
    assert depth == 2 and bsz == SUBLANES and seq % TIME_TILE == 0
    assert d == 2 * PACK_SUBLANES * LANES

    mods = _ada_call(c, ada_w, ada_b[:, None, :])
    fmods = _ada_call(c, fin_ada_w[None], fin_ada_b[None, None, :])
    fmod = _mod_table(fmods[0], 2)
    n_steps = seq // TIME_TILE
    granule = n_steps // math.gcd(n_steps, ROW_BLOCK)
    n_blocks = -(-(-(-t // ROW_BLOCK) + N_CLASSES) // granule) * granule
    mod0, mod1 = _mod_table(mods[0], 6), _mod_table(mods[1], 6)

    wrt, brt = _router_operands(moe_wg[0], moe_bg[0], moe_wr[0], moe_br[0])
    dw = jnp.concatenate([cm_dw[0], jnp.zeros((1, d), jnp.float32)], axis=0)
    dw = jnp.transpose(dw.reshape(CONF_KERNEL + 1, d // LANES, LANES), (1, 0, 2))
    expert_weights = (moe_w_gate, moe_w_up, moe_w_down)
    (x1, hrows, cls, rank, cnt, xs_init), weights0 = _conf_call(
        x, mod0, _row(norm1_g[0]), _row(norm2_g[0]), _bf16(cm_w1[0]), _row(cm_b1[0]),
        dw, cm_dwb[0].reshape(d // LANES, 1, LANES), _row(cm_ln_g[0]), _row(cm_ln_b[0]),
        _bf16(cm_w2[0]), _row(cm_b2[0]), wrt, brt, n_blocks * ROW_BLOCK * SUBLANES,
        expert_weights, 0)
    xs, ys, dest = _moe_experts(hrows, cls, rank, cnt, weights0, n_blocks, xs_init)

    wrt, brt = _router_operands(moe_wg[1], moe_bg[1], moe_wr[1], moe_br[1])
    (x1, hrows, cls, rank, cnt), weights1 = _lru_call(
        dest, x1, mod0, ys, mod1, _row(norm1_g[1]), _row(norm2_g[1]),
        _bf16(rg_wy[0]), _row(rg_by[0]), _bf16(rg_wx[0]), _row(rg_bx[0]),
        rg_cw[0], _row(rg_cb[0]), _bf16(rg_wa[0]), _row(rg_ba[0]), _bf16(rg_wi[0]), _row(rg_bi[0]),
        _row(rg_lambda[0]), _bf16(rg_wo[0]), _row(rg_bo[0]), wrt, brt, expert_weights, 1)
    _, ys, dest = _moe_experts(hrows, cls, rank, cnt, weights1, n_blocks, xs)
    return _final_call(dest, x1, mod1, ys, bsz, fmod, _row(fin_g))
```

```python
import functools
import math

import jax
import jax.numpy as jnp
import numpy as np
from jax import lax
from jax.experimental import pallas as pl
from jax.experimental.pallas import tpu as pltpu

EPS = 1e-6
CONF_KERNEL = 31
LRU_HEADS = 4
LRU_CONV = 4
LRU_C = 8.0
N_GROUPS = 4
EXPERTS_PER_GROUP = 8
N_EXPERTS = N_GROUPS * EXPERTS_PER_GROUP
PAIRS_PER_GROUP = EXPERTS_PER_GROUP * (EXPERTS_PER_GROUP - 1) // 2
N_CLASSES = N_GROUPS * PAIRS_PER_GROUP

LANES = 128
SUBLANES = 8
BF16_SUBLANES = 16
VMEM_LIMIT_BYTES = 56 * 1024 * 1024

TOKEN_TILE = 512
TIME_TILE = TOKEN_TILE // SUBLANES
SUB_TILE = 256
ROW_BLOCK = 512
ROW_STEP = 128
PERM_PITCH = TIME_TILE + SUBLANES
CONV_ROWS = 128
ROUTER_ROWS = 64
PACK_SUBLANES = 4
MOD_ROWS = 8
DISPATCH_SLOTS = 3
DMA_UNROLL = 8

_NEG_INF = float("-inf")


def _class_tables():
    lo = np.zeros((N_CLASSES,), np.int32)
    hi = np.zeros((N_CLASSES,), np.int32)
    for g in range(N_GROUPS):
        for a in range(EXPERTS_PER_GROUP):
            for b in range(a + 1, EXPERTS_PER_GROUP):
                c = g * PAIRS_PER_GROUP + (a * (2 * EXPERTS_PER_GROUP - 1 - a)) // 2 + (b - a - 1)
                lo[c] = g * EXPERTS_PER_GROUP + a
                hi[c] = g * EXPERTS_PER_GROUP + b
    return lo, hi


_CLASS_LO, _CLASS_HI = _class_tables()


def _bf16(x):
    return x.astype(jnp.bfloat16)


def _dot(a, b):
    return jnp.dot(a, b, preferred_element_type=jnp.float32)


def _bits(x):
    return lax.bitcast_convert_type(x, jnp.uint32)


def _f32_from_bits(x):
    return lax.bitcast_convert_type(x, jnp.float32)


def _token_rows(ref, sublane, n):
    return ref[pl.ds(sublane, n, stride=SUBLANES), :]


def _per_batch(v, ts):
    return jnp.tile(v, (ts // SUBLANES, 1))


def _ada_kernel(c_ref, w_ref, b_ref, o_ref):
    c = c_ref[...]
    ca = _bf16(c * jax.nn.sigmoid(c))
    o_ref[...] = _dot(ca, _bf16(w_ref[...])) + b_ref[...]


def _ada_call(c, w, b):
    n_layers, d, n = w.shape
    bsz = c.shape[0]
    nb = 1024
    return pl.pallas_call(
        _ada_kernel,
        out_shape=jax.ShapeDtypeStruct((n_layers, bsz, n), jnp.float32),
        grid=(n_layers, n // nb),
        in_specs=[
            pl.BlockSpec((bsz, d), lambda l, j: (0, 0)),
            pl.BlockSpec((None, d, nb), lambda l, j: (l, 0, j)),
            pl.BlockSpec((None, 1, nb), lambda l, j: (l, 0, j)),
        ],
        out_specs=pl.BlockSpec((None, bsz, nb), lambda l, j: (l, 0, j)),
        compiler_params=pltpu.CompilerParams(
            dimension_semantics=("arbitrary", "arbitrary"),
            vmem_limit_bytes=VMEM_LIMIT_BYTES),
        name="ada_mod",
    )(c, w, b)


def _rms_mod(x, gain, shift, scale):
    ts = x.shape[0]
    ms = jnp.mean(x * x, axis=-1, keepdims=True)
    return (x * lax.rsqrt(ms + EPS)) * _per_batch(gain * (1.0 + scale), ts) + _per_batch(shift, ts)


def _causal_taps(w_ref, bias_ref, ext_ref, n_taps, row0, n):
    d = ext_ref.shape[1]
    cols = []
    for cb in range(d // LANES):
        sl = slice(cb * LANES, (cb + 1) * LANES)
        acc = jnp.broadcast_to(bias_ref[:, sl], (n, LANES))
        for k in range(n_taps):
            lo = row0 + SUBLANES * k
            acc = acc + w_ref[k:k + 1, sl] * ext_ref[lo:lo + n, sl]
        cols.append(acc)
    return jnp.concatenate(cols, axis=1)


def _causal_taps_unit(w_ref, bias_ref, ext_ref, out_ref, cb, r0, n_taps):
    w_cb, ext_cb = w_ref.at[cb], ext_ref.at[cb]
    acc = jnp.broadcast_to(bias_ref[cb], (CONV_ROWS, LANES))
    for k in range(n_taps):
        win = ext_cb[pl.ds(pl.multiple_of(r0 + SUBLANES * k, SUBLANES), CONV_ROWS), :]
        acc = acc + w_cb[k:k + 1, :] * win
    out_ref.at[cb][pl.ds(r0, CONV_ROWS), :] = acc


def _start_tile_dmas(make_copy, n, inline=False, both_queues=False):
    if inline:
        for r in range(n):
            make_copy(r).start()
        return

    def body(g, carry):
        for k in range(DMA_UNROLL):
            make_copy(g * DMA_UNROLL + k).start(priority=k % 2 if both_queues else 0)
        return carry
    lax.fori_loop(0, n // DMA_UNROLL, body, 0)


def _wait_all_tiles(hbm_ref, vmem_tiles_ref, sem, to_hbm):
    hbm_view = hbm_ref.at[pl.ds(0, vmem_tiles_ref.shape[0])]
    if to_hbm:
        pltpu.make_async_copy(vmem_tiles_ref, hbm_view, sem).wait()
    else:
        pltpu.make_async_copy(hbm_view, vmem_tiles_ref, sem).wait()


def _gather_expert_tiles(ys_ref, ybuf, sem, dest_ref, dnext_ref, step, n_steps, ts, inline):
    slot = step % 2

    def tile_copy(dref, sl):
        def make(r):
            src = pl.multiple_of(dref[0, 0, r] * SUBLANES, SUBLANES)
            return pltpu.make_async_copy(
                ys_ref.at[pl.ds(src, SUBLANES)],
                ybuf.at[sl, pl.ds(pl.multiple_of(r * SUBLANES, SUBLANES), SUBLANES)],
                sem.at[sl])
        return make

    @pl.when(step == 0)
    def _():
        _start_tile_dmas(tile_copy(dest_ref, 0), ts)

    if not inline:
        @pl.when(step + 1 < n_steps)
        def _():
            _start_tile_dmas(tile_copy(dnext_ref, 1 - slot), ts)

    _wait_all_tiles(ys_ref, ybuf.at[slot], sem.at[slot], to_hbm=False)
    if inline:
        _start_tile_dmas(tile_copy(dnext_ref, 1 - slot), ts, inline=True)
    yv = ybuf.at[slot]
    y = jnp.concatenate([_token_rows(yv, s, ts) for s in range(SUBLANES)], axis=1)

    def retire_last():
        if inline:
            @pl.when(step + 1 == n_steps)
            def _():
                _wait_all_tiles(ys_ref, ybuf.at[1 - slot], sem.at[1 - slot], to_hbm=False)

    return y, retire_last


def _route(logits_t):
    ts = logits_t.shape[1]
    row = lax.broadcasted_iota(jnp.int32, (SUBLANES, ts), 0)
    gl = jnp.where(row < N_GROUPS, logits_t[0:SUBLANES], _NEG_INF)
    gmax = jnp.max(gl, axis=0, keepdims=True)
    gsel = jnp.min(jnp.where(gl == gmax, row, SUBLANES), axis=0, keepdims=True)
    p_g = 1.0 / jnp.sum(jnp.exp(gl - gmax), axis=0, keepdims=True)
    es = jnp.zeros((SUBLANES, ts), jnp.float32)
    for g in range(N_GROUPS):
        lo = SUBLANES * (1 + g)
        es = jnp.where(gsel == g, logits_t[lo:lo + EXPERTS_PER_GROUP], es)
    m1 = jnp.max(es, axis=0, keepdims=True)
    i1 = jnp.min(jnp.where(es == m1, row, SUBLANES), axis=0, keepdims=True)
    es2 = jnp.where(row == i1, _NEG_INF, es)
    m2 = jnp.max(es2, axis=0, keepdims=True)
    i2 = jnp.min(jnp.where(es2 == m2, row, SUBLANES), axis=0, keepdims=True)
    t = jnp.exp(m2 - m1)
    w1 = p_g / (1.0 + t)
    w2 = w1 * t
    first_lo = i1 < i2
    e_lo = jnp.minimum(i1, i2)
    e_hi = jnp.maximum(i1, i2)
    gate_lo = jnp.where(first_lo, w1, w2)
    gate_hi = jnp.where(first_lo, w2, w1)
    pair = lax.shift_right_logical(e_lo * (2 * EXPERTS_PER_GROUP - 1 - e_lo), 1) + (e_hi - e_lo - 1)
    cls = gsel * PAIRS_PER_GROUP + pair
    return cls, gate_lo, gate_hi


def _post_mixer(x1, mod_ref, n2_ref, wrt_ref, brt_ref, hrow_ref, cls_ref, rank_ref, cnt_ref, sub):
    ts = x1.shape[0]
    hrow_ref = hrow_ref.at[pl.ds(sub * ts * SUBLANES, ts * SUBLANES)]
    cols = slice(sub * ts, (sub + 1) * ts)
    h2 = _rms_mod(x1, n2_ref[...], mod_ref[3], mod_ref[4])
    h2b = _bf16(h2)
    logits_t = lax.dot_general(wrt_ref[...], h2b, (((1,), (1,)), ((), ())),
                               preferred_element_type=jnp.float32) + brt_ref[:, 0:1]
    cls, gate_lo, gate_hi = _route(logits_t)
    gates_t = jnp.concatenate(
        [gate_lo, gate_hi, jnp.zeros((LANES - 2, ts), jnp.float32)], axis=0)
    h2r = h2b.astype(jnp.float32)
    for j in range(PACK_SUBLANES):
        hi = _bits(h2r[:, (2 * j) * LANES:(2 * j + 1) * LANES])
        lo = _bits(h2r[:, (2 * j + 1) * LANES:(2 * j + 2) * LANES])
        hrow_ref[pl.ds(j, ts, stride=SUBLANES), :] = hi | lax.shift_right_logical(lo, jnp.uint32(16))
    hrow_ref[pl.ds(PACK_SUBLANES, ts, stride=SUBLANES), :] = _bits(gates_t.T)
    for j in range(PACK_SUBLANES + 1, SUBLANES):
        hrow_ref[pl.ds(j, ts, stride=SUBLANES), :] = jnp.zeros((ts, LANES), jnp.uint32)
    cls_ref[0, :, cols] = cls
    crow = lax.broadcasted_iota(jnp.int32, (LANES, ts), 0)
    onehot = jnp.where(crow == cls, 1.0, 0.0)
    cnt = jnp.sum(onehot, axis=1, keepdims=True)
    earlier = (lax.broadcasted_iota(jnp.int32, (ts, ts), 0)
               < lax.broadcasted_iota(jnp.int32, (ts, ts), 1))
    prefix = _dot(_bf16(onehot), _bf16(jnp.where(earlier, 1.0, 0.0)))
    rank = jnp.sum(onehot * (prefix + cnt_ref[:, 0:1]), axis=0, keepdims=True)
    rank_ref[0, :, cols] = rank.astype(jnp.int32)
    cnt_ref[...] += jnp.broadcast_to(cnt, cnt_ref.shape)


def _mixer_out(t, d, ts):
    n_tiles = t // ts
    out_shape = (
        jax.ShapeDtypeStruct((t, d), jnp.float32),
        jax.ShapeDtypeStruct((t * SUBLANES, LANES), jnp.uint32),
        jax.ShapeDtypeStruct((n_tiles, 1, ts), jnp.int32),
        jax.ShapeDtypeStruct((n_tiles, 1, ts), jnp.int32),
        jax.ShapeDtypeStruct((LANES, LANES), jnp.float32),
    )
    out_specs = (
        pl.BlockSpec((ts, d), lambda i: (i, 0)),
        pl.BlockSpec((ts * SUBLANES, LANES), lambda i: (i, 0)),
        pl.BlockSpec((1, 1, ts), lambda i: (i, 0, 0)),
        pl.BlockSpec((1, 1, ts), lambda i: (i, 0, 0)),
        pl.BlockSpec((LANES, LANES), lambda i: (0, 0)),
    )
    return out_shape, out_specs


def _full(shape):
    return pl.BlockSpec(shape, lambda i: tuple(0 for _ in shape))


def _expert_weight_cast(weights, layer, n_steps):
    flat = [w.reshape(w.shape[0], w.shape[1] * w.shape[2], w.shape[3]) for w in weights]
    if any(w.shape[1] % (n_steps * BF16_SUBLANES) for w in flat):
        return [], [], (), ()
    in_specs = [pl.BlockSpec((None, w.shape[1] // n_steps, w.shape[2]), lambda i: (layer, i, 0))
                for w in flat]
    out_shapes = tuple(jax.ShapeDtypeStruct(w.shape[1:], jnp.bfloat16) for w in flat)
    out_specs = tuple(pl.BlockSpec((w.shape[1] // n_steps, w.shape[2]), lambda i: (i, 0))
                      for w in flat)
    return flat, in_specs, out_shapes, out_specs


def _cast_slices(cast_in, cast_out):
    for src, dst in zip(cast_in, cast_out):
        dst[...] = _bf16(src[...])


def _cast_results(outs, weights, layer):
    if outs:
        return [o.reshape(w.shape[1:]) for o, w in zip(outs, weights)]
    return [_bf16(w[layer]) for w in weights]


def _to_tile_order(x_ref, perm, xs):
    bsz, tm, d = x_ref.shape
    for cb in range(d // LANES):
        for p in range(bsz):
            perm[cb, PERM_PITCH * p:PERM_PITCH * p + tm, :] = x_ref[p, :, cb * LANES:(cb + 1) * LANES]
    for m in range(tm):
        for cb in range(d // LANES):
            xs[m * bsz:(m + 1) * bsz, cb * LANES:(cb + 1) * LANES] = (
                perm.at[cb][pl.ds(m, bsz, stride=PERM_PITCH), :])


def _conf_kernel(n_cast, x_ref, mod_ref, n1_ref, n2_ref, w1_ref, b1_ref, dw_ref, dwb_ref,
                 lng_ref, lnb_ref, w2_ref, b2_ref, wrt_ref, brt_ref, *rest):
    cast_in, rest = rest[:n_cast], rest[n_cast:]
    x1_ref, hrow_ref, cls_ref, rank_ref, cnt_ref, zeros_ref = rest[:6]
    cast_out, (perm, xs, vext, conv) = rest[6:6 + n_cast], rest[6 + n_cast:]
    _cast_slices(cast_in, cast_out)
    i = pl.program_id(0)
    bsz, tm, d = x_ref.shape
    ts = bsz * tm
    n_chunks = d // LANES
    hist = SUBLANES * (CONF_KERNEL - 1)

    @pl.when(i == 0)
    def _():
        vext[:, 0:hist, :] = jnp.zeros((n_chunks, hist, LANES), jnp.float32)
        cnt_ref[...] = jnp.zeros_like(cnt_ref)

    zeros_ref[...] = jnp.zeros_like(zeros_ref)

    _to_tile_order(x_ref, perm, xs)
    for sub in range(ts // SUB_TILE):
        rows = slice(sub * SUB_TILE, (sub + 1) * SUB_TILE)
        h = _bf16(_rms_mod(xs[rows, :], n1_ref[...], mod_ref[0], mod_ref[1]))
        u = _dot(h, w1_ref[...]) + b1_ref[...]
        v = u[:, 0:d] * jax.nn.sigmoid(u[:, d:2 * d])
        for cb in range(n_chunks):
            vext[cb, hist + sub * SUB_TILE:hist + (sub + 1) * SUB_TILE, :] = v[:, cb * LANES:(cb + 1) * LANES]

    units_per_chunk = ts // CONV_ROWS

    def conv_unit(idx, carry):
        r0 = pl.multiple_of((idx % units_per_chunk) * CONV_ROWS, CONV_ROWS)
        _causal_taps_unit(dw_ref, dwb_ref, vext, conv, idx // units_per_chunk, r0, CONF_KERNEL)
        return carry

    lax.fori_loop(0, n_chunks * units_per_chunk, conv_unit, 0)
    vext[:, 0:hist, :] = vext[:, ts:ts + hist, :]

    for sub in range(ts // SUB_TILE):
        rows = slice(sub * SUB_TILE, (sub + 1) * SUB_TILE)
        y = jnp.concatenate([conv[cb, rows, :] for cb in range(n_chunks)], axis=1)
        mu = jnp.mean(y, axis=-1, keepdims=True)
        yc = y - mu
        var = jnp.mean(yc * yc, axis=-1, keepdims=True)
        z = yc * lax.rsqrt(var + EPS) * lng_ref[...] + lnb_ref[...]
        z = z * jax.nn.sigmoid(z)
        mix = _dot(_bf16(z), w2_ref[...]) + b2_ref[...]
        x1 = xs[rows, :] + _per_batch(mod_ref[2], SUB_TILE) * mix
        x1_ref[rows, :] = x1
        _post_mixer(x1, mod_ref, n2_ref, wrt_ref, brt_ref, hrow_ref, cls_ref, rank_ref, cnt_ref, sub)


def _conf_call(x, mod, n1, n2, w1, b1, dw, dwb, lng, lnb, w2, b2, wrt, brt, zero_rows,
               expert_weights, layer):
    bsz, seq, d = x.shape
    tm = TIME_TILE
    ts = bsz * tm
    n_steps = seq // tm
    assert zero_rows % (n_steps * SUBLANES) == 0
    cast, cast_specs, cast_shapes, cast_out_specs = _expert_weight_cast(expert_weights, layer, n_steps)
    out_shape, out_specs = _mixer_out(bsz * seq, d, ts)
    out_shape += (jax.ShapeDtypeStruct((zero_rows, LANES), jnp.uint32),) + cast_shapes
    out_specs += (pl.BlockSpec((zero_rows // n_steps, LANES), lambda i: (i, 0)),) + cast_out_specs
    hist = SUBLANES * (CONF_KERNEL - 1)
    outs = pl.pallas_call(
        functools.partial(_conf_kernel, len(cast)),
        out_shape=out_shape,
        grid=(n_steps,),
        in_specs=[
            pl.BlockSpec((bsz, tm, d), lambda i: (0, i, 0)),
            _full((MOD_ROWS, bsz, d)),
            _full((1, d)), _full((1, d)),
            _full((d, 2 * d)), _full((1, 2 * d)),
            _full((d // LANES, CONF_KERNEL + 1, LANES)), _full((d // LANES, 1, LANES)),
            _full((1, d)), _full((1, d)),
            _full((d, d)), _full((1, d)),
            _full((ROUTER_ROWS, d)), _full((ROUTER_ROWS, LANES)),
        ] + cast_specs,
        out_specs=out_specs,
        scratch_shapes=[
            pltpu.VMEM((d // LANES, bsz * PERM_PITCH, LANES), jnp.float32),
            pltpu.VMEM((ts, d), jnp.float32),
            pltpu.VMEM((d // LANES, hist + ts, LANES), jnp.float32),
            pltpu.VMEM((d // LANES, ts, LANES), jnp.float32),
        ],
        compiler_params=pltpu.CompilerParams(
            dimension_semantics=("arbitrary",), vmem_limit_bytes=VMEM_LIMIT_BYTES),
        name="conf_mixer",
    )(x, mod, n1, n2, w1, b1, dw, dwb, lng, lnb, w2, b2, wrt, brt, *cast)
    return outs[:6], _cast_results(outs[6:], expert_weights, layer)


def _lru_kernel(n_tiles, n_cast, dest_ref, dnext_ref, xprev_ref, pmod_ref, mod_ref, n1_ref, n2_ref,
                wy_ref, by_ref, wx_ref, bx_ref,
                cw_ref, cb_ref, wa_ref, ba_ref, wi_ref, bi_ref, lam_ref, wo_ref, bo_ref,
                wrt_ref, brt_ref, ys_ref, *rest):
    cast_in, rest = rest[:n_cast], rest[n_cast:]
    x1_ref, hrow_ref, cls_ref, rank_ref, cnt_ref = rest[:5]
    cast_out, (uext, hcar, ybuf, ysem) = rest[5:5 + n_cast], rest[5 + n_cast:]
    _cast_slices(cast_in, cast_out)
    i = pl.program_id(0)
    ts, d = xprev_ref.shape
    hd = d // LRU_HEADS
    hist = SUBLANES * (LRU_CONV - 1)

    @pl.when(i == 0)
    def _():
        uext[0:hist, :] = jnp.zeros((hist, d), jnp.float32)
        hcar[...] = jnp.zeros_like(hcar)
        cnt_ref[...] = jnp.zeros_like(cnt_ref)

    y_prev, retire_last = _gather_expert_tiles(ys_ref, ybuf, ysem, dest_ref, dnext_ref, i, n_tiles, ts,
                                               inline=True)
    nl = -lam_ref[...]
    softplus = jnp.maximum(nl, 0.0) + jnp.log1p(jnp.exp(-jnp.abs(nl)))
    hstate = hcar[...]
    for sub in range(ts // SUB_TILE):
        rows = slice(sub * SUB_TILE, (sub + 1) * SUB_TILE)
        x = xprev_ref[rows, :] + _per_batch(pmod_ref[5], SUB_TILE) * y_prev[rows, :]
        h = _bf16(_rms_mod(x, n1_ref[...], mod_ref[0], mod_ref[1]))
        ygate = jax.nn.gelu(_dot(h, wy_ref[...]) + by_ref[...])
        xb = _dot(h, wx_ref[...]) + bx_ref[...]
        uext[hist + sub * SUB_TILE:hist + (sub + 1) * SUB_TILE, :] = xb
        u = _causal_taps(cw_ref, cb_ref, uext, LRU_CONV, sub * SUB_TILE, SUB_TILE)

        ub = _bf16(u)
        r_parts, i_parts = [], []
        for hh in range(LRU_HEADS):
            uh = ub[:, hh * hd:(hh + 1) * hd]
            r_parts.append(_dot(uh, wa_ref[hh]))
            i_parts.append(_dot(uh, wi_ref[hh]))
        r = jax.nn.sigmoid(jnp.concatenate(r_parts, axis=1) + ba_ref[...])
        ig = jax.nn.sigmoid(jnp.concatenate(i_parts, axis=1) + bi_ref[...])

        log_a = (-LRU_C) * r * softplus
        a = jnp.exp(log_a)
        mult = jnp.sqrt(-jnp.tanh(log_a) * (1.0 + a * a))
        b = mult * (ig * u)

        outs = []
        for m in range(SUB_TILE // SUBLANES):
            sl = slice(m * SUBLANES, (m + 1) * SUBLANES)
            hstate = a[sl] * hstate + b[sl]
            outs.append(hstate)
        hs = jnp.concatenate(outs, axis=0)

        mix = _dot(_bf16(hs * ygate), wo_ref[...]) + bo_ref[...]
        x1 = x + _per_batch(mod_ref[2], SUB_TILE) * mix
        x1_ref[rows, :] = x1
        _post_mixer(x1, mod_ref, n2_ref, wrt_ref, brt_ref, hrow_ref, cls_ref, rank_ref, cnt_ref, sub)
    hcar[...] = hstate
    uext[0:hist, :] = uext[ts:ts + hist, :]
    retire_last()


def _lru_call(dest, x_prev, mod_prev, ys, mod, n1, n2, wy, by, wx, bx, cw, cb, wa, ba, wi, bi, lam,
              wo, bo, wrt, brt, expert_weights, layer):
    t, d = x_prev.shape
    n_tiles, _, ts = dest.shape
    bsz = mod.shape[1]
    hd = d // LRU_HEADS
    hist = SUBLANES * (LRU_CONV - 1)
    cast, cast_specs, cast_shapes, cast_out_specs = _expert_weight_cast(expert_weights, layer, n_tiles)
    out_shape, out_specs = _mixer_out(t, d, ts)
    out_shape += cast_shapes
    out_specs += cast_out_specs
    outs = pl.pallas_call(
        functools.partial(_lru_kernel, n_tiles, len(cast)),
        out_shape=out_shape,
        grid=(n_tiles,),
        in_specs=[
            pl.BlockSpec((1, 1, ts), lambda i: (i, 0, 0), memory_space=pltpu.SMEM),
            pl.BlockSpec((1, 1, ts), lambda i: (jnp.minimum(i + 1, n_tiles - 1), 0, 0),
                         memory_space=pltpu.SMEM),
            pl.BlockSpec((ts, d), lambda i: (i, 0)),
            _full((MOD_ROWS, bsz, d)), _full((MOD_ROWS, bsz, d)),
            _full((1, d)), _full((1, d)),
            _full((d, d)), _full((1, d)),
            _full((d, d)), _full((1, d)),
            _full((LRU_CONV, d)), _full((1, d)),
            _full((LRU_HEADS, hd, hd)), _full((1, d)),
            _full((LRU_HEADS, hd, hd)), _full((1, d)),
            _full((1, d)),
            _full((d, d)), _full((1, d)),
            _full((ROUTER_ROWS, d)), _full((ROUTER_ROWS, LANES)),
            pl.BlockSpec(memory_space=pl.ANY),
        ] + cast_specs,
        out_specs=out_specs,
        scratch_shapes=[
            pltpu.VMEM((hist + ts, d), jnp.float32),
            pltpu.VMEM((SUBLANES, d), jnp.float32),
            pltpu.VMEM((2, ts * SUBLANES, LANES), jnp.float32),
            pltpu.SemaphoreType.DMA((2,)),
        ],
        compiler_params=pltpu.CompilerParams(
            dimension_semantics=("arbitrary",), vmem_limit_bytes=VMEM_LIMIT_BYTES),
        name="lru_mixer",
    )(dest, dest, x_prev, mod_prev, mod, n1, n2, wy, by, wx, bx, cw, cb, wa, ba, wi, bi, lam, wo, bo,
      wrt, brt, ys, *cast)
    return outs[:5], _cast_results(outs[5:], expert_weights, layer)


def _plan_kernel(n_blocks, cnt_ref, clo_ref, chi_ref, cls_ref, rank_ref,
                 dest_ref, blo_ref, bhi_ref, bvalid_ref, nblk_ref, class_start):
    n_tiles, _, ts = cls_ref.shape
    class_start[...] = jnp.zeros_like(class_start)

    def per_class(c, nb_total):
        class_start[pl.ds(c, 1), :] = jnp.full((1, LANES), nb_total * ROW_BLOCK, jnp.int32)
        nb = (cnt_ref[c] + (ROW_BLOCK - 1)) // ROW_BLOCK

        def fill(k, carry):
            blo_ref[nb_total + k] = clo_ref[c]
            bhi_ref[nb_total + k] = chi_ref[c]
            bvalid_ref[nb_total + k] = jnp.minimum(cnt_ref[c] - k * ROW_BLOCK, ROW_BLOCK)
            return carry

        lax.fori_loop(0, nb, fill, 0)
        return nb_total + nb

    nb_used = lax.fori_loop(0, N_CLASSES, per_class, 0)
    nblk_ref[0] = nb_used
    last = jnp.maximum(nb_used - 1, 0)

    def tail(k, carry):
        blo_ref[k] = blo_ref[last]
        bhi_ref[k] = bhi_ref[last]
        bvalid_ref[k] = 0
        return carry

    lax.fori_loop(nb_used, n_blocks, tail, 0)

    starts = class_start[:, 0:1]
    class_id = lax.broadcasted_iota(jnp.int32, (LANES, ts), 0)

    def per_tile(t, carry):
        first = jnp.sum(jnp.where(class_id == cls_ref[t], starts, 0), axis=0, keepdims=True)
        dest_ref[t] = first + rank_ref[t]
        return carry

    lax.fori_loop(0, n_tiles, per_tile, 0)


def _plan_call(cls, rank, counts, n_blocks):
    n_tiles, _, ts = cls.shape
    smem_full = pl.BlockSpec(memory_space=pltpu.SMEM)
    vmem_full = pl.BlockSpec(memory_space=pltpu.VMEM)
    return pl.pallas_call(
        functools.partial(_plan_kernel, n_blocks),
        out_shape=(
            jax.ShapeDtypeStruct((n_tiles, 1, ts), jnp.int32),
            jax.ShapeDtypeStruct((n_blocks,), jnp.int32),
            jax.ShapeDtypeStruct((n_blocks,), jnp.int32),
            jax.ShapeDtypeStruct((n_blocks,), jnp.int32),
            jax.ShapeDtypeStruct((1,), jnp.int32),
        ),
        in_specs=[smem_full, smem_full, smem_full, vmem_full, vmem_full],
        out_specs=(vmem_full, smem_full, smem_full, smem_full, smem_full),
        scratch_shapes=[pltpu.VMEM((LANES, LANES), jnp.int32)],
        compiler_params=pltpu.CompilerParams(vmem_limit_bytes=VMEM_LIMIT_BYTES),
        name="moe_plan",
    )(counts, jnp.asarray(_CLASS_LO), jnp.asarray(_CLASS_HI), cls, rank)


def _dispatch_kernel(n_tiles, dest_ref, hrow_ref, xs_in_ref, xs_ref, stage, sem_in, sem_out):
    del xs_in_ref
    i = pl.program_id(0)
    ts = dest_ref.shape[2]
    rows = ts * SUBLANES

    def load(tile, slot):
        return pltpu.make_async_copy(hrow_ref.at[pl.ds(pl.multiple_of(tile * rows, rows), rows)],
                                     stage.at[slot], sem_in.at[slot])

    def tile_copy(slot, r, dst_row):
        return pltpu.make_async_copy(
            stage.at[slot, pl.ds(pl.multiple_of(r * SUBLANES, SUBLANES), SUBLANES)],
            xs_ref.at[pl.ds(pl.multiple_of(dst_row * SUBLANES, SUBLANES), SUBLANES)],
            sem_out.at[slot])

    def drain(slot):
        _wait_all_tiles(xs_ref, stage.at[slot], sem_out.at[slot], to_hbm=True)

    slot = i % DISPATCH_SLOTS
    nxt = (i + 1) % DISPATCH_SLOTS

    @pl.when(i == 0)
    def _():
        load(0, 0).start()

    @pl.when(i >= 2)
    def _():
        drain(nxt)

    @pl.when(i + 1 < n_tiles)
    def _():
        load(i + 1, nxt).start()

    load(i, slot).wait()

    _start_tile_dmas(lambda r: tile_copy(slot, r, dest_ref[0, 0, r]), ts, both_queues=True)

    @pl.when(i == n_tiles - 1)
    def _():
        @pl.when(i >= 1)
        def _():
            drain((i + DISPATCH_SLOTS - 1) % DISPATCH_SLOTS)
        drain(slot)


def _dispatch_call(dest, hrows, xs_init):
    n_tiles, _, ts = dest.shape
    assert xs_init.dtype == jnp.uint32 and xs_init.shape[1] == LANES
    any_spec = pl.BlockSpec(memory_space=pl.ANY)
    return pl.pallas_call(
        functools.partial(_dispatch_kernel, n_tiles),
        out_shape=jax.ShapeDtypeStruct(xs_init.shape, jnp.uint32),
        grid=(n_tiles,),
        in_specs=[
            pl.BlockSpec((1, 1, ts), lambda i: (i, 0, 0), memory_space=pltpu.SMEM),
            any_spec, any_spec,
        ],
        out_specs=any_spec,
        scratch_shapes=[
            pltpu.VMEM((DISPATCH_SLOTS, ts * SUBLANES, LANES), jnp.uint32),
            pltpu.SemaphoreType.DMA((DISPATCH_SLOTS,)),
            pltpu.SemaphoreType.DMA((DISPATCH_SLOTS,)),
        ],
        input_output_aliases={2: 0},
        compiler_params=pltpu.CompilerParams(
            dimension_semantics=("arbitrary",), vmem_limit_bytes=VMEM_LIMIT_BYTES),
        name="moe_dispatch",
    )(dest, hrows, xs_init)


def _expert_kernel(blo_ref, bhi_ref, bvalid_ref, nblk_ref, xs_ref,
                   wg_lo, wu_lo, wd_lo, wg_hi, wu_hi, wd_hi, ys_ref):
    del blo_ref, bhi_ref, nblk_ref
    valid = bvalid_ref[pl.program_id(0)]

    def experts_on(m):
        pieces = []
        for j in range(PACK_SUBLANES):
            w = _token_rows(xs_ref, j, m)
            pieces.append(_bf16(_f32_from_bits(w & jnp.uint32(0xFFFF0000))))
            pieces.append(_bf16(_f32_from_bits(lax.shift_left(w, jnp.uint32(16)))))
        xb = jnp.concatenate(pieces, axis=1)
        gates = _f32_from_bits(_token_rows(xs_ref, PACK_SUBLANES, m))
        acc = None
        for idx, (wg, wu, wd) in enumerate(((wg_lo, wu_lo, wd_lo), (wg_hi, wu_hi, wd_hi))):
            a = _dot(xb, wg[...])
            hid = (a * jax.nn.sigmoid(a)) * _dot(xb, wu[...]) * gates[:, idx:idx + 1]
            part = _dot(_bf16(hid), wd[...])
            acc = part if acc is None else acc + part
        for s in range(SUBLANES):
            ys_ref[pl.ds(s, m, stride=SUBLANES), :] = acc[:, s * LANES:(s + 1) * LANES]
        if m < ROW_BLOCK:
            ys_ref[m * SUBLANES:ROW_BLOCK * SUBLANES, :] = jnp.zeros(
                ((ROW_BLOCK - m) * SUBLANES, LANES), jnp.float32)

    for m in range(ROW_STEP, ROW_BLOCK + 1, ROW_STEP):
        pl.when((valid > m - ROW_STEP) & (valid <= m))(functools.partial(experts_on, m))


def _expert_call(blo, bhi, bvalid, nblk, xs, wg, wu, wd):
    n_rows = xs.shape[0] // SUBLANES
    d, de = wg.shape[1], wg.shape[2]
    assert d == SUBLANES * LANES
    n_blocks = n_rows // ROW_BLOCK

    def w_spec(shape, which):
        if which == 0:
            return pl.BlockSpec((None,) + shape, lambda b, lo, hi, bv, n: (lo[b], 0, 0))
        return pl.BlockSpec((None,) + shape, lambda b, lo, hi, bv, n: (hi[b], 0, 0))

    def row_map(b, lo, hi, bv, n):
        return (jnp.minimum(b, jnp.maximum(n[0] - 1, 0)), 0)

    grid_spec = pltpu.PrefetchScalarGridSpec(
        num_scalar_prefetch=4,
        grid=(n_blocks,),
        in_specs=[
            pl.BlockSpec((ROW_BLOCK * SUBLANES, LANES), row_map),
            w_spec((d, de), 0), w_spec((d, de), 0), w_spec((de, d), 0),
            w_spec((d, de), 1), w_spec((d, de), 1), w_spec((de, d), 1),
        ],
        out_specs=pl.BlockSpec((ROW_BLOCK * SUBLANES, LANES), row_map),
    )
    return pl.pallas_call(
        _expert_kernel,
        out_shape=jax.ShapeDtypeStruct((n_rows * SUBLANES, LANES), jnp.float32),
        grid_spec=grid_spec,
        compiler_params=pltpu.CompilerParams(
            dimension_semantics=("arbitrary",), vmem_limit_bytes=VMEM_LIMIT_BYTES),
        name="moe_experts",
    )(blo, bhi, bvalid, nblk, xs, wg, wu, wd, wg, wu, wd)


def _final_kernel(n_tiles, dest_ref, dnext_ref, x1_ref, mod_ref, fmod_ref, fg_ref, ys_ref,
                  out_ref, ybuf, sem, unperm):
    i = pl.program_id(0)
    ts, d = x1_ref.shape
    bsz, tm, _ = out_ref.shape
    y, retire_last = _gather_expert_tiles(ys_ref, ybuf, sem, dest_ref, dnext_ref, i, n_tiles, ts,
                                          inline=False)
    x2 = x1_ref[...] + _per_batch(mod_ref[5], ts) * y
    res = _rms_mod(x2, fg_ref[...], fmod_ref[0], fmod_ref[1])
    for cb in range(d // LANES):
        unperm[cb] = res[:, cb * LANES:(cb + 1) * LANES]
    for cb in range(d // LANES):
        for p in range(bsz):
            out_ref[p, :, cb * LANES:(cb + 1) * LANES] = unperm.at[cb][pl.ds(p, tm, stride=SUBLANES), :]
    retire_last()


def _final_call(dest, x1, mod, ys, bsz, fmod, fin_g):
    t, d = x1.shape
    n_tiles, _, ts = dest.shape
    tm = ts // bsz
    return pl.pallas_call(
        functools.partial(_final_kernel, n_tiles),
        out_shape=jax.ShapeDtypeStruct((bsz, t // bsz, d), jnp.float32),
        grid=(n_tiles,),
        in_specs=[
            pl.BlockSpec((1, 1, ts), lambda i: (i, 0, 0), memory_space=pltpu.SMEM),
            pl.BlockSpec((1, 1, ts), lambda i: (jnp.minimum(i + 1, n_tiles - 1), 0, 0),
                         memory_space=pltpu.SMEM),
            pl.BlockSpec((ts, d), lambda i: (i, 0)),
            _full((MOD_ROWS, bsz, d)), _full((MOD_ROWS, bsz, d)), _full((1, d)),
            pl.BlockSpec(memory_space=pl.ANY),
        ],
        out_specs=pl.BlockSpec((bsz, tm, d), lambda i: (0, i, 0)),
        scratch_shapes=[pltpu.VMEM((2, ts * SUBLANES, LANES), jnp.float32),
                        pltpu.SemaphoreType.DMA((2,)),
                        pltpu.VMEM((d // LANES, ts, LANES), jnp.float32)],
        compiler_params=pltpu.CompilerParams(
            dimension_semantics=("arbitrary",), vmem_limit_bytes=VMEM_LIMIT_BYTES),
        name="moe_combine_final",
    )(dest, dest, x1, mod, fmod, fin_g, ys)


def _router_operands(wg, bg, wr, br):
    d = wg.shape[0]
    pad_g = jnp.zeros((SUBLANES - N_GROUPS, d), jnp.float32)
    pad_e = jnp.zeros((ROUTER_ROWS - SUBLANES - N_EXPERTS, d), jnp.float32)
    wrt = jnp.concatenate([wg.T, pad_g, wr.T, pad_e], axis=0)
    bcol = jnp.concatenate([bg, jnp.zeros((SUBLANES - N_GROUPS,), jnp.float32), br,
                            jnp.zeros((ROUTER_ROWS - SUBLANES - N_EXPERTS,), jnp.float32)])
    brt = jnp.broadcast_to(bcol[:, None], (ROUTER_ROWS, LANES))
    return _bf16(wrt), brt


def _mod_table(m, parts):
    bsz = m.shape[0]
    d = m.shape[1] // parts
    m = jnp.transpose(m.reshape(bsz, parts, d), (1, 0, 2))
    return jnp.concatenate([m, jnp.zeros((MOD_ROWS - parts, bsz, d), jnp.float32)], axis=0)


def _row(v):
    return v.reshape(1, -1)


def _moe_experts(hrows, cls, rank, cnt, weights_bf16, n_blocks, xs_init):
    counts = cnt[:, 0].astype(jnp.int32)
    dest, blo, bhi, bvalid, nblk = _plan_call(cls, rank, counts, n_blocks)
    xs = _dispatch_call(dest, hrows, xs_init)
    ys = _expert_call(blo, bhi, bvalid, nblk, xs, *weights_bf16)
    return xs, ys, dest


def kernel(x, c, ada_w, ada_b, norm1_g, norm2_g, cm_w1, cm_b1, cm_dw, cm_dwb, cm_ln_g, cm_ln_b, cm_w2, cm_b2, rg_wy, rg_by, rg_wx, rg_bx, rg_cw, rg_cb, rg_wa, rg_ba, rg_wi, rg_bi, rg_lambda, rg_wo, rg_bo, moe_wg, moe_bg, moe_wr, moe_br, moe_w_gate, moe_w_up, moe_w_down, fin_ada_w, fin_ada_b, fin_g):
    bsz, seq, d = x.shape
    t = bsz * seq
    depth = ada_w.shape[0]
    assert depth == 2 and bsz == SUBLANES and seq % TIME_TILE == 0
    assert d == 2 * PACK_SUBLANES * LANES

    mods = _ada_call(c, ada_w, ada_b[:, None, :])
    fmods = _ada_call(c, fin_ada_w[None], fin_ada_b[None, None, :])
    fmod = _mod_table(fmods[0], 2)
    n_steps = seq // TIME_TILE
    granule = n_steps // math.gcd(n_steps, ROW_BLOCK)
    n_blocks = -(-(-(-t // ROW_BLOCK) + N_CLASSES) // granule) * granule
    mod0, mod1 = _mod_table(mods[0], 6), _mod_table(mods[1], 6)

    wrt, brt = _router_operands(moe_wg[0], moe_bg[0], moe_wr[0], moe_br[0])
    dw = jnp.concatenate([cm_dw[0], jnp.zeros((1, d), jnp.float32)], axis=0)
    dw = jnp.transpose(dw.reshape(CONF_KERNEL + 1, d // LANES, LANES), (1, 0, 2))
    expert_weights = (moe_w_gate, moe_w_up, moe_w_down)
    (x1, hrows, cls, rank, cnt, xs_init), weights0 = _conf_call(
        x, mod0, _row(norm1_g[0]), _row(norm2_g[0]), _bf16(cm_w1[0]), _row(cm_b1[0]),
        dw, cm_dwb[0].reshape(d // LANES, 1, LANES), _row(cm_ln_g[0]), _row(cm_ln_b[0]),
        _bf16(cm_w2[0]), _row(cm_b2[0]), wrt, brt, n_blocks * ROW_BLOCK * SUBLANES,
        expert_weights, 0)
    xs, ys, dest = _moe_experts(hrows, cls, rank, cnt, weights0, n_blocks, xs_init)

    wrt, brt = _router_operands(moe_wg[1], moe_bg[1], moe_wr[1], moe_br[1])
    (x1, hrows, cls, rank, cnt), weights1 = _lru_call(
        dest, x1, mod0, ys, mod1, _row(norm1_g[1]), _row(norm2_g[1]),
        _bf16(rg_wy[0]), _row(rg_by[0]), _bf16(rg_wx[0]), _row(rg_bx[0]),
        rg_cw[0], _row(rg_cb[0]), _bf16(rg_wa[0]), _row(rg_ba[0]), _bf16(rg_wi[0]), _row(rg_bi[0]),
        _row(rg_lambda[0]), _bf16(rg_wo[0]), _row(rg_bo[0]), wrt, brt, expert_weights, 1)
    _, ys, dest = _moe_experts(hrows, cls, rank, cnt, weights1, n_blocks, xs)
    return _final_call(dest, x1, mod1, ys, bsz, fmod, _row(fin_g))
```

```python
import functools
import math

import jax
import jax.numpy as jnp
import numpy as np
from jax import lax
from jax.experimental import pallas as pl
from jax.experimental.pallas import tpu as pltpu

EPS = 1e-6
CONF_KERNEL = 31
LRU_HEADS = 4
LRU_CONV = 4
LRU_C = 8.0
N_GROUPS = 4
EXPERTS_PER_GROUP = 8
N_EXPERTS = N_GROUPS * EXPERTS_PER_GROUP
PAIRS_PER_GROUP = EXPERTS_PER_GROUP * (EXPERTS_PER_GROUP - 1) // 2
N_CLASSES = N_GROUPS * PAIRS_PER_GROUP

LANES = 128
SUBLANES = 8
BF16_SUBLANES = 16
VMEM_LIMIT_BYTES = 56 * 1024 * 1024

TOKEN_TILE = 512
TIME_TILE = TOKEN_TILE // SUBLANES
SUB_TILE = 256
ROW_BLOCK = 512
ROW_STEP = 128
PERM_PITCH = TIME_TILE + SUBLANES
CONV_ROWS = 128
ROUTER_ROWS = 64
PACK_SUBLANES = 4
MOD_ROWS = 8
DISPATCH_SLOTS = 3
DMA_UNROLL = 8

_NEG_INF = float("-inf")


def _class_tables():
    lo = np.zeros((N_CLASSES,), np.int32)
    hi = np.zeros((N_CLASSES,), np.int32)
    for g in range(N_GROUPS):
        for a in range(EXPERTS_PER_GROUP):
            for b in range(a + 1, EXPERTS_PER_GROUP):
                c = g * PAIRS_PER_GROUP + (a * (2 * EXPERTS_PER_GROUP - 1 - a)) // 2 + (b - a - 1)
                lo[c] = g * EXPERTS_PER_GROUP + a
                hi[c] = g * EXPERTS_PER_GROUP + b
    return lo, hi


_CLASS_LO, _CLASS_HI = _class_tables()


def _bf16(x):
    return x.astype(jnp.bfloat16)


def _dot(a, b):
    return jnp.dot(a, b, preferred_element_type=jnp.float32)


def _bits(x):
    return lax.bitcast_convert_type(x, jnp.uint32)


def _f32_from_bits(x):
    return lax.bitcast_convert_type(x, jnp.float32)


def _token_rows(ref, sublane, n):
    return ref[pl.ds(sublane, n, stride=SUBLANES), :]


def _per_batch(v, ts):
    return jnp.tile(v, (ts // SUBLANES, 1))


def _ada_kernel(c_ref, w_ref, b_ref, o_ref):
    c = c_ref[...]
    ca = _bf16(c * jax.nn.sigmoid(c))
    o_ref[...] = _dot(ca, _bf16(w_ref[...])) + b_ref[...]


def _ada_call(c, w, b):
    n_layers, d, n = w.shape
    bsz = c.shape[0]
    nb = 1024
    return pl.pallas_call(
        _ada_kernel,
        out_shape=jax.ShapeDtypeStruct((n_layers, bsz, n), jnp.float32),
        grid=(n_layers, n // nb),
        in_specs=[
            pl.BlockSpec((bsz, d), lambda l, j: (0, 0)),
            pl.BlockSpec((None, d, nb), lambda l, j: (l, 0, j)),
            pl.BlockSpec((None, 1, nb), lambda l, j: (l, 0, j)),
        ],
        out_specs=pl.BlockSpec((None, bsz, nb), lambda l, j: (l, 0, j)),
        compiler_params=pltpu.CompilerParams(
            dimension_semantics=("arbitrary", "arbitrary"),
            vmem_limit_bytes=VMEM_LIMIT_BYTES),
        name="ada_mod",
    )(c, w, b)


def _rms_mod(x, gain, shift, scale):
    ts = x.shape[0]
    ms = jnp.mean(x * x, axis=-1, keepdims=True)
    return (x * lax.rsqrt(ms + EPS)) * _per_batch(gain * (1.0 + scale), ts) + _per_batch(shift, ts)


def _causal_taps(w_ref, bias_ref, ext_ref, n_taps, row0, n):
    d = ext_ref.shape[1]
    cols = []
    for cb in range(d // LANES):
        sl = slice(cb * LANES, (cb + 1) * LANES)
        acc = jnp.broadcast_to(bias_ref[:, sl], (n, LANES))
        for k in range(n_taps):
            lo = row0 + SUBLANES * k
            acc = acc + w_ref[k:k + 1, sl] * ext_ref[lo:lo + n, sl]
        cols.append(acc)
    return jnp.concatenate(cols, axis=1)


def _causal_taps_unit(w_ref, bias_ref, ext_ref, out_ref, cb, r0, n_taps):
    w_cb, ext_cb = w_ref.at[cb], ext_ref.at[cb]
    acc = jnp.broadcast_to(bias_ref[cb], (CONV_ROWS, LANES))
    for k in range(n_taps):
        win = ext_cb[pl.ds(pl.multiple_of(r0 + SUBLANES * k, SUBLANES), CONV_ROWS), :]
        acc = acc + w_cb[k:k + 1, :] * win
    out_ref.at[cb][pl.ds(r0, CONV_ROWS), :] = acc


def _start_tile_dmas(make_copy, n, inline=False, both_queues=False):
    if inline:
        for r in range(n):
            make_copy(r).start()
        return

    def body(g, carry):
        for k in range(DMA_UNROLL):
            make_copy(g * DMA_UNROLL + k).start(priority=k % 2 if both_queues else 0)
        return carry
    lax.fori_loop(0, n // DMA_UNROLL, body, 0)


def _wait_all_tiles(hbm_ref, vmem_tiles_ref, sem, to_hbm):
    hbm_view = hbm_ref.at[pl.ds(0, vmem_tiles_ref.shape[0])]
    if to_hbm:
        pltpu.make_async_copy(vmem_tiles_ref, hbm_view, sem).wait()
    else:
        pltpu.make_async_copy(hbm_view, vmem_tiles_ref, sem).wait()


def _gather_expert_tiles(ys_ref, ybuf, sem, dest_ref, dnext_ref, step, n_steps, ts, inline):
    slot = step % 2

    def tile_copy(dref, sl):
        def make(r):
            src = pl.multiple_of(dref[0, 0, r] * SUBLANES, SUBLANES)
            return pltpu.make_async_copy(
                ys_ref.at[pl.ds(src, SUBLANES)],
                ybuf.at[sl, pl.ds(pl.multiple_of(r * SUBLANES, SUBLANES), SUBLANES)],
                sem.at[sl])
        return make

    @pl.when(step == 0)
    def _():
        _start_tile_dmas(tile_copy(dest_ref, 0), ts)

    if not inline:
        @pl.when(step + 1 < n_steps)
        def _():
            _start_tile_dmas(tile_copy(dnext_ref, 1 - slot), ts)

    _wait_all_tiles(ys_ref, ybuf.at[slot], sem.at[slot], to_hbm=False)
    if inline:
        _start_tile_dmas(tile_copy(dnext_ref, 1 - slot), ts, inline=True)
    yv = ybuf.at[slot]
    y = jnp.concatenate([_token_rows(yv, s, ts) for s in range(SUBLANES)], axis=1)

    def retire_last():
        if inline:
            @pl.when(step + 1 == n_steps)
            def _():
                _wait_all_tiles(ys_ref, ybuf.at[1 - slot], sem.at[1 - slot], to_hbm=False)

    return y, retire_last


def _route(logits_t):
    ts = logits_t.shape[1]
    row = lax.broadcasted_iota(jnp.int32, (SUBLANES, ts), 0)
    gl = jnp.where(row < N_GROUPS, logits_t[0:SUBLANES], _NEG_INF)
    gmax = jnp.max(gl, axis=0, keepdims=True)
    gsel = jnp.min(jnp.where(gl == gmax, row, SUBLANES), axis=0, keepdims=True)
    p_g = 1.0 / jnp.sum(jnp.exp(gl - gmax), axis=0, keepdims=True)
    es = jnp.zeros((SUBLANES, ts), jnp.float32)
    for g in range(N_GROUPS):
        lo = SUBLANES * (1 + g)
        es = jnp.where(gsel == g, logits_t[lo:lo + EXPERTS_PER_GROUP], es)
    m1 = jnp.max(es, axis=0, keepdims=True)
    i1 = jnp.min(jnp.where(es == m1, row, SUBLANES), axis=0, keepdims=True)
    es2 = jnp.where(row == i1, _NEG_INF, es)
    m2 = jnp.max(es2, axis=0, keepdims=True)
    i2 = jnp.min(jnp.where(es2 == m2, row, SUBLANES), axis=0, keepdims=True)
    t = jnp.exp(m2 - m1)
    w1 = p_g / (1.0 + t)
    w2 = w1 * t
    first_lo = i1 < i2
    e_lo = jnp.minimum(i1, i2)
    e_hi = jnp.maximum(i1, i2)
    gate_lo = jnp.where(first_lo, w1, w2)
    gate_hi = jnp.where(first_lo, w2, w1)
    pair = lax.shift_right_logical(e_lo * (2 * EXPERTS_PER_GROUP - 1 - e_lo), 1) + (e_hi - e_lo - 1)
    cls = gsel * PAIRS_PER_GROUP + pair
    return cls, gate_lo, gate_hi


def _post_mixer(x1, mod_ref, n2_ref, wrt_ref, brt_ref, hrow_ref, cls_ref, rank_ref, cnt_ref, sub):
    ts = x1.shape[0]
    hrow_ref = hrow_ref.at[pl.ds(sub * ts * SUBLANES, ts * SUBLANES)]
    cols = slice(sub * ts, (sub + 1) * ts)
    h2 = _rms_mod(x1, n2_ref[...], mod_ref[3], mod_ref[4])
    h2b = _bf16(h2)
    logits_t = lax.dot_general(wrt_ref[...], h2b, (((1,), (1,)), ((), ())),
                               preferred_element_type=jnp.float32) + brt_ref[:, 0:1]
    cls, gate_lo, gate_hi = _route(logits_t)
    gates_t = jnp.concatenate(
        [gate_lo, gate_hi, jnp.zeros((LANES - 2, ts), jnp.float32)], axis=0)
    h2r = h2b.astype(jnp.float32)
    for j in range(PACK_SUBLANES):
        hi = _bits(h2r[:, (2 * j) * LANES:(2 * j + 1) * LANES])
        lo = _bits(h2r[:, (2 * j + 1) * LANES:(2 * j + 2) * LANES])
        hrow_ref[pl.ds(j, ts, stride=SUBLANES), :] = hi | lax.shift_right_logical(lo, jnp.uint32(16))
    hrow_ref[pl.ds(PACK_SUBLANES, ts, stride=SUBLANES), :] = _bits(gates_t.T)
    for j in range(PACK_SUBLANES + 1, SUBLANES):
        hrow_ref[pl.ds(j, ts, stride=SUBLANES), :] = jnp.zeros((ts, LANES), jnp.uint32)
    cls_ref[0, :, cols] = cls
    crow = lax.broadcasted_iota(jnp.int32, (LANES, ts), 0)
    onehot = jnp.where(crow == cls, 1.0, 0.0)
    cnt = jnp.sum(onehot, axis=1, keepdims=True)
    earlier = (lax.broadcasted_iota(jnp.int32, (ts, ts), 0)
               < lax.broadcasted_iota(jnp.int32, (ts, ts), 1))
    prefix = _dot(_bf16(onehot), _bf16(jnp.where(earlier, 1.0, 0.0)))
    rank = jnp.sum(onehot * (prefix + cnt_ref[:, 0:1]), axis=0, keepdims=True)
    rank_ref[0, :, cols] = rank.astype(jnp.int32)
    cnt_ref[...] += jnp.broadcast_to(cnt, cnt_ref.shape)


def _mixer_out(t, d, ts):
    n_tiles = t // ts
    out_shape = (
        jax.ShapeDtypeStruct((t, d), jnp.float32),
        jax.ShapeDtypeStruct((t * SUBLANES, LANES), jnp.uint32),
        jax.ShapeDtypeStruct((n_tiles, 1, ts), jnp.int32),
        jax.ShapeDtypeStruct((n_tiles, 1, ts), jnp.int32),
        jax.ShapeDtypeStruct((LANES, LANES), jnp.float32),
    )
    out_specs = (
        pl.BlockSpec((ts, d), lambda i: (i, 0)),
        pl.BlockSpec((ts * SUBLANES, LANES), lambda i: (i, 0)),
        pl.BlockSpec((1, 1, ts), lambda i: (i, 0, 0)),
        pl.BlockSpec((1, 1, ts), lambda i: (i, 0, 0)),
        pl.BlockSpec((LANES, LANES), lambda i: (0, 0)),
    )
    return out_shape, out_specs


def _full(shape):
    return pl.BlockSpec(shape, lambda i: tuple(0 for _ in shape))


def _expert_weight_cast(weights, layer, n_steps):
    flat = [w.reshape(w.shape[0], w.shape[1] * w.shape[2], w.shape[3]) for w in weights]
    if any(w.shape[1] % (n_steps * BF16_SUBLANES) for w in flat):
        return [], [], (), ()
    in_specs = [pl.BlockSpec((None, w.shape[1] // n_steps, w.shape[2]), lambda i: (layer, i, 0))
                for w in flat]
    out_shapes = tuple(jax.ShapeDtypeStruct(w.shape[1:], jnp.bfloat16) for w in flat)
    out_specs = tuple(pl.BlockSpec((w.shape[1] // n_steps, w.shape[2]), lambda i: (i, 0))
                      for w in flat)
    return flat, in_specs, out_shapes, out_specs


def _cast_slices(cast_in, cast_out):
    for src, dst in zip(cast_in, cast_out):
        dst[...] = _bf16(src[...])


def _cast_results(outs, weights, layer):
    if outs:
        return [o.reshape(w.shape[1:]) for o, w in zip(outs, weights)]
    return [_bf16(w[layer]) for w in weights]


def _to_tile_order(x_ref, perm, xs):
    bsz, tm, d = x_ref.shape
    for cb in range(d // LANES):
        for p in range(bsz):
            perm[cb, PERM_PITCH * p:PERM_PITCH * p + tm, :] = x_ref[p, :, cb * LANES:(cb + 1) * LANES]
    for m in range(tm):
        for cb in range(d // LANES):
            xs[m * bsz:(m + 1) * bsz, cb * LANES:(cb + 1) * LANES] = (
                perm.at[cb][pl.ds(m, bsz, stride=PERM_PITCH), :])


def _conf_kernel(n_cast, x_ref, mod_ref, n1_ref, n2_ref, w1_ref, b1_ref, dw_ref, dwb_ref,
                 lng_ref, lnb_ref, w2_ref, b2_ref, wrt_ref, brt_ref, *rest):
    cast_in, rest = rest[:n_cast], rest[n_cast:]
    x1_ref, hrow_ref, cls_ref, rank_ref, cnt_ref, zeros_ref = rest[:6]
    cast_out, (perm, xs, vext, conv) = rest[6:6 + n_cast], rest[6 + n_cast:]
    _cast_slices(cast_in, cast_out)
    i = pl.program_id(0)
    bsz, tm, d = x_ref.shape
    ts = bsz * tm
    n_chunks = d // LANES
    hist = SUBLANES * (CONF_KERNEL - 1)

    @pl.when(i == 0)
    def _():
        vext[:, 0:hist, :] = jnp.zeros((n_chunks, hist, LANES), jnp.float32)
        cnt_ref[...] = jnp.zeros_like(cnt_ref)

    zeros_ref[...] = jnp.zeros_like(zeros_ref)

    _to_tile_order(x_ref, perm, xs)
    for sub in range(ts // SUB_TILE):
        rows = slice(sub * SUB_TILE, (sub + 1) * SUB_TILE)
        h = _bf16(_rms_mod(xs[rows, :], n1_ref[...], mod_ref[0], mod_ref[1]))
        u = _dot(h, w1_ref[...]) + b1_ref[...]
        v = u[:, 0:d] * jax.nn.sigmoid(u[:, d:2 * d])
        for cb in range(n_chunks):
            vext[cb, hist + sub * SUB_TILE:hist + (sub + 1) * SUB_TILE, :] = v[:, cb * LANES:(cb + 1) * LANES]

    units_per_chunk = ts // CONV_ROWS

    def conv_unit(idx, carry):
        r0 = pl.multiple_of((idx % units_per_chunk) * CONV_ROWS, CONV_ROWS)
        _causal_taps_unit(dw_ref, dwb_ref, vext, conv, idx // units_per_chunk, r0, CONF_KERNEL)
        return carry

    lax.fori_loop(0, n_chunks * units_per_chunk, conv_unit, 0)
    vext[:, 0:hist, :] = vext[:, ts:ts + hist, :]

    for sub in range(ts // SUB_TILE):
        rows = slice(sub * SUB_TILE, (sub + 1) * SUB_TILE)
        y = jnp.concatenate([conv[cb, rows, :] for cb in range(n_chunks)], axis=1)
        mu = jnp.mean(y, axis=-1, keepdims=True)
        yc = y - mu
        var = jnp.mean(yc * yc, axis=-1, keepdims=True)
        z = yc * lax.rsqrt(var + EPS) * lng_ref[...] + lnb_ref[...]
        z = z * jax.nn.sigmoid(z)
        mix = _dot(_bf16(z), w2_ref[...]) + b2_ref[...]
        x1 = xs[rows, :] + _per_batch(mod_ref[2], SUB_TILE) * mix
        x1_ref[rows, :] = x1
        _post_mixer(x1, mod_ref, n2_ref, wrt_ref, brt_ref, hrow_ref, cls_ref, rank_ref, cnt_ref, sub)


def _conf_call(x, mod, n1, n2, w1, b1, dw, dwb, lng, lnb, w2, b2, wrt, brt, zero_rows,
               expert_weights, layer):
    bsz, seq, d = x.shape
    tm = TIME_TILE
    ts = bsz * tm
    n_steps = seq // tm
    assert zero_rows % (n_steps * SUBLANES) == 0
    cast, cast_specs, cast_shapes, cast_out_specs = _expert_weight_cast(expert_weights, layer, n_steps)
    out_shape, out_specs = _mixer_out(bsz * seq, d, ts)
    out_shape += (jax.ShapeDtypeStruct((zero_rows, LANES), jnp.uint32),) + cast_shapes
    out_specs += (pl.BlockSpec((zero_rows // n_steps, LANES), lambda i: (i, 0)),) + cast_out_specs
    hist = SUBLANES * (CONF_KERNEL - 1)
    outs = pl.pallas_call(
        functools.partial(_conf_kernel, len(cast)),
        out_shape=out_shape,
        grid=(n_steps,),
        in_specs=[
            pl.BlockSpec((bsz, tm, d), lambda i: (0, i, 0)),
            _full((MOD_ROWS, bsz, d)),
            _full((1, d)), _full((1, d)),
            _full((d, 2 * d)), _full((1, 2 * d)),
            _full((d // LANES, CONF_KERNEL + 1, LANES)), _full((d // LANES, 1, LANES)),
            _full((1, d)), _full((1, d)),
            _full((d, d)), _full((1, d)),
            _full((ROUTER_ROWS, d)), _full((ROUTER_ROWS, LANES)),
        ] + cast_specs,
        out_specs=out_specs,
        scratch_shapes=[
            pltpu.VMEM((d // LANES, bsz * PERM_PITCH, LANES), jnp.float32),
            pltpu.VMEM((ts, d), jnp.float32),
            pltpu.VMEM((d // LANES, hist + ts, LANES), jnp.float32),
            pltpu.VMEM((d // LANES, ts, LANES), jnp.float32),
        ],
        compiler_params=pltpu.CompilerParams(
            dimension_semantics=("arbitrary",), vmem_limit_bytes=VMEM_LIMIT_BYTES),
        name="conf_mixer",
    )(x, mod, n1, n2, w1, b1, dw, dwb, lng, lnb, w2, b2, wrt, brt, *cast)
    return outs[:6], _cast_results(outs[6:], expert_weights, layer)


def _lru_kernel(n_tiles, n_cast, dest_ref, dnext_ref, xprev_ref, pmod_ref, mod_ref, n1_ref, n2_ref,
                wy_ref, by_ref, wx_ref, bx_ref,
                cw_ref, cb_ref, wa_ref, ba_ref, wi_ref, bi_ref, lam_ref, wo_ref, bo_ref,
                wrt_ref, brt_ref, ys_ref, *rest):
    cast_in, rest = rest[:n_cast], rest[n_cast:]
    x1_ref, hrow_ref, cls_ref, rank_ref, cnt_ref = rest[:5]
    cast_out, (uext, hcar, ybuf, ysem) = rest[5:5 + n_cast], rest[5 + n_cast:]
    _cast_slices(cast_in, cast_out)
    i = pl.program_id(0)
    ts, d = xprev_ref.shape
    hd = d // LRU_HEADS
    hist = SUBLANES * (LRU_CONV - 1)

    @pl.when(i == 0)
    def _():
        uext[0:hist, :] = jnp.zeros((hist, d), jnp.float32)
        hcar[...] = jnp.zeros_like(hcar)
        cnt_ref[...] = jnp.zeros_like(cnt_ref)

    y_prev, retire_last = _gather_expert_tiles(ys_ref, ybuf, ysem, dest_ref, dnext_ref, i, n_tiles, ts,
                                               inline=True)
    nl = -lam_ref[...]
    softplus = jnp.maximum(nl, 0.0) + jnp.log1p(jnp.exp(-jnp.abs(nl)))
    hstate = hcar[...]
    for sub in range(ts // SUB_TILE):
        rows = slice(sub * SUB_TILE, (sub + 1) * SUB_TILE)
        x = xprev_ref[rows, :] + _per_batch(pmod_ref[5], SUB_TILE) * y_prev[rows, :]
        h = _bf16(_rms_mod(x, n1_ref[...], mod_ref[0], mod_ref[1]))
        ygate = jax.nn.gelu(_dot(h, wy_ref[...]) + by_ref[...])
        xb = _dot(h, wx_ref[...]) + bx_ref[...]
        uext[hist + sub * SUB_TILE:hist + (sub + 1) * SUB_TILE, :] = xb
        u = _causal_taps(cw_ref, cb_ref, uext, LRU_CONV, sub * SUB_TILE, SUB_TILE)

        ub = _bf16(u)
        r_parts, i_parts = [], []
        for hh in range(LRU_HEADS):
            uh = ub[:, hh * hd:(hh + 1) * hd]
            r_parts.append(_dot(uh, wa_ref[hh]))
            i_parts.append(_dot(uh, wi_ref[hh]))
        r = jax.nn.sigmoid(jnp.concatenate(r_parts, axis=1) + ba_ref[...])
        ig = jax.nn.sigmoid(jnp.concatenate(i_parts, axis=1) + bi_ref[...])

        log_a = (-LRU_C) * r * softplus
        a = jnp.exp(log_a)
        mult = jnp.sqrt(-jnp.tanh(log_a) * (1.0 + a * a))
        b = mult * (ig * u)

        outs = []
        for m in range(SUB_TILE // SUBLANES):
            sl = slice(m * SUBLANES, (m + 1) * SUBLANES)
            hstate = a[sl] * hstate + b[sl]
            outs.append(hstate)
        hs = jnp.concatenate(outs, axis=0)

        mix = _dot(_bf16(hs * ygate), wo_ref[...]) + bo_ref[...]
        x1 = x + _per_batch(mod_ref[2], SUB_TILE) * mix
        x1_ref[rows, :] = x1
        _post_mixer(x1, mod_ref, n2_ref, wrt_ref, brt_ref, hrow_ref, cls_ref, rank_ref, cnt_ref, sub)
    hcar[...] = hstate
    uext[0:hist, :] = uext[ts:ts + hist, :]
    retire_last()


def _lru_call(dest, x_prev, mod_prev, ys, mod, n1, n2, wy, by, wx, bx, cw, cb, wa, ba, wi, bi, lam,
              wo, bo, wrt, brt, expert_weights, layer):
    t, d = x_prev.shape
    n_tiles, _, ts = dest.shape
    bsz = mod.shape[1]
    hd = d // LRU_HEADS
    hist = SUBLANES * (LRU_CONV - 1)
    cast, cast_specs, cast_shapes, cast_out_specs = _expert_weight_cast(expert_weights, layer, n_tiles)
    out_shape, out_specs = _mixer_out(t, d, ts)
    out_shape += cast_shapes
    out_specs += cast_out_specs
    outs = pl.pallas_call(
        functools.partial(_lru_kernel, n_tiles, len(cast)),
        out_shape=out_shape,
        grid=(n_tiles,),
        in_specs=[
            pl.BlockSpec((1, 1, ts), lambda i: (i, 0, 0), memory_space=pltpu.SMEM),
            pl.BlockSpec((1, 1, ts), lambda i: (jnp.minimum(i + 1, n_tiles - 1), 0, 0),
                         memory_space=pltpu.SMEM),
            pl.BlockSpec((ts, d), lambda i: (i, 0)),
            _full((MOD_ROWS, bsz, d)), _full((MOD_ROWS, bsz, d)),
            _full((1, d)), _full((1, d)),
            _full((d, d)), _full((1, d)),
            _full((d, d)), _full((1, d)),
            _full((LRU_CONV, d)), _full((1, d)),
            _full((LRU_HEADS, hd, hd)), _full((1, d)),
            _full((LRU_HEADS, hd, hd)), _full((1, d)),
            _full((1, d)),
            _full((d, d)), _full((1, d)),
            _full((ROUTER_ROWS, d)), _full((ROUTER_ROWS, LANES)),
            pl.BlockSpec(memory_space=pl.ANY),
        ] + cast_specs,
        out_specs=out_specs,
        scratch_shapes=[
            pltpu.VMEM((hist + ts, d), jnp.float32),
            pltpu.VMEM((SUBLANES, d), jnp.float32),
            pltpu.VMEM((2, ts * SUBLANES, LANES), jnp.float32),
            pltpu.SemaphoreType.DMA((2,)),
        ],
        compiler_params=pltpu.CompilerParams(
            dimension_semantics=("arbitrary",), vmem_limit_bytes=VMEM_LIMIT_BYTES),
        name="lru_mixer",
    )(dest, dest, x_prev, mod_prev, mod, n1, n2, wy, by, wx, bx, cw, cb, wa, ba, wi, bi, lam, wo, bo,
      wrt, brt, ys, *cast)
    return outs[:5], _cast_results(outs[5:], expert_weights, layer)


def _plan_kernel(n_blocks, cnt_ref, clo_ref, chi_ref, cls_ref, rank_ref,
                 dest_ref, blo_ref, bhi_ref, bvalid_ref, nblk_ref, class_start):
    n_tiles, _, ts = cls_ref.shape
    class_start[...] = jnp.zeros_like(class_start)

    def per_class(c, nb_total):
        class_start[pl.ds(c, 1), :] = jnp.full((1, LANES), nb_total * ROW_BLOCK, jnp.int32)
        nb = (cnt_ref[c] + (ROW_BLOCK - 1)) // ROW_BLOCK

        def fill(k, carry):
            blo_ref[nb_total + k] = clo_ref[c]
            bhi_ref[nb_total + k] = chi_ref[c]
            bvalid_ref[nb_total + k] = jnp.minimum(cnt_ref[c] - k * ROW_BLOCK, ROW_BLOCK)
            return carry

        lax.fori_loop(0, nb, fill, 0)
        return nb_total + nb

    nb_used = lax.fori_loop(0, N_CLASSES, per_class, 0)
    nblk_ref[0] = nb_used
    last = jnp.maximum(nb_used - 1, 0)

    def tail(k, carry):
        blo_ref[k] = blo_ref[last]
        bhi_ref[k] = bhi_ref[last]
        bvalid_ref[k] = 0
        return carry

    lax.fori_loop(nb_used, n_blocks, tail, 0)

    starts = class_start[:, 0:1]
    class_id = lax.broadcasted_iota(jnp.int32, (LANES, ts), 0)

    def per_tile(t, carry):
        first = jnp.sum(jnp.where(class_id == cls_ref[t], starts, 0), axis=0, keepdims=True)
        dest_ref[t] = first + rank_ref[t]
        return carry

    lax.fori_loop(0, n_tiles, per_tile, 0)


def _plan_call(cls, rank, counts, n_blocks):
    n_tiles, _, ts = cls.shape
    smem_full = pl.BlockSpec(memory_space=pltpu.SMEM)
    vmem_full = pl.BlockSpec(memory_space=pltpu.VMEM)
    return pl.pallas_call(
        functools.partial(_plan_kernel, n_blocks),
        out_shape=(
            jax.ShapeDtypeStruct((n_tiles, 1, ts), jnp.int32),
            jax.ShapeDtypeStruct((n_blocks,), jnp.int32),
            jax.ShapeDtypeStruct((n_blocks,), jnp.int32),
            jax.ShapeDtypeStruct((n_blocks,), jnp.int32),
            jax.ShapeDtypeStruct((1,), jnp.int32),
        ),
        in_specs=[smem_full, smem_full, smem_full, vmem_full, vmem_full],
        out_specs=(vmem_full, smem_full, smem_full, smem_full, smem_full),
        scratch_shapes=[pltpu.VMEM((LANES, LANES), jnp.int32)],
        compiler_params=pltpu.CompilerParams(vmem_limit_bytes=VMEM_LIMIT_BYTES),
        name="moe_plan",
    )(counts, jnp.asarray(_CLASS_LO), jnp.asarray(_CLASS_HI), cls, rank)


def _dispatch_kernel(n_tiles, dest_ref, hrow_ref, xs_in_ref, xs_ref, stage, sem_in, sem_out):
    del xs_in_ref
    i = pl.program_id(0)
    ts = dest_ref.shape[2]
    rows = ts * SUBLANES

    def load(tile, slot):
        return pltpu.make_async_copy(hrow_ref.at[pl.ds(pl.multiple_of(tile * rows, rows), rows)],
                                     stage.at[slot], sem_in.at[slot])

    def tile_copy(slot, r, dst_row):
        return pltpu.make_async_copy(
            stage.at[slot, pl.ds(pl.multiple_of(r * SUBLANES, SUBLANES), SUBLANES)],
            xs_ref.at[pl.ds(pl.multiple_of(dst_row * SUBLANES, SUBLANES), SUBLANES)],
            sem_out.at[slot])

    def drain(slot):
        _wait_all_tiles(xs_ref, stage.at[slot], sem_out.at[slot], to_hbm=True)

    slot = i % DISPATCH_SLOTS
    nxt = (i + 1) % DISPATCH_SLOTS

    @pl.when(i == 0)
    def _():
        load(0, 0).start()

    @pl.when(i >= 2)
    def _():
        drain(nxt)

    @pl.when(i + 1 < n_tiles)
    def _():
        load(i + 1, nxt).start()

    load(i, slot).wait()

    _start_tile_dmas(lambda r: tile_copy(slot, r, dest_ref[0, 0, r]), ts, both_queues=True)

    @pl.when(i == n_tiles - 1)
    def _():
        @pl.when(i >= 1)
        def _():
            drain((i + DISPATCH_SLOTS - 1) % DISPATCH_SLOTS)
        drain(slot)


def _dispatch_call(dest, hrows, xs_init):
    n_tiles, _, ts = dest.shape
    assert xs_init.dtype == jnp.uint32 and xs_init.shape[1] == LANES
    any_spec = pl.BlockSpec(memory_space=pl.ANY)
    return pl.pallas_call(
        functools.partial(_dispatch_kernel, n_tiles),
        out_shape=jax.ShapeDtypeStruct(xs_init.shape, jnp.uint32),
        grid=(n_tiles,),
        in_specs=[
            pl.BlockSpec((1, 1, ts), lambda i: (i, 0, 0), memory_space=pltpu.SMEM),
            any_spec, any_spec,
        ],
        out_specs=any_spec,
        scratch_shapes=[
            pltpu.VMEM((DISPATCH_SLOTS, ts * SUBLANES, LANES), jnp.uint32),
            pltpu.SemaphoreType.DMA((DISPATCH_SLOTS,)),
            pltpu.SemaphoreType.DMA((DISPATCH_SLOTS,)),
        ],
        input_output_aliases={2: 0},
        compiler_params=pltpu.CompilerParams(
            dimension_semantics=("arbitrary",), vmem_limit_bytes=VMEM_LIMIT_BYTES),
        name="moe_dispatch",
    )(dest, hrows, xs_init)


def _expert_kernel(blo_ref, bhi_ref, bvalid_ref, nblk_ref, xs_ref, wg_hbm, wu_hbm, wd_hbm, ys_ref,
                   wg_vm, wu_vm, wd_vm, resident, wsem):
    del nblk_ref
    b = pl.program_id(0)
    valid = bvalid_ref[b]

    @pl.when(b == 0)
    def _():
        for s in range(EXPERTS_PER_GROUP):
            resident[s] = -1

    def slot_of(e):
        s = e % EXPERTS_PER_GROUP

        @pl.when((valid > 0) & (resident[s] != e))
        def _():
            copies = [pltpu.make_async_copy(hbm.at[e], vm.at[s], wsem.at[k])
                      for k, (hbm, vm) in enumerate(((wg_hbm, wg_vm), (wu_hbm, wu_vm), (wd_hbm, wd_vm)))]
            for c in copies:
                c.start()
            for c in copies:
                c.wait()
            resident[s] = e

        return s

    s_lo, s_hi = slot_of(blo_ref[b]), slot_of(bhi_ref[b])

    def experts_on(m):
        pieces = []
        for j in range(PACK_SUBLANES):
            w = _token_rows(xs_ref, j, m)
            pieces.append(_bf16(_f32_from_bits(w & jnp.uint32(0xFFFF0000))))
            pieces.append(_bf16(_f32_from_bits(lax.shift_left(w, jnp.uint32(16)))))
        xb = jnp.concatenate(pieces, axis=1)
        gates = _f32_from_bits(_token_rows(xs_ref, PACK_SUBLANES, m))
        acc = None
        for idx, s in enumerate((s_lo, s_hi)):
            a = _dot(xb, wg_vm[s])
            hid = (a * jax.nn.sigmoid(a)) * _dot(xb, wu_vm[s]) * gates[:, idx:idx + 1]
            part = _dot(_bf16(hid), wd_vm[s])
            acc = part if acc is None else acc + part
        for s in range(SUBLANES):
            ys_ref[pl.ds(s, m, stride=SUBLANES), :] = acc[:, s * LANES:(s + 1) * LANES]
        if m < ROW_BLOCK:
            ys_ref[m * SUBLANES:ROW_BLOCK * SUBLANES, :] = jnp.zeros(
                ((ROW_BLOCK - m) * SUBLANES, LANES), jnp.float32)

    for m in range(ROW_STEP, ROW_BLOCK + 1, ROW_STEP):
        pl.when((valid > m - ROW_STEP) & (valid <= m))(functools.partial(experts_on, m))


def _expert_call(blo, bhi, bvalid, nblk, xs, wg, wu, wd):
    n_rows = xs.shape[0] // SUBLANES
    d, de = wg.shape[1], wg.shape[2]
    assert d == SUBLANES * LANES
    n_blocks = n_rows // ROW_BLOCK
    assert wg.shape[0] == N_EXPERTS

    def row_map(b, lo, hi, bv, n):
        return (jnp.minimum(b, jnp.maximum(n[0] - 1, 0)), 0)

    any_spec = pl.BlockSpec(memory_space=pl.ANY)
    grid_spec = pltpu.PrefetchScalarGridSpec(
        num_scalar_prefetch=4,
        grid=(n_blocks,),
        in_specs=[pl.BlockSpec((ROW_BLOCK * SUBLANES, LANES), row_map), any_spec, any_spec, any_spec],
        out_specs=pl.BlockSpec((ROW_BLOCK * SUBLANES, LANES), row_map),
        scratch_shapes=[
            pltpu.VMEM((EXPERTS_PER_GROUP, d, de), jnp.bfloat16),
            pltpu.VMEM((EXPERTS_PER_GROUP, d, de), jnp.bfloat16),
            pltpu.VMEM((EXPERTS_PER_GROUP, de, d), jnp.bfloat16),
            pltpu.SMEM((EXPERTS_PER_GROUP,), jnp.int32),
            pltpu.SemaphoreType.DMA((3,)),
        ],
    )
    return pl.pallas_call(
        _expert_kernel,
        out_shape=jax.ShapeDtypeStruct((n_rows * SUBLANES, LANES), jnp.float32),
        grid_spec=grid_spec,
        compiler_params=pltpu.CompilerParams(
            dimension_semantics=("arbitrary",), vmem_limit_bytes=VMEM_LIMIT_BYTES),
        name="moe_experts",
    )(blo, bhi, bvalid, nblk, xs, wg, wu, wd)


def _final_kernel(n_tiles, dest_ref, dnext_ref, x1_ref, mod_ref, fmod_ref, fg_ref, ys_ref,
                  out_ref, ybuf, sem, unperm):
    i = pl.program_id(0)
    ts, d = x1_ref.shape
    bsz, tm, _ = out_ref.shape
    y, retire_last = _gather_expert_tiles(ys_ref, ybuf, sem, dest_ref, dnext_ref, i, n_tiles, ts,
                                          inline=False)
    x2 = x1_ref[...] + _per_batch(mod_ref[5], ts) * y
    res = _rms_mod(x2, fg_ref[...], fmod_ref[0], fmod_ref[1])
    for cb in range(d // LANES):
        unperm[cb] = res[:, cb * LANES:(cb + 1) * LANES]
    for cb in range(d // LANES):
        for p in range(bsz):
            out_ref[p, :, cb * LANES:(cb + 1) * LANES] = unperm.at[cb][pl.ds(p, tm, stride=SUBLANES), :]
    retire_last()


def _final_call(dest, x1, mod, ys, bsz, fmod, fin_g):
    t, d = x1.shape
    n_tiles, _, ts = dest.shape
    tm = ts // bsz
    return pl.pallas_call(
        functools.partial(_final_kernel, n_tiles),
        out_shape=jax.ShapeDtypeStruct((bsz, t // bsz, d), jnp.float32),
        grid=(n_tiles,),
        in_specs=[
            pl.BlockSpec((1, 1, ts), lambda i: (i, 0, 0), memory_space=pltpu.SMEM),
            pl.BlockSpec((1, 1, ts), lambda i: (jnp.minimum(i + 1, n_tiles - 1), 0, 0),
                         memory_space=pltpu.SMEM),
            pl.BlockSpec((ts, d), lambda i: (i, 0)),
            _full((MOD_ROWS, bsz, d)), _full((MOD_ROWS, bsz, d)), _full((1, d)),
            pl.BlockSpec(memory_space=pl.ANY),
        ],
        out_specs=pl.BlockSpec((bsz, tm, d), lambda i: (0, i, 0)),
        scratch_shapes=[pltpu.VMEM((2, ts * SUBLANES, LANES), jnp.float32),
                        pltpu.SemaphoreType.DMA((2,)),
                        pltpu.VMEM((d // LANES, ts, LANES), jnp.float32)],
        compiler_params=pltpu.CompilerParams(
            dimension_semantics=("arbitrary",), vmem_limit_bytes=VMEM_LIMIT_BYTES),
        name="moe_combine_final",
    )(dest, dest, x1, mod, fmod, fin_g, ys)


def _router_operands(wg, bg, wr, br):
    d = wg.shape[0]
    pad_g = jnp.zeros((SUBLANES - N_GROUPS, d), jnp.float32)
    pad_e = jnp.zeros((ROUTER_ROWS - SUBLANES - N_EXPERTS, d), jnp.float32)
    wrt = jnp.concatenate([wg.T, pad_g, wr.T, pad_e], axis=0)
    bcol = jnp.concatenate([bg, jnp.zeros((SUBLANES - N_GROUPS,), jnp.float32), br,
                            jnp.zeros((ROUTER_ROWS - SUBLANES - N_EXPERTS,), jnp.float32)])
    brt = jnp.broadcast_to(bcol[:, None], (ROUTER_ROWS, LANES))
    return _bf16(wrt), brt


def _mod_table(m, parts):
    bsz = m.shape[0]
    d = m.shape[1] // parts
    m = jnp.transpose(m.reshape(bsz, parts, d), (1, 0, 2))
    return jnp.concatenate([m, jnp.zeros((MOD_ROWS - parts, bsz, d), jnp.float32)], axis=0)


def _row(v):
    return v.reshape(1, -1)


def _moe_experts(hrows, cls, rank, cnt, weights_bf16, n_blocks, xs_init):
    counts = cnt[:, 0].astype(jnp.int32)
    dest, blo, bhi, bvalid, nblk = _plan_call(cls, rank, counts, n_blocks)
    xs = _dispatch_call(dest, hrows, xs_init)
    ys = _expert_call(blo, bhi, bvalid, nblk, xs, *weights_bf16)
    return xs, ys, dest


def kernel(x, c, ada_w, ada_b, norm1_g, norm2_g, cm_w1, cm_b1, cm_dw, cm_dwb, cm_ln_g, cm_ln_b, cm_w2, cm_b2, rg_wy, rg_by, rg_wx, rg_bx, rg_cw, rg_cb, rg_wa, rg_ba, rg_wi, rg_bi, rg_lambda, rg_wo, rg_bo, moe_wg, moe_bg, moe_wr, moe_br, moe_w_gate, moe_w_up, moe_w_down, fin_ada_w, fin_ada_b, fin_g):
    bsz, seq, d = x.shape
    t = bsz * seq
    depth = ada_w.shape[0]
    assert depth == 2 and bsz == SUBLANES and seq % TIME_TILE == 0
    assert d == 2 * PACK_SUBLANES * LANES

    mods = _ada_call(c, ada_w, ada_b[:, None, :])
    fmods = _ada_call(c, fin_ada_w[None], fin_ada_b[None, None, :])
    fmod = _mod_table(fmods[0], 2)
    n_steps = seq // TIME_TILE
    granule = n_steps // math.gcd(n_steps, ROW_BLOCK)
    n_blocks = -(-(-(-t // ROW_BLOCK) + N_CLASSES) // granule) * granule
    mod0, mod1 = _mod_table(mods[0], 6), _mod_table(mods[1], 6)

    wrt, brt = _router_operands(moe_wg[0], moe_bg[0], moe_wr[0], moe_br[0])
    dw = jnp.concatenate([cm_dw[0], jnp.zeros((1, d), jnp.float32)], axis=0)
    dw = jnp.transpose(dw.reshape(CONF_KERNEL + 1, d // LANES, LANES), (1, 0, 2))
    expert_weights = (moe_w_gate, moe_w_up, moe_w_down)
    (x1, hrows, cls, rank, cnt, xs_init), weights0 = _conf_call(
        x, mod0, _row(norm1_g[0]), _row(norm2_g[0]), _bf16(cm_w1[0]), _row(cm_b1[0]),
        dw, cm_dwb[0].reshape(d // LANES, 1, LANES), _row(cm_ln_g[0]), _row(cm_ln_b[0]),
        _bf16(cm_w2[0]), _row(cm_b2[0]), wrt, brt, n_blocks * ROW_BLOCK * SUBLANES,
        expert_weights, 0)
    xs, ys, dest = _moe_experts(hrows, cls, rank, cnt, weights0, n_blocks, xs_init)

    wrt, brt = _router_operands(moe_wg[1], moe_bg[1], moe_wr[1], moe_br[1])
    (x1, hrows, cls, rank, cnt), weights1 = _lru_call(
        dest, x1, mod0, ys, mod1, _row(norm1_g[1]), _row(norm2_g[1]),
        _bf16(rg_wy[0]), _row(rg_by[0]), _bf16(rg_wx[0]), _row(rg_bx[0]),
        rg_cw[0], _row(rg_cb[0]), _bf16(rg_wa[0]), _row(rg_ba[0]), _bf16(rg_wi[0]), _row(rg_bi[0]),
        _row(rg_lambda[0]), _bf16(rg_wo[0]), _row(rg_bo[0]), wrt, brt, expert_weights, 1)
    _, ys, dest = _moe_experts(hrows, cls, rank, cnt, weights1, n_blocks, xs)
    return _final_call(dest, x1, mod1, ys, bsz, fmod, _row(fin_g))
```

```python
import functools
import math

import jax
import jax.numpy as jnp
import numpy as np
from jax import lax
from jax.experimental import pallas as pl
from jax.experimental.pallas import tpu as pltpu

EPS = 1e-6
CONF_KERNEL = 31
LRU_HEADS = 4
LRU_CONV = 4
LRU_C = 8.0
N_GROUPS = 4
EXPERTS_PER_GROUP = 8
N_EXPERTS = N_GROUPS * EXPERTS_PER_GROUP
PAIRS_PER_GROUP = EXPERTS_PER_GROUP * (EXPERTS_PER_GROUP - 1) // 2
N_CLASSES = N_GROUPS * PAIRS_PER_GROUP

LANES = 128
SUBLANES = 8
BF16_SUBLANES = 16
VMEM_LIMIT_BYTES = 56 * 1024 * 1024

TOKEN_TILE = 512
TIME_TILE = TOKEN_TILE // SUBLANES
SUB_TILE = 256
ROW_BLOCK = 512
ROW_STEP = 128
PERM_PITCH = TIME_TILE + SUBLANES
CONV_ROWS = 128
ROUTER_ROWS = 64
PACK_SUBLANES = 4
MOD_ROWS = 8
DISPATCH_SLOTS = 3
DMA_UNROLL = 16

_NEG_INF = float("-inf")


def _class_tables():
    lo = np.zeros((N_CLASSES,), np.int32)
    hi = np.zeros((N_CLASSES,), np.int32)
    for g in range(N_GROUPS):
        for a in range(EXPERTS_PER_GROUP):
            for b in range(a + 1, EXPERTS_PER_GROUP):
                c = g * PAIRS_PER_GROUP + (a * (2 * EXPERTS_PER_GROUP - 1 - a)) // 2 + (b - a - 1)
                lo[c] = g * EXPERTS_PER_GROUP + a
                hi[c] = g * EXPERTS_PER_GROUP + b
    return lo, hi


_CLASS_LO, _CLASS_HI = _class_tables()


def _bf16(x):
    return x.astype(jnp.bfloat16)


def _dot(a, b):
    return jnp.dot(a, b, preferred_element_type=jnp.float32)


def _bits(x):
    return lax.bitcast_convert_type(x, jnp.uint32)


def _f32_from_bits(x):
    return lax.bitcast_convert_type(x, jnp.float32)


def _gelu_tanh(x):
    c0 = math.sqrt(2.0 / math.pi)
    t = jnp.tanh(x * (c0 + (c0 * 0.044715) * (x * x)))
    hx = 0.5 * x
    return hx + hx * t


def _token_rows(ref, sublane, n):
    return ref[pl.ds(sublane, n, stride=SUBLANES), :]


def _per_batch(v, ts):
    return jnp.tile(v, (ts // SUBLANES, 1))


def _ada_kernel(c_ref, w_ref, b_ref, o_ref):
    c = c_ref[...]
    ca = _bf16(c * jax.nn.sigmoid(c))
    o_ref[...] = _dot(ca, _bf16(w_ref[...])) + b_ref[...]


def _ada_call(c, w, b):
    n_layers, d, n = w.shape
    bsz = c.shape[0]
    nb = 1024
    return pl.pallas_call(
        _ada_kernel,
        out_shape=jax.ShapeDtypeStruct((n_layers, bsz, n), jnp.float32),
        grid=(n_layers, n // nb),
        in_specs=[
            pl.BlockSpec((bsz, d), lambda l, j: (0, 0)),
            pl.BlockSpec((None, d, nb), lambda l, j: (l, 0, j)),
            pl.BlockSpec((None, 1, nb), lambda l, j: (l, 0, j)),
        ],
        out_specs=pl.BlockSpec((None, bsz, nb), lambda l, j: (l, 0, j)),
        compiler_params=pltpu.CompilerParams(
            dimension_semantics=("arbitrary", "arbitrary"),
            vmem_limit_bytes=VMEM_LIMIT_BYTES),
        name="ada_mod",
    )(c, w, b)


def _rms_mod(x, gain, shift, scale):
    ts = x.shape[0]
    ms = jnp.mean(x * x, axis=-1, keepdims=True)
    return (x * lax.rsqrt(ms + EPS)) * _per_batch(gain * (1.0 + scale), ts) + _per_batch(shift, ts)


def _causal_taps(w_ref, bias_ref, ext_ref, n_taps, row0, n):
    d = ext_ref.shape[1]
    cols = []
    for cb in range(d // LANES):
        sl = slice(cb * LANES, (cb + 1) * LANES)
        acc = jnp.broadcast_to(bias_ref[:, sl], (n, LANES))
        for k in range(n_taps):
            lo = row0 + SUBLANES * k
            acc = acc + w_ref[k:k + 1, sl] * ext_ref[lo:lo + n, sl]
        cols.append(acc)
    return jnp.concatenate(cols, axis=1)


def _causal_taps_unit(w_ref, bias_ref, ext_ref, out_ref, cb, r0, n_taps):
    w_cb, ext_cb = w_ref.at[cb], ext_ref.at[cb]
    acc = jnp.broadcast_to(bias_ref[cb], (CONV_ROWS, LANES))
    for k in range(n_taps):
        win = ext_cb[pl.ds(pl.multiple_of(r0 + SUBLANES * k, SUBLANES), CONV_ROWS), :]
        acc = acc + w_cb[k:k + 1, :] * win
    out_ref.at[cb][pl.ds(r0, CONV_ROWS), :] = acc


def _start_tile_dmas(make_copy, n, inline=False, both_queues=False):
    if inline:
        for r in range(n):
            make_copy(r).start()
        return

    def body(g, carry):
        for k in range(DMA_UNROLL):
            make_copy(g * DMA_UNROLL + k).start(priority=k % 2 if both_queues else 0)
        return carry
    lax.fori_loop(0, n // DMA_UNROLL, body, 0)


def _wait_all_tiles(hbm_ref, vmem_tiles_ref, sem, to_hbm):
    hbm_view = hbm_ref.at[pl.ds(0, vmem_tiles_ref.shape[0])]
    if to_hbm:
        pltpu.make_async_copy(vmem_tiles_ref, hbm_view, sem).wait()
    else:
        pltpu.make_async_copy(hbm_view, vmem_tiles_ref, sem).wait()


def _gather_expert_tiles(ys_ref, ybuf, sem, dest_ref, dnext_ref, step, n_steps, ts, inline):
    slot = step % 2

    def tile_copy(dref, sl):
        def make(r):
            src = pl.multiple_of(dref[0, 0, r] * SUBLANES, SUBLANES)
            return pltpu.make_async_copy(
                ys_ref.at[pl.ds(src, SUBLANES)],
                ybuf.at[sl, pl.ds(pl.multiple_of(r * SUBLANES, SUBLANES), SUBLANES)],
                sem.at[sl])
        return make

    @pl.when(step == 0)
    def _():
        _start_tile_dmas(tile_copy(dest_ref, 0), ts)

    if not inline:
        @pl.when(step + 1 < n_steps)
        def _():
            _start_tile_dmas(tile_copy(dnext_ref, 1 - slot), ts)

    _wait_all_tiles(ys_ref, ybuf.at[slot], sem.at[slot], to_hbm=False)
    if inline:
        _start_tile_dmas(tile_copy(dnext_ref, 1 - slot), ts, inline=True)
    yv = ybuf.at[slot]
    y = jnp.concatenate([_token_rows(yv, s, ts) for s in range(SUBLANES)], axis=1)

    def retire_last():
        if inline:
            @pl.when(step + 1 == n_steps)
            def _():
                _wait_all_tiles(ys_ref, ybuf.at[1 - slot], sem.at[1 - slot], to_hbm=False)

    return y, retire_last


def _route(logits_t):
    ts = logits_t.shape[1]
    row = lax.broadcasted_iota(jnp.int32, (SUBLANES, ts), 0)
    gl = jnp.where(row < N_GROUPS, logits_t[0:SUBLANES], _NEG_INF)
    gmax = jnp.max(gl, axis=0, keepdims=True)
    gsel = jnp.min(jnp.where(gl == gmax, row, SUBLANES), axis=0, keepdims=True)
    p_g = 1.0 / jnp.sum(jnp.exp(gl - gmax), axis=0, keepdims=True)
    es = jnp.zeros((SUBLANES, ts), jnp.float32)
    for g in range(N_GROUPS):
        lo = SUBLANES * (1 + g)
        es = jnp.where(gsel == g, logits_t[lo:lo + EXPERTS_PER_GROUP], es)
    m1 = jnp.max(es, axis=0, keepdims=True)
    i1 = jnp.min(jnp.where(es == m1, row, SUBLANES), axis=0, keepdims=True)
    es2 = jnp.where(row == i1, _NEG_INF, es)
    m2 = jnp.max(es2, axis=0, keepdims=True)
    i2 = jnp.min(jnp.where(es2 == m2, row, SUBLANES), axis=0, keepdims=True)
    t = jnp.exp(m2 - m1)
    w1 = p_g / (1.0 + t)
    w2 = w1 * t
    first_lo = i1 < i2
    e_lo = jnp.minimum(i1, i2)
    e_hi = jnp.maximum(i1, i2)
    gate_lo = jnp.where(first_lo, w1, w2)
    gate_hi = jnp.where(first_lo, w2, w1)
    pair = lax.shift_right_logical(e_lo * (2 * EXPERTS_PER_GROUP - 1 - e_lo), 1) + (e_hi - e_lo - 1)
    cls = gsel * PAIRS_PER_GROUP + pair
    return cls, gate_lo, gate_hi


def _post_mixer(x1, mod_ref, n2_ref, wrt_ref, brt_ref, hrow_ref, cls_ref, rank_ref, cnt_ref, sub):
    ts = x1.shape[0]
    hrow_ref = hrow_ref.at[pl.ds(sub * ts * SUBLANES, ts * SUBLANES)]
    cols = slice(sub * ts, (sub + 1) * ts)
    h2 = _rms_mod(x1, n2_ref[...], mod_ref[3], mod_ref[4])
    h2b = _bf16(h2)
    logits_t = lax.dot_general(wrt_ref[...], h2b, (((1,), (1,)), ((), ())),
                               preferred_element_type=jnp.float32) + brt_ref[:, 0:1]
    cls, gate_lo, gate_hi = _route(logits_t)
    gates_t = jnp.concatenate(
        [gate_lo, gate_hi, jnp.zeros((LANES - 2, ts), jnp.float32)], axis=0)
    h2r = h2b.astype(jnp.float32)
    for j in range(PACK_SUBLANES):
        hi = _bits(h2r[:, (2 * j) * LANES:(2 * j + 1) * LANES])
        lo = _bits(h2r[:, (2 * j + 1) * LANES:(2 * j + 2) * LANES])
        hrow_ref[pl.ds(j, ts, stride=SUBLANES), :] = hi | lax.shift_right_logical(lo, jnp.uint32(16))
    hrow_ref[pl.ds(PACK_SUBLANES, ts, stride=SUBLANES), :] = _bits(gates_t.T)
    for j in range(PACK_SUBLANES + 1, SUBLANES):
        hrow_ref[pl.ds(j, ts, stride=SUBLANES), :] = jnp.zeros((ts, LANES), jnp.uint32)
    cls_ref[0, :, cols] = cls
    crow = lax.broadcasted_iota(jnp.int32, (LANES, ts), 0)
    onehot = jnp.where(crow == cls, 1.0, 0.0)
    cnt = jnp.sum(onehot, axis=1, keepdims=True)
    earlier = (lax.broadcasted_iota(jnp.int32, (ts, ts), 0)
               < lax.broadcasted_iota(jnp.int32, (ts, ts), 1))
    prefix = _dot(_bf16(onehot), _bf16(jnp.where(earlier, 1.0, 0.0)))
    rank = jnp.sum(onehot * (prefix + cnt_ref[:, 0:1]), axis=0, keepdims=True)
    rank_ref[0, :, cols] = rank.astype(jnp.int32)
    cnt_ref[...] += jnp.broadcast_to(cnt, cnt_ref.shape)


def _mixer_out(t, d, ts):
    n_tiles = t // ts
    out_shape = (
        jax.ShapeDtypeStruct((t, d), jnp.float32),
        jax.ShapeDtypeStruct((t * SUBLANES, LANES), jnp.uint32),
        jax.ShapeDtypeStruct((n_tiles, 1, ts), jnp.int32),
        jax.ShapeDtypeStruct((n_tiles, 1, ts), jnp.int32),
        jax.ShapeDtypeStruct((LANES, LANES), jnp.float32),
    )
    out_specs = (
        pl.BlockSpec((ts, d), lambda i: (i, 0)),
        pl.BlockSpec((ts * SUBLANES, LANES), lambda i: (i, 0)),
        pl.BlockSpec((1, 1, ts), lambda i: (i, 0, 0)),
        pl.BlockSpec((1, 1, ts), lambda i: (i, 0, 0)),
        pl.BlockSpec((LANES, LANES), lambda i: (0, 0)),
    )
    return out_shape, out_specs


def _full(shape):
    return pl.BlockSpec(shape, lambda i: tuple(0 for _ in shape))


def _expert_weight_cast(weights, layer, n_steps):
    flat = [w.reshape(w.shape[0], w.shape[1] * w.shape[2], w.shape[3]) for w in weights]
    if any(w.shape[1] % (n_steps * BF16_SUBLANES) for w in flat):
        return [], [], (), ()
    in_specs = [pl.BlockSpec((None, w.shape[1] // n_steps, w.shape[2]), lambda i: (layer, i, 0))
                for w in flat]
    out_shapes = tuple(jax.ShapeDtypeStruct(w.shape[1:], jnp.bfloat16) for w in flat)
    out_specs = tuple(pl.BlockSpec((w.shape[1] // n_steps, w.shape[2]), lambda i: (i, 0))
                      for w in flat)
    return flat, in_specs, out_shapes, out_specs


def _cast_slices(cast_in, cast_out):
    for src, dst in zip(cast_in, cast_out):
        dst[...] = _bf16(src[...])


def _cast_results(outs, weights, layer):
    if outs:
        return [o.reshape(w.shape[1:]) for o, w in zip(outs, weights)]
    return [_bf16(w[layer]) for w in weights]


def _to_tile_order(x_ref, perm, xs):
    bsz, tm, d = x_ref.shape
    for cb in range(d // LANES):
        for p in range(bsz):
            perm[cb, PERM_PITCH * p:PERM_PITCH * p + tm, :] = x_ref[p, :, cb * LANES:(cb + 1) * LANES]
    for m in range(tm):
        for cb in range(d // LANES):
            xs[m * bsz:(m + 1) * bsz, cb * LANES:(cb + 1) * LANES] = (
                perm.at[cb][pl.ds(m, bsz, stride=PERM_PITCH), :])


def _conf_kernel(n_cast, x_ref, mod_ref, n1_ref, n2_ref, w1_ref, b1_ref, dw_ref, dwb_ref,
                 lng_ref, lnb_ref, w2_ref, b2_ref, wrt_ref, brt_ref, *rest):
    cast_in, rest = rest[:n_cast], rest[n_cast:]
    x1_ref, hrow_ref, cls_ref, rank_ref, cnt_ref, zeros_ref = rest[:6]
    cast_out, (perm, xs, vext, conv) = rest[6:6 + n_cast], rest[6 + n_cast:]
    _cast_slices(cast_in, cast_out)
    i = pl.program_id(0)
    bsz, tm, d = x_ref.shape
    ts = bsz * tm
    n_chunks = d // LANES
    hist = SUBLANES * (CONF_KERNEL - 1)

    @pl.when(i == 0)
    def _():
        vext[:, 0:hist, :] = jnp.zeros((n_chunks, hist, LANES), jnp.float32)
        cnt_ref[...] = jnp.zeros_like(cnt_ref)

    zeros_ref[...] = jnp.zeros_like(zeros_ref)

    _to_tile_order(x_ref, perm, xs)
    for sub in range(ts // SUB_TILE):
        rows = slice(sub * SUB_TILE, (sub + 1) * SUB_TILE)
        h = _bf16(_rms_mod(xs[rows, :], n1_ref[...], mod_ref[0], mod_ref[1]))
        u = _dot(h, w1_ref[...]) + b1_ref[...]
        v = u[:, 0:d] * jax.nn.sigmoid(u[:, d:2 * d])
        for cb in range(n_chunks):
            vext[cb, hist + sub * SUB_TILE:hist + (sub + 1) * SUB_TILE, :] = v[:, cb * LANES:(cb + 1) * LANES]

    units_per_chunk = ts // CONV_ROWS

    def conv_unit(idx, carry):
        r0 = pl.multiple_of((idx % units_per_chunk) * CONV_ROWS, CONV_ROWS)
        _causal_taps_unit(dw_ref, dwb_ref, vext, conv, idx // units_per_chunk, r0, CONF_KERNEL)
        return carry

    lax.fori_loop(0, n_chunks * units_per_chunk, conv_unit, 0)
    vext[:, 0:hist, :] = vext[:, ts:ts + hist, :]

    for sub in range(ts // SUB_TILE):
        rows = slice(sub * SUB_TILE, (sub + 1) * SUB_TILE)
        y = jnp.concatenate([conv[cb, rows, :] for cb in range(n_chunks)], axis=1)
        mu = jnp.mean(y, axis=-1, keepdims=True)
        yc = y - mu
        var = jnp.mean(yc * yc, axis=-1, keepdims=True)
        z = yc * lax.rsqrt(var + EPS) * lng_ref[...] + lnb_ref[...]
        z = z * jax.nn.sigmoid(z)
        mix = _dot(_bf16(z), w2_ref[...]) + b2_ref[...]
        x1 = xs[rows, :] + _per_batch(mod_ref[2], SUB_TILE) * mix
        x1_ref[rows, :] = x1
        _post_mixer(x1, mod_ref, n2_ref, wrt_ref, brt_ref, hrow_ref, cls_ref, rank_ref, cnt_ref, sub)


def _conf_call(x, mod, n1, n2, w1, b1, dw, dwb, lng, lnb, w2, b2, wrt, brt, zero_rows,
               expert_weights, layer):
    bsz, seq, d = x.shape
    tm = TIME_TILE
    ts = bsz * tm
    n_steps = seq // tm
    assert zero_rows % (n_steps * SUBLANES) == 0
    cast, cast_specs, cast_shapes, cast_out_specs = _expert_weight_cast(expert_weights, layer, n_steps)
    out_shape, out_specs = _mixer_out(bsz * seq, d, ts)
    out_shape += (jax.ShapeDtypeStruct((zero_rows, LANES), jnp.uint32),) + cast_shapes
    out_specs += (pl.BlockSpec((zero_rows // n_steps, LANES), lambda i: (i, 0)),) + cast_out_specs
    hist = SUBLANES * (CONF_KERNEL - 1)
    outs = pl.pallas_call(
        functools.partial(_conf_kernel, len(cast)),
        out_shape=out_shape,
        grid=(n_steps,),
        in_specs=[
            pl.BlockSpec((bsz, tm, d), lambda i: (0, i, 0)),
            _full((MOD_ROWS, bsz, d)),
            _full((1, d)), _full((1, d)),
            _full((d, 2 * d)), _full((1, 2 * d)),
            _full((d // LANES, CONF_KERNEL + 1, LANES)), _full((d // LANES, 1, LANES)),
            _full((1, d)), _full((1, d)),
            _full((d, d)), _full((1, d)),
            _full((ROUTER_ROWS, d)), _full((ROUTER_ROWS, LANES)),
        ] + cast_specs,
        out_specs=out_specs,
        scratch_shapes=[
            pltpu.VMEM((d // LANES, bsz * PERM_PITCH, LANES), jnp.float32),
            pltpu.VMEM((ts, d), jnp.float32),
            pltpu.VMEM((d // LANES, hist + ts, LANES), jnp.float32),
            pltpu.VMEM((d // LANES, ts, LANES), jnp.float32),
        ],
        compiler_params=pltpu.CompilerParams(
            dimension_semantics=("arbitrary",), vmem_limit_bytes=VMEM_LIMIT_BYTES),
        name="conf_mixer",
    )(x, mod, n1, n2, w1, b1, dw, dwb, lng, lnb, w2, b2, wrt, brt, *cast)
    return outs[:6], _cast_results(outs[6:], expert_weights, layer)


def _lru_kernel(n_tiles, n_cast, dest_ref, dnext_ref, xprev_ref, pmod_ref, mod_ref, n1_ref, n2_ref,
                wy_ref, by_ref, wx_ref, bx_ref,
                cw_ref, cb_ref, wa_ref, ba_ref, wi_ref, bi_ref, lam_ref, wo_ref, bo_ref,
                wrt_ref, brt_ref, ys_ref, *rest):
    cast_in, rest = rest[:n_cast], rest[n_cast:]
    x1_ref, hrow_ref, cls_ref, rank_ref, cnt_ref = rest[:5]
    cast_out, (uext, hcar, ybuf, ysem) = rest[5:5 + n_cast], rest[5 + n_cast:]
    _cast_slices(cast_in, cast_out)
    i = pl.program_id(0)
    ts, d = xprev_ref.shape
    hd = d // LRU_HEADS
    hist = SUBLANES * (LRU_CONV - 1)

    @pl.when(i == 0)
    def _():
        uext[0:hist, :] = jnp.zeros((hist, d), jnp.float32)
        hcar[...] = jnp.zeros_like(hcar)
        cnt_ref[...] = jnp.zeros_like(cnt_ref)

    y_prev, retire_last = _gather_expert_tiles(ys_ref, ybuf, ysem, dest_ref, dnext_ref, i, n_tiles, ts,
                                               inline=True)
    nl = -lam_ref[...]
    decay = LRU_C * (jnp.maximum(nl, 0.0) + jnp.log1p(jnp.exp(-jnp.abs(nl))))
    hstate = hcar[...]
    for sub in range(ts // SUB_TILE):
        rows = slice(sub * SUB_TILE, (sub + 1) * SUB_TILE)
        x = xprev_ref[rows, :] + _per_batch(pmod_ref[5], SUB_TILE) * y_prev[rows, :]
        h = _bf16(_rms_mod(x, n1_ref[...], mod_ref[0], mod_ref[1]))
        ygate = _gelu_tanh(_dot(h, wy_ref[...]) + by_ref[...])
        xb = _dot(h, wx_ref[...]) + bx_ref[...]
        uext[hist + sub * SUB_TILE:hist + (sub + 1) * SUB_TILE, :] = xb
        u = _causal_taps(cw_ref, cb_ref, uext, LRU_CONV, sub * SUB_TILE, SUB_TILE)

        ub = _bf16(u)
        r_parts, i_parts = [], []
        for hh in range(LRU_HEADS):
            uh = ub[:, hh * hd:(hh + 1) * hd]
            r_parts.append(_dot(uh, wa_ref[hh]))
            i_parts.append(_dot(uh, wi_ref[hh]))
        r = jax.nn.sigmoid(jnp.concatenate(r_parts, axis=1) + ba_ref[...])
        ig = jax.nn.sigmoid(jnp.concatenate(i_parts, axis=1) + bi_ref[...])

        neg_log_a = r * decay
        a = jnp.exp(-neg_log_a)
        mult = jnp.sqrt(jnp.tanh(neg_log_a) * (1.0 + a * a))
        b = mult * (ig * u)

        outs = []
        for m in range(SUB_TILE // SUBLANES):
            sl = slice(m * SUBLANES, (m + 1) * SUBLANES)
            hstate = a[sl] * hstate + b[sl]
            outs.append(hstate)
        hs = jnp.concatenate(outs, axis=0)

        mix = _dot(_bf16(hs * ygate), wo_ref[...]) + bo_ref[...]
        x1 = x + _per_batch(mod_ref[2], SUB_TILE) * mix
        x1_ref[rows, :] = x1
        _post_mixer(x1, mod_ref, n2_ref, wrt_ref, brt_ref, hrow_ref, cls_ref, rank_ref, cnt_ref, sub)
    hcar[...] = hstate
    uext[0:hist, :] = uext[ts:ts + hist, :]
    retire_last()


def _lru_call(dest, x_prev, mod_prev, ys, mod, n1, n2, wy, by, wx, bx, cw, cb, wa, ba, wi, bi, lam,
              wo, bo, wrt, brt, expert_weights, layer):
    t, d = x_prev.shape
    n_tiles, _, ts = dest.shape
    bsz = mod.shape[1]
    hd = d // LRU_HEADS
    hist = SUBLANES * (LRU_CONV - 1)
    cast, cast_specs, cast_shapes, cast_out_specs = _expert_weight_cast(expert_weights, layer, n_tiles)
    out_shape, out_specs = _mixer_out(t, d, ts)
    out_shape += cast_shapes
    out_specs += cast_out_specs
    outs = pl.pallas_call(
        functools.partial(_lru_kernel, n_tiles, len(cast)),
        out_shape=out_shape,
        grid=(n_tiles,),
        in_specs=[
            pl.BlockSpec((1, 1, ts), lambda i: (i, 0, 0), memory_space=pltpu.SMEM),
            pl.BlockSpec((1, 1, ts), lambda i: (jnp.minimum(i + 1, n_tiles - 1), 0, 0),
                         memory_space=pltpu.SMEM),
            pl.BlockSpec((ts, d), lambda i: (i, 0)),
            _full((MOD_ROWS, bsz, d)), _full((MOD_ROWS, bsz, d)),
            _full((1, d)), _full((1, d)),
            _full((d, d)), _full((1, d)),
            _full((d, d)), _full((1, d)),
            _full((LRU_CONV, d)), _full((1, d)),
            _full((LRU_HEADS, hd, hd)), _full((1, d)),
            _full((LRU_HEADS, hd, hd)), _full((1, d)),
            _full((1, d)),
            _full((d, d)), _full((1, d)),
            _full((ROUTER_ROWS, d)), _full((ROUTER_ROWS, LANES)),
            pl.BlockSpec(memory_space=pl.ANY),
        ] + cast_specs,
        out_specs=out_specs,
        scratch_shapes=[
            pltpu.VMEM((hist + ts, d), jnp.float32),
            pltpu.VMEM((SUBLANES, d), jnp.float32),
            pltpu.VMEM((2, ts * SUBLANES, LANES), jnp.float32),
            pltpu.SemaphoreType.DMA((2,)),
        ],
        compiler_params=pltpu.CompilerParams(
            dimension_semantics=("arbitrary",), vmem_limit_bytes=VMEM_LIMIT_BYTES),
        name="lru_mixer",
    )(dest, dest, x_prev, mod_prev, mod, n1, n2, wy, by, wx, bx, cw, cb, wa, ba, wi, bi, lam, wo, bo,
      wrt, brt, ys, *cast)
    return outs[:5], _cast_results(outs[5:], expert_weights, layer)


def _plan_kernel(n_blocks, cnt_ref, clo_ref, chi_ref, cls_ref, rank_ref,
                 dest_ref, blo_ref, bhi_ref, bvalid_ref, nblk_ref, class_start):
    n_tiles, _, ts = cls_ref.shape
    class_start[...] = jnp.zeros_like(class_start)

    def per_class(c, nb_total):
        class_start[pl.ds(c, 1), :] = jnp.full((1, LANES), nb_total * ROW_BLOCK, jnp.int32)
        nb = (cnt_ref[c] + (ROW_BLOCK - 1)) // ROW_BLOCK

        def fill(k, carry):
            blo_ref[nb_total + k] = clo_ref[c]
            bhi_ref[nb_total + k] = chi_ref[c]
            bvalid_ref[nb_total + k] = jnp.minimum(cnt_ref[c] - k * ROW_BLOCK, ROW_BLOCK)
            return carry

        lax.fori_loop(0, nb, fill, 0)
        return nb_total + nb

    nb_used = lax.fori_loop(0, N_CLASSES, per_class, 0)
    nblk_ref[0] = nb_used
    last = jnp.maximum(nb_used - 1, 0)

    def tail(k, carry):
        blo_ref[k] = blo_ref[last]
        bhi_ref[k] = bhi_ref[last]
        bvalid_ref[k] = 0
        return carry

    lax.fori_loop(nb_used, n_blocks, tail, 0)

    starts = class_start[:, 0:1]
    class_id = lax.broadcasted_iota(jnp.int32, (LANES, ts), 0)

    def per_tile(t, carry):
        first = jnp.sum(jnp.where(class_id == cls_ref[t], starts, 0), axis=0, keepdims=True)
        dest_ref[t] = first + rank_ref[t]
        return carry

    lax.fori_loop(0, n_tiles, per_tile, 0)


def _plan_call(cls, rank, counts, n_blocks):
    n_tiles, _, ts = cls.shape
    smem_full = pl.BlockSpec(memory_space=pltpu.SMEM)
    vmem_full = pl.BlockSpec(memory_space=pltpu.VMEM)
    return pl.pallas_call(
        functools.partial(_plan_kernel, n_blocks),
        out_shape=(
            jax.ShapeDtypeStruct((n_tiles, 1, ts), jnp.int32),
            jax.ShapeDtypeStruct((n_blocks,), jnp.int32),
            jax.ShapeDtypeStruct((n_blocks,), jnp.int32),
            jax.ShapeDtypeStruct((n_blocks,), jnp.int32),
            jax.ShapeDtypeStruct((1,), jnp.int32),
        ),
        in_specs=[smem_full, smem_full, smem_full, vmem_full, vmem_full],
        out_specs=(vmem_full, smem_full, smem_full, smem_full, smem_full),
        scratch_shapes=[pltpu.VMEM((LANES, LANES), jnp.int32)],
        compiler_params=pltpu.CompilerParams(vmem_limit_bytes=VMEM_LIMIT_BYTES),
        name="moe_plan",
    )(counts, jnp.asarray(_CLASS_LO), jnp.asarray(_CLASS_HI), cls, rank)


def _dispatch_kernel(n_tiles, dest_ref, hrow_ref, xs_in_ref, xs_ref, stage, sem_in, sem_out):
    del xs_in_ref
    i = pl.program_id(0)
    ts = dest_ref.shape[2]
    rows = ts * SUBLANES

    def load(tile, slot):
        return pltpu.make_async_copy(hrow_ref.at[pl.ds(pl.multiple_of(tile * rows, rows), rows)],
                                     stage.at[slot], sem_in.at[slot])

    def tile_copy(slot, r, dst_row):
        return pltpu.make_async_copy(
            stage.at[slot, pl.ds(pl.multiple_of(r * SUBLANES, SUBLANES), SUBLANES)],
            xs_ref.at[pl.ds(pl.multiple_of(dst_row * SUBLANES, SUBLANES), SUBLANES)],
            sem_out.at[slot])

    def drain(slot):
        _wait_all_tiles(xs_ref, stage.at[slot], sem_out.at[slot], to_hbm=True)

    slot = i % DISPATCH_SLOTS
    nxt = (i + 1) % DISPATCH_SLOTS

    @pl.when(i == 0)
    def _():
        load(0, 0).start()

    @pl.when(i >= 2)
    def _():
        drain(nxt)

    @pl.when(i + 1 < n_tiles)
    def _():
        load(i + 1, nxt).start()

    load(i, slot).wait()

    _start_tile_dmas(lambda r: tile_copy(slot, r, dest_ref[0, 0, r]), ts, both_queues=True)

    @pl.when(i == n_tiles - 1)
    def _():
        @pl.when(i >= 1)
        def _():
            drain((i + DISPATCH_SLOTS - 1) % DISPATCH_SLOTS)
        drain(slot)


def _dispatch_call(dest, hrows, xs_init):
    n_tiles, _, ts = dest.shape
    assert xs_init.dtype == jnp.uint32 and xs_init.shape[1] == LANES
    any_spec = pl.BlockSpec(memory_space=pl.ANY)
    return pl.pallas_call(
        functools.partial(_dispatch_kernel, n_tiles),
        out_shape=jax.ShapeDtypeStruct(xs_init.shape, jnp.uint32),
        grid=(n_tiles,),
        in_specs=[
            pl.BlockSpec((1, 1, ts), lambda i: (i, 0, 0), memory_space=pltpu.SMEM),
            any_spec, any_spec,
        ],
        out_specs=any_spec,
        scratch_shapes=[
            pltpu.VMEM((DISPATCH_SLOTS, ts * SUBLANES, LANES), jnp.uint32),
            pltpu.SemaphoreType.DMA((DISPATCH_SLOTS,)),
            pltpu.SemaphoreType.DMA((DISPATCH_SLOTS,)),
        ],
        input_output_aliases={2: 0},
        compiler_params=pltpu.CompilerParams(
            dimension_semantics=("arbitrary",), vmem_limit_bytes=VMEM_LIMIT_BYTES),
        name="moe_dispatch",
    )(dest, hrows, xs_init)


def _expert_kernel(blo_ref, bhi_ref, bvalid_ref, nblk_ref, xs_ref,
                   wg_lo, wu_lo, wd_lo, wg_hi, wu_hi, wd_hi, ys_ref):
    del blo_ref, bhi_ref, nblk_ref
    valid = bvalid_ref[pl.program_id(0)]

    def experts_on(m):
        pieces = []
        for j in range(PACK_SUBLANES):
            w = _token_rows(xs_ref, j, m)
            pieces.append(_bf16(_f32_from_bits(w & jnp.uint32(0xFFFF0000))))
            pieces.append(_bf16(_f32_from_bits(lax.shift_left(w, jnp.uint32(16)))))
        xb = jnp.concatenate(pieces, axis=1)
        gates = _f32_from_bits(_token_rows(xs_ref, PACK_SUBLANES, m))
        acc = None
        for idx, (wg, wu, wd) in enumerate(((wg_lo, wu_lo, wd_lo), (wg_hi, wu_hi, wd_hi))):
            a = _dot(xb, wg[...])
            hid = (a * jax.nn.sigmoid(a)) * _dot(xb, wu[...]) * gates[:, idx:idx + 1]
            part = _dot(_bf16(hid), wd[...])
            acc = part if acc is None else acc + part
        for s in range(SUBLANES):
            ys_ref[pl.ds(s, m, stride=SUBLANES), :] = acc[:, s * LANES:(s + 1) * LANES]
        if m < ROW_BLOCK:
            ys_ref[m * SUBLANES:ROW_BLOCK * SUBLANES, :] = jnp.zeros(
                ((ROW_BLOCK - m) * SUBLANES, LANES), jnp.float32)

    for m in range(ROW_STEP, ROW_BLOCK + 1, ROW_STEP):
        pl.when((valid > m - ROW_STEP) & (valid <= m))(functools.partial(experts_on, m))


def _expert_call(blo, bhi, bvalid, nblk, xs, wg, wu, wd):
    n_rows = xs.shape[0] // SUBLANES
    d, de = wg.shape[1], wg.shape[2]
    assert d == SUBLANES * LANES
    n_blocks = n_rows // ROW_BLOCK

    def w_spec(shape, which):
        if which == 0:
            return pl.BlockSpec((None,) + shape, lambda b, lo, hi, bv, n: (lo[b], 0, 0))
        return pl.BlockSpec((None,) + shape, lambda b, lo, hi, bv, n: (hi[b], 0, 0))

    def row_map(b, lo, hi, bv, n):
        return (jnp.minimum(b, jnp.maximum(n[0] - 1, 0)), 0)

    grid_spec = pltpu.PrefetchScalarGridSpec(
        num_scalar_prefetch=4,
        grid=(n_blocks,),
        in_specs=[
            pl.BlockSpec((ROW_BLOCK * SUBLANES, LANES), row_map),
            w_spec((d, de), 0), w_spec((d, de), 0), w_spec((de, d), 0),
            w_spec((d, de), 1), w_spec((d, de), 1), w_spec((de, d), 1),
        ],
        out_specs=pl.BlockSpec((ROW_BLOCK * SUBLANES, LANES), row_map),
    )
    return pl.pallas_call(
        _expert_kernel,
        out_shape=jax.ShapeDtypeStruct((n_rows * SUBLANES, LANES), jnp.float32),
        grid_spec=grid_spec,
        compiler_params=pltpu.CompilerParams(
            dimension_semantics=("arbitrary",), vmem_limit_bytes=VMEM_LIMIT_BYTES),
        name="moe_experts",
    )(blo, bhi, bvalid, nblk, xs, wg, wu, wd, wg, wu, wd)


def _final_kernel(n_tiles, dest_ref, dnext_ref, x1_ref, mod_ref, fmod_ref, fg_ref, ys_ref,
                  out_ref, ybuf, sem, unperm):
    i = pl.program_id(0)
    ts, d = x1_ref.shape
    bsz, tm, _ = out_ref.shape
    y, retire_last = _gather_expert_tiles(ys_ref, ybuf, sem, dest_ref, dnext_ref, i, n_tiles, ts,
                                          inline=False)
    x2 = x1_ref[...] + _per_batch(mod_ref[5], ts) * y
    res = _rms_mod(x2, fg_ref[...], fmod_ref[0], fmod_ref[1])
    for cb in range(d // LANES):
        unperm[cb] = res[:, cb * LANES:(cb + 1) * LANES]
    for cb in range(d // LANES):
        for p in range(bsz):
            out_ref[p, :, cb * LANES:(cb + 1) * LANES] = unperm.at[cb][pl.ds(p, tm, stride=SUBLANES), :]
    retire_last()


def _final_call(dest, x1, mod, ys, bsz, fmod, fin_g):
    t, d = x1.shape
    n_tiles, _, ts = dest.shape
    tm = ts // bsz
    return pl.pallas_call(
        functools.partial(_final_kernel, n_tiles),
        out_shape=jax.ShapeDtypeStruct((bsz, t // bsz, d), jnp.float32),
        grid=(n_tiles,),
        in_specs=[
            pl.BlockSpec((1, 1, ts), lambda i: (i, 0, 0), memory_space=pltpu.SMEM),
            pl.BlockSpec((1, 1, ts), lambda i: (jnp.minimum(i + 1, n_tiles - 1), 0, 0),
                         memory_space=pltpu.SMEM),
            pl.BlockSpec((ts, d), lambda i: (i, 0)),
            _full((MOD_ROWS, bsz, d)), _full((MOD_ROWS, bsz, d)), _full((1, d)),
            pl.BlockSpec(memory_space=pl.ANY),
        ],
        out_specs=pl.BlockSpec((bsz, tm, d), lambda i: (0, i, 0)),
        scratch_shapes=[pltpu.VMEM((2, ts * SUBLANES, LANES), jnp.float32),
                        pltpu.SemaphoreType.DMA((2,)),
                        pltpu.VMEM((d // LANES, ts, LANES), jnp.float32)],
        compiler_params=pltpu.CompilerParams(
            dimension_semantics=("arbitrary",), vmem_limit_bytes=VMEM_LIMIT_BYTES),
        name="moe_combine_final",
    )(dest, dest, x1, mod, fmod, fin_g, ys)


def _router_operands(wg, bg, wr, br):
    d = wg.shape[0]
    pad_g = jnp.zeros((SUBLANES - N_GROUPS, d), jnp.float32)
    pad_e = jnp.zeros((ROUTER_ROWS - SUBLANES - N_EXPERTS, d), jnp.float32)
    wrt = jnp.concatenate([wg.T, pad_g, wr.T, pad_e], axis=0)
    bcol = jnp.concatenate([bg, jnp.zeros((SUBLANES - N_GROUPS,), jnp.float32), br,
                            jnp.zeros((ROUTER_ROWS - SUBLANES - N_EXPERTS,), jnp.float32)])
    brt = jnp.broadcast_to(bcol[:, None], (ROUTER_ROWS, LANES))
    return _bf16(wrt), brt


def _mod_table(m, parts):
    bsz = m.shape[0]
    d = m.shape[1] // parts
    m = jnp.transpose(m.reshape(bsz, parts, d), (1, 0, 2))
    return jnp.concatenate([m, jnp.zeros((MOD_ROWS - parts, bsz, d), jnp.float32)], axis=0)


def _row(v):
    return v.reshape(1, -1)


def _moe_experts(hrows, cls, rank, cnt, weights_bf16, n_blocks, xs_init):
    counts = cnt[:, 0].astype(jnp.int32)
    dest, blo, bhi, bvalid, nblk = _plan_call(cls, rank, counts, n_blocks)
    xs = _dispatch_call(dest, hrows, xs_init)
    ys = _expert_call(blo, bhi, bvalid, nblk, xs, *weights_bf16)
    return xs, ys, dest


def kernel(x, c, ada_w, ada_b, norm1_g, norm2_g, cm_w1, cm_b1, cm_dw, cm_dwb, cm_ln_g, cm_ln_b, cm_w2, cm_b2, rg_wy, rg_by, rg_wx, rg_bx, rg_cw, rg_cb, rg_wa, rg_ba, rg_wi, rg_bi, rg_lambda, rg_wo, rg_bo, moe_wg, moe_bg, moe_wr, moe_br, moe_w_gate, moe_w_up, moe_w_down, fin_ada_w, fin_ada_b, fin_g):
    bsz, seq, d = x.shape
    t = bsz * seq
    depth = ada_w.shape[0]
    assert depth == 2 and bsz == SUBLANES and seq % TIME_TILE == 0
    assert d == 2 * PACK_SUBLANES * LANES

    mods = _ada_call(c, ada_w, ada_b[:, None, :])
    fmods = _ada_call(c, fin_ada_w[None], fin_ada_b[None, None, :])
    fmod = _mod_table(fmods[0], 2)
    n_steps = seq // TIME_TILE
    granule = n_steps // math.gcd(n_steps, ROW_BLOCK)
    n_blocks = -(-(-(-t // ROW_BLOCK) + N_CLASSES) // granule) * granule
    mod0, mod1 = _mod_table(mods[0], 6), _mod_table(mods[1], 6)

    wrt, brt = _router_operands(moe_wg[0], moe_bg[0], moe_wr[0], moe_br[0])
    dw = jnp.concatenate([cm_dw[0], jnp.zeros((1, d), jnp.float32)], axis=0)
    dw = jnp.transpose(dw.reshape(CONF_KERNEL + 1, d // LANES, LANES), (1, 0, 2))
    expert_weights = (moe_w_gate, moe_w_up, moe_w_down)
    (x1, hrows, cls, rank, cnt, xs_init), weights0 = _conf_call(
        x, mod0, _row(norm1_g[0]), _row(norm2_g[0]), _bf16(cm_w1[0]), _row(cm_b1[0]),
        dw, cm_dwb[0].reshape(d // LANES, 1, LANES), _row(cm_ln_g[0]), _row(cm_ln_b[0]),
        _bf16(cm_w2[0]), _row(cm_b2[0]), wrt, brt, n_blocks * ROW_BLOCK * SUBLANES,
        expert_weights, 0)
    xs, ys, dest = _moe_experts(hrows, cls, rank, cnt, weights0, n_blocks, xs_init)

    wrt, brt = _router_operands(moe_wg[1], moe_bg[1], moe_wr[1], moe_br[1])
    (x1, hrows, cls, rank, cnt), weights1 = _lru_call(
        dest, x1, mod0, ys, mod1, _row(norm1_g[1]), _row(norm2_g[1]),
        _bf16(rg_wy[0]), _row(rg_by[0]), _bf16(rg_wx[0]), _row(rg_bx[0]),
        rg_cw[0], _row(rg_cb[0]), _bf16(rg_wa[0]), _row(rg_ba[0]), _bf16(rg_wi[0]), _row(rg_bi[0]),
        _row(rg_lambda[0]), _bf16(rg_wo[0]), _row(rg_bo[0]), wrt, brt, expert_weights, 1)
    _, ys, dest = _moe_experts(hrows, cls, rank, cnt, weights1, n_blocks, xs)
    return _final_call(dest, x1, mod1, ys, bsz, fmod, _row(fin_g))
```

```python
import functools
import math

import jax
import jax.numpy as jnp
import numpy as np
from jax import lax
from jax.experimental import pallas as pl
from jax.experimental.pallas import tpu as pltpu

EPS = 1e-6
CONF_KERNEL = 31
LRU_HEADS = 4
LRU_CONV = 4
LRU_C = 8.0
N_GROUPS = 4
EXPERTS_PER_GROUP = 8
N_EXPERTS = N_GROUPS * EXPERTS_PER_GROUP
PAIRS_PER_GROUP = EXPERTS_PER_GROUP * (EXPERTS_PER_GROUP - 1) // 2
N_CLASSES = N_GROUPS * PAIRS_PER_GROUP

LANES = 128
SUBLANES = 8
BF16_SUBLANES = 16
VMEM_LIMIT_BYTES = 56 * 1024 * 1024

TOKEN_TILE = 512
TIME_TILE = TOKEN_TILE // SUBLANES
SUB_TILE = 256
ROW_BLOCK = 512
ROW_STEP = 128
PERM_PITCH = TIME_TILE + SUBLANES
CONV_ROWS = 128
ROUTER_ROWS = 64
PACK_SUBLANES = 4
MOD_ROWS = 8
DISPATCH_SLOTS = 3
DMA_UNROLL = 16

_NEG_INF = float("-inf")


def _class_tables():
    lo = np.zeros((N_CLASSES,), np.int32)
    hi = np.zeros((N_CLASSES,), np.int32)
    for g in range(N_GROUPS):
        for a in range(EXPERTS_PER_GROUP):
            for b in range(a + 1, EXPERTS_PER_GROUP):
                c = g * PAIRS_PER_GROUP + (a * (2 * EXPERTS_PER_GROUP - 1 - a)) // 2 + (b - a - 1)
                lo[c] = g * EXPERTS_PER_GROUP + a
                hi[c] = g * EXPERTS_PER_GROUP + b
    return lo, hi


_CLASS_LO, _CLASS_HI = _class_tables()


def _bf16(x):
    return x.astype(jnp.bfloat16)


def _dot(a, b):
    return jnp.dot(a, b, preferred_element_type=jnp.float32)


def _bits(x):
    return lax.bitcast_convert_type(x, jnp.uint32)


def _f32_from_bits(x):
    return lax.bitcast_convert_type(x, jnp.float32)


def _gelu_tanh(x):
    c0 = math.sqrt(2.0 / math.pi)
    t = jnp.tanh(x * (c0 + (c0 * 0.044715) * (x * x)))
    hx = 0.5 * x
    return hx + hx * t


def _token_rows(ref, sublane, n):
    return ref[pl.ds(sublane, n, stride=SUBLANES), :]


def _per_batch(v, ts):
    return jnp.tile(v, (ts // SUBLANES, 1))


def _ada_kernel(c_ref, w_ref, b_ref, o_ref):
    c = c_ref[...]
    ca = _bf16(c * jax.nn.sigmoid(c))
    o_ref[...] = _dot(ca, _bf16(w_ref[...])) + b_ref[...]


def _ada_call(c, w, b):
    n_layers, d, n = w.shape
    bsz = c.shape[0]
    nb = 1024
    return pl.pallas_call(
        _ada_kernel,
        out_shape=jax.ShapeDtypeStruct((n_layers, bsz, n), jnp.float32),
        grid=(n_layers, n // nb),
        in_specs=[
            pl.BlockSpec((bsz, d), lambda l, j: (0, 0)),
            pl.BlockSpec((None, d, nb), lambda l, j: (l, 0, j)),
            pl.BlockSpec((None, 1, nb), lambda l, j: (l, 0, j)),
        ],
        out_specs=pl.BlockSpec((None, bsz, nb), lambda l, j: (l, 0, j)),
        compiler_params=pltpu.CompilerParams(
            dimension_semantics=("arbitrary", "arbitrary"),
            vmem_limit_bytes=VMEM_LIMIT_BYTES),
        name="ada_mod",
    )(c, w, b)


def _rms_mod(x, gain, shift, scale):
    ts = x.shape[0]
    ms = jnp.mean(x * x, axis=-1, keepdims=True)
    return (x * lax.rsqrt(ms + EPS)) * _per_batch(gain * (1.0 + scale), ts) + _per_batch(shift, ts)


def _causal_taps(w_ref, bias_ref, ext_ref, n_taps, row0, n):
    d = ext_ref.shape[1]
    cols = []
    for cb in range(d // LANES):
        sl = slice(cb * LANES, (cb + 1) * LANES)
        acc = jnp.broadcast_to(bias_ref[:, sl], (n, LANES))
        for k in range(n_taps):
            lo = row0 + SUBLANES * k
            acc = acc + w_ref[k:k + 1, sl] * ext_ref[lo:lo + n, sl]
        cols.append(acc)
    return jnp.concatenate(cols, axis=1)


def _causal_taps_unit(w_ref, bias_ref, ext_ref, out_ref, cb, r0, n_taps):
    w_cb, ext_cb = w_ref.at[cb], ext_ref.at[cb]
    acc = jnp.broadcast_to(bias_ref[cb], (CONV_ROWS, LANES))
    for k in range(n_taps):
        win = ext_cb[pl.ds(pl.multiple_of(r0 + SUBLANES * k, SUBLANES), CONV_ROWS), :]
        acc = acc + w_cb[k:k + 1, :] * win
    out_ref.at[cb][pl.ds(r0, CONV_ROWS), :] = acc


def _start_tile_dmas(make_copy, n, inline=False, both_queues=False):
    if inline:
        for r in range(n):
            make_copy(r).start()
        return

    def body(g, carry):
        for k in range(DMA_UNROLL):
            make_copy(g * DMA_UNROLL + k).start(priority=k % 2 if both_queues else 0)
        return carry
    lax.fori_loop(0, n // DMA_UNROLL, body, 0)


def _wait_all_tiles(hbm_ref, vmem_tiles_ref, sem, to_hbm):
    hbm_view = hbm_ref.at[pl.ds(0, vmem_tiles_ref.shape[0])]
    if to_hbm:
        pltpu.make_async_copy(vmem_tiles_ref, hbm_view, sem).wait()
    else:
        pltpu.make_async_copy(hbm_view, vmem_tiles_ref, sem).wait()


def _gather_expert_tiles(ys_ref, ybuf, sem, dest_ref, dnext_ref, step, n_steps, ts, inline):
    slot = step % 2

    def tile_copy(dref, sl):
        def make(r):
            src = pl.multiple_of(dref[0, 0, r] * SUBLANES, SUBLANES)
            return pltpu.make_async_copy(
                ys_ref.at[pl.ds(src, SUBLANES)],
                ybuf.at[sl, pl.ds(pl.multiple_of(r * SUBLANES, SUBLANES), SUBLANES)],
                sem.at[sl])
        return make

    @pl.when(step == 0)
    def _():
        _start_tile_dmas(tile_copy(dest_ref, 0), ts)

    if not inline:
        @pl.when(step + 1 < n_steps)
        def _():
            _start_tile_dmas(tile_copy(dnext_ref, 1 - slot), ts)

    _wait_all_tiles(ys_ref, ybuf.at[slot], sem.at[slot], to_hbm=False)
    if inline:
        _start_tile_dmas(tile_copy(dnext_ref, 1 - slot), ts, inline=True)
    yv = ybuf.at[slot]
    y = jnp.concatenate([_token_rows(yv, s, ts) for s in range(SUBLANES)], axis=1)

    def retire_last():
        if inline:
            @pl.when(step + 1 == n_steps)
            def _():
                _wait_all_tiles(ys_ref, ybuf.at[1 - slot], sem.at[1 - slot], to_hbm=False)

    return y, retire_last


def _route(logits_t):
    ts = logits_t.shape[1]
    row = lax.broadcasted_iota(jnp.int32, (SUBLANES, ts), 0)
    gl = jnp.where(row < N_GROUPS, logits_t[0:SUBLANES], _NEG_INF)
    gmax = jnp.max(gl, axis=0, keepdims=True)
    gsel = jnp.min(jnp.where(gl == gmax, row, SUBLANES), axis=0, keepdims=True)
    p_g = 1.0 / jnp.sum(jnp.exp(gl - gmax), axis=0, keepdims=True)
    es = jnp.zeros((SUBLANES, ts), jnp.float32)
    for g in range(N_GROUPS):
        lo = SUBLANES * (1 + g)
        es = jnp.where(gsel == g, logits_t[lo:lo + EXPERTS_PER_GROUP], es)
    m1 = jnp.max(es, axis=0, keepdims=True)
    i1 = jnp.min(jnp.where(es == m1, row, SUBLANES), axis=0, keepdims=True)
    es2 = jnp.where(row == i1, _NEG_INF, es)
    m2 = jnp.max(es2, axis=0, keepdims=True)
    i2 = jnp.min(jnp.where(es2 == m2, row, SUBLANES), axis=0, keepdims=True)
    t = jnp.exp(m2 - m1)
    w1 = p_g / (1.0 + t)
    w2 = w1 * t
    first_lo = i1 < i2
    e_lo = jnp.minimum(i1, i2)
    e_hi = jnp.maximum(i1, i2)
    gate_lo = jnp.where(first_lo, w1, w2)
    gate_hi = jnp.where(first_lo, w2, w1)
    pair = lax.shift_right_logical(e_lo * (2 * EXPERTS_PER_GROUP - 1 - e_lo), 1) + (e_hi - e_lo - 1)
    cls = gsel * PAIRS_PER_GROUP + pair
    return cls, gate_lo, gate_hi


def _post_mixer(x1, mod_ref, n2_ref, wrt_ref, brt_ref, hrow_ref, cls_ref, rank_ref, cnt_ref, sub):
    ts = x1.shape[0]
    hrow_ref = hrow_ref.at[pl.ds(sub * ts * SUBLANES, ts * SUBLANES)]
    cols = slice(sub * ts, (sub + 1) * ts)
    h2 = _rms_mod(x1, n2_ref[...], mod_ref[3], mod_ref[4])
    h2b = _bf16(h2)
    logits_t = lax.dot_general(wrt_ref[...], h2b, (((1,), (1,)), ((), ())),
                               preferred_element_type=jnp.float32) + brt_ref[:, 0:1]
    cls, gate_lo, gate_hi = _route(logits_t)
    gates_t = jnp.concatenate(
        [gate_lo, gate_hi, jnp.zeros((LANES - 2, ts), jnp.float32)], axis=0)
    h2r = h2b.astype(jnp.float32)
    for j in range(PACK_SUBLANES):
        hi = _bits(h2r[:, (2 * j) * LANES:(2 * j + 1) * LANES])
        lo = _bits(h2r[:, (2 * j + 1) * LANES:(2 * j + 2) * LANES])
        hrow_ref[pl.ds(j, ts, stride=SUBLANES), :] = hi | lax.shift_right_logical(lo, jnp.uint32(16))
    hrow_ref[pl.ds(PACK_SUBLANES, ts, stride=SUBLANES), :] = _bits(gates_t.T)
    for j in range(PACK_SUBLANES + 1, SUBLANES):
        hrow_ref[pl.ds(j, ts, stride=SUBLANES), :] = jnp.zeros((ts, LANES), jnp.uint32)
    cls_ref[0, :, cols] = cls
    crow = lax.broadcasted_iota(jnp.int32, (LANES, ts), 0)
    onehot = jnp.where(crow == cls, 1.0, 0.0)
    cnt = jnp.sum(onehot, axis=1, keepdims=True)
    earlier = (lax.broadcasted_iota(jnp.int32, (ts, ts), 0)
               < lax.broadcasted_iota(jnp.int32, (ts, ts), 1))
    prefix = _dot(_bf16(onehot), _bf16(jnp.where(earlier, 1.0, 0.0)))
    rank = jnp.sum(onehot * (prefix + cnt_ref[:, 0:1]), axis=0, keepdims=True)
    rank_ref[0, :, cols] = rank.astype(jnp.int32)
    cnt_ref[...] += jnp.broadcast_to(cnt, cnt_ref.shape)


def _mixer_out(t, d, ts):
    n_tiles = t // ts
    out_shape = (
        jax.ShapeDtypeStruct((t, d), jnp.float32),
        jax.ShapeDtypeStruct((t * SUBLANES, LANES), jnp.uint32),
        jax.ShapeDtypeStruct((n_tiles, 1, ts), jnp.int32),
        jax.ShapeDtypeStruct((n_tiles, 1, ts), jnp.int32),
        jax.ShapeDtypeStruct((LANES, LANES), jnp.float32),
    )
    out_specs = (
        pl.BlockSpec((ts, d), lambda i: (i, 0)),
        pl.BlockSpec((ts * SUBLANES, LANES), lambda i: (i, 0)),
        pl.BlockSpec((1, 1, ts), lambda i: (i, 0, 0)),
        pl.BlockSpec((1, 1, ts), lambda i: (i, 0, 0)),
        pl.BlockSpec((LANES, LANES), lambda i: (0, 0)),
    )
    return out_shape, out_specs


def _full(shape):
    return pl.BlockSpec(shape, lambda i: tuple(0 for _ in shape))


def _expert_weight_cast(weights, layer, n_steps):
    flat = [w.reshape(w.shape[0], w.shape[1] * w.shape[2], w.shape[3]) for w in weights]
    if any(w.shape[1] % (n_steps * BF16_SUBLANES) for w in flat):
        return [], [], (), ()
    in_specs = [pl.BlockSpec((None, w.shape[1] // n_steps, w.shape[2]), lambda i: (layer, i, 0))
                for w in flat]
    out_shapes = tuple(jax.ShapeDtypeStruct(w.shape[1:], jnp.bfloat16) for w in flat)
    out_specs = tuple(pl.BlockSpec((w.shape[1] // n_steps, w.shape[2]), lambda i: (i, 0))
                      for w in flat)
    return flat, in_specs, out_shapes, out_specs


def _cast_slices(cast_in, cast_out):
    for src, dst in zip(cast_in, cast_out):
        dst[...] = _bf16(src[...])


def _cast_results(outs, weights, layer):
    if outs:
        return [o.reshape(w.shape[1:]) for o, w in zip(outs, weights)]
    return [_bf16(w[layer]) for w in weights]


def _to_tile_order(x_ref, perm, xs):
    bsz, tm, d = x_ref.shape
    for cb in range(d // LANES):
        for p in range(bsz):
            perm[cb, PERM_PITCH * p:PERM_PITCH * p + tm, :] = x_ref[p, :, cb * LANES:(cb + 1) * LANES]
    for m in range(tm):
        for cb in range(d // LANES):
            xs[m * bsz:(m + 1) * bsz, cb * LANES:(cb + 1) * LANES] = (
                perm.at[cb][pl.ds(m, bsz, stride=PERM_PITCH), :])


def _conf_kernel(n_cast, x_ref, mod_ref, n1_ref, n2_ref, w1_ref, b1_ref, dw_ref, dwb_ref,
                 lng_ref, lnb_ref, w2_ref, b2_ref, wrt_ref, brt_ref, *rest):
    cast_in, rest = rest[:n_cast], rest[n_cast:]
    x1_ref, hrow_ref, cls_ref, rank_ref, cnt_ref, zeros_ref = rest[:6]
    cast_out, (perm, xs, vext, conv) = rest[6:6 + n_cast], rest[6 + n_cast:]
    _cast_slices(cast_in, cast_out)
    i = pl.program_id(0)
    bsz, tm, d = x_ref.shape
    ts = bsz * tm
    n_chunks = d // LANES
    hist = SUBLANES * (CONF_KERNEL - 1)

    @pl.when(i == 0)
    def _():
        vext[:, 0:hist, :] = jnp.zeros((n_chunks, hist, LANES), jnp.float32)
        cnt_ref[...] = jnp.zeros_like(cnt_ref)

    zeros_ref[...] = jnp.zeros_like(zeros_ref)

    _to_tile_order(x_ref, perm, xs)
    for sub in range(ts // SUB_TILE):
        rows = slice(sub * SUB_TILE, (sub + 1) * SUB_TILE)
        h = _bf16(_rms_mod(xs[rows, :], n1_ref[...], mod_ref[0], mod_ref[1]))
        u = _dot(h, w1_ref[...]) + b1_ref[...]
        v = u[:, 0:d] * jax.nn.sigmoid(u[:, d:2 * d])
        for cb in range(n_chunks):
            vext[cb, hist + sub * SUB_TILE:hist + (sub + 1) * SUB_TILE, :] = v[:, cb * LANES:(cb + 1) * LANES]

    units_per_chunk = ts // CONV_ROWS

    def conv_unit(idx, carry):
        r0 = pl.multiple_of((idx % units_per_chunk) * CONV_ROWS, CONV_ROWS)
        _causal_taps_unit(dw_ref, dwb_ref, vext, conv, idx // units_per_chunk, r0, CONF_KERNEL)
        return carry

    lax.fori_loop(0, n_chunks * units_per_chunk, conv_unit, 0)
    vext[:, 0:hist, :] = vext[:, ts:ts + hist, :]

    for sub in range(ts // SUB_TILE):
        rows = slice(sub * SUB_TILE, (sub + 1) * SUB_TILE)
        y = jnp.concatenate([conv[cb, rows, :] for cb in range(n_chunks)], axis=1)
        mu = jnp.mean(y, axis=-1, keepdims=True)
        yc = y - mu
        var = jnp.mean(yc * yc, axis=-1, keepdims=True)
        z = yc * lax.rsqrt(var + EPS) * lng_ref[...] + lnb_ref[...]
        z = z * jax.nn.sigmoid(z)
        mix = _dot(_bf16(z), w2_ref[...]) + b2_ref[...]
        x1 = xs[rows, :] + _per_batch(mod_ref[2], SUB_TILE) * mix
        x1_ref[rows, :] = x1
        _post_mixer(x1, mod_ref, n2_ref, wrt_ref, brt_ref, hrow_ref, cls_ref, rank_ref, cnt_ref, sub)


def _conf_call(x, mod, n1, n2, w1, b1, dw, dwb, lng, lnb, w2, b2, wrt, brt, zero_rows,
               expert_weights, layer):
    bsz, seq, d = x.shape
    tm = TIME_TILE
    ts = bsz * tm
    n_steps = seq // tm
    assert zero_rows % (n_steps * SUBLANES) == 0
    cast, cast_specs, cast_shapes, cast_out_specs = _expert_weight_cast(expert_weights, layer, n_steps)
    out_shape, out_specs = _mixer_out(bsz * seq, d, ts)
    out_shape += (jax.ShapeDtypeStruct((zero_rows, LANES), jnp.uint32),) + cast_shapes
    out_specs += (pl.BlockSpec((zero_rows // n_steps, LANES), lambda i: (i, 0)),) + cast_out_specs
    hist = SUBLANES * (CONF_KERNEL - 1)
    outs = pl.pallas_call(
        functools.partial(_conf_kernel, len(cast)),
        out_shape=out_shape,
        grid=(n_steps,),
        in_specs=[
            pl.BlockSpec((bsz, tm, d), lambda i: (0, i, 0)),
            _full((MOD_ROWS, bsz, d)),
            _full((1, d)), _full((1, d)),
            _full((d, 2 * d)), _full((1, 2 * d)),
            _full((d // LANES, CONF_KERNEL + 1, LANES)), _full((d // LANES, 1, LANES)),
            _full((1, d)), _full((1, d)),
            _full((d, d)), _full((1, d)),
            _full((ROUTER_ROWS, d)), _full((ROUTER_ROWS, LANES)),
        ] + cast_specs,
        out_specs=out_specs,
        scratch_shapes=[
            pltpu.VMEM((d // LANES, bsz * PERM_PITCH, LANES), jnp.float32),
            pltpu.VMEM((ts, d), jnp.float32),
            pltpu.VMEM((d // LANES, hist + ts, LANES), jnp.float32),
            pltpu.VMEM((d // LANES, ts, LANES), jnp.float32),
        ],
        compiler_params=pltpu.CompilerParams(
            dimension_semantics=("arbitrary",), vmem_limit_bytes=VMEM_LIMIT_BYTES),
        name="conf_mixer",
    )(x, mod, n1, n2, w1, b1, dw, dwb, lng, lnb, w2, b2, wrt, brt, *cast)
    return outs[:6], _cast_results(outs[6:], expert_weights, layer)


def _lru_kernel(n_tiles, n_cast, dest_ref, dnext_ref, xprev_ref, pmod_ref, mod_ref, n1_ref, n2_ref,
                wy_ref, by_ref, wx_ref, bx_ref,
                cw_ref, cb_ref, wa_ref, ba_ref, wi_ref, bi_ref, lam_ref, wo_ref, bo_ref,
                wrt_ref, brt_ref, ys_ref, *rest):
    cast_in, rest = rest[:n_cast], rest[n_cast:]
    x1_ref, hrow_ref, cls_ref, rank_ref, cnt_ref = rest[:5]
    cast_out, (uext, hcar, ybuf, ysem) = rest[5:5 + n_cast], rest[5 + n_cast:]
    _cast_slices(cast_in, cast_out)
    i = pl.program_id(0)
    ts, d = xprev_ref.shape
    hd = d // LRU_HEADS
    hist = SUBLANES * (LRU_CONV - 1)

    @pl.when(i == 0)
    def _():
        uext[0:hist, :] = jnp.zeros((hist, d), jnp.float32)
        hcar[...] = jnp.zeros_like(hcar)
        cnt_ref[...] = jnp.zeros_like(cnt_ref)

    y_prev, retire_last = _gather_expert_tiles(ys_ref, ybuf, ysem, dest_ref, dnext_ref, i, n_tiles, ts,
                                               inline=True)
    nl = -lam_ref[...]
    decay = LRU_C * (jnp.maximum(nl, 0.0) + jnp.log1p(jnp.exp(-jnp.abs(nl))))
    hstate = hcar[...]
    for sub in range(ts // SUB_TILE):
        rows = slice(sub * SUB_TILE, (sub + 1) * SUB_TILE)
        x = xprev_ref[rows, :] + _per_batch(pmod_ref[5], SUB_TILE) * y_prev[rows, :]
        h = _bf16(_rms_mod(x, n1_ref[...], mod_ref[0], mod_ref[1]))
        ygate = _gelu_tanh(_dot(h, wy_ref[...]) + by_ref[...])
        xb = _dot(h, wx_ref[...]) + bx_ref[...]
        uext[hist + sub * SUB_TILE:hist + (sub + 1) * SUB_TILE, :] = xb
        u = _causal_taps(cw_ref, cb_ref, uext, LRU_CONV, sub * SUB_TILE, SUB_TILE)

        ub = _bf16(u)
        r_parts, i_parts = [], []
        for hh in range(LRU_HEADS):
            uh = ub[:, hh * hd:(hh + 1) * hd]
            r_parts.append(_dot(uh, wa_ref[hh]))
            i_parts.append(_dot(uh, wi_ref[hh]))
        r = jax.nn.sigmoid(jnp.concatenate(r_parts, axis=1) + ba_ref[...])
        ig = jax.nn.sigmoid(jnp.concatenate(i_parts, axis=1) + bi_ref[...])

        neg_log_a = r * decay
        a = jnp.exp(-neg_log_a)
        mult = jnp.sqrt(jnp.tanh(neg_log_a) * (1.0 + a * a))
        b = mult * (ig * u)

        outs = []
        for m in range(SUB_TILE // SUBLANES):
            sl = slice(m * SUBLANES, (m + 1) * SUBLANES)
            hstate = a[sl] * hstate + b[sl]
            outs.append(hstate)
        hs = jnp.concatenate(outs, axis=0)

        mix = _dot(_bf16(hs * ygate), wo_ref[...]) + bo_ref[...]
        x1 = x + _per_batch(mod_ref[2], SUB_TILE) * mix
        x1_ref[rows, :] = x1
        _post_mixer(x1, mod_ref, n2_ref, wrt_ref, brt_ref, hrow_ref, cls_ref, rank_ref, cnt_ref, sub)
    hcar[...] = hstate
    uext[0:hist, :] = uext[ts:ts + hist, :]
    retire_last()


def _lru_call(dest, x_prev, mod_prev, ys, mod, n1, n2, wy, by, wx, bx, cw, cb, wa, ba, wi, bi, lam,
              wo, bo, wrt, brt, expert_weights, layer):
    t, d = x_prev.shape
    n_tiles, _, ts = dest.shape
    bsz = mod.shape[1]
    hd = d // LRU_HEADS
    hist = SUBLANES * (LRU_CONV - 1)
    cast, cast_specs, cast_shapes, cast_out_specs = _expert_weight_cast(expert_weights, layer, n_tiles)
    out_shape, out_specs = _mixer_out(t, d, ts)
    out_shape += cast_shapes
    out_specs += cast_out_specs
    outs = pl.pallas_call(
        functools.partial(_lru_kernel, n_tiles, len(cast)),
        out_shape=out_shape,
        grid=(n_tiles,),
        in_specs=[
            pl.BlockSpec((1, 1, ts), lambda i: (i, 0, 0), memory_space=pltpu.SMEM),
            pl.BlockSpec((1, 1, ts), lambda i: (jnp.minimum(i + 1, n_tiles - 1), 0, 0),
                         memory_space=pltpu.SMEM),
            pl.BlockSpec((ts, d), lambda i: (i, 0)),
            _full((MOD_ROWS, bsz, d)), _full((MOD_ROWS, bsz, d)),
            _full((1, d)), _full((1, d)),
            _full((d, d)), _full((1, d)),
            _full((d, d)), _full((1, d)),
            _full((LRU_CONV, d)), _full((1, d)),
            _full((LRU_HEADS, hd, hd)), _full((1, d)),
            _full((LRU_HEADS, hd, hd)), _full((1, d)),
            _full((1, d)),
            _full((d, d)), _full((1, d)),
            _full((ROUTER_ROWS, d)), _full((ROUTER_ROWS, LANES)),
            pl.BlockSpec(memory_space=pl.ANY),
        ] + cast_specs,
        out_specs=out_specs,
        scratch_shapes=[
            pltpu.VMEM((hist + ts, d), jnp.float32),
            pltpu.VMEM((SUBLANES, d), jnp.float32),
            pltpu.VMEM((2, ts * SUBLANES, LANES), jnp.float32),
            pltpu.SemaphoreType.DMA((2,)),
        ],
        compiler_params=pltpu.CompilerParams(
            dimension_semantics=("arbitrary",), vmem_limit_bytes=VMEM_LIMIT_BYTES),
        name="lru_mixer",
    )(dest, dest, x_prev, mod_prev, mod, n1, n2, wy, by, wx, bx, cw, cb, wa, ba, wi, bi, lam, wo, bo,
      wrt, brt, ys, *cast)
    return outs[:5], _cast_results(outs[5:], expert_weights, layer)


def _plan_kernel(n_blocks, cnt_ref, clo_ref, chi_ref, cls_ref, rank_ref,
                 dest_ref, blo_ref, bhi_ref, bvalid_ref, nblk_ref, class_start):
    n_tiles, _, ts = cls_ref.shape
    class_start[...] = jnp.zeros_like(class_start)

    def per_class(c, nb_total):
        class_start[pl.ds(c, 1), :] = jnp.full((1, LANES), nb_total * ROW_BLOCK, jnp.int32)
        nb = (cnt_ref[c] + (ROW_BLOCK - 1)) // ROW_BLOCK

        def fill(k, carry):
            blo_ref[nb_total + k] = clo_ref[c]
            bhi_ref[nb_total + k] = chi_ref[c]
            bvalid_ref[nb_total + k] = jnp.minimum(cnt_ref[c] - k * ROW_BLOCK, ROW_BLOCK)
            return carry

        lax.fori_loop(0, nb, fill, 0)
        return nb_total + nb

    nb_used = lax.fori_loop(0, N_CLASSES, per_class, 0)
    nblk_ref[0] = nb_used
    last = jnp.maximum(nb_used - 1, 0)

    def tail(k, carry):
        blo_ref[k] = blo_ref[last]
        bhi_ref[k] = bhi_ref[last]
        bvalid_ref[k] = 0
        return carry

    lax.fori_loop(nb_used, n_blocks, tail, 0)

    starts = class_start[:, 0:1]
    class_id = lax.broadcasted_iota(jnp.int32, (LANES, ts), 0)

    def per_tile(t, carry):
        first = jnp.sum(jnp.where(class_id == cls_ref[t], starts, 0), axis=0, keepdims=True)
        dest_ref[t] = first + rank_ref[t]
        return carry

    lax.fori_loop(0, n_tiles, per_tile, 0)


def _plan_call(cls, rank, counts, n_blocks):
    n_tiles, _, ts = cls.shape
    smem_full = pl.BlockSpec(memory_space=pltpu.SMEM)
    vmem_full = pl.BlockSpec(memory_space=pltpu.VMEM)
    return pl.pallas_call(
        functools.partial(_plan_kernel, n_blocks),
        out_shape=(
            jax.ShapeDtypeStruct((n_tiles, 1, ts), jnp.int32),
            jax.ShapeDtypeStruct((n_blocks,), jnp.int32),
            jax.ShapeDtypeStruct((n_blocks,), jnp.int32),
            jax.ShapeDtypeStruct((n_blocks,), jnp.int32),
            jax.ShapeDtypeStruct((1,), jnp.int32),
        ),
        in_specs=[smem_full, smem_full, smem_full, vmem_full, vmem_full],
        out_specs=(vmem_full, smem_full, smem_full, smem_full, smem_full),
        scratch_shapes=[pltpu.VMEM((LANES, LANES), jnp.int32)],
        compiler_params=pltpu.CompilerParams(vmem_limit_bytes=VMEM_LIMIT_BYTES),
        name="moe_plan",
    )(counts, jnp.asarray(_CLASS_LO), jnp.asarray(_CLASS_HI), cls, rank)


def _dispatch_kernel(n_tiles, dest_ref, hrow_ref, xs_in_ref, xs_ref, stage, sem_in, sem_out):
    del xs_in_ref
    i = pl.program_id(0)
    ts = dest_ref.shape[2]
    rows = ts * SUBLANES

    def load(tile, slot):
        return pltpu.make_async_copy(hrow_ref.at[pl.ds(pl.multiple_of(tile * rows, rows), rows)],
                                     stage.at[slot], sem_in.at[slot])

    def tile_copy(slot, r, dst_row):
        return pltpu.make_async_copy(
            stage.at[slot, pl.ds(pl.multiple_of(r * SUBLANES, SUBLANES), SUBLANES)],
            xs_ref.at[pl.ds(pl.multiple_of(dst_row * SUBLANES, SUBLANES), SUBLANES)],
            sem_out.at[slot])

    def drain(slot):
        _wait_all_tiles(xs_ref, stage.at[slot], sem_out.at[slot], to_hbm=True)

    slot = i % DISPATCH_SLOTS
    nxt = (i + 1) % DISPATCH_SLOTS

    @pl.when(i == 0)
    def _():
        load(0, 0).start()

    @pl.when(i >= 2)
    def _():
        drain(nxt)

    @pl.when(i + 1 < n_tiles)
    def _():
        load(i + 1, nxt).start()

    load(i, slot).wait()

    _start_tile_dmas(lambda r: tile_copy(slot, r, dest_ref[0, 0, r]), ts, both_queues=True)

    @pl.when(i == n_tiles - 1)
    def _():
        @pl.when(i >= 1)
        def _():
            drain((i + DISPATCH_SLOTS - 1) % DISPATCH_SLOTS)
        drain(slot)


def _dispatch_call(dest, hrows, xs_init):
    n_tiles, _, ts = dest.shape
    assert xs_init.dtype == jnp.uint32 and xs_init.shape[1] == LANES
    any_spec = pl.BlockSpec(memory_space=pl.ANY)
    return pl.pallas_call(
        functools.partial(_dispatch_kernel, n_tiles),
        out_shape=jax.ShapeDtypeStruct(xs_init.shape, jnp.uint32),
        grid=(n_tiles,),
        in_specs=[
            pl.BlockSpec((1, 1, ts), lambda i: (i, 0, 0), memory_space=pltpu.SMEM),
            any_spec, any_spec,
        ],
        out_specs=any_spec,
        scratch_shapes=[
            pltpu.VMEM((DISPATCH_SLOTS, ts * SUBLANES, LANES), jnp.uint32),
            pltpu.SemaphoreType.DMA((DISPATCH_SLOTS,)),
            pltpu.SemaphoreType.DMA((DISPATCH_SLOTS,)),
        ],
        input_output_aliases={2: 0},
        compiler_params=pltpu.CompilerParams(
            dimension_semantics=("arbitrary",), vmem_limit_bytes=VMEM_LIMIT_BYTES),
        name="moe_dispatch",
    )(dest, hrows, xs_init)


def _expert_kernel(blo_ref, bhi_ref, bvalid_ref, nblk_ref, xs_ref,
                   wg_lo, wu_lo, wd_lo, wg_hi, wu_hi, wd_hi, ys_ref):
    del blo_ref, bhi_ref, nblk_ref
    valid = bvalid_ref[pl.program_id(0)]

    def experts_on(m):
        pieces = []
        for j in range(PACK_SUBLANES):
            w = _token_rows(xs_ref, j, m)
            pieces.append(_bf16(_f32_from_bits(w & jnp.uint32(0xFFFF0000))))
            pieces.append(_bf16(_f32_from_bits(lax.shift_left(w, jnp.uint32(16)))))
        xb = jnp.concatenate(pieces, axis=1)
        gates = _f32_from_bits(_token_rows(xs_ref, PACK_SUBLANES, m))
        acc = None
        for idx, (wg, wu, wd) in enumerate(((wg_lo, wu_lo, wd_lo), (wg_hi, wu_hi, wd_hi))):
            a = _dot(xb, wg[...])
            hid = (a * jax.nn.sigmoid(a)) * _dot(xb, wu[...]) * gates[:, idx:idx + 1]
            part = _dot(_bf16(hid), wd[...])
            acc = part if acc is None else acc + part
        for s in range(SUBLANES):
            ys_ref[pl.ds(s, m, stride=SUBLANES), :] = acc[:, s * LANES:(s + 1) * LANES]

    for m in range(ROW_STEP, ROW_BLOCK + 1, ROW_STEP):
        pl.when((valid > m - ROW_STEP) & (valid <= m))(functools.partial(experts_on, m))


def _expert_call(blo, bhi, bvalid, nblk, xs, wg, wu, wd):
    n_rows = xs.shape[0] // SUBLANES
    d, de = wg.shape[1], wg.shape[2]
    assert d == SUBLANES * LANES
    n_blocks = n_rows // ROW_BLOCK

    def w_spec(shape, which):
        if which == 0:
            return pl.BlockSpec((None,) + shape, lambda b, lo, hi, bv, n: (lo[b], 0, 0))
        return pl.BlockSpec((None,) + shape, lambda b, lo, hi, bv, n: (hi[b], 0, 0))

    def row_map(b, lo, hi, bv, n):
        return (jnp.minimum(b, jnp.maximum(n[0] - 1, 0)), 0)

    grid_spec = pltpu.PrefetchScalarGridSpec(
        num_scalar_prefetch=4,
        grid=(n_blocks,),
        in_specs=[
            pl.BlockSpec((ROW_BLOCK * SUBLANES, LANES), row_map),
            w_spec((d, de), 0), w_spec((d, de), 0), w_spec((de, d), 0),
            w_spec((d, de), 1), w_spec((d, de), 1), w_spec((de, d), 1),
        ],
        out_specs=pl.BlockSpec((ROW_BLOCK * SUBLANES, LANES), row_map),
    )
    return pl.pallas_call(
        _expert_kernel,
        out_shape=jax.ShapeDtypeStruct((n_rows * SUBLANES, LANES), jnp.float32),
        grid_spec=grid_spec,
        compiler_params=pltpu.CompilerParams(
            dimension_semantics=("arbitrary",), vmem_limit_bytes=VMEM_LIMIT_BYTES),
        name="moe_experts",
    )(blo, bhi, bvalid, nblk, xs, wg, wu, wd, wg, wu, wd)


def _final_kernel(n_tiles, dest_ref, dnext_ref, x1_ref, mod_ref, fmod_ref, fg_ref, ys_ref,
                  out_ref, ybuf, sem, unperm):
    i = pl.program_id(0)
    ts, d = x1_ref.shape
    bsz, tm, _ = out_ref.shape
    y, retire_last = _gather_expert_tiles(ys_ref, ybuf, sem, dest_ref, dnext_ref, i, n_tiles, ts,
                                          inline=False)
    x2 = x1_ref[...] + _per_batch(mod_ref[5], ts) * y
    res = _rms_mod(x2, fg_ref[...], fmod_ref[0], fmod_ref[1])
    for cb in range(d // LANES):
        unperm[cb] = res[:, cb * LANES:(cb + 1) * LANES]
    for cb in range(d // LANES):
        for p in range(bsz):
            out_ref[p, :, cb * LANES:(cb + 1) * LANES] = unperm.at[cb][pl.ds(p, tm, stride=SUBLANES), :]
    retire_last()


def _final_call(dest, x1, mod, ys, bsz, fmod, fin_g):
    t, d = x1.shape
    n_tiles, _, ts = dest.shape
    tm = ts // bsz
    return pl.pallas_call(
        functools.partial(_final_kernel, n_tiles),
        out_shape=jax.ShapeDtypeStruct((bsz, t // bsz, d), jnp.float32),
        grid=(n_tiles,),
        in_specs=[
            pl.BlockSpec((1, 1, ts), lambda i: (i, 0, 0), memory_space=pltpu.SMEM),
            pl.BlockSpec((1, 1, ts), lambda i: (jnp.minimum(i + 1, n_tiles - 1), 0, 0),
                         memory_space=pltpu.SMEM),
            pl.BlockSpec((ts, d), lambda i: (i, 0)),
            _full((MOD_ROWS, bsz, d)), _full((MOD_ROWS, bsz, d)), _full((1, d)),
            pl.BlockSpec(memory_space=pl.ANY),
        ],
        out_specs=pl.BlockSpec((bsz, tm, d), lambda i: (0, i, 0)),
        scratch_shapes=[pltpu.VMEM((2, ts * SUBLANES, LANES), jnp.float32),
                        pltpu.SemaphoreType.DMA((2,)),
                        pltpu.VMEM((d // LANES, ts, LANES), jnp.float32)],
        compiler_params=pltpu.CompilerParams(
            dimension_semantics=("arbitrary",), vmem_limit_bytes=VMEM_LIMIT_BYTES),
        name="moe_combine_final",
    )(dest, dest, x1, mod, fmod, fin_g, ys)


def _router_operands(wg, bg, wr, br):
    d = wg.shape[0]
    pad_g = jnp.zeros((SUBLANES - N_GROUPS, d), jnp.float32)
    pad_e = jnp.zeros((ROUTER_ROWS - SUBLANES - N_EXPERTS, d), jnp.float32)
    wrt = jnp.concatenate([wg.T, pad_g, wr.T, pad_e], axis=0)
    bcol = jnp.concatenate([bg, jnp.zeros((SUBLANES - N_GROUPS,), jnp.float32), br,
                            jnp.zeros((ROUTER_ROWS - SUBLANES - N_EXPERTS,), jnp.float32)])
    brt = jnp.broadcast_to(bcol[:, None], (ROUTER_ROWS, LANES))
    return _bf16(wrt), brt


def _mod_table(m, parts):
    bsz = m.shape[0]
    d = m.shape[1] // parts
    m = jnp.transpose(m.reshape(bsz, parts, d), (1, 0, 2))
    return jnp.concatenate([m, jnp.zeros((MOD_ROWS - parts, bsz, d), jnp.float32)], axis=0)


def _row(v):
    return v.reshape(1, -1)


def _moe_experts(hrows, cls, rank, cnt, weights_bf16, n_blocks, xs_init):
    counts = cnt[:, 0].astype(jnp.int32)
    dest, blo, bhi, bvalid, nblk = _plan_call(cls, rank, counts, n_blocks)
    xs = _dispatch_call(dest, hrows, xs_init)
    ys = _expert_call(blo, bhi, bvalid, nblk, xs, *weights_bf16)
    return xs, ys, dest


def kernel(x, c, ada_w, ada_b, norm1_g, norm2_g, cm_w1, cm_b1, cm_dw, cm_dwb, cm_ln_g, cm_ln_b, cm_w2, cm_b2, rg_wy, rg_by, rg_wx, rg_bx, rg_cw, rg_cb, rg_wa, rg_ba, rg_wi, rg_bi, rg_lambda, rg_wo, rg_bo, moe_wg, moe_bg, moe_wr, moe_br, moe_w_gate, moe_w_up, moe_w_down, fin_ada_w, fin_ada_b, fin_g):
    bsz, seq, d = x.shape
    t = bsz * seq
    depth = ada_w.shape[0]
    assert depth == 2 and bsz == SUBLANES and seq % TIME_TILE == 0
    assert d == 2 * PACK_SUBLANES * LANES

    mods = _ada_call(c, ada_w, ada_b[:, None, :])
    fmods = _ada_call(c, fin_ada_w[None], fin_ada_b[None, None, :])
    fmod = _mod_table(fmods[0], 2)
    n_steps = seq // TIME_TILE
    granule = n_steps // math.gcd(n_steps, ROW_BLOCK)
    n_blocks = -(-(-(-t // ROW_BLOCK) + N_CLASSES) // granule) * granule
    mod0, mod1 = _mod_table(mods[0], 6), _mod_table(mods[1], 6)

    wrt, brt = _router_operands(moe_wg[0], moe_bg[0], moe_wr[0], moe_br[0])
    dw = jnp.concatenate([cm_dw[0], jnp.zeros((1, d), jnp.float32)], axis=0)
    dw = jnp.transpose(dw.reshape(CONF_KERNEL + 1, d // LANES, LANES), (1, 0, 2))
    expert_weights = (moe_w_gate, moe_w_up, moe_w_down)
    (x1, hrows, cls, rank, cnt, xs_init), weights0 = _conf_call(
        x, mod0, _row(norm1_g[0]), _row(norm2_g[0]), _bf16(cm_w1[0]), _row(cm_b1[0]),
        dw, cm_dwb[0].reshape(d // LANES, 1, LANES), _row(cm_ln_g[0]), _row(cm_ln_b[0]),
        _bf16(cm_w2[0]), _row(cm_b2[0]), wrt, brt, n_blocks * ROW_BLOCK * SUBLANES,
        expert_weights, 0)
    xs, ys, dest = _moe_experts(hrows, cls, rank, cnt, weights0, n_blocks, xs_init)

    wrt, brt = _router_operands(moe_wg[1], moe_bg[1], moe_wr[1], moe_br[1])
    (x1, hrows, cls, rank, cnt), weights1 = _lru_call(
        dest, x1, mod0, ys, mod1, _row(norm1_g[1]), _row(norm2_g[1]),
        _bf16(rg_wy[0]), _row(rg_by[0]), _bf16(rg_wx[0]), _row(rg_bx[0]),
        rg_cw[0], _row(rg_cb[0]), _bf16(rg_wa[0]), _row(rg_ba[0]), _bf16(rg_wi[0]), _row(rg_bi[0]),
        _row(rg_lambda[0]), _bf16(rg_wo[0]), _row(rg_bo[0]), wrt, brt, expert_weights, 1)
    _, ys, dest = _moe_experts(hrows, cls, rank, cnt, weights1, n_blocks, xs)
    return _final_call(dest, x1, mod1, ys, bsz, fmod, _row(fin_g))
```

```python
import functools
import math

import jax
import jax.numpy as jnp
import numpy as np
from jax import lax
from jax.experimental import pallas as pl
from jax.experimental.pallas import tpu as pltpu

EPS = 1e-6
CONF_KERNEL = 31
LRU_HEADS = 4
LRU_CONV = 4
LRU_C = 8.0
N_GROUPS = 4
EXPERTS_PER_GROUP = 8
N_EXPERTS = N_GROUPS * EXPERTS_PER_GROUP
PAIRS_PER_GROUP = EXPERTS_PER_GROUP * (EXPERTS_PER_GROUP - 1) // 2
N_CLASSES = N_GROUPS * PAIRS_PER_GROUP

LANES = 128
SUBLANES = 8
BF16_SUBLANES = 16
VMEM_LIMIT_BYTES = 56 * 1024 * 1024

TOKEN_TILE = 512
TIME_TILE = TOKEN_TILE // SUBLANES
SUB_TILE = 256
ROW_BLOCK = 512
ROW_STEP = 128
PERM_PITCH = TIME_TILE + SUBLANES
CONV_ROWS = 128
ROUTER_ROWS = 64
PACK_SUBLANES = 4
MOD_ROWS = 8
DISPATCH_SLOTS = 3
DMA_UNROLL = 32

_NEG_INF = float("-inf")


def _class_tables():
    lo = np.zeros((N_CLASSES,), np.int32)
    hi = np.zeros((N_CLASSES,), np.int32)
    for g in range(N_GROUPS):
        for a in range(EXPERTS_PER_GROUP):
            for b in range(a + 1, EXPERTS_PER_GROUP):
                c = g * PAIRS_PER_GROUP + (a * (2 * EXPERTS_PER_GROUP - 1 - a)) // 2 + (b - a - 1)
                lo[c] = g * EXPERTS_PER_GROUP + a
                hi[c] = g * EXPERTS_PER_GROUP + b
    return lo, hi


_CLASS_LO, _CLASS_HI = _class_tables()


def _bf16(x):
    return x.astype(jnp.bfloat16)


def _dot(a, b):
    return jnp.dot(a, b, preferred_element_type=jnp.float32)


def _bits(x):
    return lax.bitcast_convert_type(x, jnp.uint32)


def _f32_from_bits(x):
    return lax.bitcast_convert_type(x, jnp.float32)


def _gelu_tanh(x):
    c0 = math.sqrt(2.0 / math.pi)
    t = jnp.tanh(x * (c0 + (c0 * 0.044715) * (x * x)))
    hx = 0.5 * x
    return hx + hx * t


def _token_rows(ref, sublane, n):
    return ref[pl.ds(sublane, n, stride=SUBLANES), :]


def _per_batch(v, ts):
    return jnp.tile(v, (ts // SUBLANES, 1))


def _ada_kernel(c_ref, w_ref, b_ref, o_ref):
    c = c_ref[...]
    ca = _bf16(c * jax.nn.sigmoid(c))
    o_ref[...] = _dot(ca, _bf16(w_ref[...])) + b_ref[...]


def _ada_call(c, w, b):
    n_layers, d, n = w.shape
    bsz = c.shape[0]
    nb = 1024
    return pl.pallas_call(
        _ada_kernel,
        out_shape=jax.ShapeDtypeStruct((n_layers, bsz, n), jnp.float32),
        grid=(n_layers, n // nb),
        in_specs=[
            pl.BlockSpec((bsz, d), lambda l, j: (0, 0)),
            pl.BlockSpec((None, d, nb), lambda l, j: (l, 0, j)),
            pl.BlockSpec((None, 1, nb), lambda l, j: (l, 0, j)),
        ],
        out_specs=pl.BlockSpec((None, bsz, nb), lambda l, j: (l, 0, j)),
        compiler_params=pltpu.CompilerParams(
            dimension_semantics=("arbitrary", "arbitrary"),
            vmem_limit_bytes=VMEM_LIMIT_BYTES),
        name="ada_mod",
    )(c, w, b)


def _rms_mod(x, gain, shift, scale):
    ts = x.shape[0]
    ms = jnp.mean(x * x, axis=-1, keepdims=True)
    return (x * lax.rsqrt(ms + EPS)) * _per_batch(gain * (1.0 + scale), ts) + _per_batch(shift, ts)


def _causal_taps(w_ref, bias_ref, ext_ref, n_taps, row0, n):
    d = ext_ref.shape[1]
    cols = []
    for cb in range(d // LANES):
        sl = slice(cb * LANES, (cb + 1) * LANES)
        acc = jnp.broadcast_to(bias_ref[:, sl], (n, LANES))
        for k in range(n_taps):
            lo = row0 + SUBLANES * k
            acc = acc + w_ref[k:k + 1, sl] * ext_ref[lo:lo + n, sl]
        cols.append(acc)
    return jnp.concatenate(cols, axis=1)


def _causal_taps_unit(w_ref, bias_ref, ext_ref, out_ref, cb, r0, n_taps):
    w_cb, ext_cb = w_ref.at[cb], ext_ref.at[cb]
    acc = jnp.broadcast_to(bias_ref[cb], (CONV_ROWS, LANES))
    for k in range(n_taps):
        win = ext_cb[pl.ds(pl.multiple_of(r0 + SUBLANES * k, SUBLANES), CONV_ROWS), :]
        acc = acc + w_cb[k:k + 1, :] * win
    out_ref.at[cb][pl.ds(r0, CONV_ROWS), :] = acc


def _start_tile_dmas(make_copy, n, inline=False, both_queues=False):
    if inline:
        for r in range(n):
            make_copy(r).start(priority=r % 2 if both_queues else 0)
        return

    def body(g, carry):
        for k in range(DMA_UNROLL):
            make_copy(g * DMA_UNROLL + k).start(priority=k % 2 if both_queues else 0)
        return carry
    lax.fori_loop(0, n // DMA_UNROLL, body, 0)


def _wait_all_tiles(hbm_ref, vmem_tiles_ref, sem, to_hbm):
    hbm_view = hbm_ref.at[pl.ds(0, vmem_tiles_ref.shape[0])]
    if to_hbm:
        pltpu.make_async_copy(vmem_tiles_ref, hbm_view, sem).wait()
    else:
        pltpu.make_async_copy(hbm_view, vmem_tiles_ref, sem).wait()


def _gather_expert_tiles(ys_ref, ybuf, sem, dest_ref, dnext_ref, step, n_steps, ts, inline):
    slot = step % 2

    def tile_copy(dref, sl):
        def make(r):
            src = pl.multiple_of(dref[0, 0, r] * SUBLANES, SUBLANES)
            return pltpu.make_async_copy(
                ys_ref.at[pl.ds(src, SUBLANES)],
                ybuf.at[sl, pl.ds(pl.multiple_of(r * SUBLANES, SUBLANES), SUBLANES)],
                sem.at[sl])
        return make

    @pl.when(step == 0)
    def _():
        _start_tile_dmas(tile_copy(dest_ref, 0), ts)

    if not inline:
        @pl.when(step + 1 < n_steps)
        def _():
            _start_tile_dmas(tile_copy(dnext_ref, 1 - slot), ts)

    _wait_all_tiles(ys_ref, ybuf.at[slot], sem.at[slot], to_hbm=False)
    if inline:
        _start_tile_dmas(tile_copy(dnext_ref, 1 - slot), ts, inline=True)
    yv = ybuf.at[slot]
    y = jnp.concatenate([_token_rows(yv, s, ts) for s in range(SUBLANES)], axis=1)

    def retire_last():
        if inline:
            @pl.when(step + 1 == n_steps)
            def _():
                _wait_all_tiles(ys_ref, ybuf.at[1 - slot], sem.at[1 - slot], to_hbm=False)

    return y, retire_last


def _route(logits_t):
    ts = logits_t.shape[1]
    row = lax.broadcasted_iota(jnp.int32, (SUBLANES, ts), 0)
    gl = jnp.where(row < N_GROUPS, logits_t[0:SUBLANES], _NEG_INF)
    gmax = jnp.max(gl, axis=0, keepdims=True)
    gsel = jnp.min(jnp.where(gl == gmax, row, SUBLANES), axis=0, keepdims=True)
    p_g = 1.0 / jnp.sum(jnp.exp(gl - gmax), axis=0, keepdims=True)
    es = jnp.zeros((SUBLANES, ts), jnp.float32)
    for g in range(N_GROUPS):
        lo = SUBLANES * (1 + g)
        es = jnp.where(gsel == g, logits_t[lo:lo + EXPERTS_PER_GROUP], es)
    m1 = jnp.max(es, axis=0, keepdims=True)
    i1 = jnp.min(jnp.where(es == m1, row, SUBLANES), axis=0, keepdims=True)
    es2 = jnp.where(row == i1, _NEG_INF, es)
    m2 = jnp.max(es2, axis=0, keepdims=True)
    i2 = jnp.min(jnp.where(es2 == m2, row, SUBLANES), axis=0, keepdims=True)
    t = jnp.exp(m2 - m1)
    w1 = p_g / (1.0 + t)
    w2 = w1 * t
    first_lo = i1 < i2
    e_lo = jnp.minimum(i1, i2)
    e_hi = jnp.maximum(i1, i2)
    gate_lo = jnp.where(first_lo, w1, w2)
    gate_hi = jnp.where(first_lo, w2, w1)
    pair = lax.shift_right_logical(e_lo * (2 * EXPERTS_PER_GROUP - 1 - e_lo), 1) + (e_hi - e_lo - 1)
    cls = gsel * PAIRS_PER_GROUP + pair
    return cls, gate_lo, gate_hi


def _post_mixer(x1, mod_ref, n2_ref, wrt_ref, brt_ref, hrow_ref, cls_ref, rank_ref, cnt_ref, sub):
    ts = x1.shape[0]
    hrow_ref = hrow_ref.at[pl.ds(sub * ts * SUBLANES, ts * SUBLANES)]
    cols = slice(sub * ts, (sub + 1) * ts)
    h2 = _rms_mod(x1, n2_ref[...], mod_ref[3], mod_ref[4])
    h2b = _bf16(h2)
    logits_t = lax.dot_general(wrt_ref[...], h2b, (((1,), (1,)), ((), ())),
                               preferred_element_type=jnp.float32) + brt_ref[:, 0:1]
    cls, gate_lo, gate_hi = _route(logits_t)
    gates_t = jnp.concatenate(
        [gate_lo, gate_hi, jnp.zeros((LANES - 2, ts), jnp.float32)], axis=0)
    h2r = h2b.astype(jnp.float32)
    for j in range(PACK_SUBLANES):
        hi = _bits(h2r[:, (2 * j) * LANES:(2 * j + 1) * LANES])
        lo = _bits(h2r[:, (2 * j + 1) * LANES:(2 * j + 2) * LANES])
        hrow_ref[pl.ds(j, ts, stride=SUBLANES), :] = hi | lax.shift_right_logical(lo, jnp.uint32(16))
    hrow_ref[pl.ds(PACK_SUBLANES, ts, stride=SUBLANES), :] = _bits(gates_t.T)
    for j in range(PACK_SUBLANES + 1, SUBLANES):
        hrow_ref[pl.ds(j, ts, stride=SUBLANES), :] = jnp.zeros((ts, LANES), jnp.uint32)
    cls_ref[0, :, cols] = cls
    crow = lax.broadcasted_iota(jnp.int32, (LANES, ts), 0)
    onehot = jnp.where(crow == cls, 1.0, 0.0)
    cnt = jnp.sum(onehot, axis=1, keepdims=True)
    earlier = (lax.broadcasted_iota(jnp.int32, (ts, ts), 0)
               < lax.broadcasted_iota(jnp.int32, (ts, ts), 1))
    prefix = _dot(_bf16(onehot), _bf16(jnp.where(earlier, 1.0, 0.0)))
    rank = jnp.sum(onehot * (prefix + cnt_ref[:, 0:1]), axis=0, keepdims=True)
    rank_ref[0, :, cols] = rank.astype(jnp.int32)
    cnt_ref[...] += jnp.broadcast_to(cnt, cnt_ref.shape)


def _mixer_out(t, d, ts):
    n_tiles = t // ts
    out_shape = (
        jax.ShapeDtypeStruct((t, d), jnp.float32),
        jax.ShapeDtypeStruct((t * SUBLANES, LANES), jnp.uint32),
        jax.ShapeDtypeStruct((n_tiles, 1, ts), jnp.int32),
        jax.ShapeDtypeStruct((n_tiles, 1, ts), jnp.int32),
        jax.ShapeDtypeStruct((LANES, LANES), jnp.float32),
    )
    out_specs = (
        pl.BlockSpec((ts, d), lambda i: (i, 0)),
        pl.BlockSpec((ts * SUBLANES, LANES), lambda i: (i, 0)),
        pl.BlockSpec((1, 1, ts), lambda i: (i, 0, 0)),
        pl.BlockSpec((1, 1, ts), lambda i: (i, 0, 0)),
        pl.BlockSpec((LANES, LANES), lambda i: (0, 0)),
    )
    return out_shape, out_specs


def _full(shape):
    return pl.BlockSpec(shape, lambda i: tuple(0 for _ in shape))


def _expert_weight_cast(weights, layer, n_steps):
    flat = [w.reshape(w.shape[0], w.shape[1] * w.shape[2], w.shape[3]) for w in weights]
    if any(w.shape[1] % (n_steps * BF16_SUBLANES) for w in flat):
        return [], [], (), ()
    in_specs = [pl.BlockSpec((None, w.shape[1] // n_steps, w.shape[2]), lambda i: (layer, i, 0))
                for w in flat]
    out_shapes = tuple(jax.ShapeDtypeStruct(w.shape[1:], jnp.bfloat16) for w in flat)
    out_specs = tuple(pl.BlockSpec((w.shape[1] // n_steps, w.shape[2]), lambda i: (i, 0))
                      for w in flat)
    return flat, in_specs, out_shapes, out_specs


def _cast_slices(cast_in, cast_out):
    for src, dst in zip(cast_in, cast_out):
        dst[...] = _bf16(src[...])


def _cast_results(outs, weights, layer):
    if outs:
        return [o.reshape(w.shape[1:]) for o, w in zip(outs, weights)]
    return [_bf16(w[layer]) for w in weights]


def _to_tile_order(x_ref, perm, xs):
    bsz, tm, d = x_ref.shape
    for cb in range(d // LANES):
        for p in range(bsz):
            perm[cb, PERM_PITCH * p:PERM_PITCH * p + tm, :] = x_ref[p, :, cb * LANES:(cb + 1) * LANES]
    for m in range(tm):
        for cb in range(d // LANES):
            xs[m * bsz:(m + 1) * bsz, cb * LANES:(cb + 1) * LANES] = (
                perm.at[cb][pl.ds(m, bsz, stride=PERM_PITCH), :])


def _conf_kernel(n_cast, x_ref, mod_ref, n1_ref, n2_ref, w1_ref, b1_ref, dw_ref, dwb_ref,
                 lng_ref, lnb_ref, w2_ref, b2_ref, wrt_ref, brt_ref, *rest):
    cast_in, rest = rest[:n_cast], rest[n_cast:]
    x1_ref, hrow_ref, cls_ref, rank_ref, cnt_ref, zeros_ref = rest[:6]
    cast_out, (perm, xs, vext, conv) = rest[6:6 + n_cast], rest[6 + n_cast:]
    _cast_slices(cast_in, cast_out)
    i = pl.program_id(0)
    bsz, tm, d = x_ref.shape
    ts = bsz * tm
    n_chunks = d // LANES
    hist = SUBLANES * (CONF_KERNEL - 1)

    @pl.when(i == 0)
    def _():
        vext[:, 0:hist, :] = jnp.zeros((n_chunks, hist, LANES), jnp.float32)
        cnt_ref[...] = jnp.zeros_like(cnt_ref)

    zeros_ref[...] = jnp.zeros_like(zeros_ref)

    _to_tile_order(x_ref, perm, xs)
    for sub in range(ts // SUB_TILE):
        rows = slice(sub * SUB_TILE, (sub + 1) * SUB_TILE)
        h = _bf16(_rms_mod(xs[rows, :], n1_ref[...], mod_ref[0], mod_ref[1]))
        u = _dot(h, w1_ref[...]) + b1_ref[...]
        v = u[:, 0:d] * jax.nn.sigmoid(u[:, d:2 * d])
        for cb in range(n_chunks):
            vext[cb, hist + sub * SUB_TILE:hist + (sub + 1) * SUB_TILE, :] = v[:, cb * LANES:(cb + 1) * LANES]

    units_per_chunk = ts // CONV_ROWS

    def conv_unit(idx, carry):
        r0 = pl.multiple_of((idx % units_per_chunk) * CONV_ROWS, CONV_ROWS)
        _causal_taps_unit(dw_ref, dwb_ref, vext, conv, idx // units_per_chunk, r0, CONF_KERNEL)
        return carry

    lax.fori_loop(0, n_chunks * units_per_chunk, conv_unit, 0)
    vext[:, 0:hist, :] = vext[:, ts:ts + hist, :]

    for sub in range(ts // SUB_TILE):
        rows = slice(sub * SUB_TILE, (sub + 1) * SUB_TILE)
        y = jnp.concatenate([conv[cb, rows, :] for cb in range(n_chunks)], axis=1)
        mu = jnp.mean(y, axis=-1, keepdims=True)
        yc = y - mu
        var = jnp.mean(yc * yc, axis=-1, keepdims=True)
        z = yc * lax.rsqrt(var + EPS) * lng_ref[...] + lnb_ref[...]
        z = z * jax.nn.sigmoid(z)
        mix = _dot(_bf16(z), w2_ref[...]) + b2_ref[...]
        x1 = xs[rows, :] + _per_batch(mod_ref[2], SUB_TILE) * mix
        x1_ref[rows, :] = x1
        _post_mixer(x1, mod_ref, n2_ref, wrt_ref, brt_ref, hrow_ref, cls_ref, rank_ref, cnt_ref, sub)


def _conf_call(x, mod, n1, n2, w1, b1, dw, dwb, lng, lnb, w2, b2, wrt, brt, zero_rows,
               expert_weights, layer):
    bsz, seq, d = x.shape
    tm = TIME_TILE
    ts = bsz * tm
    n_steps = seq // tm
    assert zero_rows % (n_steps * SUBLANES) == 0
    cast, cast_specs, cast_shapes, cast_out_specs = _expert_weight_cast(expert_weights, layer, n_steps)
    out_shape, out_specs = _mixer_out(bsz * seq, d, ts)
    out_shape += (jax.ShapeDtypeStruct((zero_rows, LANES), jnp.uint32),) + cast_shapes
    out_specs += (pl.BlockSpec((zero_rows // n_steps, LANES), lambda i: (i, 0)),) + cast_out_specs
    hist = SUBLANES * (CONF_KERNEL - 1)
    outs = pl.pallas_call(
        functools.partial(_conf_kernel, len(cast)),
        out_shape=out_shape,
        grid=(n_steps,),
        in_specs=[
            pl.BlockSpec((bsz, tm, d), lambda i: (0, i, 0)),
            _full((MOD_ROWS, bsz, d)),
            _full((1, d)), _full((1, d)),
            _full((d, 2 * d)), _full((1, 2 * d)),
            _full((d // LANES, CONF_KERNEL + 1, LANES)), _full((d // LANES, 1, LANES)),
            _full((1, d)), _full((1, d)),
            _full((d, d)), _full((1, d)),
            _full((ROUTER_ROWS, d)), _full((ROUTER_ROWS, LANES)),
        ] + cast_specs,
        out_specs=out_specs,
        scratch_shapes=[
            pltpu.VMEM((d // LANES, bsz * PERM_PITCH, LANES), jnp.float32),
            pltpu.VMEM((ts, d), jnp.float32),
            pltpu.VMEM((d // LANES, hist + ts, LANES), jnp.float32),
            pltpu.VMEM((d // LANES, ts, LANES), jnp.float32),
        ],
        compiler_params=pltpu.CompilerParams(
            dimension_semantics=("arbitrary",), vmem_limit_bytes=VMEM_LIMIT_BYTES),
        name="conf_mixer",
    )(x, mod, n1, n2, w1, b1, dw, dwb, lng, lnb, w2, b2, wrt, brt, *cast)
    return outs[:6], _cast_results(outs[6:], expert_weights, layer)


def _lru_kernel(n_tiles, n_cast, dest_ref, dnext_ref, xprev_ref, pmod_ref, mod_ref, n1_ref, n2_ref,
                wy_ref, by_ref, wx_ref, bx_ref,
                cw_ref, cb_ref, wa_ref, ba_ref, wi_ref, bi_ref, lam_ref, wo_ref, bo_ref,
                wrt_ref, brt_ref, ys_ref, *rest):
    cast_in, rest = rest[:n_cast], rest[n_cast:]
    x1_ref, hrow_ref, cls_ref, rank_ref, cnt_ref = rest[:5]
    cast_out, (uext, hcar, ybuf, ysem) = rest[5:5 + n_cast], rest[5 + n_cast:]
    _cast_slices(cast_in, cast_out)
    i = pl.program_id(0)
    ts, d = xprev_ref.shape
    hd = d // LRU_HEADS
    hist = SUBLANES * (LRU_CONV - 1)

    @pl.when(i == 0)
    def _():
        uext[0:hist, :] = jnp.zeros((hist, d), jnp.float32)
        hcar[...] = jnp.zeros_like(hcar)
        cnt_ref[...] = jnp.zeros_like(cnt_ref)

    y_prev, retire_last = _gather_expert_tiles(ys_ref, ybuf, ysem, dest_ref, dnext_ref, i, n_tiles, ts,
                                               inline=True)
    nl = -lam_ref[...]
    decay = LRU_C * (jnp.maximum(nl, 0.0) + jnp.log1p(jnp.exp(-jnp.abs(nl))))
    hstate = hcar[...]
    for sub in range(ts // SUB_TILE):
        rows = slice(sub * SUB_TILE, (sub + 1) * SUB_TILE)
        x = xprev_ref[rows, :] + _per_batch(pmod_ref[5], SUB_TILE) * y_prev[rows, :]
        h = _bf16(_rms_mod(x, n1_ref[...], mod_ref[0], mod_ref[1]))
        ygate = _gelu_tanh(_dot(h, wy_ref[...]) + by_ref[...])
        xb = _dot(h, wx_ref[...]) + bx_ref[...]
        uext[hist + sub * SUB_TILE:hist + (sub + 1) * SUB_TILE, :] = xb
        u = _causal_taps(cw_ref, cb_ref, uext, LRU_CONV, sub * SUB_TILE, SUB_TILE)

        ub = _bf16(u)
        r_parts, i_parts = [], []
        for hh in range(LRU_HEADS):
            uh = ub[:, hh * hd:(hh + 1) * hd]
            r_parts.append(_dot(uh, wa_ref[hh]))
            i_parts.append(_dot(uh, wi_ref[hh]))
        r = jax.nn.sigmoid(jnp.concatenate(r_parts, axis=1) + ba_ref[...])
        ig = jax.nn.sigmoid(jnp.concatenate(i_parts, axis=1) + bi_ref[...])

        neg_log_a = r * decay
        a = jnp.exp(-neg_log_a)
        mult = jnp.sqrt(jnp.tanh(neg_log_a) * (1.0 + a * a))
        b = mult * (ig * u)

        outs = []
        for m in range(SUB_TILE // SUBLANES):
            sl = slice(m * SUBLANES, (m + 1) * SUBLANES)
            hstate = a[sl] * hstate + b[sl]
            outs.append(hstate)
        hs = jnp.concatenate(outs, axis=0)

        mix = _dot(_bf16(hs * ygate), wo_ref[...]) + bo_ref[...]
        x1 = x + _per_batch(mod_ref[2], SUB_TILE) * mix
        x1_ref[rows, :] = x1
        _post_mixer(x1, mod_ref, n2_ref, wrt_ref, brt_ref, hrow_ref, cls_ref, rank_ref, cnt_ref, sub)
    hcar[...] = hstate
    uext[0:hist, :] = uext[ts:ts + hist, :]
    retire_last()


def _lru_call(dest, x_prev, mod_prev, ys, mod, n1, n2, wy, by, wx, bx, cw, cb, wa, ba, wi, bi, lam,
              wo, bo, wrt, brt, expert_weights, layer):
    t, d = x_prev.shape
    n_tiles, _, ts = dest.shape
    bsz = mod.shape[1]
    hd = d // LRU_HEADS
    hist = SUBLANES * (LRU_CONV - 1)
    cast, cast_specs, cast_shapes, cast_out_specs = _expert_weight_cast(expert_weights, layer, n_tiles)
    out_shape, out_specs = _mixer_out(t, d, ts)
    out_shape += cast_shapes
    out_specs += cast_out_specs
    outs = pl.pallas_call(
        functools.partial(_lru_kernel, n_tiles, len(cast)),
        out_shape=out_shape,
        grid=(n_tiles,),
        in_specs=[
            pl.BlockSpec((1, 1, ts), lambda i: (i, 0, 0), memory_space=pltpu.SMEM),
            pl.BlockSpec((1, 1, ts), lambda i: (jnp.minimum(i + 1, n_tiles - 1), 0, 0),
                         memory_space=pltpu.SMEM),
            pl.BlockSpec((ts, d), lambda i: (i, 0)),
            _full((MOD_ROWS, bsz, d)), _full((MOD_ROWS, bsz, d)),
            _full((1, d)), _full((1, d)),
            _full((d, d)), _full((1, d)),
            _full((d, d)), _full((1, d)),
            _full((LRU_CONV, d)), _full((1, d)),
            _full((LRU_HEADS, hd, hd)), _full((1, d)),
            _full((LRU_HEADS, hd, hd)), _full((1, d)),
            _full((1, d)),
            _full((d, d)), _full((1, d)),
            _full((ROUTER_ROWS, d)), _full((ROUTER_ROWS, LANES)),
            pl.BlockSpec(memory_space=pl.ANY),
        ] + cast_specs,
        out_specs=out_specs,
        scratch_shapes=[
            pltpu.VMEM((hist + ts, d), jnp.float32),
            pltpu.VMEM((SUBLANES, d), jnp.float32),
            pltpu.VMEM((2, ts * SUBLANES, LANES), jnp.float32),
            pltpu.SemaphoreType.DMA((2,)),
        ],
        compiler_params=pltpu.CompilerParams(
            dimension_semantics=("arbitrary",), vmem_limit_bytes=VMEM_LIMIT_BYTES),
        name="lru_mixer",
    )(dest, dest, x_prev, mod_prev, mod, n1, n2, wy, by, wx, bx, cw, cb, wa, ba, wi, bi, lam, wo, bo,
      wrt, brt, ys, *cast)
    return outs[:5], _cast_results(outs[5:], expert_weights, layer)


def _plan_kernel(n_blocks, cnt_ref, clo_ref, chi_ref, cls_ref, rank_ref,
                 dest_ref, blo_ref, bhi_ref, bvalid_ref, nblk_ref, class_start):
    n_tiles, _, ts = cls_ref.shape
    class_start[...] = jnp.zeros_like(class_start)

    def per_class(c, nb_total):
        class_start[pl.ds(c, 1), :] = jnp.full((1, LANES), nb_total * ROW_BLOCK, jnp.int32)
        nb = (cnt_ref[c] + (ROW_BLOCK - 1)) // ROW_BLOCK

        def fill(k, carry):
            blo_ref[nb_total + k] = clo_ref[c]
            bhi_ref[nb_total + k] = chi_ref[c]
            bvalid_ref[nb_total + k] = jnp.minimum(cnt_ref[c] - k * ROW_BLOCK, ROW_BLOCK)
            return carry

        lax.fori_loop(0, nb, fill, 0)
        return nb_total + nb

    nb_used = lax.fori_loop(0, N_CLASSES, per_class, 0)
    nblk_ref[0] = nb_used
    last = jnp.maximum(nb_used - 1, 0)

    def tail(k, carry):
        blo_ref[k] = blo_ref[last]
        bhi_ref[k] = bhi_ref[last]
        bvalid_ref[k] = 0
        return carry

    lax.fori_loop(nb_used, n_blocks, tail, 0)

    starts = class_start[:, 0:1]
    class_id = lax.broadcasted_iota(jnp.int32, (LANES, ts), 0)

    def per_tile(t, carry):
        first = jnp.sum(jnp.where(class_id == cls_ref[t], starts, 0), axis=0, keepdims=True)
        dest_ref[t] = first + rank_ref[t]
        return carry

    lax.fori_loop(0, n_tiles, per_tile, 0)


def _plan_call(cls, rank, counts, n_blocks):
    n_tiles, _, ts = cls.shape
    smem_full = pl.BlockSpec(memory_space=pltpu.SMEM)
    vmem_full = pl.BlockSpec(memory_space=pltpu.VMEM)
    return pl.pallas_call(
        functools.partial(_plan_kernel, n_blocks),
        out_shape=(
            jax.ShapeDtypeStruct((n_tiles, 1, ts), jnp.int32),
            jax.ShapeDtypeStruct((n_blocks,), jnp.int32),
            jax.ShapeDtypeStruct((n_blocks,), jnp.int32),
            jax.ShapeDtypeStruct((n_blocks,), jnp.int32),
            jax.ShapeDtypeStruct((1,), jnp.int32),
        ),
        in_specs=[smem_full, smem_full, smem_full, vmem_full, vmem_full],
        out_specs=(vmem_full, smem_full, smem_full, smem_full, smem_full),
        scratch_shapes=[pltpu.VMEM((LANES, LANES), jnp.int32)],
        compiler_params=pltpu.CompilerParams(vmem_limit_bytes=VMEM_LIMIT_BYTES),
        name="moe_plan",
    )(counts, jnp.asarray(_CLASS_LO), jnp.asarray(_CLASS_HI), cls, rank)


def _dispatch_kernel(n_tiles, dest_ref, hrow_ref, xs_in_ref, xs_ref, stage, sem_in, sem_out):
    del xs_in_ref
    i = pl.program_id(0)
    ts = dest_ref.shape[2]
    rows = ts * SUBLANES

    def load(tile, slot):
        return pltpu.make_async_copy(hrow_ref.at[pl.ds(pl.multiple_of(tile * rows, rows), rows)],
                                     stage.at[slot], sem_in.at[slot])

    def tile_copy(slot, r, dst_row):
        return pltpu.make_async_copy(
            stage.at[slot, pl.ds(pl.multiple_of(r * SUBLANES, SUBLANES), SUBLANES)],
            xs_ref.at[pl.ds(pl.multiple_of(dst_row * SUBLANES, SUBLANES), SUBLANES)],
            sem_out.at[slot])

    def drain(slot):
        _wait_all_tiles(xs_ref, stage.at[slot], sem_out.at[slot], to_hbm=True)

    slot = i % DISPATCH_SLOTS
    nxt = (i + 1) % DISPATCH_SLOTS

    @pl.when(i == 0)
    def _():
        load(0, 0).start()

    @pl.when(i >= 2)
    def _():
        drain(nxt)

    @pl.when(i + 1 < n_tiles)
    def _():
        load(i + 1, nxt).start()

    load(i, slot).wait()

    _start_tile_dmas(lambda r: tile_copy(slot, r, dest_ref[0, 0, r]), ts, inline=True, both_queues=True)

    @pl.when(i == n_tiles - 1)
    def _():
        @pl.when(i >= 1)
        def _():
            drain((i + DISPATCH_SLOTS - 1) % DISPATCH_SLOTS)
        drain(slot)


def _dispatch_call(dest, hrows, xs_init):
    n_tiles, _, ts = dest.shape
    assert xs_init.dtype == jnp.uint32 and xs_init.shape[1] == LANES
    any_spec = pl.BlockSpec(memory_space=pl.ANY)
    return pl.pallas_call(
        functools.partial(_dispatch_kernel, n_tiles),
        out_shape=jax.ShapeDtypeStruct(xs_init.shape, jnp.uint32),
        grid=(n_tiles,),
        in_specs=[
            pl.BlockSpec((1, 1, ts), lambda i: (i, 0, 0), memory_space=pltpu.SMEM),
            any_spec, any_spec,
        ],
        out_specs=any_spec,
        scratch_shapes=[
            pltpu.VMEM((DISPATCH_SLOTS, ts * SUBLANES, LANES), jnp.uint32),
            pltpu.SemaphoreType.DMA((DISPATCH_SLOTS,)),
            pltpu.SemaphoreType.DMA((DISPATCH_SLOTS,)),
        ],
        input_output_aliases={2: 0},
        compiler_params=pltpu.CompilerParams(
            dimension_semantics=("arbitrary",), vmem_limit_bytes=VMEM_LIMIT_BYTES),
        name="moe_dispatch",
    )(dest, hrows, xs_init)


def _expert_kernel(blo_ref, bhi_ref, bvalid_ref, nblk_ref, xs_ref,
                   wg_lo, wu_lo, wd_lo, wg_hi, wu_hi, wd_hi, ys_ref):
    del blo_ref, bhi_ref, nblk_ref
    valid = bvalid_ref[pl.program_id(0)]

    def experts_on(m):
        pieces = []
        for j in range(PACK_SUBLANES):
            w = _token_rows(xs_ref, j, m)
            pieces.append(_bf16(_f32_from_bits(w & jnp.uint32(0xFFFF0000))))
            pieces.append(_bf16(_f32_from_bits(lax.shift_left(w, jnp.uint32(16)))))
        xb = jnp.concatenate(pieces, axis=1)
        gates = _f32_from_bits(_token_rows(xs_ref, PACK_SUBLANES, m))
        acc = None
        for idx, (wg, wu, wd) in enumerate(((wg_lo, wu_lo, wd_lo), (wg_hi, wu_hi, wd_hi))):
            a = _dot(xb, wg[...])
            hid = (a * jax.nn.sigmoid(a)) * _dot(xb, wu[...]) * gates[:, idx:idx + 1]
            part = _dot(_bf16(hid), wd[...])
            acc = part if acc is None else acc + part
        for s in range(SUBLANES):
            ys_ref[pl.ds(s, m, stride=SUBLANES), :] = acc[:, s * LANES:(s + 1) * LANES]

    for m in range(ROW_STEP, ROW_BLOCK + 1, ROW_STEP):
        pl.when((valid > m - ROW_STEP) & (valid <= m))(functools.partial(experts_on, m))


def _expert_call(blo, bhi, bvalid, nblk, xs, wg, wu, wd):
    n_rows = xs.shape[0] // SUBLANES
    d, de = wg.shape[1], wg.shape[2]
    assert d == SUBLANES * LANES
    n_blocks = n_rows // ROW_BLOCK

    def w_spec(shape, which):
        if which == 0:
            return pl.BlockSpec((None,) + shape, lambda b, lo, hi, bv, n: (lo[b], 0, 0))
        return pl.BlockSpec((None,) + shape, lambda b, lo, hi, bv, n: (hi[b], 0, 0))

    def row_map(b, lo, hi, bv, n):
        return (jnp.minimum(b, jnp.maximum(n[0] - 1, 0)), 0)

    grid_spec = pltpu.PrefetchScalarGridSpec(
        num_scalar_prefetch=4,
        grid=(n_blocks,),
        in_specs=[
            pl.BlockSpec((ROW_BLOCK * SUBLANES, LANES), row_map),
            w_spec((d, de), 0), w_spec((d, de), 0), w_spec((de, d), 0),
            w_spec((d, de), 1), w_spec((d, de), 1), w_spec((de, d), 1),
        ],
        out_specs=pl.BlockSpec((ROW_BLOCK * SUBLANES, LANES), row_map),
    )
    return pl.pallas_call(
        _expert_kernel,
        out_shape=jax.ShapeDtypeStruct((n_rows * SUBLANES, LANES), jnp.float32),
        grid_spec=grid_spec,
        compiler_params=pltpu.CompilerParams(
            dimension_semantics=("arbitrary",), vmem_limit_bytes=VMEM_LIMIT_BYTES),
        name="moe_experts",
    )(blo, bhi, bvalid, nblk, xs, wg, wu, wd, wg, wu, wd)


def _final_kernel(n_tiles, dest_ref, dnext_ref, x1_ref, mod_ref, fmod_ref, fg_ref, ys_ref,
                  out_ref, ybuf, sem, unperm):
    i = pl.program_id(0)
    ts, d = x1_ref.shape
    bsz, tm, _ = out_ref.shape
    y, retire_last = _gather_expert_tiles(ys_ref, ybuf, sem, dest_ref, dnext_ref, i, n_tiles, ts,
                                          inline=False)
    x2 = x1_ref[...] + _per_batch(mod_ref[5], ts) * y
    res = _rms_mod(x2, fg_ref[...], fmod_ref[0], fmod_ref[1])
    for cb in range(d // LANES):
        unperm[cb] = res[:, cb * LANES:(cb + 1) * LANES]
    for cb in range(d // LANES):
        for p in range(bsz):
            out_ref[p, :, cb * LANES:(cb + 1) * LANES] = unperm.at[cb][pl.ds(p, tm, stride=SUBLANES), :]
    retire_last()


def _final_call(dest, x1, mod, ys, bsz, fmod, fin_g):
    t, d = x1.shape
    n_tiles, _, ts = dest.shape
    tm = ts // bsz
    return pl.pallas_call(
        functools.partial(_final_kernel, n_tiles),
        out_shape=jax.ShapeDtypeStruct((bsz, t // bsz, d), jnp.float32),
        grid=(n_tiles,),
        in_specs=[
            pl.BlockSpec((1, 1, ts), lambda i: (i, 0, 0), memory_space=pltpu.SMEM),
            pl.BlockSpec((1, 1, ts), lambda i: (jnp.minimum(i + 1, n_tiles - 1), 0, 0),
                         memory_space=pltpu.SMEM),
            pl.BlockSpec((ts, d), lambda i: (i, 0)),
            _full((MOD_ROWS, bsz, d)), _full((MOD_ROWS, bsz, d)), _full((1, d)),
            pl.BlockSpec(memory_space=pl.ANY),
        ],
        out_specs=pl.BlockSpec((bsz, tm, d), lambda i: (0, i, 0)),
        scratch_shapes=[pltpu.VMEM((2, ts * SUBLANES, LANES), jnp.float32),
                        pltpu.SemaphoreType.DMA((2,)),
                        pltpu.VMEM((d // LANES, ts, LANES), jnp.float32)],
        compiler_params=pltpu.CompilerParams(
            dimension_semantics=("arbitrary",), vmem_limit_bytes=VMEM_LIMIT_BYTES),
        name="moe_combine_final",
    )(dest, dest, x1, mod, fmod, fin_g, ys)


def _router_operands(wg, bg, wr, br):
    d = wg.shape[0]
    pad_g = jnp.zeros((SUBLANES - N_GROUPS, d), jnp.float32)
    pad_e = jnp.zeros((ROUTER_ROWS - SUBLANES - N_EXPERTS, d), jnp.float32)
    wrt = jnp.concatenate([wg.T, pad_g, wr.T, pad_e], axis=0)
    bcol = jnp.concatenate([bg, jnp.zeros((SUBLANES - N_GROUPS,), jnp.float32), br,
                            jnp.zeros((ROUTER_ROWS - SUBLANES - N_EXPERTS,), jnp.float32)])
    brt = jnp.broadcast_to(bcol[:, None], (ROUTER_ROWS, LANES))
    return _bf16(wrt), brt


def _mod_table(m, parts):
    bsz = m.shape[0]
    d = m.shape[1] // parts
    m = jnp.transpose(m.reshape(bsz, parts, d), (1, 0, 2))
    return jnp.concatenate([m, jnp.zeros((MOD_ROWS - parts, bsz, d), jnp.float32)], axis=0)


def _row(v):
    return v.reshape(1, -1)


def _moe_experts(hrows, cls, rank, cnt, weights_bf16, n_blocks, xs_init):
    counts = cnt[:, 0].astype(jnp.int32)
    dest, blo, bhi, bvalid, nblk = _plan_call(cls, rank, counts, n_blocks)
    xs = _dispatch_call(dest, hrows, xs_init)
    ys = _expert_call(blo, bhi, bvalid, nblk, xs, *weights_bf16)
    return xs, ys, dest


def kernel(x, c, ada_w, ada_b, norm1_g, norm2_g, cm_w1, cm_b1, cm_dw, cm_dwb, cm_ln_g, cm_ln_b, cm_w2, cm_b2, rg_wy, rg_by, rg_wx, rg_bx, rg_cw, rg_cb, rg_wa, rg_ba, rg_wi, rg_bi, rg_lambda, rg_wo, rg_bo, moe_wg, moe_bg, moe_wr, moe_br, moe_w_gate, moe_w_up, moe_w_down, fin_ada_w, fin_ada_b, fin_g):
    bsz, seq, d = x.shape
    t = bsz * seq
    depth = ada_w.shape[0]
    assert depth == 2 and bsz == SUBLANES and seq % TIME_TILE == 0
    assert d == 2 * PACK_SUBLANES * LANES

    mods = _ada_call(c, ada_w, ada_b[:, None, :])
    fmods = _ada_call(c, fin_ada_w[None], fin_ada_b[None, None, :])
    fmod = _mod_table(fmods[0], 2)
    n_steps = seq // TIME_TILE
    granule = n_steps // math.gcd(n_steps, ROW_BLOCK)
    n_blocks = -(-(-(-t // ROW_BLOCK) + N_CLASSES) // granule) * granule
    mod0, mod1 = _mod_table(mods[0], 6), _mod_table(mods[1], 6)

    wrt, brt = _router_operands(moe_wg[0], moe_bg[0], moe_wr[0], moe_br[0])
    dw = jnp.concatenate([cm_dw[0], jnp.zeros((1, d), jnp.float32)], axis=0)
    dw = jnp.transpose(dw.reshape(CONF_KERNEL + 1, d // LANES, LANES), (1, 0, 2))
    expert_weights = (moe_w_gate, moe_w_up, moe_w_down)
    (x1, hrows, cls, rank, cnt, xs_init), weights0 = _conf_call(
        x, mod0, _row(norm1_g[0]), _row(norm2_g[0]), _bf16(cm_w1[0]), _row(cm_b1[0]),
        dw, cm_dwb[0].reshape(d // LANES, 1, LANES), _row(cm_ln_g[0]), _row(cm_ln_b[0]),
        _bf16(cm_w2[0]), _row(cm_b2[0]), wrt, brt, n_blocks * ROW_BLOCK * SUBLANES,
        expert_weights, 0)
    xs, ys, dest = _moe_experts(hrows, cls, rank, cnt, weights0, n_blocks, xs_init)

    wrt, brt = _router_operands(moe_wg[1], moe_bg[1], moe_wr[1], moe_br[1])
    (x1, hrows, cls, rank, cnt), weights1 = _lru_call(
        dest, x1, mod0, ys, mod1, _row(norm1_g[1]), _row(norm2_g[1]),
        _bf16(rg_wy[0]), _row(rg_by[0]), _bf16(rg_wx[0]), _row(rg_bx[0]),
        rg_cw[0], _row(rg_cb[0]), _bf16(rg_wa[0]), _row(rg_ba[0]), _bf16(rg_wi[0]), _row(rg_bi[0]),
        _row(rg_lambda[0]), _bf16(rg_wo[0]), _row(rg_bo[0]), wrt, brt, expert_weights, 1)
    _, ys, dest = _moe_experts(hrows, cls, rank, cnt, weights1, n_blocks, xs)
    return _final_call(dest, x1, mod1, ys, bsz, fmod, _row(fin_g))
```

```python
import functools
import math

import jax
import jax.numpy as jnp
import numpy as np
from jax import lax
from jax.experimental import pallas as pl
from jax.experimental.pallas import tpu as pltpu

EPS = 1e-6
CONF_KERNEL = 31
LRU_HEADS = 4
LRU_CONV = 4
LRU_C = 8.0
N_GROUPS = 4
EXPERTS_PER_GROUP = 8
N_EXPERTS = N_GROUPS * EXPERTS_PER_GROUP
PAIRS_PER_GROUP = EXPERTS_PER_GROUP * (EXPERTS_PER_GROUP - 1) // 2
N_CLASSES = N_GROUPS * PAIRS_PER_GROUP

LANES = 128
SUBLANES = 8
BF16_SUBLANES = 16
VMEM_LIMIT_BYTES = 56 * 1024 * 1024

TOKEN_TILE = 512
TIME_TILE = TOKEN_TILE // SUBLANES
SUB_TILE = 256
ROW_BLOCK = 512
ROW_STEP = 128
PERM_PITCH = TIME_TILE + SUBLANES
CONV_ROWS = 128
ROUTER_ROWS = 64
PACK_SUBLANES = 4
MOD_ROWS = 8
DISPATCH_SLOTS = 3
DMA_UNROLL = 32

_NEG_INF = float("-inf")


def _class_tables():
    lo = np.zeros((N_CLASSES,), np.int32)
    hi = np.zeros((N_CLASSES,), np.int32)
    for g in range(N_GROUPS):
        for a in range(EXPERTS_PER_GROUP):
            for b in range(a + 1, EXPERTS_PER_GROUP):
                c = g * PAIRS_PER_GROUP + (a * (2 * EXPERTS_PER_GROUP - 1 - a)) // 2 + (b - a - 1)
                lo[c] = g * EXPERTS_PER_GROUP + a
                hi[c] = g * EXPERTS_PER_GROUP + b
    return lo, hi


_CLASS_LO, _CLASS_HI = _class_tables()


def _bf16(x):
    return x.astype(jnp.bfloat16)


def _dot(a, b):
    return jnp.dot(a, b, preferred_element_type=jnp.float32)


def _bits(x):
    return lax.bitcast_convert_type(x, jnp.uint32)


def _f32_from_bits(x):
    return lax.bitcast_convert_type(x, jnp.float32)


def _gelu_tanh(x):
    c0 = math.sqrt(2.0 / math.pi)
    t = jnp.tanh(x * (c0 + (c0 * 0.044715) * (x * x)))
    hx = 0.5 * x
    return hx + hx * t


def _token_rows(ref, sublane, n):
    return ref[pl.ds(sublane, n, stride=SUBLANES), :]


def _per_batch(v, ts):
    return jnp.tile(v, (ts // SUBLANES, 1))


def _ada_kernel(c_ref, w_ref, b_ref, o_ref):
    c = c_ref[...]
    ca = _bf16(c * jax.nn.sigmoid(c))
    o_ref[...] = _dot(ca, _bf16(w_ref[...])) + b_ref[...]


def _ada_call(c, w, b):
    n_layers, d, n = w.shape
    bsz = c.shape[0]
    nb = 1024
    return pl.pallas_call(
        _ada_kernel,
        out_shape=jax.ShapeDtypeStruct((n_layers, bsz, n), jnp.float32),
        grid=(n_layers, n // nb),
        in_specs=[
            pl.BlockSpec((bsz, d), lambda l, j: (0, 0)),
            pl.BlockSpec((None, d, nb), lambda l, j: (l, 0, j)),
            pl.BlockSpec((None, 1, nb), lambda l, j: (l, 0, j)),
        ],
        out_specs=pl.BlockSpec((None, bsz, nb), lambda l, j: (l, 0, j)),
        compiler_params=pltpu.CompilerParams(
            dimension_semantics=("arbitrary", "arbitrary"),
            vmem_limit_bytes=VMEM_LIMIT_BYTES),
        name="ada_mod",
    )(c, w, b)


def _rms_mod(x, gain, shift, scale):
    ts = x.shape[0]
    ms = jnp.mean(x * x, axis=-1, keepdims=True)
    return (x * lax.rsqrt(ms + EPS)) * _per_batch(gain * (1.0 + scale), ts) + _per_batch(shift, ts)


def _causal_taps(w_ref, bias_ref, ext_ref, n_taps, row0, n):
    d = ext_ref.shape[1]
    cols = []
    for cb in range(d // LANES):
        sl = slice(cb * LANES, (cb + 1) * LANES)
        acc = jnp.broadcast_to(bias_ref[:, sl], (n, LANES))
        for k in range(n_taps):
            lo = row0 + SUBLANES * k
            acc = acc + w_ref[k:k + 1, sl] * ext_ref[lo:lo + n, sl]
        cols.append(acc)
    return jnp.concatenate(cols, axis=1)


def _causal_taps_unit(w_ref, bias_ref, ext_ref, out_ref, cb, r0, n_taps):
    w_cb, ext_cb = w_ref.at[cb], ext_ref.at[cb]
    acc = jnp.broadcast_to(bias_ref[cb], (CONV_ROWS, LANES))
    for k in range(n_taps):
        win = ext_cb[pl.ds(pl.multiple_of(r0 + SUBLANES * k, SUBLANES), CONV_ROWS), :]
        acc = acc + w_cb[k:k + 1, :] * win
    out_ref.at[cb][pl.ds(r0, CONV_ROWS), :] = acc


def _start_tile_dmas(make_copy, n, inline=False, both_queues=False):
    if inline:
        for r in range(n):
            make_copy(r).start(priority=r % 2 if both_queues else 0)
        return

    def body(g, carry):
        for k in range(DMA_UNROLL):
            make_copy(g * DMA_UNROLL + k).start(priority=k % 2 if both_queues else 0)
        return carry
    lax.fori_loop(0, n // DMA_UNROLL, body, 0)


def _wait_all_tiles(hbm_ref, vmem_tiles_ref, sem, to_hbm):
    hbm_view = hbm_ref.at[pl.ds(0, vmem_tiles_ref.shape[0])]
    if to_hbm:
        pltpu.make_async_copy(vmem_tiles_ref, hbm_view, sem).wait()
    else:
        pltpu.make_async_copy(hbm_view, vmem_tiles_ref, sem).wait()


def _gather_expert_tiles(ys_ref, ybuf, sem, dest_ref, dnext_ref, step, n_steps, ts, inline):
    slot = step % 2

    def tile_copy(dref, sl):
        def make(r):
            src = pl.multiple_of(dref[0, 0, r] * SUBLANES, SUBLANES)
            return pltpu.make_async_copy(
                ys_ref.at[pl.ds(src, SUBLANES)],
                ybuf.at[sl, pl.ds(pl.multiple_of(r * SUBLANES, SUBLANES), SUBLANES)],
                sem.at[sl])
        return make

    @pl.when(step == 0)
    def _():
        _start_tile_dmas(tile_copy(dest_ref, 0), ts)

    if not inline:
        @pl.when(step + 1 < n_steps)
        def _():
            _start_tile_dmas(tile_copy(dnext_ref, 1 - slot), ts)

    _wait_all_tiles(ys_ref, ybuf.at[slot], sem.at[slot], to_hbm=False)
    if inline:
        _start_tile_dmas(tile_copy(dnext_ref, 1 - slot), ts, inline=True)
    yv = ybuf.at[slot]
    y = jnp.concatenate([_token_rows(yv, s, ts) for s in range(SUBLANES)], axis=1)

    def retire_last():
        if inline:
            @pl.when(step + 1 == n_steps)
            def _():
                _wait_all_tiles(ys_ref, ybuf.at[1 - slot], sem.at[1 - slot], to_hbm=False)

    return y, retire_last


def _route(logits_t):
    ts = logits_t.shape[1]
    row = lax.broadcasted_iota(jnp.int32, (SUBLANES, ts), 0)
    gl = jnp.where(row < N_GROUPS, logits_t[0:SUBLANES], _NEG_INF)
    gmax = jnp.max(gl, axis=0, keepdims=True)
    gsel = jnp.min(jnp.where(gl == gmax, row, SUBLANES), axis=0, keepdims=True)
    p_g = 1.0 / jnp.sum(jnp.exp(gl - gmax), axis=0, keepdims=True)
    es = jnp.zeros((SUBLANES, ts), jnp.float32)
    for g in range(N_GROUPS):
        lo = SUBLANES * (1 + g)
        es = jnp.where(gsel == g, logits_t[lo:lo + EXPERTS_PER_GROUP], es)
    m1 = jnp.max(es, axis=0, keepdims=True)
    i1 = jnp.min(jnp.where(es == m1, row, SUBLANES), axis=0, keepdims=True)
    es2 = jnp.where(row == i1, _NEG_INF, es)
    m2 = jnp.max(es2, axis=0, keepdims=True)
    i2 = jnp.min(jnp.where(es2 == m2, row, SUBLANES), axis=0, keepdims=True)
    t = jnp.exp(m2 - m1)
    w1 = p_g / (1.0 + t)
    w2 = w1 * t
    first_lo = i1 < i2
    e_lo = jnp.minimum(i1, i2)
    e_hi = jnp.maximum(i1, i2)
    gate_lo = jnp.where(first_lo, w1, w2)
    gate_hi = jnp.where(first_lo, w2, w1)
    pair = lax.shift_right_logical(e_lo * (2 * EXPERTS_PER_GROUP - 1 - e_lo), 1) + (e_hi - e_lo - 1)
    cls = gsel * PAIRS_PER_GROUP + pair
    return cls, gate_lo, gate_hi


def _post_mixer(x1, mod_ref, n2_ref, wrt_ref, brt_ref, hrow_ref, cls_ref, rank_ref, cnt_ref, sub):
    ts = x1.shape[0]
    hrow_ref = hrow_ref.at[pl.ds(sub * ts * SUBLANES, ts * SUBLANES)]
    cols = slice(sub * ts, (sub + 1) * ts)
    h2 = _rms_mod(x1, n2_ref[...], mod_ref[3], mod_ref[4])
    h2b = _bf16(h2)
    logits_t = lax.dot_general(wrt_ref[...], h2b, (((1,), (1,)), ((), ())),
                               preferred_element_type=jnp.float32) + brt_ref[:, 0:1]
    cls, gate_lo, gate_hi = _route(logits_t)
    gates_t = jnp.concatenate(
        [gate_lo, gate_hi, jnp.zeros((LANES - 2, ts), jnp.float32)], axis=0)
    h2r = h2b.astype(jnp.float32)
    for j in range(PACK_SUBLANES):
        hi = _bits(h2r[:, (2 * j) * LANES:(2 * j + 1) * LANES])
        lo = _bits(h2r[:, (2 * j + 1) * LANES:(2 * j + 2) * LANES])
        hrow_ref[pl.ds(j, ts, stride=SUBLANES), :] = hi | lax.shift_right_logical(lo, jnp.uint32(16))
    hrow_ref[pl.ds(PACK_SUBLANES, ts, stride=SUBLANES), :] = _bits(gates_t.T)
    for j in range(PACK_SUBLANES + 1, SUBLANES):
        hrow_ref[pl.ds(j, ts, stride=SUBLANES), :] = jnp.zeros((ts, LANES), jnp.uint32)
    cls_ref[0, :, cols] = cls
    crow = lax.broadcasted_iota(jnp.int32, (LANES, ts), 0)
    onehot = jnp.where(crow == cls, 1.0, 0.0)
    cnt = jnp.sum(onehot, axis=1, keepdims=True)
    earlier = (lax.broadcasted_iota(jnp.int32, (ts, ts), 0)
               < lax.broadcasted_iota(jnp.int32, (ts, ts), 1))
    prefix = _dot(_bf16(onehot), _bf16(jnp.where(earlier, 1.0, 0.0)))
    rank = jnp.sum(onehot * (prefix + cnt_ref[:, 0:1]), axis=0, keepdims=True)
    rank_ref[0, :, cols] = rank.astype(jnp.int32)
    cnt_ref[...] += jnp.broadcast_to(cnt, cnt_ref.shape)


def _mixer_out(t, d, ts):
    n_tiles = t // ts
    out_shape = (
        jax.ShapeDtypeStruct((t, d), jnp.float32),
        jax.ShapeDtypeStruct((t * SUBLANES, LANES), jnp.uint32),
        jax.ShapeDtypeStruct((n_tiles, 1, ts), jnp.int32),
        jax.ShapeDtypeStruct((n_tiles, 1, ts), jnp.int32),
        jax.ShapeDtypeStruct((LANES, LANES), jnp.float32),
    )
    out_specs = (
        pl.BlockSpec((ts, d), lambda i: (i, 0)),
        pl.BlockSpec((ts * SUBLANES, LANES), lambda i: (i, 0)),
        pl.BlockSpec((1, 1, ts), lambda i: (i, 0, 0)),
        pl.BlockSpec((1, 1, ts), lambda i: (i, 0, 0)),
        pl.BlockSpec((LANES, LANES), lambda i: (0, 0)),
    )
    return out_shape, out_specs


def _full(shape):
    return pl.BlockSpec(shape, lambda i: tuple(0 for _ in shape))


def _expert_weight_cast(weights, layer, n_steps):
    flat = [w.reshape(w.shape[0], w.shape[1] * w.shape[2], w.shape[3]) for w in weights]
    if any(w.shape[1] % (n_steps * BF16_SUBLANES) for w in flat):
        return [], [], (), ()
    in_specs = [pl.BlockSpec((None, w.shape[1] // n_steps, w.shape[2]), lambda i: (layer, i, 0))
                for w in flat]
    out_shapes = tuple(jax.ShapeDtypeStruct(w.shape[1:], jnp.bfloat16) for w in flat)
    out_specs = tuple(pl.BlockSpec((w.shape[1] // n_steps, w.shape[2]), lambda i: (i, 0))
                      for w in flat)
    return flat, in_specs, out_shapes, out_specs


def _cast_slices(cast_in, cast_out):
    for src, dst in zip(cast_in, cast_out):
        dst[...] = _bf16(src[...])


def _cast_results(outs, weights, layer):
    if outs:
        return [o.reshape(w.shape[1:]) for o, w in zip(outs, weights)]
    return [_bf16(w[layer]) for w in weights]


def _to_tile_order(x_ref, perm, xs):
    bsz, tm, d = x_ref.shape
    for cb in range(d // LANES):
        for p in range(bsz):
            perm[cb, PERM_PITCH * p:PERM_PITCH * p + tm, :] = x_ref[p, :, cb * LANES:(cb + 1) * LANES]
    for m in range(tm):
        for cb in range(d // LANES):
            xs[m * bsz:(m + 1) * bsz, cb * LANES:(cb + 1) * LANES] = (
                perm.at[cb][pl.ds(m, bsz, stride=PERM_PITCH), :])


def _conf_kernel(n_cast, x_ref, mod_ref, n1_ref, n2_ref, w1_ref, b1_ref, dw_ref, dwb_ref,
                 lng_ref, lnb_ref, w2_ref, b2_ref, wrt_ref, brt_ref, *rest):
    cast_in, rest = rest[:n_cast], rest[n_cast:]
    x1_ref, hrow_ref, cls_ref, rank_ref, cnt_ref, zeros_ref = rest[:6]
    cast_out, (perm, xs, vext, conv) = rest[6:6 + n_cast], rest[6 + n_cast:]
    _cast_slices(cast_in, cast_out)
    i = pl.program_id(0)
    bsz, tm, d = x_ref.shape
    ts = bsz * tm
    n_chunks = d // LANES
    hist = SUBLANES * (CONF_KERNEL - 1)

    @pl.when(i == 0)
    def _():
        vext[:, 0:hist, :] = jnp.zeros((n_chunks, hist, LANES), jnp.float32)
        cnt_ref[...] = jnp.zeros_like(cnt_ref)

    zeros_ref[...] = jnp.zeros_like(zeros_ref)

    _to_tile_order(x_ref, perm, xs)
    for sub in range(ts // SUB_TILE):
        rows = slice(sub * SUB_TILE, (sub + 1) * SUB_TILE)
        h = _bf16(_rms_mod(xs[rows, :], n1_ref[...], mod_ref[0], mod_ref[1]))
        u = _dot(h, w1_ref[...]) + b1_ref[...]
        v = u[:, 0:d] * jax.nn.sigmoid(u[:, d:2 * d])
        for cb in range(n_chunks):
            vext[cb, hist + sub * SUB_TILE:hist + (sub + 1) * SUB_TILE, :] = v[:, cb * LANES:(cb + 1) * LANES]

    units_per_chunk = ts // CONV_ROWS

    def conv_unit(idx, carry):
        r0 = pl.multiple_of((idx % units_per_chunk) * CONV_ROWS, CONV_ROWS)
        _causal_taps_unit(dw_ref, dwb_ref, vext, conv, idx // units_per_chunk, r0, CONF_KERNEL)
        return carry

    lax.fori_loop(0, n_chunks * units_per_chunk, conv_unit, 0, unroll=16)
    vext[:, 0:hist, :] = vext[:, ts:ts + hist, :]

    for sub in range(ts // SUB_TILE):
        rows = slice(sub * SUB_TILE, (sub + 1) * SUB_TILE)
        y = jnp.concatenate([conv[cb, rows, :] for cb in range(n_chunks)], axis=1)
        mu = jnp.mean(y, axis=-1, keepdims=True)
        yc = y - mu
        var = jnp.mean(yc * yc, axis=-1, keepdims=True)
        z = yc * lax.rsqrt(var + EPS) * lng_ref[...] + lnb_ref[...]
        z = z * jax.nn.sigmoid(z)
        mix = _dot(_bf16(z), w2_ref[...]) + b2_ref[...]
        x1 = xs[rows, :] + _per_batch(mod_ref[2], SUB_TILE) * mix
        x1_ref[rows, :] = x1
        _post_mixer(x1, mod_ref, n2_ref, wrt_ref, brt_ref, hrow_ref, cls_ref, rank_ref, cnt_ref, sub)


def _conf_call(x, mod, n1, n2, w1, b1, dw, dwb, lng, lnb, w2, b2, wrt, brt, zero_rows,
               expert_weights, layer):
    bsz, seq, d = x.shape
    tm = TIME_TILE
    ts = bsz * tm
    n_steps = seq // tm
    assert zero_rows % (n_steps * SUBLANES) == 0
    cast, cast_specs, cast_shapes, cast_out_specs = _expert_weight_cast(expert_weights, layer, n_steps)
    out_shape, out_specs = _mixer_out(bsz * seq, d, ts)
    out_shape += (jax.ShapeDtypeStruct((zero_rows, LANES), jnp.uint32),) + cast_shapes
    out_specs += (pl.BlockSpec((zero_rows // n_steps, LANES), lambda i: (i, 0)),) + cast_out_specs
    hist = SUBLANES * (CONF_KERNEL - 1)
    outs = pl.pallas_call(
        functools.partial(_conf_kernel, len(cast)),
        out_shape=out_shape,
        grid=(n_steps,),
        in_specs=[
            pl.BlockSpec((bsz, tm, d), lambda i: (0, i, 0)),
            _full((MOD_ROWS, bsz, d)),
            _full((1, d)), _full((1, d)),
            _full((d, 2 * d)), _full((1, 2 * d)),
            _full((d // LANES, CONF_KERNEL + 1, LANES)), _full((d // LANES, 1, LANES)),
            _full((1, d)), _full((1, d)),
            _full((d, d)), _full((1, d)),
            _full((ROUTER_ROWS, d)), _full((ROUTER_ROWS, LANES)),
        ] + cast_specs,
        out_specs=out_specs,
        scratch_shapes=[
            pltpu.VMEM((d // LANES, bsz * PERM_PITCH, LANES), jnp.float32),
            pltpu.VMEM((ts, d), jnp.float32),
            pltpu.VMEM((d // LANES, hist + ts, LANES), jnp.float32),
            pltpu.VMEM((d // LANES, ts, LANES), jnp.float32),
        ],
        compiler_params=pltpu.CompilerParams(
            dimension_semantics=("arbitrary",), vmem_limit_bytes=VMEM_LIMIT_BYTES),
        name="conf_mixer",
    )(x, mod, n1, n2, w1, b1, dw, dwb, lng, lnb, w2, b2, wrt, brt, *cast)
    return outs[:6], _cast_results(outs[6:], expert_weights, layer)


def _lru_kernel(n_tiles, n_cast, dest_ref, dnext_ref, xprev_ref, pmod_ref, mod_ref, n1_ref, n2_ref,
                wy_ref, by_ref, wx_ref, bx_ref,
                cw_ref, cb_ref, wa_ref, ba_ref, wi_ref, bi_ref, lam_ref, wo_ref, bo_ref,
                wrt_ref, brt_ref, ys_ref, *rest):
    cast_in, rest = rest[:n_cast], rest[n_cast:]
    x1_ref, hrow_ref, cls_ref, rank_ref, cnt_ref = rest[:5]
    cast_out, (uext, hcar, ybuf, ysem) = rest[5:5 + n_cast], rest[5 + n_cast:]
    _cast_slices(cast_in, cast_out)
    i = pl.program_id(0)
    ts, d = xprev_ref.shape
    hd = d // LRU_HEADS
    hist = SUBLANES * (LRU_CONV - 1)

    @pl.when(i == 0)
    def _():
        uext[0:hist, :] = jnp.zeros((hist, d), jnp.float32)
        hcar[...] = jnp.zeros_like(hcar)
        cnt_ref[...] = jnp.zeros_like(cnt_ref)

    y_prev, retire_last = _gather_expert_tiles(ys_ref, ybuf, ysem, dest_ref, dnext_ref, i, n_tiles, ts,
                                               inline=True)
    nl = -lam_ref[...]
    decay = LRU_C * (jnp.maximum(nl, 0.0) + jnp.log1p(jnp.exp(-jnp.abs(nl))))
    hstate = hcar[...]
    for sub in range(ts // SUB_TILE):
        rows = slice(sub * SUB_TILE, (sub + 1) * SUB_TILE)
        x = xprev_ref[rows, :] + _per_batch(pmod_ref[5], SUB_TILE) * y_prev[rows, :]
        h = _bf16(_rms_mod(x, n1_ref[...], mod_ref[0], mod_ref[1]))
        ygate = _gelu_tanh(_dot(h, wy_ref[...]) + by_ref[...])
        xb = _dot(h, wx_ref[...]) + bx_ref[...]
        uext[hist + sub * SUB_TILE:hist + (sub + 1) * SUB_TILE, :] = xb
        u = _causal_taps(cw_ref, cb_ref, uext, LRU_CONV, sub * SUB_TILE, SUB_TILE)

        ub = _bf16(u)
        r_parts, i_parts = [], []
        for hh in range(LRU_HEADS):
            uh = ub[:, hh * hd:(hh + 1) * hd]
            r_parts.append(_dot(uh, wa_ref[hh]))
            i_parts.append(_dot(uh, wi_ref[hh]))
        r = jax.nn.sigmoid(jnp.concatenate(r_parts, axis=1) + ba_ref[...])
        ig = jax.nn.sigmoid(jnp.concatenate(i_parts, axis=1) + bi_ref[...])

        neg_log_a = r * decay
        a = jnp.exp(-neg_log_a)
        mult = jnp.sqrt(jnp.tanh(neg_log_a) * (1.0 + a * a))
        b = mult * (ig * u)

        outs = []
        for m in range(SUB_TILE // SUBLANES):
            sl = slice(m * SUBLANES, (m + 1) * SUBLANES)
            hstate = a[sl] * hstate + b[sl]
            outs.append(hstate)
        hs = jnp.concatenate(outs, axis=0)

        mix = _dot(_bf16(hs * ygate), wo_ref[...]) + bo_ref[...]
        x1 = x + _per_batch(mod_ref[2], SUB_TILE) * mix
        x1_ref[rows, :] = x1
        _post_mixer(x1, mod_ref, n2_ref, wrt_ref, brt_ref, hrow_ref, cls_ref, rank_ref, cnt_ref, sub)
    hcar[...] = hstate
    uext[0:hist, :] = uext[ts:ts + hist, :]
    retire_last()


def _lru_call(dest, x_prev, mod_prev, ys, mod, n1, n2, wy, by, wx, bx, cw, cb, wa, ba, wi, bi, lam,
              wo, bo, wrt, brt, expert_weights, layer):
    t, d = x_prev.shape
    n_tiles, _, ts = dest.shape
    bsz = mod.shape[1]
    hd = d // LRU_HEADS
    hist = SUBLANES * (LRU_CONV - 1)
    cast, cast_specs, cast_shapes, cast_out_specs = _expert_weight_cast(expert_weights, layer, n_tiles)
    out_shape, out_specs = _mixer_out(t, d, ts)
    out_shape += cast_shapes
    out_specs += cast_out_specs
    outs = pl.pallas_call(
        functools.partial(_lru_kernel, n_tiles, len(cast)),
        out_shape=out_shape,
        grid=(n_tiles,),
        in_specs=[
            pl.BlockSpec((1, 1, ts), lambda i: (i, 0, 0), memory_space=pltpu.SMEM),
            pl.BlockSpec((1, 1, ts), lambda i: (jnp.minimum(i + 1, n_tiles - 1), 0, 0),
                         memory_space=pltpu.SMEM),
            pl.BlockSpec((ts, d), lambda i: (i, 0)),
            _full((MOD_ROWS, bsz, d)), _full((MOD_ROWS, bsz, d)),
            _full((1, d)), _full((1, d)),
            _full((d, d)), _full((1, d)),
            _full((d, d)), _full((1, d)),
            _full((LRU_CONV, d)), _full((1, d)),
            _full((LRU_HEADS, hd, hd)), _full((1, d)),
            _full((LRU_HEADS, hd, hd)), _full((1, d)),
            _full((1, d)),
            _full((d, d)), _full((1, d)),
            _full((ROUTER_ROWS, d)), _full((ROUTER_ROWS, LANES)),
            pl.BlockSpec(memory_space=pl.ANY),
        ] + cast_specs,
        out_specs=out_specs,
        scratch_shapes=[
            pltpu.VMEM((hist + ts, d), jnp.float32),
            pltpu.VMEM((SUBLANES, d), jnp.float32),
            pltpu.VMEM((2, ts * SUBLANES, LANES), jnp.float32),
            pltpu.SemaphoreType.DMA((2,)),
        ],
        compiler_params=pltpu.CompilerParams(
            dimension_semantics=("arbitrary",), vmem_limit_bytes=VMEM_LIMIT_BYTES),
        name="lru_mixer",
    )(dest, dest, x_prev, mod_prev, mod, n1, n2, wy, by, wx, bx, cw, cb, wa, ba, wi, bi, lam, wo, bo,
      wrt, brt, ys, *cast)
    return outs[:5], _cast_results(outs[5:], expert_weights, layer)


def _plan_kernel(n_blocks, cnt_ref, clo_ref, chi_ref, cls_ref, rank_ref,
                 dest_ref, blo_ref, bhi_ref, bvalid_ref, nblk_ref, class_start):
    n_tiles, _, ts = cls_ref.shape
    class_start[...] = jnp.zeros_like(class_start)

    def per_class(c, nb_total):
        class_start[pl.ds(c, 1), :] = jnp.full((1, LANES), nb_total * ROW_BLOCK, jnp.int32)
        nb = (cnt_ref[c] + (ROW_BLOCK - 1)) // ROW_BLOCK

        def fill(k, carry):
            blo_ref[nb_total + k] = clo_ref[c]
            bhi_ref[nb_total + k] = chi_ref[c]
            bvalid_ref[nb_total + k] = jnp.minimum(cnt_ref[c] - k * ROW_BLOCK, ROW_BLOCK)
            return carry

        lax.fori_loop(0, nb, fill, 0)
        return nb_total + nb

    nb_used = lax.fori_loop(0, N_CLASSES, per_class, 0)
    nblk_ref[0] = nb_used
    last = jnp.maximum(nb_used - 1, 0)

    def tail(k, carry):
        blo_ref[k] = blo_ref[last]
        bhi_ref[k] = bhi_ref[last]
        bvalid_ref[k] = 0
        return carry

    lax.fori_loop(nb_used, n_blocks, tail, 0)

    starts = class_start[:, 0:1]
    class_id = lax.broadcasted_iota(jnp.int32, (LANES, ts), 0)

    def per_tile(t, carry):
        first = jnp.sum(jnp.where(class_id == cls_ref[t], starts, 0), axis=0, keepdims=True)
        dest_ref[t] = first + rank_ref[t]
        return carry

    lax.fori_loop(0, n_tiles, per_tile, 0)


def _plan_call(cls, rank, counts, n_blocks):
    n_tiles, _, ts = cls.shape
    smem_full = pl.BlockSpec(memory_space=pltpu.SMEM)
    vmem_full = pl.BlockSpec(memory_space=pltpu.VMEM)
    return pl.pallas_call(
        functools.partial(_plan_kernel, n_blocks),
        out_shape=(
            jax.ShapeDtypeStruct((n_tiles, 1, ts), jnp.int32),
            jax.ShapeDtypeStruct((n_blocks,), jnp.int32),
            jax.ShapeDtypeStruct((n_blocks,), jnp.int32),
            jax.ShapeDtypeStruct((n_blocks,), jnp.int32),
            jax.ShapeDtypeStruct((1,), jnp.int32),
        ),
        in_specs=[smem_full, smem_full, smem_full, vmem_full, vmem_full],
        out_specs=(vmem_full, smem_full, smem_full, smem_full, smem_full),
        scratch_shapes=[pltpu.VMEM((LANES, LANES), jnp.int32)],
        compiler_params=pltpu.CompilerParams(vmem_limit_bytes=VMEM_LIMIT_BYTES),
        name="moe_plan",
    )(counts, jnp.asarray(_CLASS_LO), jnp.asarray(_CLASS_HI), cls, rank)


def _dispatch_kernel(n_tiles, dest_ref, hrow_ref, xs_in_ref, xs_ref, stage, sem_in, sem_out):
    del xs_in_ref
    i = pl.program_id(0)
    ts = dest_ref.shape[2]
    rows = ts * SUBLANES

    def load(tile, slot):
        return pltpu.make_async_copy(hrow_ref.at[pl.ds(pl.multiple_of(tile * rows, rows), rows)],
                                     stage.at[slot], sem_in.at[slot])

    def tile_copy(slot, r, dst_row):
        return pltpu.make_async_copy(
            stage.at[slot, pl.ds(pl.multiple_of(r * SUBLANES, SUBLANES), SUBLANES)],
            xs_ref.at[pl.ds(pl.multiple_of(dst_row * SUBLANES, SUBLANES), SUBLANES)],
            sem_out.at[slot])

    def drain(slot):
        _wait_all_tiles(xs_ref, stage.at[slot], sem_out.at[slot], to_hbm=True)

    slot = i % DISPATCH_SLOTS
    nxt = (i + 1) % DISPATCH_SLOTS

    @pl.when(i == 0)
    def _():
        load(0, 0).start()

    @pl.when(i >= 2)
    def _():
        drain(nxt)

    @pl.when(i + 1 < n_tiles)
    def _():
        load(i + 1, nxt).start()

    load(i, slot).wait()

    _start_tile_dmas(lambda r: tile_copy(slot, r, dest_ref[0, 0, r]), ts, inline=True, both_queues=True)

    @pl.when(i == n_tiles - 1)
    def _():
        @pl.when(i >= 1)
        def _():
            drain((i + DISPATCH_SLOTS - 1) % DISPATCH_SLOTS)
        drain(slot)


def _dispatch_call(dest, hrows, xs_init):
    n_tiles, _, ts = dest.shape
    assert xs_init.dtype == jnp.uint32 and xs_init.shape[1] == LANES
    any_spec = pl.BlockSpec(memory_space=pl.ANY)
    return pl.pallas_call(
        functools.partial(_dispatch_kernel, n_tiles),
        out_shape=jax.ShapeDtypeStruct(xs_init.shape, jnp.uint32),
        grid=(n_tiles,),
        in_specs=[
            pl.BlockSpec((1, 1, ts), lambda i: (i, 0, 0), memory_space=pltpu.SMEM),
            any_spec, any_spec,
        ],
        out_specs=any_spec,
        scratch_shapes=[
            pltpu.VMEM((DISPATCH_SLOTS, ts * SUBLANES, LANES), jnp.uint32),
            pltpu.SemaphoreType.DMA((DISPATCH_SLOTS,)),
            pltpu.SemaphoreType.DMA((DISPATCH_SLOTS,)),
        ],
        input_output_aliases={2: 0},
        compiler_params=pltpu.CompilerParams(
            dimension_semantics=("arbitrary",), vmem_limit_bytes=VMEM_LIMIT_BYTES),
        name="moe_dispatch",
    )(dest, hrows, xs_init)


def _expert_kernel(blo_ref, bhi_ref, bvalid_ref, nblk_ref, xs_ref,
                   wg_lo, wu_lo, wd_lo, wg_hi, wu_hi, wd_hi, ys_ref):
    del blo_ref, bhi_ref, nblk_ref
    valid = bvalid_ref[pl.program_id(0)]

    def experts_on(m):
        pieces = []
        for j in range(PACK_SUBLANES):
            w = _token_rows(xs_ref, j, m)
            pieces.append(_bf16(_f32_from_bits(w & jnp.uint32(0xFFFF0000))))
            pieces.append(_bf16(_f32_from_bits(lax.shift_left(w, jnp.uint32(16)))))
        xb = jnp.concatenate(pieces, axis=1)
        gates = _f32_from_bits(_token_rows(xs_ref, PACK_SUBLANES, m))
        acc = None
        for idx, (wg, wu, wd) in enumerate(((wg_lo, wu_lo, wd_lo), (wg_hi, wu_hi, wd_hi))):
            a = _dot(xb, wg[...])
            hid = (a * jax.nn.sigmoid(a)) * _dot(xb, wu[...]) * gates[:, idx:idx + 1]
            part = _dot(_bf16(hid), wd[...])
            acc = part if acc is None else acc + part
        for s in range(SUBLANES):
            ys_ref[pl.ds(s, m, stride=SUBLANES), :] = acc[:, s * LANES:(s + 1) * LANES]

    for m in range(ROW_STEP, ROW_BLOCK + 1, ROW_STEP):
        pl.when((valid > m - ROW_STEP) & (valid <= m))(functools.partial(experts_on, m))


def _expert_call(blo, bhi, bvalid, nblk, xs, wg, wu, wd):
    n_rows = xs.shape[0] // SUBLANES
    d, de = wg.shape[1], wg.shape[2]
    assert d == SUBLANES * LANES
    n_blocks = n_rows // ROW_BLOCK

    def w_spec(shape, which):
        if which == 0:
            return pl.BlockSpec((None,) + shape, lambda b, lo, hi, bv, n: (lo[b], 0, 0))
        return pl.BlockSpec((None,) + shape, lambda b, lo, hi, bv, n: (hi[b], 0, 0))

    def row_map(b, lo, hi, bv, n):
        return (jnp.minimum(b, jnp.maximum(n[0] - 1, 0)), 0)

    grid_spec = pltpu.PrefetchScalarGridSpec(
        num_scalar_prefetch=4,
        grid=(n_blocks,),
        in_specs=[
            pl.BlockSpec((ROW_BLOCK * SUBLANES, LANES), row_map),
            w_spec((d, de), 0), w_spec((d, de), 0), w_spec((de, d), 0),
            w_spec((d, de), 1), w_spec((d, de), 1), w_spec((de, d), 1),
        ],
        out_specs=pl.BlockSpec((ROW_BLOCK * SUBLANES, LANES), row_map),
    )
    return pl.pallas_call(
        _expert_kernel,
        out_shape=jax.ShapeDtypeStruct((n_rows * SUBLANES, LANES), jnp.float32),
        grid_spec=grid_spec,
        compiler_params=pltpu.CompilerParams(
            dimension_semantics=("arbitrary",), vmem_limit_bytes=VMEM_LIMIT_BYTES),
        name="moe_experts",
    )(blo, bhi, bvalid, nblk, xs, wg, wu, wd, wg, wu, wd)


def _final_kernel(n_tiles, dest_ref, dnext_ref, x1_ref, mod_ref, fmod_ref, fg_ref, ys_ref,
                  out_ref, ybuf, sem, unperm):
    i = pl.program_id(0)
    ts, d = x1_ref.shape
    bsz, tm, _ = out_ref.shape
    y, retire_last = _gather_expert_tiles(ys_ref, ybuf, sem, dest_ref, dnext_ref, i, n_tiles, ts,
                                          inline=False)
    x2 = x1_ref[...] + _per_batch(mod_ref[5], ts) * y
    res = _rms_mod(x2, fg_ref[...], fmod_ref[0], fmod_ref[1])
    for cb in range(d // LANES):
        unperm[cb] = res[:, cb * LANES:(cb + 1) * LANES]
    for cb in range(d // LANES):
        for p in range(bsz):
            out_ref[p, :, cb * LANES:(cb + 1) * LANES] = unperm.at[cb][pl.ds(p, tm, stride=SUBLANES), :]
    retire_last()


def _final_call(dest, x1, mod, ys, bsz, fmod, fin_g):
    t, d = x1.shape
    n_tiles, _, ts = dest.shape
    tm = ts // bsz
    return pl.pallas_call(
        functools.partial(_final_kernel, n_tiles),
        out_shape=jax.ShapeDtypeStruct((bsz, t // bsz, d), jnp.float32),
        grid=(n_tiles,),
        in_specs=[
            pl.BlockSpec((1, 1, ts), lambda i: (i, 0, 0), memory_space=pltpu.SMEM),
            pl.BlockSpec((1, 1, ts), lambda i: (jnp.minimum(i + 1, n_tiles - 1), 0, 0),
                         memory_space=pltpu.SMEM),
            pl.BlockSpec((ts, d), lambda i: (i, 0)),
            _full((MOD_ROWS, bsz, d)), _full((MOD_ROWS, bsz, d)), _full((1, d)),
            pl.BlockSpec(memory_space=pl.ANY),
        ],
        out_specs=pl.BlockSpec((bsz, tm, d), lambda i: (0, i, 0)),
        scratch_shapes=[pltpu.VMEM((2, ts * SUBLANES, LANES), jnp.float32),
                        pltpu.SemaphoreType.DMA((2,)),
                        pltpu.VMEM((d // LANES, ts, LANES), jnp.float32)],
        compiler_params=pltpu.CompilerParams(
            dimension_semantics=("arbitrary",), vmem_limit_bytes=VMEM_LIMIT_BYTES),
        name="moe_combine_final",
    )(dest, dest, x1, mod, fmod, fin_g, ys)


def _router_operands(wg, bg, wr, br):
    d = wg.shape[0]
    pad_g = jnp.zeros((SUBLANES - N_GROUPS, d), jnp.float32)
    pad_e = jnp.zeros((ROUTER_ROWS - SUBLANES - N_EXPERTS, d), jnp.float32)
    wrt = jnp.concatenate([wg.T, pad_g, wr.T, pad_e], axis=0)
    bcol = jnp.concatenate([bg, jnp.zeros((SUBLANES - N_GROUPS,), jnp.float32), br,
                            jnp.zeros((ROUTER_ROWS - SUBLANES - N_EXPERTS,), jnp.float32)])
    brt = jnp.broadcast_to(bcol[:, None], (ROUTER_ROWS, LANES))
    return _bf16(wrt), brt


def _mod_table(m, parts):
    bsz = m.shape[0]
    d = m.shape[1] // parts
    m = jnp.transpose(m.reshape(bsz, parts, d), (1, 0, 2))
    return jnp.concatenate([m, jnp.zeros((MOD_ROWS - parts, bsz, d), jnp.float32)], axis=0)


def _row(v):
    return v.reshape(1, -1)


def _moe_experts(hrows, cls, rank, cnt, weights_bf16, n_blocks, xs_init):
    counts = cnt[:, 0].astype(jnp.int32)
    dest, blo, bhi, bvalid, nblk = _plan_call(cls, rank, counts, n_blocks)
    xs = _dispatch_call(dest, hrows, xs_init)
    ys = _expert_call(blo, bhi, bvalid, nblk, xs, *weights_bf16)
    return xs, ys, dest


def kernel(x, c, ada_w, ada_b, norm1_g, norm2_g, cm_w1, cm_b1, cm_dw, cm_dwb, cm_ln_g, cm_ln_b, cm_w2, cm_b2, rg_wy, rg_by, rg_wx, rg_bx, rg_cw, rg_cb, rg_wa, rg_ba, rg_wi, rg_bi, rg_lambda, rg_wo, rg_bo, moe_wg, moe_bg, moe_wr, moe_br, moe_w_gate, moe_w_up, moe_w_down, fin_ada_w, fin_ada_b, fin_g):
    bsz, seq, d = x.shape
    t = bsz * seq
    depth = ada_w.shape[0]
    assert depth == 2 and bsz == SUBLANES and seq % TIME_TILE == 0
    assert d == 2 * PACK_SUBLANES * LANES

    mods = _ada_call(c, ada_w, ada_b[:, None, :])
    fmods = _ada_call(c, fin_ada_w[None], fin_ada_b[None, None, :])
    fmod = _mod_table(fmods[0], 2)
    n_steps = seq // TIME_TILE
    granule = n_steps // math.gcd(n_steps, ROW_BLOCK)
    n_blocks = -(-(-(-t // ROW_BLOCK) + N_CLASSES) // granule) * granule
    mod0, mod1 = _mod_table(mods[0], 6), _mod_table(mods[1], 6)

    wrt, brt = _router_operands(moe_wg[0], moe_bg[0], moe_wr[0], moe_br[0])
    dw = jnp.concatenate([cm_dw[0], jnp.zeros((1, d), jnp.float32)], axis=0)
    dw = jnp.transpose(dw.reshape(CONF_KERNEL + 1, d // LANES, LANES), (1, 0, 2))
    expert_weights = (moe_w_gate, moe_w_up, moe_w_down)
    (x1, hrows, cls, rank, cnt, xs_init), weights0 = _conf_call(
        x, mod0, _row(norm1_g[0]), _row(norm2_g[0]), _bf16(cm_w1[0]), _row(cm_b1[0]),
        dw, cm_dwb[0].reshape(d // LANES, 1, LANES), _row(cm_ln_g[0]), _row(cm_ln_b[0]),
        _bf16(cm_w2[0]), _row(cm_b2[0]), wrt, brt, n_blocks * ROW_BLOCK * SUBLANES,
        expert_weights, 0)
    xs, ys, dest = _moe_experts(hrows, cls, rank, cnt, weights0, n_blocks, xs_init)

    wrt, brt = _router_operands(moe_wg[1], moe_bg[1], moe_wr[1], moe_br[1])
    (x1, hrows, cls, rank, cnt), weights1 = _lru_call(
        dest, x1, mod0, ys, mod1, _row(norm1_g[1]), _row(norm2_g[1]),
        _bf16(rg_wy[0]), _row(rg_by[0]), _bf16(rg_wx[0]), _row(rg_bx[0]),
        rg_cw[0], _row(rg_cb[0]), _bf16(rg_wa[0]), _row(rg_ba[0]), _bf16(rg_wi[0]), _row(rg_bi[0]),
        _row(rg_lambda[0]), _bf16(rg_wo[0]), _row(rg_bo[0]), wrt, brt, expert_weights, 1)
    _, ys, dest = _moe_experts(hrows, cls, rank, cnt, weights1, n_blocks, xs)
    return _final_call(dest, x1, mod1, ys, bsz, fmod, _row(fin_g))
```

```python
import functools
import math

import jax
import jax.numpy as jnp
import numpy as np
from jax import lax
from jax.experimental import pallas as pl
from jax.experimental.pallas import tpu as pltpu

EPS = 1e-6
CONF_KERNEL = 31
LRU_HEADS = 4
LRU_CONV = 4
LRU_C = 8.0
N_GROUPS = 4
EXPERTS_PER_GROUP = 8
N_EXPERTS = N_GROUPS * EXPERTS_PER_GROUP
PAIRS_PER_GROUP = EXPERTS_PER_GROUP * (EXPERTS_PER_GROUP - 1) // 2
N_CLASSES = N_GROUPS * PAIRS_PER_GROUP

LANES = 128
SUBLANES = 8
BF16_SUBLANES = 16
VMEM_LIMIT_BYTES = 56 * 1024 * 1024

TOKEN_TILE = 512
TIME_TILE = TOKEN_TILE // SUBLANES
SUB_TILE = 256
ROW_BLOCK = 512
ROW_STEP = 128
PERM_PITCH = TIME_TILE + SUBLANES
CONV_ROWS = 64
ROUTER_ROWS = 64
PACK_SUBLANES = 4
MOD_ROWS = 8
DISPATCH_SLOTS = 3
DMA_UNROLL = 32

_NEG_INF = float("-inf")


def _class_tables():
    lo = np.zeros((N_CLASSES,), np.int32)
    hi = np.zeros((N_CLASSES,), np.int32)
    for g in range(N_GROUPS):
        for a in range(EXPERTS_PER_GROUP):
            for b in range(a + 1, EXPERTS_PER_GROUP):
                c = g * PAIRS_PER_GROUP + (a * (2 * EXPERTS_PER_GROUP - 1 - a)) // 2 + (b - a - 1)
                lo[c] = g * EXPERTS_PER_GROUP + a
                hi[c] = g * EXPERTS_PER_GROUP + b
    return lo, hi


_CLASS_LO, _CLASS_HI = _class_tables()


def _bf16(x):
    return x.astype(jnp.bfloat16)


def _dot(a, b):
    return jnp.dot(a, b, preferred_element_type=jnp.float32)


def _bits(x):
    return lax.bitcast_convert_type(x, jnp.uint32)


def _f32_from_bits(x):
    return lax.bitcast_convert_type(x, jnp.float32)


def _gelu_tanh(x):
    c0 = math.sqrt(2.0 / math.pi)
    t = jnp.tanh(x * (c0 + (c0 * 0.044715) * (x * x)))
    hx = 0.5 * x
    return hx + hx * t


def _token_rows(ref, sublane, n):
    return ref[pl.ds(sublane, n, stride=SUBLANES), :]


def _per_batch(v, ts):
    return jnp.tile(v, (ts // SUBLANES, 1))


def _ada_kernel(c_ref, w_ref, b_ref, o_ref):
    c = c_ref[...]
    ca = _bf16(c * jax.nn.sigmoid(c))
    o_ref[...] = _dot(ca, _bf16(w_ref[...])) + b_ref[...]


def _ada_call(c, w, b):
    n_layers, d, n = w.shape
    bsz = c.shape[0]
    nb = 1024
    return pl.pallas_call(
        _ada_kernel,
        out_shape=jax.ShapeDtypeStruct((n_layers, bsz, n), jnp.float32),
        grid=(n_layers, n // nb),
        in_specs=[
            pl.BlockSpec((bsz, d), lambda l, j: (0, 0)),
            pl.BlockSpec((None, d, nb), lambda l, j: (l, 0, j)),
            pl.BlockSpec((None, 1, nb), lambda l, j: (l, 0, j)),
        ],
        out_specs=pl.BlockSpec((None, bsz, nb), lambda l, j: (l, 0, j)),
        compiler_params=pltpu.CompilerParams(
            dimension_semantics=("arbitrary", "arbitrary"),
            vmem_limit_bytes=VMEM_LIMIT_BYTES),
        name="ada_mod",
    )(c, w, b)


def _rms_mod(x, gain, shift, scale):
    ts = x.shape[0]
    ms = jnp.mean(x * x, axis=-1, keepdims=True)
    return (x * lax.rsqrt(ms + EPS)) * _per_batch(gain * (1.0 + scale), ts) + _per_batch(shift, ts)


def _causal_taps(w_ref, bias_ref, ext_ref, n_taps, row0, n):
    d = ext_ref.shape[1]
    cols = []
    for cb in range(d // LANES):
        sl = slice(cb * LANES, (cb + 1) * LANES)
        acc = jnp.broadcast_to(bias_ref[:, sl], (n, LANES))
        for k in range(n_taps):
            lo = row0 + SUBLANES * k
            acc = acc + w_ref[k:k + 1, sl] * ext_ref[lo:lo + n, sl]
        cols.append(acc)
    return jnp.concatenate(cols, axis=1)


def _causal_taps_unit(w_ref, bias_ref, ext_ref, out_ref, cb, r0, n_taps):
    w_cb, ext_cb = w_ref.at[cb], ext_ref.at[cb]
    acc = jnp.broadcast_to(bias_ref[cb], (CONV_ROWS, LANES))
    for k in range(n_taps):
        win = ext_cb[pl.ds(pl.multiple_of(r0 + SUBLANES * k, SUBLANES), CONV_ROWS), :]
        acc = acc + w_cb[k:k + 1, :] * win
    out_ref.at[cb][pl.ds(r0, CONV_ROWS), :] = acc


def _start_tile_dmas(make_copy, n, inline=False, both_queues=False):
    if inline:
        for r in range(n):
            make_copy(r).start(priority=r % 2 if both_queues else 0)
        return

    def body(g, carry):
        for k in range(DMA_UNROLL):
            make_copy(g * DMA_UNROLL + k).start(priority=k % 2 if both_queues else 0)
        return carry
    lax.fori_loop(0, n // DMA_UNROLL, body, 0)


def _wait_all_tiles(hbm_ref, vmem_tiles_ref, sem, to_hbm):
    hbm_view = hbm_ref.at[pl.ds(0, vmem_tiles_ref.shape[0])]
    if to_hbm:
        pltpu.make_async_copy(vmem_tiles_ref, hbm_view, sem).wait()
    else:
        pltpu.make_async_copy(hbm_view, vmem_tiles_ref, sem).wait()


def _gather_expert_tiles(ys_ref, ybuf, sem, dest_ref, dnext_ref, step, n_steps, ts, inline):
    slot = step % 2

    def tile_copy(dref, sl):
        def make(r):
            src = pl.multiple_of(dref[0, 0, r] * SUBLANES, SUBLANES)
            return pltpu.make_async_copy(
                ys_ref.at[pl.ds(src, SUBLANES)],
                ybuf.at[sl, pl.ds(pl.multiple_of(r * SUBLANES, SUBLANES), SUBLANES)],
                sem.at[sl])
        return make

    @pl.when(step == 0)
    def _():
        _start_tile_dmas(tile_copy(dest_ref, 0), ts)

    if not inline:
        @pl.when(step + 1 < n_steps)
        def _():
            _start_tile_dmas(tile_copy(dnext_ref, 1 - slot), ts)

    _wait_all_tiles(ys_ref, ybuf.at[slot], sem.at[slot], to_hbm=False)
    if inline:
        _start_tile_dmas(tile_copy(dnext_ref, 1 - slot), ts, inline=True)
    yv = ybuf.at[slot]
    y = jnp.concatenate([_token_rows(yv, s, ts) for s in range(SUBLANES)], axis=1)

    def retire_last():
        if inline:
            @pl.when(step + 1 == n_steps)
            def _():
                _wait_all_tiles(ys_ref, ybuf.at[1 - slot], sem.at[1 - slot], to_hbm=False)

    return y, retire_last


def _route(logits_t):
    ts = logits_t.shape[1]
    row = lax.broadcasted_iota(jnp.int32, (SUBLANES, ts), 0)
    gl = jnp.where(row < N_GROUPS, logits_t[0:SUBLANES], _NEG_INF)
    gmax = jnp.max(gl, axis=0, keepdims=True)
    gsel = jnp.min(jnp.where(gl == gmax, row, SUBLANES), axis=0, keepdims=True)
    p_g = 1.0 / jnp.sum(jnp.exp(gl - gmax), axis=0, keepdims=True)
    es = jnp.zeros((SUBLANES, ts), jnp.float32)
    for g in range(N_GROUPS):
        lo = SUBLANES * (1 + g)
        es = jnp.where(gsel == g, logits_t[lo:lo + EXPERTS_PER_GROUP], es)
    m1 = jnp.max(es, axis=0, keepdims=True)
    i1 = jnp.min(jnp.where(es == m1, row, SUBLANES), axis=0, keepdims=True)
    es2 = jnp.where(row == i1, _NEG_INF, es)
    m2 = jnp.max(es2, axis=0, keepdims=True)
    i2 = jnp.min(jnp.where(es2 == m2, row, SUBLANES), axis=0, keepdims=True)
    t = jnp.exp(m2 - m1)
    w1 = p_g / (1.0 + t)
    w2 = w1 * t
    first_lo = i1 < i2
    e_lo = jnp.minimum(i1, i2)
    e_hi = jnp.maximum(i1, i2)
    gate_lo = jnp.where(first_lo, w1, w2)
    gate_hi = jnp.where(first_lo, w2, w1)
    pair = lax.shift_right_logical(e_lo * (2 * EXPERTS_PER_GROUP - 1 - e_lo), 1) + (e_hi - e_lo - 1)
    cls = gsel * PAIRS_PER_GROUP + pair
    return cls, gate_lo, gate_hi


def _post_mixer(x1, mod_ref, n2_ref, wrt_ref, brt_ref, hrow_ref, cls_ref, rank_ref, cnt_ref, sub):
    ts = x1.shape[0]
    hrow_ref = hrow_ref.at[pl.ds(sub * ts * SUBLANES, ts * SUBLANES)]
    cols = slice(sub * ts, (sub + 1) * ts)
    h2 = _rms_mod(x1, n2_ref[...], mod_ref[3], mod_ref[4])
    h2b = _bf16(h2)
    logits_t = lax.dot_general(wrt_ref[...], h2b, (((1,), (1,)), ((), ())),
                               preferred_element_type=jnp.float32) + brt_ref[:, 0:1]
    cls, gate_lo, gate_hi = _route(logits_t)
    gates_t = jnp.concatenate(
        [gate_lo, gate_hi, jnp.zeros((LANES - 2, ts), jnp.float32)], axis=0)
    h2r = h2b.astype(jnp.float32)
    for j in range(PACK_SUBLANES):
        hi = _bits(h2r[:, (2 * j) * LANES:(2 * j + 1) * LANES])
        lo = _bits(h2r[:, (2 * j + 1) * LANES:(2 * j + 2) * LANES])
        hrow_ref[pl.ds(j, ts, stride=SUBLANES), :] = hi | lax.shift_right_logical(lo, jnp.uint32(16))
    hrow_ref[pl.ds(PACK_SUBLANES, ts, stride=SUBLANES), :] = _bits(gates_t.T)
    for j in range(PACK_SUBLANES + 1, SUBLANES):
        hrow_ref[pl.ds(j, ts, stride=SUBLANES), :] = jnp.zeros((ts, LANES), jnp.uint32)
    cls_ref[0, :, cols] = cls
    crow = lax.broadcasted_iota(jnp.int32, (LANES, ts), 0)
    onehot = jnp.where(crow == cls, 1.0, 0.0)
    cnt = jnp.sum(onehot, axis=1, keepdims=True)
    earlier = (lax.broadcasted_iota(jnp.int32, (ts, ts), 0)
               < lax.broadcasted_iota(jnp.int32, (ts, ts), 1))
    prefix = _dot(_bf16(onehot), _bf16(jnp.where(earlier, 1.0, 0.0)))
    rank = jnp.sum(onehot * (prefix + cnt_ref[:, 0:1]), axis=0, keepdims=True)
    rank_ref[0, :, cols] = rank.astype(jnp.int32)
    cnt_ref[...] += jnp.broadcast_to(cnt, cnt_ref.shape)


def _mixer_out(t, d, ts):
    n_tiles = t // ts
    out_shape = (
        jax.ShapeDtypeStruct((t, d), jnp.float32),
        jax.ShapeDtypeStruct((t * SUBLANES, LANES), jnp.uint32),
        jax.ShapeDtypeStruct((n_tiles, 1, ts), jnp.int32),
        jax.ShapeDtypeStruct((n_tiles, 1, ts), jnp.int32),
        jax.ShapeDtypeStruct((LANES, LANES), jnp.float32),
    )
    out_specs = (
        pl.BlockSpec((ts, d), lambda i: (i, 0)),
        pl.BlockSpec((ts * SUBLANES, LANES), lambda i: (i, 0)),
        pl.BlockSpec((1, 1, ts), lambda i: (i, 0, 0)),
        pl.BlockSpec((1, 1, ts), lambda i: (i, 0, 0)),
        pl.BlockSpec((LANES, LANES), lambda i: (0, 0)),
    )
    return out_shape, out_specs


def _full(shape):
    return pl.BlockSpec(shape, lambda i: tuple(0 for _ in shape))


def _expert_weight_cast(weights, layer, n_steps):
    flat = [w.reshape(w.shape[0], w.shape[1] * w.shape[2], w.shape[3]) for w in weights]
    if any(w.shape[1] % (n_steps * BF16_SUBLANES) for w in flat):
        return [], [], (), ()
    in_specs = [pl.BlockSpec((None, w.shape[1] // n_steps, w.shape[2]), lambda i: (layer, i, 0))
                for w in flat]
    out_shapes = tuple(jax.ShapeDtypeStruct(w.shape[1:], jnp.bfloat16) for w in flat)
    out_specs = tuple(pl.BlockSpec((w.shape[1] // n_steps, w.shape[2]), lambda i: (i, 0))
                      for w in flat)
    return flat, in_specs, out_shapes, out_specs


def _cast_slices(cast_in, cast_out):
    for src, dst in zip(cast_in, cast_out):
        dst[...] = _bf16(src[...])


def _cast_results(outs, weights, layer):
    if outs:
        return [o.reshape(w.shape[1:]) for o, w in zip(outs, weights)]
    return [_bf16(w[layer]) for w in weights]


def _to_tile_order(x_ref, perm, xs):
    bsz, tm, d = x_ref.shape
    for cb in range(d // LANES):
        for p in range(bsz):
            perm[cb, PERM_PITCH * p:PERM_PITCH * p + tm, :] = x_ref[p, :, cb * LANES:(cb + 1) * LANES]
    for m in range(tm):
        for cb in range(d // LANES):
            xs[m * bsz:(m + 1) * bsz, cb * LANES:(cb + 1) * LANES] = (
                perm.at[cb][pl.ds(m, bsz, stride=PERM_PITCH), :])


def _conf_kernel(n_cast, x_ref, mod_ref, n1_ref, n2_ref, w1_ref, b1_ref, dw_ref, dwb_ref,
                 lng_ref, lnb_ref, w2_ref, b2_ref, wrt_ref, brt_ref, *rest):
    cast_in, rest = rest[:n_cast], rest[n_cast:]
    x1_ref, hrow_ref, cls_ref, rank_ref, cnt_ref, zeros_ref = rest[:6]
    cast_out, (perm, xs, vext, conv) = rest[6:6 + n_cast], rest[6 + n_cast:]
    _cast_slices(cast_in, cast_out)
    i = pl.program_id(0)
    bsz, tm, d = x_ref.shape
    ts = bsz * tm
    n_chunks = d // LANES
    hist = SUBLANES * (CONF_KERNEL - 1)

    @pl.when(i == 0)
    def _():
        vext[:, 0:hist, :] = jnp.zeros((n_chunks, hist, LANES), jnp.float32)
        cnt_ref[...] = jnp.zeros_like(cnt_ref)

    zeros_ref[...] = jnp.zeros_like(zeros_ref)

    _to_tile_order(x_ref, perm, xs)
    for sub in range(ts // SUB_TILE):
        rows = slice(sub * SUB_TILE, (sub + 1) * SUB_TILE)
        h = _bf16(_rms_mod(xs[rows, :], n1_ref[...], mod_ref[0], mod_ref[1]))
        u = _dot(h, w1_ref[...]) + b1_ref[...]
        v = u[:, 0:d] * jax.nn.sigmoid(u[:, d:2 * d])
        for cb in range(n_chunks):
            vext[cb, hist + sub * SUB_TILE:hist + (sub + 1) * SUB_TILE, :] = v[:, cb * LANES:(cb + 1) * LANES]

    units_per_chunk = ts // CONV_ROWS

    def conv_unit(idx, carry):
        r0 = pl.multiple_of((idx % units_per_chunk) * CONV_ROWS, CONV_ROWS)
        _causal_taps_unit(dw_ref, dwb_ref, vext, conv, idx // units_per_chunk, r0, CONF_KERNEL)
        return carry

    lax.fori_loop(0, n_chunks * units_per_chunk, conv_unit, 0, unroll=32)
    vext[:, 0:hist, :] = vext[:, ts:ts + hist, :]

    for sub in range(ts // SUB_TILE):
        rows = slice(sub * SUB_TILE, (sub + 1) * SUB_TILE)
        y = jnp.concatenate([conv[cb, rows, :] for cb in range(n_chunks)], axis=1)
        mu = jnp.mean(y, axis=-1, keepdims=True)
        yc = y - mu
        var = jnp.mean(yc * yc, axis=-1, keepdims=True)
        z = yc * lax.rsqrt(var + EPS) * lng_ref[...] + lnb_ref[...]
        z = z * jax.nn.sigmoid(z)
        mix = _dot(_bf16(z), w2_ref[...]) + b2_ref[...]
        x1 = xs[rows, :] + _per_batch(mod_ref[2], SUB_TILE) * mix
        x1_ref[rows, :] = x1
        _post_mixer(x1, mod_ref, n2_ref, wrt_ref, brt_ref, hrow_ref, cls_ref, rank_ref, cnt_ref, sub)


def _conf_call(x, mod, n1, n2, w1, b1, dw, dwb, lng, lnb, w2, b2, wrt, brt, zero_rows,
               expert_weights, layer):
    bsz, seq, d = x.shape
    tm = TIME_TILE
    ts = bsz * tm
    n_steps = seq // tm
    assert zero_rows % (n_steps * SUBLANES) == 0
    cast, cast_specs, cast_shapes, cast_out_specs = _expert_weight_cast(expert_weights, layer, n_steps)
    out_shape, out_specs = _mixer_out(bsz * seq, d, ts)
    out_shape += (jax.ShapeDtypeStruct((zero_rows, LANES), jnp.uint32),) + cast_shapes
    out_specs += (pl.BlockSpec((zero_rows // n_steps, LANES), lambda i: (i, 0)),) + cast_out_specs
    hist = SUBLANES * (CONF_KERNEL - 1)
    outs = pl.pallas_call(
        functools.partial(_conf_kernel, len(cast)),
        out_shape=out_shape,
        grid=(n_steps,),
        in_specs=[
            pl.BlockSpec((bsz, tm, d), lambda i: (0, i, 0)),
            _full((MOD_ROWS, bsz, d)),
            _full((1, d)), _full((1, d)),
            _full((d, 2 * d)), _full((1, 2 * d)),
            _full((d // LANES, CONF_KERNEL + 1, LANES)), _full((d // LANES, 1, LANES)),
            _full((1, d)), _full((1, d)),
            _full((d, d)), _full((1, d)),
            _full((ROUTER_ROWS, d)), _full((ROUTER_ROWS, LANES)),
        ] + cast_specs,
        out_specs=out_specs,
        scratch_shapes=[
            pltpu.VMEM((d // LANES, bsz * PERM_PITCH, LANES), jnp.float32),
            pltpu.VMEM((ts, d), jnp.float32),
            pltpu.VMEM((d // LANES, hist + ts, LANES), jnp.float32),
            pltpu.VMEM((d // LANES, ts, LANES), jnp.float32),
        ],
        compiler_params=pltpu.CompilerParams(
            dimension_semantics=("arbitrary",), vmem_limit_bytes=VMEM_LIMIT_BYTES),
        name="conf_mixer",
    )(x, mod, n1, n2, w1, b1, dw, dwb, lng, lnb, w2, b2, wrt, brt, *cast)
    return outs[:6], _cast_results(outs[6:], expert_weights, layer)


def _lru_kernel(n_tiles, n_cast, dest_ref, dnext_ref, xprev_ref, pmod_ref, mod_ref, n1_ref, n2_ref,
                wy_ref, by_ref, wx_ref, bx_ref,
                cw_ref, cb_ref, wa_ref, ba_ref, wi_ref, bi_ref, lam_ref, wo_ref, bo_ref,
                wrt_ref, brt_ref, ys_ref, *rest):
    cast_in, rest = rest[:n_cast], rest[n_cast:]
    x1_ref, hrow_ref, cls_ref, rank_ref, cnt_ref = rest[:5]
    cast_out, (uext, hcar, ybuf, ysem) = rest[5:5 + n_cast], rest[5 + n_cast:]
    _cast_slices(cast_in, cast_out)
    i = pl.program_id(0)
    ts, d = xprev_ref.shape
    hd = d // LRU_HEADS
    hist = SUBLANES * (LRU_CONV - 1)

    @pl.when(i == 0)
    def _():
        uext[0:hist, :] = jnp.zeros((hist, d), jnp.float32)
        hcar[...] = jnp.zeros_like(hcar)
        cnt_ref[...] = jnp.zeros_like(cnt_ref)

    y_prev, retire_last = _gather_expert_tiles(ys_ref, ybuf, ysem, dest_ref, dnext_ref, i, n_tiles, ts,
                                               inline=True)
    nl = -lam_ref[...]
    decay = LRU_C * (jnp.maximum(nl, 0.0) + jnp.log1p(jnp.exp(-jnp.abs(nl))))
    hstate = hcar[...]
    for sub in range(ts // SUB_TILE):
        rows = slice(sub * SUB_TILE, (sub + 1) * SUB_TILE)
        x = xprev_ref[rows, :] + _per_batch(pmod_ref[5], SUB_TILE) * y_prev[rows, :]
        h = _bf16(_rms_mod(x, n1_ref[...], mod_ref[0], mod_ref[1]))
        ygate = _gelu_tanh(_dot(h, wy_ref[...]) + by_ref[...])
        xb = _dot(h, wx_ref[...]) + bx_ref[...]
        uext[hist + sub * SUB_TILE:hist + (sub + 1) * SUB_TILE, :] = xb
        u = _causal_taps(cw_ref, cb_ref, uext, LRU_CONV, sub * SUB_TILE, SUB_TILE)

        ub = _bf16(u)
        r_parts, i_parts = [], []
        for hh in range(LRU_HEADS):
            uh = ub[:, hh * hd:(hh + 1) * hd]
            r_parts.append(_dot(uh, wa_ref[hh]))
            i_parts.append(_dot(uh, wi_ref[hh]))
        r = jax.nn.sigmoid(jnp.concatenate(r_parts, axis=1) + ba_ref[...])
        ig = jax.nn.sigmoid(jnp.concatenate(i_parts, axis=1) + bi_ref[...])

        neg_log_a = r * decay
        a = jnp.exp(-neg_log_a)
        mult = jnp.sqrt(jnp.tanh(neg_log_a) * (1.0 + a * a))
        b = mult * (ig * u)

        outs = []
        for m in range(SUB_TILE // SUBLANES):
            sl = slice(m * SUBLANES, (m + 1) * SUBLANES)
            hstate = a[sl] * hstate + b[sl]
            outs.append(hstate)
        hs = jnp.concatenate(outs, axis=0)

        mix = _dot(_bf16(hs * ygate), wo_ref[...]) + bo_ref[...]
        x1 = x + _per_batch(mod_ref[2], SUB_TILE) * mix
        x1_ref[rows, :] = x1
        _post_mixer(x1, mod_ref, n2_ref, wrt_ref, brt_ref, hrow_ref, cls_ref, rank_ref, cnt_ref, sub)
    hcar[...] = hstate
    uext[0:hist, :] = uext[ts:ts + hist, :]
    retire_last()


def _lru_call(dest, x_prev, mod_prev, ys, mod, n1, n2, wy, by, wx, bx, cw, cb, wa, ba, wi, bi, lam,
              wo, bo, wrt, brt, expert_weights, layer):
    t, d = x_prev.shape
    n_tiles, _, ts = dest.shape
    bsz = mod.shape[1]
    hd = d // LRU_HEADS
    hist = SUBLANES * (LRU_CONV - 1)
    cast, cast_specs, cast_shapes, cast_out_specs = _expert_weight_cast(expert_weights, layer, n_tiles)
    out_shape, out_specs = _mixer_out(t, d, ts)
    out_shape += cast_shapes
    out_specs += cast_out_specs
    outs = pl.pallas_call(
        functools.partial(_lru_kernel, n_tiles, len(cast)),
        out_shape=out_shape,
        grid=(n_tiles,),
        in_specs=[
            pl.BlockSpec((1, 1, ts), lambda i: (i, 0, 0), memory_space=pltpu.SMEM),
            pl.BlockSpec((1, 1, ts), lambda i: (jnp.minimum(i + 1, n_tiles - 1), 0, 0),
                         memory_space=pltpu.SMEM),
            pl.BlockSpec((ts, d), lambda i: (i, 0)),
            _full((MOD_ROWS, bsz, d)), _full((MOD_ROWS, bsz, d)),
            _full((1, d)), _full((1, d)),
            _full((d, d)), _full((1, d)),
            _full((d, d)), _full((1, d)),
            _full((LRU_CONV, d)), _full((1, d)),
            _full((LRU_HEADS, hd, hd)), _full((1, d)),
            _full((LRU_HEADS, hd, hd)), _full((1, d)),
            _full((1, d)),
            _full((d, d)), _full((1, d)),
            _full((ROUTER_ROWS, d)), _full((ROUTER_ROWS, LANES)),
            pl.BlockSpec(memory_space=pl.ANY),
        ] + cast_specs,
        out_specs=out_specs,
        scratch_shapes=[
            pltpu.VMEM((hist + ts, d), jnp.float32),
            pltpu.VMEM((SUBLANES, d), jnp.float32),
            pltpu.VMEM((2, ts * SUBLANES, LANES), jnp.float32),
            pltpu.SemaphoreType.DMA((2,)),
        ],
        compiler_params=pltpu.CompilerParams(
            dimension_semantics=("arbitrary",), vmem_limit_bytes=VMEM_LIMIT_BYTES),
        name="lru_mixer",
    )(dest, dest, x_prev, mod_prev, mod, n1, n2, wy, by, wx, bx, cw, cb, wa, ba, wi, bi, lam, wo, bo,
      wrt, brt, ys, *cast)
    return outs[:5], _cast_results(outs[5:], expert_weights, layer)


def _plan_kernel(n_blocks, cnt_ref, clo_ref, chi_ref, cls_ref, rank_ref,
                 dest_ref, blo_ref, bhi_ref, bvalid_ref, nblk_ref, class_start):
    n_tiles, _, ts = cls_ref.shape
    class_start[...] = jnp.zeros_like(class_start)

    def per_class(c, nb_total):
        class_start[pl.ds(c, 1), :] = jnp.full((1, LANES), nb_total * ROW_BLOCK, jnp.int32)
        nb = (cnt_ref[c] + (ROW_BLOCK - 1)) // ROW_BLOCK

        def fill(k, carry):
            blo_ref[nb_total + k] = clo_ref[c]
            bhi_ref[nb_total + k] = chi_ref[c]
            bvalid_ref[nb_total + k] = jnp.minimum(cnt_ref[c] - k * ROW_BLOCK, ROW_BLOCK)
            return carry

        lax.fori_loop(0, nb, fill, 0)
        return nb_total + nb

    nb_used = lax.fori_loop(0, N_CLASSES, per_class, 0)
    nblk_ref[0] = nb_used
    last = jnp.maximum(nb_used - 1, 0)

    def tail(k, carry):
        blo_ref[k] = blo_ref[last]
        bhi_ref[k] = bhi_ref[last]
        bvalid_ref[k] = 0
        return carry

    lax.fori_loop(nb_used, n_blocks, tail, 0)

    starts = class_start[:, 0:1]
    class_id = lax.broadcasted_iota(jnp.int32, (LANES, ts), 0)

    def per_tile(t, carry):
        first = jnp.sum(jnp.where(class_id == cls_ref[t], starts, 0), axis=0, keepdims=True)
        dest_ref[t] = first + rank_ref[t]
        return carry

    lax.fori_loop(0, n_tiles, per_tile, 0)


def _plan_call(cls, rank, counts, n_blocks):
    n_tiles, _, ts = cls.shape
    smem_full = pl.BlockSpec(memory_space=pltpu.SMEM)
    vmem_full = pl.BlockSpec(memory_space=pltpu.VMEM)
    return pl.pallas_call(
        functools.partial(_plan_kernel, n_blocks),
        out_shape=(
            jax.ShapeDtypeStruct((n_tiles, 1, ts), jnp.int32),
            jax.ShapeDtypeStruct((n_blocks,), jnp.int32),
            jax.ShapeDtypeStruct((n_blocks,), jnp.int32),
            jax.ShapeDtypeStruct((n_blocks,), jnp.int32),
            jax.ShapeDtypeStruct((1,), jnp.int32),
        ),
        in_specs=[smem_full, smem_full, smem_full, vmem_full, vmem_full],
        out_specs=(vmem_full, smem_full, smem_full, smem_full, smem_full),
        scratch_shapes=[pltpu.VMEM((LANES, LANES), jnp.int32)],
        compiler_params=pltpu.CompilerParams(vmem_limit_bytes=VMEM_LIMIT_BYTES),
        name="moe_plan",
    )(counts, jnp.asarray(_CLASS_LO), jnp.asarray(_CLASS_HI), cls, rank)


def _dispatch_kernel(n_tiles, dest_ref, hrow_ref, xs_in_ref, xs_ref, stage, sem_in, sem_out):
    del xs_in_ref
    i = pl.program_id(0)
    ts = dest_ref.shape[2]
    rows = ts * SUBLANES

    def load(tile, slot):
        return pltpu.make_async_copy(hrow_ref.at[pl.ds(pl.multiple_of(tile * rows, rows), rows)],
                                     stage.at[slot], sem_in.at[slot])

    def tile_copy(slot, r, dst_row):
        return pltpu.make_async_copy(
            stage.at[slot, pl.ds(pl.multiple_of(r * SUBLANES, SUBLANES), SUBLANES)],
            xs_ref.at[pl.ds(pl.multiple_of(dst_row * SUBLANES, SUBLANES), SUBLANES)],
            sem_out.at[slot])

    def drain(slot):
        _wait_all_tiles(xs_ref, stage.at[slot], sem_out.at[slot], to_hbm=True)

    slot = i % DISPATCH_SLOTS
    nxt = (i + 1) % DISPATCH_SLOTS

    @pl.when(i == 0)
    def _():
        load(0, 0).start()

    @pl.when(i >= 2)
    def _():
        drain(nxt)

    @pl.when(i + 1 < n_tiles)
    def _():
        load(i + 1, nxt).start()

    load(i, slot).wait()

    _start_tile_dmas(lambda r: tile_copy(slot, r, dest_ref[0, 0, r]), ts, inline=True, both_queues=True)

    @pl.when(i == n_tiles - 1)
    def _():
        @pl.when(i >= 1)
        def _():
            drain((i + DISPATCH_SLOTS - 1) % DISPATCH_SLOTS)
        drain(slot)


def _dispatch_call(dest, hrows, xs_init):
    n_tiles, _, ts = dest.shape
    assert xs_init.dtype == jnp.uint32 and xs_init.shape[1] == LANES
    any_spec = pl.BlockSpec(memory_space=pl.ANY)
    return pl.pallas_call(
        functools.partial(_dispatch_kernel, n_tiles),
        out_shape=jax.ShapeDtypeStruct(xs_init.shape, jnp.uint32),
        grid=(n_tiles,),
        in_specs=[
            pl.BlockSpec((1, 1, ts), lambda i: (i, 0, 0), memory_space=pltpu.SMEM),
            any_spec, any_spec,
        ],
        out_specs=any_spec,
        scratch_shapes=[
            pltpu.VMEM((DISPATCH_SLOTS, ts * SUBLANES, LANES), jnp.uint32),
            pltpu.SemaphoreType.DMA((DISPATCH_SLOTS,)),
            pltpu.SemaphoreType.DMA((DISPATCH_SLOTS,)),
        ],
        input_output_aliases={2: 0},
        compiler_params=pltpu.CompilerParams(
            dimension_semantics=("arbitrary",), vmem_limit_bytes=VMEM_LIMIT_BYTES),
        name="moe_dispatch",
    )(dest, hrows, xs_init)


def _expert_kernel(blo_ref, bhi_ref, bvalid_ref, nblk_ref, xs_ref,
                   wg_lo, wu_lo, wd_lo, wg_hi, wu_hi, wd_hi, ys_ref):
    del blo_ref, bhi_ref, nblk_ref
    valid = bvalid_ref[pl.program_id(0)]

    def experts_on(m):
        pieces = []
        for j in range(PACK_SUBLANES):
            w = _token_rows(xs_ref, j, m)
            pieces.append(_bf16(_f32_from_bits(w & jnp.uint32(0xFFFF0000))))
            pieces.append(_bf16(_f32_from_bits(lax.shift_left(w, jnp.uint32(16)))))
        xb = jnp.concatenate(pieces, axis=1)
        gates = _f32_from_bits(_token_rows(xs_ref, PACK_SUBLANES, m))
        acc = None
        for idx, (wg, wu, wd) in enumerate(((wg_lo, wu_lo, wd_lo), (wg_hi, wu_hi, wd_hi))):
            a = _dot(xb, wg[...])
            hid = (a * jax.nn.sigmoid(a)) * _dot(xb, wu[...]) * gates[:, idx:idx + 1]
            part = _dot(_bf16(hid), wd[...])
            acc = part if acc is None else acc + part
        for s in range(SUBLANES):
            ys_ref[pl.ds(s, m, stride=SUBLANES), :] = acc[:, s * LANES:(s + 1) * LANES]

    for m in range(ROW_STEP, ROW_BLOCK + 1, ROW_STEP):
        pl.when((valid > m - ROW_STEP) & (valid <= m))(functools.partial(experts_on, m))


def _expert_call(blo, bhi, bvalid, nblk, xs, wg, wu, wd):
    n_rows = xs.shape[0] // SUBLANES
    d, de = wg.shape[1], wg.shape[2]
    assert d == SUBLANES * LANES
    n_blocks = n_rows // ROW_BLOCK

    def w_spec(shape, which):
        if which == 0:
            return pl.BlockSpec((None,) + shape, lambda b, lo, hi, bv, n: (lo[b], 0, 0))
        return pl.BlockSpec((None,) + shape, lambda b, lo, hi, bv, n: (hi[b], 0, 0))

    def row_map(b, lo, hi, bv, n):
        return (jnp.minimum(b, jnp.maximum(n[0] - 1, 0)), 0)

    grid_spec = pltpu.PrefetchScalarGridSpec(
        num_scalar_prefetch=4,
        grid=(n_blocks,),
        in_specs=[
            pl.BlockSpec((ROW_BLOCK * SUBLANES, LANES), row_map),
            w_spec((d, de), 0), w_spec((d, de), 0), w_spec((de, d), 0),
            w_spec((d, de), 1), w_spec((d, de), 1), w_spec((de, d), 1),
        ],
        out_specs=pl.BlockSpec((ROW_BLOCK * SUBLANES, LANES), row_map),
    )
    return pl.pallas_call(
        _expert_kernel,
        out_shape=jax.ShapeDtypeStruct((n_rows * SUBLANES, LANES), jnp.float32),
        grid_spec=grid_spec,
        compiler_params=pltpu.CompilerParams(
            dimension_semantics=("arbitrary",), vmem_limit_bytes=VMEM_LIMIT_BYTES),
        name="moe_experts",
    )(blo, bhi, bvalid, nblk, xs, wg, wu, wd, wg, wu, wd)


def _final_kernel(n_tiles, dest_ref, dnext_ref, x1_ref, mod_ref, fmod_ref, fg_ref, ys_ref,
                  out_ref, ybuf, sem, unperm):
    i = pl.program_id(0)
    ts, d = x1_ref.shape
    bsz, tm, _ = out_ref.shape
    y, retire_last = _gather_expert_tiles(ys_ref, ybuf, sem, dest_ref, dnext_ref, i, n_tiles, ts,
                                          inline=False)
    x2 = x1_ref[...] + _per_batch(mod_ref[5], ts) * y
    res = _rms_mod(x2, fg_ref[...], fmod_ref[0], fmod_ref[1])
    for cb in range(d // LANES):
        unperm[cb] = res[:, cb * LANES:(cb + 1) * LANES]
    for cb in range(d // LANES):
        for p in range(bsz):
            out_ref[p, :, cb * LANES:(cb + 1) * LANES] = unperm.at[cb][pl.ds(p, tm, stride=SUBLANES), :]
    retire_last()


def _final_call(dest, x1, mod, ys, bsz, fmod, fin_g):
    t, d = x1.shape
    n_tiles, _, ts = dest.shape
    tm = ts // bsz
    return pl.pallas_call(
        functools.partial(_final_kernel, n_tiles),
        out_shape=jax.ShapeDtypeStruct((bsz, t // bsz, d), jnp.float32),
        grid=(n_tiles,),
        in_specs=[
            pl.BlockSpec((1, 1, ts), lambda i: (i, 0, 0), memory_space=pltpu.SMEM),
            pl.BlockSpec((1, 1, ts), lambda i: (jnp.minimum(i + 1, n_tiles - 1), 0, 0),
                         memory_space=pltpu.SMEM),
            pl.BlockSpec((ts, d), lambda i: (i, 0)),
            _full((MOD_ROWS, bsz, d)), _full((MOD_ROWS, bsz, d)), _full((1, d)),
            pl.BlockSpec(memory_space=pl.ANY),
        ],
        out_specs=pl.BlockSpec((bsz, tm, d), lambda i: (0, i, 0)),
        scratch_shapes=[pltpu.VMEM((2, ts * SUBLANES, LANES), jnp.float32),
                        pltpu.SemaphoreType.DMA((2,)),
                        pltpu.VMEM((d // LANES, ts, LANES), jnp.float32)],
        compiler_params=pltpu.CompilerParams(
            dimension_semantics=("arbitrary",), vmem_limit_bytes=VMEM_LIMIT_BYTES),
        name="moe_combine_final",
    )(dest, dest, x1, mod, fmod, fin_g, ys)


def _router_operands(wg, bg, wr, br):
    d = wg.shape[0]
    pad_g = jnp.zeros((SUBLANES - N_GROUPS, d), jnp.float32)
    pad_e = jnp.zeros((ROUTER_ROWS - SUBLANES - N_EXPERTS, d), jnp.float32)
    wrt = jnp.concatenate([wg.T, pad_g, wr.T, pad_e], axis=0)
    bcol = jnp.concatenate([bg, jnp.zeros((SUBLANES - N_GROUPS,), jnp.float32), br,
                            jnp.zeros((ROUTER_ROWS - SUBLANES - N_EXPERTS,), jnp.float32)])
    brt = jnp.broadcast_to(bcol[:, None], (ROUTER_ROWS, LANES))
    return _bf16(wrt), brt


def _mod_table(m, parts):
    bsz = m.shape[0]
    d = m.shape[1] // parts
    m = jnp.transpose(m.reshape(bsz, parts, d), (1, 0, 2))
    return jnp.concatenate([m, jnp.zeros((MOD_ROWS - parts, bsz, d), jnp.float32)], axis=0)


def _row(v):
    return v.reshape(1, -1)


def _moe_experts(hrows, cls, rank, cnt, weights_bf16, n_blocks, xs_init):
    counts = cnt[:, 0].astype(jnp.int32)
    dest, blo, bhi, bvalid, nblk = _plan_call(cls, rank, counts, n_blocks)
    xs = _dispatch_call(dest, hrows, xs_init)
    ys = _expert_call(blo, bhi, bvalid, nblk, xs, *weights_bf16)
    return xs, ys, dest


def kernel(x, c, ada_w, ada_b, norm1_g, norm2_g, cm_w1, cm_b1, cm_dw, cm_dwb, cm_ln_g, cm_ln_b, cm_w2, cm_b2, rg_wy, rg_by, rg_wx, rg_bx, rg_cw, rg_cb, rg_wa, rg_ba, rg_wi, rg_bi, rg_lambda, rg_wo, rg_bo, moe_wg, moe_bg, moe_wr, moe_br, moe_w_gate, moe_w_up, moe_w_down, fin_ada_w, fin_ada_b, fin_g):
    bsz, seq, d = x.shape
    t = bsz * seq
    depth = ada_w.shape[0]
    assert depth == 2 and bsz == SUBLANES and seq % TIME_TILE == 0
    assert d == 2 * PACK_SUBLANES * LANES

    mods = _ada_call(c, ada_w, ada_b[:, None, :])
    fmods = _ada_call(c, fin_ada_w[None], fin_ada_b[None, None, :])
    fmod = _mod_table(fmods[0], 2)
    n_steps = seq // TIME_TILE
    granule = n_steps // math.gcd(n_steps, ROW_BLOCK)
    n_blocks = -(-(-(-t // ROW_BLOCK) + N_CLASSES) // granule) * granule
    mod0, mod1 = _mod_table(mods[0], 6), _mod_table(mods[1], 6)

    wrt, brt = _router_operands(moe_wg[0], moe_bg[0], moe_wr[0], moe_br[0])
    dw = jnp.concatenate([cm_dw[0], jnp.zeros((1, d), jnp.float32)], axis=0)
    dw = jnp.transpose(dw.reshape(CONF_KERNEL + 1, d // LANES, LANES), (1, 0, 2))
    expert_weights = (moe_w_gate, moe_w_up, moe_w_down)
    (x1, hrows, cls, rank, cnt, xs_init), weights0 = _conf_call(
        x, mod0, _row(norm1_g[0]), _row(norm2_g[0]), _bf16(cm_w1[0]), _row(cm_b1[0]),
        dw, cm_dwb[0].reshape(d // LANES, 1, LANES), _row(cm_ln_g[0]), _row(cm_ln_b[0]),
        _bf16(cm_w2[0]), _row(cm_b2[0]), wrt, brt, n_blocks * ROW_BLOCK * SUBLANES,
        expert_weights, 0)
    xs, ys, dest = _moe_experts(hrows, cls, rank, cnt, weights0, n_blocks, xs_init)

    wrt, brt = _router_operands(moe_wg[1], moe_bg[1], moe_wr[1], moe_br[1])
    (x1, hrows, cls, rank, cnt), weights1 = _lru_call(
        dest, x1, mod0, ys, mod1, _row(norm1_g[1]), _row(norm2_g[1]),
        _bf16(rg_wy[0]), _row(rg_by[0]), _bf16(rg_wx[0]), _row(rg_bx[0]),
        rg_cw[0], _row(rg_cb[0]), _bf16(rg_wa[0]), _row(rg_ba[0]), _bf16(rg_wi[0]), _row(rg_bi[0]),
        _row(rg_lambda[0]), _bf16(rg_wo[0]), _row(rg_bo[0]), wrt, brt, expert_weights, 1)
    _, ys, dest = _moe_experts(hrows, cls, rank, cnt, weights1, n_blocks, xs)
    return _final_call(dest, x1, mod1, ys, bsz, fmod, _row(fin_g))
```

```python
import functools
import math

import jax
import jax.numpy as jnp
import numpy as np
from jax import lax
from jax.experimental import pallas as pl
from jax.experimental.pallas import tpu as pltpu

EPS = 1e-6
CONF_KERNEL = 31
LRU_HEADS = 4
LRU_CONV = 4
LRU_C = 8.0
N_GROUPS = 4
EXPERTS_PER_GROUP = 8
N_EXPERTS = N_GROUPS * EXPERTS_PER_GROUP
PAIRS_PER_GROUP = EXPERTS_PER_GROUP * (EXPERTS_PER_GROUP - 1) // 2
N_CLASSES = N_GROUPS * PAIRS_PER_GROUP

LANES = 128
SUBLANES = 8
BF16_SUBLANES = 16
VMEM_LIMIT_BYTES = 56 * 1024 * 1024

TOKEN_TILE = 512
TIME_TILE = TOKEN_TILE // SUBLANES
SUB_TILE = 256
ROW_BLOCK = 512
ROW_STEP = 64
PERM_PITCH = TIME_TILE + SUBLANES
CONV_ROWS = 64
ROUTER_ROWS = 64
PACK_SUBLANES = 4
MOD_ROWS = 8
DISPATCH_SLOTS = 3
DMA_UNROLL = 32

_NEG_INF = float("-inf")


def _class_tables():
    lo = np.zeros((N_CLASSES,), np.int32)
    hi = np.zeros((N_CLASSES,), np.int32)
    for g in range(N_GROUPS):
        for a in range(EXPERTS_PER_GROUP):
            for b in range(a + 1, EXPERTS_PER_GROUP):
                c = g * PAIRS_PER_GROUP + (a * (2 * EXPERTS_PER_GROUP - 1 - a)) // 2 + (b - a - 1)
                lo[c] = g * EXPERTS_PER_GROUP + a
                hi[c] = g * EXPERTS_PER_GROUP + b
    return lo, hi


_CLASS_LO, _CLASS_HI = _class_tables()


def _bf16(x):
    return x.astype(jnp.bfloat16)


def _dot(a, b):
    return jnp.dot(a, b, preferred_element_type=jnp.float32)


def _bits(x):
    return lax.bitcast_convert_type(x, jnp.uint32)


def _f32_from_bits(x):
    return lax.bitcast_convert_type(x, jnp.float32)


def _gelu_tanh(x):
    c0 = math.sqrt(2.0 / math.pi)
    t = jnp.tanh(x * (c0 + (c0 * 0.044715) * (x * x)))
    hx = 0.5 * x
    return hx + hx * t


def _token_rows(ref, sublane, n):
    return ref[pl.ds(sublane, n, stride=SUBLANES), :]


def _per_batch(v, ts):
    return jnp.tile(v, (ts // SUBLANES, 1))


def _ada_kernel(c_ref, w_ref, b_ref, o_ref):
    c = c_ref[...]
    ca = _bf16(c * jax.nn.sigmoid(c))
    o_ref[...] = _dot(ca, _bf16(w_ref[...])) + b_ref[...]


def _ada_call(c, w, b):
    n_layers, d, n = w.shape
    bsz = c.shape[0]
    nb = 1024
    return pl.pallas_call(
        _ada_kernel,
        out_shape=jax.ShapeDtypeStruct((n_layers, bsz, n), jnp.float32),
        grid=(n_layers, n // nb),
        in_specs=[
            pl.BlockSpec((bsz, d), lambda l, j: (0, 0)),
            pl.BlockSpec((None, d, nb), lambda l, j: (l, 0, j)),
            pl.BlockSpec((None, 1, nb), lambda l, j: (l, 0, j)),
        ],
        out_specs=pl.BlockSpec((None, bsz, nb), lambda l, j: (l, 0, j)),
        compiler_params=pltpu.CompilerParams(
            dimension_semantics=("arbitrary", "arbitrary"),
            vmem_limit_bytes=VMEM_LIMIT_BYTES),
        name="ada_mod",
    )(c, w, b)


def _rms_mod(x, gain, shift, scale):
    ts = x.shape[0]
    ms = jnp.mean(x * x, axis=-1, keepdims=True)
    return (x * lax.rsqrt(ms + EPS)) * _per_batch(gain * (1.0 + scale), ts) + _per_batch(shift, ts)


def _causal_taps(w_ref, bias_ref, ext_ref, n_taps, row0, n):
    d = ext_ref.shape[1]
    cols = []
    for cb in range(d // LANES):
        sl = slice(cb * LANES, (cb + 1) * LANES)
        acc = jnp.broadcast_to(bias_ref[:, sl], (n, LANES))
        for k in range(n_taps):
            lo = row0 + SUBLANES * k
            acc = acc + w_ref[k:k + 1, sl] * ext_ref[lo:lo + n, sl]
        cols.append(acc)
    return jnp.concatenate(cols, axis=1)


def _causal_taps_unit(w_ref, bias_ref, ext_ref, out_ref, cb, r0, n_taps):
    w_cb, ext_cb = w_ref.at[cb], ext_ref.at[cb]
    acc = jnp.broadcast_to(bias_ref[cb], (CONV_ROWS, LANES))
    for k in range(n_taps):
        win = ext_cb[pl.ds(pl.multiple_of(r0 + SUBLANES * k, SUBLANES), CONV_ROWS), :]
        acc = acc + w_cb[k:k + 1, :] * win
    out_ref.at[cb][pl.ds(r0, CONV_ROWS), :] = acc


def _start_tile_dmas(make_copy, n, inline=False, both_queues=False):
    if inline:
        for r in range(n):
            make_copy(r).start(priority=r % 2 if both_queues else 0)
        return

    def body(g, carry):
        for k in range(DMA_UNROLL):
            make_copy(g * DMA_UNROLL + k).start(priority=k % 2 if both_queues else 0)
        return carry
    lax.fori_loop(0, n // DMA_UNROLL, body, 0)


def _wait_all_tiles(hbm_ref, vmem_tiles_ref, sem, to_hbm):
    hbm_view = hbm_ref.at[pl.ds(0, vmem_tiles_ref.shape[0])]
    if to_hbm:
        pltpu.make_async_copy(vmem_tiles_ref, hbm_view, sem).wait()
    else:
        pltpu.make_async_copy(hbm_view, vmem_tiles_ref, sem).wait()


def _gather_expert_tiles(ys_ref, ybuf, sem, dest_ref, dnext_ref, step, n_steps, ts, inline):
    slot = step % 2

    def tile_copy(dref, sl):
        def make(r):
            src = pl.multiple_of(dref[0, 0, r] * SUBLANES, SUBLANES)
            return pltpu.make_async_copy(
                ys_ref.at[pl.ds(src, SUBLANES)],
                ybuf.at[sl, pl.ds(pl.multiple_of(r * SUBLANES, SUBLANES), SUBLANES)],
                sem.at[sl])
        return make

    @pl.when(step == 0)
    def _():
        _start_tile_dmas(tile_copy(dest_ref, 0), ts)

    if not inline:
        @pl.when(step + 1 < n_steps)
        def _():
            _start_tile_dmas(tile_copy(dnext_ref, 1 - slot), ts)

    _wait_all_tiles(ys_ref, ybuf.at[slot], sem.at[slot], to_hbm=False)
    if inline:
        _start_tile_dmas(tile_copy(dnext_ref, 1 - slot), ts, inline=True)
    yv = ybuf.at[slot]
    y = jnp.concatenate([_token_rows(yv, s, ts) for s in range(SUBLANES)], axis=1)

    def retire_last():
        if inline:
            @pl.when(step + 1 == n_steps)
            def _():
                _wait_all_tiles(ys_ref, ybuf.at[1 - slot], sem.at[1 - slot], to_hbm=False)

    return y, retire_last


def _route(logits_t):
    ts = logits_t.shape[1]
    row = lax.broadcasted_iota(jnp.int32, (SUBLANES, ts), 0)
    gl = jnp.where(row < N_GROUPS, logits_t[0:SUBLANES], _NEG_INF)
    gmax = jnp.max(gl, axis=0, keepdims=True)
    gsel = jnp.min(jnp.where(gl == gmax, row, SUBLANES), axis=0, keepdims=True)
    p_g = 1.0 / jnp.sum(jnp.exp(gl - gmax), axis=0, keepdims=True)
    es = jnp.zeros((SUBLANES, ts), jnp.float32)
    for g in range(N_GROUPS):
        lo = SUBLANES * (1 + g)
        es = jnp.where(gsel == g, logits_t[lo:lo + EXPERTS_PER_GROUP], es)
    m1 = jnp.max(es, axis=0, keepdims=True)
    i1 = jnp.min(jnp.where(es == m1, row, SUBLANES), axis=0, keepdims=True)
    es2 = jnp.where(row == i1, _NEG_INF, es)
    m2 = jnp.max(es2, axis=0, keepdims=True)
    i2 = jnp.min(jnp.where(es2 == m2, row, SUBLANES), axis=0, keepdims=True)
    t = jnp.exp(m2 - m1)
    w1 = p_g / (1.0 + t)
    w2 = w1 * t
    first_lo = i1 < i2
    e_lo = jnp.minimum(i1, i2)
    e_hi = jnp.maximum(i1, i2)
    gate_lo = jnp.where(first_lo, w1, w2)
    gate_hi = jnp.where(first_lo, w2, w1)
    pair = lax.shift_right_logical(e_lo * (2 * EXPERTS_PER_GROUP - 1 - e_lo), 1) + (e_hi - e_lo - 1)
    cls = gsel * PAIRS_PER_GROUP + pair
    return cls, gate_lo, gate_hi


def _post_mixer(x1, mod_ref, n2_ref, wrt_ref, brt_ref, hrow_ref, cls_ref, rank_ref, cnt_ref, sub):
    ts = x1.shape[0]
    hrow_ref = hrow_ref.at[pl.ds(sub * ts * SUBLANES, ts * SUBLANES)]
    cols = slice(sub * ts, (sub + 1) * ts)
    h2 = _rms_mod(x1, n2_ref[...], mod_ref[3], mod_ref[4])
    h2b = _bf16(h2)
    logits_t = lax.dot_general(wrt_ref[...], h2b, (((1,), (1,)), ((), ())),
                               preferred_element_type=jnp.float32) + brt_ref[:, 0:1]
    cls, gate_lo, gate_hi = _route(logits_t)
    gates_t = jnp.concatenate(
        [gate_lo, gate_hi, jnp.zeros((LANES - 2, ts), jnp.float32)], axis=0)
    h2r = h2b.astype(jnp.float32)
    for j in range(PACK_SUBLANES):
        hi = _bits(h2r[:, (2 * j) * LANES:(2 * j + 1) * LANES])
        lo = _bits(h2r[:, (2 * j + 1) * LANES:(2 * j + 2) * LANES])
        hrow_ref[pl.ds(j, ts, stride=SUBLANES), :] = hi | lax.shift_right_logical(lo, jnp.uint32(16))
    hrow_ref[pl.ds(PACK_SUBLANES, ts, stride=SUBLANES), :] = _bits(gates_t.T)
    for j in range(PACK_SUBLANES + 1, SUBLANES):
        hrow_ref[pl.ds(j, ts, stride=SUBLANES), :] = jnp.zeros((ts, LANES), jnp.uint32)
    cls_ref[0, :, cols] = cls
    crow = lax.broadcasted_iota(jnp.int32, (LANES, ts), 0)
    onehot = jnp.where(crow == cls, 1.0, 0.0)
    cnt = jnp.sum(onehot, axis=1, keepdims=True)
    earlier = (lax.broadcasted_iota(jnp.int32, (ts, ts), 0)
               < lax.broadcasted_iota(jnp.int32, (ts, ts), 1))
    prefix = _dot(_bf16(onehot), _bf16(jnp.where(earlier, 1.0, 0.0)))
    rank = jnp.sum(onehot * (prefix + cnt_ref[:, 0:1]), axis=0, keepdims=True)
    rank_ref[0, :, cols] = rank.astype(jnp.int32)
    cnt_ref[...] += jnp.broadcast_to(cnt, cnt_ref.shape)


def _mixer_out(t, d, ts):
    n_tiles = t // ts
    out_shape = (
        jax.ShapeDtypeStruct((t, d), jnp.float32),
        jax.ShapeDtypeStruct((t * SUBLANES, LANES), jnp.uint32),
        jax.ShapeDtypeStruct((n_tiles, 1, ts), jnp.int32),
        jax.ShapeDtypeStruct((n_tiles, 1, ts), jnp.int32),
        jax.ShapeDtypeStruct((LANES, LANES), jnp.float32),
    )
    out_specs = (
        pl.BlockSpec((ts, d), lambda i: (i, 0)),
        pl.BlockSpec((ts * SUBLANES, LANES), lambda i: (i, 0)),
        pl.BlockSpec((1, 1, ts), lambda i: (i, 0, 0)),
        pl.BlockSpec((1, 1, ts), lambda i: (i, 0, 0)),
        pl.BlockSpec((LANES, LANES), lambda i: (0, 0)),
    )
    return out_shape, out_specs


def _full(shape):
    return pl.BlockSpec(shape, lambda i: tuple(0 for _ in shape))


def _expert_weight_cast(weights, layer, n_steps):
    flat = [w.reshape(w.shape[0], w.shape[1] * w.shape[2], w.shape[3]) for w in weights]
    if any(w.shape[1] % (n_steps * BF16_SUBLANES) for w in flat):
        return [], [], (), ()
    in_specs = [pl.BlockSpec((None, w.shape[1] // n_steps, w.shape[2]), lambda i: (layer, i, 0))
                for w in flat]
    out_shapes = tuple(jax.ShapeDtypeStruct(w.shape[1:], jnp.bfloat16) for w in flat)
    out_specs = tuple(pl.BlockSpec((w.shape[1] // n_steps, w.shape[2]), lambda i: (i, 0))
                      for w in flat)
    return flat, in_specs, out_shapes, out_specs


def _cast_slices(cast_in, cast_out):
    for src, dst in zip(cast_in, cast_out):
        dst[...] = _bf16(src[...])


def _cast_results(outs, weights, layer):
    if outs:
        return [o.reshape(w.shape[1:]) for o, w in zip(outs, weights)]
    return [_bf16(w[layer]) for w in weights]


def _to_tile_order(x_ref, perm, xs):
    bsz, tm, d = x_ref.shape
    for cb in range(d // LANES):
        for p in range(bsz):
            perm[cb, PERM_PITCH * p:PERM_PITCH * p + tm, :] = x_ref[p, :, cb * LANES:(cb + 1) * LANES]
    for m in range(tm):
        for cb in range(d // LANES):
            xs[m * bsz:(m + 1) * bsz, cb * LANES:(cb + 1) * LANES] = (
                perm.at[cb][pl.ds(m, bsz, stride=PERM_PITCH), :])


def _conf_kernel(n_cast, x_ref, mod_ref, n1_ref, n2_ref, w1_ref, b1_ref, dw_ref, dwb_ref,
                 lng_ref, lnb_ref, w2_ref, b2_ref, wrt_ref, brt_ref, *rest):
    cast_in, rest = rest[:n_cast], rest[n_cast:]
    x1_ref, hrow_ref, cls_ref, rank_ref, cnt_ref, zeros_ref = rest[:6]
    cast_out, (perm, xs, vext, conv) = rest[6:6 + n_cast], rest[6 + n_cast:]
    _cast_slices(cast_in, cast_out)
    i = pl.program_id(0)
    bsz, tm, d = x_ref.shape
    ts = bsz * tm
    n_chunks = d // LANES
    hist = SUBLANES * (CONF_KERNEL - 1)

    @pl.when(i == 0)
    def _():
        vext[:, 0:hist, :] = jnp.zeros((n_chunks, hist, LANES), jnp.float32)
        cnt_ref[...] = jnp.zeros_like(cnt_ref)

    zeros_ref[...] = jnp.zeros_like(zeros_ref)

    _to_tile_order(x_ref, perm, xs)
    for sub in range(ts // SUB_TILE):
        rows = slice(sub * SUB_TILE, (sub + 1) * SUB_TILE)
        h = _bf16(_rms_mod(xs[rows, :], n1_ref[...], mod_ref[0], mod_ref[1]))
        u = _dot(h, w1_ref[...]) + b1_ref[...]
        v = u[:, 0:d] * jax.nn.sigmoid(u[:, d:2 * d])
        for cb in range(n_chunks):
            vext[cb, hist + sub * SUB_TILE:hist + (sub + 1) * SUB_TILE, :] = v[:, cb * LANES:(cb + 1) * LANES]

    units_per_chunk = ts // CONV_ROWS

    def conv_unit(idx, carry):
        r0 = pl.multiple_of((idx % units_per_chunk) * CONV_ROWS, CONV_ROWS)
        _causal_taps_unit(dw_ref, dwb_ref, vext, conv, idx // units_per_chunk, r0, CONF_KERNEL)
        return carry

    lax.fori_loop(0, n_chunks * units_per_chunk, conv_unit, 0, unroll=32)
    vext[:, 0:hist, :] = vext[:, ts:ts + hist, :]

    for sub in range(ts // SUB_TILE):
        rows = slice(sub * SUB_TILE, (sub + 1) * SUB_TILE)
        y = jnp.concatenate([conv[cb, rows, :] for cb in range(n_chunks)], axis=1)
        mu = jnp.mean(y, axis=-1, keepdims=True)
        yc = y - mu
        var = jnp.mean(yc * yc, axis=-1, keepdims=True)
        z = yc * lax.rsqrt(var + EPS) * lng_ref[...] + lnb_ref[...]
        z = z * jax.nn.sigmoid(z)
        mix = _dot(_bf16(z), w2_ref[...]) + b2_ref[...]
        x1 = xs[rows, :] + _per_batch(mod_ref[2], SUB_TILE) * mix
        x1_ref[rows, :] = x1
        _post_mixer(x1, mod_ref, n2_ref, wrt_ref, brt_ref, hrow_ref, cls_ref, rank_ref, cnt_ref, sub)


def _conf_call(x, mod, n1, n2, w1, b1, dw, dwb, lng, lnb, w2, b2, wrt, brt, zero_rows,
               expert_weights, layer):
    bsz, seq, d = x.shape
    tm = TIME_TILE
    ts = bsz * tm
    n_steps = seq // tm
    assert zero_rows % (n_steps * SUBLANES) == 0
    cast, cast_specs, cast_shapes, cast_out_specs = _expert_weight_cast(expert_weights, layer, n_steps)
    out_shape, out_specs = _mixer_out(bsz * seq, d, ts)
    out_shape += (jax.ShapeDtypeStruct((zero_rows, LANES), jnp.uint32),) + cast_shapes
    out_specs += (pl.BlockSpec((zero_rows // n_steps, LANES), lambda i: (i, 0)),) + cast_out_specs
    hist = SUBLANES * (CONF_KERNEL - 1)
    outs = pl.pallas_call(
        functools.partial(_conf_kernel, len(cast)),
        out_shape=out_shape,
        grid=(n_steps,),
        in_specs=[
            pl.BlockSpec((bsz, tm, d), lambda i: (0, i, 0)),
            _full((MOD_ROWS, bsz, d)),
            _full((1, d)), _full((1, d)),
            _full((d, 2 * d)), _full((1, 2 * d)),
            _full((d // LANES, CONF_KERNEL + 1, LANES)), _full((d // LANES, 1, LANES)),
            _full((1, d)), _full((1, d)),
            _full((d, d)), _full((1, d)),
            _full((ROUTER_ROWS, d)), _full((ROUTER_ROWS, LANES)),
        ] + cast_specs,
        out_specs=out_specs,
        scratch_shapes=[
            pltpu.VMEM((d // LANES, bsz * PERM_PITCH, LANES), jnp.float32),
            pltpu.VMEM((ts, d), jnp.float32),
            pltpu.VMEM((d // LANES, hist + ts, LANES), jnp.float32),
            pltpu.VMEM((d // LANES, ts, LANES), jnp.float32),
        ],
        compiler_params=pltpu.CompilerParams(
            dimension_semantics=("arbitrary",), vmem_limit_bytes=VMEM_LIMIT_BYTES),
        name="conf_mixer",
    )(x, mod, n1, n2, w1, b1, dw, dwb, lng, lnb, w2, b2, wrt, brt, *cast)
    return outs[:6], _cast_results(outs[6:], expert_weights, layer)


def _lru_kernel(n_tiles, n_cast, dest_ref, dnext_ref, xprev_ref, pmod_ref, mod_ref, n1_ref, n2_ref,
                wy_ref, by_ref, wx_ref, bx_ref,
                cw_ref, cb_ref, wa_ref, ba_ref, wi_ref, bi_ref, lam_ref, wo_ref, bo_ref,
                wrt_ref, brt_ref, ys_ref, *rest):
    cast_in, rest = rest[:n_cast], rest[n_cast:]
    x1_ref, hrow_ref, cls_ref, rank_ref, cnt_ref = rest[:5]
    cast_out, (uext, hcar, ybuf, ysem) = rest[5:5 + n_cast], rest[5 + n_cast:]
    _cast_slices(cast_in, cast_out)
    i = pl.program_id(0)
    ts, d = xprev_ref.shape
    hd = d // LRU_HEADS
    hist = SUBLANES * (LRU_CONV - 1)

    @pl.when(i == 0)
    def _():
        uext[0:hist, :] = jnp.zeros((hist, d), jnp.float32)
        hcar[...] = jnp.zeros_like(hcar)
        cnt_ref[...] = jnp.zeros_like(cnt_ref)

    y_prev, retire_last = _gather_expert_tiles(ys_ref, ybuf, ysem, dest_ref, dnext_ref, i, n_tiles, ts,
                                               inline=True)
    nl = -lam_ref[...]
    decay = LRU_C * (jnp.maximum(nl, 0.0) + jnp.log1p(jnp.exp(-jnp.abs(nl))))
    hstate = hcar[...]
    for sub in range(ts // SUB_TILE):
        rows = slice(sub * SUB_TILE, (sub + 1) * SUB_TILE)
        x = xprev_ref[rows, :] + _per_batch(pmod_ref[5], SUB_TILE) * y_prev[rows, :]
        h = _bf16(_rms_mod(x, n1_ref[...], mod_ref[0], mod_ref[1]))
        ygate = _gelu_tanh(_dot(h, wy_ref[...]) + by_ref[...])
        xb = _dot(h, wx_ref[...]) + bx_ref[...]
        uext[hist + sub * SUB_TILE:hist + (sub + 1) * SUB_TILE, :] = xb
        u = _causal_taps(cw_ref, cb_ref, uext, LRU_CONV, sub * SUB_TILE, SUB_TILE)

        ub = _bf16(u)
        r_parts, i_parts = [], []
        for hh in range(LRU_HEADS):
            uh = ub[:, hh * hd:(hh + 1) * hd]
            r_parts.append(_dot(uh, wa_ref[hh]))
            i_parts.append(_dot(uh, wi_ref[hh]))
        r = jax.nn.sigmoid(jnp.concatenate(r_parts, axis=1) + ba_ref[...])
        ig = jax.nn.sigmoid(jnp.concatenate(i_parts, axis=1) + bi_ref[...])

        neg_log_a = r * decay
        a = jnp.exp(-neg_log_a)
        mult = jnp.sqrt(jnp.tanh(neg_log_a) * (1.0 + a * a))
        b = mult * (ig * u)

        outs = []
        for m in range(SUB_TILE // SUBLANES):
            sl = slice(m * SUBLANES, (m + 1) * SUBLANES)
            hstate = a[sl] * hstate + b[sl]
            outs.append(hstate)
        hs = jnp.concatenate(outs, axis=0)

        mix = _dot(_bf16(hs * ygate), wo_ref[...]) + bo_ref[...]
        x1 = x + _per_batch(mod_ref[2], SUB_TILE) * mix
        x1_ref[rows, :] = x1
        _post_mixer(x1, mod_ref, n2_ref, wrt_ref, brt_ref, hrow_ref, cls_ref, rank_ref, cnt_ref, sub)
    hcar[...] = hstate
    uext[0:hist, :] = uext[ts:ts + hist, :]
    retire_last()


def _lru_call(dest, x_prev, mod_prev, ys, mod, n1, n2, wy, by, wx, bx, cw, cb, wa, ba, wi, bi, lam,
              wo, bo, wrt, brt, expert_weights, layer):
    t, d = x_prev.shape
    n_tiles, _, ts = dest.shape
    bsz = mod.shape[1]
    hd = d // LRU_HEADS
    hist = SUBLANES * (LRU_CONV - 1)
    cast, cast_specs, cast_shapes, cast_out_specs = _expert_weight_cast(expert_weights, layer, n_tiles)
    out_shape, out_specs = _mixer_out(t, d, ts)
    out_shape += cast_shapes
    out_specs += cast_out_specs
    outs = pl.pallas_call(
        functools.partial(_lru_kernel, n_tiles, len(cast)),
        out_shape=out_shape,
        grid=(n_tiles,),
        in_specs=[
            pl.BlockSpec((1, 1, ts), lambda i: (i, 0, 0), memory_space=pltpu.SMEM),
            pl.BlockSpec((1, 1, ts), lambda i: (jnp.minimum(i + 1, n_tiles - 1), 0, 0),
                         memory_space=pltpu.SMEM),
            pl.BlockSpec((ts, d), lambda i: (i, 0)),
            _full((MOD_ROWS, bsz, d)), _full((MOD_ROWS, bsz, d)),
            _full((1, d)), _full((1, d)),
            _full((d, d)), _full((1, d)),
            _full((d, d)), _full((1, d)),
            _full((LRU_CONV, d)), _full((1, d)),
            _full((LRU_HEADS, hd, hd)), _full((1, d)),
            _full((LRU_HEADS, hd, hd)), _full((1, d)),
            _full((1, d)),
            _full((d, d)), _full((1, d)),
            _full((ROUTER_ROWS, d)), _full((ROUTER_ROWS, LANES)),
            pl.BlockSpec(memory_space=pl.ANY),
        ] + cast_specs,
        out_specs=out_specs,
        scratch_shapes=[
            pltpu.VMEM((hist + ts, d), jnp.float32),
            pltpu.VMEM((SUBLANES, d), jnp.float32),
            pltpu.VMEM((2, ts * SUBLANES, LANES), jnp.float32),
            pltpu.SemaphoreType.DMA((2,)),
        ],
        compiler_params=pltpu.CompilerParams(
            dimension_semantics=("arbitrary",), vmem_limit_bytes=VMEM_LIMIT_BYTES),
        name="lru_mixer",
    )(dest, dest, x_prev, mod_prev, mod, n1, n2, wy, by, wx, bx, cw, cb, wa, ba, wi, bi, lam, wo, bo,
      wrt, brt, ys, *cast)
    return outs[:5], _cast_results(outs[5:], expert_weights, layer)


def _plan_kernel(n_blocks, cnt_ref, clo_ref, chi_ref, cls_ref, rank_ref,
                 dest_ref, blo_ref, bhi_ref, bvalid_ref, nblk_ref, class_start):
    n_tiles, _, ts = cls_ref.shape
    class_start[...] = jnp.zeros_like(class_start)

    def per_class(c, nb_total):
        class_start[pl.ds(c, 1), :] = jnp.full((1, LANES), nb_total * ROW_BLOCK, jnp.int32)
        nb = (cnt_ref[c] + (ROW_BLOCK - 1)) // ROW_BLOCK

        def fill(k, carry):
            blo_ref[nb_total + k] = clo_ref[c]
            bhi_ref[nb_total + k] = chi_ref[c]
            bvalid_ref[nb_total + k] = jnp.minimum(cnt_ref[c] - k * ROW_BLOCK, ROW_BLOCK)
            return carry

        lax.fori_loop(0, nb, fill, 0)
        return nb_total + nb

    nb_used = lax.fori_loop(0, N_CLASSES, per_class, 0)
    nblk_ref[0] = nb_used
    last = jnp.maximum(nb_used - 1, 0)

    def tail(k, carry):
        blo_ref[k] = blo_ref[last]
        bhi_ref[k] = bhi_ref[last]
        bvalid_ref[k] = 0
        return carry

    lax.fori_loop(nb_used, n_blocks, tail, 0)

    starts = class_start[:, 0:1]
    class_id = lax.broadcasted_iota(jnp.int32, (LANES, ts), 0)

    def per_tile(t, carry):
        first = jnp.sum(jnp.where(class_id == cls_ref[t], starts, 0), axis=0, keepdims=True)
        dest_ref[t] = first + rank_ref[t]
        return carry

    lax.fori_loop(0, n_tiles, per_tile, 0)


def _plan_call(cls, rank, counts, n_blocks):
    n_tiles, _, ts = cls.shape
    smem_full = pl.BlockSpec(memory_space=pltpu.SMEM)
    vmem_full = pl.BlockSpec(memory_space=pltpu.VMEM)
    return pl.pallas_call(
        functools.partial(_plan_kernel, n_blocks),
        out_shape=(
            jax.ShapeDtypeStruct((n_tiles, 1, ts), jnp.int32),
            jax.ShapeDtypeStruct((n_blocks,), jnp.int32),
            jax.ShapeDtypeStruct((n_blocks,), jnp.int32),
            jax.ShapeDtypeStruct((n_blocks,), jnp.int32),
            jax.ShapeDtypeStruct((1,), jnp.int32),
        ),
        in_specs=[smem_full, smem_full, smem_full, vmem_full, vmem_full],
        out_specs=(vmem_full, smem_full, smem_full, smem_full, smem_full),
        scratch_shapes=[pltpu.VMEM((LANES, LANES), jnp.int32)],
        compiler_params=pltpu.CompilerParams(vmem_limit_bytes=VMEM_LIMIT_BYTES),
        name="moe_plan",
    )(counts, jnp.asarray(_CLASS_LO), jnp.asarray(_CLASS_HI), cls, rank)


def _dispatch_kernel(n_tiles, dest_ref, hrow_ref, xs_in_ref, xs_ref, stage, sem_in, sem_out):
    del xs_in_ref
    i = pl.program_id(0)
    ts = dest_ref.shape[2]
    rows = ts * SUBLANES

    def load(tile, slot):
        return pltpu.make_async_copy(hrow_ref.at[pl.ds(pl.multiple_of(tile * rows, rows), rows)],
                                     stage.at[slot], sem_in.at[slot])

    def tile_copy(slot, r, dst_row):
        return pltpu.make_async_copy(
            stage.at[slot, pl.ds(pl.multiple_of(r * SUBLANES, SUBLANES), SUBLANES)],
            xs_ref.at[pl.ds(pl.multiple_of(dst_row * SUBLANES, SUBLANES), SUBLANES)],
            sem_out.at[slot])

    def drain(slot):
        _wait_all_tiles(xs_ref, stage.at[slot], sem_out.at[slot], to_hbm=True)

    slot = i % DISPATCH_SLOTS
    nxt = (i + 1) % DISPATCH_SLOTS

    @pl.when(i == 0)
    def _():
        load(0, 0).start()

    @pl.when(i >= 2)
    def _():
        drain(nxt)

    @pl.when(i + 1 < n_tiles)
    def _():
        load(i + 1, nxt).start()

    load(i, slot).wait()

    _start_tile_dmas(lambda r: tile_copy(slot, r, dest_ref[0, 0, r]), ts, inline=True, both_queues=True)

    @pl.when(i == n_tiles - 1)
    def _():
        @pl.when(i >= 1)
        def _():
            drain((i + DISPATCH_SLOTS - 1) % DISPATCH_SLOTS)
        drain(slot)


def _dispatch_call(dest, hrows, xs_init):
    n_tiles, _, ts = dest.shape
    assert xs_init.dtype == jnp.uint32 and xs_init.shape[1] == LANES
    any_spec = pl.BlockSpec(memory_space=pl.ANY)
    return pl.pallas_call(
        functools.partial(_dispatch_kernel, n_tiles),
        out_shape=jax.ShapeDtypeStruct(xs_init.shape, jnp.uint32),
        grid=(n_tiles,),
        in_specs=[
            pl.BlockSpec((1, 1, ts), lambda i: (i, 0, 0), memory_space=pltpu.SMEM),
            any_spec, any_spec,
        ],
        out_specs=any_spec,
        scratch_shapes=[
            pltpu.VMEM((DISPATCH_SLOTS, ts * SUBLANES, LANES), jnp.uint32),
            pltpu.SemaphoreType.DMA((DISPATCH_SLOTS,)),
            pltpu.SemaphoreType.DMA((DISPATCH_SLOTS,)),
        ],
        input_output_aliases={2: 0},
        compiler_params=pltpu.CompilerParams(
            dimension_semantics=("arbitrary",), vmem_limit_bytes=VMEM_LIMIT_BYTES),
        name="moe_dispatch",
    )(dest, hrows, xs_init)


def _expert_kernel(blo_ref, bhi_ref, bvalid_ref, nblk_ref, xs_ref,
                   wg_lo, wu_lo, wd_lo, wg_hi, wu_hi, wd_hi, ys_ref):
    del blo_ref, bhi_ref, nblk_ref
    valid = bvalid_ref[pl.program_id(0)]

    def experts_on(m):
        pieces = []
        for j in range(PACK_SUBLANES):
            w = _token_rows(xs_ref, j, m)
            pieces.append(_bf16(_f32_from_bits(w & jnp.uint32(0xFFFF0000))))
            pieces.append(_bf16(_f32_from_bits(lax.shift_left(w, jnp.uint32(16)))))
        xb = jnp.concatenate(pieces, axis=1)
        gates = _f32_from_bits(_token_rows(xs_ref, PACK_SUBLANES, m))
        acc = None
        for idx, (wg, wu, wd) in enumerate(((wg_lo, wu_lo, wd_lo), (wg_hi, wu_hi, wd_hi))):
            a = _dot(xb, wg[...])
            hid = (a * jax.nn.sigmoid(a)) * _dot(xb, wu[...]) * gates[:, idx:idx + 1]
            part = _dot(_bf16(hid), wd[...])
            acc = part if acc is None else acc + part
        for s in range(SUBLANES):
            ys_ref[pl.ds(s, m, stride=SUBLANES), :] = acc[:, s * LANES:(s + 1) * LANES]

    for m in range(ROW_STEP, ROW_BLOCK + 1, ROW_STEP):
        pl.when((valid > m - ROW_STEP) & (valid <= m))(functools.partial(experts_on, m))


def _expert_call(blo, bhi, bvalid, nblk, xs, wg, wu, wd):
    n_rows = xs.shape[0] // SUBLANES
    d, de = wg.shape[1], wg.shape[2]
    assert d == SUBLANES * LANES
    n_blocks = n_rows // ROW_BLOCK

    def w_spec(shape, which):
        if which == 0:
            return pl.BlockSpec((None,) + shape, lambda b, lo, hi, bv, n: (lo[b], 0, 0))
        return pl.BlockSpec((None,) + shape, lambda b, lo, hi, bv, n: (hi[b], 0, 0))

    def row_map(b, lo, hi, bv, n):
        return (jnp.minimum(b, jnp.maximum(n[0] - 1, 0)), 0)

    grid_spec = pltpu.PrefetchScalarGridSpec(
        num_scalar_prefetch=4,
        grid=(n_blocks,),
        in_specs=[
            pl.BlockSpec((ROW_BLOCK * SUBLANES, LANES), row_map),
            w_spec((d, de), 0), w_spec((d, de), 0), w_spec((de, d), 0),
            w_spec((d, de), 1), w_spec((d, de), 1), w_spec((de, d), 1),
        ],
        out_specs=pl.BlockSpec((ROW_BLOCK * SUBLANES, LANES), row_map),
    )
    return pl.pallas_call(
        _expert_kernel,
        out_shape=jax.ShapeDtypeStruct((n_rows * SUBLANES, LANES), jnp.float32),
        grid_spec=grid_spec,
        compiler_params=pltpu.CompilerParams(
            dimension_semantics=("arbitrary",), vmem_limit_bytes=VMEM_LIMIT_BYTES),
        name="moe_experts",
    )(blo, bhi, bvalid, nblk, xs, wg, wu, wd, wg, wu, wd)


def _final_kernel(n_tiles, dest_ref, dnext_ref, x1_ref, mod_ref, fmod_ref, fg_ref, ys_ref,
                  out_ref, ybuf, sem, unperm):
    i = pl.program_id(0)
    ts, d = x1_ref.shape
    bsz, tm, _ = out_ref.shape
    y, retire_last = _gather_expert_tiles(ys_ref, ybuf, sem, dest_ref, dnext_ref, i, n_tiles, ts,
                                          inline=False)
    x2 = x1_ref[...] + _per_batch(mod_ref[5], ts) * y
    res = _rms_mod(x2, fg_ref[...], fmod_ref[0], fmod_ref[1])
    for cb in range(d // LANES):
        unperm[cb] = res[:, cb * LANES:(cb + 1) * LANES]
    for cb in range(d // LANES):
        for p in range(bsz):
            out_ref[p, :, cb * LANES:(cb + 1) * LANES] = unperm.at[cb][pl.ds(p, tm, stride=SUBLANES), :]
    retire_last()


def _final_call(dest, x1, mod, ys, bsz, fmod, fin_g):
    t, d = x1.shape
    n_tiles, _, ts = dest.shape
    tm = ts // bsz
    return pl.pallas_call(
        functools.partial(_final_kernel, n_tiles),
        out_shape=jax.ShapeDtypeStruct((bsz, t // bsz, d), jnp.float32),
        grid=(n_tiles,),
        in_specs=[
            pl.BlockSpec((1, 1, ts), lambda i: (i, 0, 0), memory_space=pltpu.SMEM),
            pl.BlockSpec((1, 1, ts), lambda i: (jnp.minimum(i + 1, n_tiles - 1), 0, 0),
                         memory_space=pltpu.SMEM),
            pl.BlockSpec((ts, d), lambda i: (i, 0)),
            _full((MOD_ROWS, bsz, d)), _full((MOD_ROWS, bsz, d)), _full((1, d)),
            pl.BlockSpec(memory_space=pl.ANY),
        ],
        out_specs=pl.BlockSpec((bsz, tm, d), lambda i: (0, i, 0)),
        scratch_shapes=[pltpu.VMEM((2, ts * SUBLANES, LANES), jnp.float32),
                        pltpu.SemaphoreType.DMA((2,)),
                        pltpu.VMEM((d // LANES, ts, LANES), jnp.float32)],
        compiler_params=pltpu.CompilerParams(
            dimension_semantics=("arbitrary",), vmem_limit_bytes=VMEM_LIMIT_BYTES),
        name="moe_combine_final",
    )(dest, dest, x1, mod, fmod, fin_g, ys)


def _router_operands(wg, bg, wr, br):
    d = wg.shape[0]
    pad_g = jnp.zeros((SUBLANES - N_GROUPS, d), jnp.float32)
    pad_e = jnp.zeros((ROUTER_ROWS - SUBLANES - N_EXPERTS, d), jnp.float32)
    wrt = jnp.concatenate([wg.T, pad_g, wr.T, pad_e], axis=0)
    bcol = jnp.concatenate([bg, jnp.zeros((SUBLANES - N_GROUPS,), jnp.float32), br,
                            jnp.zeros((ROUTER_ROWS - SUBLANES - N_EXPERTS,), jnp.float32)])
    brt = jnp.broadcast_to(bcol[:, None], (ROUTER_ROWS, LANES))
    return _bf16(wrt), brt


def _mod_table(m, parts):
    bsz = m.shape[0]
    d = m.shape[1] // parts
    m = jnp.transpose(m.reshape(bsz, parts, d), (1, 0, 2))
    return jnp.concatenate([m, jnp.zeros((MOD_ROWS - parts, bsz, d), jnp.float32)], axis=0)


def _row(v):
    return v.reshape(1, -1)


def _moe_experts(hrows, cls, rank, cnt, weights_bf16, n_blocks, xs_init):
    counts = cnt[:, 0].astype(jnp.int32)
    dest, blo, bhi, bvalid, nblk = _plan_call(cls, rank, counts, n_blocks)
    xs = _dispatch_call(dest, hrows, xs_init)
    ys = _expert_call(blo, bhi, bvalid, nblk, xs, *weights_bf16)
    return xs, ys, dest


def kernel(x, c, ada_w, ada_b, norm1_g, norm2_g, cm_w1, cm_b1, cm_dw, cm_dwb, cm_ln_g, cm_ln_b, cm_w2, cm_b2, rg_wy, rg_by, rg_wx, rg_bx, rg_cw, rg_cb, rg_wa, rg_ba, rg_wi, rg_bi, rg_lambda, rg_wo, rg_bo, moe_wg, moe_bg, moe_wr, moe_br, moe_w_gate, moe_w_up, moe_w_down, fin_ada_w, fin_ada_b, fin_g):
    bsz, seq, d = x.shape
    t = bsz * seq
    depth = ada_w.shape[0]
    assert depth == 2 and bsz == SUBLANES and seq % TIME_TILE == 0
    assert d == 2 * PACK_SUBLANES * LANES

    mods = _ada_call(c, ada_w, ada_b[:, None, :])
    fmods = _ada_call(c, fin_ada_w[None], fin_ada_b[None, None, :])
    fmod = _mod_table(fmods[0], 2)
    n_steps = seq // TIME_TILE
    granule = n_steps // math.gcd(n_steps, ROW_BLOCK)
    n_blocks = -(-(-(-t // ROW_BLOCK) + N_CLASSES) // granule) * granule
    mod0, mod1 = _mod_table(mods[0], 6), _mod_table(mods[1], 6)

    wrt, brt = _router_operands(moe_wg[0], moe_bg[0], moe_wr[0], moe_br[0])
    dw = jnp.concatenate([cm_dw[0], jnp.zeros((1, d), jnp.float32)], axis=0)
    dw = jnp.transpose(dw.reshape(CONF_KERNEL + 1, d // LANES, LANES), (1, 0, 2))
    expert_weights = (moe_w_gate, moe_w_up, moe_w_down)
    (x1, hrows, cls, rank, cnt, xs_init), weights0 = _conf_call(
        x, mod0, _row(norm1_g[0]), _row(norm2_g[0]), _bf16(cm_w1[0]), _row(cm_b1[0]),
        dw, cm_dwb[0].reshape(d // LANES, 1, LANES), _row(cm_ln_g[0]), _row(cm_ln_b[0]),
        _bf16(cm_w2[0]), _row(cm_b2[0]), wrt, brt, n_blocks * ROW_BLOCK * SUBLANES,
        expert_weights, 0)
    xs, ys, dest = _moe_experts(hrows, cls, rank, cnt, weights0, n_blocks, xs_init)

    wrt, brt = _router_operands(moe_wg[1], moe_bg[1], moe_wr[1], moe_br[1])
    (x1, hrows, cls, rank, cnt), weights1 = _lru_call(
        dest, x1, mod0, ys, mod1, _row(norm1_g[1]), _row(norm2_g[1]),
        _bf16(rg_wy[0]), _row(rg_by[0]), _bf16(rg_wx[0]), _row(rg_bx[0]),
        rg_cw[0], _row(rg_cb[0]), _bf16(rg_wa[0]), _row(rg_ba[0]), _bf16(rg_wi[0]), _row(rg_bi[0]),
        _row(rg_lambda[0]), _bf16(rg_wo[0]), _row(rg_bo[0]), wrt, brt, expert_weights, 1)
    _, ys, dest = _moe_experts(hrows, cls, rank, cnt, weights1, n_blocks, xs)
    return _final_call(dest, x1, mod1, ys, bsz, fmod, _row(fin_g))
```

```python
import functools
import math

import jax
import jax.numpy as jnp
import numpy as np
from jax import lax
from jax.experimental import pallas as pl
from jax.experimental.pallas import tpu as pltpu

EPS = 1e-6
CONF_KERNEL = 31
LRU_HEADS = 4
LRU_CONV = 4
LRU_C = 8.0
N_GROUPS = 4
EXPERTS_PER_GROUP = 8
N_EXPERTS = N_GROUPS * EXPERTS_PER_GROUP
PAIRS_PER_GROUP = EXPERTS_PER_GROUP * (EXPERTS_PER_GROUP - 1) // 2
N_CLASSES = N_GROUPS * PAIRS_PER_GROUP

LANES = 128
SUBLANES = 8
BF16_SUBLANES = 16
VMEM_LIMIT_BYTES = 56 * 1024 * 1024

TOKEN_TILE = 512
TIME_TILE = TOKEN_TILE // SUBLANES
SUB_TILE = 256
ROW_BLOCK = 512
ROW_STEP = 64
PERM_PITCH = TIME_TILE + SUBLANES
CONV_ROWS = 64
ROUTER_ROWS = 64
PACK_SUBLANES = 4
MOD_ROWS = 8
DISPATCH_SLOTS = 3
DMA_UNROLL = 32

_NEG_INF = float("-inf")


def _class_tables():
    lo = np.zeros((N_CLASSES,), np.int32)
    hi = np.zeros((N_CLASSES,), np.int32)
    for g in range(N_GROUPS):
        for a in range(EXPERTS_PER_GROUP):
            for b in range(a + 1, EXPERTS_PER_GROUP):
                c = g * PAIRS_PER_GROUP + (a * (2 * EXPERTS_PER_GROUP - 1 - a)) // 2 + (b - a - 1)
                lo[c] = g * EXPERTS_PER_GROUP + a
                hi[c] = g * EXPERTS_PER_GROUP + b
    return lo, hi


_CLASS_LO, _CLASS_HI = _class_tables()


def _bf16(x):
    return x.astype(jnp.bfloat16)


def _dot(a, b):
    return jnp.dot(a, b, preferred_element_type=jnp.float32)


def _bits(x):
    return lax.bitcast_convert_type(x, jnp.uint32)


def _f32_from_bits(x):
    return lax.bitcast_convert_type(x, jnp.float32)


def _gelu_tanh(x):
    c0 = math.sqrt(2.0 / math.pi)
    t = jnp.tanh(x * (c0 + (c0 * 0.044715) * (x * x)))
    hx = 0.5 * x
    return hx + hx * t


def _token_rows(ref, sublane, n):
    return ref[pl.ds(sublane, n, stride=SUBLANES), :]


def _per_batch(v, ts):
    return jnp.tile(v, (ts // SUBLANES, 1))


def _ada_kernel(c_ref, w_ref, b_ref, o_ref):
    c = c_ref[...]
    ca = _bf16(c * jax.nn.sigmoid(c))
    o_ref[...] = _dot(ca, _bf16(w_ref[...])) + b_ref[...]


def _ada_call(c, w, b):
    n_layers, d, n = w.shape
    bsz = c.shape[0]
    nb = 1024
    return pl.pallas_call(
        _ada_kernel,
        out_shape=jax.ShapeDtypeStruct((n_layers, bsz, n), jnp.float32),
        grid=(n_layers, n // nb),
        in_specs=[
            pl.BlockSpec((bsz, d), lambda l, j: (0, 0)),
            pl.BlockSpec((None, d, nb), lambda l, j: (l, 0, j)),
            pl.BlockSpec((None, 1, nb), lambda l, j: (l, 0, j)),
        ],
        out_specs=pl.BlockSpec((None, bsz, nb), lambda l, j: (l, 0, j)),
        compiler_params=pltpu.CompilerParams(
            dimension_semantics=("arbitrary", "arbitrary"),
            vmem_limit_bytes=VMEM_LIMIT_BYTES),
        name="ada_mod",
    )(c, w, b)


def _rms_mod(x, gain, shift, scale):
    ts = x.shape[0]
    ms = jnp.mean(x * x, axis=-1, keepdims=True)
    return (x * lax.rsqrt(ms + EPS)) * _per_batch(gain * (1.0 + scale), ts) + _per_batch(shift, ts)


def _causal_taps(w_ref, bias_ref, ext_ref, n_taps, row0, n):
    d = ext_ref.shape[1]
    cols = []
    for cb in range(d // LANES):
        sl = slice(cb * LANES, (cb + 1) * LANES)
        acc = jnp.broadcast_to(bias_ref[:, sl], (n, LANES))
        for k in range(n_taps):
            lo = row0 + SUBLANES * k
            acc = acc + w_ref[k:k + 1, sl] * ext_ref[lo:lo + n, sl]
        cols.append(acc)
    return jnp.concatenate(cols, axis=1)


def _causal_taps_unit(w_ref, bias_ref, ext_ref, out_ref, cb, r0, n_taps):
    w_cb, ext_cb = w_ref.at[cb], ext_ref.at[cb]
    acc = jnp.broadcast_to(bias_ref[cb], (CONV_ROWS, LANES))
    for k in range(n_taps):
        win = ext_cb[pl.ds(pl.multiple_of(r0 + SUBLANES * k, SUBLANES), CONV_ROWS), :]
        acc = acc + w_cb[k:k + 1, :] * win
    out_ref.at[cb][pl.ds(r0, CONV_ROWS), :] = acc


def _start_tile_dmas(make_copy, n, inline=False, both_queues=False):
    if inline:
        for r in range(n):
            make_copy(r).start(priority=r % 2 if both_queues else 0)
        return

    def body(g, carry):
        for k in range(DMA_UNROLL):
            make_copy(g * DMA_UNROLL + k).start(priority=k % 2 if both_queues else 0)
        return carry
    lax.fori_loop(0, n // DMA_UNROLL, body, 0)


def _wait_all_tiles(hbm_ref, vmem_tiles_ref, sem, to_hbm):
    hbm_view = hbm_ref.at[pl.ds(0, vmem_tiles_ref.shape[0])]
    if to_hbm:
        pltpu.make_async_copy(vmem_tiles_ref, hbm_view, sem).wait()
    else:
        pltpu.make_async_copy(hbm_view, vmem_tiles_ref, sem).wait()


def _gather_expert_tiles(ys_ref, ybuf, sem, dest_ref, dnext_ref, step, n_steps, ts, inline):
    slot = step % 2

    def tile_copy(dref, sl):
        def make(r):
            src = pl.multiple_of(dref[0, 0, r] * SUBLANES, SUBLANES)
            return pltpu.make_async_copy(
                ys_ref.at[pl.ds(src, SUBLANES)],
                ybuf.at[sl, pl.ds(pl.multiple_of(r * SUBLANES, SUBLANES), SUBLANES)],
                sem.at[sl])
        return make

    @pl.when(step == 0)
    def _():
        _start_tile_dmas(tile_copy(dest_ref, 0), ts)

    if not inline:
        @pl.when(step + 1 < n_steps)
        def _():
            _start_tile_dmas(tile_copy(dnext_ref, 1 - slot), ts)

    _wait_all_tiles(ys_ref, ybuf.at[slot], sem.at[slot], to_hbm=False)
    yv = ybuf.at[slot]
    y = jnp.concatenate([_token_rows(yv, s, ts) for s in range(SUBLANES)], axis=1)

    def request_part(part, n_parts):
        if inline:
            make = tile_copy(dnext_ref, 1 - slot)
            for r in range(part * ts // n_parts, (part + 1) * ts // n_parts):
                make(r).start()

    def retire_last():
        if inline:
            @pl.when(step + 1 == n_steps)
            def _():
                _wait_all_tiles(ys_ref, ybuf.at[1 - slot], sem.at[1 - slot], to_hbm=False)

    return y, request_part, retire_last


def _route(logits_t):
    ts = logits_t.shape[1]
    row = lax.broadcasted_iota(jnp.int32, (SUBLANES, ts), 0)
    gl = jnp.where(row < N_GROUPS, logits_t[0:SUBLANES], _NEG_INF)
    gmax = jnp.max(gl, axis=0, keepdims=True)
    gsel = jnp.min(jnp.where(gl == gmax, row, SUBLANES), axis=0, keepdims=True)
    p_g = 1.0 / jnp.sum(jnp.exp(gl - gmax), axis=0, keepdims=True)
    es = jnp.zeros((SUBLANES, ts), jnp.float32)
    for g in range(N_GROUPS):
        lo = SUBLANES * (1 + g)
        es = jnp.where(gsel == g, logits_t[lo:lo + EXPERTS_PER_GROUP], es)
    m1 = jnp.max(es, axis=0, keepdims=True)
    i1 = jnp.min(jnp.where(es == m1, row, SUBLANES), axis=0, keepdims=True)
    es2 = jnp.where(row == i1, _NEG_INF, es)
    m2 = jnp.max(es2, axis=0, keepdims=True)
    i2 = jnp.min(jnp.where(es2 == m2, row, SUBLANES), axis=0, keepdims=True)
    t = jnp.exp(m2 - m1)
    w1 = p_g / (1.0 + t)
    w2 = w1 * t
    first_lo = i1 < i2
    e_lo = jnp.minimum(i1, i2)
    e_hi = jnp.maximum(i1, i2)
    gate_lo = jnp.where(first_lo, w1, w2)
    gate_hi = jnp.where(first_lo, w2, w1)
    pair = lax.shift_right_logical(e_lo * (2 * EXPERTS_PER_GROUP - 1 - e_lo), 1) + (e_hi - e_lo - 1)
    cls = gsel * PAIRS_PER_GROUP + pair
    return cls, gate_lo, gate_hi


def _post_mixer(x1, mod_ref, n2_ref, wrt_ref, brt_ref, hrow_ref, cls_ref, rank_ref, cnt_ref, sub):
    ts = x1.shape[0]
    hrow_ref = hrow_ref.at[pl.ds(sub * ts * SUBLANES, ts * SUBLANES)]
    cols = slice(sub * ts, (sub + 1) * ts)
    h2 = _rms_mod(x1, n2_ref[...], mod_ref[3], mod_ref[4])
    h2b = _bf16(h2)
    logits_t = lax.dot_general(wrt_ref[...], h2b, (((1,), (1,)), ((), ())),
                               preferred_element_type=jnp.float32) + brt_ref[:, 0:1]
    cls, gate_lo, gate_hi = _route(logits_t)
    gates_t = jnp.concatenate(
        [gate_lo, gate_hi, jnp.zeros((LANES - 2, ts), jnp.float32)], axis=0)
    h2r = h2b.astype(jnp.float32)
    for j in range(PACK_SUBLANES):
        hi = _bits(h2r[:, (2 * j) * LANES:(2 * j + 1) * LANES])
        lo = _bits(h2r[:, (2 * j + 1) * LANES:(2 * j + 2) * LANES])
        hrow_ref[pl.ds(j, ts, stride=SUBLANES), :] = hi | lax.shift_right_logical(lo, jnp.uint32(16))
    hrow_ref[pl.ds(PACK_SUBLANES, ts, stride=SUBLANES), :] = _bits(gates_t.T)
    for j in range(PACK_SUBLANES + 1, SUBLANES):
        hrow_ref[pl.ds(j, ts, stride=SUBLANES), :] = jnp.zeros((ts, LANES), jnp.uint32)
    cls_ref[0, :, cols] = cls
    crow = lax.broadcasted_iota(jnp.int32, (LANES, ts), 0)
    onehot = jnp.where(crow == cls, 1.0, 0.0)
    cnt = jnp.sum(onehot, axis=1, keepdims=True)
    earlier = (lax.broadcasted_iota(jnp.int32, (ts, ts), 0)
               < lax.broadcasted_iota(jnp.int32, (ts, ts), 1))
    prefix = _dot(_bf16(onehot), _bf16(jnp.where(earlier, 1.0, 0.0)))
    rank = jnp.sum(onehot * (prefix + cnt_ref[:, 0:1]), axis=0, keepdims=True)
    rank_ref[0, :, cols] = rank.astype(jnp.int32)
    cnt_ref[...] += jnp.broadcast_to(cnt, cnt_ref.shape)


def _mixer_out(t, d, ts):
    n_tiles = t // ts
    out_shape = (
        jax.ShapeDtypeStruct((t, d), jnp.float32),
        jax.ShapeDtypeStruct((t * SUBLANES, LANES), jnp.uint32),
        jax.ShapeDtypeStruct((n_tiles, 1, ts), jnp.int32),
        jax.ShapeDtypeStruct((n_tiles, 1, ts), jnp.int32),
        jax.ShapeDtypeStruct((LANES, LANES), jnp.float32),
    )
    out_specs = (
        pl.BlockSpec((ts, d), lambda i: (i, 0)),
        pl.BlockSpec((ts * SUBLANES, LANES), lambda i: (i, 0)),
        pl.BlockSpec((1, 1, ts), lambda i: (i, 0, 0)),
        pl.BlockSpec((1, 1, ts), lambda i: (i, 0, 0)),
        pl.BlockSpec((LANES, LANES), lambda i: (0, 0)),
    )
    return out_shape, out_specs


def _full(shape):
    return pl.BlockSpec(shape, lambda i: tuple(0 for _ in shape))


def _expert_weight_cast(weights, layer, n_steps):
    flat = [w.reshape(w.shape[0], w.shape[1] * w.shape[2], w.shape[3]) for w in weights]
    if any(w.shape[1] % (n_steps * BF16_SUBLANES) for w in flat):
        return [], [], (), ()
    in_specs = [pl.BlockSpec((None, w.shape[1] // n_steps, w.shape[2]), lambda i: (layer, i, 0))
                for w in flat]
    out_shapes = tuple(jax.ShapeDtypeStruct(w.shape[1:], jnp.bfloat16) for w in flat)
    out_specs = tuple(pl.BlockSpec((w.shape[1] // n_steps, w.shape[2]), lambda i: (i, 0))
                      for w in flat)
    return flat, in_specs, out_shapes, out_specs


def _cast_slices(cast_in, cast_out):
    for src, dst in zip(cast_in, cast_out):
        dst[...] = _bf16(src[...])


def _cast_results(outs, weights, layer):
    if outs:
        return [o.reshape(w.shape[1:]) for o, w in zip(outs, weights)]
    return [_bf16(w[layer]) for w in weights]


def _to_tile_order(x_ref, perm, xs):
    bsz, tm, d = x_ref.shape
    for cb in range(d // LANES):
        for p in range(bsz):
            perm[cb, PERM_PITCH * p:PERM_PITCH * p + tm, :] = x_ref[p, :, cb * LANES:(cb + 1) * LANES]
    for m in range(tm):
        for cb in range(d // LANES):
            xs[m * bsz:(m + 1) * bsz, cb * LANES:(cb + 1) * LANES] = (
                perm.at[cb][pl.ds(m, bsz, stride=PERM_PITCH), :])


def _conf_kernel(n_cast, x_ref, mod_ref, n1_ref, n2_ref, w1_ref, b1_ref, dw_ref, dwb_ref,
                 lng_ref, lnb_ref, w2_ref, b2_ref, wrt_ref, brt_ref, *rest):
    cast_in, rest = rest[:n_cast], rest[n_cast:]
    x1_ref, hrow_ref, cls_ref, rank_ref, cnt_ref, zeros_ref = rest[:6]
    cast_out, (perm, xs, vext, conv) = rest[6:6 + n_cast], rest[6 + n_cast:]
    _cast_slices(cast_in, cast_out)
    i = pl.program_id(0)
    bsz, tm, d = x_ref.shape
    ts = bsz * tm
    n_chunks = d // LANES
    hist = SUBLANES * (CONF_KERNEL - 1)

    @pl.when(i == 0)
    def _():
        vext[:, 0:hist, :] = jnp.zeros((n_chunks, hist, LANES), jnp.float32)
        cnt_ref[...] = jnp.zeros_like(cnt_ref)

    zeros_ref[...] = jnp.zeros_like(zeros_ref)

    _to_tile_order(x_ref, perm, xs)
    for sub in range(ts // SUB_TILE):
        rows = slice(sub * SUB_TILE, (sub + 1) * SUB_TILE)
        h = _bf16(_rms_mod(xs[rows, :], n1_ref[...], mod_ref[0], mod_ref[1]))
        u = _dot(h, w1_ref[...]) + b1_ref[...]
        v = u[:, 0:d] * jax.nn.sigmoid(u[:, d:2 * d])
        for cb in range(n_chunks):
            vext[cb, hist + sub * SUB_TILE:hist + (sub + 1) * SUB_TILE, :] = v[:, cb * LANES:(cb + 1) * LANES]

    units_per_chunk = ts // CONV_ROWS

    def conv_unit(idx, carry):
        r0 = pl.multiple_of((idx % units_per_chunk) * CONV_ROWS, CONV_ROWS)
        _causal_taps_unit(dw_ref, dwb_ref, vext, conv, idx // units_per_chunk, r0, CONF_KERNEL)
        return carry

    lax.fori_loop(0, n_chunks * units_per_chunk, conv_unit, 0, unroll=32)
    vext[:, 0:hist, :] = vext[:, ts:ts + hist, :]

    for sub in range(ts // SUB_TILE):
        rows = slice(sub * SUB_TILE, (sub + 1) * SUB_TILE)
        y = jnp.concatenate([conv[cb, rows, :] for cb in range(n_chunks)], axis=1)
        mu = jnp.mean(y, axis=-1, keepdims=True)
        yc = y - mu
        var = jnp.mean(yc * yc, axis=-1, keepdims=True)
        z = yc * lax.rsqrt(var + EPS) * lng_ref[...] + lnb_ref[...]
        z = z * jax.nn.sigmoid(z)
        mix = _dot(_bf16(z), w2_ref[...]) + b2_ref[...]
        x1 = xs[rows, :] + _per_batch(mod_ref[2], SUB_TILE) * mix
        x1_ref[rows, :] = x1
        _post_mixer(x1, mod_ref, n2_ref, wrt_ref, brt_ref, hrow_ref, cls_ref, rank_ref, cnt_ref, sub)


def _conf_call(x, mod, n1, n2, w1, b1, dw, dwb, lng, lnb, w2, b2, wrt, brt, zero_rows,
               expert_weights, layer):
    bsz, seq, d = x.shape
    tm = TIME_TILE
    ts = bsz * tm
    n_steps = seq // tm
    assert zero_rows % (n_steps * SUBLANES) == 0
    cast, cast_specs, cast_shapes, cast_out_specs = _expert_weight_cast(expert_weights, layer, n_steps)
    out_shape, out_specs = _mixer_out(bsz * seq, d, ts)
    out_shape += (jax.ShapeDtypeStruct((zero_rows, LANES), jnp.uint32),) + cast_shapes
    out_specs += (pl.BlockSpec((zero_rows // n_steps, LANES), lambda i: (i, 0)),) + cast_out_specs
    hist = SUBLANES * (CONF_KERNEL - 1)
    outs = pl.pallas_call(
        functools.partial(_conf_kernel, len(cast)),
        out_shape=out_shape,
        grid=(n_steps,),
        in_specs=[
            pl.BlockSpec((bsz, tm, d), lambda i: (0, i, 0)),
            _full((MOD_ROWS, bsz, d)),
            _full((1, d)), _full((1, d)),
            _full((d, 2 * d)), _full((1, 2 * d)),
            _full((d // LANES, CONF_KERNEL + 1, LANES)), _full((d // LANES, 1, LANES)),
            _full((1, d)), _full((1, d)),
            _full((d, d)), _full((1, d)),
            _full((ROUTER_ROWS, d)), _full((ROUTER_ROWS, LANES)),
        ] + cast_specs,
        out_specs=out_specs,
        scratch_shapes=[
            pltpu.VMEM((d // LANES, bsz * PERM_PITCH, LANES), jnp.float32),
            pltpu.VMEM((ts, d), jnp.float32),
            pltpu.VMEM((d // LANES, hist + ts, LANES), jnp.float32),
            pltpu.VMEM((d // LANES, ts, LANES), jnp.float32),
        ],
        compiler_params=pltpu.CompilerParams(
            dimension_semantics=("arbitrary",), vmem_limit_bytes=VMEM_LIMIT_BYTES),
        name="conf_mixer",
    )(x, mod, n1, n2, w1, b1, dw, dwb, lng, lnb, w2, b2, wrt, brt, *cast)
    return outs[:6], _cast_results(outs[6:], expert_weights, layer)


def _lru_kernel(n_tiles, n_cast, dest_ref, dnext_ref, xprev_ref, pmod_ref, mod_ref, n1_ref, n2_ref,
                wy_ref, by_ref, wx_ref, bx_ref,
                cw_ref, cb_ref, wa_ref, ba_ref, wi_ref, bi_ref, lam_ref, wo_ref, bo_ref,
                wrt_ref, brt_ref, ys_ref, *rest):
    cast_in, rest = rest[:n_cast], rest[n_cast:]
    x1_ref, hrow_ref, cls_ref, rank_ref, cnt_ref = rest[:5]
    cast_out, (uext, hcar, ybuf, ysem) = rest[5:5 + n_cast], rest[5 + n_cast:]
    _cast_slices(cast_in, cast_out)
    i = pl.program_id(0)
    ts, d = xprev_ref.shape
    hd = d // LRU_HEADS
    hist = SUBLANES * (LRU_CONV - 1)

    @pl.when(i == 0)
    def _():
        uext[0:hist, :] = jnp.zeros((hist, d), jnp.float32)
        hcar[...] = jnp.zeros_like(hcar)
        cnt_ref[...] = jnp.zeros_like(cnt_ref)

    y_prev, request_part, retire_last = _gather_expert_tiles(ys_ref, ybuf, ysem, dest_ref, dnext_ref, i, n_tiles, ts,
                                               inline=True)
    nl = -lam_ref[...]
    decay = LRU_C * (jnp.maximum(nl, 0.0) + jnp.log1p(jnp.exp(-jnp.abs(nl))))
    hstate = hcar[...]
    for sub in range(ts // SUB_TILE):
        rows = slice(sub * SUB_TILE, (sub + 1) * SUB_TILE)
        x = xprev_ref[rows, :] + _per_batch(pmod_ref[5], SUB_TILE) * y_prev[rows, :]
        h = _bf16(_rms_mod(x, n1_ref[...], mod_ref[0], mod_ref[1]))
        ygate = _gelu_tanh(_dot(h, wy_ref[...]) + by_ref[...])
        xb = _dot(h, wx_ref[...]) + bx_ref[...]
        uext[hist + sub * SUB_TILE:hist + (sub + 1) * SUB_TILE, :] = xb
        u = _causal_taps(cw_ref, cb_ref, uext, LRU_CONV, sub * SUB_TILE, SUB_TILE)
        request_part(sub, ts // SUB_TILE)

        ub = _bf16(u)
        r_parts, i_parts = [], []
        for hh in range(LRU_HEADS):
            uh = ub[:, hh * hd:(hh + 1) * hd]
            r_parts.append(_dot(uh, wa_ref[hh]))
            i_parts.append(_dot(uh, wi_ref[hh]))
        r = jax.nn.sigmoid(jnp.concatenate(r_parts, axis=1) + ba_ref[...])
        ig = jax.nn.sigmoid(jnp.concatenate(i_parts, axis=1) + bi_ref[...])

        neg_log_a = r * decay
        a = jnp.exp(-neg_log_a)
        mult = jnp.sqrt(jnp.tanh(neg_log_a) * (1.0 + a * a))
        b = mult * (ig * u)

        outs = []
        for m in range(SUB_TILE // SUBLANES):
            sl = slice(m * SUBLANES, (m + 1) * SUBLANES)
            hstate = a[sl] * hstate + b[sl]
            outs.append(hstate)
        hs = jnp.concatenate(outs, axis=0)

        mix = _dot(_bf16(hs * ygate), wo_ref[...]) + bo_ref[...]
        x1 = x + _per_batch(mod_ref[2], SUB_TILE) * mix
        x1_ref[rows, :] = x1
        _post_mixer(x1, mod_ref, n2_ref, wrt_ref, brt_ref, hrow_ref, cls_ref, rank_ref, cnt_ref, sub)
    hcar[...] = hstate
    uext[0:hist, :] = uext[ts:ts + hist, :]
    retire_last()


def _lru_call(dest, x_prev, mod_prev, ys, mod, n1, n2, wy, by, wx, bx, cw, cb, wa, ba, wi, bi, lam,
              wo, bo, wrt, brt, expert_weights, layer):
    t, d = x_prev.shape
    n_tiles, _, ts = dest.shape
    bsz = mod.shape[1]
    hd = d // LRU_HEADS
    hist = SUBLANES * (LRU_CONV - 1)
    cast, cast_specs, cast_shapes, cast_out_specs = _expert_weight_cast(expert_weights, layer, n_tiles)
    out_shape, out_specs = _mixer_out(t, d, ts)
    out_shape += cast_shapes
    out_specs += cast_out_specs
    outs = pl.pallas_call(
        functools.partial(_lru_kernel, n_tiles, len(cast)),
        out_shape=out_shape,
        grid=(n_tiles,),
        in_specs=[
            pl.BlockSpec((1, 1, ts), lambda i: (i, 0, 0), memory_space=pltpu.SMEM),
            pl.BlockSpec((1, 1, ts), lambda i: (jnp.minimum(i + 1, n_tiles - 1), 0, 0),
                         memory_space=pltpu.SMEM),
            pl.BlockSpec((ts, d), lambda i: (i, 0)),
            _full((MOD_ROWS, bsz, d)), _full((MOD_ROWS, bsz, d)),
            _full((1, d)), _full((1, d)),
            _full((d, d)), _full((1, d)),
            _full((d, d)), _full((1, d)),
            _full((LRU_CONV, d)), _full((1, d)),
            _full((LRU_HEADS, hd, hd)), _full((1, d)),
            _full((LRU_HEADS, hd, hd)), _full((1, d)),
            _full((1, d)),
            _full((d, d)), _full((1, d)),
            _full((ROUTER_ROWS, d)), _full((ROUTER_ROWS, LANES)),
            pl.BlockSpec(memory_space=pl.ANY),
        ] + cast_specs,
        out_specs=out_specs,
        scratch_shapes=[
            pltpu.VMEM((hist + ts, d), jnp.float32),
            pltpu.VMEM((SUBLANES, d), jnp.float32),
            pltpu.VMEM((2, ts * SUBLANES, LANES), jnp.float32),
            pltpu.SemaphoreType.DMA((2,)),
        ],
        compiler_params=pltpu.CompilerParams(
            dimension_semantics=("arbitrary",), vmem_limit_bytes=VMEM_LIMIT_BYTES),
        name="lru_mixer",
    )(dest, dest, x_prev, mod_prev, mod, n1, n2, wy, by, wx, bx, cw, cb, wa, ba, wi, bi, lam, wo, bo,
      wrt, brt, ys, *cast)
    return outs[:5], _cast_results(outs[5:], expert_weights, layer)


def _plan_kernel(n_blocks, cnt_ref, clo_ref, chi_ref, cls_ref, rank_ref,
                 dest_ref, blo_ref, bhi_ref, bvalid_ref, nblk_ref, class_start):
    n_tiles, _, ts = cls_ref.shape
    class_start[...] = jnp.zeros_like(class_start)

    def per_class(c, nb_total):
        class_start[pl.ds(c, 1), :] = jnp.full((1, LANES), nb_total * ROW_BLOCK, jnp.int32)
        nb = (cnt_ref[c] + (ROW_BLOCK - 1)) // ROW_BLOCK

        def fill(k, carry):
            blo_ref[nb_total + k] = clo_ref[c]
            bhi_ref[nb_total + k] = chi_ref[c]
            bvalid_ref[nb_total + k] = jnp.minimum(cnt_ref[c] - k * ROW_BLOCK, ROW_BLOCK)
            return carry

        lax.fori_loop(0, nb, fill, 0)
        return nb_total + nb

    nb_used = lax.fori_loop(0, N_CLASSES, per_class, 0)
    nblk_ref[0] = nb_used
    last = jnp.maximum(nb_used - 1, 0)

    def tail(k, carry):
        blo_ref[k] = blo_ref[last]
        bhi_ref[k] = bhi_ref[last]
        bvalid_ref[k] = 0
        return carry

    lax.fori_loop(nb_used, n_blocks, tail, 0)

    starts = class_start[:, 0:1]
    class_id = lax.broadcasted_iota(jnp.int32, (LANES, ts), 0)

    def per_tile(t, carry):
        first = jnp.sum(jnp.where(class_id == cls_ref[t], starts, 0), axis=0, keepdims=True)
        dest_ref[t] = first + rank_ref[t]
        return carry

    lax.fori_loop(0, n_tiles, per_tile, 0)


def _plan_call(cls, rank, counts, n_blocks):
    n_tiles, _, ts = cls.shape
    smem_full = pl.BlockSpec(memory_space=pltpu.SMEM)
    vmem_full = pl.BlockSpec(memory_space=pltpu.VMEM)
    return pl.pallas_call(
        functools.partial(_plan_kernel, n_blocks),
        out_shape=(
            jax.ShapeDtypeStruct((n_tiles, 1, ts), jnp.int32),
            jax.ShapeDtypeStruct((n_blocks,), jnp.int32),
            jax.ShapeDtypeStruct((n_blocks,), jnp.int32),
            jax.ShapeDtypeStruct((n_blocks,), jnp.int32),
            jax.ShapeDtypeStruct((1,), jnp.int32),
        ),
        in_specs=[smem_full, smem_full, smem_full, vmem_full, vmem_full],
        out_specs=(vmem_full, smem_full, smem_full, smem_full, smem_full),
        scratch_shapes=[pltpu.VMEM((LANES, LANES), jnp.int32)],
        compiler_params=pltpu.CompilerParams(vmem_limit_bytes=VMEM_LIMIT_BYTES),
        name="moe_plan",
    )(counts, jnp.asarray(_CLASS_LO), jnp.asarray(_CLASS_HI), cls, rank)


def _dispatch_kernel(n_tiles, dest_ref, hrow_ref, xs_in_ref, xs_ref, stage, sem_in, sem_out):
    del xs_in_ref
    i = pl.program_id(0)
    ts = dest_ref.shape[2]
    rows = ts * SUBLANES

    def load(tile, slot):
        return pltpu.make_async_copy(hrow_ref.at[pl.ds(pl.multiple_of(tile * rows, rows), rows)],
                                     stage.at[slot], sem_in.at[slot])

    def tile_copy(slot, r, dst_row):
        return pltpu.make_async_copy(
            stage.at[slot, pl.ds(pl.multiple_of(r * SUBLANES, SUBLANES), SUBLANES)],
            xs_ref.at[pl.ds(pl.multiple_of(dst_row * SUBLANES, SUBLANES), SUBLANES)],
            sem_out.at[slot])

    def drain(slot):
        _wait_all_tiles(xs_ref, stage.at[slot], sem_out.at[slot], to_hbm=True)

    slot = i % DISPATCH_SLOTS
    nxt = (i + 1) % DISPATCH_SLOTS

    @pl.when(i == 0)
    def _():
        load(0, 0).start()

    @pl.when(i >= 2)
    def _():
        drain(nxt)

    @pl.when(i + 1 < n_tiles)
    def _():
        load(i + 1, nxt).start()

    load(i, slot).wait()

    _start_tile_dmas(lambda r: tile_copy(slot, r, dest_ref[0, 0, r]), ts, inline=True, both_queues=True)

    @pl.when(i == n_tiles - 1)
    def _():
        @pl.when(i >= 1)
        def _():
            drain((i + DISPATCH_SLOTS - 1) % DISPATCH_SLOTS)
        drain(slot)


def _dispatch_call(dest, hrows, xs_init):
    n_tiles, _, ts = dest.shape
    assert xs_init.dtype == jnp.uint32 and xs_init.shape[1] == LANES
    any_spec = pl.BlockSpec(memory_space=pl.ANY)
    return pl.pallas_call(
        functools.partial(_dispatch_kernel, n_tiles),
        out_shape=jax.ShapeDtypeStruct(xs_init.shape, jnp.uint32),
        grid=(n_tiles,),
        in_specs=[
            pl.BlockSpec((1, 1, ts), lambda i: (i, 0, 0), memory_space=pltpu.SMEM),
            any_spec, any_spec,
        ],
        out_specs=any_spec,
        scratch_shapes=[
            pltpu.VMEM((DISPATCH_SLOTS, ts * SUBLANES, LANES), jnp.uint32),
            pltpu.SemaphoreType.DMA((DISPATCH_SLOTS,)),
            pltpu.SemaphoreType.DMA((DISPATCH_SLOTS,)),
        ],
        input_output_aliases={2: 0},
        compiler_params=pltpu.CompilerParams(
            dimension_semantics=("arbitrary",), vmem_limit_bytes=VMEM_LIMIT_BYTES),
        name="moe_dispatch",
    )(dest, hrows, xs_init)


def _expert_kernel(blo_ref, bhi_ref, bvalid_ref, nblk_ref, xs_ref,
                   wg_lo, wu_lo, wd_lo, wg_hi, wu_hi, wd_hi, ys_ref):
    del blo_ref, bhi_ref, nblk_ref
    valid = bvalid_ref[pl.program_id(0)]

    def experts_on(m):
        pieces = []
        for j in range(PACK_SUBLANES):
            w = _token_rows(xs_ref, j, m)
            pieces.append(_bf16(_f32_from_bits(w & jnp.uint32(0xFFFF0000))))
            pieces.append(_bf16(_f32_from_bits(lax.shift_left(w, jnp.uint32(16)))))
        xb = jnp.concatenate(pieces, axis=1)
        gates = _f32_from_bits(_token_rows(xs_ref, PACK_SUBLANES, m))
        acc = None
        for idx, (wg, wu, wd) in enumerate(((wg_lo, wu_lo, wd_lo), (wg_hi, wu_hi, wd_hi))):
            a = _dot(xb, wg[...])
            hid = (a * jax.nn.sigmoid(a)) * _dot(xb, wu[...]) * gates[:, idx:idx + 1]
            part = _dot(_bf16(hid), wd[...])
            acc = part if acc is None else acc + part
        for s in range(SUBLANES):
            ys_ref[pl.ds(s, m, stride=SUBLANES), :] = acc[:, s * LANES:(s + 1) * LANES]

    for m in range(ROW_STEP, ROW_BLOCK + 1, ROW_STEP):
        pl.when((valid > m - ROW_STEP) & (valid <= m))(functools.partial(experts_on, m))


def _expert_call(blo, bhi, bvalid, nblk, xs, wg, wu, wd):
    n_rows = xs.shape[0] // SUBLANES
    d, de = wg.shape[1], wg.shape[2]
    assert d == SUBLANES * LANES
    n_blocks = n_rows // ROW_BLOCK

    def w_spec(shape, which):
        if which == 0:
            return pl.BlockSpec((None,) + shape, lambda b, lo, hi, bv, n: (lo[b], 0, 0))
        return pl.BlockSpec((None,) + shape, lambda b, lo, hi, bv, n: (hi[b], 0, 0))

    def row_map(b, lo, hi, bv, n):
        return (jnp.minimum(b, jnp.maximum(n[0] - 1, 0)), 0)

    grid_spec = pltpu.PrefetchScalarGridSpec(
        num_scalar_prefetch=4,
        grid=(n_blocks,),
        in_specs=[
            pl.BlockSpec((ROW_BLOCK * SUBLANES, LANES), row_map),
            w_spec((d, de), 0), w_spec((d, de), 0), w_spec((de, d), 0),
            w_spec((d, de), 1), w_spec((d, de), 1), w_spec((de, d), 1),
        ],
        out_specs=pl.BlockSpec((ROW_BLOCK * SUBLANES, LANES), row_map),
    )
    return pl.pallas_call(
        _expert_kernel,
        out_shape=jax.ShapeDtypeStruct((n_rows * SUBLANES, LANES), jnp.float32),
        grid_spec=grid_spec,
        compiler_params=pltpu.CompilerParams(
            dimension_semantics=("arbitrary",), vmem_limit_bytes=VMEM_LIMIT_BYTES),
        name="moe_experts",
    )(blo, bhi, bvalid, nblk, xs, wg, wu, wd, wg, wu, wd)


def _final_kernel(n_tiles, dest_ref, dnext_ref, x1_ref, mod_ref, fmod_ref, fg_ref, ys_ref,
                  out_ref, ybuf, sem, unperm):
    i = pl.program_id(0)
    ts, d = x1_ref.shape
    bsz, tm, _ = out_ref.shape
    y, _, retire_last = _gather_expert_tiles(ys_ref, ybuf, sem, dest_ref, dnext_ref, i, n_tiles, ts,
                                          inline=False)
    x2 = x1_ref[...] + _per_batch(mod_ref[5], ts) * y
    res = _rms_mod(x2, fg_ref[...], fmod_ref[0], fmod_ref[1])
    for cb in range(d // LANES):
        unperm[cb] = res[:, cb * LANES:(cb + 1) * LANES]
    for cb in range(d // LANES):
        for p in range(bsz):
            out_ref[p, :, cb * LANES:(cb + 1) * LANES] = unperm.at[cb][pl.ds(p, tm, stride=SUBLANES), :]
    retire_last()


def _final_call(dest, x1, mod, ys, bsz, fmod, fin_g):
    t, d = x1.shape
    n_tiles, _, ts = dest.shape
    tm = ts // bsz
    return pl.pallas_call(
        functools.partial(_final_kernel, n_tiles),
        out_shape=jax.ShapeDtypeStruct((bsz, t // bsz, d), jnp.float32),
        grid=(n_tiles,),
        in_specs=[
            pl.BlockSpec((1, 1, ts), lambda i: (i, 0, 0), memory_space=pltpu.SMEM),
            pl.BlockSpec((1, 1, ts), lambda i: (jnp.minimum(i + 1, n_tiles - 1), 0, 0),
                         memory_space=pltpu.SMEM),
            pl.BlockSpec((ts, d), lambda i: (i, 0)),
            _full((MOD_ROWS, bsz, d)), _full((MOD_ROWS, bsz, d)), _full((1, d)),
            pl.BlockSpec(memory_space=pl.ANY),
        ],
        out_specs=pl.BlockSpec((bsz, tm, d), lambda i: (0, i, 0)),
        scratch_shapes=[pltpu.VMEM((2, ts * SUBLANES, LANES), jnp.float32),
                        pltpu.SemaphoreType.DMA((2,)),
                        pltpu.VMEM((d // LANES, ts, LANES), jnp.float32)],
        compiler_params=pltpu.CompilerParams(
            dimension_semantics=("arbitrary",), vmem_limit_bytes=VMEM_LIMIT_BYTES),
        name="moe_combine_final",
    )(dest, dest, x1, mod, fmod, fin_g, ys)


def _router_operands(wg, bg, wr, br):
    d = wg.shape[0]
    pad_g = jnp.zeros((SUBLANES - N_GROUPS, d), jnp.float32)
    pad_e = jnp.zeros((ROUTER_ROWS - SUBLANES - N_EXPERTS, d), jnp.float32)
    wrt = jnp.concatenate([wg.T, pad_g, wr.T, pad_e], axis=0)
    bcol = jnp.concatenate([bg, jnp.zeros((SUBLANES - N_GROUPS,), jnp.float32), br,
                            jnp.zeros((ROUTER_ROWS - SUBLANES - N_EXPERTS,), jnp.float32)])
    brt = jnp.broadcast_to(bcol[:, None], (ROUTER_ROWS, LANES))
    return _bf16(wrt), brt


def _mod_table(m, parts):
    bsz = m.shape[0]
    d = m.shape[1] // parts
    m = jnp.transpose(m.reshape(bsz, parts, d), (1, 0, 2))
    return jnp.concatenate([m, jnp.zeros((MOD_ROWS - parts, bsz, d), jnp.float32)], axis=0)


def _row(v):
    return v.reshape(1, -1)


def _moe_experts(hrows, cls, rank, cnt, weights_bf16, n_blocks, xs_init):
    counts = cnt[:, 0].astype(jnp.int32)
    dest, blo, bhi, bvalid, nblk = _plan_call(cls, rank, counts, n_blocks)
    xs = _dispatch_call(dest, hrows, xs_init)
    ys = _expert_call(blo, bhi, bvalid, nblk, xs, *weights_bf16)
    return xs, ys, dest


def kernel(x, c, ada_w, ada_b, norm1_g, norm2_g, cm_w1, cm_b1, cm_dw, cm_dwb, cm_ln_g, cm_ln_b, cm_w2, cm_b2, rg_wy, rg_by, rg_wx, rg_bx, rg_cw, rg_cb, rg_wa, rg_ba, rg_wi, rg_bi, rg_lambda, rg_wo, rg_bo, moe_wg, moe_bg, moe_wr, moe_br, moe_w_gate, moe_w_up, moe_w_down, fin_ada_w, fin_ada_b, fin_g):
    bsz, seq, d = x.shape
    t = bsz * seq
    depth = ada_w.shape[0]
    assert depth == 2 and bsz == SUBLANES and seq % TIME_TILE == 0
    assert d == 2 * PACK_SUBLANES * LANES

    mods = _ada_call(c, ada_w, ada_b[:, None, :])
    fmods = _ada_call(c, fin_ada_w[None], fin_ada_b[None, None, :])
    fmod = _mod_table(fmods[0], 2)
    n_steps = seq // TIME_TILE
    granule = n_steps // math.gcd(n_steps, ROW_BLOCK)
    n_blocks = -(-(-(-t // ROW_BLOCK) + N_CLASSES) // granule) * granule
    mod0, mod1 = _mod_table(mods[0], 6), _mod_table(mods[1], 6)

    wrt, brt = _router_operands(moe_wg[0], moe_bg[0], moe_wr[0], moe_br[0])
    dw = jnp.concatenate([cm_dw[0], jnp.zeros((1, d), jnp.float32)], axis=0)
    dw = jnp.transpose(dw.reshape(CONF_KERNEL + 1, d // LANES, LANES), (1, 0, 2))
    expert_weights = (moe_w_gate, moe_w_up, moe_w_down)
    (x1, hrows, cls, rank, cnt, xs_init), weights0 = _conf_call(
        x, mod0, _row(norm1_g[0]), _row(norm2_g[0]), _bf16(cm_w1[0]), _row(cm_b1[0]),
        dw, cm_dwb[0].reshape(d // LANES, 1, LANES), _row(cm_ln_g[0]), _row(cm_ln_b[0]),
        _bf16(cm_w2[0]), _row(cm_b2[0]), wrt, brt, n_blocks * ROW_BLOCK * SUBLANES,
        expert_weights, 0)
    xs, ys, dest = _moe_experts(hrows, cls, rank, cnt, weights0, n_blocks, xs_init)

    wrt, brt = _router_operands(moe_wg[1], moe_bg[1], moe_wr[1], moe_br[1])
    (x1, hrows, cls, rank, cnt), weights1 = _lru_call(
        dest, x1, mod0, ys, mod1, _row(norm1_g[1]), _row(norm2_g[1]),
        _bf16(rg_wy[0]), _row(rg_by[0]), _bf16(rg_wx[0]), _row(rg_bx[0]),
        rg_cw[0], _row(rg_cb[0]), _bf16(rg_wa[0]), _row(rg_ba[0]), _bf16(rg_wi[0]), _row(rg_bi[0]),
        _row(rg_lambda[0]), _bf16(rg_wo[0]), _row(rg_bo[0]), wrt, brt, expert_weights, 1)
    _, ys, dest = _moe_experts(hrows, cls, rank, cnt, weights1, n_blocks, xs)
    return _final_call(dest, x1, mod1, ys, bsz, fmod, _row(fin_g))
```
